```python
import math
import jax, jax.numpy as jnp
from jax import lax
import numpy as np

D_MODEL = 2048
BATCH = 8
SEQ = 8192
DEPTH = 1

POOL_WINDOWS = (2, 4, 8, 16)
POOL_GROUPS = len(POOL_WINDOWS)
POOL_GROUP_WIDTH = D_MODEL // 8
POOL_WIDTH = POOL_GROUPS * POOL_GROUP_WIDTH

ATTN_GROUPS = ((128, 1), (512, 4), (2048, 16))
HEADS_PER_GROUP = 4
N_ATTN_HEADS = HEADS_PER_GROUP * len(ATTN_GROUPS)
HEAD_DIM = 128
ATTN_WIDTH = N_ATTN_HEADS * HEAD_DIM
ATTN_OUT_WIDTH = HEADS_PER_GROUP * HEAD_DIM

N_BRANCHES = 2
IN_WIDTH = POOL_WIDTH + 3 * ATTN_WIDTH + N_BRANCHES * D_MODEL

D_FF = 5632
CONV_WIDTH = 3

RMS_EPS = 1e-6

kernel_name = "hybrid_pool_dilated_alibi_convffn_block"


def alibi_slopes(n_heads):
    return np.array([2.0 ** (-8.0 * (h + 1) / n_heads) for h in range(n_heads)], dtype=np.float32)


def rms_norm(x, g):
    xf = x.astype(jnp.float32)
    y = xf * lax.rsqrt(jnp.mean(xf * xf, axis=-1, keepdims=True) + RMS_EPS) * g.astype(jnp.float32)
    return y.astype(x.dtype)


def pool_mixer(u, w_lin, scale):
    B, S, _ = u.shape
    uf = u.astype(jnp.float32).reshape(B, S, POOL_GROUPS, POOL_GROUP_WIDTH)
    t = jnp.arange(S)
    outs = []
    for gi, w in enumerate(POOL_WINDOWS):
        ug = uf[:, :, gi]
        cs = jnp.cumsum(ug, axis=1)
        lag = jnp.pad(cs, ((0, 0), (w, 0), (0, 0)))[:, :S]
        cnt = jnp.minimum(t + 1, w).astype(jnp.float32)[None, :, None]
        outs.append((cs - lag) / cnt - ug)
    pooled = jnp.stack(outs, axis=2)
    y = jnp.einsum('bsgc,gce->bsge', pooled, w_lin.astype(jnp.float32))
    y = y.reshape(B, S, POOL_WIDTH) * scale.astype(jnp.float32)
    return y.astype(u.dtype)


def dilated_group_attention(q, k, v, slopes, window, dilation):
    B, S, H, Dh = q.shape
    span = window // dilation
    L = S // dilation
    nb = -(-L // span)
    Lp = nb * span
    N = B * dilation

    def to_sub(a):
        a = a.reshape(B, L, dilation, H, Dh).transpose(0, 2, 1, 3, 4).reshape(N, L, H, Dh)
        a = jnp.pad(a, ((0, 0), (0, Lp - L), (0, 0), (0, 0)))
        return a.reshape(N, nb, span, H, Dh)

    def with_prev(a):
        prev = jnp.pad(a, ((0, 0), (1, 0), (0, 0), (0, 0), (0, 0)))[:, :nb]
        return jnp.concatenate([prev, a], axis=2)

    qb = to_sub(q)
    kk = with_prev(to_sub(k))
    vv = with_prev(to_sub(v))

    s = jnp.einsum('nbqhd,nbkhd->nbhqk', qb, kk, preferred_element_type=jnp.float32) * (Dh ** -0.5)
    qi = jnp.arange(span)[:, None] + span
    ki = jnp.arange(2 * span)[None, :]
    j = qi - ki
    key_abs = (jnp.arange(nb) * span)[:, None] - span + jnp.arange(2 * span)[None, :]
    valid = ((j >= 0) & (j <= span))[None] & (key_abs >= 0)[:, None, :]
    bias = -slopes[:, None, None] * (j * dilation).astype(jnp.float32)[None]
    s = jnp.where(valid[None, :, None], s + bias[None, None], -jnp.inf)
    m = jnp.max(s, axis=-1, keepdims=True)
    p = jnp.exp(s - m)
    l = jnp.sum(p, axis=-1, keepdims=True)
    o = jnp.einsum('nbhqk,nbkhd->nbqhd', p, vv.astype(jnp.float32)) / jnp.swapaxes(l, 2, 3)
    lse = jnp.swapaxes((m + jnp.log(l))[..., 0], 2, 3)

    o = o.reshape(N, Lp, H, Dh)[:, :L].reshape(B, dilation, L, H, Dh).transpose(0, 2, 1, 3, 4).reshape(B, S, H, Dh)
    lse = lse.reshape(N, Lp, H)[:, :L].reshape(B, dilation, L, H).transpose(0, 2, 1, 3).reshape(B, S, H)
    return o, lse


def dilated_attention_mixer(q, k, v):
    B, S = q.shape[:2]
    slopes = jnp.asarray(alibi_slopes(N_ATTN_HEADS))
    outs, lses = [], []
    for gi, (window, dilation) in enumerate(ATTN_GROUPS):
        hs = slice(gi * HEADS_PER_GROUP, (gi + 1) * HEADS_PER_GROUP)
        o, lse = dilated_group_attention(q[:, :, hs], k[:, :, hs], v[:, :, hs], slopes[hs], window, dilation)
        outs.append(o)
        lses.append(lse)
    wts = jax.nn.softmax(jnp.stack(lses, axis=0), axis=0)
    y = jnp.sum(wts[..., None] * jnp.stack(outs, axis=0), axis=0)
    return y.reshape(B, S, ATTN_OUT_WIDTH).astype(q.dtype)


def causal_dwconv(u, w, b):
    S = u.shape[1]
    up = jnp.pad(u, ((0, 0), (CONV_WIDTH - 1, 0), (0, 0)))
    y = b
    for i in range(CONV_WIDTH):
        y = y + w[i] * up[:, i:i + S]
    return y


def _fwd_setup_inputs(seed: int = 0) -> dict:
    key = jax.random.key(seed)
    ks = jax.random.split(key, 16)
    f32 = jnp.float32
    nrm = lambda k, shape, fan: jax.random.normal(k, shape, f32) * (fan ** -0.5)
    return {
        "x": jax.random.normal(ks[0], (BATCH, SEQ, D_MODEL), f32),
        "g_mix": 1.0 + 0.02 * jax.random.normal(ks[1], (DEPTH, D_MODEL), f32),
        "w_in": nrm(ks[2], (DEPTH, D_MODEL, IN_WIDTH), D_MODEL),
        "b_gate": 0.1 * jax.random.normal(ks[3], (DEPTH, N_BRANCHES * D_MODEL), f32),
        "w_pool_lin": nrm(ks[4], (DEPTH, POOL_GROUPS, POOL_GROUP_WIDTH, POOL_GROUP_WIDTH), POOL_GROUP_WIDTH),
        "pool_scale": 1.0 + 0.02 * jax.random.normal(ks[5], (DEPTH, POOL_WIDTH), f32),
        "w_pool_out": nrm(ks[6], (DEPTH, POOL_WIDTH, D_MODEL), POOL_WIDTH),
        "w_attn_out": nrm(ks[7], (DEPTH, ATTN_OUT_WIDTH, D_MODEL), ATTN_OUT_WIDTH),
        "w_out": nrm(ks[8], (DEPTH, D_MODEL, D_MODEL), D_MODEL),
        "g_ffn": 1.0 + 0.02 * jax.random.normal(ks[9], (DEPTH, D_MODEL), f32),
        "w_up": nrm(ks[10], (DEPTH, D_MODEL, 2 * D_FF), D_MODEL),
        "conv_w": nrm(ks[11], (DEPTH, CONV_WIDTH, 2 * D_FF), CONV_WIDTH),
        "conv_b": 0.02 * jax.random.normal(ks[12], (DEPTH, 2 * D_FF), f32),
        "w_down": nrm(ks[13], (DEPTH, D_FF, D_MODEL), D_FF),
        "g_final": 1.0 + 0.02 * jax.random.normal(ks[14], (D_MODEL,), f32),
    }


def _fwd_reference(x, g_mix, w_in, b_gate, w_pool_lin, pool_scale, w_pool_out, w_attn_out, w_out,
              g_ffn, w_up, conv_w, conv_b, w_down, g_final):
    B, S, _ = x.shape
    o_q = POOL_WIDTH
    o_k = o_q + ATTN_WIDTH
    o_v = o_k + ATTN_WIDTH
    o_g = o_v + ATTN_WIDTH
    for l in range(DEPTH):
        h = rms_norm(x, g_mix[l])
        proj = h @ w_in[l]
        u = proj[..., :o_q]
        q = proj[..., o_q:o_k].reshape(B, S, N_ATTN_HEADS, HEAD_DIM)
        k = proj[..., o_k:o_v].reshape(B, S, N_ATTN_HEADS, HEAD_DIM)
        v = proj[..., o_v:o_g].reshape(B, S, N_ATTN_HEADS, HEAD_DIM)
        gates = jax.nn.sigmoid(proj[..., o_g:] + b_gate[l]).reshape(B, S, N_BRANCHES, D_MODEL)

        y_pool = pool_mixer(u, w_pool_lin[l], pool_scale[l]) @ w_pool_out[l]
        y_attn = dilated_attention_mixer(q, k, v) @ w_attn_out[l]
        mixed = gates[:, :, 0] * y_pool + gates[:, :, 1] * y_attn
        x = x + mixed @ w_out[l]

        h = rms_norm(x, g_ffn[l])
        up = causal_dwconv(h @ w_up[l], conv_w[l], conv_b[l])
        a, b = up[..., :D_FF], up[..., D_FF:]
        x = x + (jax.nn.gelu(a, approximate=False) * b) @ w_down[l]
    return rms_norm(x, g_final)


import jax as _jax
import jax.numpy as _jnp

TWIN_FORMAT = 'train_step'
FWD_PARAMS = ['x', 'g_mix', 'w_in', 'b_gate', 'w_pool_lin', 'pool_scale', 'w_pool_out', 'w_attn_out', 'w_out', 'g_ffn', 'w_up', 'conv_w', 'conv_b', 'w_down', 'g_final']
TWIN_WEIGHTS = ['g_mix', 'w_in', 'b_gate', 'w_pool_lin', 'pool_scale', 'w_pool_out', 'w_attn_out', 'w_out', 'g_ffn', 'w_up', 'conv_w', 'conv_b', 'w_down', 'g_final']
TWIN_DIFF_INPUT = 'x'
TWIN_INPUTS = ['x', 'g_mix', 'w_in', 'b_gate', 'w_pool_lin', 'pool_scale', 'w_pool_out', 'w_attn_out', 'w_out', 'g_ffn', 'w_up', 'conv_w', 'conv_b', 'w_down', 'g_final', 'loss_target', 'm_g_mix', 'm_w_in', 'm_b_gate', 'm_w_pool_lin', 'm_pool_scale', 'm_w_pool_out', 'm_w_attn_out', 'm_w_out', 'm_g_ffn', 'm_w_up', 'm_conv_w', 'm_conv_b', 'm_w_down', 'm_g_final', 'v_g_mix', 'v_w_in', 'v_b_gate', 'v_w_pool_lin', 'v_pool_scale', 'v_w_pool_out', 'v_w_attn_out', 'v_w_out', 'v_g_ffn', 'v_w_up', 'v_conv_w', 'v_conv_b', 'v_w_down', 'v_g_final']
TWIN_OUTPUTS = ['loss', 'grad_x', 'grad_g_mix', 'grad_w_in', 'grad_b_gate', 'grad_w_pool_lin', 'grad_pool_scale', 'grad_w_pool_out', 'grad_w_attn_out', 'grad_w_out', 'grad_g_ffn', 'grad_w_up', 'grad_conv_w', 'grad_conv_b', 'grad_w_down', 'grad_g_final', 'delta_g_mix', 'delta_w_in', 'delta_b_gate', 'delta_w_pool_lin', 'delta_pool_scale', 'delta_w_pool_out', 'delta_w_attn_out', 'delta_w_out', 'delta_g_ffn', 'delta_w_up', 'delta_conv_w', 'delta_conv_b', 'delta_w_down', 'delta_g_final', 'new_m_g_mix', 'new_m_w_in', 'new_m_b_gate', 'new_m_w_pool_lin', 'new_m_pool_scale', 'new_m_w_pool_out', 'new_m_w_attn_out', 'new_m_w_out', 'new_m_g_ffn', 'new_m_w_up', 'new_m_conv_w', 'new_m_conv_b', 'new_m_w_down', 'new_m_g_final', 'new_v_g_mix', 'new_v_w_in', 'new_v_b_gate', 'new_v_w_pool_lin', 'new_v_pool_scale', 'new_v_w_pool_out', 'new_v_w_attn_out', 'new_v_w_out', 'new_v_g_ffn', 'new_v_w_up', 'new_v_conv_w', 'new_v_conv_b', 'new_v_w_down', 'new_v_g_final']
TWIN_LEAF_KINDS = {'loss': 'loss', 'grad_x': 'grad_x', 'grad_g_mix': 'grad_w', 'grad_w_in': 'grad_w', 'grad_b_gate': 'grad_w', 'grad_w_pool_lin': 'grad_w', 'grad_pool_scale': 'grad_w', 'grad_w_pool_out': 'grad_w', 'grad_w_attn_out': 'grad_w', 'grad_w_out': 'grad_w', 'grad_g_ffn': 'grad_w', 'grad_w_up': 'grad_w', 'grad_conv_w': 'grad_w', 'grad_conv_b': 'grad_w', 'grad_w_down': 'grad_w', 'grad_g_final': 'grad_w', 'delta_g_mix': 'delta_w', 'delta_w_in': 'delta_w', 'delta_b_gate': 'delta_w', 'delta_w_pool_lin': 'delta_w', 'delta_pool_scale': 'delta_w', 'delta_w_pool_out': 'delta_w', 'delta_w_attn_out': 'delta_w', 'delta_w_out': 'delta_w', 'delta_g_ffn': 'delta_w', 'delta_w_up': 'delta_w', 'delta_conv_w': 'delta_w', 'delta_conv_b': 'delta_w', 'delta_w_down': 'delta_w', 'delta_g_final': 'delta_w', 'new_m_g_mix': 'new_m', 'new_m_w_in': 'new_m', 'new_m_b_gate': 'new_m', 'new_m_w_pool_lin': 'new_m', 'new_m_pool_scale': 'new_m', 'new_m_w_pool_out': 'new_m', 'new_m_w_attn_out': 'new_m', 'new_m_w_out': 'new_m', 'new_m_g_ffn': 'new_m', 'new_m_w_up': 'new_m', 'new_m_conv_w': 'new_m', 'new_m_conv_b': 'new_m', 'new_m_w_down': 'new_m', 'new_m_g_final': 'new_m', 'new_v_g_mix': 'new_v', 'new_v_w_in': 'new_v', 'new_v_b_gate': 'new_v', 'new_v_w_pool_lin': 'new_v', 'new_v_pool_scale': 'new_v', 'new_v_w_pool_out': 'new_v', 'new_v_w_attn_out': 'new_v', 'new_v_w_out': 'new_v', 'new_v_g_ffn': 'new_v', 'new_v_w_up': 'new_v', 'new_v_conv_w': 'new_v', 'new_v_conv_b': 'new_v', 'new_v_w_down': 'new_v', 'new_v_g_final': 'new_v'}


def _forward(args):
    return _fwd_reference(*[args[k] for k in FWD_PARAMS])


def _output_shape():
    def fwd():
        inp = _fwd_setup_inputs(0)
        return _fwd_reference(*[inp[k] for k in FWD_PARAMS])
    out = _jax.eval_shape(fwd)
    return out.shape, out.dtype

N_MICROBATCH = 1
ADAM_LR = 0.001
ADAM_B1 = 0.9
ADAM_B2 = 0.999
ADAM_EPS = 1e-08
ADAM_WD = 0.01
ADAM_STEP = 10
PER_EXAMPLE_BATCH_AXIS = {'x': 0, 'loss_target': 0}
SHARED_INPUTS = []
_WEIGHT_DTYPES = {'g_mix': _jnp.float32, 'w_in': _jnp.float32, 'b_gate': _jnp.float32, 'w_pool_lin': _jnp.float32, 'pool_scale': _jnp.float32, 'w_pool_out': _jnp.float32, 'w_attn_out': _jnp.float32, 'w_out': _jnp.float32, 'g_ffn': _jnp.float32, 'w_up': _jnp.float32, 'conv_w': _jnp.float32, 'conv_b': _jnp.float32, 'w_down': _jnp.float32, 'g_final': _jnp.float32}
MOMENT_SCALE = {'g_mix': 7.220909e-02, 'w_in': 3.352007e-02, 'b_gate': 1.761402e-02, 'w_pool_lin': 8.571459e-02, 'pool_scale': 8.659339e-02, 'w_pool_out': 6.047193e-02, 'w_attn_out': 2.234437e-02, 'w_out': 6.366478e-02, 'g_ffn': 9.628240e-02, 'w_up': 3.971690e-02, 'conv_w': 3.943356e-02, 'conv_b': 3.862026e-02, 'w_down': 6.490204e-02, 'g_final': 3.195700e+01}


def _to_microbatches(a, axis):
    t = _jnp.moveaxis(a, axis, 0)
    t = t.reshape((N_MICROBATCH, t.shape[0] // N_MICROBATCH) + t.shape[1:])
    return _jnp.moveaxis(t, 1, axis + 1)


def setup_inputs(seed: int = 0) -> dict:
    inp = _fwd_setup_inputs(seed)
    key = _jax.random.fold_in(_jax.random.key(seed), 7919)
    shape, _ = _output_shape()
    out = dict(inp)
    out["loss_target"] = _jax.random.normal(_jax.random.fold_in(key, 0), shape, _jnp.float32)
    for i, name in enumerate(TWIN_WEIGHTS):
        w = inp[name].astype(_jnp.float32)
        if MOMENT_SCALE is None:
            s = _jnp.sqrt(_jnp.mean(_jnp.square(w)) + 1e-30)
        else:
            s = MOMENT_SCALE[name]
        km, kv = _jax.random.split(_jax.random.fold_in(key, i + 1))
        out[name] = w
        out["m_" + name] = s * _jax.random.normal(km, w.shape, _jnp.float32)
        out["v_" + name] = (s * s) * _jax.random.uniform(kv, w.shape, _jnp.float32, 0.5, 1.5)
    if N_MICROBATCH > 1:
        for name, axis in PER_EXAMPLE_BATCH_AXIS.items():
            out[name] = _to_microbatches(out[name], axis)
    return {'x': out['x'], 'g_mix': out['g_mix'], 'w_in': out['w_in'], 'b_gate': out['b_gate'], 'w_pool_lin': out['w_pool_lin'], 'pool_scale': out['pool_scale'], 'w_pool_out': out['w_pool_out'], 'w_attn_out': out['w_attn_out'], 'w_out': out['w_out'], 'g_ffn': out['g_ffn'], 'w_up': out['w_up'], 'conv_w': out['conv_w'], 'conv_b': out['conv_b'], 'w_down': out['w_down'], 'g_final': out['g_final'], 'loss_target': out['loss_target'], 'm_g_mix': out['m_g_mix'], 'm_w_in': out['m_w_in'], 'm_b_gate': out['m_b_gate'], 'm_w_pool_lin': out['m_w_pool_lin'], 'm_pool_scale': out['m_pool_scale'], 'm_w_pool_out': out['m_w_pool_out'], 'm_w_attn_out': out['m_w_attn_out'], 'm_w_out': out['m_w_out'], 'm_g_ffn': out['m_g_ffn'], 'm_w_up': out['m_w_up'], 'm_conv_w': out['m_conv_w'], 'm_conv_b': out['m_conv_b'], 'm_w_down': out['m_w_down'], 'm_g_final': out['m_g_final'], 'v_g_mix': out['v_g_mix'], 'v_w_in': out['v_w_in'], 'v_b_gate': out['v_b_gate'], 'v_w_pool_lin': out['v_w_pool_lin'], 'v_pool_scale': out['v_pool_scale'], 'v_w_pool_out': out['v_w_pool_out'], 'v_w_attn_out': out['v_w_attn_out'], 'v_w_out': out['v_w_out'], 'v_g_ffn': out['v_g_ffn'], 'v_w_up': out['v_w_up'], 'v_conv_w': out['v_conv_w'], 'v_conv_b': out['v_conv_b'], 'v_w_down': out['v_w_down'], 'v_g_final': out['v_g_final']}


def _loss(weights, diff, rest, loss_target):
    with _jax.named_scope("forward"):
        args = {**rest, TWIN_DIFF_INPUT: diff, **{k: w.astype(_WEIGHT_DTYPES[k]) for k, w in weights.items()}}
        y = _forward(args)
    with _jax.named_scope("loss_head"):
        err = _jnp.square(y.astype(_jnp.float32) - loss_target)
        return 0.5 * _jnp.sum(_jnp.mean(err, axis=-1)) if err.ndim else 0.5 * err


def _adamw(w, g, m, v):
    m = ADAM_B1 * m + (1.0 - ADAM_B1) * g
    v = ADAM_B2 * v + (1.0 - ADAM_B2) * _jnp.square(g)
    m_hat = m / (1.0 - ADAM_B1 ** ADAM_STEP)
    v_hat = v / (1.0 - ADAM_B2 ** ADAM_STEP)
    delta = -ADAM_LR * (m_hat / (_jnp.sqrt(v_hat) + ADAM_EPS) + ADAM_WD * w)
    return delta, m, v


def reference(x, g_mix, w_in, b_gate, w_pool_lin, pool_scale, w_pool_out, w_attn_out, w_out, g_ffn, w_up, conv_w, conv_b, w_down, g_final, loss_target, m_g_mix, m_w_in, m_b_gate, m_w_pool_lin, m_pool_scale, m_w_pool_out, m_w_attn_out, m_w_out, m_g_ffn, m_w_up, m_conv_w, m_conv_b, m_w_down, m_g_final, v_g_mix, v_w_in, v_b_gate, v_w_pool_lin, v_pool_scale, v_w_pool_out, v_w_attn_out, v_w_out, v_g_ffn, v_w_up, v_conv_w, v_conv_b, v_w_down, v_g_final):
    given = dict(x=x, g_mix=g_mix, w_in=w_in, b_gate=b_gate, w_pool_lin=w_pool_lin, pool_scale=pool_scale, w_pool_out=w_pool_out, w_attn_out=w_attn_out, w_out=w_out, g_ffn=g_ffn, w_up=w_up, conv_w=conv_w, conv_b=conv_b, w_down=w_down, g_final=g_final, loss_target=loss_target, m_g_mix=m_g_mix, m_w_in=m_w_in, m_b_gate=m_b_gate, m_w_pool_lin=m_w_pool_lin, m_pool_scale=m_pool_scale, m_w_pool_out=m_w_pool_out, m_w_attn_out=m_w_attn_out, m_w_out=m_w_out, m_g_ffn=m_g_ffn, m_w_up=m_w_up, m_conv_w=m_conv_w, m_conv_b=m_conv_b, m_w_down=m_w_down, m_g_final=m_g_final, v_g_mix=v_g_mix, v_w_in=v_w_in, v_b_gate=v_b_gate, v_w_pool_lin=v_w_pool_lin, v_pool_scale=v_pool_scale, v_w_pool_out=v_w_pool_out, v_w_attn_out=v_w_attn_out, v_w_out=v_w_out, v_g_ffn=v_g_ffn, v_w_up=v_w_up, v_conv_w=v_conv_w, v_conv_b=v_conv_b, v_w_down=v_w_down, v_g_final=v_g_final)
    weights = {n: given[n] for n in TWIN_WEIGHTS}
    shared = {n: given[n] for n in SHARED_INPUTS}
    per_example = {n: given[n] for n in ['x']}
    grad_fn = _jax.value_and_grad(_loss, argnums=(0, 1))

    def one_microbatch(ex, loss_target):
        ex = dict(ex)
        diff = ex.pop(TWIN_DIFF_INPUT)
        return grad_fn(weights, diff, {**shared, **ex}, loss_target)

    if N_MICROBATCH == 1:
        loss, (grad_w, grad_x) = one_microbatch(per_example, given["loss_target"])
    else:
        def body(carry, xs):
            loss_sum, grad_sum = carry
            l_k, (gw_k, gx_k) = one_microbatch(xs[0], xs[1])
            with _jax.named_scope("update"):
                return (loss_sum + l_k, _jax.tree.map(_jnp.add, grad_sum, gw_k)), gx_k

        init = (_jnp.zeros((), _jnp.float32), _jax.tree.map(_jnp.zeros_like, weights))
        (loss, grad_w), grad_x = _jax.lax.scan(body, init, (per_example, given["loss_target"]))
    with _jax.named_scope("update"):
        delta_w, new_m, new_v = {}, {}, {}
        for n in TWIN_WEIGHTS:
            delta_w[n], new_m[n], new_v[n] = _adamw(weights[n], grad_w[n], given["m_" + n], given["v_" + n])
    return (loss, grad_x, *[grad_w[n] for n in TWIN_WEIGHTS], *[delta_w[n] for n in TWIN_WEIGHTS],
            *[new_m[n] for n in TWIN_WEIGHTS], *[new_v[n] for n in TWIN_WEIGHTS])
```

```python
import math

import jax
import jax.numpy as jnp
from jax import lax
from jax.experimental import pallas as pl
from jax.experimental.pallas import tpu as pltpu

F32 = jnp.float32
BF16 = jnp.bfloat16
MESH = pl.DeviceIdType.MESH
AXES = ("x", "y", "c")
N_DEV = 8

D_MODEL = 2048
POOL_WINDOWS = (2, 4, 8, 16)
POOL_GROUP_WIDTH = 256
POOL_WIDTH = 1024
ATTN_GROUPS = ((128, 1), (512, 4), (2048, 16))
SPAN = 128
HEADS_PER_GROUP = 4
N_ATTN_HEADS = 12
HEAD_DIM = 128
ATTN_WIDTH = 1536
QKV_WIDTH = 3 * ATTN_WIDTH
ATTN_OUT_WIDTH = 512
IN_WIDTH = 9728
D_FF = 5632
RMS_EPS = 1e-6
ADAM_LR, ADAM_B1, ADAM_B2, ADAM_EPS, ADAM_WD, ADAM_STEP = 0.001, 0.9, 0.999, 1e-08, 0.01, 10

VMEM_LIMIT_BYTES = 48 * 1024 * 1024
HALO = 16
PACK_COLS = 1024
PACK_ROWS = 7680
SMALL_ROWS = 24
NEG_BIG = -1e30

_DN = {"nn": (((1,), (0,)), ((), ())), "nt": (((1,), (1,)), ((), ())), "tn": (((0,), (0,)), ((), ()))}


def _params(*sem):
    return pltpu.CompilerParams(dimension_semantics=sem, vmem_limit_bytes=VMEM_LIMIT_BYTES)


def _dot(a, b, mode):
    return lax.dot_general(a.astype(BF16), b.astype(BF16), _DN[mode], preferred_element_type=F32)


def _mm(name, a, b, *, mode, grid, a_spec, b_spec, o_spec, o_shape, o_dtype, acc_shape, res=None, res_spec=None):
    nk = grid[2]

    def body(*refs):
        if res is not None:
            a_ref, b_ref, r_ref, o_ref = refs[:4]
            scr = refs[4:]
        else:
            a_ref, b_ref, o_ref = refs[:3]
            r_ref = None
            scr = refs[3:]
        prod = _dot(a_ref[...], b_ref[...], mode)

        def finish(val):
            if r_ref is not None:
                val = val + r_ref[...]
            o_ref[...] = val.astype(o_ref.dtype)

        if nk == 1:
            finish(prod)
        else:
            acc = scr[0]
            k = pl.program_id(2)

            @pl.when(k == 0)
            def _():
                acc[...] = prod

            @pl.when(k > 0)
            def _():
                acc[...] += prod

            @pl.when(k == nk - 1)
            def _():
                finish(acc[...])

    in_specs = [a_spec, b_spec]
    args = [a, b]
    if res is not None:
        in_specs.append(res_spec)
        args.append(res)
    return pl.pallas_call(
        body, grid=grid, in_specs=in_specs, out_specs=o_spec,
        out_shape=jax.ShapeDtypeStruct(o_shape, o_dtype),
        scratch_shapes=[] if nk == 1 else [pltpu.VMEM(acc_shape, F32)],
        compiler_params=_params("parallel", "parallel", "arbitrary"), name=name)(*args)


def _mm_nn(name, a, b, *, tm, tn, tk, o_dtype, n_cols=None, b_col_off=0, res=None):
    M, K = a.shape
    N = b.shape[1] if n_cols is None else n_cols
    off = b_col_off // tn
    return _mm(name, a, b, mode="nn", grid=(M // tm, N // tn, K // tk),
               a_spec=pl.BlockSpec((tm, tk), lambda i, j, k: (i, k)),
               b_spec=pl.BlockSpec((tk, tn), lambda i, j, k: (k, j + off)),
               o_spec=pl.BlockSpec((tm, tn), lambda i, j, k: (i, j)),
               o_shape=(M, N), o_dtype=o_dtype, acc_shape=(tm, tn), res=res,
               res_spec=pl.BlockSpec((tm, tn), lambda i, j, k: (i, j)))


def _mm_nt(name, a, b, *, tm, tn, tk, o_dtype):
    M, K = a.shape
    N = b.shape[0]
    return _mm(name, a, b, mode="nt", grid=(M // tm, N // tn, K // tk),
               a_spec=pl.BlockSpec((tm, tk), lambda i, j, k: (i, k)),
               b_spec=pl.BlockSpec((tn, tk), lambda i, j, k: (j, k)),
               o_spec=pl.BlockSpec((tm, tn), lambda i, j, k: (i, j)),
               o_shape=(M, N), o_dtype=o_dtype, acc_shape=(tm, tn))


def _mm_tn(name, a, b, *, tm, tn, tk):
    K, M = a.shape
    N = b.shape[1]
    return _mm(name, a, b, mode="tn", grid=(M // tm, N // tn, K // tk),
               a_spec=pl.BlockSpec((tk, tm), lambda i, j, k: (k, i)),
               b_spec=pl.BlockSpec((tk, tn), lambda i, j, k: (k, j)),
               o_spec=pl.BlockSpec((tm, tn), lambda i, j, k: (i, j)),
               o_shape=(M, N), o_dtype=F32, acc_shape=(tm, tn))


def _rms_fwd(name, x, g):
    S, D = x.shape
    T = 512

    def body(x_ref, g_ref, h_ref):
        xv = x_ref[...]
        r = lax.rsqrt(jnp.mean(xv * xv, axis=-1, keepdims=True) + RMS_EPS)
        h_ref[...] = (xv * r * g_ref[...]).astype(BF16)

    return pl.pallas_call(
        body, grid=(S // T,),
        in_specs=[pl.BlockSpec((T, D), lambda i: (i, 0)), pl.BlockSpec((1, D), lambda i: (0, 0))],
        out_specs=pl.BlockSpec((T, D), lambda i: (i, 0)),
        out_shape=jax.ShapeDtypeStruct((S, D), BF16), compiler_params=_params("parallel"), name=name)(x, g)


def _rms_bwd(name, x, g, dh, dres):
    S, D = x.shape
    T = 256

    def body(x_ref, g_ref, dh_ref, dres_ref, dx_ref, dxb_ref, dg_ref):
        i = pl.program_id(0)
        xv = x_ref[...]
        r = lax.rsqrt(jnp.mean(xv * xv, axis=-1, keepdims=True) + RMS_EPS)
        xhat = xv * r
        dhv = dh_ref[...].astype(F32)
        gdh = dhv * g_ref[...]
        c = jnp.mean(xhat * gdh, axis=-1, keepdims=True)
        dx = dres_ref[...] + r * (gdh - xhat * c)
        dx_ref[...] = dx
        dxb_ref[...] = dx.astype(BF16)
        part = jnp.sum(dhv * xhat, axis=0, keepdims=True)

        @pl.when(i == 0)
        def _():
            dg_ref[...] = part

        @pl.when(i > 0)
        def _():
            dg_ref[...] += part

    row = pl.BlockSpec((T, D), lambda i: (i, 0))
    vec = pl.BlockSpec((1, D), lambda i: (0, 0))
    return pl.pallas_call(
        body, grid=(S // T,), in_specs=[row, vec, row, row], out_specs=[row, row, vec],
        out_shape=[jax.ShapeDtypeStruct((S, D), F32), jax.ShapeDtypeStruct((S, D), BF16),
                   jax.ShapeDtypeStruct((1, D), F32)],
        compiler_params=_params("arbitrary"), name=name)(x, g, dh, dres)


def _final_loss(name, x, g, target):
    S, D = x.shape
    T = 256

    def body(x_ref, g_ref, t_ref, dx_ref, dxb_ref, dg_ref, sq_ref):
        i = pl.program_id(0)
        xv = x_ref[...]
        r = lax.rsqrt(jnp.mean(xv * xv, axis=-1, keepdims=True) + RMS_EPS)
        xhat = xv * r
        err = xhat * g_ref[...] - t_ref[...]
        dy = err * (1.0 / D)
        gdy = dy * g_ref[...]
        c = jnp.mean(xhat * gdy, axis=-1, keepdims=True)
        dx = r * (gdy - xhat * c)
        dx_ref[...] = dx
        dxb_ref[...] = dx.astype(BF16)
        dg_part = jnp.sum(dy * xhat, axis=0, keepdims=True)
        sq_part = jnp.sum(err * err, axis=0, keepdims=True)

        @pl.when(i == 0)
        def _():
            dg_ref[...] = dg_part
            sq_ref[...] = sq_part

        @pl.when(i > 0)
        def _():
            dg_ref[...] += dg_part
            sq_ref[...] += sq_part

    row = pl.BlockSpec((T, D), lambda i: (i, 0))
    vec = pl.BlockSpec((1, D), lambda i: (0, 0))
    return pl.pallas_call(
        body, grid=(S // T,), in_specs=[row, vec, row], out_specs=[row, row, vec, vec],
        out_shape=[jax.ShapeDtypeStruct((S, D), F32), jax.ShapeDtypeStruct((S, D), BF16),
                   jax.ShapeDtypeStruct((1, D), F32), jax.ShapeDtypeStruct((1, D), F32)],
        compiler_params=_params("arbitrary"), name=name)(x, g, target)


def _prev_halo_spec(T, C, col_map):
    per = T // HALO
    return pl.BlockSpec((HALO, C), lambda *g: (jnp.maximum(g[0] * per - 1, 0), col_map(*g)))


def _pool_window_sums(ext, T, i):
    t = i * T + lax.broadcasted_iota(jnp.int32, (T, 1), 0)
    outs = []
    cnts = []
    for gi, w in enumerate(POOL_WINDOWS):
        s = ext[:, gi * POOL_GROUP_WIDTH:(gi + 1) * POOL_GROUP_WIDTH]
        sh = 1
        while sh < w:
            s = s + pltpu.roll(s, sh, 0)
            sh *= 2
        cnt = jnp.minimum(t + 1, w).astype(F32)
        outs.append(s[HALO:] / cnt - ext[HALO:, gi * POOL_GROUP_WIDTH:(gi + 1) * POOL_GROUP_WIDTH])
        cnts.append(cnt)
    return outs, cnts


def _load_ext(ext_ref, halo_ref, blk_ref, first):
    ext_ref[0:HALO, :] = jnp.where(first, 0.0, halo_ref[...].astype(F32))
    ext_ref[HALO:, :] = blk_ref[...].astype(F32)


def _pool_fwd(u, w_lin, scale):
    S = u.shape[0]
    T = 512
    W = POOL_WIDTH

    def body(u_ref, halo_ref, wl_ref, sc_ref, y_ref, ext_ref):
        i = pl.program_id(0)
        _load_ext(ext_ref, halo_ref, u_ref, i == 0)
        pooled, _ = _pool_window_sums(ext_ref[...], T, i)
        for gi in range(4):
            z = _dot(pooled[gi], wl_ref[gi], "nn")
            cs = slice(gi * POOL_GROUP_WIDTH, (gi + 1) * POOL_GROUP_WIDTH)
            y_ref[:, cs] = (z * sc_ref[:, cs]).astype(BF16)

    return pl.pallas_call(
        body, grid=(S // T,),
        in_specs=[pl.BlockSpec((T, W), lambda i: (i, 0)), _prev_halo_spec(T, W, lambda i: 0),
                  pl.BlockSpec((4, 256, 256), lambda i: (0, 0, 0)), pl.BlockSpec((1, W), lambda i: (0, 0))],
        out_specs=pl.BlockSpec((T, W), lambda i: (i, 0)),
        out_shape=jax.ShapeDtypeStruct((S, W), BF16),
        scratch_shapes=[pltpu.VMEM((HALO + T, W), F32)],
        compiler_params=_params("parallel"), name="pool_fwd")(u, u, w_lin, scale)


def _pool_bwd_a(u, dy, w_lin, scale):
    S = u.shape[0]
    T = 512
    W = POOL_WIDTH

    def body(u_ref, halo_ref, dy_ref, wl_ref, sc_ref, e_ref, dwl_ref, dsc_ref, ext_ref):
        i = pl.program_id(0)
        _load_ext(ext_ref, halo_ref, u_ref, i == 0)
        pooled, cnts = _pool_window_sums(ext_ref[...], T, i)
        for gi in range(4):
            cs = slice(gi * POOL_GROUP_WIDTH, (gi + 1) * POOL_GROUP_WIDTH)
            z = _dot(pooled[gi], wl_ref[gi], "nn")
            dyg = dy_ref[:, cs]
            dsc = jnp.sum(dyg * z, axis=0, keepdims=True)
            dz = dyg * sc_ref[:, cs]
            dwl = _dot(pooled[gi], dz, "tn")
            dpooled = _dot(dz, wl_ref[gi], "nt")
            e_ref[:, cs] = dpooled / cnts[gi]

            @pl.when(i == 0)
            def _():
                dwl_ref[gi] = dwl
                dsc_ref[:, cs] = dsc

            @pl.when(i > 0)
            def _():
                dwl_ref[gi] += dwl
                dsc_ref[:, cs] += dsc

    return pl.pallas_call(
        body, grid=(S // T,),
        in_specs=[pl.BlockSpec((T, W), lambda i: (i, 0)), _prev_halo_spec(T, W, lambda i: 0),
                  pl.BlockSpec((T, W), lambda i: (i, 0)),
                  pl.BlockSpec((4, 256, 256), lambda i: (0, 0, 0)), pl.BlockSpec((1, W), lambda i: (0, 0))],
        out_specs=[pl.BlockSpec((T, W), lambda i: (i, 0)), pl.BlockSpec((4, 256, 256), lambda i: (0, 0, 0)),
                   pl.BlockSpec((1, W), lambda i: (0, 0))],
        out_shape=[jax.ShapeDtypeStruct((S, W), F32), jax.ShapeDtypeStruct((4, 256, 256), F32),
                   jax.ShapeDtypeStruct((1, W), F32)],
        scratch_shapes=[pltpu.VMEM((HALO + T, W), F32)],
        compiler_params=_params("arbitrary"), name="pool_bwd_a")(u, u, dy, w_lin, scale)


def _pool_bwd_b(e):
    S = e.shape[0]
    T = 512
    W = POOL_WIDTH
    per = T // HALO
    n_blk = S // T

    def body(e_ref, halo_ref, du_ref, ext_ref):
        i = pl.program_id(0)
        ext_ref[0:T, :] = e_ref[...]
        ext_ref[T:, :] = jnp.where(i == n_blk - 1, 0.0, halo_ref[...])
        ext = ext_ref[...]
        n = T + HALO
        t = i * T + lax.broadcasted_iota(jnp.int32, (T, 1), 0)
        for gi, w in enumerate(POOL_WINDOWS):
            cs = slice(gi * POOL_GROUP_WIDTH, (gi + 1) * POOL_GROUP_WIDTH)
            s = ext[:, cs]
            sh = 1
            while sh < w:
                s = s + pltpu.roll(s, n - sh, 0)
                sh *= 2
            cnt = jnp.minimum(t + 1, w).astype(F32)
            du_ref[:, cs] = (s[0:T] - ext[0:T, cs] * cnt).astype(BF16)

    return pl.pallas_call(
        body, grid=(n_blk,),
        in_specs=[pl.BlockSpec((T, W), lambda i: (i, 0)),
                  pl.BlockSpec((HALO, W), lambda i: (jnp.minimum((i + 1) * per, S // HALO - 1), 0))],
        out_specs=pl.BlockSpec((T, W), lambda i: (i, 0)),
        out_shape=jax.ShapeDtypeStruct((S, W), BF16),
        scratch_shapes=[pltpu.VMEM((T + HALO, W), F32)],
        compiler_params=_params("parallel"), name="pool_bwd_b")(e, e)


def _slope(head):
    return 2.0 ** (-8.0 * (head + 1) / N_ATTN_HEADS)


def _attn_masks(b):
    row = lax.broadcasted_iota(jnp.int32, (SPAN, SPAN), 0)
    col = lax.broadcasted_iota(jnp.int32, (SPAN, SPAN), 1)
    dist_cur = (row - col).astype(F32)
    valid_cur = col <= row
    valid_prev = jnp.logical_and(col >= row, b > 0)
    return dist_cur, valid_cur, valid_prev


def _attn_fwd(gi, qkv):
    S = qkv.shape[0]
    d = ATTN_GROUPS[gi][1]
    L = S // d
    nb = L // SPAN
    view = qkv.reshape(L, d * QKV_WIDTH)
    W = ATTN_OUT_WIDTH
    ncb = QKV_WIDTH // W
    scale = HEAD_DIM ** -0.5

    def body(q_ref, kc_ref, kp_ref, vc_ref, vp_ref, o_ref, lse_ref):
        b = pl.program_id(1)
        dist_cur, valid_cur, valid_prev = _attn_masks(b)
        for hi in range(HEADS_PER_GROUP):
            hs = slice(hi * HEAD_DIM, (hi + 1) * HEAD_DIM)
            sl = _slope(gi * HEADS_PER_GROUP + hi) * d
            q = q_ref[:, hs]
            s_cur = _dot(q, kc_ref[:, hs], "nt") * scale + (-sl) * dist_cur
            s_prev = _dot(q, kp_ref[:, hs], "nt") * scale + (-sl) * (dist_cur + SPAN)
            s_cur = jnp.where(valid_cur, s_cur, NEG_BIG)
            s_prev = jnp.where(valid_prev, s_prev, NEG_BIG)
            m = jnp.maximum(jnp.max(s_cur, axis=-1, keepdims=True), jnp.max(s_prev, axis=-1, keepdims=True))
            p_cur = jnp.exp(s_cur - m)
            p_prev = jnp.exp(s_prev - m)
            l = jnp.sum(p_cur, axis=-1, keepdims=True) + jnp.sum(p_prev, axis=-1, keepdims=True)
            o = (_dot(p_cur, vc_ref[:, hs], "nn") + _dot(p_prev, vp_ref[:, hs], "nn")) / l
            o_ref[:, hs] = o
            lse_ref[:, hs] = jnp.broadcast_to(m + jnp.log(l), (SPAN, HEAD_DIM))

    def cur(sec):
        return pl.BlockSpec((SPAN, W), lambda r, b: (b, r * ncb + 3 * sec + gi))

    def prev(sec):
        return pl.BlockSpec((SPAN, W), lambda r, b: (jnp.maximum(b - 1, 0), r * ncb + 3 * sec + gi))

    out = pl.BlockSpec((SPAN, W), lambda r, b: (b, r))
    o, lse = pl.pallas_call(
        body, grid=(d, nb), in_specs=[cur(0), cur(1), prev(1), cur(2), prev(2)], out_specs=[out, out],
        out_shape=[jax.ShapeDtypeStruct((L, d * W), F32), jax.ShapeDtypeStruct((L, d * W), F32)],
        compiler_params=_params("parallel", "parallel"), name=f"attn_fwd_g{gi}")(view, view, view, view, view)
    return o.reshape(S, W), lse.reshape(S, W)


def _attn_bwd(gi, qkv, do, lse, dp):
    S = qkv.shape[0]
    d = ATTN_GROUPS[gi][1]
    L = S // d
    nb = L // SPAN
    W = ATTN_OUT_WIDTH
    ncb = QKV_WIDTH // W
    scale = HEAD_DIM ** -0.5
    view = qkv.reshape(L, d * QKV_WIDTH)
    do_v, lse_v, dp_v = (a.reshape(L, d * W) for a in (do, lse, dp))

    def body(q_ref, kc_ref, vc_ref, kp_ref, vp_ref, qn_ref, do_ref, don_ref, lse_ref, lsen_ref, dp_ref, dpn_ref,
             dq_ref, dk_ref, dv_ref):
        b = pl.program_id(1)
        dist_cur, valid_cur, valid_prev = _attn_masks(b)
        row = lax.broadcasted_iota(jnp.int32, (SPAN, SPAN), 0)
        col = lax.broadcasted_iota(jnp.int32, (SPAN, SPAN), 1)
        valid_next = jnp.logical_and(col >= row, b < nb - 1)
        for hi in range(HEADS_PER_GROUP):
            hs = slice(hi * HEAD_DIM, (hi + 1) * HEAD_DIM)
            sl = _slope(gi * HEADS_PER_GROUP + hi) * d
            q, kc, vc, kp, vp, qn = (r[:, hs] for r in (q_ref, kc_ref, vc_ref, kp_ref, vp_ref, qn_ref))
            dov, don = do_ref[:, hs], don_ref[:, hs]
            lse_b, lse_n, dp_b, dp_n = lse_ref[:, hs], lsen_ref[:, hs], dp_ref[:, hs], dpn_ref[:, hs]
            bias_cur = (-sl) * dist_cur
            bias_far = (-sl) * (dist_cur + SPAN)
            p_cur = jnp.where(valid_cur, jnp.exp(_dot(q, kc, "nt") * scale + bias_cur - lse_b), 0.0)
            p_prev = jnp.where(valid_prev, jnp.exp(_dot(q, kp, "nt") * scale + bias_far - lse_b), 0.0)
            ds_cur = p_cur * (_dot(dov, vc, "nt") - dp_b)
            ds_prev = p_prev * (_dot(dov, vp, "nt") - dp_b)
            dq_ref[:, hs] = ((_dot(ds_cur, kc, "nn") + _dot(ds_prev, kp, "nn")) * scale).astype(BF16)
            p_next = jnp.where(valid_next, jnp.exp(_dot(qn, kc, "nt") * scale + bias_far - lse_n), 0.0)
            ds_next = p_next * (_dot(don, vc, "nt") - dp_n)
            dv_ref[:, hs] = (_dot(p_cur, dov, "tn") + _dot(p_next, don, "tn")).astype(BF16)
            dk_ref[:, hs] = ((_dot(ds_cur, q, "tn") + _dot(ds_next, qn, "tn")) * scale).astype(BF16)

    def qkv_spec(sec, shift):
        def imap(r, b):
            return (jnp.clip(b + shift, 0, nb - 1), r * ncb + 3 * sec + gi)
        return pl.BlockSpec((SPAN, W), imap)

    def act_spec(shift):
        return pl.BlockSpec((SPAN, W), lambda r, b: (jnp.clip(b + shift, 0, nb - 1), r))

    out = act_spec(0)
    dq, dk, dv = pl.pallas_call(
        body, grid=(d, nb),
        in_specs=[qkv_spec(0, 0), qkv_spec(1, 0), qkv_spec(2, 0), qkv_spec(1, -1), qkv_spec(2, -1), qkv_spec(0, 1),
                  act_spec(0), act_spec(1), act_spec(0), act_spec(1), act_spec(0), act_spec(1)],
        out_specs=[out, out, out],
        out_shape=[jax.ShapeDtypeStruct((L, d * W), BF16)] * 3,
        compiler_params=_params("parallel", "parallel"), name=f"attn_bwd_g{gi}")(
            view, view, view, view, view, view, do_v, do_v, lse_v, lse_v, dp_v, dp_v)
    return dq.reshape(S, W), dk.reshape(S, W), dv.reshape(S, W)


def _group_weights(l_refs):
    l0, l1, l2 = (r[...] for r in l_refs)
    m = jnp.maximum(jnp.maximum(l0, l1), l2)
    e = [jnp.exp(l0 - m), jnp.exp(l1 - m), jnp.exp(l2 - m)]
    tot = e[0] + e[1] + e[2]
    return [ei / tot for ei in e]


def _merge_fwd(outs, lses):
    S, W = outs[0].shape
    T = 512

    def body(o0, o1, o2, l0, l1, l2, y_ref):
        w = _group_weights((l0, l1, l2))
        y_ref[...] = (w[0] * o0[...] + w[1] * o1[...] + w[2] * o2[...]).astype(BF16)

    blk = pl.BlockSpec((T, W), lambda i: (i, 0))
    return pl.pallas_call(
        body, grid=(S // T,), in_specs=[blk] * 6, out_specs=blk, out_shape=jax.ShapeDtypeStruct((S, W), BF16),
        compiler_params=_params("parallel"), name="merge_fwd")(*outs, *lses)


def _merge_bwd(dy, outs, lses):
    S, W = outs[0].shape
    T = 512

    def body(dy_ref, o0, o1, o2, l0, l1, l2, do0, do1, do2, dp0, dp1, dp2):
        w = _group_weights((l0, l1, l2))
        dyv = dy_ref[...]
        y = w[0] * o0[...] + w[1] * o1[...] + w[2] * o2[...]
        prod = dyv * y
        dots = []
        for hi in range(HEADS_PER_GROUP):
            hs = slice(hi * HEAD_DIM, (hi + 1) * HEAD_DIM)
            dots.append(jnp.broadcast_to(jnp.sum(prod[:, hs], axis=-1, keepdims=True), (T, HEAD_DIM)))
        dot_b = jnp.concatenate(dots, axis=1)
        for wg, do_ref, dp_ref in zip(w, (do0, do1, do2), (dp0, dp1, dp2)):
            do_ref[...] = (wg * dyv).astype(BF16)
            dp_ref[...] = wg * dot_b

    blk = pl.BlockSpec((T, W), lambda i: (i, 0))
    res = pl.pallas_call(
        body, grid=(S // T,), in_specs=[blk] * 7, out_specs=[blk] * 6,
        out_shape=[jax.ShapeDtypeStruct((S, W), BF16)] * 3 + [jax.ShapeDtypeStruct((S, W), F32)] * 3,
        compiler_params=_params("parallel"), name="merge_bwd")(dy, *outs, *lses)
    return res[:3], res[3:]


def _mix_fwd(pool_y, attn_y, w_pool_out, w_attn_out, gates_pre, b_gate):
    S = pool_y.shape[0]
    D = D_MODEL
    tm, tn = 512, 512
    nj = D // tn

    def body(p_ref, a_ref, wp_ref, wa_ref, g0_ref, g1_ref, b0_ref, b1_ref, mixed_ref, yp_ref, ya_ref):
        yp = _dot(p_ref[...], wp_ref[...], "nn")
        ya = _dot(a_ref[...], wa_ref[...], "nn")
        g0 = jax.nn.sigmoid(g0_ref[...] + b0_ref[...])
        g1 = jax.nn.sigmoid(g1_ref[...] + b1_ref[...])
        mixed_ref[...] = (g0 * yp + g1 * ya).astype(BF16)
        yp_ref[...] = yp.astype(BF16)
        ya_ref[...] = ya.astype(BF16)

    out = pl.BlockSpec((tm, tn), lambda i, j: (i, j))
    return pl.pallas_call(
        body, grid=(S // tm, nj),
        in_specs=[pl.BlockSpec((tm, POOL_WIDTH), lambda i, j: (i, 0)),
                  pl.BlockSpec((tm, ATTN_OUT_WIDTH), lambda i, j: (i, 0)),
                  pl.BlockSpec((POOL_WIDTH, tn), lambda i, j: (0, j)),
                  pl.BlockSpec((ATTN_OUT_WIDTH, tn), lambda i, j: (0, j)),
                  pl.BlockSpec((tm, tn), lambda i, j: (i, j)), pl.BlockSpec((tm, tn), lambda i, j: (i, j + nj)),
                  pl.BlockSpec((1, tn), lambda i, j: (0, j)), pl.BlockSpec((1, tn), lambda i, j: (0, j + nj))],
        out_specs=[out, out, out], out_shape=[jax.ShapeDtypeStruct((S, D), BF16)] * 3,
        compiler_params=_params("parallel", "parallel"), name="mix_fwd")(
            pool_y, attn_y, w_pool_out, w_attn_out, gates_pre, gates_pre, b_gate, b_gate)


def _mix_bwd(dmixed, gates_pre, b_gate, y_pool, y_attn):
    S = dmixed.shape[0]
    D = D_MODEL
    T = 256

    def body(dm_ref, gp_ref, b_ref, yp_ref, ya_ref, dyp_ref, dya_ref, dg_ref, db_ref):
        i = pl.program_id(0)
        dm = dm_ref[...].astype(F32)
        parts = []
        for br, (y_ref, dy_ref) in enumerate(((yp_ref, dyp_ref), (ya_ref, dya_ref))):
            cs = slice(br * D, (br + 1) * D)
            g = jax.nn.sigmoid(gp_ref[:, cs] + b_ref[:, cs])
            dy_ref[...] = (dm * g).astype(BF16)
            dpre = dm * y_ref[...].astype(F32) * g * (1.0 - g)
            dg_ref[:, cs] = dpre.astype(BF16)
            parts.append(jnp.sum(dpre, axis=0, keepdims=True))

        @pl.when(i == 0)
        def _():
            db_ref[:, 0:D] = parts[0]
            db_ref[:, D:2 * D] = parts[1]

        @pl.when(i > 0)
        def _():
            db_ref[:, 0:D] += parts[0]
            db_ref[:, D:2 * D] += parts[1]

    row = pl.BlockSpec((T, D), lambda i: (i, 0))
    row2 = pl.BlockSpec((T, 2 * D), lambda i: (i, 0))
    vec2 = pl.BlockSpec((1, 2 * D), lambda i: (0, 0))
    return pl.pallas_call(
        body, grid=(S // T,), in_specs=[row, row2, vec2, row, row], out_specs=[row, row, row2, vec2],
        out_shape=[jax.ShapeDtypeStruct((S, D), BF16), jax.ShapeDtypeStruct((S, D), BF16),
                   jax.ShapeDtypeStruct((S, 2 * D), BF16), jax.ShapeDtypeStruct((1, 2 * D), F32)],
        compiler_params=_params("arbitrary"), name="mix_bwd")(dmixed, gates_pre, b_gate, y_pool, y_attn)


_CONV_T, _CONV_C = 512, 512


def _conv_taps(ext, w_ref, b_ref):
    x2 = pltpu.roll(ext, 2, 0)[HALO:]
    x1 = pltpu.roll(ext, 1, 0)[HALO:]
    x0 = ext[HALO:]
    y = b_ref[...] + w_ref[0:1, :] * x2
    y = y + w_ref[1:2, :] * x1
    y = y + w_ref[2:3, :] * x0
    return y, (x2, x1, x0)


def _gelu_parts(a):
    cdf = 0.5 * (1.0 + lax.erf(a * (1.0 / math.sqrt(2.0))))
    return cdf, a * cdf


def _conv_specs(S):
    T, C = _CONV_T, _CONV_C
    nj = D_FF // C
    blk = pl.BlockSpec((2, T, C), lambda i, j: (0, i, j))
    per = T // HALO
    halo = pl.BlockSpec((2, HALO, C), lambda i, j: (0, jnp.maximum(i * per - 1, 0), j))
    w_a = pl.BlockSpec((3, C), lambda i, j: (0, j))
    w_b = pl.BlockSpec((3, C), lambda i, j: (0, j + nj))
    b_a = pl.BlockSpec((1, C), lambda i, j: (0, j))
    b_b = pl.BlockSpec((1, C), lambda i, j: (0, j + nj))
    return T, C, nj, blk, halo, w_a, w_b, b_a, b_b


def _conv_load(ext_ref, halo_ref, up_ref, first):
    for h in range(2):
        ext_ref[h, 0:HALO, :] = jnp.where(first, 0.0, halo_ref[h].astype(F32))
        ext_ref[h, HALO:, :] = up_ref[h].astype(F32)


def _convffn_fwd(up_pre, conv_w, conv_b):
    S = up_pre.shape[1]
    T, C, nj, blk, halo, w_a, w_b, b_a, b_b = _conv_specs(S)

    def body(up_ref, halo_ref, wa_ref, wb_ref, ba_ref, bb_ref, act_ref, ext_ref):
        _conv_load(ext_ref, halo_ref, up_ref, pl.program_id(0) == 0)
        a, _ = _conv_taps(ext_ref[0], wa_ref, ba_ref)
        b, _ = _conv_taps(ext_ref[1], wb_ref, bb_ref)
        _, gelu = _gelu_parts(a)
        act_ref[...] = (gelu * b).astype(BF16)

    return pl.pallas_call(
        body, grid=(S // T, nj), in_specs=[blk, halo, w_a, w_b, b_a, b_b],
        out_specs=pl.BlockSpec((T, C), lambda i, j: (i, j)),
        out_shape=jax.ShapeDtypeStruct((S, D_FF), BF16),
        scratch_shapes=[pltpu.VMEM((2, HALO + T, C), F32)],
        compiler_params=_params("parallel", "parallel"), name="convffn_fwd")(
            up_pre, up_pre, conv_w, conv_w, conv_b, conv_b)


def _convffn_bwd_a(up_pre, dact, conv_w, conv_b):
    S = up_pre.shape[1]
    T, C, nj, blk, halo, w_a, w_b, b_a, b_b = _conv_specs(S)
    def swap(spec):
        return pl.BlockSpec(spec.block_shape, lambda j, i, _m=spec.index_map: _m(i, j))

    def body(up_ref, halo_ref, dact_ref, wa_ref, wb_ref, ba_ref, bb_ref, dup_ref, dwa_ref, dwb_ref, dba_ref, dbb_ref,
             ext_ref):
        i = pl.program_id(1)
        _conv_load(ext_ref, halo_ref, up_ref, i == 0)
        a, xa = _conv_taps(ext_ref[0], wa_ref, ba_ref)
        b, xb = _conv_taps(ext_ref[1], wb_ref, bb_ref)
        cdf, gelu = _gelu_parts(a)
        dgelu = cdf + a * (jnp.exp(-0.5 * a * a) * (1.0 / math.sqrt(2.0 * math.pi)))
        dact = dact_ref[...].astype(F32)
        da = dact * b * dgelu
        db = dact * gelu
        dup_ref[0] = da.astype(BF16)
        dup_ref[1] = db.astype(BF16)
        for dval, xs, dw_ref, dbias_ref in ((da, xa, dwa_ref, dba_ref), (db, xb, dwb_ref, dbb_ref)):
            dw = jnp.concatenate([jnp.sum(dval * xk, axis=0, keepdims=True) for xk in xs], axis=0)
            dbias = jnp.sum(dval, axis=0, keepdims=True)

            @pl.when(i == 0)
            def _():
                dw_ref[...] = dw
                dbias_ref[...] = dbias

            @pl.when(i > 0)
            def _():
                dw_ref[...] += dw
                dbias_ref[...] += dbias

    dact_spec = pl.BlockSpec((T, C), lambda j, i: (i, j))
    dw_spec = pl.BlockSpec((3, C), lambda j, i: (0, j))
    db_spec = pl.BlockSpec((1, C), lambda j, i: (0, j))
    res = pl.pallas_call(
        body, grid=(nj, S // T),
        in_specs=[swap(blk), swap(halo), dact_spec, swap(w_a), swap(w_b), swap(b_a), swap(b_b)],
        out_specs=[swap(blk), dw_spec, dw_spec, db_spec, db_spec],
        out_shape=[jax.ShapeDtypeStruct((2, S, D_FF), BF16), jax.ShapeDtypeStruct((3, D_FF), F32),
                   jax.ShapeDtypeStruct((3, D_FF), F32), jax.ShapeDtypeStruct((1, D_FF), F32),
                   jax.ShapeDtypeStruct((1, D_FF), F32)],
        scratch_shapes=[pltpu.VMEM((2, HALO + T, C), F32)],
        compiler_params=_params("parallel", "arbitrary"), name="convffn_bwd_a")(
            up_pre, up_pre, dact, conv_w, conv_w, conv_b, conv_b)
    dup, dwa, dwb, dba, dbb = res
    dconv_w = jnp.concatenate([dwa, dwb], axis=1)
    dconv_b = jnp.concatenate([dba, dbb], axis=1)
    return dup, dconv_w, dconv_b


def _convffn_bwd_b(dup, conv_w):
    S = dup.shape[1]
    T, C = _CONV_T, _CONV_C
    nj = D_FF // C
    per = T // HALO
    n_blk = S // T

    def body(d_ref, halo_ref, wa_ref, wb_ref, o_ref, ext_ref):
        i = pl.program_id(0)
        n = T + HALO
        for h, w_ref in enumerate((wa_ref, wb_ref)):
            ext_ref[h, 0:T, :] = d_ref[h].astype(F32)
            ext_ref[h, T:, :] = jnp.where(i == n_blk - 1, 0.0, halo_ref[h].astype(F32))
            ext = ext_ref[h]
            y = w_ref[2:3, :] * ext[0:T] + w_ref[1:2, :] * pltpu.roll(ext, n - 1, 0)[0:T]
            y = y + w_ref[0:1, :] * pltpu.roll(ext, n - 2, 0)[0:T]
            o_ref[h] = y.astype(BF16)

    blk = pl.BlockSpec((2, T, C), lambda i, j: (0, i, j))
    halo = pl.BlockSpec((2, HALO, C), lambda i, j: (0, jnp.minimum((i + 1) * per, S // HALO - 1), j))
    return pl.pallas_call(
        body, grid=(n_blk, nj),
        in_specs=[blk, halo, pl.BlockSpec((3, C), lambda i, j: (0, j)), pl.BlockSpec((3, C), lambda i, j: (0, j + nj))],
        out_specs=blk, out_shape=jax.ShapeDtypeStruct((2, S, D_FF), BF16),
        scratch_shapes=[pltpu.VMEM((2, T + HALO, C), F32)],
        compiler_params=_params("parallel", "parallel"), name="convffn_bwd_b")(dup, dup, conv_w, conv_w)


def _position():
    return lax.axis_index("x"), lax.axis_index("y"), lax.axis_index("c")


def _flip(pos, k):
    x, y, c = pos
    return (1 - x if k & 4 else x, 1 - y if k & 2 else y, 1 - c if k & 1 else c)


def _index(pos):
    return 4 * pos[0] + 2 * pos[1] + pos[2]


def _all_gather_weights(wp):
    R, C = wp.shape

    def body(w_ref, out_ref, send_sems, recv_sems, local_sem):
        me = _position()
        x, y, c = me
        sibling = (x, y, 1 - c)
        chips = [(1 - x, y), (x, 1 - y), (1 - x, 1 - y)]

        def slot(pos):
            return out_ref.at[_index(pos)]

        def copy(k, block, to, src=None):
            return pltpu.make_async_remote_copy(
                src_ref=slot(block) if src is None else src, dst_ref=slot(block),
                send_sem=send_sems.at[k], recv_sem=recv_sems.at[k], device_id=to, device_id_type=MESH)

        mine = pltpu.make_async_copy(w_ref, slot(me), local_sem)
        mine.start()
        first = [copy(0, me, sibling, src=w_ref)]
        first += [copy(1 + j, me, (*chip, c), src=w_ref) for j, chip in enumerate(chips)]
        for cp in first:
            cp.start()
        passed = [copy(4 + j, (*chip, c), sibling) for j, chip in enumerate(chips)]
        for j, chip in enumerate(chips):
            copy(1 + j, (*chip, c), me).wait_recv()
            passed[j].start()
        copy(0, sibling, me).wait_recv()
        for j, chip in enumerate(chips):
            copy(4 + j, (*chip, 1 - c), me).wait_recv()
        for cp in first + passed:
            cp.wait_send()
        mine.wait()

    return pl.pallas_call(
        body, out_shape=jax.ShapeDtypeStruct((N_DEV, R, C), wp.dtype),
        in_specs=[pl.BlockSpec(memory_space=pl.ANY)], out_specs=pl.BlockSpec(memory_space=pl.ANY),
        scratch_shapes=[pltpu.SemaphoreType.DMA((7,)), pltpu.SemaphoreType.DMA((7,)), pltpu.SemaphoreType.DMA],
        name="all_gather_weights")(wp)


def _exchange_grads(gp):
    _, R, C = gp.shape

    def body(g_ref, out_ref, send_sems, recv_sems, local_sem):
        me = _position()
        mine = pltpu.make_async_copy(g_ref.at[_index(me)], out_ref.at[_index(me)], local_sem)
        mine.start()
        copies = []
        for k in range(1, N_DEV):
            peer = _flip(me, k)
            copies.append(pltpu.make_async_remote_copy(
                src_ref=g_ref.at[_index(peer)], dst_ref=out_ref.at[_index(me)],
                send_sem=send_sems.at[k - 1], recv_sem=recv_sems.at[k - 1], device_id=peer, device_id_type=MESH))
        for cp in copies:
            cp.start()
        for k in range(1, N_DEV):
            peer = _flip(me, k)
            pltpu.make_async_remote_copy(
                src_ref=g_ref.at[_index(peer)], dst_ref=out_ref.at[_index(peer)],
                send_sem=send_sems.at[k - 1], recv_sem=recv_sems.at[k - 1], device_id=peer,
                device_id_type=MESH).wait_recv()
        for cp in copies:
            cp.wait_send()
        mine.wait()

    return pl.pallas_call(
        body, out_shape=jax.ShapeDtypeStruct(gp.shape, gp.dtype),
        in_specs=[pl.BlockSpec(memory_space=pl.ANY)], out_specs=pl.BlockSpec(memory_space=pl.ANY),
        scratch_shapes=[pltpu.SemaphoreType.DMA((7,)), pltpu.SemaphoreType.DMA((7,)), pltpu.SemaphoreType.DMA],
        name="exchange_grads")(gp)


def _adamw(w, g, m, v):
    m = ADAM_B1 * m + (1.0 - ADAM_B1) * g
    v = ADAM_B2 * v + (1.0 - ADAM_B2) * (g * g)
    m_hat = m / (1.0 - ADAM_B1 ** ADAM_STEP)
    v_hat = v / (1.0 - ADAM_B2 ** ADAM_STEP)
    delta = -ADAM_LR * (m_hat / (jnp.sqrt(v_hat) + ADAM_EPS) + ADAM_WD * w)
    return delta, m, v


def _sum_adamw(recv, w, m, v):
    _, R, C = recv.shape
    T = 256

    def body(r_ref, w_ref, m_ref, v_ref, g_out, d_out, m_out, v_out):
        g = r_ref[0].astype(F32)
        for s in range(1, N_DEV):
            g = g + r_ref[s].astype(F32)
        delta, mn, vn = _adamw(w_ref[...], g, m_ref[...], v_ref[...])
        g_out[...] = g
        d_out[...] = delta
        m_out[...] = mn
        v_out[...] = vn

    blk = pl.BlockSpec((T, C), lambda i: (i, 0))
    return pl.pallas_call(
        body, grid=(R // T,), in_specs=[pl.BlockSpec((N_DEV, T, C), lambda i: (0, i, 0)), blk, blk, blk],
        out_specs=[blk] * 4, out_shape=[jax.ShapeDtypeStruct((R, C), F32)] * 4,
        compiler_params=_params("parallel"), name="sum_adamw")(recv, w, m, v)


def _gather_vmem(p_ref, gath, send_sems, recv_sems):
    me = _position()
    gath[_index(me)] = p_ref[...]
    copies = []
    for k in range(1, N_DEV):
        peer = _flip(me, k)
        copies.append(pltpu.make_async_remote_copy(
            src_ref=p_ref, dst_ref=gath.at[_index(me)], send_sem=send_sems.at[k - 1],
            recv_sem=recv_sems.at[k - 1], device_id=peer, device_id_type=MESH))
    for cp in copies:
        cp.start()
    for k in range(1, N_DEV):
        peer = _flip(me, k)
        pltpu.make_async_remote_copy(
            src_ref=p_ref, dst_ref=gath.at[_index(peer)], send_sem=send_sems.at[k - 1],
            recv_sem=recv_sems.at[k - 1], device_id=peer, device_id_type=MESH).wait_recv()
    for cp in copies:
        cp.wait_send()


def _all_gather_small(p):
    R, C = p.shape

    def body(p_ref, out_ref, send_sems, recv_sems):
        _gather_vmem(p_ref, out_ref, send_sems, recv_sems)

    vm = pl.BlockSpec(memory_space=pltpu.VMEM)
    return pl.pallas_call(
        body, in_specs=[vm], out_specs=vm, out_shape=jax.ShapeDtypeStruct((N_DEV, R, C), F32),
        scratch_shapes=[pltpu.SemaphoreType.DMA((7,)), pltpu.SemaphoreType.DMA((7,))],
        name="all_gather_small")(p)


def _small_allreduce_adamw(part, w, m, v):
    R, C = part.shape

    def body(p_ref, w_ref, m_ref, v_ref, g_out, d_out, m_out, v_out, gath, send_sems, recv_sems):
        _gather_vmem(p_ref, gath, send_sems, recv_sems)
        g = gath[0]
        for s in range(1, N_DEV):
            g = g + gath[s]
        delta, mn, vn = _adamw(w_ref[...], g, m_ref[...], v_ref[...])
        g_out[...] = g
        d_out[...] = delta
        m_out[...] = mn
        v_out[...] = vn

    vm = pl.BlockSpec(memory_space=pltpu.VMEM)
    return pl.pallas_call(
        body, in_specs=[vm] * 4, out_specs=[vm] * 4, out_shape=[jax.ShapeDtypeStruct((R, C), F32)] * 4,
        scratch_shapes=[pltpu.VMEM((N_DEV, R, C), F32), pltpu.SemaphoreType.DMA((7,)), pltpu.SemaphoreType.DMA((7,))],
        name="small_allreduce_adamw")(part, w, m, v)


_SHARDED = (("w_in", (2048, 1216), 1), ("w_up", (2048, 1408), 1), ("w_down", (704, 2048), 0),
            ("w_out", (256, 2048), 0), ("w_pool_out", (1024, 256), 1), ("w_attn_out", (512, 256), 1),
            ("w_pool_lin", (4, 32, 256), 1), ("conv_w", (3, 1408), 1))
_SMALL = (("g_mix", 2048), ("b_gate", 4096), ("pool_scale", 1024), ("g_ffn", 2048), ("conv_b", 11264), ("g_final", 2048))


def _pack(parts, rows, dtype, lead=()):
    flat = [p.astype(dtype).reshape(lead + (-1,)) for p in parts]
    used = sum(f.shape[-1] for f in flat)
    flat.append(jnp.zeros(lead + (rows * PACK_COLS - used,), dtype))
    return jnp.concatenate(flat, axis=-1).reshape(lead + (rows, PACK_COLS))


def _unpack_shards(packed):
    flat = packed.reshape(-1)
    out, off = {}, 0
    for name, shape, _ in _SHARDED:
        n = math.prod(shape)
        out[name] = flat[off:off + n].reshape((1,) + shape)
        off += n
    return out


def _unpack_full(gathered):
    flat = gathered.reshape(N_DEV, -1)
    out, off = {}, 0
    for name, shape, axis in _SHARDED:
        n = math.prod(shape)
        w = flat[:, off:off + n].reshape((N_DEV,) + shape)
        w = jnp.moveaxis(w, 0, axis)
        full = shape[:axis] + (N_DEV * shape[axis],) + shape[axis + 1:]
        out[name] = w.reshape(full)
        off += n
    return out


def _pack_grads(grads):
    parts = []
    for name, shape, axis in _SHARDED:
        g = grads[name].astype(BF16)
        g = g.reshape(shape[:axis] + (N_DEV, shape[axis]) + shape[axis + 1:])
        parts.append(jnp.moveaxis(g, axis, 0))
    return _pack(parts, PACK_ROWS, BF16, lead=(N_DEV,))


def _pack_small(vals):
    return _pack([vals[n] for n, _ in _SMALL], SMALL_ROWS, F32)


def _unpack_small(packed, like):
    flat = packed.reshape(-1)
    out, off = {}, 0
    for name, n in _SMALL:
        out[name] = flat[off:off + n].reshape(like[name].shape)
        off += n
    return out


def _local_step(x, target, W, small):
    S = x.shape[0]
    o_qkv, o_gate = POOL_WIDTH, POOL_WIDTH + QKV_WIDTH
    w_in = W["w_in"]

    h1 = _rms_fwd("rms1_fwd", x, small["g_mix"])
    u = _mm_nn("proj_u", h1, w_in, tm=1024, tn=512, tk=2048, o_dtype=F32, n_cols=POOL_WIDTH, b_col_off=0)
    qkv = _mm_nn("proj_qkv", h1, w_in, tm=1024, tn=512, tk=2048, o_dtype=BF16, n_cols=QKV_WIDTH, b_col_off=o_qkv)
    gates_pre = _mm_nn("proj_gates", h1, w_in, tm=1024, tn=512, tk=2048, o_dtype=F32, n_cols=2 * D_MODEL,
                       b_col_off=o_gate)
    pool_y = _pool_fwd(u, W["w_pool_lin"], small["pool_scale"])
    outs, lses = zip(*[_attn_fwd(gi, qkv) for gi in range(3)])
    attn_y = _merge_fwd(outs, lses)
    mixed, y_pool, y_attn = _mix_fwd(pool_y, attn_y, W["w_pool_out"], W["w_attn_out"], gates_pre, small["b_gate"])
    x1 = _mm_nn("out_proj", mixed, W["w_out"], tm=1024, tn=512, tk=2048, o_dtype=F32, res=x)

    h2 = _rms_fwd("rms2_fwd", x1, small["g_ffn"])
    nj = D_FF // 512
    up_pre = _mm("up_proj", h2, W["w_up"], mode="nn", grid=(S // 1024, 2 * nj, 1),
                 a_spec=pl.BlockSpec((1024, D_MODEL), lambda i, j, k: (i, 0)),
                 b_spec=pl.BlockSpec((D_MODEL, 512), lambda i, j, k: (0, j)),
                 o_spec=pl.BlockSpec((None, 1024, 512), lambda i, j, k: (j // nj, i, j % nj)),
                 o_shape=(2, S, D_FF), o_dtype=BF16, acc_shape=(1024, 512))
    act = _convffn_fwd(up_pre, small["conv_w_full"], small["conv_b"])
    x2 = _mm_nn("down_proj", act, W["w_down"], tm=1024, tn=512, tk=D_FF // 2, o_dtype=F32, res=x1)

    dx2, dx2_b, dg_final, sq_cols = _final_loss("final_loss", x2, small["g_final"], target)

    dact = _mm_nt("d_act", dx2_b, W["w_down"], tm=1024, tn=512, tk=2048, o_dtype=BF16)
    dw_down = _mm_tn("dw_down", act, dx2_b, tm=512, tn=2048, tk=512)
    dup, dconv_w, dconv_b = _convffn_bwd_a(up_pre, dact, small["conv_w_full"], small["conv_b"])
    dup_pre = _convffn_bwd_b(dup, small["conv_w_full"])
    tkh = D_FF // 2
    dh2 = _mm("d_h2", dup_pre, W["w_up"], mode="nt", grid=(S // 1024, D_MODEL // 1024, 4),
              a_spec=pl.BlockSpec((None, 1024, tkh), lambda i, j, k: (k // 2, i, k % 2)),
              b_spec=pl.BlockSpec((1024, tkh), lambda i, j, k: (j, k)),
              o_spec=pl.BlockSpec((1024, 1024), lambda i, j, k: (i, j)),
              o_shape=(S, D_MODEL), o_dtype=F32, acc_shape=(1024, 1024))
    tnu = 1408
    nju = D_FF // tnu
    dw_up = _mm("dw_up", h2, dup_pre, mode="tn", grid=(D_MODEL // 1024, 2 * nju, S // 512),
                a_spec=pl.BlockSpec((512, 1024), lambda i, j, k: (k, i)),
                b_spec=pl.BlockSpec((None, 512, tnu), lambda i, j, k: (j // nju, k, j % nju)),
                o_spec=pl.BlockSpec((1024, tnu), lambda i, j, k: (i, j)),
                o_shape=(D_MODEL, 2 * D_FF), o_dtype=F32, acc_shape=(1024, tnu))
    dx1, dx1_b, dg_ffn = _rms_bwd("rms2_bwd", x1, small["g_ffn"], dh2, dx2)

    dmixed = _mm_nt("d_mixed", dx1_b, W["w_out"], tm=1024, tn=512, tk=2048, o_dtype=BF16)
    dw_out = _mm_tn("dw_out", mixed, dx1_b, tm=1024, tn=2048, tk=512)
    dy_pool, dy_attn, dgates, db_gate = _mix_bwd(dmixed, gates_pre, small["b_gate"], y_pool, y_attn)
    dpool_y = _mm_nt("d_pool_y", dy_pool, W["w_pool_out"], tm=1024, tn=512, tk=2048, o_dtype=F32)
    dw_pool_out = _mm_tn("dw_pool_out", pool_y, dy_pool, tm=1024, tn=2048, tk=512)
    dattn_y = _mm_nt("d_attn_y", dy_attn, W["w_attn_out"], tm=1024, tn=512, tk=2048, o_dtype=F32)
    dw_attn_out = _mm_tn("dw_attn_out", attn_y, dy_attn, tm=512, tn=2048, tk=512)
    e, dw_pool_lin, dpool_scale = _pool_bwd_a(u, dpool_y, W["w_pool_lin"], small["pool_scale"])
    du = _pool_bwd_b(e)
    dos, dps = _merge_bwd(dattn_y, outs, lses)
    dqkv = [_attn_bwd(gi, qkv, dos[gi], lses[gi], dps[gi]) for gi in range(3)]
    dproj = jnp.concatenate([du] + [dqkv[gi][sec] for sec in range(3) for gi in range(3)] + [dgates], axis=1)
    dh1 = _mm_nt("d_h1", dproj, w_in, tm=1024, tn=1024, tk=IN_WIDTH // 4, o_dtype=F32)
    dw_in = _mm_tn("dw_in", h1, dproj, tm=1024, tn=IN_WIDTH // 4, tk=512)
    grad_x, _, dg_mix = _rms_bwd("rms1_bwd", x, small["g_mix"], dh1, dx1)

    grads = dict(w_in=dw_in, w_up=dw_up, w_down=dw_down, w_out=dw_out, w_pool_out=dw_pool_out,
                 w_attn_out=dw_attn_out, w_pool_lin=dw_pool_lin, conv_w=dconv_w)
    small_grads = dict(g_mix=dg_mix, b_gate=db_gate, pool_scale=dpool_scale, g_ffn=dg_ffn, conv_b=dconv_b,
                       g_final=dg_final)
    return sq_cols, grad_x, grads, small_grads


def kernel(x, g_mix, w_in, b_gate, w_pool_lin, pool_scale, w_pool_out, w_attn_out, w_out, g_ffn, w_up, conv_w, conv_b, w_down, g_final, loss_target, m_g_mix, m_w_in, m_b_gate, m_w_pool_lin, m_pool_scale, m_w_pool_out, m_w_attn_out, m_w_out, m_g_ffn, m_w_up, m_conv_w, m_conv_b, m_w_down, m_g_final, v_g_mix, v_w_in, v_b_gate, v_w_pool_lin, v_pool_scale, v_w_pool_out, v_w_attn_out, v_w_out, v_g_ffn, v_w_up, v_conv_w, v_conv_b, v_w_down, v_g_final):
    given = dict(g_mix=g_mix, w_in=w_in, b_gate=b_gate, w_pool_lin=w_pool_lin, pool_scale=pool_scale,
                 w_pool_out=w_pool_out, w_attn_out=w_attn_out, w_out=w_out, g_ffn=g_ffn, w_up=w_up, conv_w=conv_w,
                 conv_b=conv_b, w_down=w_down, g_final=g_final)
    mom_m = dict(g_mix=m_g_mix, w_in=m_w_in, b_gate=m_b_gate, w_pool_lin=m_w_pool_lin, pool_scale=m_pool_scale,
                 w_pool_out=m_w_pool_out, w_attn_out=m_w_attn_out, w_out=m_w_out, g_ffn=m_g_ffn, w_up=m_w_up,
                 conv_w=m_conv_w, conv_b=m_conv_b, w_down=m_w_down, g_final=m_g_final)
    mom_v = dict(g_mix=v_g_mix, w_in=v_w_in, b_gate=v_b_gate, w_pool_lin=v_w_pool_lin, pool_scale=v_pool_scale,
                 w_pool_out=v_w_pool_out, w_attn_out=v_w_attn_out, w_out=v_w_out, g_ffn=v_g_ffn, w_up=v_w_up,
                 conv_w=v_conv_w, conv_b=v_conv_b, w_down=v_w_down, g_final=v_g_final)
    sharded = [n for n, _, _ in _SHARDED]

    gathered = _all_gather_weights(_pack([given[n] for n in sharded], PACK_ROWS, BF16))
    W = _unpack_full(gathered)
    conv_w_all = _all_gather_small(jnp.pad(conv_w[0], ((0, 5), (0, 0))))
    conv_w_full = jnp.moveaxis(conv_w_all[:, :3, :], 0, 1).reshape(3, 2 * D_FF)
    small = dict(g_mix=g_mix, b_gate=b_gate, pool_scale=pool_scale, g_ffn=g_ffn, conv_b=conv_b,
                 g_final=g_final.reshape(1, D_MODEL), conv_w_full=conv_w_full)

    sq_cols, grad_x, grads, small_grads = _local_step(x[0], loss_target[0], W, small)
    loss = lax.psum(0.5 / D_MODEL * jnp.sum(sq_cols), AXES)

    recv = _exchange_grads(_pack_grads(grads))
    packed = [_pack([d[n] for n in sharded], PACK_ROWS, F32) for d in (given, mom_m, mom_v)]
    res_sharded = [_unpack_shards(p) for p in _sum_adamw(recv, *packed)]

    packed_small = [_pack_small(d) for d in (given, mom_m, mom_v)]
    res_small = [_unpack_small(p, given) for p in _small_allreduce_adamw(_pack_small(small_grads), *packed_small)]

    order = ["g_mix", "w_in", "b_gate", "w_pool_lin", "pool_scale", "w_pool_out", "w_attn_out", "w_out", "g_ffn",
             "w_up", "conv_w", "conv_b", "w_down", "g_final"]
    outs = [loss, grad_x[None]]
    for kind in range(4):
        for n in order:
            outs.append(res_sharded[kind][n] if n in res_sharded[kind] else res_small[kind][n])
    return tuple(outs)
```

```python
import math

import jax
import jax.numpy as jnp
from jax import lax
from jax.experimental import pallas as pl
from jax.experimental.pallas import tpu as pltpu

F32 = jnp.float32
BF16 = jnp.bfloat16
MESH = pl.DeviceIdType.MESH
AXES = ("x", "y", "c")
N_DEV = 8

D_MODEL = 2048
POOL_WINDOWS = (2, 4, 8, 16)
POOL_GROUP_WIDTH = 256
POOL_WIDTH = 1024
ATTN_GROUPS = ((128, 1), (512, 4), (2048, 16))
SPAN = 128
HEADS_PER_GROUP = 4
N_ATTN_HEADS = 12
HEAD_DIM = 128
ATTN_WIDTH = 1536
QKV_WIDTH = 3 * ATTN_WIDTH
ATTN_OUT_WIDTH = 512
IN_WIDTH = 9728
D_FF = 5632
RMS_EPS = 1e-6
ADAM_LR, ADAM_B1, ADAM_B2, ADAM_EPS, ADAM_WD, ADAM_STEP = 0.001, 0.9, 0.999, 1e-08, 0.01, 10

VMEM_LIMIT_BYTES = 48 * 1024 * 1024
HALO = 16
PACK_COLS = 1024
SMALL_ROWS = 24
NEG_BIG = -1e30

_DN = {"nn": (((1,), (0,)), ((), ())), "nt": (((1,), (1,)), ((), ())), "tn": (((0,), (0,)), ((), ()))}


def _params(*sem):
    return pltpu.CompilerParams(dimension_semantics=sem, vmem_limit_bytes=VMEM_LIMIT_BYTES)


def _dot(a, b, mode):
    return lax.dot_general(a.astype(BF16), b.astype(BF16), _DN[mode], preferred_element_type=F32)


def _mm(name, a, b, *, mode, grid, a_spec, b_spec, o_spec, o_shape, o_dtype, acc_shape, res=None, res_spec=None):
    nk = grid[2]

    def body(*refs):
        if res is not None:
            a_ref, b_ref, r_ref, o_ref = refs[:4]
            scr = refs[4:]
        else:
            a_ref, b_ref, o_ref = refs[:3]
            r_ref = None
            scr = refs[3:]
        prod = _dot(a_ref[...], b_ref[...], mode)

        def finish(val):
            if r_ref is not None:
                val = val + r_ref[...]
            o_ref[...] = val.astype(o_ref.dtype)

        if nk == 1:
            finish(prod)
        else:
            acc = scr[0]
            k = pl.program_id(2)

            @pl.when(k == 0)
            def _():
                acc[...] = prod

            @pl.when(k > 0)
            def _():
                acc[...] += prod

            @pl.when(k == nk - 1)
            def _():
                finish(acc[...])

    in_specs = [a_spec, b_spec]
    args = [a, b]
    if res is not None:
        in_specs.append(res_spec)
        args.append(res)
    return pl.pallas_call(
        body, grid=grid, in_specs=in_specs, out_specs=o_spec,
        out_shape=jax.ShapeDtypeStruct(o_shape, o_dtype),
        scratch_shapes=[] if nk == 1 else [pltpu.VMEM(acc_shape, F32)],
        compiler_params=_params("parallel", "parallel", "arbitrary"), name=name)(*args)


def _mm_nn(name, a, b, *, tm, tn, tk, o_dtype, n_cols=None, b_col_off=0, res=None):
    M, K = a.shape
    N = b.shape[1] if n_cols is None else n_cols
    off = b_col_off // tn
    return _mm(name, a, b, mode="nn", grid=(M // tm, N // tn, K // tk),
               a_spec=pl.BlockSpec((tm, tk), lambda i, j, k: (i, k)),
               b_spec=pl.BlockSpec((tk, tn), lambda i, j, k: (k, j + off)),
               o_spec=pl.BlockSpec((tm, tn), lambda i, j, k: (i, j)),
               o_shape=(M, N), o_dtype=o_dtype, acc_shape=(tm, tn), res=res,
               res_spec=pl.BlockSpec((tm, tn), lambda i, j, k: (i, j)))


def _mm_nt(name, a, b, *, tm, tn, tk, o_dtype, n_rows=None, b_row_off=0):
    M, K = a.shape
    N = b.shape[0] if n_rows is None else n_rows
    off = b_row_off // tn
    return _mm(name, a, b, mode="nt", grid=(M // tm, N // tn, K // tk),
               a_spec=pl.BlockSpec((tm, tk), lambda i, j, k: (i, k)),
               b_spec=pl.BlockSpec((tn, tk), lambda i, j, k: (j + off, k)),
               o_spec=pl.BlockSpec((tm, tn), lambda i, j, k: (i, j)),
               o_shape=(M, N), o_dtype=o_dtype, acc_shape=(tm, tn))


def _mm_tn(name, a, b, *, tm, tn, tk):
    K, M = a.shape
    N = b.shape[1]
    return _mm(name, a, b, mode="tn", grid=(M // tm, N // tn, K // tk),
               a_spec=pl.BlockSpec((tk, tm), lambda i, j, k: (k, i)),
               b_spec=pl.BlockSpec((tk, tn), lambda i, j, k: (k, j)),
               o_spec=pl.BlockSpec((tm, tn), lambda i, j, k: (i, j)),
               o_shape=(M, N), o_dtype=BF16, acc_shape=(tm, tn))


def _rms_fwd(name, x, g):
    S, D = x.shape
    T = 512

    def body(x_ref, g_ref, h_ref):
        xv = x_ref[...]
        r = lax.rsqrt(jnp.mean(xv * xv, axis=-1, keepdims=True) + RMS_EPS)
        h_ref[...] = (xv * r * g_ref[...]).astype(BF16)

    return pl.pallas_call(
        body, grid=(S // T,),
        in_specs=[pl.BlockSpec((T, D), lambda i: (i, 0)), pl.BlockSpec((1, D), lambda i: (0, 0))],
        out_specs=pl.BlockSpec((T, D), lambda i: (i, 0)),
        out_shape=jax.ShapeDtypeStruct((S, D), BF16), compiler_params=_params("parallel"), name=name)(x, g)


def _rms_bwd(name, x, g, dh, dres):
    S, D = x.shape
    T = 256

    def body(x_ref, g_ref, dh_ref, dres_ref, dx_ref, dxb_ref, dg_ref):
        i = pl.program_id(0)
        xv = x_ref[...]
        r = lax.rsqrt(jnp.mean(xv * xv, axis=-1, keepdims=True) + RMS_EPS)
        xhat = xv * r
        dhv = dh_ref[...].astype(F32)
        gdh = dhv * g_ref[...]
        c = jnp.mean(xhat * gdh, axis=-1, keepdims=True)
        dx = dres_ref[...] + r * (gdh - xhat * c)
        dx_ref[...] = dx
        dxb_ref[...] = dx.astype(BF16)
        part = jnp.sum(dhv * xhat, axis=0, keepdims=True)

        @pl.when(i == 0)
        def _():
            dg_ref[...] = part

        @pl.when(i > 0)
        def _():
            dg_ref[...] += part

    row = pl.BlockSpec((T, D), lambda i: (i, 0))
    vec = pl.BlockSpec((1, D), lambda i: (0, 0))
    return pl.pallas_call(
        body, grid=(S // T,), in_specs=[row, vec, row, row], out_specs=[row, row, vec],
        out_shape=[jax.ShapeDtypeStruct((S, D), F32), jax.ShapeDtypeStruct((S, D), BF16),
                   jax.ShapeDtypeStruct((1, D), F32)],
        compiler_params=_params("arbitrary"), name=name)(x, g, dh, dres)


def _final_loss(name, x, g, target):
    S, D = x.shape
    T = 256

    def body(x_ref, g_ref, t_ref, dx_ref, dxb_ref, dg_ref, sq_ref):
        i = pl.program_id(0)
        xv = x_ref[...]
        r = lax.rsqrt(jnp.mean(xv * xv, axis=-1, keepdims=True) + RMS_EPS)
        xhat = xv * r
        err = xhat * g_ref[...] - t_ref[...]
        dy = err * (1.0 / D)
        gdy = dy * g_ref[...]
        c = jnp.mean(xhat * gdy, axis=-1, keepdims=True)
        dx = r * (gdy - xhat * c)
        dx_ref[...] = dx
        dxb_ref[...] = dx.astype(BF16)
        dg_part = jnp.sum(dy * xhat, axis=0, keepdims=True)
        sq_part = jnp.sum(err * err, axis=0, keepdims=True)

        @pl.when(i == 0)
        def _():
            dg_ref[...] = dg_part
            sq_ref[...] = sq_part

        @pl.when(i > 0)
        def _():
            dg_ref[...] += dg_part
            sq_ref[...] += sq_part

    row = pl.BlockSpec((T, D), lambda i: (i, 0))
    vec = pl.BlockSpec((1, D), lambda i: (0, 0))
    return pl.pallas_call(
        body, grid=(S // T,), in_specs=[row, vec, row], out_specs=[row, row, vec, vec],
        out_shape=[jax.ShapeDtypeStruct((S, D), F32), jax.ShapeDtypeStruct((S, D), BF16),
                   jax.ShapeDtypeStruct((1, D), F32), jax.ShapeDtypeStruct((1, D), F32)],
        compiler_params=_params("arbitrary"), name=name)(x, g, target)


def _prev_halo_spec(T, C, col_map):
    per = T // HALO
    return pl.BlockSpec((HALO, C), lambda *g: (jnp.maximum(g[0] * per - 1, 0), col_map(*g)))


def _pool_window_sums(ext, T, i):
    t = i * T + lax.broadcasted_iota(jnp.int32, (T, 1), 0)
    outs = []
    cnts = []
    for gi, w in enumerate(POOL_WINDOWS):
        s = ext[:, gi * POOL_GROUP_WIDTH:(gi + 1) * POOL_GROUP_WIDTH]
        sh = 1
        while sh < w:
            s = s + pltpu.roll(s, sh, 0)
            sh *= 2
        cnt = jnp.minimum(t + 1, w).astype(F32)
        outs.append(s[HALO:] / cnt - ext[HALO:, gi * POOL_GROUP_WIDTH:(gi + 1) * POOL_GROUP_WIDTH])
        cnts.append(cnt)
    return outs, cnts


def _load_ext(ext_ref, halo_ref, blk_ref, first):
    ext_ref[0:HALO, :] = jnp.where(first, 0.0, halo_ref[...].astype(F32))
    ext_ref[HALO:, :] = blk_ref[...].astype(F32)


def _pool_fwd(u, w_lin, scale):
    S = u.shape[0]
    T = 512
    W = POOL_WIDTH

    def body(u_ref, halo_ref, wl_ref, sc_ref, y_ref, ext_ref):
        i = pl.program_id(0)
        _load_ext(ext_ref, halo_ref, u_ref, i == 0)
        pooled, _ = _pool_window_sums(ext_ref[...], T, i)
        for gi in range(4):
            z = _dot(pooled[gi], wl_ref[gi], "nn")
            cs = slice(gi * POOL_GROUP_WIDTH, (gi + 1) * POOL_GROUP_WIDTH)
            y_ref[:, cs] = (z * sc_ref[:, cs]).astype(BF16)

    return pl.pallas_call(
        body, grid=(S // T,),
        in_specs=[pl.BlockSpec((T, W), lambda i: (i, 0)), _prev_halo_spec(T, W, lambda i: 0),
                  pl.BlockSpec((4, 256, 256), lambda i: (0, 0, 0)), pl.BlockSpec((1, W), lambda i: (0, 0))],
        out_specs=pl.BlockSpec((T, W), lambda i: (i, 0)),
        out_shape=jax.ShapeDtypeStruct((S, W), BF16),
        scratch_shapes=[pltpu.VMEM((HALO + T, W), F32)],
        compiler_params=_params("parallel"), name="pool_fwd")(u, u, w_lin, scale)


def _pool_bwd_a(u, dy, w_lin, scale):
    S = u.shape[0]
    T = 512
    W = POOL_WIDTH

    def body(u_ref, halo_ref, dy_ref, wl_ref, sc_ref, e_ref, dwl_ref, dsc_ref, ext_ref):
        i = pl.program_id(0)
        _load_ext(ext_ref, halo_ref, u_ref, i == 0)
        pooled, cnts = _pool_window_sums(ext_ref[...], T, i)
        for gi in range(4):
            cs = slice(gi * POOL_GROUP_WIDTH, (gi + 1) * POOL_GROUP_WIDTH)
            z = _dot(pooled[gi], wl_ref[gi], "nn")
            dyg = dy_ref[:, cs]
            dsc = jnp.sum(dyg * z, axis=0, keepdims=True)
            dz = dyg * sc_ref[:, cs]
            dwl = _dot(pooled[gi], dz, "tn")
            dpooled = _dot(dz, wl_ref[gi], "nt")
            e_ref[:, cs] = dpooled / cnts[gi]

            @pl.when(i == 0)
            def _():
                dwl_ref[gi] = dwl
                dsc_ref[:, cs] = dsc

            @pl.when(i > 0)
            def _():
                dwl_ref[gi] += dwl
                dsc_ref[:, cs] += dsc

    return pl.pallas_call(
        body, grid=(S // T,),
        in_specs=[pl.BlockSpec((T, W), lambda i: (i, 0)), _prev_halo_spec(T, W, lambda i: 0),
                  pl.BlockSpec((T, W), lambda i: (i, 0)),
                  pl.BlockSpec((4, 256, 256), lambda i: (0, 0, 0)), pl.BlockSpec((1, W), lambda i: (0, 0))],
        out_specs=[pl.BlockSpec((T, W), lambda i: (i, 0)), pl.BlockSpec((4, 256, 256), lambda i: (0, 0, 0)),
                   pl.BlockSpec((1, W), lambda i: (0, 0))],
        out_shape=[jax.ShapeDtypeStruct((S, W), F32), jax.ShapeDtypeStruct((4, 256, 256), F32),
                   jax.ShapeDtypeStruct((1, W), F32)],
        scratch_shapes=[pltpu.VMEM((HALO + T, W), F32)],
        compiler_params=_params("arbitrary"), name="pool_bwd_a")(u, u, dy, w_lin, scale)


def _pool_bwd_b(e):
    S = e.shape[0]
    T = 512
    W = POOL_WIDTH
    per = T // HALO
    n_blk = S // T

    def body(e_ref, halo_ref, du_ref, ext_ref):
        i = pl.program_id(0)
        ext_ref[0:T, :] = e_ref[...]
        ext_ref[T:, :] = jnp.where(i == n_blk - 1, 0.0, halo_ref[...])
        ext = ext_ref[...]
        n = T + HALO
        t = i * T + lax.broadcasted_iota(jnp.int32, (T, 1), 0)
        for gi, w in enumerate(POOL_WINDOWS):
            cs = slice(gi * POOL_GROUP_WIDTH, (gi + 1) * POOL_GROUP_WIDTH)
            s = ext[:, cs]
            sh = 1
            while sh < w:
                s = s + pltpu.roll(s, n - sh, 0)
                sh *= 2
            cnt = jnp.minimum(t + 1, w).astype(F32)
            du_ref[:, cs] = (s[0:T] - ext[0:T, cs] * cnt).astype(BF16)

    return pl.pallas_call(
        body, grid=(n_blk,),
        in_specs=[pl.BlockSpec((T, W), lambda i: (i, 0)),
                  pl.BlockSpec((HALO, W), lambda i: (jnp.minimum((i + 1) * per, S // HALO - 1), 0))],
        out_specs=pl.BlockSpec((T, W), lambda i: (i, 0)),
        out_shape=jax.ShapeDtypeStruct((S, W), BF16),
        scratch_shapes=[pltpu.VMEM((T + HALO, W), F32)],
        compiler_params=_params("parallel"), name="pool_bwd_b")(e, e)


def _slope(head):
    return 2.0 ** (-8.0 * (head + 1) / N_ATTN_HEADS)


def _attn_masks(b):
    row = lax.broadcasted_iota(jnp.int32, (SPAN, SPAN), 0)
    col = lax.broadcasted_iota(jnp.int32, (SPAN, SPAN), 1)
    dist_cur = (row - col).astype(F32)
    valid_cur = col <= row
    valid_prev = jnp.logical_and(col >= row, b > 0)
    return dist_cur, valid_cur, valid_prev


def _attn_fwd(gi, qkv):
    S = qkv.shape[0]
    d = ATTN_GROUPS[gi][1]
    L = S // d
    nb = L // SPAN
    view = qkv.reshape(L, d * QKV_WIDTH)
    W = ATTN_OUT_WIDTH
    ncb = QKV_WIDTH // W
    scale = HEAD_DIM ** -0.5

    def body(q_ref, kc_ref, kp_ref, vc_ref, vp_ref, o_ref, lse_ref):
        b = pl.program_id(1)
        dist_cur, valid_cur, valid_prev = _attn_masks(b)
        for hi in range(HEADS_PER_GROUP):
            hs = slice(hi * HEAD_DIM, (hi + 1) * HEAD_DIM)
            sl = _slope(gi * HEADS_PER_GROUP + hi) * d
            q = q_ref[:, hs]
            s_cur = _dot(q, kc_ref[:, hs], "nt") * scale + (-sl) * dist_cur
            s_prev = _dot(q, kp_ref[:, hs], "nt") * scale + (-sl) * (dist_cur + SPAN)
            s_cur = jnp.where(valid_cur, s_cur, NEG_BIG)
            s_prev = jnp.where(valid_prev, s_prev, NEG_BIG)
            m = jnp.maximum(jnp.max(s_cur, axis=-1, keepdims=True), jnp.max(s_prev, axis=-1, keepdims=True))
            p_cur = jnp.exp(s_cur - m)
            p_prev = jnp.exp(s_prev - m)
            l = jnp.sum(p_cur, axis=-1, keepdims=True) + jnp.sum(p_prev, axis=-1, keepdims=True)
            o = (_dot(p_cur, vc_ref[:, hs], "nn") + _dot(p_prev, vp_ref[:, hs], "nn")) / l
            o_ref[:, hs] = o
            lse_ref[:, hs] = jnp.broadcast_to(m + jnp.log(l), (SPAN, HEAD_DIM))

    def cur(sec):
        return pl.BlockSpec((SPAN, W), lambda r, b: (b, r * ncb + 3 * sec + gi))

    def prev(sec):
        return pl.BlockSpec((SPAN, W), lambda r, b: (jnp.maximum(b - 1, 0), r * ncb + 3 * sec + gi))

    out = pl.BlockSpec((SPAN, W), lambda r, b: (b, r))
    o, lse = pl.pallas_call(
        body, grid=(d, nb), in_specs=[cur(0), cur(1), prev(1), cur(2), prev(2)], out_specs=[out, out],
        out_shape=[jax.ShapeDtypeStruct((L, d * W), F32), jax.ShapeDtypeStruct((L, d * W), F32)],
        compiler_params=_params("parallel", "parallel"), name=f"attn_fwd_g{gi}")(view, view, view, view, view)
    return o.reshape(S, W), lse.reshape(S, W)


def _attn_bwd(gi, qkv, do, lse, dp):
    S = qkv.shape[0]
    d = ATTN_GROUPS[gi][1]
    L = S // d
    nb = L // SPAN
    W = ATTN_OUT_WIDTH
    ncb = QKV_WIDTH // W
    scale = HEAD_DIM ** -0.5
    view = qkv.reshape(L, d * QKV_WIDTH)
    do_v, lse_v, dp_v = (a.reshape(L, d * W) for a in (do, lse, dp))

    def body(q_ref, kc_ref, vc_ref, kp_ref, vp_ref, qn_ref, do_ref, don_ref, lse_ref, lsen_ref, dp_ref, dpn_ref,
             dq_ref, dk_ref, dv_ref):
        b = pl.program_id(1)
        dist_cur, valid_cur, valid_prev = _attn_masks(b)
        row = lax.broadcasted_iota(jnp.int32, (SPAN, SPAN), 0)
        col = lax.broadcasted_iota(jnp.int32, (SPAN, SPAN), 1)
        valid_next = jnp.logical_and(col >= row, b < nb - 1)
        for hi in range(HEADS_PER_GROUP):
            hs = slice(hi * HEAD_DIM, (hi + 1) * HEAD_DIM)
            sl = _slope(gi * HEADS_PER_GROUP + hi) * d
            q, kc, vc, kp, vp, qn = (r[:, hs] for r in (q_ref, kc_ref, vc_ref, kp_ref, vp_ref, qn_ref))
            dov, don = do_ref[:, hs], don_ref[:, hs]
            lse_b, lse_n, dp_b, dp_n = lse_ref[:, hs], lsen_ref[:, hs], dp_ref[:, hs], dpn_ref[:, hs]
            bias_cur = (-sl) * dist_cur
            bias_far = (-sl) * (dist_cur + SPAN)
            p_cur = jnp.where(valid_cur, jnp.exp(_dot(q, kc, "nt") * scale + bias_cur - lse_b), 0.0)
            p_prev = jnp.where(valid_prev, jnp.exp(_dot(q, kp, "nt") * scale + bias_far - lse_b), 0.0)
            ds_cur = p_cur * (_dot(dov, vc, "nt") - dp_b)
            ds_prev = p_prev * (_dot(dov, vp, "nt") - dp_b)
            dq_ref[:, hs] = ((_dot(ds_cur, kc, "nn") + _dot(ds_prev, kp, "nn")) * scale).astype(BF16)
            p_next = jnp.where(valid_next, jnp.exp(_dot(qn, kc, "nt") * scale + bias_far - lse_n), 0.0)
            ds_next = p_next * (_dot(don, vc, "nt") - dp_n)
            dv_ref[:, hs] = (_dot(p_cur, dov, "tn") + _dot(p_next, don, "tn")).astype(BF16)
            dk_ref[:, hs] = ((_dot(ds_cur, q, "tn") + _dot(ds_next, qn, "tn")) * scale).astype(BF16)

    def qkv_spec(sec, shift):
        def imap(r, b):
            return (jnp.clip(b + shift, 0, nb - 1), r * ncb + 3 * sec + gi)
        return pl.BlockSpec((SPAN, W), imap)

    def act_spec(shift):
        return pl.BlockSpec((SPAN, W), lambda r, b: (jnp.clip(b + shift, 0, nb - 1), r))

    out = act_spec(0)
    dq, dk, dv = pl.pallas_call(
        body, grid=(d, nb),
        in_specs=[qkv_spec(0, 0), qkv_spec(1, 0), qkv_spec(2, 0), qkv_spec(1, -1), qkv_spec(2, -1), qkv_spec(0, 1),
                  act_spec(0), act_spec(1), act_spec(0), act_spec(1), act_spec(0), act_spec(1)],
        out_specs=[out, out, out],
        out_shape=[jax.ShapeDtypeStruct((L, d * W), BF16)] * 3,
        compiler_params=_params("parallel", "parallel"), name=f"attn_bwd_g{gi}")(
            view, view, view, view, view, view, do_v, do_v, lse_v, lse_v, dp_v, dp_v)
    return dq.reshape(S, W), dk.reshape(S, W), dv.reshape(S, W)


def _group_weights(l_refs):
    l0, l1, l2 = (r[...] for r in l_refs)
    m = jnp.maximum(jnp.maximum(l0, l1), l2)
    e = [jnp.exp(l0 - m), jnp.exp(l1 - m), jnp.exp(l2 - m)]
    tot = e[0] + e[1] + e[2]
    return [ei / tot for ei in e]


def _merge_fwd(outs, lses):
    S, W = outs[0].shape
    T = 512

    def body(o0, o1, o2, l0, l1, l2, y_ref):
        w = _group_weights((l0, l1, l2))
        y_ref[...] = (w[0] * o0[...] + w[1] * o1[...] + w[2] * o2[...]).astype(BF16)

    blk = pl.BlockSpec((T, W), lambda i: (i, 0))
    return pl.pallas_call(
        body, grid=(S // T,), in_specs=[blk] * 6, out_specs=blk, out_shape=jax.ShapeDtypeStruct((S, W), BF16),
        compiler_params=_params("parallel"), name="merge_fwd")(*outs, *lses)


def _merge_bwd(dy, outs, lses):
    S, W = outs[0].shape
    T = 512

    def body(dy_ref, o0, o1, o2, l0, l1, l2, do0, do1, do2, dp0, dp1, dp2):
        w = _group_weights((l0, l1, l2))
        dyv = dy_ref[...]
        y = w[0] * o0[...] + w[1] * o1[...] + w[2] * o2[...]
        prod = dyv * y
        dots = []
        for hi in range(HEADS_PER_GROUP):
            hs = slice(hi * HEAD_DIM, (hi + 1) * HEAD_DIM)
            dots.append(jnp.broadcast_to(jnp.sum(prod[:, hs], axis=-1, keepdims=True), (T, HEAD_DIM)))
        dot_b = jnp.concatenate(dots, axis=1)
        for wg, do_ref, dp_ref in zip(w, (do0, do1, do2), (dp0, dp1, dp2)):
            do_ref[...] = (wg * dyv).astype(BF16)
            dp_ref[...] = wg * dot_b

    blk = pl.BlockSpec((T, W), lambda i: (i, 0))
    res = pl.pallas_call(
        body, grid=(S // T,), in_specs=[blk] * 7, out_specs=[blk] * 6,
        out_shape=[jax.ShapeDtypeStruct((S, W), BF16)] * 3 + [jax.ShapeDtypeStruct((S, W), F32)] * 3,
        compiler_params=_params("parallel"), name="merge_bwd")(dy, *outs, *lses)
    return res[:3], res[3:]


def _mix_fwd(pool_y, attn_y, w_pool_out_t, w_attn_out_t, gates_pre, b_gate):
    S = pool_y.shape[0]
    D = D_MODEL
    tm, tn = 512, 512
    nj = D // tn

    def body(p_ref, a_ref, wp_ref, wa_ref, g0_ref, g1_ref, b0_ref, b1_ref, mixed_ref, yp_ref, ya_ref):
        yp = _dot(p_ref[...], wp_ref[...], "nt")
        ya = _dot(a_ref[...], wa_ref[...], "nt")
        g0 = jax.nn.sigmoid(g0_ref[...] + b0_ref[...])
        g1 = jax.nn.sigmoid(g1_ref[...] + b1_ref[...])
        mixed_ref[...] = (g0 * yp + g1 * ya).astype(BF16)
        yp_ref[...] = yp.astype(BF16)
        ya_ref[...] = ya.astype(BF16)

    out = pl.BlockSpec((tm, tn), lambda i, j: (i, j))
    return pl.pallas_call(
        body, grid=(S // tm, nj),
        in_specs=[pl.BlockSpec((tm, POOL_WIDTH), lambda i, j: (i, 0)),
                  pl.BlockSpec((tm, ATTN_OUT_WIDTH), lambda i, j: (i, 0)),
                  pl.BlockSpec((tn, POOL_WIDTH), lambda i, j: (j, 0)),
                  pl.BlockSpec((tn, ATTN_OUT_WIDTH), lambda i, j: (j, 0)),
                  pl.BlockSpec((tm, tn), lambda i, j: (i, j)), pl.BlockSpec((tm, tn), lambda i, j: (i, j + nj)),
                  pl.BlockSpec((1, tn), lambda i, j: (0, j)), pl.BlockSpec((1, tn), lambda i, j: (0, j + nj))],
        out_specs=[out, out, out], out_shape=[jax.ShapeDtypeStruct((S, D), BF16)] * 3,
        compiler_params=_params("parallel", "parallel"), name="mix_fwd")(
            pool_y, attn_y, w_pool_out_t, w_attn_out_t, gates_pre, gates_pre, b_gate, b_gate)


def _mix_bwd(dmixed, gates_pre, b_gate, y_pool, y_attn):
    S = dmixed.shape[0]
    D = D_MODEL
    T = 256

    def body(dm_ref, gp_ref, b_ref, yp_ref, ya_ref, dyp_ref, dya_ref, dg_ref, db_ref):
        i = pl.program_id(0)
        dm = dm_ref[...].astype(F32)
        parts = []
        for br, (y_ref, dy_ref) in enumerate(((yp_ref, dyp_ref), (ya_ref, dya_ref))):
            cs = slice(br * D, (br + 1) * D)
            g = jax.nn.sigmoid(gp_ref[:, cs] + b_ref[:, cs])
            dy_ref[...] = (dm * g).astype(BF16)
            dpre = dm * y_ref[...].astype(F32) * g * (1.0 - g)
            dg_ref[:, cs] = dpre.astype(BF16)
            parts.append(jnp.sum(dpre, axis=0, keepdims=True))

        @pl.when(i == 0)
        def _():
            db_ref[:, 0:D] = parts[0]
            db_ref[:, D:2 * D] = parts[1]

        @pl.when(i > 0)
        def _():
            db_ref[:, 0:D] += parts[0]
            db_ref[:, D:2 * D] += parts[1]

    row = pl.BlockSpec((T, D), lambda i: (i, 0))
    row2 = pl.BlockSpec((T, 2 * D), lambda i: (i, 0))
    vec2 = pl.BlockSpec((1, 2 * D), lambda i: (0, 0))
    return pl.pallas_call(
        body, grid=(S // T,), in_specs=[row, row2, vec2, row, row], out_specs=[row, row, row2, vec2],
        out_shape=[jax.ShapeDtypeStruct((S, D), BF16), jax.ShapeDtypeStruct((S, D), BF16),
                   jax.ShapeDtypeStruct((S, 2 * D), BF16), jax.ShapeDtypeStruct((1, 2 * D), F32)],
        compiler_params=_params("arbitrary"), name="mix_bwd")(dmixed, gates_pre, b_gate, y_pool, y_attn)


_CONV_T, _CONV_C = 512, 512


def _conv_taps(ext, w_ref, b_ref):
    x2 = pltpu.roll(ext, 2, 0)[HALO:]
    x1 = pltpu.roll(ext, 1, 0)[HALO:]
    x0 = ext[HALO:]
    y = b_ref[...] + w_ref[0:1, :] * x2
    y = y + w_ref[1:2, :] * x1
    y = y + w_ref[2:3, :] * x0
    return y, (x2, x1, x0)


def _gelu_parts(a):
    cdf = 0.5 * (1.0 + lax.erf(a * (1.0 / math.sqrt(2.0))))
    return cdf, a * cdf


def _conv_specs(S):
    T, C = _CONV_T, _CONV_C
    nj = D_FF // C
    blk = pl.BlockSpec((2, T, C), lambda i, j: (0, i, j))
    per = T // HALO
    halo = pl.BlockSpec((2, HALO, C), lambda i, j: (0, jnp.maximum(i * per - 1, 0), j))
    w_a = pl.BlockSpec((3, C), lambda i, j: (0, j))
    w_b = pl.BlockSpec((3, C), lambda i, j: (0, j + nj))
    b_a = pl.BlockSpec((1, C), lambda i, j: (0, j))
    b_b = pl.BlockSpec((1, C), lambda i, j: (0, j + nj))
    return T, C, nj, blk, halo, w_a, w_b, b_a, b_b


def _conv_load(ext_ref, halo_ref, up_ref, first):
    for h in range(2):
        ext_ref[h, 0:HALO, :] = jnp.where(first, 0.0, halo_ref[h].astype(F32))
        ext_ref[h, HALO:, :] = up_ref[h].astype(F32)


def _convffn_fwd(up_pre, conv_w, conv_b):
    S = up_pre.shape[1]
    T, C, nj, blk, halo, w_a, w_b, b_a, b_b = _conv_specs(S)

    def body(up_ref, halo_ref, wa_ref, wb_ref, ba_ref, bb_ref, act_ref, ext_ref):
        _conv_load(ext_ref, halo_ref, up_ref, pl.program_id(0) == 0)
        a, _ = _conv_taps(ext_ref[0], wa_ref, ba_ref)
        b, _ = _conv_taps(ext_ref[1], wb_ref, bb_ref)
        _, gelu = _gelu_parts(a)
        act_ref[...] = (gelu * b).astype(BF16)

    return pl.pallas_call(
        body, grid=(S // T, nj), in_specs=[blk, halo, w_a, w_b, b_a, b_b],
        out_specs=pl.BlockSpec((T, C), lambda i, j: (i, j)),
        out_shape=jax.ShapeDtypeStruct((S, D_FF), BF16),
        scratch_shapes=[pltpu.VMEM((2, HALO + T, C), F32)],
        compiler_params=_params("parallel", "parallel"), name="convffn_fwd")(
            up_pre, up_pre, conv_w, conv_w, conv_b, conv_b)


def _convffn_bwd_a(up_pre, dact, conv_w, conv_b):
    S = up_pre.shape[1]
    T, C, nj, blk, halo, w_a, w_b, b_a, b_b = _conv_specs(S)
    def swap(spec):
        return pl.BlockSpec(spec.block_shape, lambda j, i, _m=spec.index_map: _m(i, j))

    def body(up_ref, halo_ref, dact_ref, wa_ref, wb_ref, ba_ref, bb_ref, dup_ref, dwa_ref, dwb_ref, dba_ref, dbb_ref,
             ext_ref):
        i = pl.program_id(1)
        _conv_load(ext_ref, halo_ref, up_ref, i == 0)
        a, xa = _conv_taps(ext_ref[0], wa_ref, ba_ref)
        b, xb = _conv_taps(ext_ref[1], wb_ref, bb_ref)
        cdf, gelu = _gelu_parts(a)
        dgelu = cdf + a * (jnp.exp(-0.5 * a * a) * (1.0 / math.sqrt(2.0 * math.pi)))
        dact = dact_ref[...].astype(F32)
        da = dact * b * dgelu
        db = dact * gelu
        dup_ref[0] = da.astype(BF16)
        dup_ref[1] = db.astype(BF16)
        for dval, xs, dw_ref, dbias_ref in ((da, xa, dwa_ref, dba_ref), (db, xb, dwb_ref, dbb_ref)):
            dw = jnp.concatenate([jnp.sum(dval * xk, axis=0, keepdims=True) for xk in xs], axis=0)
            dbias = jnp.sum(dval, axis=0, keepdims=True)

            @pl.when(i == 0)
            def _():
                dw_ref[...] = dw
                dbias_ref[...] = dbias

            @pl.when(i > 0)
            def _():
                dw_ref[...] += dw
                dbias_ref[...] += dbias

    dact_spec = pl.BlockSpec((T, C), lambda j, i: (i, j))
    dw_spec = pl.BlockSpec((3, C), lambda j, i: (0, j))
    db_spec = pl.BlockSpec((1, C), lambda j, i: (0, j))
    res = pl.pallas_call(
        body, grid=(nj, S // T),
        in_specs=[swap(blk), swap(halo), dact_spec, swap(w_a), swap(w_b), swap(b_a), swap(b_b)],
        out_specs=[swap(blk), dw_spec, dw_spec, db_spec, db_spec],
        out_shape=[jax.ShapeDtypeStruct((2, S, D_FF), BF16), jax.ShapeDtypeStruct((3, D_FF), F32),
                   jax.ShapeDtypeStruct((3, D_FF), F32), jax.ShapeDtypeStruct((1, D_FF), F32),
                   jax.ShapeDtypeStruct((1, D_FF), F32)],
        scratch_shapes=[pltpu.VMEM((2, HALO + T, C), F32)],
        compiler_params=_params("parallel", "arbitrary"), name="convffn_bwd_a")(
            up_pre, up_pre, dact, conv_w, conv_w, conv_b, conv_b)
    dup, dwa, dwb, dba, dbb = res
    dconv_w = jnp.concatenate([dwa, dwb], axis=1)
    dconv_b = jnp.concatenate([dba, dbb], axis=1)
    return dup, dconv_w, dconv_b


def _convffn_bwd_b(dup, conv_w):
    S = dup.shape[1]
    T, C = _CONV_T, _CONV_C
    nj = D_FF // C
    per = T // HALO
    n_blk = S // T

    def body(d_ref, halo_ref, wa_ref, wb_ref, o_ref, ext_ref):
        i = pl.program_id(0)
        n = T + HALO
        for h, w_ref in enumerate((wa_ref, wb_ref)):
            ext_ref[h, 0:T, :] = d_ref[h].astype(F32)
            ext_ref[h, T:, :] = jnp.where(i == n_blk - 1, 0.0, halo_ref[h].astype(F32))
            ext = ext_ref[h]
            y = w_ref[2:3, :] * ext[0:T] + w_ref[1:2, :] * pltpu.roll(ext, n - 1, 0)[0:T]
            y = y + w_ref[0:1, :] * pltpu.roll(ext, n - 2, 0)[0:T]
            o_ref[h] = y.astype(BF16)

    blk = pl.BlockSpec((2, T, C), lambda i, j: (0, i, j))
    halo = pl.BlockSpec((2, HALO, C), lambda i, j: (0, jnp.minimum((i + 1) * per, S // HALO - 1), j))
    return pl.pallas_call(
        body, grid=(n_blk, nj),
        in_specs=[blk, halo, pl.BlockSpec((3, C), lambda i, j: (0, j)), pl.BlockSpec((3, C), lambda i, j: (0, j + nj))],
        out_specs=blk, out_shape=jax.ShapeDtypeStruct((2, S, D_FF), BF16),
        scratch_shapes=[pltpu.VMEM((2, T + HALO, C), F32)],
        compiler_params=_params("parallel", "parallel"), name="convffn_bwd_b")(dup, dup, conv_w, conv_w)


def _position():
    return lax.axis_index("x"), lax.axis_index("y"), lax.axis_index("c")


def _flip(pos, k):
    x, y, c = pos
    return (1 - x if k & 4 else x, 1 - y if k & 2 else y, 1 - c if k & 1 else c)


def _index(pos):
    return 4 * pos[0] + 2 * pos[1] + pos[2]


def _all_gather_weights(shards):
    n = len(shards)

    def body(*refs):
        ins, outs = refs[:n], refs[n:2 * n]
        send_sems, recv_sems, local_sems = refs[2 * n:]
        me = _position()
        x, y, c = me
        sibling = (x, y, 1 - c)
        chips = [(1 - x, y), (x, 1 - y), (1 - x, 1 - y)]

        def copy(a, k, block, to, own=False):
            slot = outs[a].at[_index(block)]
            return pltpu.make_async_remote_copy(
                src_ref=ins[a] if own else slot, dst_ref=slot, send_sem=send_sems.at[7 * a + k],
                recv_sem=recv_sems.at[7 * a + k], device_id=to, device_id_type=MESH)

        mines = [pltpu.make_async_copy(ins[a], outs[a].at[_index(me)], local_sems.at[a]) for a in range(n)]
        started = []
        for a in range(n):
            mines[a].start()
            first = [copy(a, 0, me, sibling, own=True)]
            first += [copy(a, 1 + j, me, (*chip, c), own=True) for j, chip in enumerate(chips)]
            for cp in first:
                cp.start()
            started += first
        for j, chip in enumerate(chips):
            for a in range(n):
                copy(a, 1 + j, (*chip, c), me).wait_recv()
                passed = copy(a, 4 + j, (*chip, c), sibling)
                passed.start()
                started.append(passed)
        for a in range(n):
            copy(a, 0, sibling, me).wait_recv()
            for j, chip in enumerate(chips):
                copy(a, 4 + j, (*chip, 1 - c), me).wait_recv()
        for cp in started:
            cp.wait_send()
        for cp in mines:
            cp.wait()

    hbm = pl.BlockSpec(memory_space=pl.ANY)
    return pl.pallas_call(
        body, out_shape=[jax.ShapeDtypeStruct((N_DEV,) + s.shape, s.dtype) for s in shards],
        in_specs=[hbm] * n, out_specs=[hbm] * n,
        scratch_shapes=[pltpu.SemaphoreType.DMA((7 * n,)), pltpu.SemaphoreType.DMA((7 * n,)),
                        pltpu.SemaphoreType.DMA((n,))],
        name="all_gather_weights")(*shards)


def _exchange_grads(parts):
    n = len(parts)

    def body(*refs):
        ins, outs = refs[:n], refs[n:2 * n]
        send_sems, recv_sems, local_sems = refs[2 * n:]
        me = _position()
        mines = [pltpu.make_async_copy(ins[a].at[_index(me)], outs[a].at[_index(me)], local_sems.at[a])
                 for a in range(n)]
        for cp in mines:
            cp.start()

        def copy(a, k, land):
            peer = _flip(me, k)
            return pltpu.make_async_remote_copy(
                src_ref=ins[a].at[_index(peer)], dst_ref=outs[a].at[_index(land)],
                send_sem=send_sems.at[7 * a + k - 1], recv_sem=recv_sems.at[7 * a + k - 1], device_id=peer,
                device_id_type=MESH)

        copies = [copy(a, k, me) for a in range(n) for k in range(1, N_DEV)]
        for cp in copies:
            cp.start()
        for a in range(n):
            for k in range(1, N_DEV):
                copy(a, k, _flip(me, k)).wait_recv()
        for cp in copies:
            cp.wait_send()
        for cp in mines:
            cp.wait()

    hbm = pl.BlockSpec(memory_space=pl.ANY)
    return pl.pallas_call(
        body, out_shape=[jax.ShapeDtypeStruct(p.shape, p.dtype) for p in parts],
        in_specs=[hbm] * n, out_specs=[hbm] * n,
        scratch_shapes=[pltpu.SemaphoreType.DMA((7 * n,)), pltpu.SemaphoreType.DMA((7 * n,)),
                        pltpu.SemaphoreType.DMA((n,))],
        name="exchange_grads")(*parts)


def _adamw(w, g, m, v):
    m = ADAM_B1 * m + (1.0 - ADAM_B1) * g
    v = ADAM_B2 * v + (1.0 - ADAM_B2) * (g * g)
    m_hat = m / (1.0 - ADAM_B1 ** ADAM_STEP)
    v_hat = v / (1.0 - ADAM_B2 ** ADAM_STEP)
    delta = -ADAM_LR * (m_hat / (jnp.sqrt(v_hat) + ADAM_EPS) + ADAM_WD * w)
    return delta, m, v


def _row_tile(rows, unit, cap=256):
    best = rows
    for t in range(unit, min(rows, cap) + 1, unit):
        if rows % t == 0:
            best = t
    return best


def _sum_partials(name, recv):
    _, R, C = recv.shape
    T = _row_tile(R, 16)

    def body(r_ref, g_out):
        g = r_ref[0].astype(F32)
        for s in range(1, N_DEV):
            g = g + r_ref[s].astype(F32)
        g_out[...] = g

    return pl.pallas_call(
        body, grid=(R // T,), in_specs=[pl.BlockSpec((N_DEV, T, C), lambda i: (0, i, 0))],
        out_specs=pl.BlockSpec((T, C), lambda i: (i, 0)), out_shape=jax.ShapeDtypeStruct((R, C), F32),
        compiler_params=_params("parallel"), name=name)(recv)


def _adamw_update(name, w, g, m, v):
    R, C = w.shape
    T = _row_tile(R, 8)

    def body(w_ref, g_ref, m_ref, v_ref, d_out, m_out, v_out):
        delta, mn, vn = _adamw(w_ref[...], g_ref[...], m_ref[...], v_ref[...])
        d_out[...] = delta
        m_out[...] = mn
        v_out[...] = vn

    blk = pl.BlockSpec((T, C), lambda i: (i, 0))
    return pl.pallas_call(
        body, grid=(R // T,), in_specs=[blk] * 4, out_specs=[blk] * 3,
        out_shape=[jax.ShapeDtypeStruct((R, C), F32)] * 3, compiler_params=_params("parallel"), name=name)(w, g, m, v)


def _gather_vmem(p_ref, gath, send_sems, recv_sems):
    me = _position()
    gath[_index(me)] = p_ref[...]
    copies = []
    for k in range(1, N_DEV):
        peer = _flip(me, k)
        copies.append(pltpu.make_async_remote_copy(
            src_ref=p_ref, dst_ref=gath.at[_index(me)], send_sem=send_sems.at[k - 1],
            recv_sem=recv_sems.at[k - 1], device_id=peer, device_id_type=MESH))
    for cp in copies:
        cp.start()
    for k in range(1, N_DEV):
        peer = _flip(me, k)
        pltpu.make_async_remote_copy(
            src_ref=p_ref, dst_ref=gath.at[_index(peer)], send_sem=send_sems.at[k - 1],
            recv_sem=recv_sems.at[k - 1], device_id=peer, device_id_type=MESH).wait_recv()
    for cp in copies:
        cp.wait_send()


def _all_gather_small(p):
    R, C = p.shape

    def body(p_ref, out_ref, send_sems, recv_sems):
        _gather_vmem(p_ref, out_ref, send_sems, recv_sems)

    vm = pl.BlockSpec(memory_space=pltpu.VMEM)
    return pl.pallas_call(
        body, in_specs=[vm], out_specs=vm, out_shape=jax.ShapeDtypeStruct((N_DEV, R, C), F32),
        scratch_shapes=[pltpu.SemaphoreType.DMA((7,)), pltpu.SemaphoreType.DMA((7,))],
        name="all_gather_small")(p)


def _small_allreduce_adamw(part, w, m, v):
    R, C = part.shape

    def body(p_ref, w_ref, m_ref, v_ref, g_out, d_out, m_out, v_out, gath, send_sems, recv_sems):
        _gather_vmem(p_ref, gath, send_sems, recv_sems)
        g = gath[0]
        for s in range(1, N_DEV):
            g = g + gath[s]
        delta, mn, vn = _adamw(w_ref[...], g, m_ref[...], v_ref[...])
        g_out[...] = g
        d_out[...] = delta
        m_out[...] = mn
        v_out[...] = vn

    vm = pl.BlockSpec(memory_space=pltpu.VMEM)
    return pl.pallas_call(
        body, in_specs=[vm] * 4, out_specs=[vm] * 4, out_shape=[jax.ShapeDtypeStruct((R, C), F32)] * 4,
        scratch_shapes=[pltpu.VMEM((N_DEV, R, C), F32), pltpu.SemaphoreType.DMA((7,)), pltpu.SemaphoreType.DMA((7,))],
        name="small_allreduce_adamw")(part, w, m, v)


_BIG = ("w_in", "w_up", "w_down", "w_out", "w_pool_out", "w_attn_out")
_TRANSPOSED = ("w_in", "w_up", "w_pool_out", "w_attn_out")
_SMALL = (("g_mix", 2048), ("b_gate", 4096), ("pool_scale", 1024), ("g_ffn", 2048), ("conv_b", 11264), ("g_final", 2048))
CONV_ROWS = 8


def _to_comm(name, shard):
    return shard.T if name in _TRANSPOSED else shard


def _pack_small(vals):
    flat = [vals[n].reshape(-1) for n, _ in _SMALL]
    used = sum(f.shape[0] for f in flat)
    flat.append(jnp.zeros((SMALL_ROWS * PACK_COLS - used,), F32))
    return jnp.concatenate(flat).reshape(SMALL_ROWS, PACK_COLS)


def _unpack_small(packed, like):
    flat = packed.reshape(-1)
    out, off = {}, 0
    for name, n in _SMALL:
        out[name] = flat[off:off + n].reshape(like[name].shape)
        off += n
    return out


def _local_step(x, target, W, small):
    S = x.shape[0]
    o_qkv, o_gate = POOL_WIDTH, POOL_WIDTH + QKV_WIDTH
    w_in_t, w_up_t = W["w_in"], W["w_up"]

    h1 = _rms_fwd("rms1_fwd", x, small["g_mix"])
    u = _mm_nt("proj_u", h1, w_in_t, tm=1024, tn=512, tk=2048, o_dtype=F32, n_rows=POOL_WIDTH, b_row_off=0)
    qkv = _mm_nt("proj_qkv", h1, w_in_t, tm=1024, tn=512, tk=2048, o_dtype=BF16, n_rows=QKV_WIDTH, b_row_off=o_qkv)
    gates_pre = _mm_nt("proj_gates", h1, w_in_t, tm=1024, tn=512, tk=2048, o_dtype=F32, n_rows=2 * D_MODEL,
                       b_row_off=o_gate)
    pool_y = _pool_fwd(u, W["w_pool_lin"], small["pool_scale"])
    outs, lses = zip(*[_attn_fwd(gi, qkv) for gi in range(3)])
    attn_y = _merge_fwd(outs, lses)
    mixed, y_pool, y_attn = _mix_fwd(pool_y, attn_y, W["w_pool_out"], W["w_attn_out"], gates_pre, small["b_gate"])
    x1 = _mm_nn("out_proj", mixed, W["w_out"], tm=1024, tn=512, tk=2048, o_dtype=F32, res=x)

    h2 = _rms_fwd("rms2_fwd", x1, small["g_ffn"])
    nj = D_FF // 512
    up_pre = _mm("up_proj", h2, w_up_t, mode="nt", grid=(S // 1024, 2 * nj, 1),
                 a_spec=pl.BlockSpec((1024, D_MODEL), lambda i, j, k: (i, 0)),
                 b_spec=pl.BlockSpec((512, D_MODEL), lambda i, j, k: (j, 0)),
                 o_spec=pl.BlockSpec((None, 1024, 512), lambda i, j, k: (j // nj, i, j % nj)),
                 o_shape=(2, S, D_FF), o_dtype=BF16, acc_shape=(1024, 512))
    act = _convffn_fwd(up_pre, small["conv_w_full"], small["conv_b"])
    x2 = _mm_nn("down_proj", act, W["w_down"], tm=1024, tn=512, tk=D_FF // 2, o_dtype=F32, res=x1)

    dx2, dx2_b, dg_final, sq_cols = _final_loss("final_loss", x2, small["g_final"], target)

    dact = _mm_nt("d_act", dx2_b, W["w_down"], tm=1024, tn=512, tk=2048, o_dtype=BF16)
    dw_down = _mm_tn("dw_down", act, dx2_b, tm=512, tn=2048, tk=512)
    dup, dconv_w, dconv_b = _convffn_bwd_a(up_pre, dact, small["conv_w_full"], small["conv_b"])
    dup_pre = _convffn_bwd_b(dup, small["conv_w_full"])
    tkh = D_FF // 2
    dh2 = _mm("d_h2", dup_pre, w_up_t, mode="nn", grid=(S // 1024, D_MODEL // 1024, 4),
              a_spec=pl.BlockSpec((None, 1024, tkh), lambda i, j, k: (k // 2, i, k % 2)),
              b_spec=pl.BlockSpec((tkh, 1024), lambda i, j, k: (k, j)),
              o_spec=pl.BlockSpec((1024, 1024), lambda i, j, k: (i, j)),
              o_shape=(S, D_MODEL), o_dtype=F32, acc_shape=(1024, 1024))
    dw_up = _mm("dw_up", dup_pre, h2, mode="tn", grid=(2 * nj, 1, S // 512),
                a_spec=pl.BlockSpec((None, 512, 512), lambda i, j, k: (i // nj, k, i % nj)),
                b_spec=pl.BlockSpec((512, D_MODEL), lambda i, j, k: (k, 0)),
                o_spec=pl.BlockSpec((512, D_MODEL), lambda i, j, k: (i, 0)),
                o_shape=(2 * D_FF, D_MODEL), o_dtype=BF16, acc_shape=(512, D_MODEL))
    dx1, dx1_b, dg_ffn = _rms_bwd("rms2_bwd", x1, small["g_ffn"], dh2, dx2)

    dmixed = _mm_nt("d_mixed", dx1_b, W["w_out"], tm=1024, tn=512, tk=2048, o_dtype=BF16)
    dw_out = _mm_tn("dw_out", mixed, dx1_b, tm=1024, tn=2048, tk=512)
    dy_pool, dy_attn, dgates, db_gate = _mix_bwd(dmixed, gates_pre, small["b_gate"], y_pool, y_attn)
    dpool_y = _mm_nn("d_pool_y", dy_pool, W["w_pool_out"], tm=1024, tn=512, tk=2048, o_dtype=F32)
    dw_pool_out = _mm_tn("dw_pool_out", dy_pool, pool_y, tm=1024, tn=1024, tk=512)
    dattn_y = _mm_nn("d_attn_y", dy_attn, W["w_attn_out"], tm=1024, tn=512, tk=2048, o_dtype=F32)
    dw_attn_out = _mm_tn("dw_attn_out", dy_attn, attn_y, tm=1024, tn=512, tk=512)
    e, dw_pool_lin, dpool_scale = _pool_bwd_a(u, dpool_y, W["w_pool_lin"], small["pool_scale"])
    du = _pool_bwd_b(e)
    dos, dps = _merge_bwd(dattn_y, outs, lses)
    dqkv = [_attn_bwd(gi, qkv, dos[gi], lses[gi], dps[gi]) for gi in range(3)]
    dproj = jnp.concatenate([du] + [dqkv[gi][sec] for sec in range(3) for gi in range(3)] + [dgates], axis=1)
    dh1 = _mm_nn("d_h1", dproj, w_in_t, tm=1024, tn=1024, tk=IN_WIDTH // 4, o_dtype=F32)
    dw_in = _mm_tn("dw_in", dproj, h1, tm=512, tn=2048, tk=512)
    grad_x, _, dg_mix = _rms_bwd("rms1_bwd", x, small["g_mix"], dh1, dx1)

    grads = dict(w_in=dw_in, w_up=dw_up, w_down=dw_down, w_out=dw_out, w_pool_out=dw_pool_out,
                 w_attn_out=dw_attn_out, w_pool_lin=dw_pool_lin, conv_w=dconv_w)
    small_grads = dict(g_mix=dg_mix, b_gate=db_gate, pool_scale=dpool_scale, g_ffn=dg_ffn, conv_b=dconv_b,
                       g_final=dg_final)
    return sq_cols, grad_x, grads, small_grads


def kernel(x, g_mix, w_in, b_gate, w_pool_lin, pool_scale, w_pool_out, w_attn_out, w_out, g_ffn, w_up, conv_w, conv_b, w_down, g_final, loss_target, m_g_mix, m_w_in, m_b_gate, m_w_pool_lin, m_pool_scale, m_w_pool_out, m_w_attn_out, m_w_out, m_g_ffn, m_w_up, m_conv_w, m_conv_b, m_w_down, m_g_final, v_g_mix, v_w_in, v_b_gate, v_w_pool_lin, v_pool_scale, v_w_pool_out, v_w_attn_out, v_w_out, v_g_ffn, v_w_up, v_conv_w, v_conv_b, v_w_down, v_g_final):
    given = dict(g_mix=g_mix, w_in=w_in, b_gate=b_gate, w_pool_lin=w_pool_lin, pool_scale=pool_scale,
                 w_pool_out=w_pool_out, w_attn_out=w_attn_out, w_out=w_out, g_ffn=g_ffn, w_up=w_up, conv_w=conv_w,
                 conv_b=conv_b, w_down=w_down, g_final=g_final)
    mom_m = dict(g_mix=m_g_mix, w_in=m_w_in, b_gate=m_b_gate, w_pool_lin=m_w_pool_lin, pool_scale=m_pool_scale,
                 w_pool_out=m_w_pool_out, w_attn_out=m_w_attn_out, w_out=m_w_out, g_ffn=m_g_ffn, w_up=m_w_up,
                 conv_w=m_conv_w, conv_b=m_conv_b, w_down=m_w_down, g_final=m_g_final)
    mom_v = dict(g_mix=v_g_mix, w_in=v_w_in, b_gate=v_b_gate, w_pool_lin=v_w_pool_lin, pool_scale=v_pool_scale,
                 w_pool_out=v_w_pool_out, w_attn_out=v_w_attn_out, w_out=v_w_out, g_ffn=v_g_ffn, w_up=v_w_up,
                 conv_w=v_conv_w, conv_b=v_conv_b, w_down=v_w_down, g_final=v_g_final)
    gathered = _all_gather_weights([_to_comm(n, given[n][0]).astype(BF16) for n in _BIG])
    W = {n: g.reshape(N_DEV * g.shape[1], g.shape[2]) for n, g in zip(_BIG, gathered)}
    lin_all = _all_gather_small(w_pool_lin[0].reshape(4 * 32, POOL_GROUP_WIDTH))
    W["w_pool_lin"] = jnp.moveaxis(lin_all.reshape(N_DEV, 4, 32, POOL_GROUP_WIDTH), 0, 1).reshape(
        4, POOL_GROUP_WIDTH, POOL_GROUP_WIDTH).astype(BF16)
    conv_w_all = _all_gather_small(jnp.pad(conv_w[0], ((0, CONV_ROWS - 3), (0, 0))))
    conv_w_full = jnp.moveaxis(conv_w_all[:, :3, :], 0, 1).reshape(3, 2 * D_FF)
    small = dict(g_mix=g_mix, b_gate=b_gate, pool_scale=pool_scale, g_ffn=g_ffn, conv_b=conv_b,
                 g_final=g_final.reshape(1, D_MODEL), conv_w_full=conv_w_full)

    sq_cols, grad_x, grads, small_grads = _local_step(x[0], loss_target[0], W, small)
    loss = lax.psum(0.5 / D_MODEL * jnp.sum(sq_cols), AXES)

    parts = [grads[n].reshape(N_DEV, grads[n].shape[0] // N_DEV, grads[n].shape[1]) for n in _BIG]
    parts.append(jnp.moveaxis(grads["w_pool_lin"].reshape(4, N_DEV, 32, POOL_GROUP_WIDTH), 1, 0).reshape(
        N_DEV, 4 * 32, POOL_GROUP_WIDTH).astype(BF16))
    parts.append(jnp.pad(jnp.moveaxis(grads["conv_w"].reshape(3, N_DEV, 2 * D_FF // N_DEV), 1, 0),
                         ((0, 0), (0, CONV_ROWS - 3), (0, 0))).astype(BF16))
    names = _BIG + ("w_pool_lin", "conv_w")
    recv = _exchange_grads(parts)
    res = {}
    for n, r in zip(names, recv):
        g = _sum_partials("sum_" + n, r)
        if n in _TRANSPOSED:
            g = g.T
        elif n == "conv_w":
            g = g[:3]
        shape = given[n].shape
        rows = math.prod(shape[:-1])
        w2, m2, v2 = (d[n].reshape(rows, shape[-1]) for d in (given, mom_m, mom_v))
        upd = _adamw_update("adamw_" + n, w2, g.reshape(rows, shape[-1]), m2, v2)
        res[n] = [a.reshape(shape) for a in (g,) + tuple(upd)]

    packed_small = [_pack_small(d) for d in (given, mom_m, mom_v)]
    res_small = [_unpack_small(p, given) for p in _small_allreduce_adamw(_pack_small(small_grads), *packed_small)]

    order = ["g_mix", "w_in", "b_gate", "w_pool_lin", "pool_scale", "w_pool_out", "w_attn_out", "w_out", "g_ffn",
             "w_up", "conv_w", "conv_b", "w_down", "g_final"]
    outs = [loss, grad_x[None]]
    for kind in range(4):
        for n in order:
            outs.append(res[n][kind] if n in res else res_small[kind][n])
    return tuple(outs)
```

```python
import math

import jax
import jax.numpy as jnp
from jax import lax
from jax.experimental import pallas as pl
from jax.experimental.pallas import tpu as pltpu

F32 = jnp.float32
BF16 = jnp.bfloat16
MESH = pl.DeviceIdType.MESH
AXES = ("x", "y", "c")
N_DEV = 8

D_MODEL = 2048
POOL_WINDOWS = (2, 4, 8, 16)
POOL_GROUP_WIDTH = 256
POOL_WIDTH = 1024
ATTN_GROUPS = ((128, 1), (512, 4), (2048, 16))
SPAN = 128
HEADS_PER_GROUP = 4
N_ATTN_HEADS = 12
HEAD_DIM = 128
ATTN_WIDTH = 1536
QKV_WIDTH = 3 * ATTN_WIDTH
ATTN_OUT_WIDTH = 512
IN_WIDTH = 9728
D_FF = 5632
RMS_EPS = 1e-6
ADAM_LR, ADAM_B1, ADAM_B2, ADAM_EPS, ADAM_WD, ADAM_STEP = 0.001, 0.9, 0.999, 1e-08, 0.01, 10

VMEM_LIMIT_BYTES = 48 * 1024 * 1024
HALO = 16
PACK_COLS = 1024
SMALL_ROWS = 24
NEG_BIG = -1e30

_DN = {"nn": (((1,), (0,)), ((), ())), "nt": (((1,), (1,)), ((), ())), "tn": (((0,), (0,)), ((), ()))}


def _params(*sem):
    return pltpu.CompilerParams(dimension_semantics=sem, vmem_limit_bytes=VMEM_LIMIT_BYTES)


def _dot(a, b, mode):
    return lax.dot_general(a.astype(BF16), b.astype(BF16), _DN[mode], preferred_element_type=F32)


def _mm(name, a, b, *, mode, grid, a_spec, b_spec, o_spec, o_shape, o_dtype, acc_shape, res=None, res_spec=None):
    nk = grid[2]

    def body(*refs):
        if res is not None:
            a_ref, b_ref, r_ref, o_ref = refs[:4]
            scr = refs[4:]
        else:
            a_ref, b_ref, o_ref = refs[:3]
            r_ref = None
            scr = refs[3:]
        prod = _dot(a_ref[...], b_ref[...], mode)

        def finish(val):
            if r_ref is not None:
                val = val + r_ref[...]
            o_ref[...] = val.astype(o_ref.dtype)

        if nk == 1:
            finish(prod)
        else:
            acc = scr[0]
            k = pl.program_id(2)

            @pl.when(k == 0)
            def _():
                acc[...] = prod

            @pl.when(k > 0)
            def _():
                acc[...] += prod

            @pl.when(k == nk - 1)
            def _():
                finish(acc[...])

    in_specs = [a_spec, b_spec]
    args = [a, b]
    if res is not None:
        in_specs.append(res_spec)
        args.append(res)
    return pl.pallas_call(
        body, grid=grid, in_specs=in_specs, out_specs=o_spec,
        out_shape=jax.ShapeDtypeStruct(o_shape, o_dtype),
        scratch_shapes=[] if nk == 1 else [pltpu.VMEM(acc_shape, F32)],
        compiler_params=_params("parallel", "parallel", "arbitrary"), name=name)(*args)


def _mm_nn(name, a, b, *, tm, tn, tk, o_dtype, n_cols=None, b_col_off=0, res=None):
    M, K = a.shape
    N = b.shape[1] if n_cols is None else n_cols
    off = b_col_off // tn
    return _mm(name, a, b, mode="nn", grid=(M // tm, N // tn, K // tk),
               a_spec=pl.BlockSpec((tm, tk), lambda i, j, k: (i, k)),
               b_spec=pl.BlockSpec((tk, tn), lambda i, j, k: (k, j + off)),
               o_spec=pl.BlockSpec((tm, tn), lambda i, j, k: (i, j)),
               o_shape=(M, N), o_dtype=o_dtype, acc_shape=(tm, tn), res=res,
               res_spec=pl.BlockSpec((tm, tn), lambda i, j, k: (i, j)))


def _mm_nt(name, a, b, *, tm, tn, tk, o_dtype, n_rows=None, b_row_off=0):
    M, K = a.shape
    N = b.shape[0] if n_rows is None else n_rows
    off = b_row_off // tn
    return _mm(name, a, b, mode="nt", grid=(M // tm, N // tn, K // tk),
               a_spec=pl.BlockSpec((tm, tk), lambda i, j, k: (i, k)),
               b_spec=pl.BlockSpec((tn, tk), lambda i, j, k: (j + off, k)),
               o_spec=pl.BlockSpec((tm, tn), lambda i, j, k: (i, j)),
               o_shape=(M, N), o_dtype=o_dtype, acc_shape=(tm, tn))


def _mm_tn(name, a, b, *, tm, tn, tk):
    K, M = a.shape
    N = b.shape[1]
    return _mm(name, a, b, mode="tn", grid=(M // tm, N // tn, K // tk),
               a_spec=pl.BlockSpec((tk, tm), lambda i, j, k: (k, i)),
               b_spec=pl.BlockSpec((tk, tn), lambda i, j, k: (k, j)),
               o_spec=pl.BlockSpec((tm, tn), lambda i, j, k: (i, j)),
               o_shape=(M, N), o_dtype=BF16, acc_shape=(tm, tn))


def _rms_fwd(name, x, g):
    S, D = x.shape
    T = 512

    def body(x_ref, g_ref, h_ref):
        xv = x_ref[...]
        r = lax.rsqrt(jnp.mean(xv * xv, axis=-1, keepdims=True) + RMS_EPS)
        h_ref[...] = (xv * r * g_ref[...]).astype(BF16)

    return pl.pallas_call(
        body, grid=(S // T,),
        in_specs=[pl.BlockSpec((T, D), lambda i: (i, 0)), pl.BlockSpec((1, D), lambda i: (0, 0))],
        out_specs=pl.BlockSpec((T, D), lambda i: (i, 0)),
        out_shape=jax.ShapeDtypeStruct((S, D), BF16), compiler_params=_params("parallel"), name=name)(x, g)


def _rms_bwd(name, x, g, dh, dres):
    S, D = x.shape
    T = 256

    def body(x_ref, g_ref, dh_ref, dres_ref, dx_ref, dxb_ref, dg_ref):
        i = pl.program_id(0)
        xv = x_ref[...]
        r = lax.rsqrt(jnp.mean(xv * xv, axis=-1, keepdims=True) + RMS_EPS)
        xhat = xv * r
        dhv = dh_ref[...].astype(F32)
        gdh = dhv * g_ref[...]
        c = jnp.mean(xhat * gdh, axis=-1, keepdims=True)
        dx = dres_ref[...] + r * (gdh - xhat * c)
        dx_ref[...] = dx
        dxb_ref[...] = dx.astype(BF16)
        part = jnp.sum(dhv * xhat, axis=0, keepdims=True)

        @pl.when(i == 0)
        def _():
            dg_ref[...] = part

        @pl.when(i > 0)
        def _():
            dg_ref[...] += part

    row = pl.BlockSpec((T, D), lambda i: (i, 0))
    vec = pl.BlockSpec((1, D), lambda i: (0, 0))
    return pl.pallas_call(
        body, grid=(S // T,), in_specs=[row, vec, row, row], out_specs=[row, row, vec],
        out_shape=[jax.ShapeDtypeStruct((S, D), F32), jax.ShapeDtypeStruct((S, D), BF16),
                   jax.ShapeDtypeStruct((1, D), F32)],
        compiler_params=_params("arbitrary"), name=name)(x, g, dh, dres)


def _final_loss(name, x, g, target):
    S, D = x.shape
    T = 256

    def body(x_ref, g_ref, t_ref, dx_ref, dxb_ref, dg_ref, sq_ref):
        i = pl.program_id(0)
        xv = x_ref[...]
        r = lax.rsqrt(jnp.mean(xv * xv, axis=-1, keepdims=True) + RMS_EPS)
        xhat = xv * r
        err = xhat * g_ref[...] - t_ref[...]
        dy = err * (1.0 / D)
        gdy = dy * g_ref[...]
        c = jnp.mean(xhat * gdy, axis=-1, keepdims=True)
        dx = r * (gdy - xhat * c)
        dx_ref[...] = dx
        dxb_ref[...] = dx.astype(BF16)
        dg_part = jnp.sum(dy * xhat, axis=0, keepdims=True)
        sq_part = jnp.sum(err * err, axis=0, keepdims=True)

        @pl.when(i == 0)
        def _():
            dg_ref[...] = dg_part
            sq_ref[...] = sq_part

        @pl.when(i > 0)
        def _():
            dg_ref[...] += dg_part
            sq_ref[...] += sq_part

    row = pl.BlockSpec((T, D), lambda i: (i, 0))
    vec = pl.BlockSpec((1, D), lambda i: (0, 0))
    return pl.pallas_call(
        body, grid=(S // T,), in_specs=[row, vec, row], out_specs=[row, row, vec, vec],
        out_shape=[jax.ShapeDtypeStruct((S, D), F32), jax.ShapeDtypeStruct((S, D), BF16),
                   jax.ShapeDtypeStruct((1, D), F32), jax.ShapeDtypeStruct((1, D), F32)],
        compiler_params=_params("arbitrary"), name=name)(x, g, target)


def _prev_halo_spec(T, C, col_map):
    per = T // HALO
    return pl.BlockSpec((HALO, C), lambda *g: (jnp.maximum(g[0] * per - 1, 0), col_map(*g)))


def _pool_window_sums(ext, T, i):
    t = i * T + lax.broadcasted_iota(jnp.int32, (T, 1), 0)
    outs = []
    cnts = []
    for gi, w in enumerate(POOL_WINDOWS):
        s = ext[:, gi * POOL_GROUP_WIDTH:(gi + 1) * POOL_GROUP_WIDTH]
        sh = 1
        while sh < w:
            s = s + pltpu.roll(s, sh, 0)
            sh *= 2
        cnt = jnp.minimum(t + 1, w).astype(F32)
        outs.append(s[HALO:] / cnt - ext[HALO:, gi * POOL_GROUP_WIDTH:(gi + 1) * POOL_GROUP_WIDTH])
        cnts.append(cnt)
    return outs, cnts


def _load_ext(ext_ref, halo_ref, blk_ref, first):
    ext_ref[0:HALO, :] = jnp.where(first, 0.0, halo_ref[...].astype(F32))
    ext_ref[HALO:, :] = blk_ref[...].astype(F32)


def _pool_fwd(u, w_lin, scale):
    S = u.shape[0]
    T = 512
    W = POOL_WIDTH

    def body(u_ref, halo_ref, wl_ref, sc_ref, y_ref, ext_ref):
        i = pl.program_id(0)
        _load_ext(ext_ref, halo_ref, u_ref, i == 0)
        pooled, _ = _pool_window_sums(ext_ref[...], T, i)
        for gi in range(4):
            z = _dot(pooled[gi], wl_ref[gi], "nn")
            cs = slice(gi * POOL_GROUP_WIDTH, (gi + 1) * POOL_GROUP_WIDTH)
            y_ref[:, cs] = (z * sc_ref[:, cs]).astype(BF16)

    return pl.pallas_call(
        body, grid=(S // T,),
        in_specs=[pl.BlockSpec((T, W), lambda i: (i, 0)), _prev_halo_spec(T, W, lambda i: 0),
                  pl.BlockSpec((4, 256, 256), lambda i: (0, 0, 0)), pl.BlockSpec((1, W), lambda i: (0, 0))],
        out_specs=pl.BlockSpec((T, W), lambda i: (i, 0)),
        out_shape=jax.ShapeDtypeStruct((S, W), BF16),
        scratch_shapes=[pltpu.VMEM((HALO + T, W), F32)],
        compiler_params=_params("parallel"), name="pool_fwd")(u, u, w_lin, scale)


def _pool_bwd_a(u, dy, w_lin, scale):
    S = u.shape[0]
    T = 512
    W = POOL_WIDTH

    def body(u_ref, halo_ref, dy_ref, wl_ref, sc_ref, e_ref, dwl_ref, dsc_ref, ext_ref):
        i = pl.program_id(0)
        _load_ext(ext_ref, halo_ref, u_ref, i == 0)
        pooled, cnts = _pool_window_sums(ext_ref[...], T, i)
        for gi in range(4):
            cs = slice(gi * POOL_GROUP_WIDTH, (gi + 1) * POOL_GROUP_WIDTH)
            z = _dot(pooled[gi], wl_ref[gi], "nn")
            dyg = dy_ref[:, cs]
            dsc = jnp.sum(dyg * z, axis=0, keepdims=True)
            dz = dyg * sc_ref[:, cs]
            dwl = _dot(pooled[gi], dz, "tn")
            dpooled = _dot(dz, wl_ref[gi], "nt")
            e_ref[:, cs] = dpooled / cnts[gi]

            @pl.when(i == 0)
            def _():
                dwl_ref[gi] = dwl
                dsc_ref[:, cs] = dsc

            @pl.when(i > 0)
            def _():
                dwl_ref[gi] += dwl
                dsc_ref[:, cs] += dsc

    return pl.pallas_call(
        body, grid=(S // T,),
        in_specs=[pl.BlockSpec((T, W), lambda i: (i, 0)), _prev_halo_spec(T, W, lambda i: 0),
                  pl.BlockSpec((T, W), lambda i: (i, 0)),
                  pl.BlockSpec((4, 256, 256), lambda i: (0, 0, 0)), pl.BlockSpec((1, W), lambda i: (0, 0))],
        out_specs=[pl.BlockSpec((T, W), lambda i: (i, 0)), pl.BlockSpec((4, 256, 256), lambda i: (0, 0, 0)),
                   pl.BlockSpec((1, W), lambda i: (0, 0))],
        out_shape=[jax.ShapeDtypeStruct((S, W), F32), jax.ShapeDtypeStruct((4, 256, 256), F32),
                   jax.ShapeDtypeStruct((1, W), F32)],
        scratch_shapes=[pltpu.VMEM((HALO + T, W), F32)],
        compiler_params=_params("arbitrary"), name="pool_bwd_a")(u, u, dy, w_lin, scale)


def _pool_bwd_b(e):
    S = e.shape[0]
    T = 512
    W = POOL_WIDTH
    per = T // HALO
    n_blk = S // T

    def body(e_ref, halo_ref, du_ref, ext_ref):
        i = pl.program_id(0)
        ext_ref[0:T, :] = e_ref[...]
        ext_ref[T:, :] = jnp.where(i == n_blk - 1, 0.0, halo_ref[...])
        ext = ext_ref[...]
        n = T + HALO
        t = i * T + lax.broadcasted_iota(jnp.int32, (T, 1), 0)
        for gi, w in enumerate(POOL_WINDOWS):
            cs = slice(gi * POOL_GROUP_WIDTH, (gi + 1) * POOL_GROUP_WIDTH)
            s = ext[:, cs]
            sh = 1
            while sh < w:
                s = s + pltpu.roll(s, n - sh, 0)
                sh *= 2
            cnt = jnp.minimum(t + 1, w).astype(F32)
            du_ref[:, cs] = (s[0:T] - ext[0:T, cs] * cnt).astype(BF16)

    return pl.pallas_call(
        body, grid=(n_blk,),
        in_specs=[pl.BlockSpec((T, W), lambda i: (i, 0)),
                  pl.BlockSpec((HALO, W), lambda i: (jnp.minimum((i + 1) * per, S // HALO - 1), 0))],
        out_specs=pl.BlockSpec((T, W), lambda i: (i, 0)),
        out_shape=jax.ShapeDtypeStruct((S, W), BF16),
        scratch_shapes=[pltpu.VMEM((T + HALO, W), F32)],
        compiler_params=_params("parallel"), name="pool_bwd_b")(e, e)


def _slope(head):
    return 2.0 ** (-8.0 * (head + 1) / N_ATTN_HEADS)


def _attn_masks(b):
    row = lax.broadcasted_iota(jnp.int32, (SPAN, SPAN), 0)
    col = lax.broadcasted_iota(jnp.int32, (SPAN, SPAN), 1)
    dist_cur = (row - col).astype(F32)
    valid_cur = col <= row
    valid_prev = jnp.logical_and(col >= row, b > 0)
    return dist_cur, valid_cur, valid_prev


def _attn_fwd(gi, qkv):
    S = qkv.shape[0]
    d = ATTN_GROUPS[gi][1]
    L = S // d
    nb = L // SPAN
    view = qkv.reshape(L, d * QKV_WIDTH)
    W = ATTN_OUT_WIDTH
    ncb = QKV_WIDTH // W
    scale = HEAD_DIM ** -0.5

    def body(q_ref, kc_ref, kp_ref, vc_ref, vp_ref, o_ref, lse_ref):
        b = pl.program_id(1)
        dist_cur, valid_cur, valid_prev = _attn_masks(b)
        for hi in range(HEADS_PER_GROUP):
            hs = slice(hi * HEAD_DIM, (hi + 1) * HEAD_DIM)
            sl = _slope(gi * HEADS_PER_GROUP + hi) * d
            q = q_ref[:, hs]
            s_cur = _dot(q, kc_ref[:, hs], "nt") * scale + (-sl) * dist_cur
            s_prev = _dot(q, kp_ref[:, hs], "nt") * scale + (-sl) * (dist_cur + SPAN)
            s_cur = jnp.where(valid_cur, s_cur, NEG_BIG)
            s_prev = jnp.where(valid_prev, s_prev, NEG_BIG)
            m = jnp.maximum(jnp.max(s_cur, axis=-1, keepdims=True), jnp.max(s_prev, axis=-1, keepdims=True))
            p_cur = jnp.exp(s_cur - m)
            p_prev = jnp.exp(s_prev - m)
            l = jnp.sum(p_cur, axis=-1, keepdims=True) + jnp.sum(p_prev, axis=-1, keepdims=True)
            o = (_dot(p_cur, vc_ref[:, hs], "nn") + _dot(p_prev, vp_ref[:, hs], "nn")) / l
            o_ref[:, hs] = o
            lse_ref[:, hs] = jnp.broadcast_to(m + jnp.log(l), (SPAN, HEAD_DIM))

    def cur(sec):
        return pl.BlockSpec((SPAN, W), lambda r, b: (b, r * ncb + 3 * sec + gi))

    def prev(sec):
        return pl.BlockSpec((SPAN, W), lambda r, b: (jnp.maximum(b - 1, 0), r * ncb + 3 * sec + gi))

    out = pl.BlockSpec((SPAN, W), lambda r, b: (b, r))
    o, lse = pl.pallas_call(
        body, grid=(d, nb), in_specs=[cur(0), cur(1), prev(1), cur(2), prev(2)], out_specs=[out, out],
        out_shape=[jax.ShapeDtypeStruct((L, d * W), F32), jax.ShapeDtypeStruct((L, d * W), F32)],
        compiler_params=_params("parallel", "parallel"), name=f"attn_fwd_g{gi}")(view, view, view, view, view)
    return o.reshape(S, W), lse.reshape(S, W)


def _attn_bwd(gi, qkv, do, lse, dp):
    S = qkv.shape[0]
    d = ATTN_GROUPS[gi][1]
    L = S // d
    nb = L // SPAN
    W = ATTN_OUT_WIDTH
    ncb = QKV_WIDTH // W
    scale = HEAD_DIM ** -0.5
    view = qkv.reshape(L, d * QKV_WIDTH)
    do_v, lse_v, dp_v = (a.reshape(L, d * W) for a in (do, lse, dp))

    def body(q_ref, kc_ref, vc_ref, kp_ref, vp_ref, qn_ref, do_ref, don_ref, lse_ref, lsen_ref, dp_ref, dpn_ref,
             dq_ref, dk_ref, dv_ref):
        b = pl.program_id(1)
        dist_cur, valid_cur, valid_prev = _attn_masks(b)
        row = lax.broadcasted_iota(jnp.int32, (SPAN, SPAN), 0)
        col = lax.broadcasted_iota(jnp.int32, (SPAN, SPAN), 1)
        valid_next = jnp.logical_and(col >= row, b < nb - 1)
        for hi in range(HEADS_PER_GROUP):
            hs = slice(hi * HEAD_DIM, (hi + 1) * HEAD_DIM)
            sl = _slope(gi * HEADS_PER_GROUP + hi) * d
            q, kc, vc, kp, vp, qn = (r[:, hs] for r in (q_ref, kc_ref, vc_ref, kp_ref, vp_ref, qn_ref))
            dov, don = do_ref[:, hs], don_ref[:, hs]
            lse_b, lse_n, dp_b, dp_n = lse_ref[:, hs], lsen_ref[:, hs], dp_ref[:, hs], dpn_ref[:, hs]
            bias_cur = (-sl) * dist_cur
            bias_far = (-sl) * (dist_cur + SPAN)
            p_cur = jnp.where(valid_cur, jnp.exp(_dot(q, kc, "nt") * scale + bias_cur - lse_b), 0.0)
            p_prev = jnp.where(valid_prev, jnp.exp(_dot(q, kp, "nt") * scale + bias_far - lse_b), 0.0)
            ds_cur = p_cur * (_dot(dov, vc, "nt") - dp_b)
            ds_prev = p_prev * (_dot(dov, vp, "nt") - dp_b)
            dq_ref[:, hs] = ((_dot(ds_cur, kc, "nn") + _dot(ds_prev, kp, "nn")) * scale).astype(BF16)
            p_next = jnp.where(valid_next, jnp.exp(_dot(qn, kc, "nt") * scale + bias_far - lse_n), 0.0)
            ds_next = p_next * (_dot(don, vc, "nt") - dp_n)
            dv_ref[:, hs] = (_dot(p_cur, dov, "tn") + _dot(p_next, don, "tn")).astype(BF16)
            dk_ref[:, hs] = ((_dot(ds_cur, q, "tn") + _dot(ds_next, qn, "tn")) * scale).astype(BF16)

    def qkv_spec(sec, shift):
        def imap(r, b):
            return (jnp.clip(b + shift, 0, nb - 1), r * ncb + 3 * sec + gi)
        return pl.BlockSpec((SPAN, W), imap)

    def act_spec(shift):
        return pl.BlockSpec((SPAN, W), lambda r, b: (jnp.clip(b + shift, 0, nb - 1), r))

    out = act_spec(0)
    dq, dk, dv = pl.pallas_call(
        body, grid=(d, nb),
        in_specs=[qkv_spec(0, 0), qkv_spec(1, 0), qkv_spec(2, 0), qkv_spec(1, -1), qkv_spec(2, -1), qkv_spec(0, 1),
                  act_spec(0), act_spec(1), act_spec(0), act_spec(1), act_spec(0), act_spec(1)],
        out_specs=[out, out, out],
        out_shape=[jax.ShapeDtypeStruct((L, d * W), BF16)] * 3,
        compiler_params=_params("parallel", "parallel"), name=f"attn_bwd_g{gi}")(
            view, view, view, view, view, view, do_v, do_v, lse_v, lse_v, dp_v, dp_v)
    return dq.reshape(S, W), dk.reshape(S, W), dv.reshape(S, W)


def _group_weights(l_refs):
    l0, l1, l2 = (r[...] for r in l_refs)
    m = jnp.maximum(jnp.maximum(l0, l1), l2)
    e = [jnp.exp(l0 - m), jnp.exp(l1 - m), jnp.exp(l2 - m)]
    tot = e[0] + e[1] + e[2]
    return [ei / tot for ei in e]


def _merge_fwd(outs, lses):
    S, W = outs[0].shape
    T = 512

    def body(o0, o1, o2, l0, l1, l2, y_ref):
        w = _group_weights((l0, l1, l2))
        y_ref[...] = (w[0] * o0[...] + w[1] * o1[...] + w[2] * o2[...]).astype(BF16)

    blk = pl.BlockSpec((T, W), lambda i: (i, 0))
    return pl.pallas_call(
        body, grid=(S // T,), in_specs=[blk] * 6, out_specs=blk, out_shape=jax.ShapeDtypeStruct((S, W), BF16),
        compiler_params=_params("parallel"), name="merge_fwd")(*outs, *lses)


def _merge_bwd(dy, outs, lses):
    S, W = outs[0].shape
    T = 512

    def body(dy_ref, o0, o1, o2, l0, l1, l2, do0, do1, do2, dp0, dp1, dp2):
        w = _group_weights((l0, l1, l2))
        dyv = dy_ref[...]
        y = w[0] * o0[...] + w[1] * o1[...] + w[2] * o2[...]
        prod = dyv * y
        dots = []
        for hi in range(HEADS_PER_GROUP):
            hs = slice(hi * HEAD_DIM, (hi + 1) * HEAD_DIM)
            dots.append(jnp.broadcast_to(jnp.sum(prod[:, hs], axis=-1, keepdims=True), (T, HEAD_DIM)))
        dot_b = jnp.concatenate(dots, axis=1)
        for wg, do_ref, dp_ref in zip(w, (do0, do1, do2), (dp0, dp1, dp2)):
            do_ref[...] = (wg * dyv).astype(BF16)
            dp_ref[...] = wg * dot_b

    blk = pl.BlockSpec((T, W), lambda i: (i, 0))
    res = pl.pallas_call(
        body, grid=(S // T,), in_specs=[blk] * 7, out_specs=[blk] * 6,
        out_shape=[jax.ShapeDtypeStruct((S, W), BF16)] * 3 + [jax.ShapeDtypeStruct((S, W), F32)] * 3,
        compiler_params=_params("parallel"), name="merge_bwd")(dy, *outs, *lses)
    return res[:3], res[3:]


def _mix_fwd(pool_y, attn_y, w_pool_out_t, w_attn_out_t, gates_pre, b_gate):
    S = pool_y.shape[0]
    D = D_MODEL
    tm, tn = 512, 512
    nj = D // tn

    def body(p_ref, a_ref, wp_ref, wa_ref, g0_ref, g1_ref, b0_ref, b1_ref, mixed_ref, yp_ref, ya_ref):
        yp = _dot(p_ref[...], wp_ref[...], "nt")
        ya = _dot(a_ref[...], wa_ref[...], "nt")
        g0 = jax.nn.sigmoid(g0_ref[...] + b0_ref[...])
        g1 = jax.nn.sigmoid(g1_ref[...] + b1_ref[...])
        mixed_ref[...] = (g0 * yp + g1 * ya).astype(BF16)
        yp_ref[...] = yp.astype(BF16)
        ya_ref[...] = ya.astype(BF16)

    out = pl.BlockSpec((tm, tn), lambda i, j: (i, j))
    return pl.pallas_call(
        body, grid=(S // tm, nj),
        in_specs=[pl.BlockSpec((tm, POOL_WIDTH), lambda i, j: (i, 0)),
                  pl.BlockSpec((tm, ATTN_OUT_WIDTH), lambda i, j: (i, 0)),
                  pl.BlockSpec((tn, POOL_WIDTH), lambda i, j: (j, 0)),
                  pl.BlockSpec((tn, ATTN_OUT_WIDTH), lambda i, j: (j, 0)),
                  pl.BlockSpec((tm, tn), lambda i, j: (i, j)), pl.BlockSpec((tm, tn), lambda i, j: (i, j + nj)),
                  pl.BlockSpec((1, tn), lambda i, j: (0, j)), pl.BlockSpec((1, tn), lambda i, j: (0, j + nj))],
        out_specs=[out, out, out], out_shape=[jax.ShapeDtypeStruct((S, D), BF16)] * 3,
        compiler_params=_params("parallel", "parallel"), name="mix_fwd")(
            pool_y, attn_y, w_pool_out_t, w_attn_out_t, gates_pre, gates_pre, b_gate, b_gate)


def _mix_bwd(dmixed, gates_pre, b_gate, y_pool, y_attn):
    S = dmixed.shape[0]
    D = D_MODEL
    T = 256

    def body(dm_ref, gp_ref, b_ref, yp_ref, ya_ref, dyp_ref, dya_ref, dg_ref, db_ref):
        i = pl.program_id(0)
        dm = dm_ref[...].astype(F32)
        parts = []
        for br, (y_ref, dy_ref) in enumerate(((yp_ref, dyp_ref), (ya_ref, dya_ref))):
            cs = slice(br * D, (br + 1) * D)
            g = jax.nn.sigmoid(gp_ref[:, cs] + b_ref[:, cs])
            dy_ref[...] = (dm * g).astype(BF16)
            dpre = dm * y_ref[...].astype(F32) * g * (1.0 - g)
            dg_ref[:, cs] = dpre.astype(BF16)
            parts.append(jnp.sum(dpre, axis=0, keepdims=True))

        @pl.when(i == 0)
        def _():
            db_ref[:, 0:D] = parts[0]
            db_ref[:, D:2 * D] = parts[1]

        @pl.when(i > 0)
        def _():
            db_ref[:, 0:D] += parts[0]
            db_ref[:, D:2 * D] += parts[1]

    row = pl.BlockSpec((T, D), lambda i: (i, 0))
    row2 = pl.BlockSpec((T, 2 * D), lambda i: (i, 0))
    vec2 = pl.BlockSpec((1, 2 * D), lambda i: (0, 0))
    return pl.pallas_call(
        body, grid=(S // T,), in_specs=[row, row2, vec2, row, row], out_specs=[row, row, row2, vec2],
        out_shape=[jax.ShapeDtypeStruct((S, D), BF16), jax.ShapeDtypeStruct((S, D), BF16),
                   jax.ShapeDtypeStruct((S, 2 * D), BF16), jax.ShapeDtypeStruct((1, 2 * D), F32)],
        compiler_params=_params("arbitrary"), name="mix_bwd")(dmixed, gates_pre, b_gate, y_pool, y_attn)


_CONV_T, _CONV_C = 512, 512


def _conv_taps(ext, w_ref, b_ref):
    x2 = pltpu.roll(ext, 2, 0)[HALO:]
    x1 = pltpu.roll(ext, 1, 0)[HALO:]
    x0 = ext[HALO:]
    y = b_ref[...] + w_ref[0:1, :] * x2
    y = y + w_ref[1:2, :] * x1
    y = y + w_ref[2:3, :] * x0
    return y, (x2, x1, x0)


def _gelu_parts(a):
    cdf = 0.5 * (1.0 + lax.erf(a * (1.0 / math.sqrt(2.0))))
    return cdf, a * cdf


def _conv_specs(S):
    T, C = _CONV_T, _CONV_C
    nj = D_FF // C
    blk = pl.BlockSpec((2, T, C), lambda i, j: (0, i, j))
    per = T // HALO
    halo = pl.BlockSpec((2, HALO, C), lambda i, j: (0, jnp.maximum(i * per - 1, 0), j))
    w_a = pl.BlockSpec((3, C), lambda i, j: (0, j))
    w_b = pl.BlockSpec((3, C), lambda i, j: (0, j + nj))
    b_a = pl.BlockSpec((1, C), lambda i, j: (0, j))
    b_b = pl.BlockSpec((1, C), lambda i, j: (0, j + nj))
    return T, C, nj, blk, halo, w_a, w_b, b_a, b_b


def _conv_load(ext_ref, halo_ref, up_ref, first):
    for h in range(2):
        ext_ref[h, 0:HALO, :] = jnp.where(first, 0.0, halo_ref[h].astype(F32))
        ext_ref[h, HALO:, :] = up_ref[h].astype(F32)


def _convffn_fwd(up_pre, conv_w, conv_b):
    S = up_pre.shape[1]
    T, C, nj, blk, halo, w_a, w_b, b_a, b_b = _conv_specs(S)

    def body(up_ref, halo_ref, wa_ref, wb_ref, ba_ref, bb_ref, act_ref, ext_ref):
        _conv_load(ext_ref, halo_ref, up_ref, pl.program_id(0) == 0)
        a, _ = _conv_taps(ext_ref[0], wa_ref, ba_ref)
        b, _ = _conv_taps(ext_ref[1], wb_ref, bb_ref)
        _, gelu = _gelu_parts(a)
        act_ref[...] = (gelu * b).astype(BF16)

    return pl.pallas_call(
        body, grid=(S // T, nj), in_specs=[blk, halo, w_a, w_b, b_a, b_b],
        out_specs=pl.BlockSpec((T, C), lambda i, j: (i, j)),
        out_shape=jax.ShapeDtypeStruct((S, D_FF), BF16),
        scratch_shapes=[pltpu.VMEM((2, HALO + T, C), F32)],
        compiler_params=_params("parallel", "parallel"), name="convffn_fwd")(
            up_pre, up_pre, conv_w, conv_w, conv_b, conv_b)


def _convffn_bwd_a(up_pre, dact, conv_w, conv_b):
    S = up_pre.shape[1]
    T, C, nj, blk, halo, w_a, w_b, b_a, b_b = _conv_specs(S)
    def swap(spec):
        return pl.BlockSpec(spec.block_shape, lambda j, i, _m=spec.index_map: _m(i, j))

    def body(up_ref, halo_ref, dact_ref, wa_ref, wb_ref, ba_ref, bb_ref, dup_ref, dwa_ref, dwb_ref, dba_ref, dbb_ref,
             ext_ref):
        i = pl.program_id(1)
        _conv_load(ext_ref, halo_ref, up_ref, i == 0)
        a, xa = _conv_taps(ext_ref[0], wa_ref, ba_ref)
        b, xb = _conv_taps(ext_ref[1], wb_ref, bb_ref)
        cdf, gelu = _gelu_parts(a)
        dgelu = cdf + a * (jnp.exp(-0.5 * a * a) * (1.0 / math.sqrt(2.0 * math.pi)))
        dact = dact_ref[...].astype(F32)
        da = dact * b * dgelu
        db = dact * gelu
        dup_ref[0] = da.astype(BF16)
        dup_ref[1] = db.astype(BF16)
        for dval, xs, dw_ref, dbias_ref in ((da, xa, dwa_ref, dba_ref), (db, xb, dwb_ref, dbb_ref)):
            dw = jnp.concatenate([jnp.sum(dval * xk, axis=0, keepdims=True) for xk in xs], axis=0)
            dbias = jnp.sum(dval, axis=0, keepdims=True)

            @pl.when(i == 0)
            def _():
                dw_ref[...] = dw
                dbias_ref[...] = dbias

            @pl.when(i > 0)
            def _():
                dw_ref[...] += dw
                dbias_ref[...] += dbias

    dact_spec = pl.BlockSpec((T, C), lambda j, i: (i, j))
    dw_spec = pl.BlockSpec((3, C), lambda j, i: (0, j))
    db_spec = pl.BlockSpec((1, C), lambda j, i: (0, j))
    res = pl.pallas_call(
        body, grid=(nj, S // T),
        in_specs=[swap(blk), swap(halo), dact_spec, swap(w_a), swap(w_b), swap(b_a), swap(b_b)],
        out_specs=[swap(blk), dw_spec, dw_spec, db_spec, db_spec],
        out_shape=[jax.ShapeDtypeStruct((2, S, D_FF), BF16), jax.ShapeDtypeStruct((3, D_FF), F32),
                   jax.ShapeDtypeStruct((3, D_FF), F32), jax.ShapeDtypeStruct((1, D_FF), F32),
                   jax.ShapeDtypeStruct((1, D_FF), F32)],
        scratch_shapes=[pltpu.VMEM((2, HALO + T, C), F32)],
        compiler_params=_params("parallel", "arbitrary"), name="convffn_bwd_a")(
            up_pre, up_pre, dact, conv_w, conv_w, conv_b, conv_b)
    dup, dwa, dwb, dba, dbb = res
    dconv_w = jnp.concatenate([dwa, dwb], axis=1)
    dconv_b = jnp.concatenate([dba, dbb], axis=1)
    return dup, dconv_w, dconv_b


def _convffn_bwd_b(dup, conv_w):
    S = dup.shape[1]
    T, C = _CONV_T, _CONV_C
    nj = D_FF // C
    per = T // HALO
    n_blk = S // T

    def body(d_ref, halo_ref, wa_ref, wb_ref, o_ref, ext_ref):
        i = pl.program_id(0)
        n = T + HALO
        for h, w_ref in enumerate((wa_ref, wb_ref)):
            ext_ref[h, 0:T, :] = d_ref[h].astype(F32)
            ext_ref[h, T:, :] = jnp.where(i == n_blk - 1, 0.0, halo_ref[h].astype(F32))
            ext = ext_ref[h]
            y = w_ref[2:3, :] * ext[0:T] + w_ref[1:2, :] * pltpu.roll(ext, n - 1, 0)[0:T]
            y = y + w_ref[0:1, :] * pltpu.roll(ext, n - 2, 0)[0:T]
            o_ref[h] = y.astype(BF16)

    blk = pl.BlockSpec((2, T, C), lambda i, j: (0, i, j))
    halo = pl.BlockSpec((2, HALO, C), lambda i, j: (0, jnp.minimum((i + 1) * per, S // HALO - 1), j))
    return pl.pallas_call(
        body, grid=(n_blk, nj),
        in_specs=[blk, halo, pl.BlockSpec((3, C), lambda i, j: (0, j)), pl.BlockSpec((3, C), lambda i, j: (0, j + nj))],
        out_specs=blk, out_shape=jax.ShapeDtypeStruct((2, S, D_FF), BF16),
        scratch_shapes=[pltpu.VMEM((2, T + HALO, C), F32)],
        compiler_params=_params("parallel", "parallel"), name="convffn_bwd_b")(dup, dup, conv_w, conv_w)


def _position():
    return lax.axis_index("x"), lax.axis_index("y"), lax.axis_index("c")


def _flip(pos, k):
    x, y, c = pos
    return (1 - x if k & 4 else x, 1 - y if k & 2 else y, 1 - c if k & 1 else c)


def _index(pos):
    return 4 * pos[0] + 2 * pos[1] + pos[2]


def _all_gather_weights(shards):
    n = len(shards)

    def body(*refs):
        ins, outs = refs[:n], refs[n:2 * n]
        send_sems, recv_sems, local_sems = refs[2 * n:]
        me = _position()
        x, y, c = me
        sibling = (x, y, 1 - c)
        chips = [(1 - x, y), (x, 1 - y), (1 - x, 1 - y)]

        def copy(a, k, block, to, own=False):
            slot = outs[a].at[_index(block)]
            return pltpu.make_async_remote_copy(
                src_ref=ins[a] if own else slot, dst_ref=slot, send_sem=send_sems.at[7 * a + k],
                recv_sem=recv_sems.at[7 * a + k], device_id=to, device_id_type=MESH)

        mines = [pltpu.make_async_copy(ins[a], outs[a].at[_index(me)], local_sems.at[a]) for a in range(n)]
        started = []
        for a in range(n):
            mines[a].start()
            first = [copy(a, 0, me, sibling, own=True)]
            first += [copy(a, 1 + j, me, (*chip, c), own=True) for j, chip in enumerate(chips)]
            for cp in first:
                cp.start()
            started += first
        for j, chip in enumerate(chips):
            for a in range(n):
                copy(a, 1 + j, (*chip, c), me).wait_recv()
                passed = copy(a, 4 + j, (*chip, c), sibling)
                passed.start()
                started.append(passed)
        for a in range(n):
            copy(a, 0, sibling, me).wait_recv()
            for j, chip in enumerate(chips):
                copy(a, 4 + j, (*chip, 1 - c), me).wait_recv()
        for cp in started:
            cp.wait_send()
        for cp in mines:
            cp.wait()

    hbm = pl.BlockSpec(memory_space=pl.ANY)
    return pl.pallas_call(
        body, out_shape=[jax.ShapeDtypeStruct((N_DEV,) + s.shape, s.dtype) for s in shards],
        in_specs=[hbm] * n, out_specs=[hbm] * n,
        scratch_shapes=[pltpu.SemaphoreType.DMA((7 * n,)), pltpu.SemaphoreType.DMA((7 * n,)),
                        pltpu.SemaphoreType.DMA((n,))],
        name="all_gather_weights")(*shards)


_HBM = pl.BlockSpec(memory_space=pltpu.HBM)
_SEM = pl.BlockSpec(memory_space=pltpu.SEMAPHORE)
_ANY = pl.BlockSpec(memory_space=pl.ANY)
_EFFECT = pltpu.SideEffectType.DATAFLOW_SIDE_EFFECTING


def _in_hbm(a):
    return pltpu.with_memory_space_constraint(a, pltpu.HBM)


def _split_copies(name, srcs, lands, n_copies, plan, start_after, wait_after):
    ns, nl = len(srcs), len(lands)

    def start_body(*refs):
        src_refs, land_refs = refs[:ns], refs[ns:ns + nl]
        send_sems, recv_sems = refs[ns + nl + 1], refs[ns + nl + 2]
        token = refs[-1]
        for k, (src, dst, _, peer) in enumerate(plan(src_refs, land_refs)):
            pltpu.make_async_remote_copy(src_ref=src, dst_ref=dst, send_sem=send_sems.at[k], recv_sem=recv_sems.at[k],
                                         device_id=peer, device_id_type=MESH).start()
        token[...] = jnp.zeros_like(token)

    thru = [pltpu.HBM(a.shape, a.dtype) for a in list(srcs) + list(lands)]
    res = pl.pallas_call(
        start_body, name=name + "_start",
        out_shape=[pltpu.SemaphoreType.DMA((n_copies,)), pltpu.SemaphoreType.DMA((n_copies,))] + thru
        + [jax.ShapeDtypeStruct((8, 128), F32)],
        in_specs=[_HBM] * (ns + nl) + [_ANY], out_specs=[_SEM, _SEM] + [_HBM] * (ns + nl) + [pl.BlockSpec(memory_space=pltpu.VMEM)],
        input_output_aliases={i: 2 + i for i in range(ns + nl)},
        compiler_params=pltpu.CompilerParams(has_side_effects=_EFFECT),
    )(*[_in_hbm(a) for a in list(srcs) + list(lands)], start_after)
    send_sems, recv_sems, token = res[0], res[1], res[-1]
    thru_vals = res[2:-1]

    def wait():
        def wait_body(*refs):
            src_refs, land_refs = refs[:ns], refs[ns:ns + nl]
            s_sems, r_sems = refs[ns + nl], refs[ns + nl + 1]
            for k, (src, _, dst, peer) in enumerate(plan(src_refs, land_refs)):
                cp = pltpu.make_async_remote_copy(src_ref=src, dst_ref=dst, send_sem=s_sems.at[k], recv_sem=r_sems.at[k],
                                                  device_id=peer, device_id_type=MESH)
                cp.wait_send()
                cp.wait_recv()

        out = pl.pallas_call(
            wait_body, name=name + "_wait", out_shape=thru,
            in_specs=[_HBM] * (ns + nl) + [_SEM, _SEM, _ANY], out_specs=[_HBM] * (ns + nl),
            input_output_aliases={i: i for i in range(ns + nl)},
            compiler_params=pltpu.CompilerParams(has_side_effects=_EFFECT),
        )(*thru_vals, send_sems, recv_sems, wait_after())
        return list(out[:ns]), list(out[ns:])

    return wait, token


def _gather_rest_plan(n):
    def plan(src_refs, land_refs):
        me = _position()
        x, y, c = me
        peers = [(x, y, 1 - c), (1 - x, y, c), (x, 1 - y, c), (1 - x, 1 - y, c)]
        return [(src_refs[a], land_refs[a].at[_index(me)], land_refs[a].at[_index(p)], p)
                for a in range(n) for p in peers]
    return plan


def _gather_forward(shards, lands):
    n = len(shards)

    def body(*refs):
        ins, lin = refs[:n], refs[n:2 * n]
        send_sems, recv_sems, local_sems = refs[3 * n:]
        me = _position()
        x, y, c = me
        sibling = (x, y, 1 - c)
        chips = [(1 - x, y), (x, 1 - y), (1 - x, 1 - y)]
        mines = [pltpu.make_async_copy(ins[a], lin[a].at[_index(me)], local_sems.at[a]) for a in range(n)]
        for cp in mines:
            cp.start()

        def copy(a, j, core):
            slot = lin[a].at[_index((*chips[j], core))]
            return pltpu.make_async_remote_copy(
                src_ref=slot, dst_ref=slot, send_sem=send_sems.at[3 * a + j], recv_sem=recv_sems.at[3 * a + j],
                device_id=sibling, device_id_type=MESH)

        sent = [copy(a, j, c) for a in range(n) for j in range(3)]
        for cp in sent:
            cp.start()
        for a in range(n):
            for j in range(3):
                copy(a, j, 1 - c).wait_recv()
        for cp in sent:
            cp.wait_send()
        for cp in mines:
            cp.wait()

    return pl.pallas_call(
        body, out_shape=[jax.ShapeDtypeStruct(l.shape, l.dtype) for l in lands],
        in_specs=[_ANY] * (2 * n), out_specs=[_ANY] * n, input_output_aliases={n + a: a for a in range(n)},
        scratch_shapes=[pltpu.SemaphoreType.DMA((3 * n,)), pltpu.SemaphoreType.DMA((3 * n,)),
                        pltpu.SemaphoreType.DMA((n,))],
        name="gather_forward")(*shards, *lands)


def _chip_index(pos):
    return 2 * pos[0] + pos[1]


def _exchange_d2d(name, parts):
    n = len(parts)

    def body(*refs):
        ins, outs = refs[:n], refs[n:2 * n]
        send_sems, recv_sems = refs[2 * n:]
        x, y, c = _position()
        sibling = (x, y, 1 - c)

        def copy(a, q, core):
            return pltpu.make_async_remote_copy(
                src_ref=ins[a].at[2 * q + core], dst_ref=outs[a].at[q], send_sem=send_sems.at[4 * a + q],
                recv_sem=recv_sems.at[4 * a + q], device_id=sibling, device_id_type=MESH)

        sent = [copy(a, q, 1 - c) for a in range(n) for q in range(4)]
        for cp in sent:
            cp.start()
        for cp in sent:
            cp.wait_recv()
        for cp in sent:
            cp.wait_send()

    return pl.pallas_call(
        body, out_shape=[jax.ShapeDtypeStruct((4,) + p.shape[1:], p.dtype) for p in parts],
        in_specs=[_ANY] * n, out_specs=[_ANY] * n,
        scratch_shapes=[pltpu.SemaphoreType.DMA((4 * n,)), pltpu.SemaphoreType.DMA((4 * n,))],
        name=name)(*parts)


def _pair_sum(name, part, got):
    _, R, C = part.shape
    T = _row_tile(R, 16)

    def body(p_ref, g_ref, s_out, l_out):
        c = lax.axis_index("c")
        mine = jnp.where(c == 0, p_ref[0, 0].astype(F32), p_ref[0, 1].astype(F32))
        s = (mine + g_ref[0].astype(F32)).astype(BF16)
        s_out[0] = s
        l_out[0] = s

    out = pl.BlockSpec((1, T, C), lambda q, i: (q, i, 0))
    return pl.pallas_call(
        body, grid=(4, R // T),
        in_specs=[pl.BlockSpec((1, 2, T, C), lambda q, i: (q, 0, i, 0)), out], out_specs=[out, out],
        out_shape=[jax.ShapeDtypeStruct((4, R, C), BF16)] * 2,
        compiler_params=_params("parallel", "parallel"), name=name)(part.reshape(4, 2, R, C), got)


def _exchange_ici_plan(n):
    def plan(src_refs, land_refs):
        me = _position()
        x, y, c = me
        peers = [(1 - x, y, c), (x, 1 - y, c), (1 - x, 1 - y, c)]
        return [(src_refs[a].at[_chip_index(p)], land_refs[a].at[_chip_index(me)], land_refs[a].at[_chip_index(p)], p)
                for a in range(n) for p in peers]
    return plan


def _adamw(w, g, m, v):
    m = ADAM_B1 * m + (1.0 - ADAM_B1) * g
    v = ADAM_B2 * v + (1.0 - ADAM_B2) * (g * g)
    m_hat = m / (1.0 - ADAM_B1 ** ADAM_STEP)
    v_hat = v / (1.0 - ADAM_B2 ** ADAM_STEP)
    delta = -ADAM_LR * (m_hat / (jnp.sqrt(v_hat) + ADAM_EPS) + ADAM_WD * w)
    return delta, m, v


def _row_tile(rows, unit, cap=256):
    best = rows
    for t in range(unit, min(rows, cap) + 1, unit):
        if rows % t == 0:
            best = t
    return best


def _sum_partials(name, recv):
    n_src, R, C = recv.shape
    T = _row_tile(R, 16)

    def body(r_ref, g_out):
        g = r_ref[0].astype(F32)
        for s in range(1, n_src):
            g = g + r_ref[s].astype(F32)
        g_out[...] = g

    return pl.pallas_call(
        body, grid=(R // T,), in_specs=[pl.BlockSpec((n_src, T, C), lambda i: (0, i, 0))],
        out_specs=pl.BlockSpec((T, C), lambda i: (i, 0)), out_shape=jax.ShapeDtypeStruct((R, C), F32),
        compiler_params=_params("parallel"), name=name)(recv)


def _adamw_update(name, w, g, m, v):
    R, C = w.shape
    T = _row_tile(R, 8)

    def body(w_ref, g_ref, m_ref, v_ref, d_out, m_out, v_out):
        delta, mn, vn = _adamw(w_ref[...], g_ref[...], m_ref[...], v_ref[...])
        d_out[...] = delta
        m_out[...] = mn
        v_out[...] = vn

    blk = pl.BlockSpec((T, C), lambda i: (i, 0))
    return pl.pallas_call(
        body, grid=(R // T,), in_specs=[blk] * 4, out_specs=[blk] * 3,
        out_shape=[jax.ShapeDtypeStruct((R, C), F32)] * 3, compiler_params=_params("parallel"), name=name)(w, g, m, v)


def _gather_vmem(p_ref, gath, send_sems, recv_sems):
    me = _position()
    gath[_index(me)] = p_ref[...]
    copies = []
    for k in range(1, N_DEV):
        peer = _flip(me, k)
        copies.append(pltpu.make_async_remote_copy(
            src_ref=p_ref, dst_ref=gath.at[_index(me)], send_sem=send_sems.at[k - 1],
            recv_sem=recv_sems.at[k - 1], device_id=peer, device_id_type=MESH))
    for cp in copies:
        cp.start()
    for k in range(1, N_DEV):
        peer = _flip(me, k)
        pltpu.make_async_remote_copy(
            src_ref=p_ref, dst_ref=gath.at[_index(peer)], send_sem=send_sems.at[k - 1],
            recv_sem=recv_sems.at[k - 1], device_id=peer, device_id_type=MESH).wait_recv()
    for cp in copies:
        cp.wait_send()


def _all_gather_small(p):
    R, C = p.shape

    def body(p_ref, out_ref, send_sems, recv_sems):
        _gather_vmem(p_ref, out_ref, send_sems, recv_sems)

    vm = pl.BlockSpec(memory_space=pltpu.VMEM)
    return pl.pallas_call(
        body, in_specs=[vm], out_specs=vm, out_shape=jax.ShapeDtypeStruct((N_DEV, R, C), F32),
        scratch_shapes=[pltpu.SemaphoreType.DMA((7,)), pltpu.SemaphoreType.DMA((7,))],
        name="all_gather_small")(p)


def _small_allreduce_adamw(part, w, m, v):
    R, C = part.shape

    def body(p_ref, w_ref, m_ref, v_ref, g_out, d_out, m_out, v_out, gath, send_sems, recv_sems):
        _gather_vmem(p_ref, gath, send_sems, recv_sems)
        g = gath[0]
        for s in range(1, N_DEV):
            g = g + gath[s]
        delta, mn, vn = _adamw(w_ref[...], g, m_ref[...], v_ref[...])
        g_out[...] = g
        d_out[...] = delta
        m_out[...] = mn
        v_out[...] = vn

    vm = pl.BlockSpec(memory_space=pltpu.VMEM)
    return pl.pallas_call(
        body, in_specs=[vm] * 4, out_specs=[vm] * 4, out_shape=[jax.ShapeDtypeStruct((R, C), F32)] * 4,
        scratch_shapes=[pltpu.VMEM((N_DEV, R, C), F32), pltpu.SemaphoreType.DMA((7,)), pltpu.SemaphoreType.DMA((7,))],
        name="small_allreduce_adamw")(part, w, m, v)


_BIG = ("w_in", "w_up", "w_down", "w_out", "w_pool_out", "w_attn_out")
_TRANSPOSED = ("w_in", "w_up", "w_pool_out", "w_attn_out")
_SMALL = (("g_mix", 2048), ("b_gate", 4096), ("pool_scale", 1024), ("g_ffn", 2048), ("conv_b", 11264), ("g_final", 2048))
CONV_ROWS = 8


def _to_comm(name, shard):
    return shard.T if name in _TRANSPOSED else shard


def _pack_small(vals):
    flat = [vals[n].reshape(-1) for n, _ in _SMALL]
    used = sum(f.shape[0] for f in flat)
    flat.append(jnp.zeros((SMALL_ROWS * PACK_COLS - used,), F32))
    return jnp.concatenate(flat).reshape(SMALL_ROWS, PACK_COLS)


def _unpack_small(packed, like):
    flat = packed.reshape(-1)
    out, off = {}, 0
    for name, n in _SMALL:
        out[name] = flat[off:off + n].reshape(like[name].shape)
        off += n
    return out


def _local_step(x, target, W, small, first_token=0.0, late_weights=None, emit=None):
    S = x.shape[0]
    o_qkv, o_gate = POOL_WIDTH, POOL_WIDTH + QKV_WIDTH
    if emit is None:
        emit = lambda grads: 0.0
    w_in_t = W["w_in"]

    h1 = _rms_fwd("rms1_fwd", x, small["g_mix"] + first_token)
    u = _mm_nt("proj_u", h1, w_in_t, tm=1024, tn=512, tk=2048, o_dtype=F32, n_rows=POOL_WIDTH, b_row_off=0)
    qkv = _mm_nt("proj_qkv", h1, w_in_t, tm=1024, tn=512, tk=2048, o_dtype=BF16, n_rows=QKV_WIDTH, b_row_off=o_qkv)
    gates_pre = _mm_nt("proj_gates", h1, w_in_t, tm=1024, tn=512, tk=2048, o_dtype=F32, n_rows=2 * D_MODEL,
                       b_row_off=o_gate)
    pool_y = _pool_fwd(u, W["w_pool_lin"], small["pool_scale"])
    outs, lses = zip(*[_attn_fwd(gi, qkv) for gi in range(3)])
    attn_y = _merge_fwd(outs, lses)
    if late_weights is not None:
        W = dict(W, **late_weights(attn_y))
    w_up_t = W["w_up"]
    mixed, y_pool, y_attn = _mix_fwd(pool_y, attn_y, W["w_pool_out"], W["w_attn_out"], gates_pre, small["b_gate"])
    x1 = _mm_nn("out_proj", mixed, W["w_out"], tm=1024, tn=512, tk=2048, o_dtype=F32, res=x)

    h2 = _rms_fwd("rms2_fwd", x1, small["g_ffn"])
    nj = D_FF // 512
    up_pre = _mm("up_proj", h2, w_up_t, mode="nt", grid=(S // 1024, 2 * nj, 1),
                 a_spec=pl.BlockSpec((1024, D_MODEL), lambda i, j, k: (i, 0)),
                 b_spec=pl.BlockSpec((512, D_MODEL), lambda i, j, k: (j, 0)),
                 o_spec=pl.BlockSpec((None, 1024, 512), lambda i, j, k: (j // nj, i, j % nj)),
                 o_shape=(2, S, D_FF), o_dtype=BF16, acc_shape=(1024, 512))
    act = _convffn_fwd(up_pre, small["conv_w_full"], small["conv_b"])
    x2 = _mm_nn("down_proj", act, W["w_down"], tm=1024, tn=512, tk=D_FF // 2, o_dtype=F32, res=x1)

    dx2, dx2_b, dg_final, sq_cols = _final_loss("final_loss", x2, small["g_final"], target)

    dact = _mm_nt("d_act", dx2_b, W["w_down"], tm=1024, tn=512, tk=2048, o_dtype=BF16)
    dw_down = _mm_tn("dw_down", act, dx2_b, tm=512, tn=2048, tk=512)
    tok = emit(dict(w_down=dw_down))
    dup, dconv_w, dconv_b = _convffn_bwd_a(up_pre, dact, small["conv_w_full"], small["conv_b"] + tok)
    dup_pre = _convffn_bwd_b(dup, small["conv_w_full"])
    tkh = D_FF // 2
    dh2 = _mm("d_h2", dup_pre, w_up_t, mode="nn", grid=(S // 1024, D_MODEL // 1024, 4),
              a_spec=pl.BlockSpec((None, 1024, tkh), lambda i, j, k: (k // 2, i, k % 2)),
              b_spec=pl.BlockSpec((tkh, 1024), lambda i, j, k: (k, j)),
              o_spec=pl.BlockSpec((1024, 1024), lambda i, j, k: (i, j)),
              o_shape=(S, D_MODEL), o_dtype=F32, acc_shape=(1024, 1024))
    dw_up = _mm("dw_up", dup_pre, h2, mode="tn", grid=(2 * nj, 1, S // 512),
                a_spec=pl.BlockSpec((None, 512, 512), lambda i, j, k: (i // nj, k, i % nj)),
                b_spec=pl.BlockSpec((512, D_MODEL), lambda i, j, k: (k, 0)),
                o_spec=pl.BlockSpec((512, D_MODEL), lambda i, j, k: (i, 0)),
                o_shape=(2 * D_FF, D_MODEL), o_dtype=BF16, acc_shape=(512, D_MODEL))
    tok = emit(dict(w_up=dw_up, conv_w=dconv_w))
    dx1, dx1_b, dg_ffn = _rms_bwd("rms2_bwd", x1, small["g_ffn"] + tok, dh2, dx2)

    dmixed = _mm_nt("d_mixed", dx1_b, W["w_out"], tm=1024, tn=512, tk=2048, o_dtype=BF16)
    dw_out = _mm_tn("dw_out", mixed, dx1_b, tm=1024, tn=2048, tk=512)
    tok = emit(dict(w_out=dw_out))
    dy_pool, dy_attn, dgates, db_gate = _mix_bwd(dmixed, gates_pre, small["b_gate"] + tok, y_pool, y_attn)
    dpool_y = _mm_nn("d_pool_y", dy_pool, W["w_pool_out"], tm=1024, tn=512, tk=2048, o_dtype=F32)
    dw_pool_out = _mm_tn("dw_pool_out", dy_pool, pool_y, tm=1024, tn=1024, tk=512)
    dattn_y = _mm_nn("d_attn_y", dy_attn, W["w_attn_out"], tm=1024, tn=512, tk=2048, o_dtype=F32)
    dw_attn_out = _mm_tn("dw_attn_out", dy_attn, attn_y, tm=1024, tn=512, tk=512)
    e, dw_pool_lin, dpool_scale = _pool_bwd_a(u, dpool_y, W["w_pool_lin"], small["pool_scale"])
    du = _pool_bwd_b(e)
    dos, dps = _merge_bwd(dattn_y, outs, lses)
    dqkv = [_attn_bwd(gi, qkv, dos[gi], lses[gi], dps[gi]) for gi in range(3)]
    dproj = jnp.concatenate([du] + [dqkv[gi][sec] for sec in range(3) for gi in range(3)] + [dgates], axis=1)
    dw_in = _mm_tn("dw_in", dproj, h1, tm=512, tn=2048, tk=512)
    tok = emit(dict(w_in=dw_in, w_pool_out=dw_pool_out, w_attn_out=dw_attn_out, w_pool_lin=dw_pool_lin))
    dh1 = _mm_nn("d_h1", dproj, w_in_t, tm=1024, tn=1024, tk=IN_WIDTH // 4, o_dtype=F32)
    grad_x, _, dg_mix = _rms_bwd("rms1_bwd", x, small["g_mix"] + tok, dh1, dx1)

    grads = dict(w_in=dw_in, w_up=dw_up, w_down=dw_down, w_out=dw_out, w_pool_out=dw_pool_out,
                 w_attn_out=dw_attn_out, w_pool_lin=dw_pool_lin, conv_w=dconv_w)
    small_grads = dict(g_mix=dg_mix, b_gate=db_gate, pool_scale=dpool_scale, g_ffn=dg_ffn, conv_b=dconv_b,
                       g_final=dg_final)
    return sq_cols, grad_x, grads, small_grads


def kernel(x, g_mix, w_in, b_gate, w_pool_lin, pool_scale, w_pool_out, w_attn_out, w_out, g_ffn, w_up, conv_w, conv_b, w_down, g_final, loss_target, m_g_mix, m_w_in, m_b_gate, m_w_pool_lin, m_pool_scale, m_w_pool_out, m_w_attn_out, m_w_out, m_g_ffn, m_w_up, m_conv_w, m_conv_b, m_w_down, m_g_final, v_g_mix, v_w_in, v_b_gate, v_w_pool_lin, v_pool_scale, v_w_pool_out, v_w_attn_out, v_w_out, v_g_ffn, v_w_up, v_conv_w, v_conv_b, v_w_down, v_g_final):
    given = dict(g_mix=g_mix, w_in=w_in, b_gate=b_gate, w_pool_lin=w_pool_lin, pool_scale=pool_scale,
                 w_pool_out=w_pool_out, w_attn_out=w_attn_out, w_out=w_out, g_ffn=g_ffn, w_up=w_up, conv_w=conv_w,
                 conv_b=conv_b, w_down=w_down, g_final=g_final)
    mom_m = dict(g_mix=m_g_mix, w_in=m_w_in, b_gate=m_b_gate, w_pool_lin=m_w_pool_lin, pool_scale=m_pool_scale,
                 w_pool_out=m_w_pool_out, w_attn_out=m_w_attn_out, w_out=m_w_out, g_ffn=m_g_ffn, w_up=m_w_up,
                 conv_w=m_conv_w, conv_b=m_conv_b, w_down=m_w_down, g_final=m_g_final)
    mom_v = dict(g_mix=v_g_mix, w_in=v_w_in, b_gate=v_b_gate, w_pool_lin=v_w_pool_lin, pool_scale=v_pool_scale,
                 w_pool_out=v_w_pool_out, w_attn_out=v_w_attn_out, w_out=v_w_out, g_ffn=v_g_ffn, w_up=v_w_up,
                 conv_w=v_conv_w, conv_b=v_conv_b, w_down=v_w_down, g_final=v_g_final)
    def full(g):
        return g.reshape(N_DEV * g.shape[1], g.shape[2])

    shard = {n: _to_comm(n, given[n][0]).astype(BF16) for n in _BIG}
    W = dict(w_in=full(_all_gather_weights([shard["w_in"]])[0]))
    rest = [n for n in _BIG if n != "w_in"]
    rest_lands = [lax.empty((N_DEV,) + shard[n].shape, BF16) for n in rest]
    rest_wait, rest_token = _split_copies(
        "gather_rest", [shard[n] for n in rest], rest_lands, 4 * len(rest), _gather_rest_plan(len(rest)),
        start_after=W["w_in"], wait_after=lambda: rest_after[0])
    rest_after = []

    def late_weights(after):
        rest_after.append(after)
        lands = _gather_forward(*rest_wait())
        return {n: full(l) for n, l in zip(rest, lands)}

    lin_all = _all_gather_small(w_pool_lin[0].reshape(4 * 32, POOL_GROUP_WIDTH))
    W["w_pool_lin"] = jnp.moveaxis(lin_all.reshape(N_DEV, 4, 32, POOL_GROUP_WIDTH), 0, 1).reshape(
        4, POOL_GROUP_WIDTH, POOL_GROUP_WIDTH).astype(BF16)
    conv_w_all = _all_gather_small(jnp.pad(conv_w[0], ((0, CONV_ROWS - 3), (0, 0))))
    conv_w_full = jnp.moveaxis(conv_w_all[:, :3, :], 0, 1).reshape(3, 2 * D_FF)
    small = dict(g_mix=g_mix, b_gate=b_gate, pool_scale=pool_scale, g_ffn=g_ffn, conv_b=conv_b,
                 g_final=g_final.reshape(1, D_MODEL), conv_w_full=conv_w_full)

    pending = []
    done = {}

    def emit(group):
        names = list(group)
        parts = []
        for n in names:
            g = group[n]
            if n == "w_pool_lin":
                g = jnp.moveaxis(g.reshape(4, N_DEV, 32, POOL_GROUP_WIDTH), 1, 0).reshape(
                    N_DEV, 4 * 32, POOL_GROUP_WIDTH).astype(BF16)
            elif n == "conv_w":
                g = jnp.pad(jnp.moveaxis(g.reshape(3, N_DEV, 2 * D_FF // N_DEV), 1, 0),
                            ((0, 0), (0, CONV_ROWS - 3), (0, 0))).astype(BF16)
            else:
                g = g.reshape(N_DEV, g.shape[0] // N_DEV, g.shape[1])
            parts.append(g)
        tag = names[0]
        got = _exchange_d2d("exchange_d2d_" + tag, parts)
        sums, lands = zip(*[_pair_sum("pair_sum_" + n, p, r) for n, p, r in zip(names, parts, got)])
        wait, token = _split_copies("exchange_ici_" + tag, sums, lands, 3 * len(names), _exchange_ici_plan(len(names)),
                                    start_after=sums[0], wait_after=lambda: done["grad_x"])
        pending.append((names, wait))
        return token[0, 0]

    sq_cols, grad_x, _, small_grads = _local_step(x[0], loss_target[0], W, small, first_token=rest_token[0, 0],
                                                  late_weights=late_weights, emit=emit)
    done["grad_x"] = grad_x
    loss = lax.psum(0.5 / D_MODEL * jnp.sum(sq_cols), AXES)

    res = {}
    for n, r in [(n, r) for names, wait in pending for n, r in zip(names, wait()[1])]:
        g = _sum_partials("sum_" + n, r)
        if n in _TRANSPOSED:
            g = g.T
        elif n == "conv_w":
            g = g[:3]
        shape = given[n].shape
        rows = math.prod(shape[:-1])
        w2, m2, v2 = (d[n].reshape(rows, shape[-1]) for d in (given, mom_m, mom_v))
        upd = _adamw_update("adamw_" + n, w2, g.reshape(rows, shape[-1]), m2, v2)
        res[n] = [a.reshape(shape) for a in (g,) + tuple(upd)]

    packed_small = [_pack_small(d) for d in (given, mom_m, mom_v)]
    res_small = [_unpack_small(p, given) for p in _small_allreduce_adamw(_pack_small(small_grads), *packed_small)]

    order = ["g_mix", "w_in", "b_gate", "w_pool_lin", "pool_scale", "w_pool_out", "w_attn_out", "w_out", "g_ffn",
             "w_up", "conv_w", "conv_b", "w_down", "g_final"]
    outs = [loss, grad_x[None]]
    for kind in range(4):
        for n in order:
            outs.append(res[n][kind] if n in res else res_small[kind][n])
    return tuple(outs)
```

```python
import math

import jax
import jax.numpy as jnp
from jax import lax
from jax.experimental import pallas as pl
from jax.experimental.pallas import tpu as pltpu

F32 = jnp.float32
BF16 = jnp.bfloat16
MESH = pl.DeviceIdType.MESH
AXES = ("x", "y", "c")
N_DEV = 8

D_MODEL = 2048
POOL_WINDOWS = (2, 4, 8, 16)
POOL_GROUP_WIDTH = 256
POOL_WIDTH = 1024
ATTN_GROUPS = ((128, 1), (512, 4), (2048, 16))
SPAN = 128
HEADS_PER_GROUP = 4
N_ATTN_HEADS = 12
HEAD_DIM = 128
ATTN_WIDTH = 1536
QKV_WIDTH = 3 * ATTN_WIDTH
ATTN_OUT_WIDTH = 512
IN_WIDTH = 9728
D_FF = 5632
RMS_EPS = 1e-6
ADAM_LR, ADAM_B1, ADAM_B2, ADAM_EPS, ADAM_WD, ADAM_STEP = 0.001, 0.9, 0.999, 1e-08, 0.01, 10

VMEM_LIMIT_BYTES = 48 * 1024 * 1024
HALO = 16
PACK_COLS = 1024
SMALL_ROWS = 24
NEG_BIG = -1e30

_DN = {"nn": (((1,), (0,)), ((), ())), "nt": (((1,), (1,)), ((), ())), "tn": (((0,), (0,)), ((), ()))}


def _params(*sem):
    return pltpu.CompilerParams(dimension_semantics=sem, vmem_limit_bytes=VMEM_LIMIT_BYTES)


def _dot(a, b, mode):
    return lax.dot_general(a.astype(BF16), b.astype(BF16), _DN[mode], preferred_element_type=F32)


def _mm(name, a, b, *, mode, grid, a_spec, b_spec, o_spec, o_shape, o_dtype, acc_shape, res=None, res_spec=None,
        after=None):
    nk = grid[2]
    n_in = 2 + (res is not None) + (after is not None)

    def body(*refs):
        a_ref, b_ref = refs[:2]
        r_ref = refs[2] if res is not None else None
        o_ref = refs[n_in]
        scr = refs[n_in + 1:]
        prod = _dot(a_ref[...], b_ref[...], mode)

        def finish(val):
            if r_ref is not None:
                val = val + r_ref[...]
            o_ref[...] = val.astype(o_ref.dtype)

        if nk == 1:
            finish(prod)
        else:
            acc = scr[0]
            k = pl.program_id(2)

            @pl.when(k == 0)
            def _():
                acc[...] = prod

            @pl.when(k > 0)
            def _():
                acc[...] += prod

            @pl.when(k == nk - 1)
            def _():
                finish(acc[...])

    in_specs = [a_spec, b_spec]
    args = [a, b]
    if res is not None:
        in_specs.append(res_spec)
        args.append(res)
    if after is not None:
        in_specs.append(pl.BlockSpec((8, 128), lambda i, j, k: (0, 0)))
        args.append(after)
    return pl.pallas_call(
        body, grid=grid, in_specs=in_specs, out_specs=o_spec,
        out_shape=jax.ShapeDtypeStruct(o_shape, o_dtype),
        scratch_shapes=[] if nk == 1 else [pltpu.VMEM(acc_shape, F32)],
        compiler_params=_params("parallel", "parallel", "arbitrary"), name=name)(*args)


def _mm_nn(name, a, b, *, tm, tn, tk, o_dtype, n_cols=None, b_col_off=0, res=None, after=None):
    M, K = a.shape
    N = b.shape[1] if n_cols is None else n_cols
    off = b_col_off // tn
    return _mm(name, a, b, mode="nn", grid=(M // tm, N // tn, K // tk),
               a_spec=pl.BlockSpec((tm, tk), lambda i, j, k: (i, k)),
               b_spec=pl.BlockSpec((tk, tn), lambda i, j, k: (k, j + off)),
               o_spec=pl.BlockSpec((tm, tn), lambda i, j, k: (i, j)),
               o_shape=(M, N), o_dtype=o_dtype, acc_shape=(tm, tn), res=res,
               res_spec=pl.BlockSpec((tm, tn), lambda i, j, k: (i, j)), after=after)


def _mm_nt(name, a, b, *, tm, tn, tk, o_dtype, n_rows=None, b_row_off=0):
    M, K = a.shape
    N = b.shape[0] if n_rows is None else n_rows
    off = b_row_off // tn
    return _mm(name, a, b, mode="nt", grid=(M // tm, N // tn, K // tk),
               a_spec=pl.BlockSpec((tm, tk), lambda i, j, k: (i, k)),
               b_spec=pl.BlockSpec((tn, tk), lambda i, j, k: (j + off, k)),
               o_spec=pl.BlockSpec((tm, tn), lambda i, j, k: (i, j)),
               o_shape=(M, N), o_dtype=o_dtype, acc_shape=(tm, tn))


def _mm_tn(name, a, b, *, tm, tn, tk):
    K, M = a.shape
    N = b.shape[1]
    return _mm(name, a, b, mode="tn", grid=(M // tm, N // tn, K // tk),
               a_spec=pl.BlockSpec((tk, tm), lambda i, j, k: (k, i)),
               b_spec=pl.BlockSpec((tk, tn), lambda i, j, k: (k, j)),
               o_spec=pl.BlockSpec((tm, tn), lambda i, j, k: (i, j)),
               o_shape=(M, N), o_dtype=BF16, acc_shape=(tm, tn))


def _rms_fwd(name, x, g):
    S, D = x.shape
    T = 512

    def body(x_ref, g_ref, h_ref):
        xv = x_ref[...]
        r = lax.rsqrt(jnp.mean(xv * xv, axis=-1, keepdims=True) + RMS_EPS)
        h_ref[...] = (xv * r * g_ref[...]).astype(BF16)

    return pl.pallas_call(
        body, grid=(S // T,),
        in_specs=[pl.BlockSpec((T, D), lambda i: (i, 0)), pl.BlockSpec((1, D), lambda i: (0, 0))],
        out_specs=pl.BlockSpec((T, D), lambda i: (i, 0)),
        out_shape=jax.ShapeDtypeStruct((S, D), BF16), compiler_params=_params("parallel"), name=name)(x, g)


def _rms_bwd(name, x, g, dh, dres):
    S, D = x.shape
    T = 256

    def body(x_ref, g_ref, dh_ref, dres_ref, dx_ref, dxb_ref, dg_ref):
        i = pl.program_id(0)
        xv = x_ref[...]
        r = lax.rsqrt(jnp.mean(xv * xv, axis=-1, keepdims=True) + RMS_EPS)
        xhat = xv * r
        dhv = dh_ref[...].astype(F32)
        gdh = dhv * g_ref[...]
        c = jnp.mean(xhat * gdh, axis=-1, keepdims=True)
        dx = dres_ref[...] + r * (gdh - xhat * c)
        dx_ref[...] = dx
        dxb_ref[...] = dx.astype(BF16)
        part = jnp.sum(dhv * xhat, axis=0, keepdims=True)

        @pl.when(i == 0)
        def _():
            dg_ref[...] = part

        @pl.when(i > 0)
        def _():
            dg_ref[...] += part

    row = pl.BlockSpec((T, D), lambda i: (i, 0))
    vec = pl.BlockSpec((1, D), lambda i: (0, 0))
    return pl.pallas_call(
        body, grid=(S // T,), in_specs=[row, vec, row, row], out_specs=[row, row, vec],
        out_shape=[jax.ShapeDtypeStruct((S, D), F32), jax.ShapeDtypeStruct((S, D), BF16),
                   jax.ShapeDtypeStruct((1, D), F32)],
        compiler_params=_params("arbitrary"), name=name)(x, g, dh, dres)


def _final_loss(name, x, g, target):
    S, D = x.shape
    T = 256

    def body(x_ref, g_ref, t_ref, dx_ref, dxb_ref, dg_ref, sq_ref):
        i = pl.program_id(0)
        xv = x_ref[...]
        r = lax.rsqrt(jnp.mean(xv * xv, axis=-1, keepdims=True) + RMS_EPS)
        xhat = xv * r
        err = xhat * g_ref[...] - t_ref[...]
        dy = err * (1.0 / D)
        gdy = dy * g_ref[...]
        c = jnp.mean(xhat * gdy, axis=-1, keepdims=True)
        dx = r * (gdy - xhat * c)
        dx_ref[...] = dx
        dxb_ref[...] = dx.astype(BF16)
        dg_part = jnp.sum(dy * xhat, axis=0, keepdims=True)
        sq_part = jnp.sum(err * err, axis=0, keepdims=True)

        @pl.when(i == 0)
        def _():
            dg_ref[...] = dg_part
            sq_ref[...] = sq_part

        @pl.when(i > 0)
        def _():
            dg_ref[...] += dg_part
            sq_ref[...] += sq_part

    row = pl.BlockSpec((T, D), lambda i: (i, 0))
    vec = pl.BlockSpec((1, D), lambda i: (0, 0))
    return pl.pallas_call(
        body, grid=(S // T,), in_specs=[row, vec, row], out_specs=[row, row, vec, vec],
        out_shape=[jax.ShapeDtypeStruct((S, D), F32), jax.ShapeDtypeStruct((S, D), BF16),
                   jax.ShapeDtypeStruct((1, D), F32), jax.ShapeDtypeStruct((1, D), F32)],
        compiler_params=_params("arbitrary"), name=name)(x, g, target)


def _prev_halo_spec(T, C, col_map):
    per = T // HALO
    return pl.BlockSpec((HALO, C), lambda *g: (jnp.maximum(g[0] * per - 1, 0), col_map(*g)))


def _pool_window_sums(ext, T, i):
    t = i * T + lax.broadcasted_iota(jnp.int32, (T, 1), 0)
    outs = []
    cnts = []
    for gi, w in enumerate(POOL_WINDOWS):
        s = ext[:, gi * POOL_GROUP_WIDTH:(gi + 1) * POOL_GROUP_WIDTH]
        sh = 1
        while sh < w:
            s = s + pltpu.roll(s, sh, 0)
            sh *= 2
        cnt = jnp.minimum(t + 1, w).astype(F32)
        outs.append(s[HALO:] / cnt - ext[HALO:, gi * POOL_GROUP_WIDTH:(gi + 1) * POOL_GROUP_WIDTH])
        cnts.append(cnt)
    return outs, cnts


def _load_ext(ext_ref, halo_ref, blk_ref, first):
    ext_ref[0:HALO, :] = jnp.where(first, 0.0, halo_ref[...].astype(F32))
    ext_ref[HALO:, :] = blk_ref[...].astype(F32)


def _pool_fwd(u, w_lin, scale):
    S = u.shape[0]
    T = 512
    W = POOL_WIDTH

    def body(u_ref, halo_ref, wl_ref, sc_ref, y_ref, ext_ref):
        i = pl.program_id(0)
        _load_ext(ext_ref, halo_ref, u_ref, i == 0)
        pooled, _ = _pool_window_sums(ext_ref[...], T, i)
        for gi in range(4):
            z = _dot(pooled[gi], wl_ref[gi], "nn")
            cs = slice(gi * POOL_GROUP_WIDTH, (gi + 1) * POOL_GROUP_WIDTH)
            y_ref[:, cs] = (z * sc_ref[:, cs]).astype(BF16)

    return pl.pallas_call(
        body, grid=(S // T,),
        in_specs=[pl.BlockSpec((T, W), lambda i: (i, 0)), _prev_halo_spec(T, W, lambda i: 0),
                  pl.BlockSpec((4, 256, 256), lambda i: (0, 0, 0)), pl.BlockSpec((1, W), lambda i: (0, 0))],
        out_specs=pl.BlockSpec((T, W), lambda i: (i, 0)),
        out_shape=jax.ShapeDtypeStruct((S, W), BF16),
        scratch_shapes=[pltpu.VMEM((HALO + T, W), F32)],
        compiler_params=_params("parallel"), name="pool_fwd")(u, u, w_lin, scale)


def _pool_bwd_a(u, dy, w_lin, scale):
    S = u.shape[0]
    T = 512
    W = POOL_WIDTH

    def body(u_ref, halo_ref, dy_ref, wl_ref, sc_ref, e_ref, dwl_ref, dsc_ref, ext_ref):
        i = pl.program_id(0)
        _load_ext(ext_ref, halo_ref, u_ref, i == 0)
        pooled, cnts = _pool_window_sums(ext_ref[...], T, i)
        for gi in range(4):
            cs = slice(gi * POOL_GROUP_WIDTH, (gi + 1) * POOL_GROUP_WIDTH)
            z = _dot(pooled[gi], wl_ref[gi], "nn")
            dyg = dy_ref[:, cs]
            dsc = jnp.sum(dyg * z, axis=0, keepdims=True)
            dz = dyg * sc_ref[:, cs]
            dwl = _dot(pooled[gi], dz, "tn")
            dpooled = _dot(dz, wl_ref[gi], "nt")
            e_ref[:, cs] = dpooled / cnts[gi]

            @pl.when(i == 0)
            def _():
                dwl_ref[gi] = dwl
                dsc_ref[:, cs] = dsc

            @pl.when(i > 0)
            def _():
                dwl_ref[gi] += dwl
                dsc_ref[:, cs] += dsc

    return pl.pallas_call(
        body, grid=(S // T,),
        in_specs=[pl.BlockSpec((T, W), lambda i: (i, 0)), _prev_halo_spec(T, W, lambda i: 0),
                  pl.BlockSpec((T, W), lambda i: (i, 0)),
                  pl.BlockSpec((4, 256, 256), lambda i: (0, 0, 0)), pl.BlockSpec((1, W), lambda i: (0, 0))],
        out_specs=[pl.BlockSpec((T, W), lambda i: (i, 0)), pl.BlockSpec((4, 256, 256), lambda i: (0, 0, 0)),
                   pl.BlockSpec((1, W), lambda i: (0, 0))],
        out_shape=[jax.ShapeDtypeStruct((S, W), F32), jax.ShapeDtypeStruct((4, 256, 256), F32),
                   jax.ShapeDtypeStruct((1, W), F32)],
        scratch_shapes=[pltpu.VMEM((HALO + T, W), F32)],
        compiler_params=_params("arbitrary"), name="pool_bwd_a")(u, u, dy, w_lin, scale)


def _pool_bwd_b(e):
    S = e.shape[0]
    T = 512
    W = POOL_WIDTH
    per = T // HALO
    n_blk = S // T

    def body(e_ref, halo_ref, du_ref, ext_ref):
        i = pl.program_id(0)
        ext_ref[0:T, :] = e_ref[...]
        ext_ref[T:, :] = jnp.where(i == n_blk - 1, 0.0, halo_ref[...])
        ext = ext_ref[...]
        n = T + HALO
        t = i * T + lax.broadcasted_iota(jnp.int32, (T, 1), 0)
        for gi, w in enumerate(POOL_WINDOWS):
            cs = slice(gi * POOL_GROUP_WIDTH, (gi + 1) * POOL_GROUP_WIDTH)
            s = ext[:, cs]
            sh = 1
            while sh < w:
                s = s + pltpu.roll(s, n - sh, 0)
                sh *= 2
            cnt = jnp.minimum(t + 1, w).astype(F32)
            du_ref[:, cs] = (s[0:T] - ext[0:T, cs] * cnt).astype(BF16)

    return pl.pallas_call(
        body, grid=(n_blk,),
        in_specs=[pl.BlockSpec((T, W), lambda i: (i, 0)),
                  pl.BlockSpec((HALO, W), lambda i: (jnp.minimum((i + 1) * per, S // HALO - 1), 0))],
        out_specs=pl.BlockSpec((T, W), lambda i: (i, 0)),
        out_shape=jax.ShapeDtypeStruct((S, W), BF16),
        scratch_shapes=[pltpu.VMEM((T + HALO, W), F32)],
        compiler_params=_params("parallel"), name="pool_bwd_b")(e, e)


def _slope(head):
    return 2.0 ** (-8.0 * (head + 1) / N_ATTN_HEADS)


def _attn_masks(b):
    row = lax.broadcasted_iota(jnp.int32, (SPAN, SPAN), 0)
    col = lax.broadcasted_iota(jnp.int32, (SPAN, SPAN), 1)
    dist_cur = (row - col).astype(F32)
    valid_cur = col <= row
    valid_prev = jnp.logical_and(col >= row, b > 0)
    return dist_cur, valid_cur, valid_prev


def _attn_fwd(gi, qkv):
    S = qkv.shape[0]
    d = ATTN_GROUPS[gi][1]
    L = S // d
    nb = L // SPAN
    view = qkv.reshape(L, d * QKV_WIDTH)
    W = ATTN_OUT_WIDTH
    ncb = QKV_WIDTH // W
    scale = HEAD_DIM ** -0.5

    def body(q_ref, kc_ref, kp_ref, vc_ref, vp_ref, o_ref, lse_ref):
        b = pl.program_id(1)
        dist_cur, valid_cur, valid_prev = _attn_masks(b)
        for hi in range(HEADS_PER_GROUP):
            hs = slice(hi * HEAD_DIM, (hi + 1) * HEAD_DIM)
            sl = _slope(gi * HEADS_PER_GROUP + hi) * d
            q = q_ref[:, hs]
            s_cur = _dot(q, kc_ref[:, hs], "nt") * scale + (-sl) * dist_cur
            s_prev = _dot(q, kp_ref[:, hs], "nt") * scale + (-sl) * (dist_cur + SPAN)
            s_cur = jnp.where(valid_cur, s_cur, NEG_BIG)
            s_prev = jnp.where(valid_prev, s_prev, NEG_BIG)
            m = jnp.maximum(jnp.max(s_cur, axis=-1, keepdims=True), jnp.max(s_prev, axis=-1, keepdims=True))
            p_cur = jnp.exp(s_cur - m)
            p_prev = jnp.exp(s_prev - m)
            l = jnp.sum(p_cur, axis=-1, keepdims=True) + jnp.sum(p_prev, axis=-1, keepdims=True)
            o = (_dot(p_cur, vc_ref[:, hs], "nn") + _dot(p_prev, vp_ref[:, hs], "nn")) / l
            o_ref[:, hs] = o
            lse_ref[:, hs] = jnp.broadcast_to(m + jnp.log(l), (SPAN, HEAD_DIM))

    def cur(sec):
        return pl.BlockSpec((SPAN, W), lambda r, b: (b, r * ncb + 3 * sec + gi))

    def prev(sec):
        return pl.BlockSpec((SPAN, W), lambda r, b: (jnp.maximum(b - 1, 0), r * ncb + 3 * sec + gi))

    out = pl.BlockSpec((SPAN, W), lambda r, b: (b, r))
    o, lse = pl.pallas_call(
        body, grid=(d, nb), in_specs=[cur(0), cur(1), prev(1), cur(2), prev(2)], out_specs=[out, out],
        out_shape=[jax.ShapeDtypeStruct((L, d * W), F32), jax.ShapeDtypeStruct((L, d * W), F32)],
        compiler_params=_params("parallel", "parallel"), name=f"attn_fwd_g{gi}")(view, view, view, view, view)
    return o.reshape(S, W), lse.reshape(S, W)


def _attn_bwd(gi, qkv, do, lse, dp):
    S = qkv.shape[0]
    d = ATTN_GROUPS[gi][1]
    L = S // d
    nb = L // SPAN
    W = ATTN_OUT_WIDTH
    ncb = QKV_WIDTH // W
    scale = HEAD_DIM ** -0.5
    view = qkv.reshape(L, d * QKV_WIDTH)
    do_v, lse_v, dp_v = (a.reshape(L, d * W) for a in (do, lse, dp))

    def body(q_ref, kc_ref, vc_ref, kp_ref, vp_ref, qn_ref, do_ref, don_ref, lse_ref, lsen_ref, dp_ref, dpn_ref,
             dq_ref, dk_ref, dv_ref):
        b = pl.program_id(1)
        dist_cur, valid_cur, valid_prev = _attn_masks(b)
        row = lax.broadcasted_iota(jnp.int32, (SPAN, SPAN), 0)
        col = lax.broadcasted_iota(jnp.int32, (SPAN, SPAN), 1)
        valid_next = jnp.logical_and(col >= row, b < nb - 1)
        for hi in range(HEADS_PER_GROUP):
            hs = slice(hi * HEAD_DIM, (hi + 1) * HEAD_DIM)
            sl = _slope(gi * HEADS_PER_GROUP + hi) * d
            q, kc, vc, kp, vp, qn = (r[:, hs] for r in (q_ref, kc_ref, vc_ref, kp_ref, vp_ref, qn_ref))
            dov, don = do_ref[:, hs], don_ref[:, hs]
            lse_b, lse_n, dp_b, dp_n = lse_ref[:, hs], lsen_ref[:, hs], dp_ref[:, hs], dpn_ref[:, hs]
            bias_cur = (-sl) * dist_cur
            bias_far = (-sl) * (dist_cur + SPAN)
            p_cur = jnp.where(valid_cur, jnp.exp(_dot(q, kc, "nt") * scale + bias_cur - lse_b), 0.0)
            p_prev = jnp.where(valid_prev, jnp.exp(_dot(q, kp, "nt") * scale + bias_far - lse_b), 0.0)
            ds_cur = p_cur * (_dot(dov, vc, "nt") - dp_b)
            ds_prev = p_prev * (_dot(dov, vp, "nt") - dp_b)
            dq_ref[:, hs] = ((_dot(ds_cur, kc, "nn") + _dot(ds_prev, kp, "nn")) * scale).astype(BF16)
            p_next = jnp.where(valid_next, jnp.exp(_dot(qn, kc, "nt") * scale + bias_far - lse_n), 0.0)
            ds_next = p_next * (_dot(don, vc, "nt") - dp_n)
            dv_ref[:, hs] = (_dot(p_cur, dov, "tn") + _dot(p_next, don, "tn")).astype(BF16)
            dk_ref[:, hs] = ((_dot(ds_cur, q, "tn") + _dot(ds_next, qn, "tn")) * scale).astype(BF16)

    def qkv_spec(sec, shift):
        def imap(r, b):
            return (jnp.clip(b + shift, 0, nb - 1), r * ncb + 3 * sec + gi)
        return pl.BlockSpec((SPAN, W), imap)

    def act_spec(shift):
        return pl.BlockSpec((SPAN, W), lambda r, b: (jnp.clip(b + shift, 0, nb - 1), r))

    out = act_spec(0)
    dq, dk, dv = pl.pallas_call(
        body, grid=(d, nb),
        in_specs=[qkv_spec(0, 0), qkv_spec(1, 0), qkv_spec(2, 0), qkv_spec(1, -1), qkv_spec(2, -1), qkv_spec(0, 1),
                  act_spec(0), act_spec(1), act_spec(0), act_spec(1), act_spec(0), act_spec(1)],
        out_specs=[out, out, out],
        out_shape=[jax.ShapeDtypeStruct((L, d * W), BF16)] * 3,
        compiler_params=_params("parallel", "parallel"), name=f"attn_bwd_g{gi}")(
            view, view, view, view, view, view, do_v, do_v, lse_v, lse_v, dp_v, dp_v)
    return dq.reshape(S, W), dk.reshape(S, W), dv.reshape(S, W)


def _group_weights(l_refs):
    l0, l1, l2 = (r[...] for r in l_refs)
    m = jnp.maximum(jnp.maximum(l0, l1), l2)
    e = [jnp.exp(l0 - m), jnp.exp(l1 - m), jnp.exp(l2 - m)]
    tot = e[0] + e[1] + e[2]
    return [ei / tot for ei in e]


def _merge_fwd(outs, lses):
    S, W = outs[0].shape
    T = 512

    def body(o0, o1, o2, l0, l1, l2, y_ref):
        w = _group_weights((l0, l1, l2))
        y_ref[...] = (w[0] * o0[...] + w[1] * o1[...] + w[2] * o2[...]).astype(BF16)

    blk = pl.BlockSpec((T, W), lambda i: (i, 0))
    return pl.pallas_call(
        body, grid=(S // T,), in_specs=[blk] * 6, out_specs=blk, out_shape=jax.ShapeDtypeStruct((S, W), BF16),
        compiler_params=_params("parallel"), name="merge_fwd")(*outs, *lses)


def _merge_bwd(dy, outs, lses):
    S, W = outs[0].shape
    T = 512

    def body(dy_ref, o0, o1, o2, l0, l1, l2, do0, do1, do2, dp0, dp1, dp2):
        w = _group_weights((l0, l1, l2))
        dyv = dy_ref[...]
        y = w[0] * o0[...] + w[1] * o1[...] + w[2] * o2[...]
        prod = dyv * y
        dots = []
        for hi in range(HEADS_PER_GROUP):
            hs = slice(hi * HEAD_DIM, (hi + 1) * HEAD_DIM)
            dots.append(jnp.broadcast_to(jnp.sum(prod[:, hs], axis=-1, keepdims=True), (T, HEAD_DIM)))
        dot_b = jnp.concatenate(dots, axis=1)
        for wg, do_ref, dp_ref in zip(w, (do0, do1, do2), (dp0, dp1, dp2)):
            do_ref[...] = (wg * dyv).astype(BF16)
            dp_ref[...] = wg * dot_b

    blk = pl.BlockSpec((T, W), lambda i: (i, 0))
    res = pl.pallas_call(
        body, grid=(S // T,), in_specs=[blk] * 7, out_specs=[blk] * 6,
        out_shape=[jax.ShapeDtypeStruct((S, W), BF16)] * 3 + [jax.ShapeDtypeStruct((S, W), F32)] * 3,
        compiler_params=_params("parallel"), name="merge_bwd")(dy, *outs, *lses)
    return res[:3], res[3:]


def _mix_fwd(pool_y, attn_y, w_pool_out_t, w_attn_out_t, gates_pre, b_gate):
    S = pool_y.shape[0]
    D = D_MODEL
    tm, tn = 512, 512
    nj = D // tn

    def body(p_ref, a_ref, wp_ref, wa_ref, g0_ref, g1_ref, b0_ref, b1_ref, mixed_ref, yp_ref, ya_ref):
        yp = _dot(p_ref[...], wp_ref[...], "nt")
        ya = _dot(a_ref[...], wa_ref[...], "nt")
        g0 = jax.nn.sigmoid(g0_ref[...] + b0_ref[...])
        g1 = jax.nn.sigmoid(g1_ref[...] + b1_ref[...])
        mixed_ref[...] = (g0 * yp + g1 * ya).astype(BF16)
        yp_ref[...] = yp.astype(BF16)
        ya_ref[...] = ya.astype(BF16)

    out = pl.BlockSpec((tm, tn), lambda i, j: (i, j))
    return pl.pallas_call(
        body, grid=(S // tm, nj),
        in_specs=[pl.BlockSpec((tm, POOL_WIDTH), lambda i, j: (i, 0)),
                  pl.BlockSpec((tm, ATTN_OUT_WIDTH), lambda i, j: (i, 0)),
                  pl.BlockSpec((tn, POOL_WIDTH), lambda i, j: (j, 0)),
                  pl.BlockSpec((tn, ATTN_OUT_WIDTH), lambda i, j: (j, 0)),
                  pl.BlockSpec((tm, tn), lambda i, j: (i, j)), pl.BlockSpec((tm, tn), lambda i, j: (i, j + nj)),
                  pl.BlockSpec((1, tn), lambda i, j: (0, j)), pl.BlockSpec((1, tn), lambda i, j: (0, j + nj))],
        out_specs=[out, out, out], out_shape=[jax.ShapeDtypeStruct((S, D), BF16)] * 3,
        compiler_params=_params("parallel", "parallel"), name="mix_fwd")(
            pool_y, attn_y, w_pool_out_t, w_attn_out_t, gates_pre, gates_pre, b_gate, b_gate)


def _mix_bwd(dmixed, gates_pre, b_gate, y_pool, y_attn):
    S = dmixed.shape[0]
    D = D_MODEL
    T = 256

    def body(dm_ref, gp_ref, b_ref, yp_ref, ya_ref, dyp_ref, dya_ref, dg_ref, db_ref):
        i = pl.program_id(0)
        dm = dm_ref[...].astype(F32)
        parts = []
        for br, (y_ref, dy_ref) in enumerate(((yp_ref, dyp_ref), (ya_ref, dya_ref))):
            cs = slice(br * D, (br + 1) * D)
            g = jax.nn.sigmoid(gp_ref[:, cs] + b_ref[:, cs])
            dy_ref[...] = (dm * g).astype(BF16)
            dpre = dm * y_ref[...].astype(F32) * g * (1.0 - g)
            dg_ref[:, cs] = dpre.astype(BF16)
            parts.append(jnp.sum(dpre, axis=0, keepdims=True))

        @pl.when(i == 0)
        def _():
            db_ref[:, 0:D] = parts[0]
            db_ref[:, D:2 * D] = parts[1]

        @pl.when(i > 0)
        def _():
            db_ref[:, 0:D] += parts[0]
            db_ref[:, D:2 * D] += parts[1]

    row = pl.BlockSpec((T, D), lambda i: (i, 0))
    row2 = pl.BlockSpec((T, 2 * D), lambda i: (i, 0))
    vec2 = pl.BlockSpec((1, 2 * D), lambda i: (0, 0))
    return pl.pallas_call(
        body, grid=(S // T,), in_specs=[row, row2, vec2, row, row], out_specs=[row, row, row2, vec2],
        out_shape=[jax.ShapeDtypeStruct((S, D), BF16), jax.ShapeDtypeStruct((S, D), BF16),
                   jax.ShapeDtypeStruct((S, 2 * D), BF16), jax.ShapeDtypeStruct((1, 2 * D), F32)],
        compiler_params=_params("arbitrary"), name="mix_bwd")(dmixed, gates_pre, b_gate, y_pool, y_attn)


_CONV_T, _CONV_C = 512, 512


def _conv_taps(ext, w_ref, b_ref):
    x2 = pltpu.roll(ext, 2, 0)[HALO:]
    x1 = pltpu.roll(ext, 1, 0)[HALO:]
    x0 = ext[HALO:]
    y = b_ref[...] + w_ref[0:1, :] * x2
    y = y + w_ref[1:2, :] * x1
    y = y + w_ref[2:3, :] * x0
    return y, (x2, x1, x0)


def _gelu_parts(a):
    cdf = 0.5 * (1.0 + lax.erf(a * (1.0 / math.sqrt(2.0))))
    return cdf, a * cdf


def _conv_specs(S):
    T, C = _CONV_T, _CONV_C
    nj = D_FF // C
    blk = pl.BlockSpec((2, T, C), lambda i, j: (0, i, j))
    per = T // HALO
    halo = pl.BlockSpec((2, HALO, C), lambda i, j: (0, jnp.maximum(i * per - 1, 0), j))
    w_a = pl.BlockSpec((3, C), lambda i, j: (0, j))
    w_b = pl.BlockSpec((3, C), lambda i, j: (0, j + nj))
    b_a = pl.BlockSpec((1, C), lambda i, j: (0, j))
    b_b = pl.BlockSpec((1, C), lambda i, j: (0, j + nj))
    return T, C, nj, blk, halo, w_a, w_b, b_a, b_b


def _conv_load(ext_ref, halo_ref, up_ref, first):
    for h in range(2):
        ext_ref[h, 0:HALO, :] = jnp.where(first, 0.0, halo_ref[h].astype(F32))
        ext_ref[h, HALO:, :] = up_ref[h].astype(F32)


def _convffn_fwd(up_pre, conv_w, conv_b):
    S = up_pre.shape[1]
    T, C, nj, blk, halo, w_a, w_b, b_a, b_b = _conv_specs(S)

    def body(up_ref, halo_ref, wa_ref, wb_ref, ba_ref, bb_ref, act_ref, ext_ref):
        _conv_load(ext_ref, halo_ref, up_ref, pl.program_id(0) == 0)
        a, _ = _conv_taps(ext_ref[0], wa_ref, ba_ref)
        b, _ = _conv_taps(ext_ref[1], wb_ref, bb_ref)
        _, gelu = _gelu_parts(a)
        act_ref[...] = (gelu * b).astype(BF16)

    return pl.pallas_call(
        body, grid=(S // T, nj), in_specs=[blk, halo, w_a, w_b, b_a, b_b],
        out_specs=pl.BlockSpec((T, C), lambda i, j: (i, j)),
        out_shape=jax.ShapeDtypeStruct((S, D_FF), BF16),
        scratch_shapes=[pltpu.VMEM((2, HALO + T, C), F32)],
        compiler_params=_params("parallel", "parallel"), name="convffn_fwd")(
            up_pre, up_pre, conv_w, conv_w, conv_b, conv_b)


def _convffn_bwd_a(up_pre, dact, conv_w, conv_b):
    S = up_pre.shape[1]
    T, C, nj, blk, halo, w_a, w_b, b_a, b_b = _conv_specs(S)
    def swap(spec):
        return pl.BlockSpec(spec.block_shape, lambda j, i, _m=spec.index_map: _m(i, j))

    def body(up_ref, halo_ref, dact_ref, wa_ref, wb_ref, ba_ref, bb_ref, dup_ref, dwa_ref, dwb_ref, dba_ref, dbb_ref,
             ext_ref):
        i = pl.program_id(1)
        _conv_load(ext_ref, halo_ref, up_ref, i == 0)
        a, xa = _conv_taps(ext_ref[0], wa_ref, ba_ref)
        b, xb = _conv_taps(ext_ref[1], wb_ref, bb_ref)
        cdf, gelu = _gelu_parts(a)
        dgelu = cdf + a * (jnp.exp(-0.5 * a * a) * (1.0 / math.sqrt(2.0 * math.pi)))
        dact = dact_ref[...].astype(F32)
        da = dact * b * dgelu
        db = dact * gelu
        dup_ref[0] = da.astype(BF16)
        dup_ref[1] = db.astype(BF16)
        for dval, xs, dw_ref, dbias_ref in ((da, xa, dwa_ref, dba_ref), (db, xb, dwb_ref, dbb_ref)):
            dw = jnp.concatenate([jnp.sum(dval * xk, axis=0, keepdims=True) for xk in xs], axis=0)
            dbias = jnp.sum(dval, axis=0, keepdims=True)

            @pl.when(i == 0)
            def _():
                dw_ref[...] = dw
                dbias_ref[...] = dbias

            @pl.when(i > 0)
            def _():
                dw_ref[...] += dw
                dbias_ref[...] += dbias

    dact_spec = pl.BlockSpec((T, C), lambda j, i: (i, j))
    dw_spec = pl.BlockSpec((3, C), lambda j, i: (0, j))
    db_spec = pl.BlockSpec((1, C), lambda j, i: (0, j))
    res = pl.pallas_call(
        body, grid=(nj, S // T),
        in_specs=[swap(blk), swap(halo), dact_spec, swap(w_a), swap(w_b), swap(b_a), swap(b_b)],
        out_specs=[swap(blk), dw_spec, dw_spec, db_spec, db_spec],
        out_shape=[jax.ShapeDtypeStruct((2, S, D_FF), BF16), jax.ShapeDtypeStruct((3, D_FF), F32),
                   jax.ShapeDtypeStruct((3, D_FF), F32), jax.ShapeDtypeStruct((1, D_FF), F32),
                   jax.ShapeDtypeStruct((1, D_FF), F32)],
        scratch_shapes=[pltpu.VMEM((2, HALO + T, C), F32)],
        compiler_params=_params("parallel", "arbitrary"), name="convffn_bwd_a")(
            up_pre, up_pre, dact, conv_w, conv_w, conv_b, conv_b)
    dup, dwa, dwb, dba, dbb = res
    dconv_w = jnp.concatenate([dwa, dwb], axis=1)
    dconv_b = jnp.concatenate([dba, dbb], axis=1)
    return dup, dconv_w, dconv_b


def _convffn_bwd_b(dup, conv_w):
    S = dup.shape[1]
    T, C = _CONV_T, _CONV_C
    nj = D_FF // C
    per = T // HALO
    n_blk = S // T

    def body(d_ref, halo_ref, wa_ref, wb_ref, o_ref, ext_ref):
        i = pl.program_id(0)
        n = T + HALO
        for h, w_ref in enumerate((wa_ref, wb_ref)):
            ext_ref[h, 0:T, :] = d_ref[h].astype(F32)
            ext_ref[h, T:, :] = jnp.where(i == n_blk - 1, 0.0, halo_ref[h].astype(F32))
            ext = ext_ref[h]
            y = w_ref[2:3, :] * ext[0:T] + w_ref[1:2, :] * pltpu.roll(ext, n - 1, 0)[0:T]
            y = y + w_ref[0:1, :] * pltpu.roll(ext, n - 2, 0)[0:T]
            o_ref[h] = y.astype(BF16)

    blk = pl.BlockSpec((2, T, C), lambda i, j: (0, i, j))
    halo = pl.BlockSpec((2, HALO, C), lambda i, j: (0, jnp.minimum((i + 1) * per, S // HALO - 1), j))
    return pl.pallas_call(
        body, grid=(n_blk, nj),
        in_specs=[blk, halo, pl.BlockSpec((3, C), lambda i, j: (0, j)), pl.BlockSpec((3, C), lambda i, j: (0, j + nj))],
        out_specs=blk, out_shape=jax.ShapeDtypeStruct((2, S, D_FF), BF16),
        scratch_shapes=[pltpu.VMEM((2, T + HALO, C), F32)],
        compiler_params=_params("parallel", "parallel"), name="convffn_bwd_b")(dup, dup, conv_w, conv_w)


def _position():
    return lax.axis_index("x"), lax.axis_index("y"), lax.axis_index("c")


def _flip(pos, k):
    x, y, c = pos
    return (1 - x if k & 4 else x, 1 - y if k & 2 else y, 1 - c if k & 1 else c)


def _index(pos):
    return 4 * pos[0] + 2 * pos[1] + pos[2]


def _all_gather_weights(shards):
    n = len(shards)

    def body(*refs):
        ins, outs = refs[:n], refs[n:2 * n]
        send_sems, recv_sems, local_sems = refs[2 * n:]
        me = _position()
        x, y, c = me
        sibling = (x, y, 1 - c)
        chips = [(1 - x, y), (x, 1 - y), (1 - x, 1 - y)]

        def copy(a, k, block, to, own=False):
            slot = outs[a].at[_index(block)]
            return pltpu.make_async_remote_copy(
                src_ref=ins[a] if own else slot, dst_ref=slot, send_sem=send_sems.at[7 * a + k],
                recv_sem=recv_sems.at[7 * a + k], device_id=to, device_id_type=MESH)

        mines = [pltpu.make_async_copy(ins[a], outs[a].at[_index(me)], local_sems.at[a]) for a in range(n)]
        started = []
        for a in range(n):
            mines[a].start()
            first = [copy(a, 0, me, sibling, own=True)]
            first += [copy(a, 1 + j, me, (*chip, c), own=True) for j, chip in enumerate(chips)]
            for cp in first:
                cp.start()
            started += first
        for j, chip in enumerate(chips):
            for a in range(n):
                copy(a, 1 + j, (*chip, c), me).wait_recv()
                passed = copy(a, 4 + j, (*chip, c), sibling)
                passed.start()
                started.append(passed)
        for a in range(n):
            copy(a, 0, sibling, me).wait_recv()
            for j, chip in enumerate(chips):
                copy(a, 4 + j, (*chip, 1 - c), me).wait_recv()
        for cp in started:
            cp.wait_send()
        for cp in mines:
            cp.wait()

    hbm = pl.BlockSpec(memory_space=pl.ANY)
    return pl.pallas_call(
        body, out_shape=[jax.ShapeDtypeStruct((N_DEV,) + s.shape, s.dtype) for s in shards],
        in_specs=[hbm] * n, out_specs=[hbm] * n,
        scratch_shapes=[pltpu.SemaphoreType.DMA((7 * n,)), pltpu.SemaphoreType.DMA((7 * n,)),
                        pltpu.SemaphoreType.DMA((n,))],
        name="all_gather_weights")(*shards)


_HBM = pl.BlockSpec(memory_space=pltpu.HBM)
_SEM = pl.BlockSpec(memory_space=pltpu.SEMAPHORE)
_ANY = pl.BlockSpec(memory_space=pl.ANY)
_EFFECT = pltpu.SideEffectType.DATAFLOW_SIDE_EFFECTING


def _in_hbm(a):
    return pltpu.with_memory_space_constraint(a, pltpu.HBM)


def _split_copies(name, srcs, lands, n_copies, plan, start_after, wait_after):
    ns, nl = len(srcs), len(lands)

    def start_body(*refs):
        src_refs, land_refs = refs[:ns], refs[ns:ns + nl]
        send_sems, recv_sems = refs[ns + nl + 1], refs[ns + nl + 2]
        token = refs[-1]
        for k, (src, dst, _, peer) in enumerate(plan(src_refs, land_refs)):
            pltpu.make_async_remote_copy(src_ref=src, dst_ref=dst, send_sem=send_sems.at[k], recv_sem=recv_sems.at[k],
                                         device_id=peer, device_id_type=MESH).start()
        token[...] = jnp.zeros_like(token)

    thru = [pltpu.HBM(a.shape, a.dtype) for a in list(srcs) + list(lands)]
    res = pl.pallas_call(
        start_body, name=name + "_start",
        out_shape=[pltpu.SemaphoreType.DMA((n_copies,)), pltpu.SemaphoreType.DMA((n_copies,))] + thru
        + [jax.ShapeDtypeStruct((8, 128), F32)],
        in_specs=[_HBM] * (ns + nl) + [_ANY], out_specs=[_SEM, _SEM] + [_HBM] * (ns + nl) + [pl.BlockSpec(memory_space=pltpu.VMEM)],
        input_output_aliases={i: 2 + i for i in range(ns + nl)},
        compiler_params=pltpu.CompilerParams(has_side_effects=_EFFECT),
    )(*[_in_hbm(a) for a in list(srcs) + list(lands)], start_after)
    send_sems, recv_sems, token = res[0], res[1], res[-1]
    thru_vals = res[2:-1]

    def wait():
        def wait_body(*refs):
            src_refs, land_refs = refs[:ns], refs[ns:ns + nl]
            s_sems, r_sems = refs[ns + nl], refs[ns + nl + 1]
            for k, (src, _, dst, peer) in enumerate(plan(src_refs, land_refs)):
                cp = pltpu.make_async_remote_copy(src_ref=src, dst_ref=dst, send_sem=s_sems.at[k], recv_sem=r_sems.at[k],
                                                  device_id=peer, device_id_type=MESH)
                cp.wait_send()
                cp.wait_recv()

        out = pl.pallas_call(
            wait_body, name=name + "_wait", out_shape=thru,
            in_specs=[_HBM] * (ns + nl) + [_SEM, _SEM, _ANY], out_specs=[_HBM] * (ns + nl),
            input_output_aliases={i: i for i in range(ns + nl)},
            compiler_params=pltpu.CompilerParams(has_side_effects=_EFFECT),
        )(*thru_vals, send_sems, recv_sems, wait_after())
        return list(out[:ns]), list(out[ns:])

    return wait, token


def _gather_direct_plan(n):
    def plan(src_refs, land_refs):
        me = _position()
        peers = [_flip(me, k) for k in range(1, N_DEV)]
        return [(src_refs[a], land_refs[a].at[_index(me)], land_refs[a].at[_index(p)], p)
                for a in range(n) for p in peers]
    return plan


def _place_own(shards):
    n = len(shards)

    def body(*refs):
        ins, outs, sems = refs[:n], refs[n:2 * n], refs[2 * n]
        me = _position()
        copies = [pltpu.make_async_copy(ins[a], outs[a].at[_index(me)], sems.at[a]) for a in range(n)]
        for cp in copies:
            cp.start()
        for cp in copies:
            cp.wait()

    return pl.pallas_call(
        body, out_shape=[jax.ShapeDtypeStruct((N_DEV,) + s.shape, s.dtype) for s in shards],
        in_specs=[_ANY] * n, out_specs=[_ANY] * n, scratch_shapes=[pltpu.SemaphoreType.DMA((n,))],
        name="place_own")(*shards)


def _chip_index(pos):
    return 2 * pos[0] + pos[1]


def _exchange_d2d(name, parts):
    n = len(parts)

    def body(*refs):
        ins, outs = refs[:n], refs[n:2 * n]
        send_sems, recv_sems = refs[2 * n:]
        x, y, c = _position()
        sibling = (x, y, 1 - c)

        def copy(a, q, core):
            return pltpu.make_async_remote_copy(
                src_ref=ins[a].at[2 * q + core], dst_ref=outs[a].at[q], send_sem=send_sems.at[4 * a + q],
                recv_sem=recv_sems.at[4 * a + q], device_id=sibling, device_id_type=MESH)

        sent = [copy(a, q, 1 - c) for a in range(n) for q in range(4)]
        for cp in sent:
            cp.start()
        for cp in sent:
            cp.wait_recv()
        for cp in sent:
            cp.wait_send()

    return pl.pallas_call(
        body, out_shape=[jax.ShapeDtypeStruct((4,) + p.shape[1:], p.dtype) for p in parts],
        in_specs=[_ANY] * n, out_specs=[_ANY] * n,
        scratch_shapes=[pltpu.SemaphoreType.DMA((4 * n,)), pltpu.SemaphoreType.DMA((4 * n,))],
        name=name)(*parts)


def _pair_sum(name, part, got):
    _, R, C = part.shape
    T = _row_tile(R, 16)

    def body(p_ref, g_ref, s_out, l_out):
        c = lax.axis_index("c")
        mine = jnp.where(c == 0, p_ref[0, 0].astype(F32), p_ref[0, 1].astype(F32))
        s = (mine + g_ref[0].astype(F32)).astype(BF16)
        s_out[0] = s
        l_out[0] = s

    out = pl.BlockSpec((1, T, C), lambda q, i: (q, i, 0))
    return pl.pallas_call(
        body, grid=(4, R // T),
        in_specs=[pl.BlockSpec((1, 2, T, C), lambda q, i: (q, 0, i, 0)), out], out_specs=[out, out],
        out_shape=[jax.ShapeDtypeStruct((4, R, C), BF16)] * 2,
        compiler_params=_params("parallel", "parallel"), name=name)(part.reshape(4, 2, R, C), got)


def _exchange_ici_plan(n):
    def plan(src_refs, land_refs):
        me = _position()
        x, y, c = me
        peers = [(1 - x, y, c), (x, 1 - y, c), (1 - x, 1 - y, c)]
        return [(src_refs[a].at[_chip_index(p)], land_refs[a].at[_chip_index(me)], land_refs[a].at[_chip_index(p)], p)
                for a in range(n) for p in peers]
    return plan


def _adamw(w, g, m, v):
    m = ADAM_B1 * m + (1.0 - ADAM_B1) * g
    v = ADAM_B2 * v + (1.0 - ADAM_B2) * (g * g)
    m_hat = m / (1.0 - ADAM_B1 ** ADAM_STEP)
    v_hat = v / (1.0 - ADAM_B2 ** ADAM_STEP)
    delta = -ADAM_LR * (m_hat / (jnp.sqrt(v_hat) + ADAM_EPS) + ADAM_WD * w)
    return delta, m, v


def _row_tile(rows, unit, cap=256):
    best = rows
    for t in range(unit, min(rows, cap) + 1, unit):
        if rows % t == 0:
            best = t
    return best


def _sum_partials(name, recv):
    n_src, R, C = recv.shape
    T = _row_tile(R, 16)

    def body(r_ref, g_out):
        g = r_ref[0].astype(F32)
        for s in range(1, n_src):
            g = g + r_ref[s].astype(F32)
        g_out[...] = g

    return pl.pallas_call(
        body, grid=(R // T,), in_specs=[pl.BlockSpec((n_src, T, C), lambda i: (0, i, 0))],
        out_specs=pl.BlockSpec((T, C), lambda i: (i, 0)), out_shape=jax.ShapeDtypeStruct((R, C), F32),
        compiler_params=_params("parallel"), name=name)(recv)


def _adamw_update(name, w, g, m, v):
    R, C = w.shape
    T = _row_tile(R, 8)

    def body(w_ref, g_ref, m_ref, v_ref, d_out, m_out, v_out):
        delta, mn, vn = _adamw(w_ref[...], g_ref[...], m_ref[...], v_ref[...])
        d_out[...] = delta
        m_out[...] = mn
        v_out[...] = vn

    blk = pl.BlockSpec((T, C), lambda i: (i, 0))
    return pl.pallas_call(
        body, grid=(R // T,), in_specs=[blk] * 4, out_specs=[blk] * 3,
        out_shape=[jax.ShapeDtypeStruct((R, C), F32)] * 3, compiler_params=_params("parallel"), name=name)(w, g, m, v)


def _gather_vmem(p_ref, gath, send_sems, recv_sems):
    me = _position()
    gath[_index(me)] = p_ref[...]
    copies = []
    for k in range(1, N_DEV):
        peer = _flip(me, k)
        copies.append(pltpu.make_async_remote_copy(
            src_ref=p_ref, dst_ref=gath.at[_index(me)], send_sem=send_sems.at[k - 1],
            recv_sem=recv_sems.at[k - 1], device_id=peer, device_id_type=MESH))
    for cp in copies:
        cp.start()
    for k in range(1, N_DEV):
        peer = _flip(me, k)
        pltpu.make_async_remote_copy(
            src_ref=p_ref, dst_ref=gath.at[_index(peer)], send_sem=send_sems.at[k - 1],
            recv_sem=recv_sems.at[k - 1], device_id=peer, device_id_type=MESH).wait_recv()
    for cp in copies:
        cp.wait_send()


def _all_gather_small(p):
    R, C = p.shape

    def body(p_ref, out_ref, send_sems, recv_sems):
        _gather_vmem(p_ref, out_ref, send_sems, recv_sems)

    vm = pl.BlockSpec(memory_space=pltpu.VMEM)
    return pl.pallas_call(
        body, in_specs=[vm], out_specs=vm, out_shape=jax.ShapeDtypeStruct((N_DEV, R, C), F32),
        scratch_shapes=[pltpu.SemaphoreType.DMA((7,)), pltpu.SemaphoreType.DMA((7,))],
        name="all_gather_small")(p)


def _small_allreduce_adamw(part, w, m, v):
    R, C = part.shape

    def body(p_ref, w_ref, m_ref, v_ref, g_out, d_out, m_out, v_out, gath, send_sems, recv_sems):
        _gather_vmem(p_ref, gath, send_sems, recv_sems)
        g = gath[0]
        for s in range(1, N_DEV):
            g = g + gath[s]
        delta, mn, vn = _adamw(w_ref[...], g, m_ref[...], v_ref[...])
        g_out[...] = g
        d_out[...] = delta
        m_out[...] = mn
        v_out[...] = vn

    vm = pl.BlockSpec(memory_space=pltpu.VMEM)
    return pl.pallas_call(
        body, in_specs=[vm] * 4, out_specs=[vm] * 4, out_shape=[jax.ShapeDtypeStruct((R, C), F32)] * 4,
        scratch_shapes=[pltpu.VMEM((N_DEV, R, C), F32), pltpu.SemaphoreType.DMA((7,)), pltpu.SemaphoreType.DMA((7,))],
        name="small_allreduce_adamw")(part, w, m, v)


_BIG = ("w_in", "w_up", "w_down", "w_out", "w_pool_out", "w_attn_out")
_TRANSPOSED = ("w_in", "w_up", "w_pool_out", "w_attn_out")
_SMALL = (("g_mix", 2048), ("b_gate", 4096), ("pool_scale", 1024), ("g_ffn", 2048), ("conv_b", 11264), ("g_final", 2048))
CONV_ROWS = 8


def _to_comm(name, shard):
    return shard.T if name in _TRANSPOSED else shard


def _pack_small(vals):
    flat = [vals[n].reshape(-1) for n, _ in _SMALL]
    used = sum(f.shape[0] for f in flat)
    flat.append(jnp.zeros((SMALL_ROWS * PACK_COLS - used,), F32))
    return jnp.concatenate(flat).reshape(SMALL_ROWS, PACK_COLS)


def _unpack_small(packed, like):
    flat = packed.reshape(-1)
    out, off = {}, 0
    for name, n in _SMALL:
        out[name] = flat[off:off + n].reshape(like[name].shape)
        off += n
    return out


def _local_step(x, target, W, small, first_token=0.0, late_weights=None, emit=None):
    S = x.shape[0]
    o_qkv, o_gate = POOL_WIDTH, POOL_WIDTH + QKV_WIDTH
    if emit is None:
        emit = lambda grads: None

    def zero(token):
        return 0.0 if token is None else token[0, 0]
    w_in_t = W["w_in"]

    h1 = _rms_fwd("rms1_fwd", x, small["g_mix"] + first_token)
    u = _mm_nt("proj_u", h1, w_in_t, tm=1024, tn=512, tk=2048, o_dtype=F32, n_rows=POOL_WIDTH, b_row_off=0)
    qkv = _mm_nt("proj_qkv", h1, w_in_t, tm=1024, tn=512, tk=2048, o_dtype=BF16, n_rows=QKV_WIDTH, b_row_off=o_qkv)
    gates_pre = _mm_nt("proj_gates", h1, w_in_t, tm=1024, tn=512, tk=2048, o_dtype=F32, n_rows=2 * D_MODEL,
                       b_row_off=o_gate)
    pool_y = _pool_fwd(u, W["w_pool_lin"], small["pool_scale"])
    outs, lses = zip(*[_attn_fwd(gi, qkv) for gi in range(3)])
    attn_y = _merge_fwd(outs, lses)
    if late_weights is not None:
        W = dict(W, **late_weights(attn_y))
    w_up_t = W["w_up"]
    mixed, y_pool, y_attn = _mix_fwd(pool_y, attn_y, W["w_pool_out"], W["w_attn_out"], gates_pre, small["b_gate"])
    x1 = _mm_nn("out_proj", mixed, W["w_out"], tm=1024, tn=512, tk=2048, o_dtype=F32, res=x)

    h2 = _rms_fwd("rms2_fwd", x1, small["g_ffn"])
    nj = D_FF // 512
    up_pre = _mm("up_proj", h2, w_up_t, mode="nt", grid=(S // 1024, 2 * nj, 1),
                 a_spec=pl.BlockSpec((1024, D_MODEL), lambda i, j, k: (i, 0)),
                 b_spec=pl.BlockSpec((512, D_MODEL), lambda i, j, k: (j, 0)),
                 o_spec=pl.BlockSpec((None, 1024, 512), lambda i, j, k: (j // nj, i, j % nj)),
                 o_shape=(2, S, D_FF), o_dtype=BF16, acc_shape=(1024, 512))
    act = _convffn_fwd(up_pre, small["conv_w_full"], small["conv_b"])
    x2 = _mm_nn("down_proj", act, W["w_down"], tm=1024, tn=512, tk=D_FF // 2, o_dtype=F32, res=x1)

    dx2, dx2_b, dg_final, sq_cols = _final_loss("final_loss", x2, small["g_final"], target)

    dact = _mm_nt("d_act", dx2_b, W["w_down"], tm=1024, tn=512, tk=2048, o_dtype=BF16)
    dw_down = _mm_tn("dw_down", act, dx2_b, tm=512, tn=2048, tk=512)
    tok = emit(dict(w_down=dw_down))
    dup, dconv_w, dconv_b = _convffn_bwd_a(up_pre, dact, small["conv_w_full"], small["conv_b"] + zero(tok))
    dup_pre = _convffn_bwd_b(dup, small["conv_w_full"])
    tkh = D_FF // 2
    dh2 = _mm("d_h2", dup_pre, w_up_t, mode="nn", grid=(S // 1024, D_MODEL // 1024, 4),
              a_spec=pl.BlockSpec((None, 1024, tkh), lambda i, j, k: (k // 2, i, k % 2)),
              b_spec=pl.BlockSpec((tkh, 1024), lambda i, j, k: (k, j)),
              o_spec=pl.BlockSpec((1024, 1024), lambda i, j, k: (i, j)),
              o_shape=(S, D_MODEL), o_dtype=F32, acc_shape=(1024, 1024))
    dw_up = _mm("dw_up", dup_pre, h2, mode="tn", grid=(2 * nj, 1, S // 512),
                a_spec=pl.BlockSpec((None, 512, 512), lambda i, j, k: (i // nj, k, i % nj)),
                b_spec=pl.BlockSpec((512, D_MODEL), lambda i, j, k: (k, 0)),
                o_spec=pl.BlockSpec((512, D_MODEL), lambda i, j, k: (i, 0)),
                o_shape=(2 * D_FF, D_MODEL), o_dtype=BF16, acc_shape=(512, D_MODEL))
    tok = emit(dict(w_up=dw_up, conv_w=dconv_w))
    dx1, dx1_b, dg_ffn = _rms_bwd("rms2_bwd", x1, small["g_ffn"] + zero(tok), dh2, dx2)

    dmixed = _mm_nt("d_mixed", dx1_b, W["w_out"], tm=1024, tn=512, tk=2048, o_dtype=BF16)
    dw_out = _mm_tn("dw_out", mixed, dx1_b, tm=1024, tn=2048, tk=512)
    tok = emit(dict(w_out=dw_out))
    dy_pool, dy_attn, dgates, db_gate = _mix_bwd(dmixed, gates_pre, small["b_gate"] + zero(tok), y_pool, y_attn)
    dpool_y = _mm_nn("d_pool_y", dy_pool, W["w_pool_out"], tm=1024, tn=512, tk=2048, o_dtype=F32)
    dw_pool_out = _mm_tn("dw_pool_out", dy_pool, pool_y, tm=1024, tn=1024, tk=512)
    dattn_y = _mm_nn("d_attn_y", dy_attn, W["w_attn_out"], tm=1024, tn=512, tk=2048, o_dtype=F32)
    dw_attn_out = _mm_tn("dw_attn_out", dy_attn, attn_y, tm=1024, tn=512, tk=512)
    e, dw_pool_lin, dpool_scale = _pool_bwd_a(u, dpool_y, W["w_pool_lin"], small["pool_scale"])
    du = _pool_bwd_b(e)
    dos, dps = _merge_bwd(dattn_y, outs, lses)
    dqkv = [_attn_bwd(gi, qkv, dos[gi], lses[gi], dps[gi]) for gi in range(3)]
    dproj = jnp.concatenate([du] + [dqkv[gi][sec] for sec in range(3) for gi in range(3)] + [dgates], axis=1)
    dw_in = _mm_tn("dw_in", dproj, h1, tm=512, tn=2048, tk=512)
    tok = emit(dict(w_in=dw_in, w_pool_out=dw_pool_out, w_attn_out=dw_attn_out, w_pool_lin=dw_pool_lin))
    dh1 = _mm_nn("d_h1", dproj, w_in_t, tm=1024, tn=1024, tk=IN_WIDTH // 4, o_dtype=F32,
                 after=tok)
    grad_x, _, dg_mix = _rms_bwd("rms1_bwd", x, small["g_mix"] + zero(tok), dh1, dx1)

    grads = dict(w_in=dw_in, w_up=dw_up, w_down=dw_down, w_out=dw_out, w_pool_out=dw_pool_out,
                 w_attn_out=dw_attn_out, w_pool_lin=dw_pool_lin, conv_w=dconv_w)
    small_grads = dict(g_mix=dg_mix, b_gate=db_gate, pool_scale=dpool_scale, g_ffn=dg_ffn, conv_b=dconv_b,
                       g_final=dg_final)
    return sq_cols, grad_x, grads, small_grads


def kernel(x, g_mix, w_in, b_gate, w_pool_lin, pool_scale, w_pool_out, w_attn_out, w_out, g_ffn, w_up, conv_w, conv_b, w_down, g_final, loss_target, m_g_mix, m_w_in, m_b_gate, m_w_pool_lin, m_pool_scale, m_w_pool_out, m_w_attn_out, m_w_out, m_g_ffn, m_w_up, m_conv_w, m_conv_b, m_w_down, m_g_final, v_g_mix, v_w_in, v_b_gate, v_w_pool_lin, v_pool_scale, v_w_pool_out, v_w_attn_out, v_w_out, v_g_ffn, v_w_up, v_conv_w, v_conv_b, v_w_down, v_g_final):
    given = dict(g_mix=g_mix, w_in=w_in, b_gate=b_gate, w_pool_lin=w_pool_lin, pool_scale=pool_scale,
                 w_pool_out=w_pool_out, w_attn_out=w_attn_out, w_out=w_out, g_ffn=g_ffn, w_up=w_up, conv_w=conv_w,
                 conv_b=conv_b, w_down=w_down, g_final=g_final)
    mom_m = dict(g_mix=m_g_mix, w_in=m_w_in, b_gate=m_b_gate, w_pool_lin=m_w_pool_lin, pool_scale=m_pool_scale,
                 w_pool_out=m_w_pool_out, w_attn_out=m_w_attn_out, w_out=m_w_out, g_ffn=m_g_ffn, w_up=m_w_up,
                 conv_w=m_conv_w, conv_b=m_conv_b, w_down=m_w_down, g_final=m_g_final)
    mom_v = dict(g_mix=v_g_mix, w_in=v_w_in, b_gate=v_b_gate, w_pool_lin=v_w_pool_lin, pool_scale=v_pool_scale,
                 w_pool_out=v_w_pool_out, w_attn_out=v_w_attn_out, w_out=v_w_out, g_ffn=v_g_ffn, w_up=v_w_up,
                 conv_w=v_conv_w, conv_b=v_conv_b, w_down=v_w_down, g_final=v_g_final)
    def full(g):
        return g.reshape(N_DEV * g.shape[1], g.shape[2])

    shard = {n: _to_comm(n, given[n][0]).astype(BF16) for n in _BIG}
    W = dict(w_in=full(_all_gather_weights([shard["w_in"]])[0]))
    rest = [n for n in _BIG if n != "w_in"]
    rest_wait, rest_token = _split_copies(
        "gather_rest", [shard[n] for n in rest], _place_own([shard[n] for n in rest]), 7 * len(rest),
        _gather_direct_plan(len(rest)), start_after=W["w_in"], wait_after=lambda: rest_after[0])
    rest_after = []

    def late_weights(after):
        rest_after.append(after)
        return {n: full(l) for n, l in zip(rest, rest_wait()[1])}

    lin_all = _all_gather_small(w_pool_lin[0].reshape(4 * 32, POOL_GROUP_WIDTH))
    W["w_pool_lin"] = jnp.moveaxis(lin_all.reshape(N_DEV, 4, 32, POOL_GROUP_WIDTH), 0, 1).reshape(
        4, POOL_GROUP_WIDTH, POOL_GROUP_WIDTH).astype(BF16)
    conv_w_all = _all_gather_small(jnp.pad(conv_w[0], ((0, CONV_ROWS - 3), (0, 0))))
    conv_w_full = jnp.moveaxis(conv_w_all[:, :3, :], 0, 1).reshape(3, 2 * D_FF)
    small = dict(g_mix=g_mix, b_gate=b_gate, pool_scale=pool_scale, g_ffn=g_ffn, conv_b=conv_b,
                 g_final=g_final.reshape(1, D_MODEL), conv_w_full=conv_w_full)

    pending = []
    done = {}

    def emit(group):
        names = list(group)
        parts = []
        for n in names:
            g = group[n]
            if n == "w_pool_lin":
                g = jnp.moveaxis(g.reshape(4, N_DEV, 32, POOL_GROUP_WIDTH), 1, 0).reshape(
                    N_DEV, 4 * 32, POOL_GROUP_WIDTH).astype(BF16)
            elif n == "conv_w":
                g = jnp.pad(jnp.moveaxis(g.reshape(3, N_DEV, 2 * D_FF // N_DEV), 1, 0),
                            ((0, 0), (0, CONV_ROWS - 3), (0, 0))).astype(BF16)
            else:
                g = g.reshape(N_DEV, g.shape[0] // N_DEV, g.shape[1])
            parts.append(g)
        tag = names[0]
        got = _exchange_d2d("exchange_d2d_" + tag, parts)
        sums, lands = zip(*[_pair_sum("pair_sum_" + n, p, r) for n, p, r in zip(names, parts, got)])
        wait, token = _split_copies("exchange_ici_" + tag, sums, lands, 3 * len(names), _exchange_ici_plan(len(names)),
                                    start_after=got[0], wait_after=lambda: done["grad_x"])
        pending.append((names, wait))
        return token

    sq_cols, grad_x, _, small_grads = _local_step(x[0], loss_target[0], W, small, first_token=rest_token[0, 0],
                                                  late_weights=late_weights, emit=emit)
    done["grad_x"] = grad_x
    loss = lax.psum(0.5 / D_MODEL * jnp.sum(sq_cols), AXES)

    res = {}
    for n, r in [(n, r) for names, wait in pending for n, r in zip(names, wait()[1])]:
        g = _sum_partials("sum_" + n, r)
        if n in _TRANSPOSED:
            g = g.T
        elif n == "conv_w":
            g = g[:3]
        shape = given[n].shape
        rows = math.prod(shape[:-1])
        w2, m2, v2 = (d[n].reshape(rows, shape[-1]) for d in (given, mom_m, mom_v))
        upd = _adamw_update("adamw_" + n, w2, g.reshape(rows, shape[-1]), m2, v2)
        res[n] = [a.reshape(shape) for a in (g,) + tuple(upd)]

    packed_small = [_pack_small(d) for d in (given, mom_m, mom_v)]
    res_small = [_unpack_small(p, given) for p in _small_allreduce_adamw(_pack_small(small_grads), *packed_small)]

    order = ["g_mix", "w_in", "b_gate", "w_pool_lin", "pool_scale", "w_pool_out", "w_attn_out", "w_out", "g_ffn",
             "w_up", "conv_w", "conv_b", "w_down", "g_final"]
    outs = [loss, grad_x[None]]
    for kind in range(4):
        for n in order:
            outs.append(res[n][kind] if n in res else res_small[kind][n])
    return tuple(outs)
```

```python
import math

import jax
import jax.numpy as jnp
from jax import lax
from jax.experimental import pallas as pl
from jax.experimental.pallas import tpu as pltpu

F32 = jnp.float32
BF16 = jnp.bfloat16
MESH = pl.DeviceIdType.MESH
AXES = ("x", "y", "c")
N_DEV = 8

D_MODEL = 2048
POOL_WINDOWS = (2, 4, 8, 16)
POOL_GROUP_WIDTH = 256
POOL_WIDTH = 1024
ATTN_GROUPS = ((128, 1), (512, 4), (2048, 16))
SPAN = 128
HEADS_PER_GROUP = 4
N_ATTN_HEADS = 12
HEAD_DIM = 128
ATTN_WIDTH = 1536
QKV_WIDTH = 3 * ATTN_WIDTH
ATTN_OUT_WIDTH = 512
IN_WIDTH = 9728
D_FF = 5632
RMS_EPS = 1e-6
ADAM_LR, ADAM_B1, ADAM_B2, ADAM_EPS, ADAM_WD, ADAM_STEP = 0.001, 0.9, 0.999, 1e-08, 0.01, 10

VMEM_LIMIT_BYTES = 48 * 1024 * 1024
HALO = 16
PACK_COLS = 1024
SMALL_ROWS = 24
NEG_BIG = -1e30

_DN = {"nn": (((1,), (0,)), ((), ())), "nt": (((1,), (1,)), ((), ())), "tn": (((0,), (0,)), ((), ()))}


def _params(*sem):
    return pltpu.CompilerParams(dimension_semantics=sem, vmem_limit_bytes=VMEM_LIMIT_BYTES)


def _dot(a, b, mode):
    return lax.dot_general(a.astype(BF16), b.astype(BF16), _DN[mode], preferred_element_type=F32)


def _mm(name, a, b, *, mode, grid, a_spec, b_spec, o_spec, o_shape, o_dtype, acc_shape, res=None, res_spec=None,
        after=None):
    nk = grid[2]
    n_in = 2 + (res is not None) + (after is not None)

    def body(*refs):
        a_ref, b_ref = refs[:2]
        r_ref = refs[2] if res is not None else None
        o_ref = refs[n_in]
        scr = refs[n_in + 1:]
        prod = _dot(a_ref[...], b_ref[...], mode)

        def finish(val):
            if r_ref is not None:
                val = val + r_ref[...]
            o_ref[...] = val.astype(o_ref.dtype)

        if nk == 1:
            finish(prod)
        else:
            acc = scr[0]
            k = pl.program_id(2)

            @pl.when(k == 0)
            def _():
                acc[...] = prod

            @pl.when(k > 0)
            def _():
                acc[...] += prod

            @pl.when(k == nk - 1)
            def _():
                finish(acc[...])

    in_specs = [a_spec, b_spec]
    args = [a, b]
    if res is not None:
        in_specs.append(res_spec)
        args.append(res)
    if after is not None:
        in_specs.append(pl.BlockSpec((8, 128), lambda i, j, k: (0, 0)))
        args.append(after)
    return pl.pallas_call(
        body, grid=grid, in_specs=in_specs, out_specs=o_spec,
        out_shape=jax.ShapeDtypeStruct(o_shape, o_dtype),
        scratch_shapes=[] if nk == 1 else [pltpu.VMEM(acc_shape, F32)],
        compiler_params=_params("parallel", "parallel", "arbitrary"), name=name)(*args)


def _mm_nn(name, a, b, *, tm, tn, tk, o_dtype, n_cols=None, b_col_off=0, res=None, after=None):
    M, K = a.shape
    N = b.shape[1] if n_cols is None else n_cols
    off = b_col_off // tn
    return _mm(name, a, b, mode="nn", grid=(M // tm, N // tn, K // tk),
               a_spec=pl.BlockSpec((tm, tk), lambda i, j, k: (i, k)),
               b_spec=pl.BlockSpec((tk, tn), lambda i, j, k: (k, j + off)),
               o_spec=pl.BlockSpec((tm, tn), lambda i, j, k: (i, j)),
               o_shape=(M, N), o_dtype=o_dtype, acc_shape=(tm, tn), res=res,
               res_spec=pl.BlockSpec((tm, tn), lambda i, j, k: (i, j)), after=after)


def _mm_nt(name, a, b, *, tm, tn, tk, o_dtype, n_rows=None, b_row_off=0):
    M, K = a.shape
    N = b.shape[0] if n_rows is None else n_rows
    off = b_row_off // tn
    return _mm(name, a, b, mode="nt", grid=(M // tm, N // tn, K // tk),
               a_spec=pl.BlockSpec((tm, tk), lambda i, j, k: (i, k)),
               b_spec=pl.BlockSpec((tn, tk), lambda i, j, k: (j + off, k)),
               o_spec=pl.BlockSpec((tm, tn), lambda i, j, k: (i, j)),
               o_shape=(M, N), o_dtype=o_dtype, acc_shape=(tm, tn))


def _mm_tn(name, a, b, *, tm, tn, tk):
    K, M = a.shape
    N = b.shape[1]
    return _mm(name, a, b, mode="tn", grid=(M // tm, N // tn, K // tk),
               a_spec=pl.BlockSpec((tk, tm), lambda i, j, k: (k, i)),
               b_spec=pl.BlockSpec((tk, tn), lambda i, j, k: (k, j)),
               o_spec=pl.BlockSpec((tm, tn), lambda i, j, k: (i, j)),
               o_shape=(M, N), o_dtype=BF16, acc_shape=(tm, tn))


def _rms_fwd(name, x, g):
    S, D = x.shape
    T = 512

    def body(x_ref, g_ref, h_ref):
        xv = x_ref[...]
        r = lax.rsqrt(jnp.mean(xv * xv, axis=-1, keepdims=True) + RMS_EPS)
        h_ref[...] = (xv * r * g_ref[...]).astype(BF16)

    return pl.pallas_call(
        body, grid=(S // T,),
        in_specs=[pl.BlockSpec((T, D), lambda i: (i, 0)), pl.BlockSpec((1, D), lambda i: (0, 0))],
        out_specs=pl.BlockSpec((T, D), lambda i: (i, 0)),
        out_shape=jax.ShapeDtypeStruct((S, D), BF16), compiler_params=_params("parallel"), name=name)(x, g)


def _rms_bwd(name, x, g, dh, dres):
    S, D = x.shape
    T = 256

    def body(x_ref, g_ref, dh_ref, dres_ref, dx_ref, dxb_ref, dg_ref):
        i = pl.program_id(0)
        xv = x_ref[...]
        r = lax.rsqrt(jnp.mean(xv * xv, axis=-1, keepdims=True) + RMS_EPS)
        xhat = xv * r
        dhv = dh_ref[...].astype(F32)
        gdh = dhv * g_ref[...]
        c = jnp.mean(xhat * gdh, axis=-1, keepdims=True)
        dx = dres_ref[...] + r * (gdh - xhat * c)
        dx_ref[...] = dx
        dxb_ref[...] = dx.astype(BF16)
        part = jnp.sum(dhv * xhat, axis=0, keepdims=True)

        @pl.when(i == 0)
        def _():
            dg_ref[...] = part

        @pl.when(i > 0)
        def _():
            dg_ref[...] += part

    row = pl.BlockSpec((T, D), lambda i: (i, 0))
    vec = pl.BlockSpec((1, D), lambda i: (0, 0))
    return pl.pallas_call(
        body, grid=(S // T,), in_specs=[row, vec, row, row], out_specs=[row, row, vec],
        out_shape=[jax.ShapeDtypeStruct((S, D), F32), jax.ShapeDtypeStruct((S, D), BF16),
                   jax.ShapeDtypeStruct((1, D), F32)],
        compiler_params=_params("arbitrary"), name=name)(x, g, dh, dres)


def _final_loss(name, x, g, target):
    S, D = x.shape
    T = 256

    def body(x_ref, g_ref, t_ref, dx_ref, dxb_ref, dg_ref, sq_ref):
        i = pl.program_id(0)
        xv = x_ref[...]
        r = lax.rsqrt(jnp.mean(xv * xv, axis=-1, keepdims=True) + RMS_EPS)
        xhat = xv * r
        err = xhat * g_ref[...] - t_ref[...]
        dy = err * (1.0 / D)
        gdy = dy * g_ref[...]
        c = jnp.mean(xhat * gdy, axis=-1, keepdims=True)
        dx = r * (gdy - xhat * c)
        dx_ref[...] = dx
        dxb_ref[...] = dx.astype(BF16)
        dg_part = jnp.sum(dy * xhat, axis=0, keepdims=True)
        sq_part = jnp.sum(err * err, axis=0, keepdims=True)

        @pl.when(i == 0)
        def _():
            dg_ref[...] = dg_part
            sq_ref[...] = sq_part

        @pl.when(i > 0)
        def _():
            dg_ref[...] += dg_part
            sq_ref[...] += sq_part

    row = pl.BlockSpec((T, D), lambda i: (i, 0))
    vec = pl.BlockSpec((1, D), lambda i: (0, 0))
    return pl.pallas_call(
        body, grid=(S // T,), in_specs=[row, vec, row], out_specs=[row, row, vec, vec],
        out_shape=[jax.ShapeDtypeStruct((S, D), F32), jax.ShapeDtypeStruct((S, D), BF16),
                   jax.ShapeDtypeStruct((1, D), F32), jax.ShapeDtypeStruct((1, D), F32)],
        compiler_params=_params("arbitrary"), name=name)(x, g, target)


def _prev_halo_spec(T, C, col_map):
    per = T // HALO
    return pl.BlockSpec((HALO, C), lambda *g: (jnp.maximum(g[0] * per - 1, 0), col_map(*g)))


def _pool_window_sums(ext, T, i):
    t = i * T + lax.broadcasted_iota(jnp.int32, (T, 1), 0)
    outs = []
    cnts = []
    for gi, w in enumerate(POOL_WINDOWS):
        s = ext[:, gi * POOL_GROUP_WIDTH:(gi + 1) * POOL_GROUP_WIDTH]
        sh = 1
        while sh < w:
            s = s + pltpu.roll(s, sh, 0)
            sh *= 2
        cnt = jnp.minimum(t + 1, w).astype(F32)
        outs.append(s[HALO:] / cnt - ext[HALO:, gi * POOL_GROUP_WIDTH:(gi + 1) * POOL_GROUP_WIDTH])
        cnts.append(cnt)
    return outs, cnts


def _load_ext(ext_ref, halo_ref, blk_ref, first):
    ext_ref[0:HALO, :] = jnp.where(first, 0.0, halo_ref[...].astype(F32))
    ext_ref[HALO:, :] = blk_ref[...].astype(F32)


def _pool_fwd(u, w_lin, scale):
    S = u.shape[0]
    T = 512
    W = POOL_WIDTH

    def body(u_ref, halo_ref, wl_ref, sc_ref, y_ref, ext_ref):
        i = pl.program_id(0)
        _load_ext(ext_ref, halo_ref, u_ref, i == 0)
        pooled, _ = _pool_window_sums(ext_ref[...], T, i)
        for gi in range(4):
            z = _dot(pooled[gi], wl_ref[gi], "nn")
            cs = slice(gi * POOL_GROUP_WIDTH, (gi + 1) * POOL_GROUP_WIDTH)
            y_ref[:, cs] = (z * sc_ref[:, cs]).astype(BF16)

    return pl.pallas_call(
        body, grid=(S // T,),
        in_specs=[pl.BlockSpec((T, W), lambda i: (i, 0)), _prev_halo_spec(T, W, lambda i: 0),
                  pl.BlockSpec((4, 256, 256), lambda i: (0, 0, 0)), pl.BlockSpec((1, W), lambda i: (0, 0))],
        out_specs=pl.BlockSpec((T, W), lambda i: (i, 0)),
        out_shape=jax.ShapeDtypeStruct((S, W), BF16),
        scratch_shapes=[pltpu.VMEM((HALO + T, W), F32)],
        compiler_params=_params("parallel"), name="pool_fwd")(u, u, w_lin, scale)


def _pool_bwd_a(u, dy, w_lin, scale):
    S = u.shape[0]
    T = 512
    W = POOL_WIDTH

    def body(u_ref, halo_ref, dy_ref, wl_ref, sc_ref, e_ref, dwl_ref, dsc_ref, ext_ref):
        i = pl.program_id(0)
        _load_ext(ext_ref, halo_ref, u_ref, i == 0)
        pooled, cnts = _pool_window_sums(ext_ref[...], T, i)
        for gi in range(4):
            cs = slice(gi * POOL_GROUP_WIDTH, (gi + 1) * POOL_GROUP_WIDTH)
            z = _dot(pooled[gi], wl_ref[gi], "nn")
            dyg = dy_ref[:, cs]
            dsc = jnp.sum(dyg * z, axis=0, keepdims=True)
            dz = dyg * sc_ref[:, cs]
            dwl = _dot(pooled[gi], dz, "tn")
            dpooled = _dot(dz, wl_ref[gi], "nt")
            e_ref[:, cs] = dpooled / cnts[gi]

            @pl.when(i == 0)
            def _():
                dwl_ref[gi] = dwl
                dsc_ref[:, cs] = dsc

            @pl.when(i > 0)
            def _():
                dwl_ref[gi] += dwl
                dsc_ref[:, cs] += dsc

    return pl.pallas_call(
        body, grid=(S // T,),
        in_specs=[pl.BlockSpec((T, W), lambda i: (i, 0)), _prev_halo_spec(T, W, lambda i: 0),
                  pl.BlockSpec((T, W), lambda i: (i, 0)),
                  pl.BlockSpec((4, 256, 256), lambda i: (0, 0, 0)), pl.BlockSpec((1, W), lambda i: (0, 0))],
        out_specs=[pl.BlockSpec((T, W), lambda i: (i, 0)), pl.BlockSpec((4, 256, 256), lambda i: (0, 0, 0)),
                   pl.BlockSpec((1, W), lambda i: (0, 0))],
        out_shape=[jax.ShapeDtypeStruct((S, W), F32), jax.ShapeDtypeStruct((4, 256, 256), F32),
                   jax.ShapeDtypeStruct((1, W), F32)],
        scratch_shapes=[pltpu.VMEM((HALO + T, W), F32)],
        compiler_params=_params("arbitrary"), name="pool_bwd_a")(u, u, dy, w_lin, scale)


def _pool_bwd_b(e):
    S = e.shape[0]
    T = 512
    W = POOL_WIDTH
    per = T // HALO
    n_blk = S // T

    def body(e_ref, halo_ref, du_ref, ext_ref):
        i = pl.program_id(0)
        ext_ref[0:T, :] = e_ref[...]
        ext_ref[T:, :] = jnp.where(i == n_blk - 1, 0.0, halo_ref[...])
        ext = ext_ref[...]
        n = T + HALO
        t = i * T + lax.broadcasted_iota(jnp.int32, (T, 1), 0)
        for gi, w in enumerate(POOL_WINDOWS):
            cs = slice(gi * POOL_GROUP_WIDTH, (gi + 1) * POOL_GROUP_WIDTH)
            s = ext[:, cs]
            sh = 1
            while sh < w:
                s = s + pltpu.roll(s, n - sh, 0)
                sh *= 2
            cnt = jnp.minimum(t + 1, w).astype(F32)
            du_ref[:, cs] = (s[0:T] - ext[0:T, cs] * cnt).astype(BF16)

    return pl.pallas_call(
        body, grid=(n_blk,),
        in_specs=[pl.BlockSpec((T, W), lambda i: (i, 0)),
                  pl.BlockSpec((HALO, W), lambda i: (jnp.minimum((i + 1) * per, S // HALO - 1), 0))],
        out_specs=pl.BlockSpec((T, W), lambda i: (i, 0)),
        out_shape=jax.ShapeDtypeStruct((S, W), BF16),
        scratch_shapes=[pltpu.VMEM((T + HALO, W), F32)],
        compiler_params=_params("parallel"), name="pool_bwd_b")(e, e)


def _slope(head):
    return 2.0 ** (-8.0 * (head + 1) / N_ATTN_HEADS)


def _attn_masks(b):
    row = lax.broadcasted_iota(jnp.int32, (SPAN, SPAN), 0)
    col = lax.broadcasted_iota(jnp.int32, (SPAN, SPAN), 1)
    dist_cur = (row - col).astype(F32)
    valid_cur = col <= row
    valid_prev = jnp.logical_and(col >= row, b > 0)
    return dist_cur, valid_cur, valid_prev


def _attn_fwd(gi, qkv):
    S = qkv.shape[0]
    d = ATTN_GROUPS[gi][1]
    L = S // d
    nb = L // SPAN
    view = qkv.reshape(L, d * QKV_WIDTH)
    W = ATTN_OUT_WIDTH
    ncb = QKV_WIDTH // W
    scale = HEAD_DIM ** -0.5

    def body(q_ref, kc_ref, kp_ref, vc_ref, vp_ref, o_ref, lse_ref):
        b = pl.program_id(1)
        dist_cur, valid_cur, valid_prev = _attn_masks(b)
        for hi in range(HEADS_PER_GROUP):
            hs = slice(hi * HEAD_DIM, (hi + 1) * HEAD_DIM)
            sl = _slope(gi * HEADS_PER_GROUP + hi) * d
            q = q_ref[:, hs]
            s_cur = _dot(q, kc_ref[:, hs], "nt") * scale + (-sl) * dist_cur
            s_prev = _dot(q, kp_ref[:, hs], "nt") * scale + (-sl) * (dist_cur + SPAN)
            s_cur = jnp.where(valid_cur, s_cur, NEG_BIG)
            s_prev = jnp.where(valid_prev, s_prev, NEG_BIG)
            m = jnp.maximum(jnp.max(s_cur, axis=-1, keepdims=True), jnp.max(s_prev, axis=-1, keepdims=True))
            p_cur = jnp.exp(s_cur - m)
            p_prev = jnp.exp(s_prev - m)
            l = jnp.sum(p_cur, axis=-1, keepdims=True) + jnp.sum(p_prev, axis=-1, keepdims=True)
            o = (_dot(p_cur, vc_ref[:, hs], "nn") + _dot(p_prev, vp_ref[:, hs], "nn")) / l
            o_ref[:, hs] = o
            lse_ref[:, hs] = jnp.broadcast_to(m + jnp.log(l), (SPAN, HEAD_DIM))

    def cur(sec):
        return pl.BlockSpec((SPAN, W), lambda r, b: (b, r * ncb + 3 * sec + gi))

    def prev(sec):
        return pl.BlockSpec((SPAN, W), lambda r, b: (jnp.maximum(b - 1, 0), r * ncb + 3 * sec + gi))

    out = pl.BlockSpec((SPAN, W), lambda r, b: (b, r))
    o, lse = pl.pallas_call(
        body, grid=(d, nb), in_specs=[cur(0), cur(1), prev(1), cur(2), prev(2)], out_specs=[out, out],
        out_shape=[jax.ShapeDtypeStruct((L, d * W), F32), jax.ShapeDtypeStruct((L, d * W), F32)],
        compiler_params=_params("parallel", "parallel"), name=f"attn_fwd_g{gi}")(view, view, view, view, view)
    return o.reshape(S, W), lse.reshape(S, W)


def _attn_bwd(gi, qkv, do, lse, dp):
    S = qkv.shape[0]
    d = ATTN_GROUPS[gi][1]
    L = S // d
    nb = L // SPAN
    W = ATTN_OUT_WIDTH
    ncb = QKV_WIDTH // W
    scale = HEAD_DIM ** -0.5
    view = qkv.reshape(L, d * QKV_WIDTH)
    do_v, lse_v, dp_v = (a.reshape(L, d * W) for a in (do, lse, dp))

    def body(q_ref, kc_ref, vc_ref, kp_ref, vp_ref, qn_ref, do_ref, don_ref, lse_ref, lsen_ref, dp_ref, dpn_ref,
             dq_ref, dk_ref, dv_ref):
        b = pl.program_id(1)
        dist_cur, valid_cur, valid_prev = _attn_masks(b)
        row = lax.broadcasted_iota(jnp.int32, (SPAN, SPAN), 0)
        col = lax.broadcasted_iota(jnp.int32, (SPAN, SPAN), 1)
        valid_next = jnp.logical_and(col >= row, b < nb - 1)
        for hi in range(HEADS_PER_GROUP):
            hs = slice(hi * HEAD_DIM, (hi + 1) * HEAD_DIM)
            sl = _slope(gi * HEADS_PER_GROUP + hi) * d
            q, kc, vc, kp, vp, qn = (r[:, hs] for r in (q_ref, kc_ref, vc_ref, kp_ref, vp_ref, qn_ref))
            dov, don = do_ref[:, hs], don_ref[:, hs]
            lse_b, lse_n, dp_b, dp_n = lse_ref[:, hs], lsen_ref[:, hs], dp_ref[:, hs], dpn_ref[:, hs]
            bias_cur = (-sl) * dist_cur
            bias_far = (-sl) * (dist_cur + SPAN)
            p_cur = jnp.where(valid_cur, jnp.exp(_dot(q, kc, "nt") * scale + bias_cur - lse_b), 0.0)
            p_prev = jnp.where(valid_prev, jnp.exp(_dot(q, kp, "nt") * scale + bias_far - lse_b), 0.0)
            ds_cur = p_cur * (_dot(dov, vc, "nt") - dp_b)
            ds_prev = p_prev * (_dot(dov, vp, "nt") - dp_b)
            dq_ref[:, hs] = ((_dot(ds_cur, kc, "nn") + _dot(ds_prev, kp, "nn")) * scale).astype(BF16)
            p_next = jnp.where(valid_next, jnp.exp(_dot(qn, kc, "nt") * scale + bias_far - lse_n), 0.0)
            ds_next = p_next * (_dot(don, vc, "nt") - dp_n)
            dv_ref[:, hs] = (_dot(p_cur, dov, "tn") + _dot(p_next, don, "tn")).astype(BF16)
            dk_ref[:, hs] = ((_dot(ds_cur, q, "tn") + _dot(ds_next, qn, "tn")) * scale).astype(BF16)

    def qkv_spec(sec, shift):
        def imap(r, b):
            return (jnp.clip(b + shift, 0, nb - 1), r * ncb + 3 * sec + gi)
        return pl.BlockSpec((SPAN, W), imap)

    def act_spec(shift):
        return pl.BlockSpec((SPAN, W), lambda r, b: (jnp.clip(b + shift, 0, nb - 1), r))

    out = act_spec(0)
    dq, dk, dv = pl.pallas_call(
        body, grid=(d, nb),
        in_specs=[qkv_spec(0, 0), qkv_spec(1, 0), qkv_spec(2, 0), qkv_spec(1, -1), qkv_spec(2, -1), qkv_spec(0, 1),
                  act_spec(0), act_spec(1), act_spec(0), act_spec(1), act_spec(0), act_spec(1)],
        out_specs=[out, out, out],
        out_shape=[jax.ShapeDtypeStruct((L, d * W), BF16)] * 3,
        compiler_params=_params("parallel", "parallel"), name=f"attn_bwd_g{gi}")(
            view, view, view, view, view, view, do_v, do_v, lse_v, lse_v, dp_v, dp_v)
    return dq.reshape(S, W), dk.reshape(S, W), dv.reshape(S, W)


def _group_weights(l_refs):
    l0, l1, l2 = (r[...] for r in l_refs)
    m = jnp.maximum(jnp.maximum(l0, l1), l2)
    e = [jnp.exp(l0 - m), jnp.exp(l1 - m), jnp.exp(l2 - m)]
    tot = e[0] + e[1] + e[2]
    return [ei / tot for ei in e]


def _merge_fwd(outs, lses):
    S, W = outs[0].shape
    T = 512

    def body(o0, o1, o2, l0, l1, l2, y_ref):
        w = _group_weights((l0, l1, l2))
        y_ref[...] = (w[0] * o0[...] + w[1] * o1[...] + w[2] * o2[...]).astype(BF16)

    blk = pl.BlockSpec((T, W), lambda i: (i, 0))
    return pl.pallas_call(
        body, grid=(S // T,), in_specs=[blk] * 6, out_specs=blk, out_shape=jax.ShapeDtypeStruct((S, W), BF16),
        compiler_params=_params("parallel"), name="merge_fwd")(*outs, *lses)


def _merge_bwd(dy, outs, lses):
    S, W = outs[0].shape
    T = 512

    def body(dy_ref, o0, o1, o2, l0, l1, l2, do0, do1, do2, dp0, dp1, dp2):
        w = _group_weights((l0, l1, l2))
        dyv = dy_ref[...]
        y = w[0] * o0[...] + w[1] * o1[...] + w[2] * o2[...]
        prod = dyv * y
        dots = []
        for hi in range(HEADS_PER_GROUP):
            hs = slice(hi * HEAD_DIM, (hi + 1) * HEAD_DIM)
            dots.append(jnp.broadcast_to(jnp.sum(prod[:, hs], axis=-1, keepdims=True), (T, HEAD_DIM)))
        dot_b = jnp.concatenate(dots, axis=1)
        for wg, do_ref, dp_ref in zip(w, (do0, do1, do2), (dp0, dp1, dp2)):
            do_ref[...] = (wg * dyv).astype(BF16)
            dp_ref[...] = wg * dot_b

    blk = pl.BlockSpec((T, W), lambda i: (i, 0))
    res = pl.pallas_call(
        body, grid=(S // T,), in_specs=[blk] * 7, out_specs=[blk] * 6,
        out_shape=[jax.ShapeDtypeStruct((S, W), BF16)] * 3 + [jax.ShapeDtypeStruct((S, W), F32)] * 3,
        compiler_params=_params("parallel"), name="merge_bwd")(dy, *outs, *lses)
    return res[:3], res[3:]


def _mix_fwd(pool_y, attn_y, w_pool_out_t, w_attn_out_t, gates_pre, b_gate):
    S = pool_y.shape[0]
    D = D_MODEL
    tm, tn = 512, 512
    nj = D // tn

    def body(p_ref, a_ref, wp_ref, wa_ref, g0_ref, g1_ref, b0_ref, b1_ref, mixed_ref, yp_ref, ya_ref):
        yp = _dot(p_ref[...], wp_ref[...], "nt")
        ya = _dot(a_ref[...], wa_ref[...], "nt")
        g0 = jax.nn.sigmoid(g0_ref[...] + b0_ref[...])
        g1 = jax.nn.sigmoid(g1_ref[...] + b1_ref[...])
        mixed_ref[...] = (g0 * yp + g1 * ya).astype(BF16)
        yp_ref[...] = yp.astype(BF16)
        ya_ref[...] = ya.astype(BF16)

    out = pl.BlockSpec((tm, tn), lambda i, j: (i, j))
    return pl.pallas_call(
        body, grid=(S // tm, nj),
        in_specs=[pl.BlockSpec((tm, POOL_WIDTH), lambda i, j: (i, 0)),
                  pl.BlockSpec((tm, ATTN_OUT_WIDTH), lambda i, j: (i, 0)),
                  pl.BlockSpec((tn, POOL_WIDTH), lambda i, j: (j, 0)),
                  pl.BlockSpec((tn, ATTN_OUT_WIDTH), lambda i, j: (j, 0)),
                  pl.BlockSpec((tm, tn), lambda i, j: (i, j)), pl.BlockSpec((tm, tn), lambda i, j: (i, j + nj)),
                  pl.BlockSpec((1, tn), lambda i, j: (0, j)), pl.BlockSpec((1, tn), lambda i, j: (0, j + nj))],
        out_specs=[out, out, out], out_shape=[jax.ShapeDtypeStruct((S, D), BF16)] * 3,
        compiler_params=_params("parallel", "parallel"), name="mix_fwd")(
            pool_y, attn_y, w_pool_out_t, w_attn_out_t, gates_pre, gates_pre, b_gate, b_gate)


def _mix_bwd(dmixed, gates_pre, b_gate, y_pool, y_attn):
    S = dmixed.shape[0]
    D = D_MODEL
    T = 256

    def body(dm_ref, gp_ref, b_ref, yp_ref, ya_ref, dyp_ref, dya_ref, dg_ref, db_ref):
        i = pl.program_id(0)
        dm = dm_ref[...].astype(F32)
        parts = []
        for br, (y_ref, dy_ref) in enumerate(((yp_ref, dyp_ref), (ya_ref, dya_ref))):
            cs = slice(br * D, (br + 1) * D)
            g = jax.nn.sigmoid(gp_ref[:, cs] + b_ref[:, cs])
            dy_ref[...] = (dm * g).astype(BF16)
            dpre = dm * y_ref[...].astype(F32) * g * (1.0 - g)
            dg_ref[:, cs] = dpre.astype(BF16)
            parts.append(jnp.sum(dpre, axis=0, keepdims=True))

        @pl.when(i == 0)
        def _():
            db_ref[:, 0:D] = parts[0]
            db_ref[:, D:2 * D] = parts[1]

        @pl.when(i > 0)
        def _():
            db_ref[:, 0:D] += parts[0]
            db_ref[:, D:2 * D] += parts[1]

    row = pl.BlockSpec((T, D), lambda i: (i, 0))
    row2 = pl.BlockSpec((T, 2 * D), lambda i: (i, 0))
    vec2 = pl.BlockSpec((1, 2 * D), lambda i: (0, 0))
    return pl.pallas_call(
        body, grid=(S // T,), in_specs=[row, row2, vec2, row, row], out_specs=[row, row, row2, vec2],
        out_shape=[jax.ShapeDtypeStruct((S, D), BF16), jax.ShapeDtypeStruct((S, D), BF16),
                   jax.ShapeDtypeStruct((S, 2 * D), BF16), jax.ShapeDtypeStruct((1, 2 * D), F32)],
        compiler_params=_params("arbitrary"), name="mix_bwd")(dmixed, gates_pre, b_gate, y_pool, y_attn)


_CONV_T, _CONV_C = 512, 512


def _conv_taps(ext, w_ref, b_ref):
    x2 = pltpu.roll(ext, 2, 0)[HALO:]
    x1 = pltpu.roll(ext, 1, 0)[HALO:]
    x0 = ext[HALO:]
    y = b_ref[...] + w_ref[0:1, :] * x2
    y = y + w_ref[1:2, :] * x1
    y = y + w_ref[2:3, :] * x0
    return y, (x2, x1, x0)


def _gelu_parts(a):
    cdf = 0.5 * (1.0 + lax.erf(a * (1.0 / math.sqrt(2.0))))
    return cdf, a * cdf


def _conv_specs(S):
    T, C = _CONV_T, _CONV_C
    nj = D_FF // C
    blk = pl.BlockSpec((2, T, C), lambda i, j: (0, i, j))
    per = T // HALO
    halo = pl.BlockSpec((2, HALO, C), lambda i, j: (0, jnp.maximum(i * per - 1, 0), j))
    w_a = pl.BlockSpec((3, C), lambda i, j: (0, j))
    w_b = pl.BlockSpec((3, C), lambda i, j: (0, j + nj))
    b_a = pl.BlockSpec((1, C), lambda i, j: (0, j))
    b_b = pl.BlockSpec((1, C), lambda i, j: (0, j + nj))
    return T, C, nj, blk, halo, w_a, w_b, b_a, b_b


def _conv_load(ext_ref, halo_ref, up_ref, first):
    for h in range(2):
        ext_ref[h, 0:HALO, :] = jnp.where(first, 0.0, halo_ref[h].astype(F32))
        ext_ref[h, HALO:, :] = up_ref[h].astype(F32)


def _convffn_fwd(up_pre, conv_w, conv_b):
    S = up_pre.shape[1]
    T, C, nj, blk, halo, w_a, w_b, b_a, b_b = _conv_specs(S)

    def body(up_ref, halo_ref, wa_ref, wb_ref, ba_ref, bb_ref, act_ref, ext_ref):
        _conv_load(ext_ref, halo_ref, up_ref, pl.program_id(0) == 0)
        a, _ = _conv_taps(ext_ref[0], wa_ref, ba_ref)
        b, _ = _conv_taps(ext_ref[1], wb_ref, bb_ref)
        _, gelu = _gelu_parts(a)
        act_ref[...] = (gelu * b).astype(BF16)

    return pl.pallas_call(
        body, grid=(S // T, nj), in_specs=[blk, halo, w_a, w_b, b_a, b_b],
        out_specs=pl.BlockSpec((T, C), lambda i, j: (i, j)),
        out_shape=jax.ShapeDtypeStruct((S, D_FF), BF16),
        scratch_shapes=[pltpu.VMEM((2, HALO + T, C), F32)],
        compiler_params=_params("parallel", "parallel"), name="convffn_fwd")(
            up_pre, up_pre, conv_w, conv_w, conv_b, conv_b)


def _convffn_bwd_a(up_pre, dact, conv_w, conv_b):
    S = up_pre.shape[1]
    T, C, nj, blk, halo, w_a, w_b, b_a, b_b = _conv_specs(S)
    def swap(spec):
        return pl.BlockSpec(spec.block_shape, lambda j, i, _m=spec.index_map: _m(i, j))

    def body(up_ref, halo_ref, dact_ref, wa_ref, wb_ref, ba_ref, bb_ref, dup_ref, dwa_ref, dwb_ref, dba_ref, dbb_ref,
             ext_ref):
        i = pl.program_id(1)
        _conv_load(ext_ref, halo_ref, up_ref, i == 0)
        a, xa = _conv_taps(ext_ref[0], wa_ref, ba_ref)
        b, xb = _conv_taps(ext_ref[1], wb_ref, bb_ref)
        cdf, gelu = _gelu_parts(a)
        dgelu = cdf + a * (jnp.exp(-0.5 * a * a) * (1.0 / math.sqrt(2.0 * math.pi)))
        dact = dact_ref[...].astype(F32)
        da = dact * b * dgelu
        db = dact * gelu
        dup_ref[0] = da.astype(BF16)
        dup_ref[1] = db.astype(BF16)
        for dval, xs, dw_ref, dbias_ref in ((da, xa, dwa_ref, dba_ref), (db, xb, dwb_ref, dbb_ref)):
            dw = jnp.concatenate([jnp.sum(dval * xk, axis=0, keepdims=True) for xk in xs], axis=0)
            dbias = jnp.sum(dval, axis=0, keepdims=True)

            @pl.when(i == 0)
            def _():
                dw_ref[...] = dw
                dbias_ref[...] = dbias

            @pl.when(i > 0)
            def _():
                dw_ref[...] += dw
                dbias_ref[...] += dbias

    dact_spec = pl.BlockSpec((T, C), lambda j, i: (i, j))
    dw_spec = pl.BlockSpec((3, C), lambda j, i: (0, j))
    db_spec = pl.BlockSpec((1, C), lambda j, i: (0, j))
    res = pl.pallas_call(
        body, grid=(nj, S // T),
        in_specs=[swap(blk), swap(halo), dact_spec, swap(w_a), swap(w_b), swap(b_a), swap(b_b)],
        out_specs=[swap(blk), dw_spec, dw_spec, db_spec, db_spec],
        out_shape=[jax.ShapeDtypeStruct((2, S, D_FF), BF16), jax.ShapeDtypeStruct((3, D_FF), F32),
                   jax.ShapeDtypeStruct((3, D_FF), F32), jax.ShapeDtypeStruct((1, D_FF), F32),
                   jax.ShapeDtypeStruct((1, D_FF), F32)],
        scratch_shapes=[pltpu.VMEM((2, HALO + T, C), F32)],
        compiler_params=_params("parallel", "arbitrary"), name="convffn_bwd_a")(
            up_pre, up_pre, dact, conv_w, conv_w, conv_b, conv_b)
    dup, dwa, dwb, dba, dbb = res
    dconv_w = jnp.concatenate([dwa, dwb], axis=1)
    dconv_b = jnp.concatenate([dba, dbb], axis=1)
    return dup, dconv_w, dconv_b


def _convffn_bwd_b(dup, conv_w):
    S = dup.shape[1]
    T, C = _CONV_T, _CONV_C
    nj = D_FF // C
    per = T // HALO
    n_blk = S // T

    def body(d_ref, halo_ref, wa_ref, wb_ref, o_ref, ext_ref):
        i = pl.program_id(0)
        n = T + HALO
        for h, w_ref in enumerate((wa_ref, wb_ref)):
            ext_ref[h, 0:T, :] = d_ref[h].astype(F32)
            ext_ref[h, T:, :] = jnp.where(i == n_blk - 1, 0.0, halo_ref[h].astype(F32))
            ext = ext_ref[h]
            y = w_ref[2:3, :] * ext[0:T] + w_ref[1:2, :] * pltpu.roll(ext, n - 1, 0)[0:T]
            y = y + w_ref[0:1, :] * pltpu.roll(ext, n - 2, 0)[0:T]
            o_ref[h] = y.astype(BF16)

    blk = pl.BlockSpec((2, T, C), lambda i, j: (0, i, j))
    halo = pl.BlockSpec((2, HALO, C), lambda i, j: (0, jnp.minimum((i + 1) * per, S // HALO - 1), j))
    return pl.pallas_call(
        body, grid=(n_blk, nj),
        in_specs=[blk, halo, pl.BlockSpec((3, C), lambda i, j: (0, j)), pl.BlockSpec((3, C), lambda i, j: (0, j + nj))],
        out_specs=blk, out_shape=jax.ShapeDtypeStruct((2, S, D_FF), BF16),
        scratch_shapes=[pltpu.VMEM((2, T + HALO, C), F32)],
        compiler_params=_params("parallel", "parallel"), name="convffn_bwd_b")(dup, dup, conv_w, conv_w)


def _position():
    return lax.axis_index("x"), lax.axis_index("y"), lax.axis_index("c")


def _flip(pos, k):
    x, y, c = pos
    return (1 - x if k & 4 else x, 1 - y if k & 2 else y, 1 - c if k & 1 else c)


def _index(pos):
    return 4 * pos[0] + 2 * pos[1] + pos[2]


def _all_gather_weights(shards):
    n = len(shards)

    def body(*refs):
        ins, outs = refs[:n], refs[n:2 * n]
        send_sems, recv_sems, local_sems = refs[2 * n:]
        me = _position()
        x, y, c = me
        sibling = (x, y, 1 - c)
        chips = [(1 - x, y), (x, 1 - y), (1 - x, 1 - y)]

        def copy(a, k, block, to, own=False):
            slot = outs[a].at[_index(block)]
            return pltpu.make_async_remote_copy(
                src_ref=ins[a] if own else slot, dst_ref=slot, send_sem=send_sems.at[7 * a + k],
                recv_sem=recv_sems.at[7 * a + k], device_id=to, device_id_type=MESH)

        mines = [pltpu.make_async_copy(ins[a], outs[a].at[_index(me)], local_sems.at[a]) for a in range(n)]
        started = []
        for a in range(n):
            mines[a].start()
            first = [copy(a, 0, me, sibling, own=True)]
            first += [copy(a, 1 + j, me, (*chip, c), own=True) for j, chip in enumerate(chips)]
            for cp in first:
                cp.start()
            started += first
        for j, chip in enumerate(chips):
            for a in range(n):
                copy(a, 1 + j, (*chip, c), me).wait_recv()
                passed = copy(a, 4 + j, (*chip, c), sibling)
                passed.start()
                started.append(passed)
        for a in range(n):
            copy(a, 0, sibling, me).wait_recv()
            for j, chip in enumerate(chips):
                copy(a, 4 + j, (*chip, 1 - c), me).wait_recv()
        for cp in started:
            cp.wait_send()
        for cp in mines:
            cp.wait()

    hbm = pl.BlockSpec(memory_space=pl.ANY)
    return pl.pallas_call(
        body, out_shape=[jax.ShapeDtypeStruct((N_DEV,) + s.shape, s.dtype) for s in shards],
        in_specs=[hbm] * n, out_specs=[hbm] * n,
        scratch_shapes=[pltpu.SemaphoreType.DMA((7 * n,)), pltpu.SemaphoreType.DMA((7 * n,)),
                        pltpu.SemaphoreType.DMA((n,))],
        name="all_gather_weights")(*shards)


_HBM = pl.BlockSpec(memory_space=pltpu.HBM)
_SEM = pl.BlockSpec(memory_space=pltpu.SEMAPHORE)
_ANY = pl.BlockSpec(memory_space=pl.ANY)
_EFFECT = pltpu.SideEffectType.DATAFLOW_SIDE_EFFECTING


def _in_hbm(a):
    return pltpu.with_memory_space_constraint(a, pltpu.HBM)


def _split_copies(name, srcs, lands, n_copies, plan, start_after, wait_after):
    ns, nl = len(srcs), len(lands)

    def start_body(*refs):
        src_refs, land_refs = refs[:ns], refs[ns:ns + nl]
        send_sems, recv_sems = refs[ns + nl + 1], refs[ns + nl + 2]
        token = refs[-1]
        for k, (src, dst, _, peer) in enumerate(plan(src_refs, land_refs)):
            pltpu.make_async_remote_copy(src_ref=src, dst_ref=dst, send_sem=send_sems.at[k], recv_sem=recv_sems.at[k],
                                         device_id=peer, device_id_type=MESH).start()
        token[...] = jnp.zeros_like(token)

    thru = [pltpu.HBM(a.shape, a.dtype) for a in list(srcs) + list(lands)]
    res = pl.pallas_call(
        start_body, name=name + "_start",
        out_shape=[pltpu.SemaphoreType.DMA((n_copies,)), pltpu.SemaphoreType.DMA((n_copies,))] + thru
        + [jax.ShapeDtypeStruct((8, 128), F32)],
        in_specs=[_HBM] * (ns + nl) + [_ANY], out_specs=[_SEM, _SEM] + [_HBM] * (ns + nl) + [pl.BlockSpec(memory_space=pltpu.VMEM)],
        input_output_aliases={i: 2 + i for i in range(ns + nl)},
        compiler_params=pltpu.CompilerParams(has_side_effects=_EFFECT),
    )(*[_in_hbm(a) for a in list(srcs) + list(lands)], start_after)
    send_sems, recv_sems, token = res[0], res[1], res[-1]
    thru_vals = res[2:-1]

    def wait():
        def wait_body(*refs):
            src_refs, land_refs = refs[:ns], refs[ns:ns + nl]
            s_sems, r_sems = refs[ns + nl], refs[ns + nl + 1]
            for k, (src, _, dst, peer) in enumerate(plan(src_refs, land_refs)):
                cp = pltpu.make_async_remote_copy(src_ref=src, dst_ref=dst, send_sem=s_sems.at[k], recv_sem=r_sems.at[k],
                                                  device_id=peer, device_id_type=MESH)
                cp.wait_send()
                cp.wait_recv()

        out = pl.pallas_call(
            wait_body, name=name + "_wait", out_shape=thru,
            in_specs=[_HBM] * (ns + nl) + [_SEM, _SEM, _ANY], out_specs=[_HBM] * (ns + nl),
            input_output_aliases={i: i for i in range(ns + nl)},
            compiler_params=pltpu.CompilerParams(has_side_effects=_EFFECT),
        )(*thru_vals, send_sems, recv_sems, wait_after())
        return list(out[:ns]), list(out[ns:])

    return wait, token


def _gather_direct_plan(n):
    def plan(src_refs, land_refs):
        me = _position()
        peers = [_flip(me, k) for k in range(1, N_DEV)]
        return [(src_refs[a], land_refs[a].at[_index(me)], land_refs[a].at[_index(p)], p)
                for a in range(n) for p in peers]
    return plan


def _place_own(shards):
    me = _index(_position())
    return [lax.dynamic_update_slice(lax.empty((N_DEV,) + s.shape, s.dtype), s[None], (me, 0, 0)) for s in shards]


def _chip_index(pos):
    return 2 * pos[0] + pos[1]


def _exchange_d2d(name, parts):
    n = len(parts)

    def body(*refs):
        ins, outs = refs[:n], refs[n:2 * n]
        send_sems, recv_sems = refs[2 * n:]
        x, y, c = _position()
        sibling = (x, y, 1 - c)

        def copy(a, q, core):
            return pltpu.make_async_remote_copy(
                src_ref=ins[a].at[2 * q + core], dst_ref=outs[a].at[q], send_sem=send_sems.at[4 * a + q],
                recv_sem=recv_sems.at[4 * a + q], device_id=sibling, device_id_type=MESH)

        sent = [copy(a, q, 1 - c) for a in range(n) for q in range(4)]
        for cp in sent:
            cp.start()
        for cp in sent:
            cp.wait_recv()
        for cp in sent:
            cp.wait_send()

    return pl.pallas_call(
        body, out_shape=[jax.ShapeDtypeStruct((4,) + p.shape[1:], p.dtype) for p in parts],
        in_specs=[_ANY] * n, out_specs=[_ANY] * n,
        scratch_shapes=[pltpu.SemaphoreType.DMA((4 * n,)), pltpu.SemaphoreType.DMA((4 * n,))],
        name=name)(*parts)


def _pair_sum(name, part, got):
    _, R, C = part.shape
    T = _row_tile(R, 16)

    def body(p_ref, g_ref, s_out, l_out):
        c = lax.axis_index("c")
        mine = jnp.where(c == 0, p_ref[0, 0].astype(F32), p_ref[0, 1].astype(F32))
        s = (mine + g_ref[0].astype(F32)).astype(BF16)
        s_out[0] = s
        l_out[0] = s

    out = pl.BlockSpec((1, T, C), lambda q, i: (q, i, 0))
    return pl.pallas_call(
        body, grid=(4, R // T),
        in_specs=[pl.BlockSpec((1, 2, T, C), lambda q, i: (q, 0, i, 0)), out], out_specs=[out, out],
        out_shape=[jax.ShapeDtypeStruct((4, R, C), BF16)] * 2,
        compiler_params=_params("parallel", "parallel"), name=name)(part.reshape(4, 2, R, C), got)


def _exchange_ici_plan(n):
    def plan(src_refs, land_refs):
        me = _position()
        x, y, c = me
        peers = [(1 - x, y, c), (x, 1 - y, c), (1 - x, 1 - y, c)]
        return [(src_refs[a].at[_chip_index(p)], land_refs[a].at[_chip_index(me)], land_refs[a].at[_chip_index(p)], p)
                for a in range(n) for p in peers]
    return plan


def _adamw(w, g, m, v):
    m = ADAM_B1 * m + (1.0 - ADAM_B1) * g
    v = ADAM_B2 * v + (1.0 - ADAM_B2) * (g * g)
    m_hat = m / (1.0 - ADAM_B1 ** ADAM_STEP)
    v_hat = v / (1.0 - ADAM_B2 ** ADAM_STEP)
    delta = -ADAM_LR * (m_hat / (jnp.sqrt(v_hat) + ADAM_EPS) + ADAM_WD * w)
    return delta, m, v


def _row_tile(rows, unit, cap=256):
    best = rows
    for t in range(unit, min(rows, cap) + 1, unit):
        if rows % t == 0:
            best = t
    return best


def _sum_partials(name, recv):
    n_src, R, C = recv.shape
    T = _row_tile(R, 16)

    def body(r_ref, g_out):
        g = r_ref[0].astype(F32)
        for s in range(1, n_src):
            g = g + r_ref[s].astype(F32)
        g_out[...] = g

    return pl.pallas_call(
        body, grid=(R // T,), in_specs=[pl.BlockSpec((n_src, T, C), lambda i: (0, i, 0))],
        out_specs=pl.BlockSpec((T, C), lambda i: (i, 0)), out_shape=jax.ShapeDtypeStruct((R, C), F32),
        compiler_params=_params("parallel"), name=name)(recv)


def _adamw_update(name, w, g, m, v):
    R, C = w.shape
    T = _row_tile(R, 8)

    def body(w_ref, g_ref, m_ref, v_ref, d_out, m_out, v_out):
        delta, mn, vn = _adamw(w_ref[...], g_ref[...], m_ref[...], v_ref[...])
        d_out[...] = delta
        m_out[...] = mn
        v_out[...] = vn

    blk = pl.BlockSpec((T, C), lambda i: (i, 0))
    return pl.pallas_call(
        body, grid=(R // T,), in_specs=[blk] * 4, out_specs=[blk] * 3,
        out_shape=[jax.ShapeDtypeStruct((R, C), F32)] * 3, compiler_params=_params("parallel"), name=name)(w, g, m, v)


def _gather_vmem(p_ref, gath, send_sems, recv_sems):
    me = _position()
    gath[_index(me)] = p_ref[...]
    copies = []
    for k in range(1, N_DEV):
        peer = _flip(me, k)
        copies.append(pltpu.make_async_remote_copy(
            src_ref=p_ref, dst_ref=gath.at[_index(me)], send_sem=send_sems.at[k - 1],
            recv_sem=recv_sems.at[k - 1], device_id=peer, device_id_type=MESH))
    for cp in copies:
        cp.start()
    for k in range(1, N_DEV):
        peer = _flip(me, k)
        pltpu.make_async_remote_copy(
            src_ref=p_ref, dst_ref=gath.at[_index(peer)], send_sem=send_sems.at[k - 1],
            recv_sem=recv_sems.at[k - 1], device_id=peer, device_id_type=MESH).wait_recv()
    for cp in copies:
        cp.wait_send()


def _all_gather_small(p):
    R, C = p.shape

    def body(p_ref, out_ref, send_sems, recv_sems):
        _gather_vmem(p_ref, out_ref, send_sems, recv_sems)

    vm = pl.BlockSpec(memory_space=pltpu.VMEM)
    return pl.pallas_call(
        body, in_specs=[vm], out_specs=vm, out_shape=jax.ShapeDtypeStruct((N_DEV, R, C), F32),
        scratch_shapes=[pltpu.SemaphoreType.DMA((7,)), pltpu.SemaphoreType.DMA((7,))],
        name="all_gather_small")(p)


def _small_allreduce_adamw(part, w, m, v):
    R, C = part.shape

    def body(p_ref, w_ref, m_ref, v_ref, g_out, d_out, m_out, v_out, gath, send_sems, recv_sems):
        _gather_vmem(p_ref, gath, send_sems, recv_sems)
        g = gath[0]
        for s in range(1, N_DEV):
            g = g + gath[s]
        delta, mn, vn = _adamw(w_ref[...], g, m_ref[...], v_ref[...])
        g_out[...] = g
        d_out[...] = delta
        m_out[...] = mn
        v_out[...] = vn

    vm = pl.BlockSpec(memory_space=pltpu.VMEM)
    return pl.pallas_call(
        body, in_specs=[vm] * 4, out_specs=[vm] * 4, out_shape=[jax.ShapeDtypeStruct((R, C), F32)] * 4,
        scratch_shapes=[pltpu.VMEM((N_DEV, R, C), F32), pltpu.SemaphoreType.DMA((7,)), pltpu.SemaphoreType.DMA((7,))],
        name="small_allreduce_adamw")(part, w, m, v)


_BIG = ("w_in", "w_up", "w_down", "w_out", "w_pool_out", "w_attn_out")
_TRANSPOSED = ("w_in", "w_up", "w_pool_out", "w_attn_out")
_SMALL = (("g_mix", 2048), ("b_gate", 4096), ("pool_scale", 1024), ("g_ffn", 2048), ("conv_b", 11264), ("g_final", 2048))
CONV_ROWS = 8


def _to_comm(name, shard):
    return shard.T if name in _TRANSPOSED else shard


def _pack_small(vals):
    flat = [vals[n].reshape(-1) for n, _ in _SMALL]
    used = sum(f.shape[0] for f in flat)
    flat.append(jnp.zeros((SMALL_ROWS * PACK_COLS - used,), F32))
    return jnp.concatenate(flat).reshape(SMALL_ROWS, PACK_COLS)


def _unpack_small(packed, like):
    flat = packed.reshape(-1)
    out, off = {}, 0
    for name, n in _SMALL:
        out[name] = flat[off:off + n].reshape(like[name].shape)
        off += n
    return out


def _local_step(x, target, W, small, first_token=0.0, late_weights=None, emit=None):
    S = x.shape[0]
    o_qkv, o_gate = POOL_WIDTH, POOL_WIDTH + QKV_WIDTH
    if emit is None:
        emit = lambda grads: None

    def zero(token):
        return 0.0 if token is None else token[0, 0]
    w_in_t = W["w_in"]

    h1 = _rms_fwd("rms1_fwd", x, small["g_mix"] + first_token)
    u = _mm_nt("proj_u", h1, w_in_t, tm=1024, tn=512, tk=2048, o_dtype=F32, n_rows=POOL_WIDTH, b_row_off=0)
    qkv = _mm_nt("proj_qkv", h1, w_in_t, tm=1024, tn=512, tk=2048, o_dtype=BF16, n_rows=QKV_WIDTH, b_row_off=o_qkv)
    gates_pre = _mm_nt("proj_gates", h1, w_in_t, tm=1024, tn=512, tk=2048, o_dtype=F32, n_rows=2 * D_MODEL,
                       b_row_off=o_gate)
    pool_y = _pool_fwd(u, W["w_pool_lin"], small["pool_scale"])
    outs, lses = zip(*[_attn_fwd(gi, qkv) for gi in range(3)])
    attn_y = _merge_fwd(outs, lses)
    if late_weights is not None:
        W = dict(W, **late_weights(attn_y))
    w_up_t = W["w_up"]
    mixed, y_pool, y_attn = _mix_fwd(pool_y, attn_y, W["w_pool_out"], W["w_attn_out"], gates_pre, small["b_gate"])
    x1 = _mm_nn("out_proj", mixed, W["w_out"], tm=1024, tn=512, tk=2048, o_dtype=F32, res=x)

    h2 = _rms_fwd("rms2_fwd", x1, small["g_ffn"])
    nj = D_FF // 512
    up_pre = _mm("up_proj", h2, w_up_t, mode="nt", grid=(S // 1024, 2 * nj, 1),
                 a_spec=pl.BlockSpec((1024, D_MODEL), lambda i, j, k: (i, 0)),
                 b_spec=pl.BlockSpec((512, D_MODEL), lambda i, j, k: (j, 0)),
                 o_spec=pl.BlockSpec((None, 1024, 512), lambda i, j, k: (j // nj, i, j % nj)),
                 o_shape=(2, S, D_FF), o_dtype=BF16, acc_shape=(1024, 512))
    act = _convffn_fwd(up_pre, small["conv_w_full"], small["conv_b"])
    x2 = _mm_nn("down_proj", act, W["w_down"], tm=1024, tn=512, tk=D_FF // 2, o_dtype=F32, res=x1)

    dx2, dx2_b, dg_final, sq_cols = _final_loss("final_loss", x2, small["g_final"], target)

    dact = _mm_nt("d_act", dx2_b, W["w_down"], tm=1024, tn=512, tk=2048, o_dtype=BF16)
    dw_down = _mm_tn("dw_down", act, dx2_b, tm=512, tn=2048, tk=512)
    tok = emit(dict(w_down=dw_down))
    dup, dconv_w, dconv_b = _convffn_bwd_a(up_pre, dact, small["conv_w_full"], small["conv_b"] + zero(tok))
    dup_pre = _convffn_bwd_b(dup, small["conv_w_full"])
    tkh = D_FF // 2
    dh2 = _mm("d_h2", dup_pre, w_up_t, mode="nn", grid=(S // 1024, D_MODEL // 1024, 4),
              a_spec=pl.BlockSpec((None, 1024, tkh), lambda i, j, k: (k // 2, i, k % 2)),
              b_spec=pl.BlockSpec((tkh, 1024), lambda i, j, k: (k, j)),
              o_spec=pl.BlockSpec((1024, 1024), lambda i, j, k: (i, j)),
              o_shape=(S, D_MODEL), o_dtype=F32, acc_shape=(1024, 1024))
    dw_up = _mm("dw_up", dup_pre, h2, mode="tn", grid=(2 * nj, 1, S // 512),
                a_spec=pl.BlockSpec((None, 512, 512), lambda i, j, k: (i // nj, k, i % nj)),
                b_spec=pl.BlockSpec((512, D_MODEL), lambda i, j, k: (k, 0)),
                o_spec=pl.BlockSpec((512, D_MODEL), lambda i, j, k: (i, 0)),
                o_shape=(2 * D_FF, D_MODEL), o_dtype=BF16, acc_shape=(512, D_MODEL))
    tok = emit(dict(w_up=dw_up, conv_w=dconv_w))
    dx1, dx1_b, dg_ffn = _rms_bwd("rms2_bwd", x1, small["g_ffn"] + zero(tok), dh2, dx2)

    dmixed = _mm_nt("d_mixed", dx1_b, W["w_out"], tm=1024, tn=512, tk=2048, o_dtype=BF16)
    dw_out = _mm_tn("dw_out", mixed, dx1_b, tm=1024, tn=2048, tk=512)
    tok = emit(dict(w_out=dw_out))
    dy_pool, dy_attn, dgates, db_gate = _mix_bwd(dmixed, gates_pre, small["b_gate"] + zero(tok), y_pool, y_attn)
    dpool_y = _mm_nn("d_pool_y", dy_pool, W["w_pool_out"], tm=1024, tn=512, tk=2048, o_dtype=F32)
    dw_pool_out = _mm_tn("dw_pool_out", dy_pool, pool_y, tm=1024, tn=1024, tk=512)
    dattn_y = _mm_nn("d_attn_y", dy_attn, W["w_attn_out"], tm=1024, tn=512, tk=2048, o_dtype=F32)
    dw_attn_out = _mm_tn("dw_attn_out", dy_attn, attn_y, tm=1024, tn=512, tk=512)
    e, dw_pool_lin, dpool_scale = _pool_bwd_a(u, dpool_y, W["w_pool_lin"], small["pool_scale"])
    du = _pool_bwd_b(e)
    dos, dps = _merge_bwd(dattn_y, outs, lses)
    dqkv = [_attn_bwd(gi, qkv, dos[gi], lses[gi], dps[gi]) for gi in range(3)]
    dproj = jnp.concatenate([du] + [dqkv[gi][sec] for sec in range(3) for gi in range(3)] + [dgates], axis=1)
    dw_in = _mm_tn("dw_in", dproj, h1, tm=512, tn=2048, tk=512)
    tok = emit(dict(w_in=dw_in, w_pool_out=dw_pool_out, w_attn_out=dw_attn_out, w_pool_lin=dw_pool_lin))
    dh1 = _mm_nn("d_h1", dproj, w_in_t, tm=1024, tn=1024, tk=IN_WIDTH // 4, o_dtype=F32,
                 after=tok)
    grad_x, _, dg_mix = _rms_bwd("rms1_bwd", x, small["g_mix"] + zero(tok), dh1, dx1)

    grads = dict(w_in=dw_in, w_up=dw_up, w_down=dw_down, w_out=dw_out, w_pool_out=dw_pool_out,
                 w_attn_out=dw_attn_out, w_pool_lin=dw_pool_lin, conv_w=dconv_w)
    small_grads = dict(g_mix=dg_mix, b_gate=db_gate, pool_scale=dpool_scale, g_ffn=dg_ffn, conv_b=dconv_b,
                       g_final=dg_final)
    return sq_cols, grad_x, grads, small_grads


def kernel(x, g_mix, w_in, b_gate, w_pool_lin, pool_scale, w_pool_out, w_attn_out, w_out, g_ffn, w_up, conv_w, conv_b, w_down, g_final, loss_target, m_g_mix, m_w_in, m_b_gate, m_w_pool_lin, m_pool_scale, m_w_pool_out, m_w_attn_out, m_w_out, m_g_ffn, m_w_up, m_conv_w, m_conv_b, m_w_down, m_g_final, v_g_mix, v_w_in, v_b_gate, v_w_pool_lin, v_pool_scale, v_w_pool_out, v_w_attn_out, v_w_out, v_g_ffn, v_w_up, v_conv_w, v_conv_b, v_w_down, v_g_final):
    given = dict(g_mix=g_mix, w_in=w_in, b_gate=b_gate, w_pool_lin=w_pool_lin, pool_scale=pool_scale,
                 w_pool_out=w_pool_out, w_attn_out=w_attn_out, w_out=w_out, g_ffn=g_ffn, w_up=w_up, conv_w=conv_w,
                 conv_b=conv_b, w_down=w_down, g_final=g_final)
    mom_m = dict(g_mix=m_g_mix, w_in=m_w_in, b_gate=m_b_gate, w_pool_lin=m_w_pool_lin, pool_scale=m_pool_scale,
                 w_pool_out=m_w_pool_out, w_attn_out=m_w_attn_out, w_out=m_w_out, g_ffn=m_g_ffn, w_up=m_w_up,
                 conv_w=m_conv_w, conv_b=m_conv_b, w_down=m_w_down, g_final=m_g_final)
    mom_v = dict(g_mix=v_g_mix, w_in=v_w_in, b_gate=v_b_gate, w_pool_lin=v_w_pool_lin, pool_scale=v_pool_scale,
                 w_pool_out=v_w_pool_out, w_attn_out=v_w_attn_out, w_out=v_w_out, g_ffn=v_g_ffn, w_up=v_w_up,
                 conv_w=v_conv_w, conv_b=v_conv_b, w_down=v_w_down, g_final=v_g_final)
    def full(g):
        return g.reshape(N_DEV * g.shape[1], g.shape[2])

    shard = {n: _to_comm(n, given[n][0]).astype(BF16) for n in _BIG}
    W = dict(w_in=full(_all_gather_weights([shard["w_in"]])[0]))
    rest = [n for n in _BIG if n != "w_in"]
    rest_wait, rest_token = _split_copies(
        "gather_rest", [shard[n] for n in rest], _place_own([shard[n] for n in rest]), 7 * len(rest),
        _gather_direct_plan(len(rest)), start_after=W["w_in"], wait_after=lambda: rest_after[0])
    rest_after = []

    def late_weights(after):
        rest_after.append(after)
        return {n: full(l) for n, l in zip(rest, rest_wait()[1])}

    lin_all = _all_gather_small(w_pool_lin[0].reshape(4 * 32, POOL_GROUP_WIDTH))
    W["w_pool_lin"] = jnp.moveaxis(lin_all.reshape(N_DEV, 4, 32, POOL_GROUP_WIDTH), 0, 1).reshape(
        4, POOL_GROUP_WIDTH, POOL_GROUP_WIDTH).astype(BF16)
    conv_w_all = _all_gather_small(jnp.pad(conv_w[0], ((0, CONV_ROWS - 3), (0, 0))))
    conv_w_full = jnp.moveaxis(conv_w_all[:, :3, :], 0, 1).reshape(3, 2 * D_FF)
    small = dict(g_mix=g_mix, b_gate=b_gate, pool_scale=pool_scale, g_ffn=g_ffn, conv_b=conv_b,
                 g_final=g_final.reshape(1, D_MODEL), conv_w_full=conv_w_full)

    pending = []
    done = {}

    def emit(group):
        names = list(group)
        parts = []
        for n in names:
            g = group[n]
            if n == "w_pool_lin":
                g = jnp.moveaxis(g.reshape(4, N_DEV, 32, POOL_GROUP_WIDTH), 1, 0).reshape(
                    N_DEV, 4 * 32, POOL_GROUP_WIDTH).astype(BF16)
            elif n == "conv_w":
                g = jnp.pad(jnp.moveaxis(g.reshape(3, N_DEV, 2 * D_FF // N_DEV), 1, 0),
                            ((0, 0), (0, CONV_ROWS - 3), (0, 0))).astype(BF16)
            else:
                g = g.reshape(N_DEV, g.shape[0] // N_DEV, g.shape[1])
            parts.append(g)
        tag = names[0]
        got = _exchange_d2d("exchange_d2d_" + tag, parts)
        sums, lands = zip(*[_pair_sum("pair_sum_" + n, p, r) for n, p, r in zip(names, parts, got)])
        wait, token = _split_copies("exchange_ici_" + tag, sums, lands, 3 * len(names), _exchange_ici_plan(len(names)),
                                    start_after=got[0], wait_after=lambda: done["grad_x"])
        pending.append((names, wait))
        return token

    sq_cols, grad_x, _, small_grads = _local_step(x[0], loss_target[0], W, small, first_token=rest_token[0, 0],
                                                  late_weights=late_weights, emit=emit)
    done["grad_x"] = grad_x
    loss = lax.psum(0.5 / D_MODEL * jnp.sum(sq_cols), AXES)

    res = {}
    for n, r in [(n, r) for names, wait in pending for n, r in zip(names, wait()[1])]:
        g = _sum_partials("sum_" + n, r)
        if n in _TRANSPOSED:
            g = g.T
        elif n == "conv_w":
            g = g[:3]
        shape = given[n].shape
        rows = math.prod(shape[:-1])
        w2, m2, v2 = (d[n].reshape(rows, shape[-1]) for d in (given, mom_m, mom_v))
        upd = _adamw_update("adamw_" + n, w2, g.reshape(rows, shape[-1]), m2, v2)
        res[n] = [a.reshape(shape) for a in (g,) + tuple(upd)]

    packed_small = [_pack_small(d) for d in (given, mom_m, mom_v)]
    res_small = [_unpack_small(p, given) for p in _small_allreduce_adamw(_pack_small(small_grads), *packed_small)]

    order = ["g_mix", "w_in", "b_gate", "w_pool_lin", "pool_scale", "w_pool_out", "w_attn_out", "w_out", "g_ffn",
             "w_up", "conv_w", "conv_b", "w_down", "g_final"]
    outs = [loss, grad_x[None]]
    for kind in range(4):
        for n in order:
            outs.append(res[n][kind] if n in res else res_small[kind][n])
    return tuple(outs)
```

```python
import math

import jax
import jax.numpy as jnp
from jax import lax
from jax.experimental import pallas as pl
from jax.experimental.pallas import tpu as pltpu

F32 = jnp.float32
BF16 = jnp.bfloat16
MESH = pl.DeviceIdType.MESH
AXES = ("x", "y", "c")
N_DEV = 8

D_MODEL = 2048
POOL_WINDOWS = (2, 4, 8, 16)
POOL_GROUP_WIDTH = 256
POOL_WIDTH = 1024
ATTN_GROUPS = ((128, 1), (512, 4), (2048, 16))
SPAN = 128
HEADS_PER_GROUP = 4
N_ATTN_HEADS = 12
HEAD_DIM = 128
ATTN_WIDTH = 1536
QKV_WIDTH = 3 * ATTN_WIDTH
ATTN_OUT_WIDTH = 512
IN_WIDTH = 9728
D_FF = 5632
RMS_EPS = 1e-6
ADAM_LR, ADAM_B1, ADAM_B2, ADAM_EPS, ADAM_WD, ADAM_STEP = 0.001, 0.9, 0.999, 1e-08, 0.01, 10

VMEM_LIMIT_BYTES = 48 * 1024 * 1024
HALO = 16
TOKEN_TK = 2048
PACK_COLS = 1024
SMALL_ROWS = 24
NEG_BIG = -1e30

_DN = {"nn": (((1,), (0,)), ((), ())), "nt": (((1,), (1,)), ((), ())), "tn": (((0,), (0,)), ((), ()))}


def _params(*sem):
    return pltpu.CompilerParams(dimension_semantics=sem, vmem_limit_bytes=VMEM_LIMIT_BYTES)


def _dot(a, b, mode):
    return lax.dot_general(a.astype(BF16), b.astype(BF16), _DN[mode], preferred_element_type=F32)


def _mm(name, a, b, *, mode, grid, a_spec, b_spec, o_spec, o_shape, o_dtype, acc_shape, res=None, res_spec=None,
        after=None):
    nk = grid[2]
    n_in = 2 + (res is not None) + (after is not None)

    def body(*refs):
        a_ref, b_ref = refs[:2]
        r_ref = refs[2] if res is not None else None
        o_ref = refs[n_in]
        scr = refs[n_in + 1:]
        prod = _dot(a_ref[...], b_ref[...], mode)

        def finish(val):
            if r_ref is not None:
                val = val + r_ref[...]
            o_ref[...] = val.astype(o_ref.dtype)

        if nk == 1:
            finish(prod)
        else:
            acc = scr[0]
            k = pl.program_id(2)

            @pl.when(k == 0)
            def _():
                acc[...] = prod

            @pl.when(k > 0)
            def _():
                acc[...] += prod

            @pl.when(k == nk - 1)
            def _():
                finish(acc[...])

    in_specs = [a_spec, b_spec]
    args = [a, b]
    if res is not None:
        in_specs.append(res_spec)
        args.append(res)
    if after is not None:
        in_specs.append(pl.BlockSpec((8, 128), lambda i, j, k: (0, 0)))
        args.append(after)
    return pl.pallas_call(
        body, grid=grid, in_specs=in_specs, out_specs=o_spec,
        out_shape=jax.ShapeDtypeStruct(o_shape, o_dtype),
        scratch_shapes=[] if nk == 1 else [pltpu.VMEM(acc_shape, F32)],
        compiler_params=_params("parallel", "parallel", "arbitrary"), name=name)(*args)


def _mm_nn(name, a, b, *, tm, tn, tk, o_dtype, n_cols=None, b_col_off=0, res=None, after=None):
    M, K = a.shape
    N = b.shape[1] if n_cols is None else n_cols
    off = b_col_off // tn
    return _mm(name, a, b, mode="nn", grid=(M // tm, N // tn, K // tk),
               a_spec=pl.BlockSpec((tm, tk), lambda i, j, k: (i, k)),
               b_spec=pl.BlockSpec((tk, tn), lambda i, j, k: (k, j + off)),
               o_spec=pl.BlockSpec((tm, tn), lambda i, j, k: (i, j)),
               o_shape=(M, N), o_dtype=o_dtype, acc_shape=(tm, tn), res=res,
               res_spec=pl.BlockSpec((tm, tn), lambda i, j, k: (i, j)), after=after)


def _mm_nt(name, a, b, *, tm, tn, tk, o_dtype, n_rows=None, b_row_off=0):
    M, K = a.shape
    N = b.shape[0] if n_rows is None else n_rows
    off = b_row_off // tn
    return _mm(name, a, b, mode="nt", grid=(M // tm, N // tn, K // tk),
               a_spec=pl.BlockSpec((tm, tk), lambda i, j, k: (i, k)),
               b_spec=pl.BlockSpec((tn, tk), lambda i, j, k: (j + off, k)),
               o_spec=pl.BlockSpec((tm, tn), lambda i, j, k: (i, j)),
               o_shape=(M, N), o_dtype=o_dtype, acc_shape=(tm, tn))


def _mm_tn(name, a, b, *, tm, tn, tk):
    K, M = a.shape
    N = b.shape[1]
    return _mm(name, a, b, mode="tn", grid=(M // tm, N // tn, K // tk),
               a_spec=pl.BlockSpec((tk, tm), lambda i, j, k: (k, i)),
               b_spec=pl.BlockSpec((tk, tn), lambda i, j, k: (k, j)),
               o_spec=pl.BlockSpec((tm, tn), lambda i, j, k: (i, j)),
               o_shape=(M, N), o_dtype=BF16, acc_shape=(tm, tn))


def _rms_fwd(name, x, g):
    S, D = x.shape
    T = 512

    def body(x_ref, g_ref, h_ref):
        xv = x_ref[...]
        r = lax.rsqrt(jnp.mean(xv * xv, axis=-1, keepdims=True) + RMS_EPS)
        h_ref[...] = (xv * r * g_ref[...]).astype(BF16)

    return pl.pallas_call(
        body, grid=(S // T,),
        in_specs=[pl.BlockSpec((T, D), lambda i: (i, 0)), pl.BlockSpec((1, D), lambda i: (0, 0))],
        out_specs=pl.BlockSpec((T, D), lambda i: (i, 0)),
        out_shape=jax.ShapeDtypeStruct((S, D), BF16), compiler_params=_params("parallel"), name=name)(x, g)


def _rms_bwd(name, x, g, dh, dres):
    S, D = x.shape
    T = 256

    def body(x_ref, g_ref, dh_ref, dres_ref, dx_ref, dxb_ref, dg_ref):
        i = pl.program_id(0)
        xv = x_ref[...]
        r = lax.rsqrt(jnp.mean(xv * xv, axis=-1, keepdims=True) + RMS_EPS)
        xhat = xv * r
        dhv = dh_ref[...].astype(F32)
        gdh = dhv * g_ref[...]
        c = jnp.mean(xhat * gdh, axis=-1, keepdims=True)
        dx = dres_ref[...] + r * (gdh - xhat * c)
        dx_ref[...] = dx
        dxb_ref[...] = dx.astype(BF16)
        part = jnp.sum(dhv * xhat, axis=0, keepdims=True)

        @pl.when(i == 0)
        def _():
            dg_ref[...] = part

        @pl.when(i > 0)
        def _():
            dg_ref[...] += part

    row = pl.BlockSpec((T, D), lambda i: (i, 0))
    vec = pl.BlockSpec((1, D), lambda i: (0, 0))
    return pl.pallas_call(
        body, grid=(S // T,), in_specs=[row, vec, row, row], out_specs=[row, row, vec],
        out_shape=[jax.ShapeDtypeStruct((S, D), F32), jax.ShapeDtypeStruct((S, D), BF16),
                   jax.ShapeDtypeStruct((1, D), F32)],
        compiler_params=_params("arbitrary"), name=name)(x, g, dh, dres)


def _final_loss(name, x, g, target):
    S, D = x.shape
    T = 256

    def body(x_ref, g_ref, t_ref, dx_ref, dxb_ref, dg_ref, sq_ref):
        i = pl.program_id(0)
        xv = x_ref[...]
        r = lax.rsqrt(jnp.mean(xv * xv, axis=-1, keepdims=True) + RMS_EPS)
        xhat = xv * r
        err = xhat * g_ref[...] - t_ref[...]
        dy = err * (1.0 / D)
        gdy = dy * g_ref[...]
        c = jnp.mean(xhat * gdy, axis=-1, keepdims=True)
        dx = r * (gdy - xhat * c)
        dx_ref[...] = dx
        dxb_ref[...] = dx.astype(BF16)
        dg_part = jnp.sum(dy * xhat, axis=0, keepdims=True)
        sq_part = jnp.sum(err * err, axis=0, keepdims=True)

        @pl.when(i == 0)
        def _():
            dg_ref[...] = dg_part
            sq_ref[...] = sq_part

        @pl.when(i > 0)
        def _():
            dg_ref[...] += dg_part
            sq_ref[...] += sq_part

    row = pl.BlockSpec((T, D), lambda i: (i, 0))
    vec = pl.BlockSpec((1, D), lambda i: (0, 0))
    return pl.pallas_call(
        body, grid=(S // T,), in_specs=[row, vec, row], out_specs=[row, row, vec, vec],
        out_shape=[jax.ShapeDtypeStruct((S, D), F32), jax.ShapeDtypeStruct((S, D), BF16),
                   jax.ShapeDtypeStruct((1, D), F32), jax.ShapeDtypeStruct((1, D), F32)],
        compiler_params=_params("arbitrary"), name=name)(x, g, target)


def _prev_halo_spec(T, C, col_map):
    per = T // HALO
    return pl.BlockSpec((HALO, C), lambda *g: (jnp.maximum(g[0] * per - 1, 0), col_map(*g)))


def _pool_window_sums(ext, T, i):
    t = i * T + lax.broadcasted_iota(jnp.int32, (T, 1), 0)
    outs = []
    cnts = []
    for gi, w in enumerate(POOL_WINDOWS):
        s = ext[:, gi * POOL_GROUP_WIDTH:(gi + 1) * POOL_GROUP_WIDTH]
        sh = 1
        while sh < w:
            s = s + pltpu.roll(s, sh, 0)
            sh *= 2
        cnt = jnp.minimum(t + 1, w).astype(F32)
        outs.append(s[HALO:] / cnt - ext[HALO:, gi * POOL_GROUP_WIDTH:(gi + 1) * POOL_GROUP_WIDTH])
        cnts.append(cnt)
    return outs, cnts


def _load_ext(ext_ref, halo_ref, blk_ref, first):
    ext_ref[0:HALO, :] = jnp.where(first, 0.0, halo_ref[...].astype(F32))
    ext_ref[HALO:, :] = blk_ref[...].astype(F32)


def _pool_fwd(u, w_lin, scale):
    S = u.shape[0]
    T = 512
    W = POOL_WIDTH

    def body(u_ref, halo_ref, wl_ref, sc_ref, y_ref, ext_ref):
        i = pl.program_id(0)
        _load_ext(ext_ref, halo_ref, u_ref, i == 0)
        pooled, _ = _pool_window_sums(ext_ref[...], T, i)
        for gi in range(4):
            z = _dot(pooled[gi], wl_ref[gi], "nn")
            cs = slice(gi * POOL_GROUP_WIDTH, (gi + 1) * POOL_GROUP_WIDTH)
            y_ref[:, cs] = (z * sc_ref[:, cs]).astype(BF16)

    return pl.pallas_call(
        body, grid=(S // T,),
        in_specs=[pl.BlockSpec((T, W), lambda i: (i, 0)), _prev_halo_spec(T, W, lambda i: 0),
                  pl.BlockSpec((4, 256, 256), lambda i: (0, 0, 0)), pl.BlockSpec((1, W), lambda i: (0, 0))],
        out_specs=pl.BlockSpec((T, W), lambda i: (i, 0)),
        out_shape=jax.ShapeDtypeStruct((S, W), BF16),
        scratch_shapes=[pltpu.VMEM((HALO + T, W), F32)],
        compiler_params=_params("parallel"), name="pool_fwd")(u, u, w_lin, scale)


def _pool_bwd_a(u, dy, w_lin, scale):
    S = u.shape[0]
    T = 512
    W = POOL_WIDTH

    def body(u_ref, halo_ref, dy_ref, wl_ref, sc_ref, e_ref, dwl_ref, dsc_ref, ext_ref):
        i = pl.program_id(0)
        _load_ext(ext_ref, halo_ref, u_ref, i == 0)
        pooled, cnts = _pool_window_sums(ext_ref[...], T, i)
        for gi in range(4):
            cs = slice(gi * POOL_GROUP_WIDTH, (gi + 1) * POOL_GROUP_WIDTH)
            z = _dot(pooled[gi], wl_ref[gi], "nn")
            dyg = dy_ref[:, cs]
            dsc = jnp.sum(dyg * z, axis=0, keepdims=True)
            dz = dyg * sc_ref[:, cs]
            dwl = _dot(pooled[gi], dz, "tn")
            dpooled = _dot(dz, wl_ref[gi], "nt")
            e_ref[:, cs] = dpooled / cnts[gi]

            @pl.when(i == 0)
            def _():
                dwl_ref[gi] = dwl
                dsc_ref[:, cs] = dsc

            @pl.when(i > 0)
            def _():
                dwl_ref[gi] += dwl
                dsc_ref[:, cs] += dsc

    return pl.pallas_call(
        body, grid=(S // T,),
        in_specs=[pl.BlockSpec((T, W), lambda i: (i, 0)), _prev_halo_spec(T, W, lambda i: 0),
                  pl.BlockSpec((T, W), lambda i: (i, 0)),
                  pl.BlockSpec((4, 256, 256), lambda i: (0, 0, 0)), pl.BlockSpec((1, W), lambda i: (0, 0))],
        out_specs=[pl.BlockSpec((T, W), lambda i: (i, 0)), pl.BlockSpec((4, 256, 256), lambda i: (0, 0, 0)),
                   pl.BlockSpec((1, W), lambda i: (0, 0))],
        out_shape=[jax.ShapeDtypeStruct((S, W), F32), jax.ShapeDtypeStruct((4, 256, 256), F32),
                   jax.ShapeDtypeStruct((1, W), F32)],
        scratch_shapes=[pltpu.VMEM((HALO + T, W), F32)],
        compiler_params=_params("arbitrary"), name="pool_bwd_a")(u, u, dy, w_lin, scale)


def _pool_bwd_b(e):
    S = e.shape[0]
    T = 512
    W = POOL_WIDTH
    per = T // HALO
    n_blk = S // T

    def body(e_ref, halo_ref, du_ref, ext_ref):
        i = pl.program_id(0)
        ext_ref[0:T, :] = e_ref[...]
        ext_ref[T:, :] = jnp.where(i == n_blk - 1, 0.0, halo_ref[...])
        ext = ext_ref[...]
        n = T + HALO
        t = i * T + lax.broadcasted_iota(jnp.int32, (T, 1), 0)
        for gi, w in enumerate(POOL_WINDOWS):
            cs = slice(gi * POOL_GROUP_WIDTH, (gi + 1) * POOL_GROUP_WIDTH)
            s = ext[:, cs]
            sh = 1
            while sh < w:
                s = s + pltpu.roll(s, n - sh, 0)
                sh *= 2
            cnt = jnp.minimum(t + 1, w).astype(F32)
            du_ref[:, cs] = (s[0:T] - ext[0:T, cs] * cnt).astype(BF16)

    return pl.pallas_call(
        body, grid=(n_blk,),
        in_specs=[pl.BlockSpec((T, W), lambda i: (i, 0)),
                  pl.BlockSpec((HALO, W), lambda i: (jnp.minimum((i + 1) * per, S // HALO - 1), 0))],
        out_specs=pl.BlockSpec((T, W), lambda i: (i, 0)),
        out_shape=jax.ShapeDtypeStruct((S, W), BF16),
        scratch_shapes=[pltpu.VMEM((T + HALO, W), F32)],
        compiler_params=_params("parallel"), name="pool_bwd_b")(e, e)


def _slope(head):
    return 2.0 ** (-8.0 * (head + 1) / N_ATTN_HEADS)


def _attn_masks(b):
    row = lax.broadcasted_iota(jnp.int32, (SPAN, SPAN), 0)
    col = lax.broadcasted_iota(jnp.int32, (SPAN, SPAN), 1)
    dist_cur = (row - col).astype(F32)
    valid_cur = col <= row
    valid_prev = jnp.logical_and(col >= row, b > 0)
    return dist_cur, valid_cur, valid_prev


def _attn_fwd(gi, qkv):
    S = qkv.shape[0]
    d = ATTN_GROUPS[gi][1]
    L = S // d
    nb = L // SPAN
    view = qkv.reshape(L, d * QKV_WIDTH)
    W = ATTN_OUT_WIDTH
    ncb = QKV_WIDTH // W
    scale = HEAD_DIM ** -0.5

    def body(q_ref, kc_ref, kp_ref, vc_ref, vp_ref, o_ref, lse_ref):
        b = pl.program_id(1)
        dist_cur, valid_cur, valid_prev = _attn_masks(b)
        for hi in range(HEADS_PER_GROUP):
            hs = slice(hi * HEAD_DIM, (hi + 1) * HEAD_DIM)
            sl = _slope(gi * HEADS_PER_GROUP + hi) * d
            q = q_ref[:, hs]
            s_cur = _dot(q, kc_ref[:, hs], "nt") * scale + (-sl) * dist_cur
            s_prev = _dot(q, kp_ref[:, hs], "nt") * scale + (-sl) * (dist_cur + SPAN)
            s_cur = jnp.where(valid_cur, s_cur, NEG_BIG)
            s_prev = jnp.where(valid_prev, s_prev, NEG_BIG)
            m = jnp.maximum(jnp.max(s_cur, axis=-1, keepdims=True), jnp.max(s_prev, axis=-1, keepdims=True))
            p_cur = jnp.exp(s_cur - m)
            p_prev = jnp.exp(s_prev - m)
            l = jnp.sum(p_cur, axis=-1, keepdims=True) + jnp.sum(p_prev, axis=-1, keepdims=True)
            o = (_dot(p_cur, vc_ref[:, hs], "nn") + _dot(p_prev, vp_ref[:, hs], "nn")) / l
            o_ref[:, hs] = o
            lse_ref[:, hs] = jnp.broadcast_to(m + jnp.log(l), (SPAN, HEAD_DIM))

    def cur(sec):
        return pl.BlockSpec((SPAN, W), lambda r, b: (b, r * ncb + 3 * sec + gi))

    def prev(sec):
        return pl.BlockSpec((SPAN, W), lambda r, b: (jnp.maximum(b - 1, 0), r * ncb + 3 * sec + gi))

    out = pl.BlockSpec((SPAN, W), lambda r, b: (b, r))
    o, lse = pl.pallas_call(
        body, grid=(d, nb), in_specs=[cur(0), cur(1), prev(1), cur(2), prev(2)], out_specs=[out, out],
        out_shape=[jax.ShapeDtypeStruct((L, d * W), F32), jax.ShapeDtypeStruct((L, d * W), F32)],
        compiler_params=_params("parallel", "parallel"), name=f"attn_fwd_g{gi}")(view, view, view, view, view)
    return o.reshape(S, W), lse.reshape(S, W)


def _attn_bwd(gi, qkv, do, lse, dp):
    S = qkv.shape[0]
    d = ATTN_GROUPS[gi][1]
    L = S // d
    nb = L // SPAN
    W = ATTN_OUT_WIDTH
    ncb = QKV_WIDTH // W
    scale = HEAD_DIM ** -0.5
    view = qkv.reshape(L, d * QKV_WIDTH)
    do_v, lse_v, dp_v = (a.reshape(L, d * W) for a in (do, lse, dp))

    def body(q_ref, kc_ref, vc_ref, kp_ref, vp_ref, qn_ref, do_ref, don_ref, lse_ref, lsen_ref, dp_ref, dpn_ref,
             dq_ref, dk_ref, dv_ref):
        b = pl.program_id(1)
        dist_cur, valid_cur, valid_prev = _attn_masks(b)
        row = lax.broadcasted_iota(jnp.int32, (SPAN, SPAN), 0)
        col = lax.broadcasted_iota(jnp.int32, (SPAN, SPAN), 1)
        valid_next = jnp.logical_and(col >= row, b < nb - 1)
        for hi in range(HEADS_PER_GROUP):
            hs = slice(hi * HEAD_DIM, (hi + 1) * HEAD_DIM)
            sl = _slope(gi * HEADS_PER_GROUP + hi) * d
            q, kc, vc, kp, vp, qn = (r[:, hs] for r in (q_ref, kc_ref, vc_ref, kp_ref, vp_ref, qn_ref))
            dov, don = do_ref[:, hs], don_ref[:, hs]
            lse_b, lse_n, dp_b, dp_n = lse_ref[:, hs], lsen_ref[:, hs], dp_ref[:, hs], dpn_ref[:, hs]
            bias_cur = (-sl) * dist_cur
            bias_far = (-sl) * (dist_cur + SPAN)
            p_cur = jnp.where(valid_cur, jnp.exp(_dot(q, kc, "nt") * scale + bias_cur - lse_b), 0.0)
            p_prev = jnp.where(valid_prev, jnp.exp(_dot(q, kp, "nt") * scale + bias_far - lse_b), 0.0)
            ds_cur = p_cur * (_dot(dov, vc, "nt") - dp_b)
            ds_prev = p_prev * (_dot(dov, vp, "nt") - dp_b)
            dq_ref[:, hs] = ((_dot(ds_cur, kc, "nn") + _dot(ds_prev, kp, "nn")) * scale).astype(BF16)
            p_next = jnp.where(valid_next, jnp.exp(_dot(qn, kc, "nt") * scale + bias_far - lse_n), 0.0)
            ds_next = p_next * (_dot(don, vc, "nt") - dp_n)
            dv_ref[:, hs] = (_dot(p_cur, dov, "tn") + _dot(p_next, don, "tn")).astype(BF16)
            dk_ref[:, hs] = ((_dot(ds_cur, q, "tn") + _dot(ds_next, qn, "tn")) * scale).astype(BF16)

    def qkv_spec(sec, shift):
        def imap(r, b):
            return (jnp.clip(b + shift, 0, nb - 1), r * ncb + 3 * sec + gi)
        return pl.BlockSpec((SPAN, W), imap)

    def act_spec(shift):
        return pl.BlockSpec((SPAN, W), lambda r, b: (jnp.clip(b + shift, 0, nb - 1), r))

    out = act_spec(0)
    dq, dk, dv = pl.pallas_call(
        body, grid=(d, nb),
        in_specs=[qkv_spec(0, 0), qkv_spec(1, 0), qkv_spec(2, 0), qkv_spec(1, -1), qkv_spec(2, -1), qkv_spec(0, 1),
                  act_spec(0), act_spec(1), act_spec(0), act_spec(1), act_spec(0), act_spec(1)],
        out_specs=[out, out, out],
        out_shape=[jax.ShapeDtypeStruct((L, d * W), BF16)] * 3,
        compiler_params=_params("parallel", "parallel"), name=f"attn_bwd_g{gi}")(
            view, view, view, view, view, view, do_v, do_v, lse_v, lse_v, dp_v, dp_v)
    return dq.reshape(S, W), dk.reshape(S, W), dv.reshape(S, W)


def _group_weights(l_refs):
    l0, l1, l2 = (r[...] for r in l_refs)
    m = jnp.maximum(jnp.maximum(l0, l1), l2)
    e = [jnp.exp(l0 - m), jnp.exp(l1 - m), jnp.exp(l2 - m)]
    tot = e[0] + e[1] + e[2]
    return [ei / tot for ei in e]


def _merge_fwd(outs, lses):
    S, W = outs[0].shape
    T = 512

    def body(o0, o1, o2, l0, l1, l2, y_ref):
        w = _group_weights((l0, l1, l2))
        y_ref[...] = (w[0] * o0[...] + w[1] * o1[...] + w[2] * o2[...]).astype(BF16)

    blk = pl.BlockSpec((T, W), lambda i: (i, 0))
    return pl.pallas_call(
        body, grid=(S // T,), in_specs=[blk] * 6, out_specs=blk, out_shape=jax.ShapeDtypeStruct((S, W), BF16),
        compiler_params=_params("parallel"), name="merge_fwd")(*outs, *lses)


def _merge_bwd(dy, outs, lses):
    S, W = outs[0].shape
    T = 512

    def body(dy_ref, o0, o1, o2, l0, l1, l2, do0, do1, do2, dp0, dp1, dp2):
        w = _group_weights((l0, l1, l2))
        dyv = dy_ref[...]
        y = w[0] * o0[...] + w[1] * o1[...] + w[2] * o2[...]
        prod = dyv * y
        dots = []
        for hi in range(HEADS_PER_GROUP):
            hs = slice(hi * HEAD_DIM, (hi + 1) * HEAD_DIM)
            dots.append(jnp.broadcast_to(jnp.sum(prod[:, hs], axis=-1, keepdims=True), (T, HEAD_DIM)))
        dot_b = jnp.concatenate(dots, axis=1)
        for wg, do_ref, dp_ref in zip(w, (do0, do1, do2), (dp0, dp1, dp2)):
            do_ref[...] = (wg * dyv).astype(BF16)
            dp_ref[...] = wg * dot_b

    blk = pl.BlockSpec((T, W), lambda i: (i, 0))
    res = pl.pallas_call(
        body, grid=(S // T,), in_specs=[blk] * 7, out_specs=[blk] * 6,
        out_shape=[jax.ShapeDtypeStruct((S, W), BF16)] * 3 + [jax.ShapeDtypeStruct((S, W), F32)] * 3,
        compiler_params=_params("parallel"), name="merge_bwd")(dy, *outs, *lses)
    return res[:3], res[3:]


def _mix_fwd(pool_y, attn_y, w_pool_out_t, w_attn_out_t, gates_pre, b_gate):
    S = pool_y.shape[0]
    D = D_MODEL
    tm, tn = 512, 512
    nj = D // tn

    def body(p_ref, a_ref, wp_ref, wa_ref, g0_ref, g1_ref, b0_ref, b1_ref, mixed_ref, yp_ref, ya_ref):
        yp = _dot(p_ref[...], wp_ref[...], "nt")
        ya = _dot(a_ref[...], wa_ref[...], "nt")
        g0 = jax.nn.sigmoid(g0_ref[...] + b0_ref[...])
        g1 = jax.nn.sigmoid(g1_ref[...] + b1_ref[...])
        mixed_ref[...] = (g0 * yp + g1 * ya).astype(BF16)
        yp_ref[...] = yp.astype(BF16)
        ya_ref[...] = ya.astype(BF16)

    out = pl.BlockSpec((tm, tn), lambda i, j: (i, j))
    return pl.pallas_call(
        body, grid=(S // tm, nj),
        in_specs=[pl.BlockSpec((tm, POOL_WIDTH), lambda i, j: (i, 0)),
                  pl.BlockSpec((tm, ATTN_OUT_WIDTH), lambda i, j: (i, 0)),
                  pl.BlockSpec((tn, POOL_WIDTH), lambda i, j: (j, 0)),
                  pl.BlockSpec((tn, ATTN_OUT_WIDTH), lambda i, j: (j, 0)),
                  pl.BlockSpec((tm, tn), lambda i, j: (i, j)), pl.BlockSpec((tm, tn), lambda i, j: (i, j + nj)),
                  pl.BlockSpec((1, tn), lambda i, j: (0, j)), pl.BlockSpec((1, tn), lambda i, j: (0, j + nj))],
        out_specs=[out, out, out], out_shape=[jax.ShapeDtypeStruct((S, D), BF16)] * 3,
        compiler_params=_params("parallel", "parallel"), name="mix_fwd")(
            pool_y, attn_y, w_pool_out_t, w_attn_out_t, gates_pre, gates_pre, b_gate, b_gate)


def _mix_bwd(dmixed, gates_pre, b_gate, y_pool, y_attn):
    S = dmixed.shape[0]
    D = D_MODEL
    T = 256

    def body(dm_ref, gp_ref, b_ref, yp_ref, ya_ref, dyp_ref, dya_ref, dg_ref, db_ref):
        i = pl.program_id(0)
        dm = dm_ref[...].astype(F32)
        parts = []
        for br, (y_ref, dy_ref) in enumerate(((yp_ref, dyp_ref), (ya_ref, dya_ref))):
            cs = slice(br * D, (br + 1) * D)
            g = jax.nn.sigmoid(gp_ref[:, cs] + b_ref[:, cs])
            dy_ref[...] = (dm * g).astype(BF16)
            dpre = dm * y_ref[...].astype(F32) * g * (1.0 - g)
            dg_ref[:, cs] = dpre.astype(BF16)
            parts.append(jnp.sum(dpre, axis=0, keepdims=True))

        @pl.when(i == 0)
        def _():
            db_ref[:, 0:D] = parts[0]
            db_ref[:, D:2 * D] = parts[1]

        @pl.when(i > 0)
        def _():
            db_ref[:, 0:D] += parts[0]
            db_ref[:, D:2 * D] += parts[1]

    row = pl.BlockSpec((T, D), lambda i: (i, 0))
    row2 = pl.BlockSpec((T, 2 * D), lambda i: (i, 0))
    vec2 = pl.BlockSpec((1, 2 * D), lambda i: (0, 0))
    return pl.pallas_call(
        body, grid=(S // T,), in_specs=[row, row2, vec2, row, row], out_specs=[row, row, row2, vec2],
        out_shape=[jax.ShapeDtypeStruct((S, D), BF16), jax.ShapeDtypeStruct((S, D), BF16),
                   jax.ShapeDtypeStruct((S, 2 * D), BF16), jax.ShapeDtypeStruct((1, 2 * D), F32)],
        compiler_params=_params("arbitrary"), name="mix_bwd")(dmixed, gates_pre, b_gate, y_pool, y_attn)


_CONV_T, _CONV_C = 512, 512


def _conv_taps(ext, w_ref, b_ref):
    x2 = pltpu.roll(ext, 2, 0)[HALO:]
    x1 = pltpu.roll(ext, 1, 0)[HALO:]
    x0 = ext[HALO:]
    y = b_ref[...] + w_ref[0:1, :] * x2
    y = y + w_ref[1:2, :] * x1
    y = y + w_ref[2:3, :] * x0
    return y, (x2, x1, x0)


def _gelu_parts(a):
    cdf = 0.5 * (1.0 + lax.erf(a * (1.0 / math.sqrt(2.0))))
    return cdf, a * cdf


def _conv_specs(S):
    T, C = _CONV_T, _CONV_C
    nj = D_FF // C
    blk = pl.BlockSpec((2, T, C), lambda i, j: (0, i, j))
    per = T // HALO
    halo = pl.BlockSpec((2, HALO, C), lambda i, j: (0, jnp.maximum(i * per - 1, 0), j))
    w_a = pl.BlockSpec((3, C), lambda i, j: (0, j))
    w_b = pl.BlockSpec((3, C), lambda i, j: (0, j + nj))
    b_a = pl.BlockSpec((1, C), lambda i, j: (0, j))
    b_b = pl.BlockSpec((1, C), lambda i, j: (0, j + nj))
    return T, C, nj, blk, halo, w_a, w_b, b_a, b_b


def _conv_load(ext_ref, halo_ref, up_ref, first):
    for h in range(2):
        ext_ref[h, 0:HALO, :] = jnp.where(first, 0.0, halo_ref[h].astype(F32))
        ext_ref[h, HALO:, :] = up_ref[h].astype(F32)


def _convffn_fwd(up_pre, conv_w, conv_b):
    S = up_pre.shape[1]
    T, C, nj, blk, halo, w_a, w_b, b_a, b_b = _conv_specs(S)

    def body(up_ref, halo_ref, wa_ref, wb_ref, ba_ref, bb_ref, act_ref, ext_ref):
        _conv_load(ext_ref, halo_ref, up_ref, pl.program_id(0) == 0)
        a, _ = _conv_taps(ext_ref[0], wa_ref, ba_ref)
        b, _ = _conv_taps(ext_ref[1], wb_ref, bb_ref)
        _, gelu = _gelu_parts(a)
        act_ref[...] = (gelu * b).astype(BF16)

    return pl.pallas_call(
        body, grid=(S // T, nj), in_specs=[blk, halo, w_a, w_b, b_a, b_b],
        out_specs=pl.BlockSpec((T, C), lambda i, j: (i, j)),
        out_shape=jax.ShapeDtypeStruct((S, D_FF), BF16),
        scratch_shapes=[pltpu.VMEM((2, HALO + T, C), F32)],
        compiler_params=_params("parallel", "parallel"), name="convffn_fwd")(
            up_pre, up_pre, conv_w, conv_w, conv_b, conv_b)


def _convffn_bwd_a(up_pre, dact, conv_w, conv_b):
    S = up_pre.shape[1]
    T, C, nj, blk, halo, w_a, w_b, b_a, b_b = _conv_specs(S)
    def swap(spec):
        return pl.BlockSpec(spec.block_shape, lambda j, i, _m=spec.index_map: _m(i, j))

    def body(up_ref, halo_ref, dact_ref, wa_ref, wb_ref, ba_ref, bb_ref, dup_ref, dwa_ref, dwb_ref, dba_ref, dbb_ref,
             ext_ref):
        i = pl.program_id(1)
        _conv_load(ext_ref, halo_ref, up_ref, i == 0)
        a, xa = _conv_taps(ext_ref[0], wa_ref, ba_ref)
        b, xb = _conv_taps(ext_ref[1], wb_ref, bb_ref)
        cdf, gelu = _gelu_parts(a)
        dgelu = cdf + a * (jnp.exp(-0.5 * a * a) * (1.0 / math.sqrt(2.0 * math.pi)))
        dact = dact_ref[...].astype(F32)
        da = dact * b * dgelu
        db = dact * gelu
        dup_ref[0] = da.astype(BF16)
        dup_ref[1] = db.astype(BF16)
        for dval, xs, dw_ref, dbias_ref in ((da, xa, dwa_ref, dba_ref), (db, xb, dwb_ref, dbb_ref)):
            dw = jnp.concatenate([jnp.sum(dval * xk, axis=0, keepdims=True) for xk in xs], axis=0)
            dbias = jnp.sum(dval, axis=0, keepdims=True)

            @pl.when(i == 0)
            def _():
                dw_ref[...] = dw
                dbias_ref[...] = dbias

            @pl.when(i > 0)
            def _():
                dw_ref[...] += dw
                dbias_ref[...] += dbias

    dact_spec = pl.BlockSpec((T, C), lambda j, i: (i, j))
    dw_spec = pl.BlockSpec((3, C), lambda j, i: (0, j))
    db_spec = pl.BlockSpec((1, C), lambda j, i: (0, j))
    res = pl.pallas_call(
        body, grid=(nj, S // T),
        in_specs=[swap(blk), swap(halo), dact_spec, swap(w_a), swap(w_b), swap(b_a), swap(b_b)],
        out_specs=[swap(blk), dw_spec, dw_spec, db_spec, db_spec],
        out_shape=[jax.ShapeDtypeStruct((2, S, D_FF), BF16), jax.ShapeDtypeStruct((3, D_FF), F32),
                   jax.ShapeDtypeStruct((3, D_FF), F32), jax.ShapeDtypeStruct((1, D_FF), F32),
                   jax.ShapeDtypeStruct((1, D_FF), F32)],
        scratch_shapes=[pltpu.VMEM((2, HALO + T, C), F32)],
        compiler_params=_params("parallel", "arbitrary"), name="convffn_bwd_a")(
            up_pre, up_pre, dact, conv_w, conv_w, conv_b, conv_b)
    dup, dwa, dwb, dba, dbb = res
    dconv_w = jnp.concatenate([dwa, dwb], axis=1)
    dconv_b = jnp.concatenate([dba, dbb], axis=1)
    return dup, dconv_w, dconv_b


def _convffn_bwd_b(dup, conv_w):
    S = dup.shape[1]
    T, C = _CONV_T, _CONV_C
    nj = D_FF // C
    per = T // HALO
    n_blk = S // T

    def body(d_ref, halo_ref, wa_ref, wb_ref, o_ref, ext_ref):
        i = pl.program_id(0)
        n = T + HALO
        for h, w_ref in enumerate((wa_ref, wb_ref)):
            ext_ref[h, 0:T, :] = d_ref[h].astype(F32)
            ext_ref[h, T:, :] = jnp.where(i == n_blk - 1, 0.0, halo_ref[h].astype(F32))
            ext = ext_ref[h]
            y = w_ref[2:3, :] * ext[0:T] + w_ref[1:2, :] * pltpu.roll(ext, n - 1, 0)[0:T]
            y = y + w_ref[0:1, :] * pltpu.roll(ext, n - 2, 0)[0:T]
            o_ref[h] = y.astype(BF16)

    blk = pl.BlockSpec((2, T, C), lambda i, j: (0, i, j))
    halo = pl.BlockSpec((2, HALO, C), lambda i, j: (0, jnp.minimum((i + 1) * per, S // HALO - 1), j))
    return pl.pallas_call(
        body, grid=(n_blk, nj),
        in_specs=[blk, halo, pl.BlockSpec((3, C), lambda i, j: (0, j)), pl.BlockSpec((3, C), lambda i, j: (0, j + nj))],
        out_specs=blk, out_shape=jax.ShapeDtypeStruct((2, S, D_FF), BF16),
        scratch_shapes=[pltpu.VMEM((2, T + HALO, C), F32)],
        compiler_params=_params("parallel", "parallel"), name="convffn_bwd_b")(dup, dup, conv_w, conv_w)


def _position():
    return lax.axis_index("x"), lax.axis_index("y"), lax.axis_index("c")


def _flip(pos, k):
    x, y, c = pos
    return (1 - x if k & 4 else x, 1 - y if k & 2 else y, 1 - c if k & 1 else c)


def _index(pos):
    return 4 * pos[0] + 2 * pos[1] + pos[2]


def _all_gather_weights(shards):
    n = len(shards)

    def body(*refs):
        ins, outs = refs[:n], refs[n:2 * n]
        send_sems, recv_sems, local_sems = refs[2 * n:]
        me = _position()
        x, y, c = me
        sibling = (x, y, 1 - c)
        chips = [(1 - x, y), (x, 1 - y), (1 - x, 1 - y)]

        def copy(a, k, block, to, own=False):
            slot = outs[a].at[_index(block)]
            return pltpu.make_async_remote_copy(
                src_ref=ins[a] if own else slot, dst_ref=slot, send_sem=send_sems.at[7 * a + k],
                recv_sem=recv_sems.at[7 * a + k], device_id=to, device_id_type=MESH)

        mines = [pltpu.make_async_copy(ins[a], outs[a].at[_index(me)], local_sems.at[a]) for a in range(n)]
        started = []
        for a in range(n):
            mines[a].start()
            first = [copy(a, 0, me, sibling, own=True)]
            first += [copy(a, 1 + j, me, (*chip, c), own=True) for j, chip in enumerate(chips)]
            for cp in first:
                cp.start()
            started += first
        for j, chip in enumerate(chips):
            for a in range(n):
                copy(a, 1 + j, (*chip, c), me).wait_recv()
                passed = copy(a, 4 + j, (*chip, c), sibling)
                passed.start()
                started.append(passed)
        for a in range(n):
            copy(a, 0, sibling, me).wait_recv()
            for j, chip in enumerate(chips):
                copy(a, 4 + j, (*chip, 1 - c), me).wait_recv()
        for cp in started:
            cp.wait_send()
        for cp in mines:
            cp.wait()

    hbm = pl.BlockSpec(memory_space=pl.ANY)
    return pl.pallas_call(
        body, out_shape=[jax.ShapeDtypeStruct((N_DEV,) + s.shape, s.dtype) for s in shards],
        in_specs=[hbm] * n, out_specs=[hbm] * n,
        scratch_shapes=[pltpu.SemaphoreType.DMA((7 * n,)), pltpu.SemaphoreType.DMA((7 * n,)),
                        pltpu.SemaphoreType.DMA((n,))],
        name="all_gather_weights")(*shards)


_HBM = pl.BlockSpec(memory_space=pltpu.HBM)
_SEM = pl.BlockSpec(memory_space=pltpu.SEMAPHORE)
_ANY = pl.BlockSpec(memory_space=pl.ANY)
_EFFECT = pltpu.SideEffectType.DATAFLOW_SIDE_EFFECTING


def _in_hbm(a):
    return pltpu.with_memory_space_constraint(a, pltpu.HBM)


def _split_copies(name, srcs, lands, n_copies, plan, start_after, wait_after):
    ns, nl = len(srcs), len(lands)
    start_after = tuple(start_after)
    na = len(start_after)

    def start_body(*refs):
        src_refs, land_refs = refs[:ns], refs[ns:ns + nl]
        send_sems, recv_sems = refs[ns + nl + na], refs[ns + nl + na + 1]
        token = refs[-1]
        for k, (src, dst, _, peer) in enumerate(plan(src_refs, land_refs)):
            pltpu.make_async_remote_copy(src_ref=src, dst_ref=dst, send_sem=send_sems.at[k], recv_sem=recv_sems.at[k],
                                         device_id=peer, device_id_type=MESH).start()
        token[...] = jnp.zeros_like(token)

    thru = [pltpu.HBM(a.shape, a.dtype) for a in list(srcs) + list(lands)]
    res = pl.pallas_call(
        start_body, name=name + "_start",
        out_shape=[pltpu.SemaphoreType.DMA((n_copies,)), pltpu.SemaphoreType.DMA((n_copies,))] + thru
        + [jax.ShapeDtypeStruct((8, 128), F32)],
        in_specs=[_HBM] * (ns + nl) + [_ANY] * na,
        out_specs=[_SEM, _SEM] + [_HBM] * (ns + nl) + [pl.BlockSpec(memory_space=pltpu.VMEM)],
        input_output_aliases={i: 2 + i for i in range(ns + nl)},
        compiler_params=pltpu.CompilerParams(has_side_effects=_EFFECT),
    )(*[_in_hbm(a) for a in list(srcs) + list(lands)], *start_after)
    send_sems, recv_sems, token = res[0], res[1], res[-1]
    thru_vals = res[2:-1]

    def wait():
        def wait_body(*refs):
            src_refs, land_refs = refs[:ns], refs[ns:ns + nl]
            s_sems, r_sems = refs[ns + nl], refs[ns + nl + 1]
            for k, (src, _, dst, peer) in enumerate(plan(src_refs, land_refs)):
                cp = pltpu.make_async_remote_copy(src_ref=src, dst_ref=dst, send_sem=s_sems.at[k], recv_sem=r_sems.at[k],
                                                  device_id=peer, device_id_type=MESH)
                cp.wait_send()
                cp.wait_recv()

        out = pl.pallas_call(
            wait_body, name=name + "_wait", out_shape=thru,
            in_specs=[_HBM] * (ns + nl) + [_SEM, _SEM, _ANY], out_specs=[_HBM] * (ns + nl),
            input_output_aliases={i: i for i in range(ns + nl)},
            compiler_params=pltpu.CompilerParams(has_side_effects=_EFFECT),
        )(*thru_vals, send_sems, recv_sems, wait_after())
        return list(out[:ns]), list(out[ns:])

    return wait, token


def _gather_direct_plan(n):
    def plan(src_refs, land_refs):
        me = _position()
        peers = [_flip(me, k) for k in range(1, N_DEV)]
        return [(src_refs[a], land_refs[a].at[_index(me)], land_refs[a].at[_index(p)], p)
                for a in range(n) for p in peers]
    return plan


def _place_own(shards):
    me = _index(_position())
    return [lax.dynamic_update_slice(lax.empty((N_DEV,) + s.shape, s.dtype), s[None], (me, 0, 0)) for s in shards]


def _chip_index(pos):
    return 2 * pos[0] + pos[1]


def _exchange_d2d(name, parts):
    n = len(parts)

    def body(*refs):
        ins, outs = refs[:n], refs[n:2 * n]
        send_sems, recv_sems = refs[2 * n:]
        x, y, c = _position()
        sibling = (x, y, 1 - c)

        def copy(a, q, core):
            return pltpu.make_async_remote_copy(
                src_ref=ins[a].at[2 * q + core], dst_ref=outs[a].at[q], send_sem=send_sems.at[4 * a + q],
                recv_sem=recv_sems.at[4 * a + q], device_id=sibling, device_id_type=MESH)

        sent = [copy(a, q, 1 - c) for a in range(n) for q in range(4)]
        for cp in sent:
            cp.start()
        for cp in sent:
            cp.wait_recv()
        for cp in sent:
            cp.wait_send()

    return pl.pallas_call(
        body, out_shape=[jax.ShapeDtypeStruct((4,) + p.shape[1:], p.dtype) for p in parts],
        in_specs=[_ANY] * n, out_specs=[_ANY] * n,
        scratch_shapes=[pltpu.SemaphoreType.DMA((4 * n,)), pltpu.SemaphoreType.DMA((4 * n,))],
        name=name)(*parts)


def _pair_sum(name, part, got):
    _, R, C = part.shape
    T = _row_tile(R, 16)

    def body(p_ref, g_ref, s_out, l_out):
        c = lax.axis_index("c")
        mine = jnp.where(c == 0, p_ref[0, 0].astype(F32), p_ref[0, 1].astype(F32))
        s = (mine + g_ref[0].astype(F32)).astype(BF16)
        s_out[0] = s
        l_out[0] = s

    out = pl.BlockSpec((1, T, C), lambda q, i: (q, i, 0))
    return pl.pallas_call(
        body, grid=(4, R // T),
        in_specs=[pl.BlockSpec((1, 2, T, C), lambda q, i: (q, 0, i, 0)), out], out_specs=[out, out],
        out_shape=[jax.ShapeDtypeStruct((4, R, C), BF16)] * 2,
        compiler_params=_params("parallel", "parallel"), name=name)(part.reshape(4, 2, R, C), got)


def _exchange_ici_plan(n):
    def plan(src_refs, land_refs):
        me = _position()
        x, y, c = me
        peers = [(1 - x, y, c), (x, 1 - y, c), (1 - x, 1 - y, c)]
        return [(src_refs[a].at[_chip_index(p)], land_refs[a].at[_chip_index(me)], land_refs[a].at[_chip_index(p)], p)
                for a in range(n) for p in peers]
    return plan


def _adamw(w, g, m, v):
    m = ADAM_B1 * m + (1.0 - ADAM_B1) * g
    v = ADAM_B2 * v + (1.0 - ADAM_B2) * (g * g)
    m_hat = m / (1.0 - ADAM_B1 ** ADAM_STEP)
    v_hat = v / (1.0 - ADAM_B2 ** ADAM_STEP)
    delta = -ADAM_LR * (m_hat / (jnp.sqrt(v_hat) + ADAM_EPS) + ADAM_WD * w)
    return delta, m, v


def _row_tile(rows, unit, cap=256):
    best = rows
    for t in range(unit, min(rows, cap) + 1, unit):
        if rows % t == 0:
            best = t
    return best


def _sum_partials(name, recv):
    n_src, R, C = recv.shape
    T = _row_tile(R, 16)

    def body(r_ref, g_out):
        g = r_ref[0].astype(F32)
        for s in range(1, n_src):
            g = g + r_ref[s].astype(F32)
        g_out[...] = g

    return pl.pallas_call(
        body, grid=(R // T,), in_specs=[pl.BlockSpec((n_src, T, C), lambda i: (0, i, 0))],
        out_specs=pl.BlockSpec((T, C), lambda i: (i, 0)), out_shape=jax.ShapeDtypeStruct((R, C), F32),
        compiler_params=_params("parallel"), name=name)(recv)


def _adamw_update(name, w, g, m, v):
    R, C = w.shape
    T = _row_tile(R, 8)

    def body(w_ref, g_ref, m_ref, v_ref, d_out, m_out, v_out):
        delta, mn, vn = _adamw(w_ref[...], g_ref[...], m_ref[...], v_ref[...])
        d_out[...] = delta
        m_out[...] = mn
        v_out[...] = vn

    blk = pl.BlockSpec((T, C), lambda i: (i, 0))
    return pl.pallas_call(
        body, grid=(R // T,), in_specs=[blk] * 4, out_specs=[blk] * 3,
        out_shape=[jax.ShapeDtypeStruct((R, C), F32)] * 3, compiler_params=_params("parallel"), name=name)(w, g, m, v)


def _gather_vmem(p_ref, gath, send_sems, recv_sems):
    me = _position()
    gath[_index(me)] = p_ref[...]
    copies = []
    for k in range(1, N_DEV):
        peer = _flip(me, k)
        copies.append(pltpu.make_async_remote_copy(
            src_ref=p_ref, dst_ref=gath.at[_index(me)], send_sem=send_sems.at[k - 1],
            recv_sem=recv_sems.at[k - 1], device_id=peer, device_id_type=MESH))
    for cp in copies:
        cp.start()
    for k in range(1, N_DEV):
        peer = _flip(me, k)
        pltpu.make_async_remote_copy(
            src_ref=p_ref, dst_ref=gath.at[_index(peer)], send_sem=send_sems.at[k - 1],
            recv_sem=recv_sems.at[k - 1], device_id=peer, device_id_type=MESH).wait_recv()
    for cp in copies:
        cp.wait_send()


def _all_gather_small(p):
    R, C = p.shape

    def body(p_ref, out_ref, send_sems, recv_sems):
        _gather_vmem(p_ref, out_ref, send_sems, recv_sems)

    vm = pl.BlockSpec(memory_space=pltpu.VMEM)
    return pl.pallas_call(
        body, in_specs=[vm], out_specs=vm, out_shape=jax.ShapeDtypeStruct((N_DEV, R, C), F32),
        scratch_shapes=[pltpu.SemaphoreType.DMA((7,)), pltpu.SemaphoreType.DMA((7,))],
        name="all_gather_small")(p)


def _small_allreduce_adamw(part, w, m, v):
    R, C = part.shape

    def body(p_ref, w_ref, m_ref, v_ref, g_out, d_out, m_out, v_out, gath, send_sems, recv_sems):
        _gather_vmem(p_ref, gath, send_sems, recv_sems)
        g = gath[0]
        for s in range(1, N_DEV):
            g = g + gath[s]
        delta, mn, vn = _adamw(w_ref[...], g, m_ref[...], v_ref[...])
        g_out[...] = g
        d_out[...] = delta
        m_out[...] = mn
        v_out[...] = vn

    vm = pl.BlockSpec(memory_space=pltpu.VMEM)
    return pl.pallas_call(
        body, in_specs=[vm] * 4, out_specs=[vm] * 4, out_shape=[jax.ShapeDtypeStruct((R, C), F32)] * 4,
        scratch_shapes=[pltpu.VMEM((N_DEV, R, C), F32), pltpu.SemaphoreType.DMA((7,)), pltpu.SemaphoreType.DMA((7,))],
        name="small_allreduce_adamw")(part, w, m, v)


_BIG = ("w_in", "w_up", "w_down", "w_out", "w_pool_out", "w_attn_out")
_TRANSPOSED = ("w_in", "w_up", "w_pool_out", "w_attn_out")
_SMALL = (("g_mix", 2048), ("b_gate", 4096), ("pool_scale", 1024), ("g_ffn", 2048), ("conv_b", 11264), ("g_final", 2048))
CONV_ROWS = 8


def _to_comm(name, shard):
    return shard.T if name in _TRANSPOSED else shard


def _pack_small(vals):
    flat = [vals[n].reshape(-1) for n, _ in _SMALL]
    used = sum(f.shape[0] for f in flat)
    flat.append(jnp.zeros((SMALL_ROWS * PACK_COLS - used,), F32))
    return jnp.concatenate(flat).reshape(SMALL_ROWS, PACK_COLS)


def _unpack_small(packed, like):
    flat = packed.reshape(-1)
    out, off = {}, 0
    for name, n in _SMALL:
        out[name] = flat[off:off + n].reshape(like[name].shape)
        off += n
    return out


def _local_step(x, target, W, small, first_token=0.0, late_weights=None, emit=None):
    S = x.shape[0]
    o_qkv, o_gate = POOL_WIDTH, POOL_WIDTH + QKV_WIDTH
    if emit is None:
        emit = lambda grads: None

    def zero(token):
        return 0.0 if token is None else token[0, 0]
    w_in_t = W["w_in"]

    h1 = _rms_fwd("rms1_fwd", x, small["g_mix"] + first_token)
    u = _mm_nt("proj_u", h1, w_in_t, tm=1024, tn=512, tk=2048, o_dtype=F32, n_rows=POOL_WIDTH, b_row_off=0)
    qkv = _mm_nt("proj_qkv", h1, w_in_t, tm=1024, tn=512, tk=2048, o_dtype=BF16, n_rows=QKV_WIDTH, b_row_off=o_qkv)
    gates_pre = _mm_nt("proj_gates", h1, w_in_t, tm=1024, tn=512, tk=2048, o_dtype=F32, n_rows=2 * D_MODEL,
                       b_row_off=o_gate)
    pool_y = _pool_fwd(u, W["w_pool_lin"], small["pool_scale"])
    outs, lses = zip(*[_attn_fwd(gi, qkv) for gi in range(3)])
    attn_y = _merge_fwd(outs, lses)
    if late_weights is not None:
        W = dict(W, **late_weights(attn_y))
    w_up_t = W["w_up"]
    mixed, y_pool, y_attn = _mix_fwd(pool_y, attn_y, W["w_pool_out"], W["w_attn_out"], gates_pre, small["b_gate"])
    x1 = _mm_nn("out_proj", mixed, W["w_out"], tm=1024, tn=512, tk=2048, o_dtype=F32, res=x)

    h2 = _rms_fwd("rms2_fwd", x1, small["g_ffn"])
    nj = D_FF // 512
    up_pre = _mm("up_proj", h2, w_up_t, mode="nt", grid=(S // 1024, 2 * nj, 1),
                 a_spec=pl.BlockSpec((1024, D_MODEL), lambda i, j, k: (i, 0)),
                 b_spec=pl.BlockSpec((512, D_MODEL), lambda i, j, k: (j, 0)),
                 o_spec=pl.BlockSpec((None, 1024, 512), lambda i, j, k: (j // nj, i, j % nj)),
                 o_shape=(2, S, D_FF), o_dtype=BF16, acc_shape=(1024, 512))
    act = _convffn_fwd(up_pre, small["conv_w_full"], small["conv_b"])
    x2 = _mm_nn("down_proj", act, W["w_down"], tm=1024, tn=512, tk=D_FF // 2, o_dtype=F32, res=x1)

    dx2, dx2_b, dg_final, sq_cols = _final_loss("final_loss", x2, small["g_final"], target)

    dact = _mm_nt("d_act", dx2_b, W["w_down"], tm=1024, tn=512, tk=2048, o_dtype=BF16)
    dw_down = _mm_tn("dw_down", act, dx2_b, tm=512, tn=2048, tk=TOKEN_TK)
    tok = emit(dict(w_down=dw_down))
    dup, dconv_w, dconv_b = _convffn_bwd_a(up_pre, dact, small["conv_w_full"], small["conv_b"] + zero(tok))
    dup_pre = _convffn_bwd_b(dup, small["conv_w_full"])
    tkh = D_FF // 2
    dh2 = _mm("d_h2", dup_pre, w_up_t, mode="nn", grid=(S // 1024, D_MODEL // 1024, 4),
              a_spec=pl.BlockSpec((None, 1024, tkh), lambda i, j, k: (k // 2, i, k % 2)),
              b_spec=pl.BlockSpec((tkh, 1024), lambda i, j, k: (k, j)),
              o_spec=pl.BlockSpec((1024, 1024), lambda i, j, k: (i, j)),
              o_shape=(S, D_MODEL), o_dtype=F32, acc_shape=(1024, 1024))
    dw_up = _mm("dw_up", dup_pre, h2, mode="tn", grid=(2 * nj, 1, S // TOKEN_TK),
                a_spec=pl.BlockSpec((None, TOKEN_TK, 512), lambda i, j, k: (i // nj, k, i % nj)),
                b_spec=pl.BlockSpec((TOKEN_TK, D_MODEL), lambda i, j, k: (k, 0)),
                o_spec=pl.BlockSpec((512, D_MODEL), lambda i, j, k: (i, 0)),
                o_shape=(2 * D_FF, D_MODEL), o_dtype=BF16, acc_shape=(512, D_MODEL))
    tok = emit(dict(w_up=dw_up, conv_w=dconv_w))
    dx1, dx1_b, dg_ffn = _rms_bwd("rms2_bwd", x1, small["g_ffn"] + zero(tok), dh2, dx2)

    dmixed = _mm_nt("d_mixed", dx1_b, W["w_out"], tm=1024, tn=512, tk=2048, o_dtype=BF16)
    dw_out = _mm_tn("dw_out", mixed, dx1_b, tm=512, tn=2048, tk=TOKEN_TK)
    tok = emit(dict(w_out=dw_out))
    dy_pool, dy_attn, dgates, db_gate = _mix_bwd(dmixed, gates_pre, small["b_gate"] + zero(tok), y_pool, y_attn)
    dpool_y = _mm_nn("d_pool_y", dy_pool, W["w_pool_out"], tm=1024, tn=512, tk=2048, o_dtype=F32)
    dw_pool_out = _mm_tn("dw_pool_out", dy_pool, pool_y, tm=1024, tn=1024, tk=TOKEN_TK)
    dattn_y = _mm_nn("d_attn_y", dy_attn, W["w_attn_out"], tm=1024, tn=512, tk=2048, o_dtype=F32)
    dw_attn_out = _mm_tn("dw_attn_out", dy_attn, attn_y, tm=1024, tn=512, tk=TOKEN_TK)
    e, dw_pool_lin, dpool_scale = _pool_bwd_a(u, dpool_y, W["w_pool_lin"], small["pool_scale"])
    du = _pool_bwd_b(e)
    dos, dps = _merge_bwd(dattn_y, outs, lses)
    dqkv = [_attn_bwd(gi, qkv, dos[gi], lses[gi], dps[gi]) for gi in range(3)]
    dproj = jnp.concatenate([du] + [dqkv[gi][sec] for sec in range(3) for gi in range(3)] + [dgates], axis=1)
    dw_in = _mm_tn("dw_in", dproj, h1, tm=512, tn=2048, tk=TOKEN_TK)
    tok = emit(dict(w_in=dw_in, w_pool_out=dw_pool_out, w_attn_out=dw_attn_out, w_pool_lin=dw_pool_lin))
    dh1 = _mm_nn("d_h1", dproj, w_in_t, tm=1024, tn=1024, tk=IN_WIDTH // 4, o_dtype=F32,
                 after=tok)
    grad_x, _, dg_mix = _rms_bwd("rms1_bwd", x, small["g_mix"] + zero(tok), dh1, dx1)

    grads = dict(w_in=dw_in, w_up=dw_up, w_down=dw_down, w_out=dw_out, w_pool_out=dw_pool_out,
                 w_attn_out=dw_attn_out, w_pool_lin=dw_pool_lin, conv_w=dconv_w)
    small_grads = dict(g_mix=dg_mix, b_gate=db_gate, pool_scale=dpool_scale, g_ffn=dg_ffn, conv_b=dconv_b,
                       g_final=dg_final)
    return sq_cols, grad_x, grads, small_grads


def kernel(x, g_mix, w_in, b_gate, w_pool_lin, pool_scale, w_pool_out, w_attn_out, w_out, g_ffn, w_up, conv_w, conv_b, w_down, g_final, loss_target, m_g_mix, m_w_in, m_b_gate, m_w_pool_lin, m_pool_scale, m_w_pool_out, m_w_attn_out, m_w_out, m_g_ffn, m_w_up, m_conv_w, m_conv_b, m_w_down, m_g_final, v_g_mix, v_w_in, v_b_gate, v_w_pool_lin, v_pool_scale, v_w_pool_out, v_w_attn_out, v_w_out, v_g_ffn, v_w_up, v_conv_w, v_conv_b, v_w_down, v_g_final):
    given = dict(g_mix=g_mix, w_in=w_in, b_gate=b_gate, w_pool_lin=w_pool_lin, pool_scale=pool_scale,
                 w_pool_out=w_pool_out, w_attn_out=w_attn_out, w_out=w_out, g_ffn=g_ffn, w_up=w_up, conv_w=conv_w,
                 conv_b=conv_b, w_down=w_down, g_final=g_final)
    mom_m = dict(g_mix=m_g_mix, w_in=m_w_in, b_gate=m_b_gate, w_pool_lin=m_w_pool_lin, pool_scale=m_pool_scale,
                 w_pool_out=m_w_pool_out, w_attn_out=m_w_attn_out, w_out=m_w_out, g_ffn=m_g_ffn, w_up=m_w_up,
                 conv_w=m_conv_w, conv_b=m_conv_b, w_down=m_w_down, g_final=m_g_final)
    mom_v = dict(g_mix=v_g_mix, w_in=v_w_in, b_gate=v_b_gate, w_pool_lin=v_w_pool_lin, pool_scale=v_pool_scale,
                 w_pool_out=v_w_pool_out, w_attn_out=v_w_attn_out, w_out=v_w_out, g_ffn=v_g_ffn, w_up=v_w_up,
                 conv_w=v_conv_w, conv_b=v_conv_b, w_down=v_w_down, g_final=v_g_final)
    def full(g):
        return g.reshape(N_DEV * g.shape[1], g.shape[2])

    shard = {n: _to_comm(n, given[n][0]).astype(BF16) for n in _BIG}
    W = dict(w_in=full(_all_gather_weights([shard["w_in"]])[0]))
    rest = [n for n in _BIG if n != "w_in"]
    lin_all = _all_gather_small(w_pool_lin[0].reshape(4 * 32, POOL_GROUP_WIDTH))
    W["w_pool_lin"] = jnp.moveaxis(lin_all.reshape(N_DEV, 4, 32, POOL_GROUP_WIDTH), 0, 1).reshape(
        4, POOL_GROUP_WIDTH, POOL_GROUP_WIDTH).astype(BF16)
    conv_w_all = _all_gather_small(jnp.pad(conv_w[0], ((0, CONV_ROWS - 3), (0, 0))))
    conv_w_full = jnp.moveaxis(conv_w_all[:, :3, :], 0, 1).reshape(3, 2 * D_FF)

    rest_wait, rest_token = _split_copies(
        "gather_rest", [shard[n] for n in rest], _place_own([shard[n] for n in rest]), 7 * len(rest),
        _gather_direct_plan(len(rest)), start_after=(W["w_in"], lin_all, conv_w_all),
        wait_after=lambda: rest_after[0])
    rest_after = []

    def late_weights(after):
        rest_after.append(after)
        return {n: full(l) for n, l in zip(rest, rest_wait()[1])}

    small = dict(g_mix=g_mix, b_gate=b_gate, pool_scale=pool_scale, g_ffn=g_ffn, conv_b=conv_b,
                 g_final=g_final.reshape(1, D_MODEL), conv_w_full=conv_w_full)

    pending = []
    done = {}

    def emit(group):
        names = list(group)
        parts = []
        for n in names:
            g = group[n]
            if n == "w_pool_lin":
                g = jnp.moveaxis(g.reshape(4, N_DEV, 32, POOL_GROUP_WIDTH), 1, 0).reshape(
                    N_DEV, 4 * 32, POOL_GROUP_WIDTH).astype(BF16)
            elif n == "conv_w":
                g = jnp.pad(jnp.moveaxis(g.reshape(3, N_DEV, 2 * D_FF // N_DEV), 1, 0),
                            ((0, 0), (0, CONV_ROWS - 3), (0, 0))).astype(BF16)
            else:
                g = g.reshape(N_DEV, g.shape[0] // N_DEV, g.shape[1])
            parts.append(g)
        tag = names[0]
        got = _exchange_d2d("exchange_d2d_" + tag, parts)
        sums, lands = zip(*[_pair_sum("pair_sum_" + n, p, r) for n, p, r in zip(names, parts, got)])
        wait, token = _split_copies("exchange_ici_" + tag, sums, lands, 3 * len(names), _exchange_ici_plan(len(names)),
                                    start_after=(got[0],), wait_after=lambda: done["grad_x"])
        pending.append((names, wait))
        return token

    sq_cols, grad_x, _, small_grads = _local_step(x[0], loss_target[0], W, small, first_token=rest_token[0, 0],
                                                  late_weights=late_weights, emit=emit)
    done["grad_x"] = grad_x
    loss = lax.psum(0.5 / D_MODEL * jnp.sum(sq_cols), AXES)

    res = {}
    for n, r in [(n, r) for names, wait in pending for n, r in zip(names, wait()[1])]:
        g = _sum_partials("sum_" + n, r)
        if n in _TRANSPOSED:
            g = g.T
        elif n == "conv_w":
            g = g[:3]
        shape = given[n].shape
        rows = math.prod(shape[:-1])
        w2, m2, v2 = (d[n].reshape(rows, shape[-1]) for d in (given, mom_m, mom_v))
        upd = _adamw_update("adamw_" + n, w2, g.reshape(rows, shape[-1]), m2, v2)
        res[n] = [a.reshape(shape) for a in (g,) + tuple(upd)]

    packed_small = [_pack_small(d) for d in (given, mom_m, mom_v)]
    res_small = [_unpack_small(p, given) for p in _small_allreduce_adamw(_pack_small(small_grads), *packed_small)]

    order = ["g_mix", "w_in", "b_gate", "w_pool_lin", "pool_scale", "w_pool_out", "w_attn_out", "w_out", "g_ffn",
             "w_up", "conv_w", "conv_b", "w_down", "g_final"]
    outs = [loss, grad_x[None]]
    for kind in range(4):
        for n in order:
            outs.append(res[n][kind] if n in res else res_small[kind][n])
    return tuple(outs)
```

```python
import math

import jax
import jax.numpy as jnp
from jax import lax
from jax.experimental import pallas as pl
from jax.experimental.pallas import tpu as pltpu

F32 = jnp.float32
BF16 = jnp.bfloat16
MESH = pl.DeviceIdType.MESH
AXES = ("x", "y", "c")
N_DEV = 8

D_MODEL = 2048
POOL_WINDOWS = (2, 4, 8, 16)
POOL_GROUP_WIDTH = 256
POOL_WIDTH = 1024
ATTN_GROUPS = ((128, 1), (512, 4), (2048, 16))
SPAN = 128
HEADS_PER_GROUP = 4
N_ATTN_HEADS = 12
HEAD_DIM = 128
ATTN_WIDTH = 1536
QKV_WIDTH = 3 * ATTN_WIDTH
ATTN_OUT_WIDTH = 512
IN_WIDTH = 9728
D_FF = 5632
RMS_EPS = 1e-6
ADAM_LR, ADAM_B1, ADAM_B2, ADAM_EPS, ADAM_WD, ADAM_STEP = 0.001, 0.9, 0.999, 1e-08, 0.01, 10

VMEM_LIMIT_BYTES = 48 * 1024 * 1024
HALO = 16
TOKEN_TK = 2048
PACK_COLS = 1024
SMALL_ROWS = 24
NEG_BIG = -1e30

_DN = {"nn": (((1,), (0,)), ((), ())), "nt": (((1,), (1,)), ((), ())), "tn": (((0,), (0,)), ((), ()))}


def _params(*sem):
    return pltpu.CompilerParams(dimension_semantics=sem, vmem_limit_bytes=VMEM_LIMIT_BYTES)


def _dot(a, b, mode):
    return lax.dot_general(a.astype(BF16), b.astype(BF16), _DN[mode], preferred_element_type=F32)


def _mm(name, a, b, *, mode, grid, a_spec, b_spec, o_spec, o_shape, o_dtype, acc_shape, res=None, res_spec=None,
        after=None):
    nk = grid[2]
    n_in = 2 + (res is not None) + (after is not None)

    def body(*refs):
        a_ref, b_ref = refs[:2]
        r_ref = refs[2] if res is not None else None
        o_ref = refs[n_in]
        scr = refs[n_in + 1:]
        prod = _dot(a_ref[...], b_ref[...], mode)

        def finish(val):
            if r_ref is not None:
                val = val + r_ref[...]
            o_ref[...] = val.astype(o_ref.dtype)

        if nk == 1:
            finish(prod)
        else:
            acc = scr[0]
            k = pl.program_id(2)

            @pl.when(k == 0)
            def _():
                acc[...] = prod

            @pl.when(k > 0)
            def _():
                acc[...] += prod

            @pl.when(k == nk - 1)
            def _():
                finish(acc[...])

    in_specs = [a_spec, b_spec]
    args = [a, b]
    if res is not None:
        in_specs.append(res_spec)
        args.append(res)
    if after is not None:
        in_specs.append(pl.BlockSpec((8, 128), lambda i, j, k: (0, 0)))
        args.append(after)
    return pl.pallas_call(
        body, grid=grid, in_specs=in_specs, out_specs=o_spec,
        out_shape=jax.ShapeDtypeStruct(o_shape, o_dtype),
        scratch_shapes=[] if nk == 1 else [pltpu.VMEM(acc_shape, F32)],
        compiler_params=_params("parallel", "parallel", "arbitrary"), name=name)(*args)


def _mm_nn(name, a, b, *, tm, tn, tk, o_dtype, n_cols=None, b_col_off=0, res=None, after=None):
    M, K = a.shape
    N = b.shape[1] if n_cols is None else n_cols
    off = b_col_off // tn
    return _mm(name, a, b, mode="nn", grid=(M // tm, N // tn, K // tk),
               a_spec=pl.BlockSpec((tm, tk), lambda i, j, k: (i, k)),
               b_spec=pl.BlockSpec((tk, tn), lambda i, j, k: (k, j + off)),
               o_spec=pl.BlockSpec((tm, tn), lambda i, j, k: (i, j)),
               o_shape=(M, N), o_dtype=o_dtype, acc_shape=(tm, tn), res=res,
               res_spec=pl.BlockSpec((tm, tn), lambda i, j, k: (i, j)), after=after)


def _mm_nt(name, a, b, *, tm, tn, tk, o_dtype, n_rows=None, b_row_off=0):
    M, K = a.shape
    N = b.shape[0] if n_rows is None else n_rows
    off = b_row_off // tn
    return _mm(name, a, b, mode="nt", grid=(M // tm, N // tn, K // tk),
               a_spec=pl.BlockSpec((tm, tk), lambda i, j, k: (i, k)),
               b_spec=pl.BlockSpec((tn, tk), lambda i, j, k: (j + off, k)),
               o_spec=pl.BlockSpec((tm, tn), lambda i, j, k: (i, j)),
               o_shape=(M, N), o_dtype=o_dtype, acc_shape=(tm, tn))


def _mm_tn(name, a, b, *, tm, tn, tk):
    K, M = a.shape
    N = b.shape[1]
    return _mm(name, a, b, mode="tn", grid=(M // tm, N // tn, K // tk),
               a_spec=pl.BlockSpec((tk, tm), lambda i, j, k: (k, i)),
               b_spec=pl.BlockSpec((tk, tn), lambda i, j, k: (k, j)),
               o_spec=pl.BlockSpec((tm, tn), lambda i, j, k: (i, j)),
               o_shape=(M, N), o_dtype=BF16, acc_shape=(tm, tn))


def _rms_fwd(name, x, g):
    S, D = x.shape
    T = 512

    def body(x_ref, g_ref, h_ref):
        xv = x_ref[...]
        r = lax.rsqrt(jnp.mean(xv * xv, axis=-1, keepdims=True) + RMS_EPS)
        h_ref[...] = (xv * r * g_ref[...]).astype(BF16)

    return pl.pallas_call(
        body, grid=(S // T,),
        in_specs=[pl.BlockSpec((T, D), lambda i: (i, 0)), pl.BlockSpec((1, D), lambda i: (0, 0))],
        out_specs=pl.BlockSpec((T, D), lambda i: (i, 0)),
        out_shape=jax.ShapeDtypeStruct((S, D), BF16), compiler_params=_params("parallel"), name=name)(x, g)


def _rms_bwd(name, x, g, dh, dres):
    S, D = x.shape
    T = 256

    def body(x_ref, g_ref, dh_ref, dres_ref, dx_ref, dxb_ref, dg_ref):
        i = pl.program_id(0)
        xv = x_ref[...]
        r = lax.rsqrt(jnp.mean(xv * xv, axis=-1, keepdims=True) + RMS_EPS)
        xhat = xv * r
        dhv = dh_ref[...].astype(F32)
        gdh = dhv * g_ref[...]
        c = jnp.mean(xhat * gdh, axis=-1, keepdims=True)
        dx = dres_ref[...] + r * (gdh - xhat * c)
        dx_ref[...] = dx
        dxb_ref[...] = dx.astype(BF16)
        part = jnp.sum(dhv * xhat, axis=0, keepdims=True)

        @pl.when(i == 0)
        def _():
            dg_ref[...] = part

        @pl.when(i > 0)
        def _():
            dg_ref[...] += part

    row = pl.BlockSpec((T, D), lambda i: (i, 0))
    vec = pl.BlockSpec((1, D), lambda i: (0, 0))
    return pl.pallas_call(
        body, grid=(S // T,), in_specs=[row, vec, row, row], out_specs=[row, row, vec],
        out_shape=[jax.ShapeDtypeStruct((S, D), F32), jax.ShapeDtypeStruct((S, D), BF16),
                   jax.ShapeDtypeStruct((1, D), F32)],
        compiler_params=_params("arbitrary"), name=name)(x, g, dh, dres)


def _final_loss(name, x, g, target):
    S, D = x.shape
    T = 256

    def body(x_ref, g_ref, t_ref, dx_ref, dxb_ref, dg_ref, sq_ref):
        i = pl.program_id(0)
        xv = x_ref[...]
        r = lax.rsqrt(jnp.mean(xv * xv, axis=-1, keepdims=True) + RMS_EPS)
        xhat = xv * r
        err = xhat * g_ref[...] - t_ref[...]
        dy = err * (1.0 / D)
        gdy = dy * g_ref[...]
        c = jnp.mean(xhat * gdy, axis=-1, keepdims=True)
        dx = r * (gdy - xhat * c)
        dx_ref[...] = dx
        dxb_ref[...] = dx.astype(BF16)
        dg_part = jnp.sum(dy * xhat, axis=0, keepdims=True)
        sq_part = jnp.sum(err * err, axis=0, keepdims=True)

        @pl.when(i == 0)
        def _():
            dg_ref[...] = dg_part
            sq_ref[...] = sq_part

        @pl.when(i > 0)
        def _():
            dg_ref[...] += dg_part
            sq_ref[...] += sq_part

    row = pl.BlockSpec((T, D), lambda i: (i, 0))
    vec = pl.BlockSpec((1, D), lambda i: (0, 0))
    return pl.pallas_call(
        body, grid=(S // T,), in_specs=[row, vec, row], out_specs=[row, row, vec, vec],
        out_shape=[jax.ShapeDtypeStruct((S, D), F32), jax.ShapeDtypeStruct((S, D), BF16),
                   jax.ShapeDtypeStruct((1, D), F32), jax.ShapeDtypeStruct((1, D), F32)],
        compiler_params=_params("arbitrary"), name=name)(x, g, target)


def _prev_halo_spec(T, C, col_map):
    per = T // HALO
    return pl.BlockSpec((HALO, C), lambda *g: (jnp.maximum(g[0] * per - 1, 0), col_map(*g)))


def _pool_window_sums(ext, T, i):
    t = i * T + lax.broadcasted_iota(jnp.int32, (T, 1), 0)
    outs = []
    cnts = []
    for gi, w in enumerate(POOL_WINDOWS):
        s = ext[:, gi * POOL_GROUP_WIDTH:(gi + 1) * POOL_GROUP_WIDTH]
        sh = 1
        while sh < w:
            s = s + pltpu.roll(s, sh, 0)
            sh *= 2
        cnt = jnp.minimum(t + 1, w).astype(F32)
        outs.append(s[HALO:] / cnt - ext[HALO:, gi * POOL_GROUP_WIDTH:(gi + 1) * POOL_GROUP_WIDTH])
        cnts.append(cnt)
    return outs, cnts


def _load_ext(ext_ref, halo_ref, blk_ref, first):
    ext_ref[0:HALO, :] = jnp.where(first, 0.0, halo_ref[...].astype(F32))
    ext_ref[HALO:, :] = blk_ref[...].astype(F32)


def _pool_fwd(u, w_lin, scale):
    S = u.shape[0]
    T = 512
    W = POOL_WIDTH

    def body(u_ref, halo_ref, wl_ref, sc_ref, y_ref, ext_ref):
        i = pl.program_id(0)
        _load_ext(ext_ref, halo_ref, u_ref, i == 0)
        pooled, _ = _pool_window_sums(ext_ref[...], T, i)
        for gi in range(4):
            z = _dot(pooled[gi], wl_ref[gi], "nn")
            cs = slice(gi * POOL_GROUP_WIDTH, (gi + 1) * POOL_GROUP_WIDTH)
            y_ref[:, cs] = (z * sc_ref[:, cs]).astype(BF16)

    return pl.pallas_call(
        body, grid=(S // T,),
        in_specs=[pl.BlockSpec((T, W), lambda i: (i, 0)), _prev_halo_spec(T, W, lambda i: 0),
                  pl.BlockSpec((4, 256, 256), lambda i: (0, 0, 0)), pl.BlockSpec((1, W), lambda i: (0, 0))],
        out_specs=pl.BlockSpec((T, W), lambda i: (i, 0)),
        out_shape=jax.ShapeDtypeStruct((S, W), BF16),
        scratch_shapes=[pltpu.VMEM((HALO + T, W), F32)],
        compiler_params=_params("parallel"), name="pool_fwd")(u, u, w_lin, scale)


def _pool_bwd_a(u, dy, w_lin, scale):
    S = u.shape[0]
    T = 512
    W = POOL_WIDTH

    def body(u_ref, halo_ref, dy_ref, wl_ref, sc_ref, e_ref, dwl_ref, dsc_ref, ext_ref):
        i = pl.program_id(0)
        _load_ext(ext_ref, halo_ref, u_ref, i == 0)
        pooled, cnts = _pool_window_sums(ext_ref[...], T, i)
        for gi in range(4):
            cs = slice(gi * POOL_GROUP_WIDTH, (gi + 1) * POOL_GROUP_WIDTH)
            z = _dot(pooled[gi], wl_ref[gi], "nn")
            dyg = dy_ref[:, cs]
            dsc = jnp.sum(dyg * z, axis=0, keepdims=True)
            dz = dyg * sc_ref[:, cs]
            dwl = _dot(pooled[gi], dz, "tn")
            dpooled = _dot(dz, wl_ref[gi], "nt")
            e_ref[:, cs] = dpooled / cnts[gi]

            @pl.when(i == 0)
            def _():
                dwl_ref[gi] = dwl
                dsc_ref[:, cs] = dsc

            @pl.when(i > 0)
            def _():
                dwl_ref[gi] += dwl
                dsc_ref[:, cs] += dsc

    return pl.pallas_call(
        body, grid=(S // T,),
        in_specs=[pl.BlockSpec((T, W), lambda i: (i, 0)), _prev_halo_spec(T, W, lambda i: 0),
                  pl.BlockSpec((T, W), lambda i: (i, 0)),
                  pl.BlockSpec((4, 256, 256), lambda i: (0, 0, 0)), pl.BlockSpec((1, W), lambda i: (0, 0))],
        out_specs=[pl.BlockSpec((T, W), lambda i: (i, 0)), pl.BlockSpec((4, 256, 256), lambda i: (0, 0, 0)),
                   pl.BlockSpec((1, W), lambda i: (0, 0))],
        out_shape=[jax.ShapeDtypeStruct((S, W), F32), jax.ShapeDtypeStruct((4, 256, 256), F32),
                   jax.ShapeDtypeStruct((1, W), F32)],
        scratch_shapes=[pltpu.VMEM((HALO + T, W), F32)],
        compiler_params=_params("arbitrary"), name="pool_bwd_a")(u, u, dy, w_lin, scale)


def _pool_bwd_b(e):
    S = e.shape[0]
    T = 512
    W = POOL_WIDTH
    per = T // HALO
    n_blk = S // T

    def body(e_ref, halo_ref, du_ref, ext_ref):
        i = pl.program_id(0)
        ext_ref[0:T, :] = e_ref[...]
        ext_ref[T:, :] = jnp.where(i == n_blk - 1, 0.0, halo_ref[...])
        ext = ext_ref[...]
        n = T + HALO
        t = i * T + lax.broadcasted_iota(jnp.int32, (T, 1), 0)
        for gi, w in enumerate(POOL_WINDOWS):
            cs = slice(gi * POOL_GROUP_WIDTH, (gi + 1) * POOL_GROUP_WIDTH)
            s = ext[:, cs]
            sh = 1
            while sh < w:
                s = s + pltpu.roll(s, n - sh, 0)
                sh *= 2
            cnt = jnp.minimum(t + 1, w).astype(F32)
            du_ref[:, cs] = (s[0:T] - ext[0:T, cs] * cnt).astype(BF16)

    return pl.pallas_call(
        body, grid=(n_blk,),
        in_specs=[pl.BlockSpec((T, W), lambda i: (i, 0)),
                  pl.BlockSpec((HALO, W), lambda i: (jnp.minimum((i + 1) * per, S // HALO - 1), 0))],
        out_specs=pl.BlockSpec((T, W), lambda i: (i, 0)),
        out_shape=jax.ShapeDtypeStruct((S, W), BF16),
        scratch_shapes=[pltpu.VMEM((T + HALO, W), F32)],
        compiler_params=_params("parallel"), name="pool_bwd_b")(e, e)


def _slope(head):
    return 2.0 ** (-8.0 * (head + 1) / N_ATTN_HEADS)


def _attn_masks(b):
    row = lax.broadcasted_iota(jnp.int32, (SPAN, SPAN), 0)
    col = lax.broadcasted_iota(jnp.int32, (SPAN, SPAN), 1)
    dist_cur = (row - col).astype(F32)
    valid_cur = col <= row
    valid_prev = jnp.logical_and(col >= row, b > 0)
    return dist_cur, valid_cur, valid_prev


def _attn_tiling(gi, S):
    d = ATTN_GROUPS[gi][1]
    hp = HEADS_PER_GROUP if d < 16 else 1
    return d, SPAN * d, S // (SPAN * d), hp, hp * HEAD_DIM


def _head_slope(gi, hp, hi):
    if hp == HEADS_PER_GROUP:
        return _slope(gi * HEADS_PER_GROUP + hi)
    first = pl.program_id(1) * hp
    sl = jnp.float32(_slope(gi * HEADS_PER_GROUP + hi))
    for h0 in range(hp, HEADS_PER_GROUP, hp):
        sl = jnp.where(first == h0, jnp.float32(_slope(gi * HEADS_PER_GROUP + h0 + hi)), sl)
    return sl


def _split_heads(src_ref, dst_ref, hp):
    for hi in range(hp):
        dst_ref[hi] = src_ref[:, hi * HEAD_DIM:(hi + 1) * HEAD_DIM].astype(F32)


def _attn_fwd(gi, qkv):
    S = qkv.shape[0]
    d, C, nb, hp, W = _attn_tiling(gi, S)
    scale = HEAD_DIM ** -0.5

    def body(q_ref, kc_ref, kp_ref, vc_ref, vp_ref, o_ref, lse_ref, qs, kcs, kps, vcs, vps, os, ls):
        b = pl.program_id(0)
        for src, dst in ((q_ref, qs), (kc_ref, kcs), (kp_ref, kps), (vc_ref, vcs), (vp_ref, vps)):
            _split_heads(src, dst, hp)
        dist_cur, valid_cur, valid_prev = _attn_masks(b)

        def sub_sequence(r, carry):
            rows = pl.ds(r, SPAN, stride=d)
            for hi in range(hp):
                sl = _head_slope(gi, hp, hi) * d
                q = qs[hi, rows, :]
                s_cur = _dot(q, kcs[hi, rows, :], "nt") * scale + (-sl) * dist_cur
                s_prev = _dot(q, kps[hi, rows, :], "nt") * scale + (-sl) * (dist_cur + SPAN)
                s_cur = jnp.where(valid_cur, s_cur, NEG_BIG)
                s_prev = jnp.where(valid_prev, s_prev, NEG_BIG)
                m = jnp.maximum(jnp.max(s_cur, axis=-1, keepdims=True), jnp.max(s_prev, axis=-1, keepdims=True))
                p_cur = jnp.exp(s_cur - m)
                p_prev = jnp.exp(s_prev - m)
                l = jnp.sum(p_cur, axis=-1, keepdims=True) + jnp.sum(p_prev, axis=-1, keepdims=True)
                os[hi, rows, :] = (_dot(p_cur, vcs[hi, rows, :], "nn") + _dot(p_prev, vps[hi, rows, :], "nn")) / l
                ls[hi, rows, :] = jnp.broadcast_to(m + jnp.log(l), (SPAN, HEAD_DIM))
            return carry

        lax.fori_loop(0, d, sub_sequence, 0)
        for hi in range(hp):
            hs = slice(hi * HEAD_DIM, (hi + 1) * HEAD_DIM)
            o_ref[:, hs] = os[hi]
            lse_ref[:, hs] = ls[hi]

    per_sec = ATTN_WIDTH // W
    per_grp = ATTN_OUT_WIDTH // W

    def spec(sec, shift):
        return pl.BlockSpec((C, W), lambda b, h: (jnp.maximum(b + shift, 0), sec * per_sec + gi * per_grp + h))

    out = pl.BlockSpec((C, W), lambda b, h: (b, h))
    scr = pltpu.VMEM((hp, C, HEAD_DIM), F32)
    return pl.pallas_call(
        body, grid=(nb, HEADS_PER_GROUP // hp),
        in_specs=[spec(0, 0), spec(1, 0), spec(1, -1), spec(2, 0), spec(2, -1)], out_specs=[out, out],
        out_shape=[jax.ShapeDtypeStruct((S, ATTN_OUT_WIDTH), F32)] * 2, scratch_shapes=[scr] * 7,
        compiler_params=_params("parallel", "parallel"), name=f"attn_fwd_g{gi}")(qkv, qkv, qkv, qkv, qkv)


def _attn_bwd(gi, qkv, do, lse, dp):
    S = qkv.shape[0]
    d, C, nb, hp, W = _attn_tiling(gi, S)
    scale = HEAD_DIM ** -0.5
    n_in = 12

    def body(*refs):
        ins, (dq_ref, dk_ref, dv_ref) = refs[:n_in], refs[n_in:n_in + 3]
        scr = refs[n_in + 3:]
        for src, dst in zip(ins, scr[:n_in]):
            _split_heads(src, dst, hp)
        qs, kcs, vcs, kps, vps, qns, dos, dons, lses, lsens, dps, dpns = scr[:n_in]
        dqs, dks, dvs = scr[n_in:]
        b = pl.program_id(0)
        dist_cur, valid_cur, valid_prev = _attn_masks(b)
        row = lax.broadcasted_iota(jnp.int32, (SPAN, SPAN), 0)
        col = lax.broadcasted_iota(jnp.int32, (SPAN, SPAN), 1)
        valid_next = jnp.logical_and(col >= row, b < nb - 1)

        def sub_sequence(r, carry):
            rows = pl.ds(r, SPAN, stride=d)
            for hi in range(hp):
                sl = _head_slope(gi, hp, hi) * d
                q, kc, vc, kp, vp, qn, dov, don = (s[hi, rows, :] for s in (qs, kcs, vcs, kps, vps, qns, dos, dons))
                lse_b, lse_n, dp_b, dp_n = (s[hi, rows, :] for s in (lses, lsens, dps, dpns))
                bias_cur = (-sl) * dist_cur
                bias_far = (-sl) * (dist_cur + SPAN)
                p_cur = jnp.where(valid_cur, jnp.exp(_dot(q, kc, "nt") * scale + bias_cur - lse_b), 0.0)
                p_prev = jnp.where(valid_prev, jnp.exp(_dot(q, kp, "nt") * scale + bias_far - lse_b), 0.0)
                ds_cur = p_cur * (_dot(dov, vc, "nt") - dp_b)
                ds_prev = p_prev * (_dot(dov, vp, "nt") - dp_b)
                dqs[hi, rows, :] = (_dot(ds_cur, kc, "nn") + _dot(ds_prev, kp, "nn")) * scale
                p_next = jnp.where(valid_next, jnp.exp(_dot(qn, kc, "nt") * scale + bias_far - lse_n), 0.0)
                ds_next = p_next * (_dot(don, vc, "nt") - dp_n)
                dvs[hi, rows, :] = _dot(p_cur, dov, "tn") + _dot(p_next, don, "tn")
                dks[hi, rows, :] = (_dot(ds_cur, q, "tn") + _dot(ds_next, qn, "tn")) * scale
            return carry

        lax.fori_loop(0, d, sub_sequence, 0)
        for hi in range(hp):
            hs = slice(hi * HEAD_DIM, (hi + 1) * HEAD_DIM)
            dq_ref[:, hs] = dqs[hi].astype(BF16)
            dk_ref[:, hs] = dks[hi].astype(BF16)
            dv_ref[:, hs] = dvs[hi].astype(BF16)

    per_sec = ATTN_WIDTH // W
    per_grp = ATTN_OUT_WIDTH // W

    def qkv_spec(sec, shift):
        return pl.BlockSpec((C, W), lambda b, h: (jnp.clip(b + shift, 0, nb - 1), sec * per_sec + gi * per_grp + h))

    def act_spec(shift):
        return pl.BlockSpec((C, W), lambda b, h: (jnp.clip(b + shift, 0, nb - 1), h))

    out = act_spec(0)
    scr = pltpu.VMEM((hp, C, HEAD_DIM), F32)
    return pl.pallas_call(
        body, grid=(nb, HEADS_PER_GROUP // hp),
        in_specs=[qkv_spec(0, 0), qkv_spec(1, 0), qkv_spec(2, 0), qkv_spec(1, -1), qkv_spec(2, -1), qkv_spec(0, 1),
                  act_spec(0), act_spec(1), act_spec(0), act_spec(1), act_spec(0), act_spec(1)],
        out_specs=[out, out, out], out_shape=[jax.ShapeDtypeStruct((S, ATTN_OUT_WIDTH), BF16)] * 3,
        scratch_shapes=[scr] * (n_in + 3),
        compiler_params=_params("parallel", "parallel"), name=f"attn_bwd_g{gi}")(
            qkv, qkv, qkv, qkv, qkv, qkv, do, do, lse, lse, dp, dp)


def _group_weights(l_refs):
    l0, l1, l2 = (r[...] for r in l_refs)
    m = jnp.maximum(jnp.maximum(l0, l1), l2)
    e = [jnp.exp(l0 - m), jnp.exp(l1 - m), jnp.exp(l2 - m)]
    tot = e[0] + e[1] + e[2]
    return [ei / tot for ei in e]


def _merge_fwd(outs, lses):
    S, W = outs[0].shape
    T = 512

    def body(o0, o1, o2, l0, l1, l2, y_ref):
        w = _group_weights((l0, l1, l2))
        y_ref[...] = (w[0] * o0[...] + w[1] * o1[...] + w[2] * o2[...]).astype(BF16)

    blk = pl.BlockSpec((T, W), lambda i: (i, 0))
    return pl.pallas_call(
        body, grid=(S // T,), in_specs=[blk] * 6, out_specs=blk, out_shape=jax.ShapeDtypeStruct((S, W), BF16),
        compiler_params=_params("parallel"), name="merge_fwd")(*outs, *lses)


def _merge_bwd(dy, outs, lses):
    S, W = outs[0].shape
    T = 512

    def body(dy_ref, o0, o1, o2, l0, l1, l2, do0, do1, do2, dp0, dp1, dp2):
        w = _group_weights((l0, l1, l2))
        dyv = dy_ref[...]
        y = w[0] * o0[...] + w[1] * o1[...] + w[2] * o2[...]
        prod = dyv * y
        dots = []
        for hi in range(HEADS_PER_GROUP):
            hs = slice(hi * HEAD_DIM, (hi + 1) * HEAD_DIM)
            dots.append(jnp.broadcast_to(jnp.sum(prod[:, hs], axis=-1, keepdims=True), (T, HEAD_DIM)))
        dot_b = jnp.concatenate(dots, axis=1)
        for wg, do_ref, dp_ref in zip(w, (do0, do1, do2), (dp0, dp1, dp2)):
            do_ref[...] = (wg * dyv).astype(BF16)
            dp_ref[...] = wg * dot_b

    blk = pl.BlockSpec((T, W), lambda i: (i, 0))
    res = pl.pallas_call(
        body, grid=(S // T,), in_specs=[blk] * 7, out_specs=[blk] * 6,
        out_shape=[jax.ShapeDtypeStruct((S, W), BF16)] * 3 + [jax.ShapeDtypeStruct((S, W), F32)] * 3,
        compiler_params=_params("parallel"), name="merge_bwd")(dy, *outs, *lses)
    return res[:3], res[3:]


def _mix_fwd(pool_y, attn_y, w_pool_out_t, w_attn_out_t, gates_pre, b_gate):
    S = pool_y.shape[0]
    D = D_MODEL
    tm, tn = 512, 512
    nj = D // tn

    def body(p_ref, a_ref, wp_ref, wa_ref, g0_ref, g1_ref, b0_ref, b1_ref, mixed_ref, yp_ref, ya_ref):
        yp = _dot(p_ref[...], wp_ref[...], "nt")
        ya = _dot(a_ref[...], wa_ref[...], "nt")
        g0 = jax.nn.sigmoid(g0_ref[...] + b0_ref[...])
        g1 = jax.nn.sigmoid(g1_ref[...] + b1_ref[...])
        mixed_ref[...] = (g0 * yp + g1 * ya).astype(BF16)
        yp_ref[...] = yp.astype(BF16)
        ya_ref[...] = ya.astype(BF16)

    out = pl.BlockSpec((tm, tn), lambda i, j: (i, j))
    return pl.pallas_call(
        body, grid=(S // tm, nj),
        in_specs=[pl.BlockSpec((tm, POOL_WIDTH), lambda i, j: (i, 0)),
                  pl.BlockSpec((tm, ATTN_OUT_WIDTH), lambda i, j: (i, 0)),
                  pl.BlockSpec((tn, POOL_WIDTH), lambda i, j: (j, 0)),
                  pl.BlockSpec((tn, ATTN_OUT_WIDTH), lambda i, j: (j, 0)),
                  pl.BlockSpec((tm, tn), lambda i, j: (i, j)), pl.BlockSpec((tm, tn), lambda i, j: (i, j + nj)),
                  pl.BlockSpec((1, tn), lambda i, j: (0, j)), pl.BlockSpec((1, tn), lambda i, j: (0, j + nj))],
        out_specs=[out, out, out], out_shape=[jax.ShapeDtypeStruct((S, D), BF16)] * 3,
        compiler_params=_params("parallel", "parallel"), name="mix_fwd")(
            pool_y, attn_y, w_pool_out_t, w_attn_out_t, gates_pre, gates_pre, b_gate, b_gate)


def _mix_bwd(dmixed, gates_pre, b_gate, y_pool, y_attn):
    S = dmixed.shape[0]
    D = D_MODEL
    T = 256

    def body(dm_ref, gp_ref, b_ref, yp_ref, ya_ref, dyp_ref, dya_ref, dg_ref, db_ref):
        i = pl.program_id(0)
        dm = dm_ref[...].astype(F32)
        parts = []
        for br, (y_ref, dy_ref) in enumerate(((yp_ref, dyp_ref), (ya_ref, dya_ref))):
            cs = slice(br * D, (br + 1) * D)
            g = jax.nn.sigmoid(gp_ref[:, cs] + b_ref[:, cs])
            dy_ref[...] = (dm * g).astype(BF16)
            dpre = dm * y_ref[...].astype(F32) * g * (1.0 - g)
            dg_ref[:, cs] = dpre.astype(BF16)
            parts.append(jnp.sum(dpre, axis=0, keepdims=True))

        @pl.when(i == 0)
        def _():
            db_ref[:, 0:D] = parts[0]
            db_ref[:, D:2 * D] = parts[1]

        @pl.when(i > 0)
        def _():
            db_ref[:, 0:D] += parts[0]
            db_ref[:, D:2 * D] += parts[1]

    row = pl.BlockSpec((T, D), lambda i: (i, 0))
    row2 = pl.BlockSpec((T, 2 * D), lambda i: (i, 0))
    vec2 = pl.BlockSpec((1, 2 * D), lambda i: (0, 0))
    return pl.pallas_call(
        body, grid=(S // T,), in_specs=[row, row2, vec2, row, row], out_specs=[row, row, row2, vec2],
        out_shape=[jax.ShapeDtypeStruct((S, D), BF16), jax.ShapeDtypeStruct((S, D), BF16),
                   jax.ShapeDtypeStruct((S, 2 * D), BF16), jax.ShapeDtypeStruct((1, 2 * D), F32)],
        compiler_params=_params("arbitrary"), name="mix_bwd")(dmixed, gates_pre, b_gate, y_pool, y_attn)


_CONV_T, _CONV_C = 512, 512


def _conv_taps(ext, w_ref, b_ref):
    x2 = pltpu.roll(ext, 2, 0)[HALO:]
    x1 = pltpu.roll(ext, 1, 0)[HALO:]
    x0 = ext[HALO:]
    y = b_ref[...] + w_ref[0:1, :] * x2
    y = y + w_ref[1:2, :] * x1
    y = y + w_ref[2:3, :] * x0
    return y, (x2, x1, x0)


def _gelu_parts(a):
    cdf = 0.5 * (1.0 + lax.erf(a * (1.0 / math.sqrt(2.0))))
    return cdf, a * cdf


def _conv_specs(S):
    T, C = _CONV_T, _CONV_C
    nj = D_FF // C
    blk = pl.BlockSpec((2, T, C), lambda i, j: (0, i, j))
    per = T // HALO
    halo = pl.BlockSpec((2, HALO, C), lambda i, j: (0, jnp.maximum(i * per - 1, 0), j))
    w_a = pl.BlockSpec((3, C), lambda i, j: (0, j))
    w_b = pl.BlockSpec((3, C), lambda i, j: (0, j + nj))
    b_a = pl.BlockSpec((1, C), lambda i, j: (0, j))
    b_b = pl.BlockSpec((1, C), lambda i, j: (0, j + nj))
    return T, C, nj, blk, halo, w_a, w_b, b_a, b_b


def _conv_load(ext_ref, halo_ref, up_ref, first):
    for h in range(2):
        ext_ref[h, 0:HALO, :] = jnp.where(first, 0.0, halo_ref[h].astype(F32))
        ext_ref[h, HALO:, :] = up_ref[h].astype(F32)


def _convffn_fwd(up_pre, conv_w, conv_b):
    S = up_pre.shape[1]
    T, C, nj, blk, halo, w_a, w_b, b_a, b_b = _conv_specs(S)

    def body(up_ref, halo_ref, wa_ref, wb_ref, ba_ref, bb_ref, act_ref, ext_ref):
        _conv_load(ext_ref, halo_ref, up_ref, pl.program_id(0) == 0)
        a, _ = _conv_taps(ext_ref[0], wa_ref, ba_ref)
        b, _ = _conv_taps(ext_ref[1], wb_ref, bb_ref)
        _, gelu = _gelu_parts(a)
        act_ref[...] = (gelu * b).astype(BF16)

    return pl.pallas_call(
        body, grid=(S // T, nj), in_specs=[blk, halo, w_a, w_b, b_a, b_b],
        out_specs=pl.BlockSpec((T, C), lambda i, j: (i, j)),
        out_shape=jax.ShapeDtypeStruct((S, D_FF), BF16),
        scratch_shapes=[pltpu.VMEM((2, HALO + T, C), F32)],
        compiler_params=_params("parallel", "parallel"), name="convffn_fwd")(
            up_pre, up_pre, conv_w, conv_w, conv_b, conv_b)


def _convffn_bwd_a(up_pre, dact, conv_w, conv_b):
    S = up_pre.shape[1]
    T, C, nj, blk, halo, w_a, w_b, b_a, b_b = _conv_specs(S)
    def swap(spec):
        return pl.BlockSpec(spec.block_shape, lambda j, i, _m=spec.index_map: _m(i, j))

    def body(up_ref, halo_ref, dact_ref, wa_ref, wb_ref, ba_ref, bb_ref, dup_ref, dwa_ref, dwb_ref, dba_ref, dbb_ref,
             ext_ref):
        i = pl.program_id(1)
        _conv_load(ext_ref, halo_ref, up_ref, i == 0)
        a, xa = _conv_taps(ext_ref[0], wa_ref, ba_ref)
        b, xb = _conv_taps(ext_ref[1], wb_ref, bb_ref)
        cdf, gelu = _gelu_parts(a)
        dgelu = cdf + a * (jnp.exp(-0.5 * a * a) * (1.0 / math.sqrt(2.0 * math.pi)))
        dact = dact_ref[...].astype(F32)
        da = dact * b * dgelu
        db = dact * gelu
        dup_ref[0] = da.astype(BF16)
        dup_ref[1] = db.astype(BF16)
        for dval, xs, dw_ref, dbias_ref in ((da, xa, dwa_ref, dba_ref), (db, xb, dwb_ref, dbb_ref)):
            dw = jnp.concatenate([jnp.sum(dval * xk, axis=0, keepdims=True) for xk in xs], axis=0)
            dbias = jnp.sum(dval, axis=0, keepdims=True)

            @pl.when(i == 0)
            def _():
                dw_ref[...] = dw
                dbias_ref[...] = dbias

            @pl.when(i > 0)
            def _():
                dw_ref[...] += dw
                dbias_ref[...] += dbias

    dact_spec = pl.BlockSpec((T, C), lambda j, i: (i, j))
    dw_spec = pl.BlockSpec((3, C), lambda j, i: (0, j))
    db_spec = pl.BlockSpec((1, C), lambda j, i: (0, j))
    res = pl.pallas_call(
        body, grid=(nj, S // T),
        in_specs=[swap(blk), swap(halo), dact_spec, swap(w_a), swap(w_b), swap(b_a), swap(b_b)],
        out_specs=[swap(blk), dw_spec, dw_spec, db_spec, db_spec],
        out_shape=[jax.ShapeDtypeStruct((2, S, D_FF), BF16), jax.ShapeDtypeStruct((3, D_FF), F32),
                   jax.ShapeDtypeStruct((3, D_FF), F32), jax.ShapeDtypeStruct((1, D_FF), F32),
                   jax.ShapeDtypeStruct((1, D_FF), F32)],
        scratch_shapes=[pltpu.VMEM((2, HALO + T, C), F32)],
        compiler_params=_params("parallel", "arbitrary"), name="convffn_bwd_a")(
            up_pre, up_pre, dact, conv_w, conv_w, conv_b, conv_b)
    dup, dwa, dwb, dba, dbb = res
    dconv_w = jnp.concatenate([dwa, dwb], axis=1)
    dconv_b = jnp.concatenate([dba, dbb], axis=1)
    return dup, dconv_w, dconv_b


def _convffn_bwd_b(dup, conv_w):
    S = dup.shape[1]
    T, C = _CONV_T, _CONV_C
    nj = D_FF // C
    per = T // HALO
    n_blk = S // T

    def body(d_ref, halo_ref, wa_ref, wb_ref, o_ref, ext_ref):
        i = pl.program_id(0)
        n = T + HALO
        for h, w_ref in enumerate((wa_ref, wb_ref)):
            ext_ref[h, 0:T, :] = d_ref[h].astype(F32)
            ext_ref[h, T:, :] = jnp.where(i == n_blk - 1, 0.0, halo_ref[h].astype(F32))
            ext = ext_ref[h]
            y = w_ref[2:3, :] * ext[0:T] + w_ref[1:2, :] * pltpu.roll(ext, n - 1, 0)[0:T]
            y = y + w_ref[0:1, :] * pltpu.roll(ext, n - 2, 0)[0:T]
            o_ref[h] = y.astype(BF16)

    blk = pl.BlockSpec((2, T, C), lambda i, j: (0, i, j))
    halo = pl.BlockSpec((2, HALO, C), lambda i, j: (0, jnp.minimum((i + 1) * per, S // HALO - 1), j))
    return pl.pallas_call(
        body, grid=(n_blk, nj),
        in_specs=[blk, halo, pl.BlockSpec((3, C), lambda i, j: (0, j)), pl.BlockSpec((3, C), lambda i, j: (0, j + nj))],
        out_specs=blk, out_shape=jax.ShapeDtypeStruct((2, S, D_FF), BF16),
        scratch_shapes=[pltpu.VMEM((2, T + HALO, C), F32)],
        compiler_params=_params("parallel", "parallel"), name="convffn_bwd_b")(dup, dup, conv_w, conv_w)


def _position():
    return lax.axis_index("x"), lax.axis_index("y"), lax.axis_index("c")


def _flip(pos, k):
    x, y, c = pos
    return (1 - x if k & 4 else x, 1 - y if k & 2 else y, 1 - c if k & 1 else c)


def _index(pos):
    return 4 * pos[0] + 2 * pos[1] + pos[2]


def _all_gather_weights(shards):
    n = len(shards)

    def body(*refs):
        ins, outs = refs[:n], refs[n:2 * n]
        send_sems, recv_sems, local_sems = refs[2 * n:]
        me = _position()
        x, y, c = me
        sibling = (x, y, 1 - c)
        chips = [(1 - x, y), (x, 1 - y), (1 - x, 1 - y)]

        def copy(a, k, block, to, own=False):
            slot = outs[a].at[_index(block)]
            return pltpu.make_async_remote_copy(
                src_ref=ins[a] if own else slot, dst_ref=slot, send_sem=send_sems.at[7 * a + k],
                recv_sem=recv_sems.at[7 * a + k], device_id=to, device_id_type=MESH)

        mines = [pltpu.make_async_copy(ins[a], outs[a].at[_index(me)], local_sems.at[a]) for a in range(n)]
        started = []
        for a in range(n):
            mines[a].start()
            first = [copy(a, 0, me, sibling, own=True)]
            first += [copy(a, 1 + j, me, (*chip, c), own=True) for j, chip in enumerate(chips)]
            for cp in first:
                cp.start()
            started += first
        for j, chip in enumerate(chips):
            for a in range(n):
                copy(a, 1 + j, (*chip, c), me).wait_recv()
                passed = copy(a, 4 + j, (*chip, c), sibling)
                passed.start()
                started.append(passed)
        for a in range(n):
            copy(a, 0, sibling, me).wait_recv()
            for j, chip in enumerate(chips):
                copy(a, 4 + j, (*chip, 1 - c), me).wait_recv()
        for cp in started:
            cp.wait_send()
        for cp in mines:
            cp.wait()

    hbm = pl.BlockSpec(memory_space=pl.ANY)
    return pl.pallas_call(
        body, out_shape=[jax.ShapeDtypeStruct((N_DEV,) + s.shape, s.dtype) for s in shards],
        in_specs=[hbm] * n, out_specs=[hbm] * n,
        scratch_shapes=[pltpu.SemaphoreType.DMA((7 * n,)), pltpu.SemaphoreType.DMA((7 * n,)),
                        pltpu.SemaphoreType.DMA((n,))],
        name="all_gather_weights")(*shards)


_HBM = pl.BlockSpec(memory_space=pltpu.HBM)
_SEM = pl.BlockSpec(memory_space=pltpu.SEMAPHORE)
_ANY = pl.BlockSpec(memory_space=pl.ANY)
_EFFECT = pltpu.SideEffectType.DATAFLOW_SIDE_EFFECTING


def _in_hbm(a):
    return pltpu.with_memory_space_constraint(a, pltpu.HBM)


def _split_copies(name, srcs, lands, n_copies, plan, start_after, wait_after):
    ns, nl = len(srcs), len(lands)
    start_after = tuple(start_after)
    na = len(start_after)

    def start_body(*refs):
        src_refs, land_refs = refs[:ns], refs[ns:ns + nl]
        send_sems, recv_sems = refs[ns + nl + na], refs[ns + nl + na + 1]
        token = refs[-1]
        for k, (src, dst, _, peer) in enumerate(plan(src_refs, land_refs)):
            pltpu.make_async_remote_copy(src_ref=src, dst_ref=dst, send_sem=send_sems.at[k], recv_sem=recv_sems.at[k],
                                         device_id=peer, device_id_type=MESH).start()
        token[...] = jnp.zeros_like(token)

    thru = [pltpu.HBM(a.shape, a.dtype) for a in list(srcs) + list(lands)]
    res = pl.pallas_call(
        start_body, name=name + "_start",
        out_shape=[pltpu.SemaphoreType.DMA((n_copies,)), pltpu.SemaphoreType.DMA((n_copies,))] + thru
        + [jax.ShapeDtypeStruct((8, 128), F32)],
        in_specs=[_HBM] * (ns + nl) + [_ANY] * na,
        out_specs=[_SEM, _SEM] + [_HBM] * (ns + nl) + [pl.BlockSpec(memory_space=pltpu.VMEM)],
        input_output_aliases={i: 2 + i for i in range(ns + nl)},
        compiler_params=pltpu.CompilerParams(has_side_effects=_EFFECT),
    )(*[_in_hbm(a) for a in list(srcs) + list(lands)], *start_after)
    send_sems, recv_sems, token = res[0], res[1], res[-1]
    thru_vals = res[2:-1]

    def wait():
        def wait_body(*refs):
            src_refs, land_refs = refs[:ns], refs[ns:ns + nl]
            s_sems, r_sems = refs[ns + nl], refs[ns + nl + 1]
            for k, (src, _, dst, peer) in enumerate(plan(src_refs, land_refs)):
                cp = pltpu.make_async_remote_copy(src_ref=src, dst_ref=dst, send_sem=s_sems.at[k], recv_sem=r_sems.at[k],
                                                  device_id=peer, device_id_type=MESH)
                cp.wait_send()
                cp.wait_recv()

        out = pl.pallas_call(
            wait_body, name=name + "_wait", out_shape=thru,
            in_specs=[_HBM] * (ns + nl) + [_SEM, _SEM, _ANY], out_specs=[_HBM] * (ns + nl),
            input_output_aliases={i: i for i in range(ns + nl)},
            compiler_params=pltpu.CompilerParams(has_side_effects=_EFFECT),
        )(*thru_vals, send_sems, recv_sems, wait_after())
        return list(out[:ns]), list(out[ns:])

    return wait, token


def _gather_direct_plan(n):
    def plan(src_refs, land_refs):
        me = _position()
        peers = [_flip(me, k) for k in range(1, N_DEV)]
        return [(src_refs[a], land_refs[a].at[_index(me)], land_refs[a].at[_index(p)], p)
                for a in range(n) for p in peers]
    return plan


def _place_own(shards):
    me = _index(_position())
    return [lax.dynamic_update_slice(lax.empty((N_DEV,) + s.shape, s.dtype), s[None], (me, 0, 0)) for s in shards]


def _chip_index(pos):
    return 2 * pos[0] + pos[1]


def _exchange_d2d(name, parts):
    n = len(parts)

    def body(*refs):
        ins, outs = refs[:n], refs[n:2 * n]
        send_sems, recv_sems = refs[2 * n:]
        x, y, c = _position()
        sibling = (x, y, 1 - c)

        def copy(a, q, core):
            return pltpu.make_async_remote_copy(
                src_ref=ins[a].at[2 * q + core], dst_ref=outs[a].at[q], send_sem=send_sems.at[4 * a + q],
                recv_sem=recv_sems.at[4 * a + q], device_id=sibling, device_id_type=MESH)

        sent = [copy(a, q, 1 - c) for a in range(n) for q in range(4)]
        for cp in sent:
            cp.start()
        for cp in sent:
            cp.wait_recv()
        for cp in sent:
            cp.wait_send()

    return pl.pallas_call(
        body, out_shape=[jax.ShapeDtypeStruct((4,) + p.shape[1:], p.dtype) for p in parts],
        in_specs=[_ANY] * n, out_specs=[_ANY] * n,
        scratch_shapes=[pltpu.SemaphoreType.DMA((4 * n,)), pltpu.SemaphoreType.DMA((4 * n,))],
        name=name)(*parts)


def _pair_sum(name, part, got):
    _, R, C = part.shape
    T = _row_tile(R, 16)

    def body(p_ref, g_ref, s_out, l_out):
        c = lax.axis_index("c")
        mine = jnp.where(c == 0, p_ref[0, 0].astype(F32), p_ref[0, 1].astype(F32))
        s = (mine + g_ref[0].astype(F32)).astype(BF16)
        s_out[0] = s
        l_out[0] = s

    out = pl.BlockSpec((1, T, C), lambda q, i: (q, i, 0))
    return pl.pallas_call(
        body, grid=(4, R // T),
        in_specs=[pl.BlockSpec((1, 2, T, C), lambda q, i: (q, 0, i, 0)), out], out_specs=[out, out],
        out_shape=[jax.ShapeDtypeStruct((4, R, C), BF16)] * 2,
        compiler_params=_params("parallel", "parallel"), name=name)(part.reshape(4, 2, R, C), got)


def _exchange_ici_plan(n):
    def plan(src_refs, land_refs):
        me = _position()
        x, y, c = me
        peers = [(1 - x, y, c), (x, 1 - y, c), (1 - x, 1 - y, c)]
        return [(src_refs[a].at[_chip_index(p)], land_refs[a].at[_chip_index(me)], land_refs[a].at[_chip_index(p)], p)
                for a in range(n) for p in peers]
    return plan


def _adamw(w, g, m, v):
    m = ADAM_B1 * m + (1.0 - ADAM_B1) * g
    v = ADAM_B2 * v + (1.0 - ADAM_B2) * (g * g)
    m_hat = m / (1.0 - ADAM_B1 ** ADAM_STEP)
    v_hat = v / (1.0 - ADAM_B2 ** ADAM_STEP)
    delta = -ADAM_LR * (m_hat / (jnp.sqrt(v_hat) + ADAM_EPS) + ADAM_WD * w)
    return delta, m, v


def _row_tile(rows, unit, cap=256):
    best = rows
    for t in range(unit, min(rows, cap) + 1, unit):
        if rows % t == 0:
            best = t
    return best


def _sum_partials(name, recv):
    n_src, R, C = recv.shape
    T = _row_tile(R, 16)

    def body(r_ref, g_out):
        g = r_ref[0].astype(F32)
        for s in range(1, n_src):
            g = g + r_ref[s].astype(F32)
        g_out[...] = g

    return pl.pallas_call(
        body, grid=(R // T,), in_specs=[pl.BlockSpec((n_src, T, C), lambda i: (0, i, 0))],
        out_specs=pl.BlockSpec((T, C), lambda i: (i, 0)), out_shape=jax.ShapeDtypeStruct((R, C), F32),
        compiler_params=_params("parallel"), name=name)(recv)


def _adamw_update(name, w, g, m, v):
    R, C = w.shape
    T = _row_tile(R, 8)

    def body(w_ref, g_ref, m_ref, v_ref, d_out, m_out, v_out):
        delta, mn, vn = _adamw(w_ref[...], g_ref[...], m_ref[...], v_ref[...])
        d_out[...] = delta
        m_out[...] = mn
        v_out[...] = vn

    blk = pl.BlockSpec((T, C), lambda i: (i, 0))
    return pl.pallas_call(
        body, grid=(R // T,), in_specs=[blk] * 4, out_specs=[blk] * 3,
        out_shape=[jax.ShapeDtypeStruct((R, C), F32)] * 3, compiler_params=_params("parallel"), name=name)(w, g, m, v)


def _gather_vmem(p_ref, gath, send_sems, recv_sems):
    me = _position()
    gath[_index(me)] = p_ref[...]
    copies = []
    for k in range(1, N_DEV):
        peer = _flip(me, k)
        copies.append(pltpu.make_async_remote_copy(
            src_ref=p_ref, dst_ref=gath.at[_index(me)], send_sem=send_sems.at[k - 1],
            recv_sem=recv_sems.at[k - 1], device_id=peer, device_id_type=MESH))
    for cp in copies:
        cp.start()
    for k in range(1, N_DEV):
        peer = _flip(me, k)
        pltpu.make_async_remote_copy(
            src_ref=p_ref, dst_ref=gath.at[_index(peer)], send_sem=send_sems.at[k - 1],
            recv_sem=recv_sems.at[k - 1], device_id=peer, device_id_type=MESH).wait_recv()
    for cp in copies:
        cp.wait_send()


def _all_gather_small(p):
    R, C = p.shape

    def body(p_ref, out_ref, send_sems, recv_sems):
        _gather_vmem(p_ref, out_ref, send_sems, recv_sems)

    vm = pl.BlockSpec(memory_space=pltpu.VMEM)
    return pl.pallas_call(
        body, in_specs=[vm], out_specs=vm, out_shape=jax.ShapeDtypeStruct((N_DEV, R, C), F32),
        scratch_shapes=[pltpu.SemaphoreType.DMA((7,)), pltpu.SemaphoreType.DMA((7,))],
        name="all_gather_small")(p)


def _small_allreduce_adamw(part, w, m, v):
    R, C = part.shape

    def body(p_ref, w_ref, m_ref, v_ref, g_out, d_out, m_out, v_out, gath, send_sems, recv_sems):
        _gather_vmem(p_ref, gath, send_sems, recv_sems)
        g = gath[0]
        for s in range(1, N_DEV):
            g = g + gath[s]
        delta, mn, vn = _adamw(w_ref[...], g, m_ref[...], v_ref[...])
        g_out[...] = g
        d_out[...] = delta
        m_out[...] = mn
        v_out[...] = vn

    vm = pl.BlockSpec(memory_space=pltpu.VMEM)
    return pl.pallas_call(
        body, in_specs=[vm] * 4, out_specs=[vm] * 4, out_shape=[jax.ShapeDtypeStruct((R, C), F32)] * 4,
        scratch_shapes=[pltpu.VMEM((N_DEV, R, C), F32), pltpu.SemaphoreType.DMA((7,)), pltpu.SemaphoreType.DMA((7,))],
        name="small_allreduce_adamw")(part, w, m, v)


_BIG = ("w_in", "w_up", "w_down", "w_out", "w_pool_out", "w_attn_out")
_TRANSPOSED = ("w_in", "w_up", "w_pool_out", "w_attn_out")
_SMALL = (("g_mix", 2048), ("b_gate", 4096), ("pool_scale", 1024), ("g_ffn", 2048), ("conv_b", 11264), ("g_final", 2048))
CONV_ROWS = 8


def _to_comm(name, shard):
    return shard.T if name in _TRANSPOSED else shard


def _pack_small(vals):
    flat = [vals[n].reshape(-1) for n, _ in _SMALL]
    used = sum(f.shape[0] for f in flat)
    flat.append(jnp.zeros((SMALL_ROWS * PACK_COLS - used,), F32))
    return jnp.concatenate(flat).reshape(SMALL_ROWS, PACK_COLS)


def _unpack_small(packed, like):
    flat = packed.reshape(-1)
    out, off = {}, 0
    for name, n in _SMALL:
        out[name] = flat[off:off + n].reshape(like[name].shape)
        off += n
    return out


def _local_step(x, target, W, small, first_token=0.0, late_weights=None, emit=None):
    S = x.shape[0]
    o_qkv, o_gate = POOL_WIDTH, POOL_WIDTH + QKV_WIDTH
    if emit is None:
        emit = lambda grads: None

    def zero(token):
        return 0.0 if token is None else token[0, 0]
    w_in_t = W["w_in"]

    h1 = _rms_fwd("rms1_fwd", x, small["g_mix"] + first_token)
    u = _mm_nt("proj_u", h1, w_in_t, tm=1024, tn=512, tk=2048, o_dtype=F32, n_rows=POOL_WIDTH, b_row_off=0)
    qkv = _mm_nt("proj_qkv", h1, w_in_t, tm=1024, tn=512, tk=2048, o_dtype=BF16, n_rows=QKV_WIDTH, b_row_off=o_qkv)
    gates_pre = _mm_nt("proj_gates", h1, w_in_t, tm=1024, tn=512, tk=2048, o_dtype=F32, n_rows=2 * D_MODEL,
                       b_row_off=o_gate)
    pool_y = _pool_fwd(u, W["w_pool_lin"], small["pool_scale"])
    outs, lses = zip(*[_attn_fwd(gi, qkv) for gi in range(3)])
    attn_y = _merge_fwd(outs, lses)
    if late_weights is not None:
        W = dict(W, **late_weights(attn_y))
    w_up_t = W["w_up"]
    mixed, y_pool, y_attn = _mix_fwd(pool_y, attn_y, W["w_pool_out"], W["w_attn_out"], gates_pre, small["b_gate"])
    x1 = _mm_nn("out_proj", mixed, W["w_out"], tm=1024, tn=512, tk=2048, o_dtype=F32, res=x)

    h2 = _rms_fwd("rms2_fwd", x1, small["g_ffn"])
    nj = D_FF // 512
    up_pre = _mm("up_proj", h2, w_up_t, mode="nt", grid=(S // 1024, 2 * nj, 1),
                 a_spec=pl.BlockSpec((1024, D_MODEL), lambda i, j, k: (i, 0)),
                 b_spec=pl.BlockSpec((512, D_MODEL), lambda i, j, k: (j, 0)),
                 o_spec=pl.BlockSpec((None, 1024, 512), lambda i, j, k: (j // nj, i, j % nj)),
                 o_shape=(2, S, D_FF), o_dtype=BF16, acc_shape=(1024, 512))
    act = _convffn_fwd(up_pre, small["conv_w_full"], small["conv_b"])
    x2 = _mm_nn("down_proj", act, W["w_down"], tm=1024, tn=512, tk=D_FF // 2, o_dtype=F32, res=x1)

    dx2, dx2_b, dg_final, sq_cols = _final_loss("final_loss", x2, small["g_final"], target)

    dact = _mm_nt("d_act", dx2_b, W["w_down"], tm=1024, tn=512, tk=2048, o_dtype=BF16)
    dw_down = _mm_tn("dw_down", act, dx2_b, tm=512, tn=2048, tk=TOKEN_TK)
    tok = emit(dict(w_down=dw_down))
    dup, dconv_w, dconv_b = _convffn_bwd_a(up_pre, dact, small["conv_w_full"], small["conv_b"] + zero(tok))
    dup_pre = _convffn_bwd_b(dup, small["conv_w_full"])
    tkh = D_FF // 2
    dh2 = _mm("d_h2", dup_pre, w_up_t, mode="nn", grid=(S // 1024, D_MODEL // 1024, 4),
              a_spec=pl.BlockSpec((None, 1024, tkh), lambda i, j, k: (k // 2, i, k % 2)),
              b_spec=pl.BlockSpec((tkh, 1024), lambda i, j, k: (k, j)),
              o_spec=pl.BlockSpec((1024, 1024), lambda i, j, k: (i, j)),
              o_shape=(S, D_MODEL), o_dtype=F32, acc_shape=(1024, 1024))
    dw_up = _mm("dw_up", dup_pre, h2, mode="tn", grid=(2 * nj, 1, S // TOKEN_TK),
                a_spec=pl.BlockSpec((None, TOKEN_TK, 512), lambda i, j, k: (i // nj, k, i % nj)),
                b_spec=pl.BlockSpec((TOKEN_TK, D_MODEL), lambda i, j, k: (k, 0)),
                o_spec=pl.BlockSpec((512, D_MODEL), lambda i, j, k: (i, 0)),
                o_shape=(2 * D_FF, D_MODEL), o_dtype=BF16, acc_shape=(512, D_MODEL))
    tok = emit(dict(w_up=dw_up, conv_w=dconv_w))
    dx1, dx1_b, dg_ffn = _rms_bwd("rms2_bwd", x1, small["g_ffn"] + zero(tok), dh2, dx2)

    dmixed = _mm_nt("d_mixed", dx1_b, W["w_out"], tm=1024, tn=512, tk=2048, o_dtype=BF16)
    dw_out = _mm_tn("dw_out", mixed, dx1_b, tm=512, tn=2048, tk=TOKEN_TK)
    tok = emit(dict(w_out=dw_out))
    dy_pool, dy_attn, dgates, db_gate = _mix_bwd(dmixed, gates_pre, small["b_gate"] + zero(tok), y_pool, y_attn)
    dpool_y = _mm_nn("d_pool_y", dy_pool, W["w_pool_out"], tm=1024, tn=512, tk=2048, o_dtype=F32)
    dw_pool_out = _mm_tn("dw_pool_out", dy_pool, pool_y, tm=1024, tn=1024, tk=TOKEN_TK)
    dattn_y = _mm_nn("d_attn_y", dy_attn, W["w_attn_out"], tm=1024, tn=512, tk=2048, o_dtype=F32)
    dw_attn_out = _mm_tn("dw_attn_out", dy_attn, attn_y, tm=1024, tn=512, tk=TOKEN_TK)
    e, dw_pool_lin, dpool_scale = _pool_bwd_a(u, dpool_y, W["w_pool_lin"], small["pool_scale"])
    du = _pool_bwd_b(e)
    dos, dps = _merge_bwd(dattn_y, outs, lses)
    dqkv = [_attn_bwd(gi, qkv, dos[gi], lses[gi], dps[gi]) for gi in range(3)]
    dproj = jnp.concatenate([du] + [dqkv[gi][sec] for sec in range(3) for gi in range(3)] + [dgates], axis=1)
    dw_in = _mm_tn("dw_in", dproj, h1, tm=512, tn=2048, tk=TOKEN_TK)
    tok = emit(dict(w_in=dw_in, w_pool_out=dw_pool_out, w_attn_out=dw_attn_out, w_pool_lin=dw_pool_lin))
    dh1 = _mm_nn("d_h1", dproj, w_in_t, tm=1024, tn=1024, tk=IN_WIDTH // 4, o_dtype=F32,
                 after=tok)
    grad_x, _, dg_mix = _rms_bwd("rms1_bwd", x, small["g_mix"] + zero(tok), dh1, dx1)

    grads = dict(w_in=dw_in, w_up=dw_up, w_down=dw_down, w_out=dw_out, w_pool_out=dw_pool_out,
                 w_attn_out=dw_attn_out, w_pool_lin=dw_pool_lin, conv_w=dconv_w)
    small_grads = dict(g_mix=dg_mix, b_gate=db_gate, pool_scale=dpool_scale, g_ffn=dg_ffn, conv_b=dconv_b,
                       g_final=dg_final)
    return sq_cols, grad_x, grads, small_grads


def kernel(x, g_mix, w_in, b_gate, w_pool_lin, pool_scale, w_pool_out, w_attn_out, w_out, g_ffn, w_up, conv_w, conv_b, w_down, g_final, loss_target, m_g_mix, m_w_in, m_b_gate, m_w_pool_lin, m_pool_scale, m_w_pool_out, m_w_attn_out, m_w_out, m_g_ffn, m_w_up, m_conv_w, m_conv_b, m_w_down, m_g_final, v_g_mix, v_w_in, v_b_gate, v_w_pool_lin, v_pool_scale, v_w_pool_out, v_w_attn_out, v_w_out, v_g_ffn, v_w_up, v_conv_w, v_conv_b, v_w_down, v_g_final):
    given = dict(g_mix=g_mix, w_in=w_in, b_gate=b_gate, w_pool_lin=w_pool_lin, pool_scale=pool_scale,
                 w_pool_out=w_pool_out, w_attn_out=w_attn_out, w_out=w_out, g_ffn=g_ffn, w_up=w_up, conv_w=conv_w,
                 conv_b=conv_b, w_down=w_down, g_final=g_final)
    mom_m = dict(g_mix=m_g_mix, w_in=m_w_in, b_gate=m_b_gate, w_pool_lin=m_w_pool_lin, pool_scale=m_pool_scale,
                 w_pool_out=m_w_pool_out, w_attn_out=m_w_attn_out, w_out=m_w_out, g_ffn=m_g_ffn, w_up=m_w_up,
                 conv_w=m_conv_w, conv_b=m_conv_b, w_down=m_w_down, g_final=m_g_final)
    mom_v = dict(g_mix=v_g_mix, w_in=v_w_in, b_gate=v_b_gate, w_pool_lin=v_w_pool_lin, pool_scale=v_pool_scale,
                 w_pool_out=v_w_pool_out, w_attn_out=v_w_attn_out, w_out=v_w_out, g_ffn=v_g_ffn, w_up=v_w_up,
                 conv_w=v_conv_w, conv_b=v_conv_b, w_down=v_w_down, g_final=v_g_final)
    def full(g):
        return g.reshape(N_DEV * g.shape[1], g.shape[2])

    shard = {n: _to_comm(n, given[n][0]).astype(BF16) for n in _BIG}
    W = dict(w_in=full(_all_gather_weights([shard["w_in"]])[0]))
    rest = [n for n in _BIG if n != "w_in"]
    lin_all = _all_gather_small(w_pool_lin[0].reshape(4 * 32, POOL_GROUP_WIDTH))
    W["w_pool_lin"] = jnp.moveaxis(lin_all.reshape(N_DEV, 4, 32, POOL_GROUP_WIDTH), 0, 1).reshape(
        4, POOL_GROUP_WIDTH, POOL_GROUP_WIDTH).astype(BF16)
    conv_w_all = _all_gather_small(jnp.pad(conv_w[0], ((0, CONV_ROWS - 3), (0, 0))))
    conv_w_full = jnp.moveaxis(conv_w_all[:, :3, :], 0, 1).reshape(3, 2 * D_FF)

    rest_wait, rest_token = _split_copies(
        "gather_rest", [shard[n] for n in rest], _place_own([shard[n] for n in rest]), 7 * len(rest),
        _gather_direct_plan(len(rest)), start_after=(W["w_in"], lin_all, conv_w_all),
        wait_after=lambda: rest_after[0])
    rest_after = []

    def late_weights(after):
        rest_after.append(after)
        return {n: full(l) for n, l in zip(rest, rest_wait()[1])}

    small = dict(g_mix=g_mix, b_gate=b_gate, pool_scale=pool_scale, g_ffn=g_ffn, conv_b=conv_b,
                 g_final=g_final.reshape(1, D_MODEL), conv_w_full=conv_w_full)

    pending = []
    done = {}

    def emit(group):
        names = list(group)
        parts = []
        for n in names:
            g = group[n]
            if n == "w_pool_lin":
                g = jnp.moveaxis(g.reshape(4, N_DEV, 32, POOL_GROUP_WIDTH), 1, 0).reshape(
                    N_DEV, 4 * 32, POOL_GROUP_WIDTH).astype(BF16)
            elif n == "conv_w":
                g = jnp.pad(jnp.moveaxis(g.reshape(3, N_DEV, 2 * D_FF // N_DEV), 1, 0),
                            ((0, 0), (0, CONV_ROWS - 3), (0, 0))).astype(BF16)
            else:
                g = g.reshape(N_DEV, g.shape[0] // N_DEV, g.shape[1])
            parts.append(g)
        tag = names[0]
        got = _exchange_d2d("exchange_d2d_" + tag, parts)
        sums, lands = zip(*[_pair_sum("pair_sum_" + n, p, r) for n, p, r in zip(names, parts, got)])
        wait, token = _split_copies("exchange_ici_" + tag, sums, lands, 3 * len(names), _exchange_ici_plan(len(names)),
                                    start_after=(got[0],), wait_after=lambda: done["grad_x"])
        pending.append((names, wait))
        return token

    sq_cols, grad_x, _, small_grads = _local_step(x[0], loss_target[0], W, small, first_token=rest_token[0, 0],
                                                  late_weights=late_weights, emit=emit)
    done["grad_x"] = grad_x
    loss = lax.psum(0.5 / D_MODEL * jnp.sum(sq_cols), AXES)

    res = {}
    for n, r in [(n, r) for names, wait in pending for n, r in zip(names, wait()[1])]:
        g = _sum_partials("sum_" + n, r)
        if n in _TRANSPOSED:
            g = g.T
        elif n == "conv_w":
            g = g[:3]
        shape = given[n].shape
        rows = math.prod(shape[:-1])
        w2, m2, v2 = (d[n].reshape(rows, shape[-1]) for d in (given, mom_m, mom_v))
        upd = _adamw_update("adamw_" + n, w2, g.reshape(rows, shape[-1]), m2, v2)
        res[n] = [a.reshape(shape) for a in (g,) + tuple(upd)]

    packed_small = [_pack_small(d) for d in (given, mom_m, mom_v)]
    res_small = [_unpack_small(p, given) for p in _small_allreduce_adamw(_pack_small(small_grads), *packed_small)]

    order = ["g_mix", "w_in", "b_gate", "w_pool_lin", "pool_scale", "w_pool_out", "w_attn_out", "w_out", "g_ffn",
             "w_up", "conv_w", "conv_b", "w_down", "g_final"]
    outs = [loss, grad_x[None]]
    for kind in range(4):
        for n in order:
            outs.append(res[n][kind] if n in res else res_small[kind][n])
    return tuple(outs)
```

```python
import math

import jax
import jax.numpy as jnp
from jax import lax
from jax.experimental import pallas as pl
from jax.experimental.pallas import tpu as pltpu

F32 = jnp.float32
BF16 = jnp.bfloat16
MESH = pl.DeviceIdType.MESH
AXES = ("x", "y", "c")
N_DEV = 8

D_MODEL = 2048
POOL_WINDOWS = (2, 4, 8, 16)
POOL_GROUP_WIDTH = 256
POOL_WIDTH = 1024
ATTN_GROUPS = ((128, 1), (512, 4), (2048, 16))
SPAN = 128
HEADS_PER_GROUP = 4
N_ATTN_HEADS = 12
HEAD_DIM = 128
ATTN_WIDTH = 1536
QKV_WIDTH = 3 * ATTN_WIDTH
ATTN_OUT_WIDTH = 512
IN_WIDTH = 9728
D_FF = 5632
RMS_EPS = 1e-6
ADAM_LR, ADAM_B1, ADAM_B2, ADAM_EPS, ADAM_WD, ADAM_STEP = 0.001, 0.9, 0.999, 1e-08, 0.01, 10

VMEM_LIMIT_BYTES = 48 * 1024 * 1024
HALO = 16
TOKEN_TK = 2048
PACK_COLS = 1024
SMALL_ROWS = 24
NEG_BIG = -1e30

_DN = {"nn": (((1,), (0,)), ((), ())), "nt": (((1,), (1,)), ((), ())), "tn": (((0,), (0,)), ((), ()))}


def _params(*sem):
    return pltpu.CompilerParams(dimension_semantics=sem, vmem_limit_bytes=VMEM_LIMIT_BYTES)


def _dot(a, b, mode):
    return lax.dot_general(a.astype(BF16), b.astype(BF16), _DN[mode], preferred_element_type=F32)


def _mm(name, a, b, *, mode, grid, a_spec, b_spec, o_spec, o_shape, o_dtype, acc_shape, res=None, res_spec=None,
        after=None):
    nk = grid[2]
    n_in = 2 + (res is not None) + (after is not None)

    def body(*refs):
        a_ref, b_ref = refs[:2]
        r_ref = refs[2] if res is not None else None
        o_ref = refs[n_in]
        scr = refs[n_in + 1:]
        prod = _dot(a_ref[...], b_ref[...], mode)

        def finish(val):
            if r_ref is not None:
                val = val + r_ref[...]
            o_ref[...] = val.astype(o_ref.dtype)

        if nk == 1:
            finish(prod)
        else:
            acc = scr[0]
            k = pl.program_id(2)

            @pl.when(k == 0)
            def _():
                acc[...] = prod

            @pl.when(k > 0)
            def _():
                acc[...] += prod

            @pl.when(k == nk - 1)
            def _():
                finish(acc[...])

    in_specs = [a_spec, b_spec]
    args = [a, b]
    if res is not None:
        in_specs.append(res_spec)
        args.append(res)
    if after is not None:
        in_specs.append(pl.BlockSpec((8, 128), lambda i, j, k: (0, 0)))
        args.append(after)
    return pl.pallas_call(
        body, grid=grid, in_specs=in_specs, out_specs=o_spec,
        out_shape=jax.ShapeDtypeStruct(o_shape, o_dtype),
        scratch_shapes=[] if nk == 1 else [pltpu.VMEM(acc_shape, F32)],
        compiler_params=_params("parallel", "parallel", "arbitrary"), name=name)(*args)


def _mm_nn(name, a, b, *, tm, tn, tk, o_dtype, n_cols=None, b_col_off=0, res=None, after=None):
    M, K = a.shape
    N = b.shape[1] if n_cols is None else n_cols
    off = b_col_off // tn
    return _mm(name, a, b, mode="nn", grid=(M // tm, N // tn, K // tk),
               a_spec=pl.BlockSpec((tm, tk), lambda i, j, k: (i, k)),
               b_spec=pl.BlockSpec((tk, tn), lambda i, j, k: (k, j + off)),
               o_spec=pl.BlockSpec((tm, tn), lambda i, j, k: (i, j)),
               o_shape=(M, N), o_dtype=o_dtype, acc_shape=(tm, tn), res=res,
               res_spec=pl.BlockSpec((tm, tn), lambda i, j, k: (i, j)), after=after)


def _mm_nt(name, a, b, *, tm, tn, tk, o_dtype, n_rows=None, b_row_off=0):
    M, K = a.shape
    N = b.shape[0] if n_rows is None else n_rows
    off = b_row_off // tn
    return _mm(name, a, b, mode="nt", grid=(M // tm, N // tn, K // tk),
               a_spec=pl.BlockSpec((tm, tk), lambda i, j, k: (i, k)),
               b_spec=pl.BlockSpec((tn, tk), lambda i, j, k: (j + off, k)),
               o_spec=pl.BlockSpec((tm, tn), lambda i, j, k: (i, j)),
               o_shape=(M, N), o_dtype=o_dtype, acc_shape=(tm, tn))


def _mm_tn(name, a, b, *, tm, tn, tk):
    K, M = a.shape
    N = b.shape[1]
    return _mm(name, a, b, mode="tn", grid=(M // tm, N // tn, K // tk),
               a_spec=pl.BlockSpec((tk, tm), lambda i, j, k: (k, i)),
               b_spec=pl.BlockSpec((tk, tn), lambda i, j, k: (k, j)),
               o_spec=pl.BlockSpec((tm, tn), lambda i, j, k: (i, j)),
               o_shape=(M, N), o_dtype=BF16, acc_shape=(tm, tn))


def _rms_fwd(name, x, g):
    S, D = x.shape
    T = 512

    def body(x_ref, g_ref, h_ref):
        xv = x_ref[...]
        r = lax.rsqrt(jnp.mean(xv * xv, axis=-1, keepdims=True) + RMS_EPS)
        h_ref[...] = (xv * r * g_ref[...]).astype(BF16)

    return pl.pallas_call(
        body, grid=(S // T,),
        in_specs=[pl.BlockSpec((T, D), lambda i: (i, 0)), pl.BlockSpec((1, D), lambda i: (0, 0))],
        out_specs=pl.BlockSpec((T, D), lambda i: (i, 0)),
        out_shape=jax.ShapeDtypeStruct((S, D), BF16), compiler_params=_params("parallel"), name=name)(x, g)


def _rms_bwd(name, x, g, dh, dres):
    S, D = x.shape
    T = 256

    def body(x_ref, g_ref, dh_ref, dres_ref, dx_ref, dxb_ref, dg_ref):
        i = pl.program_id(0)
        xv = x_ref[...]
        r = lax.rsqrt(jnp.mean(xv * xv, axis=-1, keepdims=True) + RMS_EPS)
        xhat = xv * r
        dhv = dh_ref[...].astype(F32)
        gdh = dhv * g_ref[...]
        c = jnp.mean(xhat * gdh, axis=-1, keepdims=True)
        dx = dres_ref[...] + r * (gdh - xhat * c)
        dx_ref[...] = dx
        dxb_ref[...] = dx.astype(BF16)
        part = jnp.sum(dhv * xhat, axis=0, keepdims=True)

        @pl.when(i == 0)
        def _():
            dg_ref[...] = part

        @pl.when(i > 0)
        def _():
            dg_ref[...] += part

    row = pl.BlockSpec((T, D), lambda i: (i, 0))
    vec = pl.BlockSpec((1, D), lambda i: (0, 0))
    return pl.pallas_call(
        body, grid=(S // T,), in_specs=[row, vec, row, row], out_specs=[row, row, vec],
        out_shape=[jax.ShapeDtypeStruct((S, D), F32), jax.ShapeDtypeStruct((S, D), BF16),
                   jax.ShapeDtypeStruct((1, D), F32)],
        compiler_params=_params("arbitrary"), name=name)(x, g, dh, dres)


def _final_loss(name, x, g, target):
    S, D = x.shape
    T = 256

    def body(x_ref, g_ref, t_ref, dx_ref, dxb_ref, dg_ref, sq_ref):
        i = pl.program_id(0)
        xv = x_ref[...]
        r = lax.rsqrt(jnp.mean(xv * xv, axis=-1, keepdims=True) + RMS_EPS)
        xhat = xv * r
        err = xhat * g_ref[...] - t_ref[...]
        dy = err * (1.0 / D)
        gdy = dy * g_ref[...]
        c = jnp.mean(xhat * gdy, axis=-1, keepdims=True)
        dx = r * (gdy - xhat * c)
        dx_ref[...] = dx
        dxb_ref[...] = dx.astype(BF16)
        dg_part = jnp.sum(dy * xhat, axis=0, keepdims=True)
        sq_part = jnp.sum(err * err, axis=0, keepdims=True)

        @pl.when(i == 0)
        def _():
            dg_ref[...] = dg_part
            sq_ref[...] = sq_part

        @pl.when(i > 0)
        def _():
            dg_ref[...] += dg_part
            sq_ref[...] += sq_part

    row = pl.BlockSpec((T, D), lambda i: (i, 0))
    vec = pl.BlockSpec((1, D), lambda i: (0, 0))
    return pl.pallas_call(
        body, grid=(S // T,), in_specs=[row, vec, row], out_specs=[row, row, vec, vec],
        out_shape=[jax.ShapeDtypeStruct((S, D), F32), jax.ShapeDtypeStruct((S, D), BF16),
                   jax.ShapeDtypeStruct((1, D), F32), jax.ShapeDtypeStruct((1, D), F32)],
        compiler_params=_params("arbitrary"), name=name)(x, g, target)


def _prev_halo_spec(T, C, col_map):
    per = T // HALO
    return pl.BlockSpec((HALO, C), lambda *g: (jnp.maximum(g[0] * per - 1, 0), col_map(*g)))


def _pool_window_sums(ext, T, i):
    t = i * T + lax.broadcasted_iota(jnp.int32, (T, 1), 0)
    outs = []
    cnts = []
    for gi, w in enumerate(POOL_WINDOWS):
        s = ext[:, gi * POOL_GROUP_WIDTH:(gi + 1) * POOL_GROUP_WIDTH]
        sh = 1
        while sh < w:
            s = s + pltpu.roll(s, sh, 0)
            sh *= 2
        cnt = jnp.minimum(t + 1, w).astype(F32)
        outs.append(s[HALO:] / cnt - ext[HALO:, gi * POOL_GROUP_WIDTH:(gi + 1) * POOL_GROUP_WIDTH])
        cnts.append(cnt)
    return outs, cnts


def _load_ext(ext_ref, halo_ref, blk_ref, first):
    ext_ref[0:HALO, :] = jnp.where(first, 0.0, halo_ref[...].astype(F32))
    ext_ref[HALO:, :] = blk_ref[...].astype(F32)


def _pool_fwd(u, w_lin, scale):
    S = u.shape[0]
    T = 512
    W = POOL_WIDTH

    def body(u_ref, halo_ref, wl_ref, sc_ref, y_ref, ext_ref):
        i = pl.program_id(0)
        _load_ext(ext_ref, halo_ref, u_ref, i == 0)
        pooled, _ = _pool_window_sums(ext_ref[...], T, i)
        for gi in range(4):
            z = _dot(pooled[gi], wl_ref[gi], "nn")
            cs = slice(gi * POOL_GROUP_WIDTH, (gi + 1) * POOL_GROUP_WIDTH)
            y_ref[:, cs] = (z * sc_ref[:, cs]).astype(BF16)

    return pl.pallas_call(
        body, grid=(S // T,),
        in_specs=[pl.BlockSpec((T, W), lambda i: (i, 0)), _prev_halo_spec(T, W, lambda i: 0),
                  pl.BlockSpec((4, 256, 256), lambda i: (0, 0, 0)), pl.BlockSpec((1, W), lambda i: (0, 0))],
        out_specs=pl.BlockSpec((T, W), lambda i: (i, 0)),
        out_shape=jax.ShapeDtypeStruct((S, W), BF16),
        scratch_shapes=[pltpu.VMEM((HALO + T, W), F32)],
        compiler_params=_params("parallel"), name="pool_fwd")(u, u, w_lin, scale)


def _pool_bwd_a(u, dy, w_lin, scale):
    S = u.shape[0]
    T = 512
    W = POOL_WIDTH

    def body(u_ref, halo_ref, dy_ref, wl_ref, sc_ref, e_ref, dwl_ref, dsc_ref, ext_ref):
        i = pl.program_id(0)
        _load_ext(ext_ref, halo_ref, u_ref, i == 0)
        pooled, cnts = _pool_window_sums(ext_ref[...], T, i)
        for gi in range(4):
            cs = slice(gi * POOL_GROUP_WIDTH, (gi + 1) * POOL_GROUP_WIDTH)
            z = _dot(pooled[gi], wl_ref[gi], "nn")
            dyg = dy_ref[:, cs]
            dsc = jnp.sum(dyg * z, axis=0, keepdims=True)
            dz = dyg * sc_ref[:, cs]
            dwl = _dot(pooled[gi], dz, "tn")
            dpooled = _dot(dz, wl_ref[gi], "nt")
            e_ref[:, cs] = dpooled / cnts[gi]

            @pl.when(i == 0)
            def _():
                dwl_ref[gi] = dwl
                dsc_ref[:, cs] = dsc

            @pl.when(i > 0)
            def _():
                dwl_ref[gi] += dwl
                dsc_ref[:, cs] += dsc

    return pl.pallas_call(
        body, grid=(S // T,),
        in_specs=[pl.BlockSpec((T, W), lambda i: (i, 0)), _prev_halo_spec(T, W, lambda i: 0),
                  pl.BlockSpec((T, W), lambda i: (i, 0)),
                  pl.BlockSpec((4, 256, 256), lambda i: (0, 0, 0)), pl.BlockSpec((1, W), lambda i: (0, 0))],
        out_specs=[pl.BlockSpec((T, W), lambda i: (i, 0)), pl.BlockSpec((4, 256, 256), lambda i: (0, 0, 0)),
                   pl.BlockSpec((1, W), lambda i: (0, 0))],
        out_shape=[jax.ShapeDtypeStruct((S, W), F32), jax.ShapeDtypeStruct((4, 256, 256), F32),
                   jax.ShapeDtypeStruct((1, W), F32)],
        scratch_shapes=[pltpu.VMEM((HALO + T, W), F32)],
        compiler_params=_params("arbitrary"), name="pool_bwd_a")(u, u, dy, w_lin, scale)


def _pool_bwd_b(e):
    S = e.shape[0]
    T = 512
    W = POOL_WIDTH
    per = T // HALO
    n_blk = S // T

    def body(e_ref, halo_ref, du_ref, ext_ref):
        i = pl.program_id(0)
        ext_ref[0:T, :] = e_ref[...]
        ext_ref[T:, :] = jnp.where(i == n_blk - 1, 0.0, halo_ref[...])
        ext = ext_ref[...]
        n = T + HALO
        t = i * T + lax.broadcasted_iota(jnp.int32, (T, 1), 0)
        for gi, w in enumerate(POOL_WINDOWS):
            cs = slice(gi * POOL_GROUP_WIDTH, (gi + 1) * POOL_GROUP_WIDTH)
            s = ext[:, cs]
            sh = 1
            while sh < w:
                s = s + pltpu.roll(s, n - sh, 0)
                sh *= 2
            cnt = jnp.minimum(t + 1, w).astype(F32)
            du_ref[:, cs] = (s[0:T] - ext[0:T, cs] * cnt).astype(BF16)

    return pl.pallas_call(
        body, grid=(n_blk,),
        in_specs=[pl.BlockSpec((T, W), lambda i: (i, 0)),
                  pl.BlockSpec((HALO, W), lambda i: (jnp.minimum((i + 1) * per, S // HALO - 1), 0))],
        out_specs=pl.BlockSpec((T, W), lambda i: (i, 0)),
        out_shape=jax.ShapeDtypeStruct((S, W), BF16),
        scratch_shapes=[pltpu.VMEM((T + HALO, W), F32)],
        compiler_params=_params("parallel"), name="pool_bwd_b")(e, e)


def _slope(head):
    return 2.0 ** (-8.0 * (head + 1) / N_ATTN_HEADS)


def _attn_masks(b):
    row = lax.broadcasted_iota(jnp.int32, (SPAN, SPAN), 0)
    col = lax.broadcasted_iota(jnp.int32, (SPAN, SPAN), 1)
    dist_cur = (row - col).astype(F32)
    valid_cur = col <= row
    valid_prev = jnp.logical_and(col >= row, b > 0)
    return dist_cur, valid_cur, valid_prev


ATTN_BATCH = 4


def _attn_tiling(gi, S):
    d = ATTN_GROUPS[gi][1]
    hp = HEADS_PER_GROUP if d < 16 else 1
    return d, SPAN * d, S // (SPAN * d), hp, hp * HEAD_DIM, ATTN_BATCH // hp


_BATCHED_DN = {"nt": (((2,), (2,)), ((0,), (0,))), "nn": (((2,), (1,)), ((0,), (0,))), "tn": (((1,), (1,)), ((0,), (0,)))}


def _bdot(a, b, mode):
    return lax.dot_general(a.astype(BF16), b.astype(BF16), _BATCHED_DN[mode], preferred_element_type=F32)


def _batch_rows(it, d, rb):
    return [pl.ds(it * rb + j, SPAN, stride=d) for j in range(rb)]


def _take(scr, hp, rows):
    return jnp.stack([scr[hi, rw, :] for hi in range(hp) for rw in rows])


def _put(scr, hp, rows, val):
    for n, (hi, rw) in enumerate((hi, rw) for hi in range(hp) for rw in rows):
        scr[hi, rw, :] = val[n]


def _batch_bias(gi, d, hp, rb, dist):
    return jnp.stack([(-(_head_slope(gi, hp, hi) * d)) * dist for hi in range(hp) for _ in range(rb)])


def _head_slope(gi, hp, hi):
    if hp == HEADS_PER_GROUP:
        return _slope(gi * HEADS_PER_GROUP + hi)
    first = pl.program_id(1) * hp
    sl = jnp.float32(_slope(gi * HEADS_PER_GROUP + hi))
    for h0 in range(hp, HEADS_PER_GROUP, hp):
        sl = jnp.where(first == h0, jnp.float32(_slope(gi * HEADS_PER_GROUP + h0 + hi)), sl)
    return sl


def _split_heads(src_ref, dst_ref, hp):
    for hi in range(hp):
        dst_ref[hi] = src_ref[:, hi * HEAD_DIM:(hi + 1) * HEAD_DIM].astype(F32)


def _attn_fwd(gi, qkv):
    S = qkv.shape[0]
    d, C, nb, hp, W, rb = _attn_tiling(gi, S)
    scale = HEAD_DIM ** -0.5

    def body(q_ref, kc_ref, kp_ref, vc_ref, vp_ref, o_ref, lse_ref, qs, kcs, kps, vcs, vps, os, ls):
        b = pl.program_id(0)
        for src, dst in ((q_ref, qs), (kc_ref, kcs), (kp_ref, kps), (vc_ref, vcs), (vp_ref, vps)):
            _split_heads(src, dst, hp)
        dist_cur, valid_cur, valid_prev = _attn_masks(b)
        bias_cur = _batch_bias(gi, d, hp, rb, dist_cur)
        bias_far = _batch_bias(gi, d, hp, rb, dist_cur + SPAN)

        def sub_sequences(it, carry):
            rows = _batch_rows(it, d, rb)
            q = _take(qs, hp, rows)
            s_cur = jnp.where(valid_cur, _bdot(q, _take(kcs, hp, rows), "nt") * scale + bias_cur, NEG_BIG)
            s_prev = jnp.where(valid_prev, _bdot(q, _take(kps, hp, rows), "nt") * scale + bias_far, NEG_BIG)
            m = jnp.maximum(jnp.max(s_cur, axis=-1, keepdims=True), jnp.max(s_prev, axis=-1, keepdims=True))
            p_cur = jnp.exp(s_cur - m)
            p_prev = jnp.exp(s_prev - m)
            l = jnp.sum(p_cur, axis=-1, keepdims=True) + jnp.sum(p_prev, axis=-1, keepdims=True)
            o = (_bdot(p_cur, _take(vcs, hp, rows), "nn") + _bdot(p_prev, _take(vps, hp, rows), "nn")) / l
            _put(os, hp, rows, o)
            _put(ls, hp, rows, jnp.broadcast_to(m + jnp.log(l), o.shape))
            return carry

        lax.fori_loop(0, d // rb, sub_sequences, 0)
        for hi in range(hp):
            hs = slice(hi * HEAD_DIM, (hi + 1) * HEAD_DIM)
            o_ref[:, hs] = os[hi]
            lse_ref[:, hs] = ls[hi]

    per_sec = ATTN_WIDTH // W
    per_grp = ATTN_OUT_WIDTH // W

    def spec(sec, shift):
        return pl.BlockSpec((C, W), lambda b, h: (jnp.maximum(b + shift, 0), sec * per_sec + gi * per_grp + h))

    out = pl.BlockSpec((C, W), lambda b, h: (b, h))
    scr = pltpu.VMEM((hp, C, HEAD_DIM), F32)
    return pl.pallas_call(
        body, grid=(nb, HEADS_PER_GROUP // hp),
        in_specs=[spec(0, 0), spec(1, 0), spec(1, -1), spec(2, 0), spec(2, -1)], out_specs=[out, out],
        out_shape=[jax.ShapeDtypeStruct((S, ATTN_OUT_WIDTH), F32)] * 2, scratch_shapes=[scr] * 7,
        compiler_params=_params("parallel", "parallel"), name=f"attn_fwd_g{gi}")(qkv, qkv, qkv, qkv, qkv)


def _attn_bwd(gi, qkv, do, lse, dp):
    S = qkv.shape[0]
    d, C, nb, hp, W, rb = _attn_tiling(gi, S)
    scale = HEAD_DIM ** -0.5
    n_in = 12

    def body(*refs):
        ins, (dq_ref, dk_ref, dv_ref) = refs[:n_in], refs[n_in:n_in + 3]
        scr = refs[n_in + 3:]
        for src, dst in zip(ins, scr[:n_in]):
            _split_heads(src, dst, hp)
        qs, kcs, vcs, kps, vps, qns, dos, dons, lses, lsens, dps, dpns = scr[:n_in]
        dqs, dks, dvs = scr[n_in:]
        b = pl.program_id(0)
        dist_cur, valid_cur, valid_prev = _attn_masks(b)
        row = lax.broadcasted_iota(jnp.int32, (SPAN, SPAN), 0)
        col = lax.broadcasted_iota(jnp.int32, (SPAN, SPAN), 1)
        valid_next = jnp.logical_and(col >= row, b < nb - 1)

        bias_cur = _batch_bias(gi, d, hp, rb, dist_cur)
        bias_far = _batch_bias(gi, d, hp, rb, dist_cur + SPAN)

        def sub_sequences(it, carry):
            rows = _batch_rows(it, d, rb)
            q, kc, vc, kp, vp, qn, dov, don = (_take(s, hp, rows) for s in (qs, kcs, vcs, kps, vps, qns, dos, dons))
            lse_b, lse_n, dp_b, dp_n = (_take(s, hp, rows) for s in (lses, lsens, dps, dpns))
            p_cur = jnp.where(valid_cur, jnp.exp(_bdot(q, kc, "nt") * scale + bias_cur - lse_b), 0.0)
            p_prev = jnp.where(valid_prev, jnp.exp(_bdot(q, kp, "nt") * scale + bias_far - lse_b), 0.0)
            ds_cur = p_cur * (_bdot(dov, vc, "nt") - dp_b)
            ds_prev = p_prev * (_bdot(dov, vp, "nt") - dp_b)
            _put(dqs, hp, rows, (_bdot(ds_cur, kc, "nn") + _bdot(ds_prev, kp, "nn")) * scale)
            p_next = jnp.where(valid_next, jnp.exp(_bdot(qn, kc, "nt") * scale + bias_far - lse_n), 0.0)
            ds_next = p_next * (_bdot(don, vc, "nt") - dp_n)
            _put(dvs, hp, rows, _bdot(p_cur, dov, "tn") + _bdot(p_next, don, "tn"))
            _put(dks, hp, rows, (_bdot(ds_cur, q, "tn") + _bdot(ds_next, qn, "tn")) * scale)
            return carry

        lax.fori_loop(0, d // rb, sub_sequences, 0)
        for hi in range(hp):
            hs = slice(hi * HEAD_DIM, (hi + 1) * HEAD_DIM)
            dq_ref[:, hs] = dqs[hi].astype(BF16)
            dk_ref[:, hs] = dks[hi].astype(BF16)
            dv_ref[:, hs] = dvs[hi].astype(BF16)

    per_sec = ATTN_WIDTH // W
    per_grp = ATTN_OUT_WIDTH // W

    def qkv_spec(sec, shift):
        return pl.BlockSpec((C, W), lambda b, h: (jnp.clip(b + shift, 0, nb - 1), sec * per_sec + gi * per_grp + h))

    def act_spec(shift):
        return pl.BlockSpec((C, W), lambda b, h: (jnp.clip(b + shift, 0, nb - 1), h))

    out = act_spec(0)
    scr = pltpu.VMEM((hp, C, HEAD_DIM), F32)
    return pl.pallas_call(
        body, grid=(nb, HEADS_PER_GROUP // hp),
        in_specs=[qkv_spec(0, 0), qkv_spec(1, 0), qkv_spec(2, 0), qkv_spec(1, -1), qkv_spec(2, -1), qkv_spec(0, 1),
                  act_spec(0), act_spec(1), act_spec(0), act_spec(1), act_spec(0), act_spec(1)],
        out_specs=[out, out, out], out_shape=[jax.ShapeDtypeStruct((S, ATTN_OUT_WIDTH), BF16)] * 3,
        scratch_shapes=[scr] * (n_in + 3),
        compiler_params=_params("parallel", "parallel"), name=f"attn_bwd_g{gi}")(
            qkv, qkv, qkv, qkv, qkv, qkv, do, do, lse, lse, dp, dp)


def _group_weights(l_refs):
    l0, l1, l2 = (r[...] for r in l_refs)
    m = jnp.maximum(jnp.maximum(l0, l1), l2)
    e = [jnp.exp(l0 - m), jnp.exp(l1 - m), jnp.exp(l2 - m)]
    tot = e[0] + e[1] + e[2]
    return [ei / tot for ei in e]


def _merge_fwd(outs, lses):
    S, W = outs[0].shape
    T = 512

    def body(o0, o1, o2, l0, l1, l2, y_ref):
        w = _group_weights((l0, l1, l2))
        y_ref[...] = (w[0] * o0[...] + w[1] * o1[...] + w[2] * o2[...]).astype(BF16)

    blk = pl.BlockSpec((T, W), lambda i: (i, 0))
    return pl.pallas_call(
        body, grid=(S // T,), in_specs=[blk] * 6, out_specs=blk, out_shape=jax.ShapeDtypeStruct((S, W), BF16),
        compiler_params=_params("parallel"), name="merge_fwd")(*outs, *lses)


def _merge_bwd(dy, outs, lses):
    S, W = outs[0].shape
    T = 512

    def body(dy_ref, o0, o1, o2, l0, l1, l2, do0, do1, do2, dp0, dp1, dp2):
        w = _group_weights((l0, l1, l2))
        dyv = dy_ref[...]
        y = w[0] * o0[...] + w[1] * o1[...] + w[2] * o2[...]
        prod = dyv * y
        dots = []
        for hi in range(HEADS_PER_GROUP):
            hs = slice(hi * HEAD_DIM, (hi + 1) * HEAD_DIM)
            dots.append(jnp.broadcast_to(jnp.sum(prod[:, hs], axis=-1, keepdims=True), (T, HEAD_DIM)))
        dot_b = jnp.concatenate(dots, axis=1)
        for wg, do_ref, dp_ref in zip(w, (do0, do1, do2), (dp0, dp1, dp2)):
            do_ref[...] = (wg * dyv).astype(BF16)
            dp_ref[...] = wg * dot_b

    blk = pl.BlockSpec((T, W), lambda i: (i, 0))
    res = pl.pallas_call(
        body, grid=(S // T,), in_specs=[blk] * 7, out_specs=[blk] * 6,
        out_shape=[jax.ShapeDtypeStruct((S, W), BF16)] * 3 + [jax.ShapeDtypeStruct((S, W), F32)] * 3,
        compiler_params=_params("parallel"), name="merge_bwd")(dy, *outs, *lses)
    return res[:3], res[3:]


def _mix_fwd(pool_y, attn_y, w_pool_out_t, w_attn_out_t, gates_pre, b_gate):
    S = pool_y.shape[0]
    D = D_MODEL
    tm, tn = 1024, 512
    nj = D // tn

    def body(p_ref, a_ref, wp_ref, wa_ref, g0_ref, g1_ref, b0_ref, b1_ref, mixed_ref, yp_ref, ya_ref):
        yp = _dot(p_ref[...], wp_ref[...], "nt")
        ya = _dot(a_ref[...], wa_ref[...], "nt")
        g0 = jax.nn.sigmoid(g0_ref[...] + b0_ref[...])
        g1 = jax.nn.sigmoid(g1_ref[...] + b1_ref[...])
        mixed_ref[...] = (g0 * yp + g1 * ya).astype(BF16)
        yp_ref[...] = yp.astype(BF16)
        ya_ref[...] = ya.astype(BF16)

    out = pl.BlockSpec((tm, tn), lambda i, j: (i, j))
    return pl.pallas_call(
        body, grid=(S // tm, nj),
        in_specs=[pl.BlockSpec((tm, POOL_WIDTH), lambda i, j: (i, 0)),
                  pl.BlockSpec((tm, ATTN_OUT_WIDTH), lambda i, j: (i, 0)),
                  pl.BlockSpec((tn, POOL_WIDTH), lambda i, j: (j, 0)),
                  pl.BlockSpec((tn, ATTN_OUT_WIDTH), lambda i, j: (j, 0)),
                  pl.BlockSpec((tm, tn), lambda i, j: (i, j)), pl.BlockSpec((tm, tn), lambda i, j: (i, j + nj)),
                  pl.BlockSpec((1, tn), lambda i, j: (0, j)), pl.BlockSpec((1, tn), lambda i, j: (0, j + nj))],
        out_specs=[out, out, out], out_shape=[jax.ShapeDtypeStruct((S, D), BF16)] * 3,
        compiler_params=_params("parallel", "parallel"), name="mix_fwd")(
            pool_y, attn_y, w_pool_out_t, w_attn_out_t, gates_pre, gates_pre, b_gate, b_gate)


def _mix_bwd(dmixed, gates_pre, b_gate, y_pool, y_attn):
    S = dmixed.shape[0]
    D = D_MODEL
    T = 256

    def body(dm_ref, gp_ref, b_ref, yp_ref, ya_ref, dyp_ref, dya_ref, dg_ref, db_ref):
        i = pl.program_id(0)
        dm = dm_ref[...].astype(F32)
        parts = []
        for br, (y_ref, dy_ref) in enumerate(((yp_ref, dyp_ref), (ya_ref, dya_ref))):
            cs = slice(br * D, (br + 1) * D)
            g = jax.nn.sigmoid(gp_ref[:, cs] + b_ref[:, cs])
            dy_ref[...] = (dm * g).astype(BF16)
            dpre = dm * y_ref[...].astype(F32) * g * (1.0 - g)
            dg_ref[:, cs] = dpre.astype(BF16)
            parts.append(jnp.sum(dpre, axis=0, keepdims=True))

        @pl.when(i == 0)
        def _():
            db_ref[:, 0:D] = parts[0]
            db_ref[:, D:2 * D] = parts[1]

        @pl.when(i > 0)
        def _():
            db_ref[:, 0:D] += parts[0]
            db_ref[:, D:2 * D] += parts[1]

    row = pl.BlockSpec((T, D), lambda i: (i, 0))
    row2 = pl.BlockSpec((T, 2 * D), lambda i: (i, 0))
    vec2 = pl.BlockSpec((1, 2 * D), lambda i: (0, 0))
    return pl.pallas_call(
        body, grid=(S // T,), in_specs=[row, row2, vec2, row, row], out_specs=[row, row, row2, vec2],
        out_shape=[jax.ShapeDtypeStruct((S, D), BF16), jax.ShapeDtypeStruct((S, D), BF16),
                   jax.ShapeDtypeStruct((S, 2 * D), BF16), jax.ShapeDtypeStruct((1, 2 * D), F32)],
        compiler_params=_params("arbitrary"), name="mix_bwd")(dmixed, gates_pre, b_gate, y_pool, y_attn)


_CONV_T, _CONV_C = 512, 512


def _conv_taps(ext_ref, h, w_ref, b_ref):
    ext = ext_ref[h]
    x2 = pltpu.roll(ext, 2, 0)[HALO:]
    x1 = pltpu.roll(ext, 1, 0)[HALO:]
    x0 = ext[HALO:]
    y = b_ref[...] + w_ref[0:1, :] * x2
    y = y + w_ref[1:2, :] * x1
    y = y + w_ref[2:3, :] * x0
    return y, (x2, x1, x0)


def _gelu_parts(a):
    cdf = 0.5 * (1.0 + lax.erf(a * (1.0 / math.sqrt(2.0))))
    return cdf, a * cdf


def _conv_specs(S):
    T, C = _CONV_T, _CONV_C
    nj = D_FF // C
    blk = pl.BlockSpec((2, T, C), lambda i, j: (0, i, j))
    per = T // HALO
    halo = pl.BlockSpec((2, HALO, C), lambda i, j: (0, jnp.maximum(i * per - 1, 0), j))
    w_a = pl.BlockSpec((3, C), lambda i, j: (0, j))
    w_b = pl.BlockSpec((3, C), lambda i, j: (0, j + nj))
    b_a = pl.BlockSpec((1, C), lambda i, j: (0, j))
    b_b = pl.BlockSpec((1, C), lambda i, j: (0, j + nj))
    return T, C, nj, blk, halo, w_a, w_b, b_a, b_b


def _conv_load(ext_ref, halo_ref, up_ref, first):
    for h in range(2):
        ext_ref[h, 0:HALO, :] = jnp.where(first, 0.0, halo_ref[h].astype(F32))
        ext_ref[h, HALO:, :] = up_ref[h].astype(F32)


def _convffn_fwd(up_pre, conv_w, conv_b):
    S = up_pre.shape[1]
    T, C, nj, blk, halo, w_a, w_b, b_a, b_b = _conv_specs(S)

    def body(up_ref, halo_ref, wa_ref, wb_ref, ba_ref, bb_ref, act_ref, ext_ref):
        _conv_load(ext_ref, halo_ref, up_ref, pl.program_id(0) == 0)
        a, _ = _conv_taps(ext_ref, 0, wa_ref, ba_ref)
        b, _ = _conv_taps(ext_ref, 1, wb_ref, bb_ref)
        _, gelu = _gelu_parts(a)
        act_ref[...] = (gelu * b).astype(BF16)

    return pl.pallas_call(
        body, grid=(S // T, nj), in_specs=[blk, halo, w_a, w_b, b_a, b_b],
        out_specs=pl.BlockSpec((T, C), lambda i, j: (i, j)),
        out_shape=jax.ShapeDtypeStruct((S, D_FF), BF16),
        scratch_shapes=[pltpu.VMEM((2, HALO + T, C), F32)],
        compiler_params=_params("parallel", "parallel"), name="convffn_fwd")(
            up_pre, up_pre, conv_w, conv_w, conv_b, conv_b)


def _convffn_bwd_a(up_pre, dact, conv_w, conv_b):
    S = up_pre.shape[1]
    T, C, nj, blk, halo, w_a, w_b, b_a, b_b = _conv_specs(S)
    def swap(spec):
        return pl.BlockSpec(spec.block_shape, lambda j, i, _m=spec.index_map: _m(i, j))

    def body(up_ref, halo_ref, dact_ref, wa_ref, wb_ref, ba_ref, bb_ref, dup_ref, dwa_ref, dwb_ref, dba_ref, dbb_ref,
             ext_ref):
        i = pl.program_id(1)
        _conv_load(ext_ref, halo_ref, up_ref, i == 0)
        a, xa = _conv_taps(ext_ref, 0, wa_ref, ba_ref)
        b, xb = _conv_taps(ext_ref, 1, wb_ref, bb_ref)
        cdf, gelu = _gelu_parts(a)
        dgelu = cdf + a * (jnp.exp(-0.5 * a * a) * (1.0 / math.sqrt(2.0 * math.pi)))
        dact = dact_ref[...].astype(F32)
        da = dact * b * dgelu
        db = dact * gelu
        dup_ref[0] = da.astype(BF16)
        dup_ref[1] = db.astype(BF16)
        for dval, xs, dw_ref, dbias_ref in ((da, xa, dwa_ref, dba_ref), (db, xb, dwb_ref, dbb_ref)):
            dw = jnp.concatenate([jnp.sum(dval * xk, axis=0, keepdims=True) for xk in xs], axis=0)
            dbias = jnp.sum(dval, axis=0, keepdims=True)

            @pl.when(i == 0)
            def _():
                dw_ref[...] = dw
                dbias_ref[...] = dbias

            @pl.when(i > 0)
            def _():
                dw_ref[...] += dw
                dbias_ref[...] += dbias

    dact_spec = pl.BlockSpec((T, C), lambda j, i: (i, j))
    dw_spec = pl.BlockSpec((3, C), lambda j, i: (0, j))
    db_spec = pl.BlockSpec((1, C), lambda j, i: (0, j))
    res = pl.pallas_call(
        body, grid=(nj, S // T),
        in_specs=[swap(blk), swap(halo), dact_spec, swap(w_a), swap(w_b), swap(b_a), swap(b_b)],
        out_specs=[swap(blk), dw_spec, dw_spec, db_spec, db_spec],
        out_shape=[jax.ShapeDtypeStruct((2, S, D_FF), BF16), jax.ShapeDtypeStruct((3, D_FF), F32),
                   jax.ShapeDtypeStruct((3, D_FF), F32), jax.ShapeDtypeStruct((1, D_FF), F32),
                   jax.ShapeDtypeStruct((1, D_FF), F32)],
        scratch_shapes=[pltpu.VMEM((2, HALO + T, C), F32)],
        compiler_params=_params("parallel", "arbitrary"), name="convffn_bwd_a")(
            up_pre, up_pre, dact, conv_w, conv_w, conv_b, conv_b)
    dup, dwa, dwb, dba, dbb = res
    dconv_w = jnp.concatenate([dwa, dwb], axis=1)
    dconv_b = jnp.concatenate([dba, dbb], axis=1)
    return dup, dconv_w, dconv_b


def _convffn_bwd_b(dup, conv_w):
    S = dup.shape[1]
    T, C = _CONV_T, _CONV_C
    nj = D_FF // C
    per = T // HALO
    n_blk = S // T

    def body(d_ref, halo_ref, wa_ref, wb_ref, o_ref, ext_ref):
        i = pl.program_id(0)
        n = T + HALO
        for h, w_ref in enumerate((wa_ref, wb_ref)):
            ext_ref[h, 0:T, :] = d_ref[h].astype(F32)
            ext_ref[h, T:, :] = jnp.where(i == n_blk - 1, 0.0, halo_ref[h].astype(F32))
            ext = ext_ref[h]
            y = w_ref[2:3, :] * ext[0:T] + w_ref[1:2, :] * pltpu.roll(ext, n - 1, 0)[0:T]
            y = y + w_ref[0:1, :] * pltpu.roll(ext, n - 2, 0)[0:T]
            o_ref[h] = y.astype(BF16)

    blk = pl.BlockSpec((2, T, C), lambda i, j: (0, i, j))
    halo = pl.BlockSpec((2, HALO, C), lambda i, j: (0, jnp.minimum((i + 1) * per, S // HALO - 1), j))
    return pl.pallas_call(
        body, grid=(n_blk, nj),
        in_specs=[blk, halo, pl.BlockSpec((3, C), lambda i, j: (0, j)), pl.BlockSpec((3, C), lambda i, j: (0, j + nj))],
        out_specs=blk, out_shape=jax.ShapeDtypeStruct((2, S, D_FF), BF16),
        scratch_shapes=[pltpu.VMEM((2, T + HALO, C), F32)],
        compiler_params=_params("parallel", "parallel"), name="convffn_bwd_b")(dup, dup, conv_w, conv_w)


def _position():
    return lax.axis_index("x"), lax.axis_index("y"), lax.axis_index("c")


def _flip(pos, k):
    x, y, c = pos
    return (1 - x if k & 4 else x, 1 - y if k & 2 else y, 1 - c if k & 1 else c)


def _index(pos):
    return 4 * pos[0] + 2 * pos[1] + pos[2]


def _all_gather_weights(shards):
    n = len(shards)

    def body(*refs):
        ins, outs = refs[:n], refs[n:2 * n]
        send_sems, recv_sems, local_sems = refs[2 * n:]
        me = _position()
        x, y, c = me
        sibling = (x, y, 1 - c)
        chips = [(1 - x, y), (x, 1 - y), (1 - x, 1 - y)]

        def copy(a, k, block, to, own=False):
            slot = outs[a].at[_index(block)]
            return pltpu.make_async_remote_copy(
                src_ref=ins[a] if own else slot, dst_ref=slot, send_sem=send_sems.at[7 * a + k],
                recv_sem=recv_sems.at[7 * a + k], device_id=to, device_id_type=MESH)

        mines = [pltpu.make_async_copy(ins[a], outs[a].at[_index(me)], local_sems.at[a]) for a in range(n)]
        started = []
        for a in range(n):
            mines[a].start()
            first = [copy(a, 0, me, sibling, own=True)]
            first += [copy(a, 1 + j, me, (*chip, c), own=True) for j, chip in enumerate(chips)]
            for cp in first:
                cp.start()
            started += first
        for j, chip in enumerate(chips):
            for a in range(n):
                copy(a, 1 + j, (*chip, c), me).wait_recv()
                passed = copy(a, 4 + j, (*chip, c), sibling)
                passed.start()
                started.append(passed)
        for a in range(n):
            copy(a, 0, sibling, me).wait_recv()
            for j, chip in enumerate(chips):
                copy(a, 4 + j, (*chip, 1 - c), me).wait_recv()
        for cp in started:
            cp.wait_send()
        for cp in mines:
            cp.wait()

    hbm = pl.BlockSpec(memory_space=pl.ANY)
    return pl.pallas_call(
        body, out_shape=[jax.ShapeDtypeStruct((N_DEV,) + s.shape, s.dtype) for s in shards],
        in_specs=[hbm] * n, out_specs=[hbm] * n,
        scratch_shapes=[pltpu.SemaphoreType.DMA((7 * n,)), pltpu.SemaphoreType.DMA((7 * n,)),
                        pltpu.SemaphoreType.DMA((n,))],
        name="all_gather_weights")(*shards)


_HBM = pl.BlockSpec(memory_space=pltpu.HBM)
_SEM = pl.BlockSpec(memory_space=pltpu.SEMAPHORE)
_ANY = pl.BlockSpec(memory_space=pl.ANY)
_EFFECT = pltpu.SideEffectType.DATAFLOW_SIDE_EFFECTING


def _in_hbm(a):
    return pltpu.with_memory_space_constraint(a, pltpu.HBM)


def _split_copies(name, srcs, lands, n_copies, plan, start_after, wait_after):
    ns, nl = len(srcs), len(lands)
    start_after = tuple(start_after)
    na = len(start_after)

    def start_body(*refs):
        src_refs, land_refs = refs[:ns], refs[ns:ns + nl]
        send_sems, recv_sems = refs[ns + nl + na], refs[ns + nl + na + 1]
        token = refs[-1]
        for k, (src, dst, _, peer) in enumerate(plan(src_refs, land_refs)):
            pltpu.make_async_remote_copy(src_ref=src, dst_ref=dst, send_sem=send_sems.at[k], recv_sem=recv_sems.at[k],
                                         device_id=peer, device_id_type=MESH).start()
        token[...] = jnp.zeros_like(token)

    thru = [pltpu.HBM(a.shape, a.dtype) for a in list(srcs) + list(lands)]
    res = pl.pallas_call(
        start_body, name=name + "_start",
        out_shape=[pltpu.SemaphoreType.DMA((n_copies,)), pltpu.SemaphoreType.DMA((n_copies,))] + thru
        + [jax.ShapeDtypeStruct((8, 128), F32)],
        in_specs=[_HBM] * (ns + nl) + [_ANY] * na,
        out_specs=[_SEM, _SEM] + [_HBM] * (ns + nl) + [pl.BlockSpec(memory_space=pltpu.VMEM)],
        input_output_aliases={i: 2 + i for i in range(ns + nl)},
        compiler_params=pltpu.CompilerParams(has_side_effects=_EFFECT),
    )(*[_in_hbm(a) for a in list(srcs) + list(lands)], *start_after)
    send_sems, recv_sems, token = res[0], res[1], res[-1]
    thru_vals = res[2:-1]

    def wait():
        def wait_body(*refs):
            src_refs, land_refs = refs[:ns], refs[ns:ns + nl]
            s_sems, r_sems = refs[ns + nl], refs[ns + nl + 1]
            for k, (src, _, dst, peer) in enumerate(plan(src_refs, land_refs)):
                cp = pltpu.make_async_remote_copy(src_ref=src, dst_ref=dst, send_sem=s_sems.at[k], recv_sem=r_sems.at[k],
                                                  device_id=peer, device_id_type=MESH)
                cp.wait_send()
                cp.wait_recv()

        out = pl.pallas_call(
            wait_body, name=name + "_wait", out_shape=thru,
            in_specs=[_HBM] * (ns + nl) + [_SEM, _SEM, _ANY], out_specs=[_HBM] * (ns + nl),
            input_output_aliases={i: i for i in range(ns + nl)},
            compiler_params=pltpu.CompilerParams(has_side_effects=_EFFECT),
        )(*thru_vals, send_sems, recv_sems, wait_after())
        return list(out[:ns]), list(out[ns:])

    return wait, token


def _gather_direct_plan(n):
    def plan(src_refs, land_refs):
        me = _position()
        peers = [_flip(me, k) for k in range(1, N_DEV)]
        return [(src_refs[a], land_refs[a].at[_index(me)], land_refs[a].at[_index(p)], p)
                for a in range(n) for p in peers]
    return plan


def _place_own(shards):
    me = _index(_position())
    return [lax.dynamic_update_slice(lax.empty((N_DEV,) + s.shape, s.dtype), s[None], (me, 0, 0)) for s in shards]


def _chip_index(pos):
    return 2 * pos[0] + pos[1]


def _exchange_d2d(name, parts):
    n = len(parts)

    def body(*refs):
        ins, outs = refs[:n], refs[n:2 * n]
        send_sems, recv_sems = refs[2 * n:]
        x, y, c = _position()
        sibling = (x, y, 1 - c)

        def copy(a, q, core):
            return pltpu.make_async_remote_copy(
                src_ref=ins[a].at[2 * q + core], dst_ref=outs[a].at[q], send_sem=send_sems.at[4 * a + q],
                recv_sem=recv_sems.at[4 * a + q], device_id=sibling, device_id_type=MESH)

        sent = [copy(a, q, 1 - c) for a in range(n) for q in range(4)]
        for cp in sent:
            cp.start()
        for cp in sent:
            cp.wait_recv()
        for cp in sent:
            cp.wait_send()

    return pl.pallas_call(
        body, out_shape=[jax.ShapeDtypeStruct((4,) + p.shape[1:], p.dtype) for p in parts],
        in_specs=[_ANY] * n, out_specs=[_ANY] * n,
        scratch_shapes=[pltpu.SemaphoreType.DMA((4 * n,)), pltpu.SemaphoreType.DMA((4 * n,))],
        name=name)(*parts)


def _pair_sum(name, part, got):
    _, R, C = part.shape
    T = _row_tile(R, 16)

    def body(p_ref, g_ref, s_out, l_out):
        c = lax.axis_index("c")
        mine = jnp.where(c == 0, p_ref[0, 0].astype(F32), p_ref[0, 1].astype(F32))
        s = (mine + g_ref[0].astype(F32)).astype(BF16)
        s_out[0] = s
        l_out[0] = s

    out = pl.BlockSpec((1, T, C), lambda q, i: (q, i, 0))
    return pl.pallas_call(
        body, grid=(4, R // T),
        in_specs=[pl.BlockSpec((1, 2, T, C), lambda q, i: (q, 0, i, 0)), out], out_specs=[out, out],
        out_shape=[jax.ShapeDtypeStruct((4, R, C), BF16)] * 2,
        compiler_params=_params("parallel", "parallel"), name=name)(part.reshape(4, 2, R, C), got)


def _exchange_direct_plan(n):
    def plan(src_refs, land_refs):
        me = _position()
        peers = [_flip(me, k) for k in range(1, N_DEV)]
        return [(src_refs[a].at[_index(p)], land_refs[a].at[_index(me)], land_refs[a].at[_index(p)], p)
                for a in range(n) for p in peers]
    return plan


def _exchange_ici_plan(n):
    def plan(src_refs, land_refs):
        me = _position()
        x, y, c = me
        peers = [(1 - x, y, c), (x, 1 - y, c), (1 - x, 1 - y, c)]
        return [(src_refs[a].at[_chip_index(p)], land_refs[a].at[_chip_index(me)], land_refs[a].at[_chip_index(p)], p)
                for a in range(n) for p in peers]
    return plan


def _adamw(w, g, m, v):
    m = ADAM_B1 * m + (1.0 - ADAM_B1) * g
    v = ADAM_B2 * v + (1.0 - ADAM_B2) * (g * g)
    m_hat = m / (1.0 - ADAM_B1 ** ADAM_STEP)
    v_hat = v / (1.0 - ADAM_B2 ** ADAM_STEP)
    delta = -ADAM_LR * (m_hat / (jnp.sqrt(v_hat) + ADAM_EPS) + ADAM_WD * w)
    return delta, m, v


def _row_tile(rows, unit, cap=256):
    best = rows
    for t in range(unit, min(rows, cap) + 1, unit):
        if rows % t == 0:
            best = t
    return best


def _sum_partials(name, recv):
    n_src, R, C = recv.shape
    T = _row_tile(R, 16)

    def body(r_ref, g_out):
        g = r_ref[0].astype(F32)
        for s in range(1, n_src):
            g = g + r_ref[s].astype(F32)
        g_out[...] = g

    return pl.pallas_call(
        body, grid=(R // T,), in_specs=[pl.BlockSpec((n_src, T, C), lambda i: (0, i, 0))],
        out_specs=pl.BlockSpec((T, C), lambda i: (i, 0)), out_shape=jax.ShapeDtypeStruct((R, C), F32),
        compiler_params=_params("parallel"), name=name)(recv)


def _adamw_update(name, w, g, m, v):
    R, C = w.shape
    T = _row_tile(R, 8)

    def body(w_ref, g_ref, m_ref, v_ref, d_out, m_out, v_out):
        delta, mn, vn = _adamw(w_ref[...], g_ref[...], m_ref[...], v_ref[...])
        d_out[...] = delta
        m_out[...] = mn
        v_out[...] = vn

    blk = pl.BlockSpec((T, C), lambda i: (i, 0))
    return pl.pallas_call(
        body, grid=(R // T,), in_specs=[blk] * 4, out_specs=[blk] * 3,
        out_shape=[jax.ShapeDtypeStruct((R, C), F32)] * 3, compiler_params=_params("parallel"), name=name)(w, g, m, v)


def _gather_vmem(p_ref, gath, send_sems, recv_sems):
    me = _position()
    gath[_index(me)] = p_ref[...]
    copies = []
    for k in range(1, N_DEV):
        peer = _flip(me, k)
        copies.append(pltpu.make_async_remote_copy(
            src_ref=p_ref, dst_ref=gath.at[_index(me)], send_sem=send_sems.at[k - 1],
            recv_sem=recv_sems.at[k - 1], device_id=peer, device_id_type=MESH))
    for cp in copies:
        cp.start()
    for k in range(1, N_DEV):
        peer = _flip(me, k)
        pltpu.make_async_remote_copy(
            src_ref=p_ref, dst_ref=gath.at[_index(peer)], send_sem=send_sems.at[k - 1],
            recv_sem=recv_sems.at[k - 1], device_id=peer, device_id_type=MESH).wait_recv()
    for cp in copies:
        cp.wait_send()


def _all_gather_small(p):
    R, C = p.shape

    def body(p_ref, out_ref, send_sems, recv_sems):
        _gather_vmem(p_ref, out_ref, send_sems, recv_sems)

    vm = pl.BlockSpec(memory_space=pltpu.VMEM)
    return pl.pallas_call(
        body, in_specs=[vm], out_specs=vm, out_shape=jax.ShapeDtypeStruct((N_DEV, R, C), F32),
        scratch_shapes=[pltpu.SemaphoreType.DMA((7,)), pltpu.SemaphoreType.DMA((7,))],
        name="all_gather_small")(p)


def _small_allreduce_adamw(part, w, m, v):
    R, C = part.shape

    def body(p_ref, w_ref, m_ref, v_ref, g_out, d_out, m_out, v_out, gath, send_sems, recv_sems):
        _gather_vmem(p_ref, gath, send_sems, recv_sems)
        g = gath[0]
        for s in range(1, N_DEV):
            g = g + gath[s]
        delta, mn, vn = _adamw(w_ref[...], g, m_ref[...], v_ref[...])
        g_out[...] = g
        d_out[...] = delta
        m_out[...] = mn
        v_out[...] = vn

    vm = pl.BlockSpec(memory_space=pltpu.VMEM)
    return pl.pallas_call(
        body, in_specs=[vm] * 4, out_specs=[vm] * 4, out_shape=[jax.ShapeDtypeStruct((R, C), F32)] * 4,
        scratch_shapes=[pltpu.VMEM((N_DEV, R, C), F32), pltpu.SemaphoreType.DMA((7,)), pltpu.SemaphoreType.DMA((7,))],
        name="small_allreduce_adamw")(part, w, m, v)


_BIG = ("w_in", "w_up", "w_down", "w_out", "w_pool_out", "w_attn_out")
_TRANSPOSED = ("w_in", "w_up", "w_pool_out", "w_attn_out")
_SMALL = (("g_mix", 2048), ("b_gate", 4096), ("pool_scale", 1024), ("g_ffn", 2048), ("conv_b", 11264), ("g_final", 2048))
CONV_ROWS = 8


def _to_comm(name, shard):
    return shard.T if name in _TRANSPOSED else shard


def _pack_small(vals):
    flat = [vals[n].reshape(-1) for n, _ in _SMALL]
    used = sum(f.shape[0] for f in flat)
    flat.append(jnp.zeros((SMALL_ROWS * PACK_COLS - used,), F32))
    return jnp.concatenate(flat).reshape(SMALL_ROWS, PACK_COLS)


def _unpack_small(packed, like):
    flat = packed.reshape(-1)
    out, off = {}, 0
    for name, n in _SMALL:
        out[name] = flat[off:off + n].reshape(like[name].shape)
        off += n
    return out


def _local_step(x, target, W, small, first_token=0.0, late_weights=None, emit=None):
    S = x.shape[0]
    o_qkv, o_gate = POOL_WIDTH, POOL_WIDTH + QKV_WIDTH
    if emit is None:
        emit = lambda grads: None

    def zero(token):
        return 0.0 if token is None else token[0, 0]
    w_in_t = W["w_in"]

    h1 = _rms_fwd("rms1_fwd", x, small["g_mix"] + first_token)
    u = _mm_nt("proj_u", h1, w_in_t, tm=1024, tn=512, tk=2048, o_dtype=F32, n_rows=POOL_WIDTH, b_row_off=0)
    qkv = _mm_nt("proj_qkv", h1, w_in_t, tm=1024, tn=512, tk=2048, o_dtype=BF16, n_rows=QKV_WIDTH, b_row_off=o_qkv)
    gates_pre = _mm_nt("proj_gates", h1, w_in_t, tm=1024, tn=512, tk=2048, o_dtype=F32, n_rows=2 * D_MODEL,
                       b_row_off=o_gate)
    pool_y = _pool_fwd(u, W["w_pool_lin"], small["pool_scale"])
    outs, lses = zip(*[_attn_fwd(gi, qkv) for gi in range(3)])
    attn_y = _merge_fwd(outs, lses)
    if late_weights is not None:
        W = dict(W, **late_weights(attn_y))
    w_up_t = W["w_up"]
    mixed, y_pool, y_attn = _mix_fwd(pool_y, attn_y, W["w_pool_out"], W["w_attn_out"], gates_pre, small["b_gate"])
    x1 = _mm_nn("out_proj", mixed, W["w_out"], tm=1024, tn=1024, tk=2048, o_dtype=F32, res=x)

    h2 = _rms_fwd("rms2_fwd", x1, small["g_ffn"])
    nj = D_FF // 512
    up_pre = _mm("up_proj", h2, w_up_t, mode="nt", grid=(S // 1024, 2 * nj, 1),
                 a_spec=pl.BlockSpec((1024, D_MODEL), lambda i, j, k: (i, 0)),
                 b_spec=pl.BlockSpec((512, D_MODEL), lambda i, j, k: (j, 0)),
                 o_spec=pl.BlockSpec((None, 1024, 512), lambda i, j, k: (j // nj, i, j % nj)),
                 o_shape=(2, S, D_FF), o_dtype=BF16, acc_shape=(1024, 512))
    act = _convffn_fwd(up_pre, small["conv_w_full"], small["conv_b"])
    x2 = _mm_nn("down_proj", act, W["w_down"], tm=1024, tn=1024, tk=D_FF // 2, o_dtype=F32, res=x1)

    dx2, dx2_b, dg_final, sq_cols = _final_loss("final_loss", x2, small["g_final"], target)

    dact = _mm_nt("d_act", dx2_b, W["w_down"], tm=1024, tn=D_FF // 4, tk=2048, o_dtype=BF16)
    dw_down = _mm_tn("dw_down", act, dx2_b, tm=512, tn=2048, tk=TOKEN_TK)
    tok = emit(dict(w_down=dw_down))
    dup, dconv_w, dconv_b = _convffn_bwd_a(up_pre, dact, small["conv_w_full"], small["conv_b"] + zero(tok))
    dup_pre = _convffn_bwd_b(dup, small["conv_w_full"])
    tkh = D_FF // 2
    dh2 = _mm("d_h2", dup_pre, w_up_t, mode="nn", grid=(S // 1024, D_MODEL // 1024, 4),
              a_spec=pl.BlockSpec((None, 1024, tkh), lambda i, j, k: (k // 2, i, k % 2)),
              b_spec=pl.BlockSpec((tkh, 1024), lambda i, j, k: (k, j)),
              o_spec=pl.BlockSpec((1024, 1024), lambda i, j, k: (i, j)),
              o_shape=(S, D_MODEL), o_dtype=F32, acc_shape=(1024, 1024))
    dw_up = _mm("dw_up", dup_pre, h2, mode="tn", grid=(2 * nj, 1, S // TOKEN_TK),
                a_spec=pl.BlockSpec((None, TOKEN_TK, 512), lambda i, j, k: (i // nj, k, i % nj)),
                b_spec=pl.BlockSpec((TOKEN_TK, D_MODEL), lambda i, j, k: (k, 0)),
                o_spec=pl.BlockSpec((512, D_MODEL), lambda i, j, k: (i, 0)),
                o_shape=(2 * D_FF, D_MODEL), o_dtype=BF16, acc_shape=(512, D_MODEL))
    tok = emit(dict(w_up=dw_up, conv_w=dconv_w))
    dx1, dx1_b, dg_ffn = _rms_bwd("rms2_bwd", x1, small["g_ffn"] + zero(tok), dh2, dx2)

    dmixed = _mm_nt("d_mixed", dx1_b, W["w_out"], tm=1024, tn=1024, tk=2048, o_dtype=BF16)
    dw_out = _mm_tn("dw_out", mixed, dx1_b, tm=512, tn=2048, tk=TOKEN_TK)
    tok = emit(dict(w_out=dw_out))
    dy_pool, dy_attn, dgates, db_gate = _mix_bwd(dmixed, gates_pre, small["b_gate"] + zero(tok), y_pool, y_attn)
    dpool_y = _mm_nn("d_pool_y", dy_pool, W["w_pool_out"], tm=1024, tn=1024, tk=2048, o_dtype=F32)
    dw_pool_out = _mm_tn("dw_pool_out", dy_pool, pool_y, tm=1024, tn=1024, tk=TOKEN_TK)
    dattn_y = _mm_nn("d_attn_y", dy_attn, W["w_attn_out"], tm=1024, tn=512, tk=2048, o_dtype=F32)
    dw_attn_out = _mm_tn("dw_attn_out", dy_attn, attn_y, tm=1024, tn=512, tk=TOKEN_TK)
    e, dw_pool_lin, dpool_scale = _pool_bwd_a(u, dpool_y, W["w_pool_lin"], small["pool_scale"])
    du = _pool_bwd_b(e)
    dos, dps = _merge_bwd(dattn_y, outs, lses)
    dqkv = [_attn_bwd(gi, qkv, dos[gi], lses[gi], dps[gi]) for gi in range(3)]
    dproj = jnp.concatenate([du] + [dqkv[gi][sec] for sec in range(3) for gi in range(3)] + [dgates], axis=1)
    dw_in = _mm_tn("dw_in", dproj, h1, tm=512, tn=2048, tk=TOKEN_TK)
    tok = emit(dict(w_in=dw_in, w_pool_out=dw_pool_out, w_attn_out=dw_attn_out, w_pool_lin=dw_pool_lin))
    dh1 = _mm_nn("d_h1", dproj, w_in_t, tm=1024, tn=1024, tk=IN_WIDTH // 4, o_dtype=F32,
                 after=tok)
    grad_x, _, dg_mix = _rms_bwd("rms1_bwd", x, small["g_mix"] + zero(tok), dh1, dx1)

    grads = dict(w_in=dw_in, w_up=dw_up, w_down=dw_down, w_out=dw_out, w_pool_out=dw_pool_out,
                 w_attn_out=dw_attn_out, w_pool_lin=dw_pool_lin, conv_w=dconv_w)
    small_grads = dict(g_mix=dg_mix, b_gate=db_gate, pool_scale=dpool_scale, g_ffn=dg_ffn, conv_b=dconv_b,
                       g_final=dg_final)
    return sq_cols, grad_x, grads, small_grads


def kernel(x, g_mix, w_in, b_gate, w_pool_lin, pool_scale, w_pool_out, w_attn_out, w_out, g_ffn, w_up, conv_w, conv_b, w_down, g_final, loss_target, m_g_mix, m_w_in, m_b_gate, m_w_pool_lin, m_pool_scale, m_w_pool_out, m_w_attn_out, m_w_out, m_g_ffn, m_w_up, m_conv_w, m_conv_b, m_w_down, m_g_final, v_g_mix, v_w_in, v_b_gate, v_w_pool_lin, v_pool_scale, v_w_pool_out, v_w_attn_out, v_w_out, v_g_ffn, v_w_up, v_conv_w, v_conv_b, v_w_down, v_g_final):
    given = dict(g_mix=g_mix, w_in=w_in, b_gate=b_gate, w_pool_lin=w_pool_lin, pool_scale=pool_scale,
                 w_pool_out=w_pool_out, w_attn_out=w_attn_out, w_out=w_out, g_ffn=g_ffn, w_up=w_up, conv_w=conv_w,
                 conv_b=conv_b, w_down=w_down, g_final=g_final)
    mom_m = dict(g_mix=m_g_mix, w_in=m_w_in, b_gate=m_b_gate, w_pool_lin=m_w_pool_lin, pool_scale=m_pool_scale,
                 w_pool_out=m_w_pool_out, w_attn_out=m_w_attn_out, w_out=m_w_out, g_ffn=m_g_ffn, w_up=m_w_up,
                 conv_w=m_conv_w, conv_b=m_conv_b, w_down=m_w_down, g_final=m_g_final)
    mom_v = dict(g_mix=v_g_mix, w_in=v_w_in, b_gate=v_b_gate, w_pool_lin=v_w_pool_lin, pool_scale=v_pool_scale,
                 w_pool_out=v_w_pool_out, w_attn_out=v_w_attn_out, w_out=v_w_out, g_ffn=v_g_ffn, w_up=v_w_up,
                 conv_w=v_conv_w, conv_b=v_conv_b, w_down=v_w_down, g_final=v_g_final)
    def full(g):
        return g.reshape(N_DEV * g.shape[1], g.shape[2])

    shard = {n: _to_comm(n, given[n][0]).astype(BF16) for n in _BIG}
    W = dict(w_in=full(_all_gather_weights([shard["w_in"]])[0]))
    rest = [n for n in _BIG if n != "w_in"]
    lin_all = _all_gather_small(w_pool_lin[0].reshape(4 * 32, POOL_GROUP_WIDTH))
    W["w_pool_lin"] = jnp.moveaxis(lin_all.reshape(N_DEV, 4, 32, POOL_GROUP_WIDTH), 0, 1).reshape(
        4, POOL_GROUP_WIDTH, POOL_GROUP_WIDTH).astype(BF16)
    conv_w_all = _all_gather_small(jnp.pad(conv_w[0], ((0, CONV_ROWS - 3), (0, 0))))
    conv_w_full = jnp.moveaxis(conv_w_all[:, :3, :], 0, 1).reshape(3, 2 * D_FF)

    rest_wait, rest_token = _split_copies(
        "gather_rest", [shard[n] for n in rest], _place_own([shard[n] for n in rest]), 7 * len(rest),
        _gather_direct_plan(len(rest)), start_after=(W["w_in"], lin_all, conv_w_all),
        wait_after=lambda: rest_after[0])
    rest_after = []

    def late_weights(after):
        rest_after.append(after)
        return {n: full(l) for n, l in zip(rest, rest_wait()[1])}

    small = dict(g_mix=g_mix, b_gate=b_gate, pool_scale=pool_scale, g_ffn=g_ffn, conv_b=conv_b,
                 g_final=g_final.reshape(1, D_MODEL), conv_w_full=conv_w_full)

    pending = []
    done = {}

    def emit(group):
        names = list(group)
        parts = []
        for n in names:
            g = group[n]
            if n == "w_pool_lin":
                g = jnp.moveaxis(g.reshape(4, N_DEV, 32, POOL_GROUP_WIDTH), 1, 0).reshape(
                    N_DEV, 4 * 32, POOL_GROUP_WIDTH).astype(BF16)
            elif n == "conv_w":
                g = jnp.pad(jnp.moveaxis(g.reshape(3, N_DEV, 2 * D_FF // N_DEV), 1, 0),
                            ((0, 0), (0, CONV_ROWS - 3), (0, 0))).astype(BF16)
            else:
                g = g.reshape(N_DEV, g.shape[0] // N_DEV, g.shape[1])
            parts.append(g)
        tag = names[0]
        if "w_in" in group:
            got = _exchange_d2d("exchange_d2d_" + tag, parts)
            sums, lands = zip(*[_pair_sum("pair_sum_" + n, p, r) for n, p, r in zip(names, parts, got)])
            wait, token = _split_copies(
                "exchange_ici_" + tag, sums, lands, 3 * len(names), _exchange_ici_plan(len(names)),
                start_after=(got[0],), wait_after=lambda: done["grad_x"])
        else:
            me = _index(_position())
            lands = [lax.dynamic_update_slice(lax.empty(p.shape, p.dtype), lax.dynamic_index_in_dim(p, me, 0),
                                              (me, 0, 0)) for p in parts]
            wait, token = _split_copies(
                "exchange_direct_" + tag, parts, lands, 7 * len(names), _exchange_direct_plan(len(names)),
                start_after=(), wait_after=lambda: done["grad_x"])
        pending.append((names, wait))
        return token

    sq_cols, grad_x, _, small_grads = _local_step(x[0], loss_target[0], W, small, first_token=rest_token[0, 0],
                                                  late_weights=late_weights, emit=emit)
    done["grad_x"] = grad_x
    loss = lax.psum(0.5 / D_MODEL * jnp.sum(sq_cols), AXES)

    res = {}
    for n, r in [(n, r) for names, wait in pending for n, r in zip(names, wait()[1])]:
        g = _sum_partials("sum_" + n, r)
        if n in _TRANSPOSED:
            g = g.T
        elif n == "conv_w":
            g = g[:3]
        shape = given[n].shape
        rows = math.prod(shape[:-1])
        w2, m2, v2 = (d[n].reshape(rows, shape[-1]) for d in (given, mom_m, mom_v))
        upd = _adamw_update("adamw_" + n, w2, g.reshape(rows, shape[-1]), m2, v2)
        res[n] = [a.reshape(shape) for a in (g,) + tuple(upd)]

    packed_small = [_pack_small(d) for d in (given, mom_m, mom_v)]
    res_small = [_unpack_small(p, given) for p in _small_allreduce_adamw(_pack_small(small_grads), *packed_small)]

    order = ["g_mix", "w_in", "b_gate", "w_pool_lin", "pool_scale", "w_pool_out", "w_attn_out", "w_out", "g_ffn",
             "w_up", "conv_w", "conv_b", "w_down", "g_final"]
    outs = [loss, grad_x[None]]
    for kind in range(4):
        for n in order:
            outs.append(res[n][kind] if n in res else res_small[kind][n])
    return tuple(outs)
```

```python
import math

import jax
import jax.numpy as jnp
from jax import lax
from jax.experimental import pallas as pl
from jax.experimental.pallas import tpu as pltpu

F32 = jnp.float32
BF16 = jnp.bfloat16
MESH = pl.DeviceIdType.MESH
AXES = ("x", "y", "c")
N_DEV = 8

D_MODEL = 2048
POOL_WINDOWS = (2, 4, 8, 16)
POOL_GROUP_WIDTH = 256
POOL_WIDTH = 1024
ATTN_GROUPS = ((128, 1), (512, 4), (2048, 16))
SPAN = 128
HEADS_PER_GROUP = 4
N_ATTN_HEADS = 12
HEAD_DIM = 128
ATTN_WIDTH = 1536
QKV_WIDTH = 3 * ATTN_WIDTH
ATTN_OUT_WIDTH = 512
IN_WIDTH = 9728
D_FF = 5632
RMS_EPS = 1e-6
ADAM_LR, ADAM_B1, ADAM_B2, ADAM_EPS, ADAM_WD, ADAM_STEP = 0.001, 0.9, 0.999, 1e-08, 0.01, 10

VMEM_LIMIT_BYTES = 48 * 1024 * 1024
HALO = 16
TOKEN_TK = 2048
PACK_COLS = 1024
SMALL_ROWS = 24
NEG_BIG = -1e30

_DN = {"nn": (((1,), (0,)), ((), ())), "nt": (((1,), (1,)), ((), ())), "tn": (((0,), (0,)), ((), ()))}


def _params(*sem):
    return pltpu.CompilerParams(dimension_semantics=sem, vmem_limit_bytes=VMEM_LIMIT_BYTES)


def _dot(a, b, mode):
    return lax.dot_general(a.astype(BF16), b.astype(BF16), _DN[mode], preferred_element_type=F32)


def _mm(name, a, b, *, mode, grid, a_spec, b_spec, o_spec, o_shape, o_dtype, acc_shape, res=None, res_spec=None,
        after=None):
    nk = grid[2]
    n_in = 2 + (res is not None) + (after is not None)

    def body(*refs):
        a_ref, b_ref = refs[:2]
        r_ref = refs[2] if res is not None else None
        o_ref = refs[n_in]
        scr = refs[n_in + 1:]
        prod = _dot(a_ref[...], b_ref[...], mode)

        def finish(val):
            if r_ref is not None:
                val = val + r_ref[...]
            o_ref[...] = val.astype(o_ref.dtype)

        if nk == 1:
            finish(prod)
        else:
            acc = scr[0]
            k = pl.program_id(2)

            @pl.when(k == 0)
            def _():
                acc[...] = prod

            @pl.when(k > 0)
            def _():
                acc[...] += prod

            @pl.when(k == nk - 1)
            def _():
                finish(acc[...])

    in_specs = [a_spec, b_spec]
    args = [a, b]
    if res is not None:
        in_specs.append(res_spec)
        args.append(res)
    if after is not None:
        in_specs.append(pl.BlockSpec((8, 128), lambda i, j, k: (0, 0)))
        args.append(after)
    return pl.pallas_call(
        body, grid=grid, in_specs=in_specs, out_specs=o_spec,
        out_shape=jax.ShapeDtypeStruct(o_shape, o_dtype),
        scratch_shapes=[] if nk == 1 else [pltpu.VMEM(acc_shape, F32)],
        compiler_params=_params("parallel", "parallel", "arbitrary"), name=name)(*args)


def _mm_nn(name, a, b, *, tm, tn, tk, o_dtype, n_cols=None, b_col_off=0, res=None, after=None):
    M, K = a.shape
    N = b.shape[1] if n_cols is None else n_cols
    off = b_col_off // tn
    return _mm(name, a, b, mode="nn", grid=(M // tm, N // tn, K // tk),
               a_spec=pl.BlockSpec((tm, tk), lambda i, j, k: (i, k)),
               b_spec=pl.BlockSpec((tk, tn), lambda i, j, k: (k, j + off)),
               o_spec=pl.BlockSpec((tm, tn), lambda i, j, k: (i, j)),
               o_shape=(M, N), o_dtype=o_dtype, acc_shape=(tm, tn), res=res,
               res_spec=pl.BlockSpec((tm, tn), lambda i, j, k: (i, j)), after=after)


def _mm_nt(name, a, b, *, tm, tn, tk, o_dtype, n_rows=None, b_row_off=0):
    M, K = a.shape
    N = b.shape[0] if n_rows is None else n_rows
    off = b_row_off // tn
    return _mm(name, a, b, mode="nt", grid=(M // tm, N // tn, K // tk),
               a_spec=pl.BlockSpec((tm, tk), lambda i, j, k: (i, k)),
               b_spec=pl.BlockSpec((tn, tk), lambda i, j, k: (j + off, k)),
               o_spec=pl.BlockSpec((tm, tn), lambda i, j, k: (i, j)),
               o_shape=(M, N), o_dtype=o_dtype, acc_shape=(tm, tn))


def _mm_tn(name, a, b, *, tm, tn, tk):
    K, M = a.shape
    N = b.shape[1]
    return _mm(name, a, b, mode="tn", grid=(M // tm, N // tn, K // tk),
               a_spec=pl.BlockSpec((tk, tm), lambda i, j, k: (k, i)),
               b_spec=pl.BlockSpec((tk, tn), lambda i, j, k: (k, j)),
               o_spec=pl.BlockSpec((tm, tn), lambda i, j, k: (i, j)),
               o_shape=(M, N), o_dtype=BF16, acc_shape=(tm, tn))


def _rms_fwd(name, x, g):
    S, D = x.shape
    T = 512

    def body(x_ref, g_ref, h_ref):
        xv = x_ref[...]
        r = lax.rsqrt(jnp.mean(xv * xv, axis=-1, keepdims=True) + RMS_EPS)
        h_ref[...] = (xv * r * g_ref[...]).astype(BF16)

    return pl.pallas_call(
        body, grid=(S // T,),
        in_specs=[pl.BlockSpec((T, D), lambda i: (i, 0)), pl.BlockSpec((1, D), lambda i: (0, 0))],
        out_specs=pl.BlockSpec((T, D), lambda i: (i, 0)),
        out_shape=jax.ShapeDtypeStruct((S, D), BF16), compiler_params=_params("parallel"), name=name)(x, g)


def _rms_bwd(name, x, g, dh, dres):
    S, D = x.shape
    T = 256

    def body(x_ref, g_ref, dh_ref, dres_ref, dx_ref, dxb_ref, dg_ref):
        i = pl.program_id(0)
        xv = x_ref[...]
        r = lax.rsqrt(jnp.mean(xv * xv, axis=-1, keepdims=True) + RMS_EPS)
        xhat = xv * r
        dhv = dh_ref[...].astype(F32)
        gdh = dhv * g_ref[...]
        c = jnp.mean(xhat * gdh, axis=-1, keepdims=True)
        dx = dres_ref[...] + r * (gdh - xhat * c)
        dx_ref[...] = dx
        dxb_ref[...] = dx.astype(BF16)
        part = jnp.sum(dhv * xhat, axis=0, keepdims=True)

        @pl.when(i == 0)
        def _():
            dg_ref[...] = part

        @pl.when(i > 0)
        def _():
            dg_ref[...] += part

    row = pl.BlockSpec((T, D), lambda i: (i, 0))
    vec = pl.BlockSpec((1, D), lambda i: (0, 0))
    return pl.pallas_call(
        body, grid=(S // T,), in_specs=[row, vec, row, row], out_specs=[row, row, vec],
        out_shape=[jax.ShapeDtypeStruct((S, D), F32), jax.ShapeDtypeStruct((S, D), BF16),
                   jax.ShapeDtypeStruct((1, D), F32)],
        compiler_params=_params("arbitrary"), name=name)(x, g, dh, dres)


def _final_loss(name, x, g, target):
    S, D = x.shape
    T = 256

    def body(x_ref, g_ref, t_ref, dx_ref, dxb_ref, dg_ref, sq_ref):
        i = pl.program_id(0)
        xv = x_ref[...]
        r = lax.rsqrt(jnp.mean(xv * xv, axis=-1, keepdims=True) + RMS_EPS)
        xhat = xv * r
        err = xhat * g_ref[...] - t_ref[...]
        dy = err * (1.0 / D)
        gdy = dy * g_ref[...]
        c = jnp.mean(xhat * gdy, axis=-1, keepdims=True)
        dx = r * (gdy - xhat * c)
        dx_ref[...] = dx
        dxb_ref[...] = dx.astype(BF16)
        dg_part = jnp.sum(dy * xhat, axis=0, keepdims=True)
        sq_part = jnp.sum(err * err, axis=0, keepdims=True)

        @pl.when(i == 0)
        def _():
            dg_ref[...] = dg_part
            sq_ref[...] = sq_part

        @pl.when(i > 0)
        def _():
            dg_ref[...] += dg_part
            sq_ref[...] += sq_part

    row = pl.BlockSpec((T, D), lambda i: (i, 0))
    vec = pl.BlockSpec((1, D), lambda i: (0, 0))
    return pl.pallas_call(
        body, grid=(S // T,), in_specs=[row, vec, row], out_specs=[row, row, vec, vec],
        out_shape=[jax.ShapeDtypeStruct((S, D), F32), jax.ShapeDtypeStruct((S, D), BF16),
                   jax.ShapeDtypeStruct((1, D), F32), jax.ShapeDtypeStruct((1, D), F32)],
        compiler_params=_params("arbitrary"), name=name)(x, g, target)


def _prev_halo_spec(T, C, col_map):
    per = T // HALO
    return pl.BlockSpec((HALO, C), lambda *g: (jnp.maximum(g[0] * per - 1, 0), col_map(*g)))


def _pool_window_sums(ext, T, i):
    t = i * T + lax.broadcasted_iota(jnp.int32, (T, 1), 0)
    outs = []
    cnts = []
    for gi, w in enumerate(POOL_WINDOWS):
        s = ext[:, gi * POOL_GROUP_WIDTH:(gi + 1) * POOL_GROUP_WIDTH]
        sh = 1
        while sh < w:
            s = s + pltpu.roll(s, sh, 0)
            sh *= 2
        cnt = jnp.minimum(t + 1, w).astype(F32)
        outs.append(s[HALO:] / cnt - ext[HALO:, gi * POOL_GROUP_WIDTH:(gi + 1) * POOL_GROUP_WIDTH])
        cnts.append(cnt)
    return outs, cnts


def _load_ext(ext_ref, halo_ref, blk_ref, first):
    ext_ref[0:HALO, :] = jnp.where(first, 0.0, halo_ref[...].astype(F32))
    ext_ref[HALO:, :] = blk_ref[...].astype(F32)


def _pool_fwd(u, w_lin, scale):
    S = u.shape[0]
    T = 512
    W = POOL_WIDTH

    def body(u_ref, halo_ref, wl_ref, sc_ref, y_ref, ext_ref):
        i = pl.program_id(0)
        _load_ext(ext_ref, halo_ref, u_ref, i == 0)
        pooled, _ = _pool_window_sums(ext_ref[...], T, i)
        for gi in range(4):
            z = _dot(pooled[gi], wl_ref[gi], "nn")
            cs = slice(gi * POOL_GROUP_WIDTH, (gi + 1) * POOL_GROUP_WIDTH)
            y_ref[:, cs] = (z * sc_ref[:, cs]).astype(BF16)

    return pl.pallas_call(
        body, grid=(S // T,),
        in_specs=[pl.BlockSpec((T, W), lambda i: (i, 0)), _prev_halo_spec(T, W, lambda i: 0),
                  pl.BlockSpec((4, 256, 256), lambda i: (0, 0, 0)), pl.BlockSpec((1, W), lambda i: (0, 0))],
        out_specs=pl.BlockSpec((T, W), lambda i: (i, 0)),
        out_shape=jax.ShapeDtypeStruct((S, W), BF16),
        scratch_shapes=[pltpu.VMEM((HALO + T, W), F32)],
        compiler_params=_params("parallel"), name="pool_fwd")(u, u, w_lin, scale)


def _pool_bwd_a(u, dy, w_lin, scale):
    S = u.shape[0]
    T = 512
    W = POOL_WIDTH

    def body(u_ref, halo_ref, dy_ref, wl_ref, sc_ref, e_ref, dwl_ref, dsc_ref, ext_ref):
        i = pl.program_id(0)
        _load_ext(ext_ref, halo_ref, u_ref, i == 0)
        pooled, cnts = _pool_window_sums(ext_ref[...], T, i)
        for gi in range(4):
            cs = slice(gi * POOL_GROUP_WIDTH, (gi + 1) * POOL_GROUP_WIDTH)
            z = _dot(pooled[gi], wl_ref[gi], "nn")
            dyg = dy_ref[:, cs]
            dsc = jnp.sum(dyg * z, axis=0, keepdims=True)
            dz = dyg * sc_ref[:, cs]
            dwl = _dot(pooled[gi], dz, "tn")
            dpooled = _dot(dz, wl_ref[gi], "nt")
            e_ref[:, cs] = dpooled / cnts[gi]

            @pl.when(i == 0)
            def _():
                dwl_ref[gi] = dwl
                dsc_ref[:, cs] = dsc

            @pl.when(i > 0)
            def _():
                dwl_ref[gi] += dwl
                dsc_ref[:, cs] += dsc

    return pl.pallas_call(
        body, grid=(S // T,),
        in_specs=[pl.BlockSpec((T, W), lambda i: (i, 0)), _prev_halo_spec(T, W, lambda i: 0),
                  pl.BlockSpec((T, W), lambda i: (i, 0)),
                  pl.BlockSpec((4, 256, 256), lambda i: (0, 0, 0)), pl.BlockSpec((1, W), lambda i: (0, 0))],
        out_specs=[pl.BlockSpec((T, W), lambda i: (i, 0)), pl.BlockSpec((4, 256, 256), lambda i: (0, 0, 0)),
                   pl.BlockSpec((1, W), lambda i: (0, 0))],
        out_shape=[jax.ShapeDtypeStruct((S, W), F32), jax.ShapeDtypeStruct((4, 256, 256), F32),
                   jax.ShapeDtypeStruct((1, W), F32)],
        scratch_shapes=[pltpu.VMEM((HALO + T, W), F32)],
        compiler_params=_params("arbitrary"), name="pool_bwd_a")(u, u, dy, w_lin, scale)


def _pool_bwd_b(e):
    S = e.shape[0]
    T = 512
    W = POOL_WIDTH
    per = T // HALO
    n_blk = S // T

    def body(e_ref, halo_ref, du_ref, ext_ref):
        i = pl.program_id(0)
        ext_ref[0:T, :] = e_ref[...]
        ext_ref[T:, :] = jnp.where(i == n_blk - 1, 0.0, halo_ref[...])
        ext = ext_ref[...]
        n = T + HALO
        t = i * T + lax.broadcasted_iota(jnp.int32, (T, 1), 0)
        for gi, w in enumerate(POOL_WINDOWS):
            cs = slice(gi * POOL_GROUP_WIDTH, (gi + 1) * POOL_GROUP_WIDTH)
            s = ext[:, cs]
            sh = 1
            while sh < w:
                s = s + pltpu.roll(s, n - sh, 0)
                sh *= 2
            cnt = jnp.minimum(t + 1, w).astype(F32)
            du_ref[:, cs] = (s[0:T] - ext[0:T, cs] * cnt).astype(BF16)

    return pl.pallas_call(
        body, grid=(n_blk,),
        in_specs=[pl.BlockSpec((T, W), lambda i: (i, 0)),
                  pl.BlockSpec((HALO, W), lambda i: (jnp.minimum((i + 1) * per, S // HALO - 1), 0))],
        out_specs=pl.BlockSpec((T, W), lambda i: (i, 0)),
        out_shape=jax.ShapeDtypeStruct((S, W), BF16),
        scratch_shapes=[pltpu.VMEM((T + HALO, W), F32)],
        compiler_params=_params("parallel"), name="pool_bwd_b")(e, e)


def _slope(head):
    return 2.0 ** (-8.0 * (head + 1) / N_ATTN_HEADS)


def _attn_masks(b):
    row = lax.broadcasted_iota(jnp.int32, (SPAN, SPAN), 0)
    col = lax.broadcasted_iota(jnp.int32, (SPAN, SPAN), 1)
    dist_cur = (row - col).astype(F32)
    valid_cur = col <= row
    valid_prev = jnp.logical_and(col >= row, b > 0)
    return dist_cur, valid_cur, valid_prev


ATTN_BATCH = 4


def _attn_tiling(gi, S):
    d = ATTN_GROUPS[gi][1]
    hp = HEADS_PER_GROUP if d < 16 else 1
    return d, SPAN * d, S // (SPAN * d), hp, hp * HEAD_DIM, ATTN_BATCH // hp


_BATCHED_DN = {"nt": (((2,), (2,)), ((0,), (0,))), "nn": (((2,), (1,)), ((0,), (0,))), "tn": (((1,), (1,)), ((0,), (0,)))}


def _bdot(a, b, mode):
    return lax.dot_general(a.astype(BF16), b.astype(BF16), _BATCHED_DN[mode], preferred_element_type=F32)


def _batch_rows(it, d, rb):
    return [pl.ds(it * rb + j, SPAN, stride=d) for j in range(rb)]


def _take(scr, hp, rows):
    return jnp.stack([scr[hi, rw, :] for hi in range(hp) for rw in rows])


def _put(scr, hp, rows, val):
    for n, (hi, rw) in enumerate((hi, rw) for hi in range(hp) for rw in rows):
        scr[hi, rw, :] = val[n]


def _batch_bias(gi, d, hp, rb, dist):
    return jnp.stack([(-(_head_slope(gi, hp, hi) * d)) * dist for hi in range(hp) for _ in range(rb)])


def _head_slope(gi, hp, hi):
    if hp == HEADS_PER_GROUP:
        return _slope(gi * HEADS_PER_GROUP + hi)
    first = pl.program_id(1) * hp
    sl = jnp.float32(_slope(gi * HEADS_PER_GROUP + hi))
    for h0 in range(hp, HEADS_PER_GROUP, hp):
        sl = jnp.where(first == h0, jnp.float32(_slope(gi * HEADS_PER_GROUP + h0 + hi)), sl)
    return sl


def _split_heads(src_ref, dst_ref, hp):
    for hi in range(hp):
        dst_ref[hi] = src_ref[:, hi * HEAD_DIM:(hi + 1) * HEAD_DIM].astype(F32)


def _attn_fwd(gi, qkv):
    S = qkv.shape[0]
    d, C, nb, hp, W, rb = _attn_tiling(gi, S)
    scale = HEAD_DIM ** -0.5

    def body(q_ref, kc_ref, kp_ref, vc_ref, vp_ref, o_ref, lse_ref, qs, kcs, kps, vcs, vps, os, ls):
        b = pl.program_id(0)
        for src, dst in ((q_ref, qs), (kc_ref, kcs), (kp_ref, kps), (vc_ref, vcs), (vp_ref, vps)):
            _split_heads(src, dst, hp)
        dist_cur, valid_cur, valid_prev = _attn_masks(b)
        bias_cur = _batch_bias(gi, d, hp, rb, dist_cur)
        bias_far = _batch_bias(gi, d, hp, rb, dist_cur + SPAN)

        def sub_sequences(it, carry):
            rows = _batch_rows(it, d, rb)
            q = _take(qs, hp, rows)
            s_cur = jnp.where(valid_cur, _bdot(q, _take(kcs, hp, rows), "nt") * scale + bias_cur, NEG_BIG)
            s_prev = jnp.where(valid_prev, _bdot(q, _take(kps, hp, rows), "nt") * scale + bias_far, NEG_BIG)
            m = jnp.maximum(jnp.max(s_cur, axis=-1, keepdims=True), jnp.max(s_prev, axis=-1, keepdims=True))
            p_cur = jnp.exp(s_cur - m)
            p_prev = jnp.exp(s_prev - m)
            l = jnp.sum(p_cur, axis=-1, keepdims=True) + jnp.sum(p_prev, axis=-1, keepdims=True)
            o = (_bdot(p_cur, _take(vcs, hp, rows), "nn") + _bdot(p_prev, _take(vps, hp, rows), "nn")) / l
            _put(os, hp, rows, o)
            _put(ls, hp, rows, jnp.broadcast_to(m + jnp.log(l), o.shape))
            return carry

        lax.fori_loop(0, d // rb, sub_sequences, 0)
        for hi in range(hp):
            hs = slice(hi * HEAD_DIM, (hi + 1) * HEAD_DIM)
            o_ref[:, hs] = os[hi]
            lse_ref[:, hs] = ls[hi]

    per_sec = ATTN_WIDTH // W
    per_grp = ATTN_OUT_WIDTH // W

    def spec(sec, shift):
        return pl.BlockSpec((C, W), lambda b, h: (jnp.maximum(b + shift, 0), sec * per_sec + gi * per_grp + h))

    out = pl.BlockSpec((C, W), lambda b, h: (b, h))
    scr = pltpu.VMEM((hp, C, HEAD_DIM), F32)
    return pl.pallas_call(
        body, grid=(nb, HEADS_PER_GROUP // hp),
        in_specs=[spec(0, 0), spec(1, 0), spec(1, -1), spec(2, 0), spec(2, -1)], out_specs=[out, out],
        out_shape=[jax.ShapeDtypeStruct((S, ATTN_OUT_WIDTH), F32)] * 2, scratch_shapes=[scr] * 7,
        compiler_params=_params("parallel", "parallel"), name=f"attn_fwd_g{gi}")(qkv, qkv, qkv, qkv, qkv)


def _attn_bwd(gi, qkv, do, lse, dp):
    S = qkv.shape[0]
    d, C, nb, hp, W, rb = _attn_tiling(gi, S)
    scale = HEAD_DIM ** -0.5
    n_in = 12

    def body(*refs):
        ins, (dq_ref, dk_ref, dv_ref) = refs[:n_in], refs[n_in:n_in + 3]
        scr = refs[n_in + 3:]
        for src, dst in zip(ins, scr[:n_in]):
            _split_heads(src, dst, hp)
        qs, kcs, vcs, kps, vps, qns, dos, dons, lses, lsens, dps, dpns = scr[:n_in]
        dqs, dks, dvs = scr[n_in:]
        b = pl.program_id(0)
        dist_cur, valid_cur, valid_prev = _attn_masks(b)
        row = lax.broadcasted_iota(jnp.int32, (SPAN, SPAN), 0)
        col = lax.broadcasted_iota(jnp.int32, (SPAN, SPAN), 1)
        valid_next = jnp.logical_and(col >= row, b < nb - 1)

        bias_cur = _batch_bias(gi, d, hp, rb, dist_cur)
        bias_far = _batch_bias(gi, d, hp, rb, dist_cur + SPAN)

        def sub_sequences(it, carry):
            rows = _batch_rows(it, d, rb)
            q, kc, vc, kp, vp, qn, dov, don = (_take(s, hp, rows) for s in (qs, kcs, vcs, kps, vps, qns, dos, dons))
            lse_b, lse_n, dp_b, dp_n = (_take(s, hp, rows) for s in (lses, lsens, dps, dpns))
            p_cur = jnp.where(valid_cur, jnp.exp(_bdot(q, kc, "nt") * scale + bias_cur - lse_b), 0.0)
            p_prev = jnp.where(valid_prev, jnp.exp(_bdot(q, kp, "nt") * scale + bias_far - lse_b), 0.0)
            ds_cur = p_cur * (_bdot(dov, vc, "nt") - dp_b)
            ds_prev = p_prev * (_bdot(dov, vp, "nt") - dp_b)
            _put(dqs, hp, rows, (_bdot(ds_cur, kc, "nn") + _bdot(ds_prev, kp, "nn")) * scale)
            p_next = jnp.where(valid_next, jnp.exp(_bdot(qn, kc, "nt") * scale + bias_far - lse_n), 0.0)
            ds_next = p_next * (_bdot(don, vc, "nt") - dp_n)
            _put(dvs, hp, rows, _bdot(p_cur, dov, "tn") + _bdot(p_next, don, "tn"))
            _put(dks, hp, rows, (_bdot(ds_cur, q, "tn") + _bdot(ds_next, qn, "tn")) * scale)
            return carry

        lax.fori_loop(0, d // rb, sub_sequences, 0)
        for hi in range(hp):
            hs = slice(hi * HEAD_DIM, (hi + 1) * HEAD_DIM)
            dq_ref[:, hs] = dqs[hi].astype(BF16)
            dk_ref[:, hs] = dks[hi].astype(BF16)
            dv_ref[:, hs] = dvs[hi].astype(BF16)

    per_sec = ATTN_WIDTH // W
    per_grp = ATTN_OUT_WIDTH // W

    def qkv_spec(sec, shift):
        return pl.BlockSpec((C, W), lambda b, h: (jnp.clip(b + shift, 0, nb - 1), sec * per_sec + gi * per_grp + h))

    def act_spec(shift):
        return pl.BlockSpec((C, W), lambda b, h: (jnp.clip(b + shift, 0, nb - 1), h))

    out = act_spec(0)
    scr = pltpu.VMEM((hp, C, HEAD_DIM), F32)
    return pl.pallas_call(
        body, grid=(nb, HEADS_PER_GROUP // hp),
        in_specs=[qkv_spec(0, 0), qkv_spec(1, 0), qkv_spec(2, 0), qkv_spec(1, -1), qkv_spec(2, -1), qkv_spec(0, 1),
                  act_spec(0), act_spec(1), act_spec(0), act_spec(1), act_spec(0), act_spec(1)],
        out_specs=[out, out, out], out_shape=[jax.ShapeDtypeStruct((S, ATTN_OUT_WIDTH), BF16)] * 3,
        scratch_shapes=[scr] * (n_in + 3),
        compiler_params=_params("parallel", "parallel"), name=f"attn_bwd_g{gi}")(
            qkv, qkv, qkv, qkv, qkv, qkv, do, do, lse, lse, dp, dp)


def _group_weights(l_refs):
    l0, l1, l2 = (r[...] for r in l_refs)
    m = jnp.maximum(jnp.maximum(l0, l1), l2)
    e = [jnp.exp(l0 - m), jnp.exp(l1 - m), jnp.exp(l2 - m)]
    tot = e[0] + e[1] + e[2]
    return [ei / tot for ei in e]


def _merge_fwd(outs, lses):
    S, W = outs[0].shape
    T = 512

    def body(o0, o1, o2, l0, l1, l2, y_ref):
        w = _group_weights((l0, l1, l2))
        y_ref[...] = (w[0] * o0[...] + w[1] * o1[...] + w[2] * o2[...]).astype(BF16)

    blk = pl.BlockSpec((T, W), lambda i: (i, 0))
    return pl.pallas_call(
        body, grid=(S // T,), in_specs=[blk] * 6, out_specs=blk, out_shape=jax.ShapeDtypeStruct((S, W), BF16),
        compiler_params=_params("parallel"), name="merge_fwd")(*outs, *lses)


def _merge_bwd(dy, outs, lses):
    S, W = outs[0].shape
    T = 512

    def body(dy_ref, o0, o1, o2, l0, l1, l2, do0, do1, do2, dp0, dp1, dp2):
        w = _group_weights((l0, l1, l2))
        dyv = dy_ref[...]
        y = w[0] * o0[...] + w[1] * o1[...] + w[2] * o2[...]
        prod = dyv * y
        dots = []
        for hi in range(HEADS_PER_GROUP):
            hs = slice(hi * HEAD_DIM, (hi + 1) * HEAD_DIM)
            dots.append(jnp.broadcast_to(jnp.sum(prod[:, hs], axis=-1, keepdims=True), (T, HEAD_DIM)))
        dot_b = jnp.concatenate(dots, axis=1)
        for wg, do_ref, dp_ref in zip(w, (do0, do1, do2), (dp0, dp1, dp2)):
            do_ref[...] = (wg * dyv).astype(BF16)
            dp_ref[...] = wg * dot_b

    blk = pl.BlockSpec((T, W), lambda i: (i, 0))
    res = pl.pallas_call(
        body, grid=(S // T,), in_specs=[blk] * 7, out_specs=[blk] * 6,
        out_shape=[jax.ShapeDtypeStruct((S, W), BF16)] * 3 + [jax.ShapeDtypeStruct((S, W), F32)] * 3,
        compiler_params=_params("parallel"), name="merge_bwd")(dy, *outs, *lses)
    return res[:3], res[3:]


def _mix_fwd(pool_y, attn_y, w_pool_out_t, w_attn_out_t, gates_pre, b_gate):
    S = pool_y.shape[0]
    D = D_MODEL
    tm, tn = 1024, 512
    nj = D // tn

    def body(p_ref, a_ref, wp_ref, wa_ref, g0_ref, g1_ref, b0_ref, b1_ref, mixed_ref, yp_ref, ya_ref):
        yp = _dot(p_ref[...], wp_ref[...], "nt")
        ya = _dot(a_ref[...], wa_ref[...], "nt")
        g0 = jax.nn.sigmoid(g0_ref[...] + b0_ref[...])
        g1 = jax.nn.sigmoid(g1_ref[...] + b1_ref[...])
        mixed_ref[...] = (g0 * yp + g1 * ya).astype(BF16)
        yp_ref[...] = yp.astype(BF16)
        ya_ref[...] = ya.astype(BF16)

    out = pl.BlockSpec((tm, tn), lambda i, j: (i, j))
    return pl.pallas_call(
        body, grid=(S // tm, nj),
        in_specs=[pl.BlockSpec((tm, POOL_WIDTH), lambda i, j: (i, 0)),
                  pl.BlockSpec((tm, ATTN_OUT_WIDTH), lambda i, j: (i, 0)),
                  pl.BlockSpec((tn, POOL_WIDTH), lambda i, j: (j, 0)),
                  pl.BlockSpec((tn, ATTN_OUT_WIDTH), lambda i, j: (j, 0)),
                  pl.BlockSpec((tm, tn), lambda i, j: (i, j)), pl.BlockSpec((tm, tn), lambda i, j: (i, j + nj)),
                  pl.BlockSpec((1, tn), lambda i, j: (0, j)), pl.BlockSpec((1, tn), lambda i, j: (0, j + nj))],
        out_specs=[out, out, out], out_shape=[jax.ShapeDtypeStruct((S, D), BF16)] * 3,
        compiler_params=_params("parallel", "parallel"), name="mix_fwd")(
            pool_y, attn_y, w_pool_out_t, w_attn_out_t, gates_pre, gates_pre, b_gate, b_gate)


def _mix_bwd(dmixed, gates_pre, b_gate, y_pool, y_attn):
    S = dmixed.shape[0]
    D = D_MODEL
    T = 256

    def body(dm_ref, gp_ref, b_ref, yp_ref, ya_ref, dyp_ref, dya_ref, dg_ref, db_ref):
        i = pl.program_id(0)
        dm = dm_ref[...].astype(F32)
        parts = []
        for br, (y_ref, dy_ref) in enumerate(((yp_ref, dyp_ref), (ya_ref, dya_ref))):
            cs = slice(br * D, (br + 1) * D)
            g = jax.nn.sigmoid(gp_ref[:, cs] + b_ref[:, cs])
            dy_ref[...] = (dm * g).astype(BF16)
            dpre = dm * y_ref[...].astype(F32) * g * (1.0 - g)
            dg_ref[:, cs] = dpre.astype(BF16)
            parts.append(jnp.sum(dpre, axis=0, keepdims=True))

        @pl.when(i == 0)
        def _():
            db_ref[:, 0:D] = parts[0]
            db_ref[:, D:2 * D] = parts[1]

        @pl.when(i > 0)
        def _():
            db_ref[:, 0:D] += parts[0]
            db_ref[:, D:2 * D] += parts[1]

    row = pl.BlockSpec((T, D), lambda i: (i, 0))
    row2 = pl.BlockSpec((T, 2 * D), lambda i: (i, 0))
    vec2 = pl.BlockSpec((1, 2 * D), lambda i: (0, 0))
    return pl.pallas_call(
        body, grid=(S // T,), in_specs=[row, row2, vec2, row, row], out_specs=[row, row, row2, vec2],
        out_shape=[jax.ShapeDtypeStruct((S, D), BF16), jax.ShapeDtypeStruct((S, D), BF16),
                   jax.ShapeDtypeStruct((S, 2 * D), BF16), jax.ShapeDtypeStruct((1, 2 * D), F32)],
        compiler_params=_params("arbitrary"), name="mix_bwd")(dmixed, gates_pre, b_gate, y_pool, y_attn)


_CONV_T, _CONV_C = 512, 512


def _conv_taps(ext_ref, h, w_ref, b_ref):
    ext = ext_ref[h]
    x2 = pltpu.roll(ext, 2, 0)[HALO:]
    x1 = pltpu.roll(ext, 1, 0)[HALO:]
    x0 = ext[HALO:]
    y = b_ref[...] + w_ref[0:1, :] * x2
    y = y + w_ref[1:2, :] * x1
    y = y + w_ref[2:3, :] * x0
    return y, (x2, x1, x0)


def _gelu_parts(a):
    cdf = 0.5 * (1.0 + lax.erf(a * (1.0 / math.sqrt(2.0))))
    return cdf, a * cdf


def _conv_specs(S):
    T, C = _CONV_T, _CONV_C
    nj = D_FF // C
    blk = pl.BlockSpec((2, T, C), lambda i, j: (0, i, j))
    per = T // HALO
    halo = pl.BlockSpec((2, HALO, C), lambda i, j: (0, jnp.maximum(i * per - 1, 0), j))
    w_a = pl.BlockSpec((3, C), lambda i, j: (0, j))
    w_b = pl.BlockSpec((3, C), lambda i, j: (0, j + nj))
    b_a = pl.BlockSpec((1, C), lambda i, j: (0, j))
    b_b = pl.BlockSpec((1, C), lambda i, j: (0, j + nj))
    return T, C, nj, blk, halo, w_a, w_b, b_a, b_b


def _conv_load(ext_ref, halo_ref, up_ref, first):
    for h in range(2):
        ext_ref[h, 0:HALO, :] = jnp.where(first, 0.0, halo_ref[h].astype(F32))
        ext_ref[h, HALO:, :] = up_ref[h].astype(F32)


def _convffn_fwd(up_pre, conv_w, conv_b):
    S = up_pre.shape[1]
    T, C, nj, blk, halo, w_a, w_b, b_a, b_b = _conv_specs(S)

    def body(up_ref, halo_ref, wa_ref, wb_ref, ba_ref, bb_ref, act_ref, ext_ref):
        _conv_load(ext_ref, halo_ref, up_ref, pl.program_id(0) == 0)
        a, _ = _conv_taps(ext_ref, 0, wa_ref, ba_ref)
        b, _ = _conv_taps(ext_ref, 1, wb_ref, bb_ref)
        _, gelu = _gelu_parts(a)
        act_ref[...] = (gelu * b).astype(BF16)

    return pl.pallas_call(
        body, grid=(S // T, nj), in_specs=[blk, halo, w_a, w_b, b_a, b_b],
        out_specs=pl.BlockSpec((T, C), lambda i, j: (i, j)),
        out_shape=jax.ShapeDtypeStruct((S, D_FF), BF16),
        scratch_shapes=[pltpu.VMEM((2, HALO + T, C), F32)],
        compiler_params=_params("parallel", "parallel"), name="convffn_fwd")(
            up_pre, up_pre, conv_w, conv_w, conv_b, conv_b)


def _convffn_bwd_a(up_pre, dact, conv_w, conv_b):
    S = up_pre.shape[1]
    T, C, nj, blk, halo, w_a, w_b, b_a, b_b = _conv_specs(S)
    def swap(spec):
        return pl.BlockSpec(spec.block_shape, lambda j, i, _m=spec.index_map: _m(i, j))

    def body(up_ref, halo_ref, dact_ref, wa_ref, wb_ref, ba_ref, bb_ref, dup_ref, dwa_ref, dwb_ref, dba_ref, dbb_ref,
             ext_ref):
        i = pl.program_id(1)
        _conv_load(ext_ref, halo_ref, up_ref, i == 0)
        a, xa = _conv_taps(ext_ref, 0, wa_ref, ba_ref)
        b, xb = _conv_taps(ext_ref, 1, wb_ref, bb_ref)
        cdf, gelu = _gelu_parts(a)
        dgelu = cdf + a * (jnp.exp(-0.5 * a * a) * (1.0 / math.sqrt(2.0 * math.pi)))
        dact = dact_ref[...].astype(F32)
        da = dact * b * dgelu
        db = dact * gelu
        dup_ref[0] = da.astype(BF16)
        dup_ref[1] = db.astype(BF16)
        for dval, xs, dw_ref, dbias_ref in ((da, xa, dwa_ref, dba_ref), (db, xb, dwb_ref, dbb_ref)):
            dw = jnp.concatenate([jnp.sum(dval * xk, axis=0, keepdims=True) for xk in xs], axis=0)
            dbias = jnp.sum(dval, axis=0, keepdims=True)

            @pl.when(i == 0)
            def _():
                dw_ref[...] = dw
                dbias_ref[...] = dbias

            @pl.when(i > 0)
            def _():
                dw_ref[...] += dw
                dbias_ref[...] += dbias

    dact_spec = pl.BlockSpec((T, C), lambda j, i: (i, j))
    dw_spec = pl.BlockSpec((3, C), lambda j, i: (0, j))
    db_spec = pl.BlockSpec((1, C), lambda j, i: (0, j))
    res = pl.pallas_call(
        body, grid=(nj, S // T),
        in_specs=[swap(blk), swap(halo), dact_spec, swap(w_a), swap(w_b), swap(b_a), swap(b_b)],
        out_specs=[swap(blk), dw_spec, dw_spec, db_spec, db_spec],
        out_shape=[jax.ShapeDtypeStruct((2, S, D_FF), BF16), jax.ShapeDtypeStruct((3, D_FF), F32),
                   jax.ShapeDtypeStruct((3, D_FF), F32), jax.ShapeDtypeStruct((1, D_FF), F32),
                   jax.ShapeDtypeStruct((1, D_FF), F32)],
        scratch_shapes=[pltpu.VMEM((2, HALO + T, C), F32)],
        compiler_params=_params("parallel", "arbitrary"), name="convffn_bwd_a")(
            up_pre, up_pre, dact, conv_w, conv_w, conv_b, conv_b)
    dup, dwa, dwb, dba, dbb = res
    dconv_w = jnp.concatenate([dwa, dwb], axis=1)
    dconv_b = jnp.concatenate([dba, dbb], axis=1)
    return dup, dconv_w, dconv_b


def _convffn_bwd_b(dup, conv_w):
    S = dup.shape[1]
    T, C = _CONV_T, _CONV_C
    nj = D_FF // C
    per = T // HALO
    n_blk = S // T

    def body(d_ref, halo_ref, wa_ref, wb_ref, o_ref, ext_ref):
        i = pl.program_id(0)
        n = T + HALO
        for h, w_ref in enumerate((wa_ref, wb_ref)):
            ext_ref[h, 0:T, :] = d_ref[h].astype(F32)
            ext_ref[h, T:, :] = jnp.where(i == n_blk - 1, 0.0, halo_ref[h].astype(F32))
            ext = ext_ref[h]
            y = w_ref[2:3, :] * ext[0:T] + w_ref[1:2, :] * pltpu.roll(ext, n - 1, 0)[0:T]
            y = y + w_ref[0:1, :] * pltpu.roll(ext, n - 2, 0)[0:T]
            o_ref[h] = y.astype(BF16)

    blk = pl.BlockSpec((2, T, C), lambda i, j: (0, i, j))
    halo = pl.BlockSpec((2, HALO, C), lambda i, j: (0, jnp.minimum((i + 1) * per, S // HALO - 1), j))
    return pl.pallas_call(
        body, grid=(n_blk, nj),
        in_specs=[blk, halo, pl.BlockSpec((3, C), lambda i, j: (0, j)), pl.BlockSpec((3, C), lambda i, j: (0, j + nj))],
        out_specs=blk, out_shape=jax.ShapeDtypeStruct((2, S, D_FF), BF16),
        scratch_shapes=[pltpu.VMEM((2, T + HALO, C), F32)],
        compiler_params=_params("parallel", "parallel"), name="convffn_bwd_b")(dup, dup, conv_w, conv_w)


def _position():
    return lax.axis_index("x"), lax.axis_index("y"), lax.axis_index("c")


def _flip(pos, k):
    x, y, c = pos
    return (1 - x if k & 4 else x, 1 - y if k & 2 else y, 1 - c if k & 1 else c)


def _index(pos):
    return 4 * pos[0] + 2 * pos[1] + pos[2]


def _all_gather_weights(shards):
    n = len(shards)

    def body(*refs):
        ins, outs = refs[:n], refs[n:2 * n]
        send_sems, recv_sems, local_sems = refs[2 * n:]
        me = _position()
        x, y, c = me
        sibling = (x, y, 1 - c)
        chips = [(1 - x, y), (x, 1 - y), (1 - x, 1 - y)]

        def copy(a, k, block, to, own=False):
            slot = outs[a].at[_index(block)]
            return pltpu.make_async_remote_copy(
                src_ref=ins[a] if own else slot, dst_ref=slot, send_sem=send_sems.at[7 * a + k],
                recv_sem=recv_sems.at[7 * a + k], device_id=to, device_id_type=MESH)

        mines = [pltpu.make_async_copy(ins[a], outs[a].at[_index(me)], local_sems.at[a]) for a in range(n)]
        started = []
        for a in range(n):
            mines[a].start()
            first = [copy(a, 0, me, sibling, own=True)]
            first += [copy(a, 1 + j, me, (*chip, c), own=True) for j, chip in enumerate(chips)]
            for cp in first:
                cp.start()
            started += first
        for j, chip in enumerate(chips):
            for a in range(n):
                copy(a, 1 + j, (*chip, c), me).wait_recv()
                passed = copy(a, 4 + j, (*chip, c), sibling)
                passed.start()
                started.append(passed)
        for a in range(n):
            copy(a, 0, sibling, me).wait_recv()
            for j, chip in enumerate(chips):
                copy(a, 4 + j, (*chip, 1 - c), me).wait_recv()
        for cp in started:
            cp.wait_send()
        for cp in mines:
            cp.wait()

    hbm = pl.BlockSpec(memory_space=pl.ANY)
    return pl.pallas_call(
        body, out_shape=[jax.ShapeDtypeStruct((N_DEV,) + s.shape, s.dtype) for s in shards],
        in_specs=[hbm] * n, out_specs=[hbm] * n,
        scratch_shapes=[pltpu.SemaphoreType.DMA((7 * n,)), pltpu.SemaphoreType.DMA((7 * n,)),
                        pltpu.SemaphoreType.DMA((n,))],
        name="all_gather_weights")(*shards)


_HBM = pl.BlockSpec(memory_space=pltpu.HBM)
_SEM = pl.BlockSpec(memory_space=pltpu.SEMAPHORE)
_ANY = pl.BlockSpec(memory_space=pl.ANY)
_EFFECT = pltpu.SideEffectType.DATAFLOW_SIDE_EFFECTING


def _in_hbm(a):
    return pltpu.with_memory_space_constraint(a, pltpu.HBM)


def _split_copies(name, srcs, lands, n_copies, plan, start_after, wait_after):
    ns, nl = len(srcs), len(lands)
    start_after = tuple(start_after)
    na = len(start_after)

    def start_body(*refs):
        src_refs, land_refs = refs[:ns], refs[ns:ns + nl]
        send_sems, recv_sems = refs[ns + nl + na], refs[ns + nl + na + 1]
        token = refs[-1]
        for k, (src, dst, _, peer) in enumerate(plan(src_refs, land_refs)):
            pltpu.make_async_remote_copy(src_ref=src, dst_ref=dst, send_sem=send_sems.at[k], recv_sem=recv_sems.at[k],
                                         device_id=peer, device_id_type=MESH).start()
        token[...] = jnp.zeros_like(token)

    thru = [pltpu.HBM(a.shape, a.dtype) for a in list(srcs) + list(lands)]
    res = pl.pallas_call(
        start_body, name=name + "_start",
        out_shape=[pltpu.SemaphoreType.DMA((n_copies,)), pltpu.SemaphoreType.DMA((n_copies,))] + thru
        + [jax.ShapeDtypeStruct((8, 128), F32)],
        in_specs=[_HBM] * (ns + nl) + [_ANY] * na,
        out_specs=[_SEM, _SEM] + [_HBM] * (ns + nl) + [pl.BlockSpec(memory_space=pltpu.VMEM)],
        input_output_aliases={i: 2 + i for i in range(ns + nl)},
        compiler_params=pltpu.CompilerParams(has_side_effects=_EFFECT),
    )(*[_in_hbm(a) for a in list(srcs) + list(lands)], *start_after)
    send_sems, recv_sems, token = res[0], res[1], res[-1]
    thru_vals = res[2:-1]

    def wait():
        def wait_body(*refs):
            src_refs, land_refs = refs[:ns], refs[ns:ns + nl]
            s_sems, r_sems = refs[ns + nl], refs[ns + nl + 1]
            for k, (src, _, dst, peer) in enumerate(plan(src_refs, land_refs)):
                cp = pltpu.make_async_remote_copy(src_ref=src, dst_ref=dst, send_sem=s_sems.at[k], recv_sem=r_sems.at[k],
                                                  device_id=peer, device_id_type=MESH)
                cp.wait_send()
                cp.wait_recv()

        out = pl.pallas_call(
            wait_body, name=name + "_wait", out_shape=thru,
            in_specs=[_HBM] * (ns + nl) + [_SEM, _SEM, _ANY], out_specs=[_HBM] * (ns + nl),
            input_output_aliases={i: i for i in range(ns + nl)},
            compiler_params=pltpu.CompilerParams(has_side_effects=_EFFECT),
        )(*thru_vals, send_sems, recv_sems, wait_after())
        return list(out[:ns]), list(out[ns:])

    return wait, token


def _gather_direct_plan(n):
    def plan(src_refs, land_refs):
        me = _position()
        peers = [_flip(me, k) for k in range(1, N_DEV)]
        return [(src_refs[a], land_refs[a].at[_index(me)], land_refs[a].at[_index(p)], p)
                for a in range(n) for p in peers]
    return plan


def _place_own(shards):
    me = _index(_position())
    return [lax.dynamic_update_slice(lax.empty((N_DEV,) + s.shape, s.dtype), s[None], (me, 0, 0)) for s in shards]


def _exchange_direct_plan(n):
    def plan(src_refs, land_refs):
        me = _position()
        peers = [_flip(me, k) for k in range(1, N_DEV)]
        return [(src_refs[a].at[_index(p)], land_refs[a].at[_index(me)], land_refs[a].at[_index(p)], p)
                for a in range(n) for p in peers]
    return plan


def _adamw(w, g, m, v):
    m = ADAM_B1 * m + (1.0 - ADAM_B1) * g
    v = ADAM_B2 * v + (1.0 - ADAM_B2) * (g * g)
    m_hat = m / (1.0 - ADAM_B1 ** ADAM_STEP)
    v_hat = v / (1.0 - ADAM_B2 ** ADAM_STEP)
    delta = -ADAM_LR * (m_hat / (jnp.sqrt(v_hat) + ADAM_EPS) + ADAM_WD * w)
    return delta, m, v


def _row_tile(rows, unit, cap=256):
    best = rows
    for t in range(unit, min(rows, cap) + 1, unit):
        if rows % t == 0:
            best = t
    return best


def _sum_partials(name, recv):
    n_src, R, C = recv.shape
    T = _row_tile(R, 16)

    def body(r_ref, g_out):
        g = r_ref[0].astype(F32)
        for s in range(1, n_src):
            g = g + r_ref[s].astype(F32)
        g_out[...] = g

    return pl.pallas_call(
        body, grid=(R // T,), in_specs=[pl.BlockSpec((n_src, T, C), lambda i: (0, i, 0))],
        out_specs=pl.BlockSpec((T, C), lambda i: (i, 0)), out_shape=jax.ShapeDtypeStruct((R, C), F32),
        compiler_params=_params("parallel"), name=name)(recv)


def _adamw_update(name, w, g, m, v):
    R, C = w.shape
    T = _row_tile(R, 8)

    def body(w_ref, g_ref, m_ref, v_ref, d_out, m_out, v_out):
        delta, mn, vn = _adamw(w_ref[...], g_ref[...], m_ref[...], v_ref[...])
        d_out[...] = delta
        m_out[...] = mn
        v_out[...] = vn

    blk = pl.BlockSpec((T, C), lambda i: (i, 0))
    return pl.pallas_call(
        body, grid=(R // T,), in_specs=[blk] * 4, out_specs=[blk] * 3,
        out_shape=[jax.ShapeDtypeStruct((R, C), F32)] * 3, compiler_params=_params("parallel"), name=name)(w, g, m, v)


def _gather_vmem(p_ref, gath, send_sems, recv_sems):
    me = _position()
    gath[_index(me)] = p_ref[...]
    copies = []
    for k in range(1, N_DEV):
        peer = _flip(me, k)
        copies.append(pltpu.make_async_remote_copy(
            src_ref=p_ref, dst_ref=gath.at[_index(me)], send_sem=send_sems.at[k - 1],
            recv_sem=recv_sems.at[k - 1], device_id=peer, device_id_type=MESH))
    for cp in copies:
        cp.start()
    for k in range(1, N_DEV):
        peer = _flip(me, k)
        pltpu.make_async_remote_copy(
            src_ref=p_ref, dst_ref=gath.at[_index(peer)], send_sem=send_sems.at[k - 1],
            recv_sem=recv_sems.at[k - 1], device_id=peer, device_id_type=MESH).wait_recv()
    for cp in copies:
        cp.wait_send()


def _all_gather_small(p):
    R, C = p.shape

    def body(p_ref, out_ref, send_sems, recv_sems):
        _gather_vmem(p_ref, out_ref, send_sems, recv_sems)

    vm = pl.BlockSpec(memory_space=pltpu.VMEM)
    return pl.pallas_call(
        body, in_specs=[vm], out_specs=vm, out_shape=jax.ShapeDtypeStruct((N_DEV, R, C), F32),
        scratch_shapes=[pltpu.SemaphoreType.DMA((7,)), pltpu.SemaphoreType.DMA((7,))],
        name="all_gather_small")(p)


def _small_allreduce_adamw(part, w, m, v):
    R, C = part.shape

    def body(p_ref, w_ref, m_ref, v_ref, g_out, d_out, m_out, v_out, gath, send_sems, recv_sems):
        _gather_vmem(p_ref, gath, send_sems, recv_sems)
        g = gath[0]
        for s in range(1, N_DEV):
            g = g + gath[s]
        delta, mn, vn = _adamw(w_ref[...], g, m_ref[...], v_ref[...])
        g_out[...] = g
        d_out[...] = delta
        m_out[...] = mn
        v_out[...] = vn

    vm = pl.BlockSpec(memory_space=pltpu.VMEM)
    return pl.pallas_call(
        body, in_specs=[vm] * 4, out_specs=[vm] * 4, out_shape=[jax.ShapeDtypeStruct((R, C), F32)] * 4,
        scratch_shapes=[pltpu.VMEM((N_DEV, R, C), F32), pltpu.SemaphoreType.DMA((7,)), pltpu.SemaphoreType.DMA((7,))],
        name="small_allreduce_adamw")(part, w, m, v)


_BIG = ("w_in", "w_up", "w_down", "w_out", "w_pool_out", "w_attn_out")
_TRANSPOSED = ("w_in", "w_up", "w_pool_out", "w_attn_out")
_SMALL = (("g_mix", 2048), ("b_gate", 4096), ("pool_scale", 1024), ("g_ffn", 2048), ("conv_b", 11264), ("g_final", 2048))
CONV_ROWS = 8


def _to_comm(name, shard):
    return shard.T if name in _TRANSPOSED else shard


def _pack_small(vals):
    flat = [vals[n].reshape(-1) for n, _ in _SMALL]
    used = sum(f.shape[0] for f in flat)
    flat.append(jnp.zeros((SMALL_ROWS * PACK_COLS - used,), F32))
    return jnp.concatenate(flat).reshape(SMALL_ROWS, PACK_COLS)


def _unpack_small(packed, like):
    flat = packed.reshape(-1)
    out, off = {}, 0
    for name, n in _SMALL:
        out[name] = flat[off:off + n].reshape(like[name].shape)
        off += n
    return out


def _local_step(x, target, W, small, first_token=0.0, late_weights=None, emit=None):
    S = x.shape[0]
    o_qkv, o_gate = POOL_WIDTH, POOL_WIDTH + QKV_WIDTH
    if emit is None:
        emit = lambda grads: None

    def zero(token):
        return 0.0 if token is None else token[0, 0]
    w_in_t = W["w_in"]

    h1 = _rms_fwd("rms1_fwd", x, small["g_mix"] + first_token)
    u = _mm_nt("proj_u", h1, w_in_t, tm=1024, tn=512, tk=2048, o_dtype=F32, n_rows=POOL_WIDTH, b_row_off=0)
    qkv = _mm_nt("proj_qkv", h1, w_in_t, tm=1024, tn=512, tk=2048, o_dtype=BF16, n_rows=QKV_WIDTH, b_row_off=o_qkv)
    gates_pre = _mm_nt("proj_gates", h1, w_in_t, tm=1024, tn=512, tk=2048, o_dtype=F32, n_rows=2 * D_MODEL,
                       b_row_off=o_gate)
    pool_y = _pool_fwd(u, W["w_pool_lin"], small["pool_scale"])
    outs, lses = zip(*[_attn_fwd(gi, qkv) for gi in range(3)])
    attn_y = _merge_fwd(outs, lses)
    if late_weights is not None:
        W = dict(W, **late_weights(0, attn_y))
    w_up_t = W["w_up"]
    mixed, y_pool, y_attn = _mix_fwd(pool_y, attn_y, W["w_pool_out"], W["w_attn_out"], gates_pre, small["b_gate"])
    x1 = _mm_nn("out_proj", mixed, W["w_out"], tm=1024, tn=1024, tk=2048, o_dtype=F32, res=x)

    h2 = _rms_fwd("rms2_fwd", x1, small["g_ffn"])
    nj = D_FF // 512
    up_pre = _mm("up_proj", h2, w_up_t, mode="nt", grid=(S // 1024, 2 * nj, 1),
                 a_spec=pl.BlockSpec((1024, D_MODEL), lambda i, j, k: (i, 0)),
                 b_spec=pl.BlockSpec((512, D_MODEL), lambda i, j, k: (j, 0)),
                 o_spec=pl.BlockSpec((None, 1024, 512), lambda i, j, k: (j // nj, i, j % nj)),
                 o_shape=(2, S, D_FF), o_dtype=BF16, acc_shape=(1024, 512))
    act = _convffn_fwd(up_pre, small["conv_w_full"], small["conv_b"])
    if late_weights is not None:
        W = dict(W, **late_weights(1, act))
    x2 = _mm_nn("down_proj", act, W["w_down"], tm=1024, tn=1024, tk=D_FF // 2, o_dtype=F32, res=x1)

    dx2, dx2_b, dg_final, sq_cols = _final_loss("final_loss", x2, small["g_final"], target)

    dact = _mm_nt("d_act", dx2_b, W["w_down"], tm=1024, tn=D_FF // 4, tk=2048, o_dtype=BF16)
    dw_down = _mm_tn("dw_down", act, dx2_b, tm=512, tn=2048, tk=TOKEN_TK)
    tok = emit(dict(w_down=dw_down))
    dup, dconv_w, dconv_b = _convffn_bwd_a(up_pre, dact, small["conv_w_full"], small["conv_b"] + zero(tok))
    dup_pre = _convffn_bwd_b(dup, small["conv_w_full"])
    tkh = D_FF // 2
    dh2 = _mm("d_h2", dup_pre, w_up_t, mode="nn", grid=(S // 1024, D_MODEL // 1024, 4),
              a_spec=pl.BlockSpec((None, 1024, tkh), lambda i, j, k: (k // 2, i, k % 2)),
              b_spec=pl.BlockSpec((tkh, 1024), lambda i, j, k: (k, j)),
              o_spec=pl.BlockSpec((1024, 1024), lambda i, j, k: (i, j)),
              o_shape=(S, D_MODEL), o_dtype=F32, acc_shape=(1024, 1024))
    dw_up = _mm("dw_up", dup_pre, h2, mode="tn", grid=(2 * nj, 1, S // TOKEN_TK),
                a_spec=pl.BlockSpec((None, TOKEN_TK, 512), lambda i, j, k: (i // nj, k, i % nj)),
                b_spec=pl.BlockSpec((TOKEN_TK, D_MODEL), lambda i, j, k: (k, 0)),
                o_spec=pl.BlockSpec((512, D_MODEL), lambda i, j, k: (i, 0)),
                o_shape=(2 * D_FF, D_MODEL), o_dtype=BF16, acc_shape=(512, D_MODEL))
    tok = emit(dict(w_up=dw_up, conv_w=dconv_w))
    dx1, dx1_b, dg_ffn = _rms_bwd("rms2_bwd", x1, small["g_ffn"] + zero(tok), dh2, dx2)

    dmixed = _mm_nt("d_mixed", dx1_b, W["w_out"], tm=1024, tn=1024, tk=2048, o_dtype=BF16)
    dw_out = _mm_tn("dw_out", mixed, dx1_b, tm=512, tn=2048, tk=TOKEN_TK)
    tok = emit(dict(w_out=dw_out))
    dy_pool, dy_attn, dgates, db_gate = _mix_bwd(dmixed, gates_pre, small["b_gate"] + zero(tok), y_pool, y_attn)
    dpool_y = _mm_nn("d_pool_y", dy_pool, W["w_pool_out"], tm=1024, tn=1024, tk=2048, o_dtype=F32)
    dw_pool_out = _mm_tn("dw_pool_out", dy_pool, pool_y, tm=1024, tn=1024, tk=TOKEN_TK)
    dattn_y = _mm_nn("d_attn_y", dy_attn, W["w_attn_out"], tm=1024, tn=512, tk=2048, o_dtype=F32)
    dw_attn_out = _mm_tn("dw_attn_out", dy_attn, attn_y, tm=1024, tn=512, tk=TOKEN_TK)
    e, dw_pool_lin, dpool_scale = _pool_bwd_a(u, dpool_y, W["w_pool_lin"], small["pool_scale"])
    du = _pool_bwd_b(e)
    dos, dps = _merge_bwd(dattn_y, outs, lses)
    dqkv = [_attn_bwd(gi, qkv, dos[gi], lses[gi], dps[gi]) for gi in range(3)]
    dproj = jnp.concatenate([du] + [dqkv[gi][sec] for sec in range(3) for gi in range(3)] + [dgates], axis=1)
    dw_in = _mm_tn("dw_in", dproj, h1, tm=512, tn=2048, tk=TOKEN_TK)
    tok = emit(dict(w_in=dw_in, w_pool_out=dw_pool_out, w_attn_out=dw_attn_out, w_pool_lin=dw_pool_lin))
    dh1 = _mm_nn("d_h1", dproj, w_in_t, tm=1024, tn=1024, tk=IN_WIDTH // 4, o_dtype=F32,
                 after=tok)
    grad_x, _, dg_mix = _rms_bwd("rms1_bwd", x, small["g_mix"] + zero(tok), dh1, dx1)

    grads = dict(w_in=dw_in, w_up=dw_up, w_down=dw_down, w_out=dw_out, w_pool_out=dw_pool_out,
                 w_attn_out=dw_attn_out, w_pool_lin=dw_pool_lin, conv_w=dconv_w)
    small_grads = dict(g_mix=dg_mix, b_gate=db_gate, pool_scale=dpool_scale, g_ffn=dg_ffn, conv_b=dconv_b,
                       g_final=dg_final)
    return sq_cols, grad_x, grads, small_grads


def kernel(x, g_mix, w_in, b_gate, w_pool_lin, pool_scale, w_pool_out, w_attn_out, w_out, g_ffn, w_up, conv_w, conv_b, w_down, g_final, loss_target, m_g_mix, m_w_in, m_b_gate, m_w_pool_lin, m_pool_scale, m_w_pool_out, m_w_attn_out, m_w_out, m_g_ffn, m_w_up, m_conv_w, m_conv_b, m_w_down, m_g_final, v_g_mix, v_w_in, v_b_gate, v_w_pool_lin, v_pool_scale, v_w_pool_out, v_w_attn_out, v_w_out, v_g_ffn, v_w_up, v_conv_w, v_conv_b, v_w_down, v_g_final):
    given = dict(g_mix=g_mix, w_in=w_in, b_gate=b_gate, w_pool_lin=w_pool_lin, pool_scale=pool_scale,
                 w_pool_out=w_pool_out, w_attn_out=w_attn_out, w_out=w_out, g_ffn=g_ffn, w_up=w_up, conv_w=conv_w,
                 conv_b=conv_b, w_down=w_down, g_final=g_final)
    mom_m = dict(g_mix=m_g_mix, w_in=m_w_in, b_gate=m_b_gate, w_pool_lin=m_w_pool_lin, pool_scale=m_pool_scale,
                 w_pool_out=m_w_pool_out, w_attn_out=m_w_attn_out, w_out=m_w_out, g_ffn=m_g_ffn, w_up=m_w_up,
                 conv_w=m_conv_w, conv_b=m_conv_b, w_down=m_w_down, g_final=m_g_final)
    mom_v = dict(g_mix=v_g_mix, w_in=v_w_in, b_gate=v_b_gate, w_pool_lin=v_w_pool_lin, pool_scale=v_pool_scale,
                 w_pool_out=v_w_pool_out, w_attn_out=v_w_attn_out, w_out=v_w_out, g_ffn=v_g_ffn, w_up=v_w_up,
                 conv_w=v_conv_w, conv_b=v_conv_b, w_down=v_w_down, g_final=v_g_final)
    def full(g):
        return g.reshape(N_DEV * g.shape[1], g.shape[2])

    shard = {n: _to_comm(n, given[n][0]).astype(BF16) for n in _BIG}
    W = dict(w_in=full(_all_gather_weights([shard["w_in"]])[0]))
    lin_all = _all_gather_small(w_pool_lin[0].reshape(4 * 32, POOL_GROUP_WIDTH))
    W["w_pool_lin"] = jnp.moveaxis(lin_all.reshape(N_DEV, 4, 32, POOL_GROUP_WIDTH), 0, 1).reshape(
        4, POOL_GROUP_WIDTH, POOL_GROUP_WIDTH).astype(BF16)
    conv_w_all = _all_gather_small(jnp.pad(conv_w[0], ((0, CONV_ROWS - 3), (0, 0))))
    conv_w_full = jnp.moveaxis(conv_w_all[:, :3, :], 0, 1).reshape(3, 2 * D_FF)

    stages = (("w_out", "w_pool_out", "w_attn_out", "w_up"), ("w_down",))
    stage_after = [None, None]
    stage_wait = []
    order_after = (W["w_in"], lin_all, conv_w_all)
    for k, names in enumerate(stages):
        srcs = [shard[n] for n in names]
        wait, token = _split_copies(f"gather_stage{k}", srcs, _place_own(srcs), 7 * len(names),
                                    _gather_direct_plan(len(names)), start_after=order_after,
                                    wait_after=lambda k=k: stage_after[k])
        stage_wait.append(wait)
        order_after = (token,)
    rest_token = token

    def late_weights(stage, after):
        stage_after[stage] = after
        return {n: full(l) for n, l in zip(stages[stage], stage_wait[stage]()[1])}

    small = dict(g_mix=g_mix, b_gate=b_gate, pool_scale=pool_scale, g_ffn=g_ffn, conv_b=conv_b,
                 g_final=g_final.reshape(1, D_MODEL), conv_w_full=conv_w_full)

    pending = []
    done = {}

    def emit(group):
        names = list(group)
        parts = []
        for n in names:
            g = group[n]
            if n == "w_pool_lin":
                g = jnp.moveaxis(g.reshape(4, N_DEV, 32, POOL_GROUP_WIDTH), 1, 0).reshape(
                    N_DEV, 4 * 32, POOL_GROUP_WIDTH).astype(BF16)
            elif n == "conv_w":
                g = jnp.pad(jnp.moveaxis(g.reshape(3, N_DEV, 2 * D_FF // N_DEV), 1, 0),
                            ((0, 0), (0, CONV_ROWS - 3), (0, 0))).astype(BF16)
            else:
                g = g.reshape(N_DEV, g.shape[0] // N_DEV, g.shape[1])
            parts.append(g)
        me = _index(_position())
        lands = [lax.dynamic_update_slice(lax.empty(p.shape, p.dtype), lax.dynamic_index_in_dim(p, me, 0),
                                          (me, 0, 0)) for p in parts]
        wait, token = _split_copies(
            "exchange_" + names[0], parts, lands, 7 * len(names), _exchange_direct_plan(len(names)),
            start_after=(), wait_after=lambda: done["grad_x"])
        pending.append((names, wait))
        return token

    sq_cols, grad_x, _, small_grads = _local_step(x[0], loss_target[0], W, small, first_token=rest_token[0, 0],
                                                  late_weights=late_weights, emit=emit)
    done["grad_x"] = grad_x
    loss = lax.psum(0.5 / D_MODEL * jnp.sum(sq_cols), AXES)

    res = {}
    for n, r in [(n, r) for names, wait in pending for n, r in zip(names, wait()[1])]:
        g = _sum_partials("sum_" + n, r)
        if n in _TRANSPOSED:
            g = g.T
        elif n == "conv_w":
            g = g[:3]
        shape = given[n].shape
        rows = math.prod(shape[:-1])
        w2, m2, v2 = (d[n].reshape(rows, shape[-1]) for d in (given, mom_m, mom_v))
        upd = _adamw_update("adamw_" + n, w2, g.reshape(rows, shape[-1]), m2, v2)
        res[n] = [a.reshape(shape) for a in (g,) + tuple(upd)]

    packed_small = [_pack_small(d) for d in (given, mom_m, mom_v)]
    res_small = [_unpack_small(p, given) for p in _small_allreduce_adamw(_pack_small(small_grads), *packed_small)]

    order = ["g_mix", "w_in", "b_gate", "w_pool_lin", "pool_scale", "w_pool_out", "w_attn_out", "w_out", "g_ffn",
             "w_up", "conv_w", "conv_b", "w_down", "g_final"]
    outs = [loss, grad_x[None]]
    for kind in range(4):
        for n in order:
            outs.append(res[n][kind] if n in res else res_small[kind][n])
    return tuple(outs)
```

```python
import math

import jax
import jax.numpy as jnp
from jax import lax
from jax.experimental import pallas as pl
from jax.experimental.pallas import tpu as pltpu

F32 = jnp.float32
BF16 = jnp.bfloat16
MESH = pl.DeviceIdType.MESH
AXES = ("x", "y", "c")
N_DEV = 8

D_MODEL = 2048
POOL_WINDOWS = (2, 4, 8, 16)
POOL_GROUP_WIDTH = 256
POOL_WIDTH = 1024
ATTN_GROUPS = ((128, 1), (512, 4), (2048, 16))
SPAN = 128
HEADS_PER_GROUP = 4
N_ATTN_HEADS = 12
HEAD_DIM = 128
ATTN_WIDTH = 1536
QKV_WIDTH = 3 * ATTN_WIDTH
ATTN_OUT_WIDTH = 512
IN_WIDTH = 9728
D_FF = 5632
RMS_EPS = 1e-6
ADAM_LR, ADAM_B1, ADAM_B2, ADAM_EPS, ADAM_WD, ADAM_STEP = 0.001, 0.9, 0.999, 1e-08, 0.01, 10

VMEM_LIMIT_BYTES = 48 * 1024 * 1024
HALO = 16
TOKEN_TK = 2048
PACK_COLS = 1024
SMALL_ROWS = 24
NEG_BIG = -1e30

_DN = {"nn": (((1,), (0,)), ((), ())), "nt": (((1,), (1,)), ((), ())), "tn": (((0,), (0,)), ((), ()))}


def _params(*sem):
    return pltpu.CompilerParams(dimension_semantics=sem, vmem_limit_bytes=VMEM_LIMIT_BYTES)


def _dot(a, b, mode):
    return lax.dot_general(a.astype(BF16), b.astype(BF16), _DN[mode], preferred_element_type=F32)


def _mm(name, a, b, *, mode, grid, a_spec, b_spec, o_spec, o_shape, o_dtype, acc_shape, res=None, res_spec=None,
        after=None):
    nk = grid[2]
    n_in = 2 + (res is not None) + (after is not None)

    def body(*refs):
        a_ref, b_ref = refs[:2]
        r_ref = refs[2] if res is not None else None
        o_ref = refs[n_in]
        scr = refs[n_in + 1:]
        prod = _dot(a_ref[...], b_ref[...], mode)

        def finish(val):
            if r_ref is not None:
                val = val + r_ref[...]
            o_ref[...] = val.astype(o_ref.dtype)

        if nk == 1:
            finish(prod)
        else:
            acc = scr[0]
            k = pl.program_id(2)

            @pl.when(k == 0)
            def _():
                acc[...] = prod

            @pl.when(k > 0)
            def _():
                acc[...] += prod

            @pl.when(k == nk - 1)
            def _():
                finish(acc[...])

    in_specs = [a_spec, b_spec]
    args = [a, b]
    if res is not None:
        in_specs.append(res_spec)
        args.append(res)
    if after is not None:
        in_specs.append(pl.BlockSpec((8, 128), lambda i, j, k: (0, 0)))
        args.append(after)
    return pl.pallas_call(
        body, grid=grid, in_specs=in_specs, out_specs=o_spec,
        out_shape=jax.ShapeDtypeStruct(o_shape, o_dtype),
        scratch_shapes=[] if nk == 1 else [pltpu.VMEM(acc_shape, F32)],
        compiler_params=_params("parallel", "parallel", "arbitrary"), name=name)(*args)


def _mm_nn(name, a, b, *, tm, tn, tk, o_dtype, n_cols=None, b_col_off=0, res=None, after=None):
    M, K = a.shape
    N = b.shape[1] if n_cols is None else n_cols
    off = b_col_off // tn
    return _mm(name, a, b, mode="nn", grid=(M // tm, N // tn, K // tk),
               a_spec=pl.BlockSpec((tm, tk), lambda i, j, k: (i, k)),
               b_spec=pl.BlockSpec((tk, tn), lambda i, j, k: (k, j + off)),
               o_spec=pl.BlockSpec((tm, tn), lambda i, j, k: (i, j)),
               o_shape=(M, N), o_dtype=o_dtype, acc_shape=(tm, tn), res=res,
               res_spec=pl.BlockSpec((tm, tn), lambda i, j, k: (i, j)), after=after)


def _mm_nt(name, a, b, *, tm, tn, tk, o_dtype, n_rows=None, b_row_off=0):
    M, K = a.shape
    N = b.shape[0] if n_rows is None else n_rows
    off = b_row_off // tn
    return _mm(name, a, b, mode="nt", grid=(M // tm, N // tn, K // tk),
               a_spec=pl.BlockSpec((tm, tk), lambda i, j, k: (i, k)),
               b_spec=pl.BlockSpec((tn, tk), lambda i, j, k: (j + off, k)),
               o_spec=pl.BlockSpec((tm, tn), lambda i, j, k: (i, j)),
               o_shape=(M, N), o_dtype=o_dtype, acc_shape=(tm, tn))


def _mm_tn(name, a, b, *, tm, tn, tk):
    K, M = a.shape
    N = b.shape[1]
    return _mm(name, a, b, mode="tn", grid=(M // tm, N // tn, K // tk),
               a_spec=pl.BlockSpec((tk, tm), lambda i, j, k: (k, i)),
               b_spec=pl.BlockSpec((tk, tn), lambda i, j, k: (k, j)),
               o_spec=pl.BlockSpec((tm, tn), lambda i, j, k: (i, j)),
               o_shape=(M, N), o_dtype=BF16, acc_shape=(tm, tn))


def _rms_fwd(name, x, g):
    S, D = x.shape
    T = 512

    def body(x_ref, g_ref, h_ref):
        xv = x_ref[...]
        r = lax.rsqrt(jnp.mean(xv * xv, axis=-1, keepdims=True) + RMS_EPS)
        h_ref[...] = (xv * r * g_ref[...]).astype(BF16)

    return pl.pallas_call(
        body, grid=(S // T,),
        in_specs=[pl.BlockSpec((T, D), lambda i: (i, 0)), pl.BlockSpec((1, D), lambda i: (0, 0))],
        out_specs=pl.BlockSpec((T, D), lambda i: (i, 0)),
        out_shape=jax.ShapeDtypeStruct((S, D), BF16), compiler_params=_params("parallel"), name=name)(x, g)


def _rms_bwd(name, x, g, dh, dres):
    S, D = x.shape
    T = 256

    def body(x_ref, g_ref, dh_ref, dres_ref, dx_ref, dxb_ref, dg_ref):
        i = pl.program_id(0)
        xv = x_ref[...]
        r = lax.rsqrt(jnp.mean(xv * xv, axis=-1, keepdims=True) + RMS_EPS)
        xhat = xv * r
        dhv = dh_ref[...].astype(F32)
        gdh = dhv * g_ref[...]
        c = jnp.mean(xhat * gdh, axis=-1, keepdims=True)
        dx = dres_ref[...] + r * (gdh - xhat * c)
        dx_ref[...] = dx
        dxb_ref[...] = dx.astype(BF16)
        part = jnp.sum(dhv * xhat, axis=0, keepdims=True)

        @pl.when(i == 0)
        def _():
            dg_ref[...] = part

        @pl.when(i > 0)
        def _():
            dg_ref[...] += part

    row = pl.BlockSpec((T, D), lambda i: (i, 0))
    vec = pl.BlockSpec((1, D), lambda i: (0, 0))
    return pl.pallas_call(
        body, grid=(S // T,), in_specs=[row, vec, row, row], out_specs=[row, row, vec],
        out_shape=[jax.ShapeDtypeStruct((S, D), F32), jax.ShapeDtypeStruct((S, D), BF16),
                   jax.ShapeDtypeStruct((1, D), F32)],
        compiler_params=_params("arbitrary"), name=name)(x, g, dh, dres)


def _final_loss(name, x, g, target):
    S, D = x.shape
    T = 256

    def body(x_ref, g_ref, t_ref, dx_ref, dxb_ref, dg_ref, sq_ref):
        i = pl.program_id(0)
        xv = x_ref[...]
        r = lax.rsqrt(jnp.mean(xv * xv, axis=-1, keepdims=True) + RMS_EPS)
        xhat = xv * r
        err = xhat * g_ref[...] - t_ref[...]
        dy = err * (1.0 / D)
        gdy = dy * g_ref[...]
        c = jnp.mean(xhat * gdy, axis=-1, keepdims=True)
        dx = r * (gdy - xhat * c)
        dx_ref[...] = dx
        dxb_ref[...] = dx.astype(BF16)
        dg_part = jnp.sum(dy * xhat, axis=0, keepdims=True)
        sq_part = jnp.sum(err * err, axis=0, keepdims=True)

        @pl.when(i == 0)
        def _():
            dg_ref[...] = dg_part
            sq_ref[...] = sq_part

        @pl.when(i > 0)
        def _():
            dg_ref[...] += dg_part
            sq_ref[...] += sq_part

    row = pl.BlockSpec((T, D), lambda i: (i, 0))
    vec = pl.BlockSpec((1, D), lambda i: (0, 0))
    return pl.pallas_call(
        body, grid=(S // T,), in_specs=[row, vec, row], out_specs=[row, row, vec, vec],
        out_shape=[jax.ShapeDtypeStruct((S, D), F32), jax.ShapeDtypeStruct((S, D), BF16),
                   jax.ShapeDtypeStruct((1, D), F32), jax.ShapeDtypeStruct((1, D), F32)],
        compiler_params=_params("arbitrary"), name=name)(x, g, target)


def _prev_halo_spec(T, C, col_map):
    per = T // HALO
    return pl.BlockSpec((HALO, C), lambda *g: (jnp.maximum(g[0] * per - 1, 0), col_map(*g)))


def _pool_window_sums(ext, T, i):
    t = i * T + lax.broadcasted_iota(jnp.int32, (T, 1), 0)
    outs = []
    cnts = []
    for gi, w in enumerate(POOL_WINDOWS):
        s = ext[:, gi * POOL_GROUP_WIDTH:(gi + 1) * POOL_GROUP_WIDTH]
        sh = 1
        while sh < w:
            s = s + pltpu.roll(s, sh, 0)
            sh *= 2
        cnt = jnp.minimum(t + 1, w).astype(F32)
        outs.append(s[HALO:] / cnt - ext[HALO:, gi * POOL_GROUP_WIDTH:(gi + 1) * POOL_GROUP_WIDTH])
        cnts.append(cnt)
    return outs, cnts


def _load_ext(ext_ref, halo_ref, blk_ref, first):
    ext_ref[0:HALO, :] = jnp.where(first, 0.0, halo_ref[...].astype(F32))
    ext_ref[HALO:, :] = blk_ref[...].astype(F32)


def _pool_fwd(u, w_lin, scale):
    S = u.shape[0]
    T = 512
    W = POOL_WIDTH

    def body(u_ref, halo_ref, wl_ref, sc_ref, y_ref, ext_ref):
        i = pl.program_id(0)
        _load_ext(ext_ref, halo_ref, u_ref, i == 0)
        pooled, _ = _pool_window_sums(ext_ref[...], T, i)
        for gi in range(4):
            z = _dot(pooled[gi], wl_ref[gi], "nn")
            cs = slice(gi * POOL_GROUP_WIDTH, (gi + 1) * POOL_GROUP_WIDTH)
            y_ref[:, cs] = (z * sc_ref[:, cs]).astype(BF16)

    return pl.pallas_call(
        body, grid=(S // T,),
        in_specs=[pl.BlockSpec((T, W), lambda i: (i, 0)), _prev_halo_spec(T, W, lambda i: 0),
                  pl.BlockSpec((4, 256, 256), lambda i: (0, 0, 0)), pl.BlockSpec((1, W), lambda i: (0, 0))],
        out_specs=pl.BlockSpec((T, W), lambda i: (i, 0)),
        out_shape=jax.ShapeDtypeStruct((S, W), BF16),
        scratch_shapes=[pltpu.VMEM((HALO + T, W), F32)],
        compiler_params=_params("parallel"), name="pool_fwd")(u, u, w_lin, scale)


def _pool_bwd_a(u, dy, w_lin, scale):
    S = u.shape[0]
    T = 512
    W = POOL_WIDTH

    def body(u_ref, halo_ref, dy_ref, wl_ref, sc_ref, e_ref, dwl_ref, dsc_ref, ext_ref):
        i = pl.program_id(0)
        _load_ext(ext_ref, halo_ref, u_ref, i == 0)
        pooled, cnts = _pool_window_sums(ext_ref[...], T, i)
        for gi in range(4):
            cs = slice(gi * POOL_GROUP_WIDTH, (gi + 1) * POOL_GROUP_WIDTH)
            z = _dot(pooled[gi], wl_ref[gi], "nn")
            dyg = dy_ref[:, cs]
            dsc = jnp.sum(dyg * z, axis=0, keepdims=True)
            dz = dyg * sc_ref[:, cs]
            dwl = _dot(pooled[gi], dz, "tn")
            dpooled = _dot(dz, wl_ref[gi], "nt")
            e_ref[:, cs] = dpooled / cnts[gi]

            @pl.when(i == 0)
            def _():
                dwl_ref[gi] = dwl
                dsc_ref[:, cs] = dsc

            @pl.when(i > 0)
            def _():
                dwl_ref[gi] += dwl
                dsc_ref[:, cs] += dsc

    return pl.pallas_call(
        body, grid=(S // T,),
        in_specs=[pl.BlockSpec((T, W), lambda i: (i, 0)), _prev_halo_spec(T, W, lambda i: 0),
                  pl.BlockSpec((T, W), lambda i: (i, 0)),
                  pl.BlockSpec((4, 256, 256), lambda i: (0, 0, 0)), pl.BlockSpec((1, W), lambda i: (0, 0))],
        out_specs=[pl.BlockSpec((T, W), lambda i: (i, 0)), pl.BlockSpec((4, 256, 256), lambda i: (0, 0, 0)),
                   pl.BlockSpec((1, W), lambda i: (0, 0))],
        out_shape=[jax.ShapeDtypeStruct((S, W), F32), jax.ShapeDtypeStruct((4, 256, 256), F32),
                   jax.ShapeDtypeStruct((1, W), F32)],
        scratch_shapes=[pltpu.VMEM((HALO + T, W), F32)],
        compiler_params=_params("arbitrary"), name="pool_bwd_a")(u, u, dy, w_lin, scale)


def _pool_bwd_b(e):
    S = e.shape[0]
    T = 512
    W = POOL_WIDTH
    per = T // HALO
    n_blk = S // T

    def body(e_ref, halo_ref, du_ref, ext_ref):
        i = pl.program_id(0)
        ext_ref[0:T, :] = e_ref[...]
        ext_ref[T:, :] = jnp.where(i == n_blk - 1, 0.0, halo_ref[...])
        ext = ext_ref[...]
        n = T + HALO
        t = i * T + lax.broadcasted_iota(jnp.int32, (T, 1), 0)
        for gi, w in enumerate(POOL_WINDOWS):
            cs = slice(gi * POOL_GROUP_WIDTH, (gi + 1) * POOL_GROUP_WIDTH)
            s = ext[:, cs]
            sh = 1
            while sh < w:
                s = s + pltpu.roll(s, n - sh, 0)
                sh *= 2
            cnt = jnp.minimum(t + 1, w).astype(F32)
            du_ref[:, cs] = (s[0:T] - ext[0:T, cs] * cnt).astype(BF16)

    return pl.pallas_call(
        body, grid=(n_blk,),
        in_specs=[pl.BlockSpec((T, W), lambda i: (i, 0)),
                  pl.BlockSpec((HALO, W), lambda i: (jnp.minimum((i + 1) * per, S // HALO - 1), 0))],
        out_specs=pl.BlockSpec((T, W), lambda i: (i, 0)),
        out_shape=jax.ShapeDtypeStruct((S, W), BF16),
        scratch_shapes=[pltpu.VMEM((T + HALO, W), F32)],
        compiler_params=_params("parallel"), name="pool_bwd_b")(e, e)


def _slope(head):
    return 2.0 ** (-8.0 * (head + 1) / N_ATTN_HEADS)


def _attn_masks(b):
    row = lax.broadcasted_iota(jnp.int32, (SPAN, SPAN), 0)
    col = lax.broadcasted_iota(jnp.int32, (SPAN, SPAN), 1)
    dist_cur = (row - col).astype(F32)
    valid_cur = col <= row
    valid_prev = jnp.logical_and(col >= row, b > 0)
    return dist_cur, valid_cur, valid_prev


ATTN_BATCH = 4


def _attn_tiling(gi, S):
    d = ATTN_GROUPS[gi][1]
    hp = HEADS_PER_GROUP if d < 16 else 1
    return d, SPAN * d, S // (SPAN * d), hp, hp * HEAD_DIM, ATTN_BATCH // hp


_BATCHED_DN = {"nt": (((2,), (2,)), ((0,), (0,))), "nn": (((2,), (1,)), ((0,), (0,))), "tn": (((1,), (1,)), ((0,), (0,)))}


def _bdot(a, b, mode):
    return lax.dot_general(a.astype(BF16), b.astype(BF16), _BATCHED_DN[mode], preferred_element_type=F32)


def _batch_rows(it, d, rb):
    return [pl.ds(it * rb + j, SPAN, stride=d) for j in range(rb)]


def _take(scr, hp, rows):
    return jnp.stack([scr[hi, rw, :] for hi in range(hp) for rw in rows])


def _put(scr, hp, rows, val):
    for n, (hi, rw) in enumerate((hi, rw) for hi in range(hp) for rw in rows):
        scr[hi, rw, :] = val[n]


def _batch_bias(gi, d, hp, rb, dist):
    return jnp.stack([(-(_head_slope(gi, hp, hi) * d)) * dist for hi in range(hp) for _ in range(rb)])


def _head_slope(gi, hp, hi):
    if hp == HEADS_PER_GROUP:
        return _slope(gi * HEADS_PER_GROUP + hi)
    first = pl.program_id(1) * hp
    sl = jnp.float32(_slope(gi * HEADS_PER_GROUP + hi))
    for h0 in range(hp, HEADS_PER_GROUP, hp):
        sl = jnp.where(first == h0, jnp.float32(_slope(gi * HEADS_PER_GROUP + h0 + hi)), sl)
    return sl


def _split_heads(src_ref, dst_ref, hp):
    for hi in range(hp):
        dst_ref[hi] = src_ref[:, hi * HEAD_DIM:(hi + 1) * HEAD_DIM].astype(F32)


def _attn_fwd(gi, qkv):
    S = qkv.shape[0]
    d, C, nb, hp, W, rb = _attn_tiling(gi, S)
    scale = HEAD_DIM ** -0.5

    def body(q_ref, kc_ref, kp_ref, vc_ref, vp_ref, o_ref, lse_ref, qs, kcs, kps, vcs, vps, os, ls):
        b = pl.program_id(0)
        for src, dst in ((q_ref, qs), (kc_ref, kcs), (kp_ref, kps), (vc_ref, vcs), (vp_ref, vps)):
            _split_heads(src, dst, hp)
        dist_cur, valid_cur, valid_prev = _attn_masks(b)
        bias_cur = _batch_bias(gi, d, hp, rb, dist_cur)
        bias_far = _batch_bias(gi, d, hp, rb, dist_cur + SPAN)

        def sub_sequences(it, carry):
            rows = _batch_rows(it, d, rb)
            q = _take(qs, hp, rows)
            s_cur = jnp.where(valid_cur, _bdot(q, _take(kcs, hp, rows), "nt") * scale + bias_cur, NEG_BIG)
            s_prev = jnp.where(valid_prev, _bdot(q, _take(kps, hp, rows), "nt") * scale + bias_far, NEG_BIG)
            m = jnp.maximum(jnp.max(s_cur, axis=-1, keepdims=True), jnp.max(s_prev, axis=-1, keepdims=True))
            p_cur = jnp.exp(s_cur - m)
            p_prev = jnp.exp(s_prev - m)
            l = jnp.sum(p_cur, axis=-1, keepdims=True) + jnp.sum(p_prev, axis=-1, keepdims=True)
            o = (_bdot(p_cur, _take(vcs, hp, rows), "nn") + _bdot(p_prev, _take(vps, hp, rows), "nn")) / l
            _put(os, hp, rows, o)
            _put(ls, hp, rows, jnp.broadcast_to(m + jnp.log(l), o.shape))
            return carry

        lax.fori_loop(0, d // rb, sub_sequences, 0)
        for hi in range(hp):
            hs = slice(hi * HEAD_DIM, (hi + 1) * HEAD_DIM)
            o_ref[:, hs] = os[hi]
            lse_ref[:, hs] = ls[hi]

    per_sec = ATTN_WIDTH // W
    per_grp = ATTN_OUT_WIDTH // W

    def spec(sec, shift):
        return pl.BlockSpec((C, W), lambda b, h: (jnp.maximum(b + shift, 0), sec * per_sec + gi * per_grp + h))

    out = pl.BlockSpec((C, W), lambda b, h: (b, h))
    scr = pltpu.VMEM((hp, C, HEAD_DIM), F32)
    return pl.pallas_call(
        body, grid=(nb, HEADS_PER_GROUP // hp),
        in_specs=[spec(0, 0), spec(1, 0), spec(1, -1), spec(2, 0), spec(2, -1)], out_specs=[out, out],
        out_shape=[jax.ShapeDtypeStruct((S, ATTN_OUT_WIDTH), F32)] * 2, scratch_shapes=[scr] * 7,
        compiler_params=_params("parallel", "parallel"), name=f"attn_fwd_g{gi}")(qkv, qkv, qkv, qkv, qkv)


def _attn_bwd(gi, qkv, do, lse, dp):
    S = qkv.shape[0]
    d, C, nb, hp, W, rb = _attn_tiling(gi, S)
    scale = HEAD_DIM ** -0.5
    n_in = 12

    def body(*refs):
        ins, (dq_ref, dk_ref, dv_ref) = refs[:n_in], refs[n_in:n_in + 3]
        scr = refs[n_in + 3:]
        for src, dst in zip(ins, scr[:n_in]):
            _split_heads(src, dst, hp)
        qs, kcs, vcs, kps, vps, qns, dos, dons, lses, lsens, dps, dpns = scr[:n_in]
        dqs, dks, dvs = scr[n_in:]
        b = pl.program_id(0)
        dist_cur, valid_cur, valid_prev = _attn_masks(b)
        row = lax.broadcasted_iota(jnp.int32, (SPAN, SPAN), 0)
        col = lax.broadcasted_iota(jnp.int32, (SPAN, SPAN), 1)
        valid_next = jnp.logical_and(col >= row, b < nb - 1)

        bias_cur = _batch_bias(gi, d, hp, rb, dist_cur)
        bias_far = _batch_bias(gi, d, hp, rb, dist_cur + SPAN)

        def sub_sequences(it, carry):
            rows = _batch_rows(it, d, rb)
            q, kc, vc, kp, vp, qn, dov, don = (_take(s, hp, rows) for s in (qs, kcs, vcs, kps, vps, qns, dos, dons))
            lse_b, lse_n, dp_b, dp_n = (_take(s, hp, rows) for s in (lses, lsens, dps, dpns))
            p_cur = jnp.where(valid_cur, jnp.exp(_bdot(q, kc, "nt") * scale + bias_cur - lse_b), 0.0)
            p_prev = jnp.where(valid_prev, jnp.exp(_bdot(q, kp, "nt") * scale + bias_far - lse_b), 0.0)
            ds_cur = p_cur * (_bdot(dov, vc, "nt") - dp_b)
            ds_prev = p_prev * (_bdot(dov, vp, "nt") - dp_b)
            _put(dqs, hp, rows, (_bdot(ds_cur, kc, "nn") + _bdot(ds_prev, kp, "nn")) * scale)
            p_next = jnp.where(valid_next, jnp.exp(_bdot(qn, kc, "nt") * scale + bias_far - lse_n), 0.0)
            ds_next = p_next * (_bdot(don, vc, "nt") - dp_n)
            _put(dvs, hp, rows, _bdot(p_cur, dov, "tn") + _bdot(p_next, don, "tn"))
            _put(dks, hp, rows, (_bdot(ds_cur, q, "tn") + _bdot(ds_next, qn, "tn")) * scale)
            return carry

        lax.fori_loop(0, d // rb, sub_sequences, 0)
        for hi in range(hp):
            hs = slice(hi * HEAD_DIM, (hi + 1) * HEAD_DIM)
            dq_ref[:, hs] = dqs[hi].astype(BF16)
            dk_ref[:, hs] = dks[hi].astype(BF16)
            dv_ref[:, hs] = dvs[hi].astype(BF16)

    per_sec = ATTN_WIDTH // W
    per_grp = ATTN_OUT_WIDTH // W

    def qkv_spec(sec, shift):
        return pl.BlockSpec((C, W), lambda b, h: (jnp.clip(b + shift, 0, nb - 1), sec * per_sec + gi * per_grp + h))

    def act_spec(shift):
        return pl.BlockSpec((C, W), lambda b, h: (jnp.clip(b + shift, 0, nb - 1), h))

    out = act_spec(0)
    scr = pltpu.VMEM((hp, C, HEAD_DIM), F32)
    return pl.pallas_call(
        body, grid=(nb, HEADS_PER_GROUP // hp),
        in_specs=[qkv_spec(0, 0), qkv_spec(1, 0), qkv_spec(2, 0), qkv_spec(1, -1), qkv_spec(2, -1), qkv_spec(0, 1),
                  act_spec(0), act_spec(1), act_spec(0), act_spec(1), act_spec(0), act_spec(1)],
        out_specs=[out, out, out], out_shape=[jax.ShapeDtypeStruct((S, ATTN_OUT_WIDTH), BF16)] * 3,
        scratch_shapes=[scr] * (n_in + 3),
        compiler_params=_params("parallel", "parallel"), name=f"attn_bwd_g{gi}")(
            qkv, qkv, qkv, qkv, qkv, qkv, do, do, lse, lse, dp, dp)


def _group_weights(l_refs):
    l0, l1, l2 = (r[...] for r in l_refs)
    m = jnp.maximum(jnp.maximum(l0, l1), l2)
    e = [jnp.exp(l0 - m), jnp.exp(l1 - m), jnp.exp(l2 - m)]
    tot = e[0] + e[1] + e[2]
    return [ei / tot for ei in e]


def _merge_fwd(outs, lses):
    S, W = outs[0].shape
    T = 512

    def body(o0, o1, o2, l0, l1, l2, y_ref):
        w = _group_weights((l0, l1, l2))
        y_ref[...] = (w[0] * o0[...] + w[1] * o1[...] + w[2] * o2[...]).astype(BF16)

    blk = pl.BlockSpec((T, W), lambda i: (i, 0))
    return pl.pallas_call(
        body, grid=(S // T,), in_specs=[blk] * 6, out_specs=blk, out_shape=jax.ShapeDtypeStruct((S, W), BF16),
        compiler_params=_params("parallel"), name="merge_fwd")(*outs, *lses)


def _merge_bwd(dy, outs, lses):
    S, W = outs[0].shape
    T = 512

    def body(dy_ref, o0, o1, o2, l0, l1, l2, do0, do1, do2, dp0, dp1, dp2):
        w = _group_weights((l0, l1, l2))
        dyv = dy_ref[...]
        y = w[0] * o0[...] + w[1] * o1[...] + w[2] * o2[...]
        prod = dyv * y
        dots = []
        for hi in range(HEADS_PER_GROUP):
            hs = slice(hi * HEAD_DIM, (hi + 1) * HEAD_DIM)
            dots.append(jnp.broadcast_to(jnp.sum(prod[:, hs], axis=-1, keepdims=True), (T, HEAD_DIM)))
        dot_b = jnp.concatenate(dots, axis=1)
        for wg, do_ref, dp_ref in zip(w, (do0, do1, do2), (dp0, dp1, dp2)):
            do_ref[...] = (wg * dyv).astype(BF16)
            dp_ref[...] = wg * dot_b

    blk = pl.BlockSpec((T, W), lambda i: (i, 0))
    res = pl.pallas_call(
        body, grid=(S // T,), in_specs=[blk] * 7, out_specs=[blk] * 6,
        out_shape=[jax.ShapeDtypeStruct((S, W), BF16)] * 3 + [jax.ShapeDtypeStruct((S, W), F32)] * 3,
        compiler_params=_params("parallel"), name="merge_bwd")(dy, *outs, *lses)
    return res[:3], res[3:]


def _mix_fwd(pool_y, attn_y, w_pool_out_t, w_attn_out_t, gates_pre, b_gate):
    S = pool_y.shape[0]
    D = D_MODEL
    tm, tn = 1024, 512
    nj = D // tn

    def body(p_ref, a_ref, wp_ref, wa_ref, g0_ref, g1_ref, b0_ref, b1_ref, mixed_ref, yp_ref, ya_ref):
        yp = _dot(p_ref[...], wp_ref[...], "nt")
        ya = _dot(a_ref[...], wa_ref[...], "nt")
        g0 = jax.nn.sigmoid(g0_ref[...] + b0_ref[...])
        g1 = jax.nn.sigmoid(g1_ref[...] + b1_ref[...])
        mixed_ref[...] = (g0 * yp + g1 * ya).astype(BF16)
        yp_ref[...] = yp.astype(BF16)
        ya_ref[...] = ya.astype(BF16)

    out = pl.BlockSpec((tm, tn), lambda i, j: (i, j))
    return pl.pallas_call(
        body, grid=(S // tm, nj),
        in_specs=[pl.BlockSpec((tm, POOL_WIDTH), lambda i, j: (i, 0)),
                  pl.BlockSpec((tm, ATTN_OUT_WIDTH), lambda i, j: (i, 0)),
                  pl.BlockSpec((tn, POOL_WIDTH), lambda i, j: (j, 0)),
                  pl.BlockSpec((tn, ATTN_OUT_WIDTH), lambda i, j: (j, 0)),
                  pl.BlockSpec((tm, tn), lambda i, j: (i, j)), pl.BlockSpec((tm, tn), lambda i, j: (i, j + nj)),
                  pl.BlockSpec((1, tn), lambda i, j: (0, j)), pl.BlockSpec((1, tn), lambda i, j: (0, j + nj))],
        out_specs=[out, out, out], out_shape=[jax.ShapeDtypeStruct((S, D), BF16)] * 3,
        compiler_params=_params("parallel", "parallel"), name="mix_fwd")(
            pool_y, attn_y, w_pool_out_t, w_attn_out_t, gates_pre, gates_pre, b_gate, b_gate)


def _mix_bwd(dmixed, gates_pre, b_gate, y_pool, y_attn):
    S = dmixed.shape[0]
    D = D_MODEL
    T = 256

    def body(dm_ref, gp_ref, b_ref, yp_ref, ya_ref, dyp_ref, dya_ref, dg_ref, db_ref):
        i = pl.program_id(0)
        dm = dm_ref[...].astype(F32)
        parts = []
        for br, (y_ref, dy_ref) in enumerate(((yp_ref, dyp_ref), (ya_ref, dya_ref))):
            cs = slice(br * D, (br + 1) * D)
            g = jax.nn.sigmoid(gp_ref[:, cs] + b_ref[:, cs])
            dy_ref[...] = (dm * g).astype(BF16)
            dpre = dm * y_ref[...].astype(F32) * g * (1.0 - g)
            dg_ref[:, cs] = dpre.astype(BF16)
            parts.append(jnp.sum(dpre, axis=0, keepdims=True))

        @pl.when(i == 0)
        def _():
            db_ref[:, 0:D] = parts[0]
            db_ref[:, D:2 * D] = parts[1]

        @pl.when(i > 0)
        def _():
            db_ref[:, 0:D] += parts[0]
            db_ref[:, D:2 * D] += parts[1]

    row = pl.BlockSpec((T, D), lambda i: (i, 0))
    row2 = pl.BlockSpec((T, 2 * D), lambda i: (i, 0))
    vec2 = pl.BlockSpec((1, 2 * D), lambda i: (0, 0))
    return pl.pallas_call(
        body, grid=(S // T,), in_specs=[row, row2, vec2, row, row], out_specs=[row, row, row2, vec2],
        out_shape=[jax.ShapeDtypeStruct((S, D), BF16), jax.ShapeDtypeStruct((S, D), BF16),
                   jax.ShapeDtypeStruct((S, 2 * D), BF16), jax.ShapeDtypeStruct((1, 2 * D), F32)],
        compiler_params=_params("arbitrary"), name="mix_bwd")(dmixed, gates_pre, b_gate, y_pool, y_attn)


_CONV_T, _CONV_C = 512, 512


def _conv_taps(ext_ref, h, w_ref, b_ref):
    ext = ext_ref[h]
    x2 = pltpu.roll(ext, 2, 0)[HALO:]
    x1 = pltpu.roll(ext, 1, 0)[HALO:]
    x0 = ext[HALO:]
    y = b_ref[...] + w_ref[0:1, :] * x2
    y = y + w_ref[1:2, :] * x1
    y = y + w_ref[2:3, :] * x0
    return y, (x2, x1, x0)


def _gelu_parts(a):
    cdf = 0.5 * (1.0 + lax.erf(a * (1.0 / math.sqrt(2.0))))
    return cdf, a * cdf


def _conv_specs(S):
    T, C = _CONV_T, _CONV_C
    nj = D_FF // C
    blk = pl.BlockSpec((2, T, C), lambda i, j: (0, i, j))
    per = T // HALO
    halo = pl.BlockSpec((2, HALO, C), lambda i, j: (0, jnp.maximum(i * per - 1, 0), j))
    w_a = pl.BlockSpec((3, C), lambda i, j: (0, j))
    w_b = pl.BlockSpec((3, C), lambda i, j: (0, j + nj))
    b_a = pl.BlockSpec((1, C), lambda i, j: (0, j))
    b_b = pl.BlockSpec((1, C), lambda i, j: (0, j + nj))
    return T, C, nj, blk, halo, w_a, w_b, b_a, b_b


def _conv_load(ext_ref, halo_ref, up_ref, first):
    for h in range(2):
        ext_ref[h, 0:HALO, :] = jnp.where(first, 0.0, halo_ref[h].astype(F32))
        ext_ref[h, HALO:, :] = up_ref[h].astype(F32)


def _up_proj_convffn(h, w_up_t, conv_w, conv_b):
    S = h.shape[0]
    tm, C = 1024, _CONV_C
    nj = D_FF // C

    def body(h_ref, wa_ref, wb_ref, cwa_ref, cwb_ref, cba_ref, cbb_ref, up_ref, act_ref, ext_ref, carry_ref):
        i, j = pl.program_id(0), pl.program_id(1)
        for half, w_ref in enumerate((wa_ref, wb_ref)):
            pre = _dot(h_ref[...], w_ref[...], "nt").astype(BF16)
            up_ref[half] = pre
            ext_ref[half, 0:HALO, :] = jnp.where(i == 0, 0.0, carry_ref[j, half])
            ext_ref[half, HALO:, :] = pre.astype(F32)
            carry_ref[j, half] = ext_ref[half, tm:tm + HALO, :]
        a, _ = _conv_taps(ext_ref, 0, cwa_ref, cba_ref)
        b, _ = _conv_taps(ext_ref, 1, cwb_ref, cbb_ref)
        _, gelu = _gelu_parts(a)
        act_ref[...] = (gelu * b).astype(BF16)

    def w_spec(half):
        return pl.BlockSpec((C, D_MODEL), lambda i, j: (j + half * nj, 0))

    def c_spec(rows, half):
        return pl.BlockSpec((rows, C), lambda i, j: (0, j + half * nj))

    return pl.pallas_call(
        body, grid=(S // tm, nj),
        in_specs=[pl.BlockSpec((tm, D_MODEL), lambda i, j: (i, 0)), w_spec(0), w_spec(1),
                  c_spec(3, 0), c_spec(3, 1), c_spec(1, 0), c_spec(1, 1)],
        out_specs=[pl.BlockSpec((2, tm, C), lambda i, j: (0, i, j)), pl.BlockSpec((tm, C), lambda i, j: (i, j))],
        out_shape=[jax.ShapeDtypeStruct((2, S, D_FF), BF16), jax.ShapeDtypeStruct((S, D_FF), BF16)],
        scratch_shapes=[pltpu.VMEM((2, HALO + tm, C), F32), pltpu.VMEM((nj, 2, HALO, C), F32)],
        compiler_params=_params("arbitrary", "arbitrary"), name="up_proj_convffn")(
            h, w_up_t, w_up_t, conv_w, conv_w, conv_b, conv_b)


def _convffn_bwd_a(up_pre, dx, w_down, conv_w, conv_b):
    S = up_pre.shape[1]
    T, C, nj, blk, halo, w_a, w_b, b_a, b_b = _conv_specs(S)
    def swap(spec):
        return pl.BlockSpec(spec.block_shape, lambda j, i, _m=spec.index_map: _m(i, j))

    def body(up_ref, halo_ref, dx_ref, wd_ref, wa_ref, wb_ref, ba_ref, bb_ref, dup_ref, dwa_ref, dwb_ref, dba_ref,
             dbb_ref, ext_ref):
        i = pl.program_id(1)
        dact = _dot(dx_ref[...], wd_ref[...], "nt")
        _conv_load(ext_ref, halo_ref, up_ref, i == 0)
        a, xa = _conv_taps(ext_ref, 0, wa_ref, ba_ref)
        b, xb = _conv_taps(ext_ref, 1, wb_ref, bb_ref)
        cdf, gelu = _gelu_parts(a)
        dgelu = cdf + a * (jnp.exp(-0.5 * a * a) * (1.0 / math.sqrt(2.0 * math.pi)))
        da = dact * b * dgelu
        db = dact * gelu
        dup_ref[0] = da.astype(BF16)
        dup_ref[1] = db.astype(BF16)
        for dval, xs, dw_ref, dbias_ref in ((da, xa, dwa_ref, dba_ref), (db, xb, dwb_ref, dbb_ref)):
            dw = jnp.concatenate([jnp.sum(dval * xk, axis=0, keepdims=True) for xk in xs], axis=0)
            dbias = jnp.sum(dval, axis=0, keepdims=True)

            @pl.when(i == 0)
            def _():
                dw_ref[...] = dw
                dbias_ref[...] = dbias

            @pl.when(i > 0)
            def _():
                dw_ref[...] += dw
                dbias_ref[...] += dbias

    dx_spec = pl.BlockSpec((T, D_MODEL), lambda j, i: (i, 0))
    wd_spec = pl.BlockSpec((C, D_MODEL), lambda j, i: (j, 0))
    dw_spec = pl.BlockSpec((3, C), lambda j, i: (0, j))
    db_spec = pl.BlockSpec((1, C), lambda j, i: (0, j))
    res = pl.pallas_call(
        body, grid=(nj, S // T),
        in_specs=[swap(blk), swap(halo), dx_spec, wd_spec, swap(w_a), swap(w_b), swap(b_a), swap(b_b)],
        out_specs=[swap(blk), dw_spec, dw_spec, db_spec, db_spec],
        out_shape=[jax.ShapeDtypeStruct((2, S, D_FF), BF16), jax.ShapeDtypeStruct((3, D_FF), F32),
                   jax.ShapeDtypeStruct((3, D_FF), F32), jax.ShapeDtypeStruct((1, D_FF), F32),
                   jax.ShapeDtypeStruct((1, D_FF), F32)],
        scratch_shapes=[pltpu.VMEM((2, HALO + T, C), F32)],
        compiler_params=_params("parallel", "arbitrary"), name="convffn_bwd_a")(
            up_pre, up_pre, dx, w_down, conv_w, conv_w, conv_b, conv_b)
    dup, dwa, dwb, dba, dbb = res
    dconv_w = jnp.concatenate([dwa, dwb], axis=1)
    dconv_b = jnp.concatenate([dba, dbb], axis=1)
    return dup, dconv_w, dconv_b


def _convffn_bwd_b(dup, conv_w):
    S = dup.shape[1]
    T, C = _CONV_T, _CONV_C
    nj = D_FF // C
    per = T // HALO
    n_blk = S // T

    def body(d_ref, halo_ref, wa_ref, wb_ref, o_ref, ext_ref):
        i = pl.program_id(0)
        n = T + HALO
        for h, w_ref in enumerate((wa_ref, wb_ref)):
            ext_ref[h, 0:T, :] = d_ref[h].astype(F32)
            ext_ref[h, T:, :] = jnp.where(i == n_blk - 1, 0.0, halo_ref[h].astype(F32))
            ext = ext_ref[h]
            y = w_ref[2:3, :] * ext[0:T] + w_ref[1:2, :] * pltpu.roll(ext, n - 1, 0)[0:T]
            y = y + w_ref[0:1, :] * pltpu.roll(ext, n - 2, 0)[0:T]
            o_ref[h] = y.astype(BF16)

    blk = pl.BlockSpec((2, T, C), lambda i, j: (0, i, j))
    halo = pl.BlockSpec((2, HALO, C), lambda i, j: (0, jnp.minimum((i + 1) * per, S // HALO - 1), j))
    return pl.pallas_call(
        body, grid=(n_blk, nj),
        in_specs=[blk, halo, pl.BlockSpec((3, C), lambda i, j: (0, j)), pl.BlockSpec((3, C), lambda i, j: (0, j + nj))],
        out_specs=blk, out_shape=jax.ShapeDtypeStruct((2, S, D_FF), BF16),
        scratch_shapes=[pltpu.VMEM((2, T + HALO, C), F32)],
        compiler_params=_params("parallel", "parallel"), name="convffn_bwd_b")(dup, dup, conv_w, conv_w)


def _position():
    return lax.axis_index("x"), lax.axis_index("y"), lax.axis_index("c")


def _flip(pos, k):
    x, y, c = pos
    return (1 - x if k & 4 else x, 1 - y if k & 2 else y, 1 - c if k & 1 else c)


def _index(pos):
    return 4 * pos[0] + 2 * pos[1] + pos[2]


def _all_gather_weights(shards):
    n = len(shards)

    def body(*refs):
        ins, outs = refs[:n], refs[n:2 * n]
        send_sems, recv_sems, local_sems = refs[2 * n:]
        me = _position()
        x, y, c = me
        sibling = (x, y, 1 - c)
        chips = [(1 - x, y), (x, 1 - y), (1 - x, 1 - y)]

        def copy(a, k, block, to, own=False):
            slot = outs[a].at[_index(block)]
            return pltpu.make_async_remote_copy(
                src_ref=ins[a] if own else slot, dst_ref=slot, send_sem=send_sems.at[7 * a + k],
                recv_sem=recv_sems.at[7 * a + k], device_id=to, device_id_type=MESH)

        mines = [pltpu.make_async_copy(ins[a], outs[a].at[_index(me)], local_sems.at[a]) for a in range(n)]
        started = []
        for a in range(n):
            mines[a].start()
            first = [copy(a, 0, me, sibling, own=True)]
            first += [copy(a, 1 + j, me, (*chip, c), own=True) for j, chip in enumerate(chips)]
            for cp in first:
                cp.start()
            started += first
        for j, chip in enumerate(chips):
            for a in range(n):
                copy(a, 1 + j, (*chip, c), me).wait_recv()
                passed = copy(a, 4 + j, (*chip, c), sibling)
                passed.start()
                started.append(passed)
        for a in range(n):
            copy(a, 0, sibling, me).wait_recv()
            for j, chip in enumerate(chips):
                copy(a, 4 + j, (*chip, 1 - c), me).wait_recv()
        for cp in started:
            cp.wait_send()
        for cp in mines:
            cp.wait()

    hbm = pl.BlockSpec(memory_space=pl.ANY)
    return pl.pallas_call(
        body, out_shape=[jax.ShapeDtypeStruct((N_DEV,) + s.shape, s.dtype) for s in shards],
        in_specs=[hbm] * n, out_specs=[hbm] * n,
        scratch_shapes=[pltpu.SemaphoreType.DMA((7 * n,)), pltpu.SemaphoreType.DMA((7 * n,)),
                        pltpu.SemaphoreType.DMA((n,))],
        name="all_gather_weights")(*shards)


_HBM = pl.BlockSpec(memory_space=pltpu.HBM)
_SEM = pl.BlockSpec(memory_space=pltpu.SEMAPHORE)
_ANY = pl.BlockSpec(memory_space=pl.ANY)
_EFFECT = pltpu.SideEffectType.DATAFLOW_SIDE_EFFECTING


def _in_hbm(a):
    return pltpu.with_memory_space_constraint(a, pltpu.HBM)


def _split_copies(name, srcs, lands, n_copies, plan, start_after, wait_after):
    ns, nl = len(srcs), len(lands)
    start_after = tuple(start_after)
    na = len(start_after)

    def start_body(*refs):
        src_refs, land_refs = refs[:ns], refs[ns:ns + nl]
        send_sems, recv_sems = refs[ns + nl + na], refs[ns + nl + na + 1]
        token = refs[-1]
        for k, (src, dst, _, peer) in enumerate(plan(src_refs, land_refs)):
            pltpu.make_async_remote_copy(src_ref=src, dst_ref=dst, send_sem=send_sems.at[k], recv_sem=recv_sems.at[k],
                                         device_id=peer, device_id_type=MESH).start()
        token[...] = jnp.zeros_like(token)

    thru = [pltpu.HBM(a.shape, a.dtype) for a in list(srcs) + list(lands)]
    res = pl.pallas_call(
        start_body, name=name + "_start",
        out_shape=[pltpu.SemaphoreType.DMA((n_copies,)), pltpu.SemaphoreType.DMA((n_copies,))] + thru
        + [jax.ShapeDtypeStruct((8, 128), F32)],
        in_specs=[_HBM] * (ns + nl) + [_ANY] * na,
        out_specs=[_SEM, _SEM] + [_HBM] * (ns + nl) + [pl.BlockSpec(memory_space=pltpu.VMEM)],
        input_output_aliases={i: 2 + i for i in range(ns + nl)},
        compiler_params=pltpu.CompilerParams(has_side_effects=_EFFECT),
    )(*[_in_hbm(a) for a in list(srcs) + list(lands)], *start_after)
    send_sems, recv_sems, token = res[0], res[1], res[-1]
    thru_vals = res[2:-1]

    def wait():
        def wait_body(*refs):
            src_refs, land_refs = refs[:ns], refs[ns:ns + nl]
            s_sems, r_sems = refs[ns + nl], refs[ns + nl + 1]
            for k, (src, _, dst, peer) in enumerate(plan(src_refs, land_refs)):
                cp = pltpu.make_async_remote_copy(src_ref=src, dst_ref=dst, send_sem=s_sems.at[k], recv_sem=r_sems.at[k],
                                                  device_id=peer, device_id_type=MESH)
                cp.wait_send()
                cp.wait_recv()

        out = pl.pallas_call(
            wait_body, name=name + "_wait", out_shape=thru,
            in_specs=[_HBM] * (ns + nl) + [_SEM, _SEM, _ANY], out_specs=[_HBM] * (ns + nl),
            input_output_aliases={i: i for i in range(ns + nl)},
            compiler_params=pltpu.CompilerParams(has_side_effects=_EFFECT),
        )(*thru_vals, send_sems, recv_sems, wait_after())
        return list(out[:ns]), list(out[ns:])

    return wait, token


def _gather_direct_plan(n):
    def plan(src_refs, land_refs):
        me = _position()
        peers = [_flip(me, k) for k in range(1, N_DEV)]
        return [(src_refs[a], land_refs[a].at[_index(me)], land_refs[a].at[_index(p)], p)
                for a in range(n) for p in peers]
    return plan


def _place_own(shards):
    me = _index(_position())
    return [lax.dynamic_update_slice(lax.empty((N_DEV,) + s.shape, s.dtype), s[None], (me, 0, 0)) for s in shards]


def _exchange_direct_plan(n):
    def plan(src_refs, land_refs):
        me = _position()
        peers = [_flip(me, k) for k in range(1, N_DEV)]
        return [(src_refs[a].at[_index(p)], land_refs[a].at[_index(me)], land_refs[a].at[_index(p)], p)
                for a in range(n) for p in peers]
    return plan


def _adamw(w, g, m, v):
    m = ADAM_B1 * m + (1.0 - ADAM_B1) * g
    v = ADAM_B2 * v + (1.0 - ADAM_B2) * (g * g)
    m_hat = m / (1.0 - ADAM_B1 ** ADAM_STEP)
    v_hat = v / (1.0 - ADAM_B2 ** ADAM_STEP)
    delta = -ADAM_LR * (m_hat / (jnp.sqrt(v_hat) + ADAM_EPS) + ADAM_WD * w)
    return delta, m, v


def _row_tile(rows, unit, cap=256):
    best = rows
    for t in range(unit, min(rows, cap) + 1, unit):
        if rows % t == 0:
            best = t
    return best


def _sum_partials(name, recv):
    n_src, R, C = recv.shape
    T = _row_tile(R, 16)

    def body(r_ref, g_out):
        g = r_ref[0].astype(F32)
        for s in range(1, n_src):
            g = g + r_ref[s].astype(F32)
        g_out[...] = g

    return pl.pallas_call(
        body, grid=(R // T,), in_specs=[pl.BlockSpec((n_src, T, C), lambda i: (0, i, 0))],
        out_specs=pl.BlockSpec((T, C), lambda i: (i, 0)), out_shape=jax.ShapeDtypeStruct((R, C), F32),
        compiler_params=_params("parallel"), name=name)(recv)


def _adamw_update(name, w, g, m, v):
    R, C = w.shape
    T = _row_tile(R, 8)

    def body(w_ref, g_ref, m_ref, v_ref, d_out, m_out, v_out):
        delta, mn, vn = _adamw(w_ref[...], g_ref[...], m_ref[...], v_ref[...])
        d_out[...] = delta
        m_out[...] = mn
        v_out[...] = vn

    blk = pl.BlockSpec((T, C), lambda i: (i, 0))
    return pl.pallas_call(
        body, grid=(R // T,), in_specs=[blk] * 4, out_specs=[blk] * 3,
        out_shape=[jax.ShapeDtypeStruct((R, C), F32)] * 3, compiler_params=_params("parallel"), name=name)(w, g, m, v)


def _gather_vmem(p_ref, gath, send_sems, recv_sems):
    me = _position()
    gath[_index(me)] = p_ref[...]
    copies = []
    for k in range(1, N_DEV):
        peer = _flip(me, k)
        copies.append(pltpu.make_async_remote_copy(
            src_ref=p_ref, dst_ref=gath.at[_index(me)], send_sem=send_sems.at[k - 1],
            recv_sem=recv_sems.at[k - 1], device_id=peer, device_id_type=MESH))
    for cp in copies:
        cp.start()
    for k in range(1, N_DEV):
        peer = _flip(me, k)
        pltpu.make_async_remote_copy(
            src_ref=p_ref, dst_ref=gath.at[_index(peer)], send_sem=send_sems.at[k - 1],
            recv_sem=recv_sems.at[k - 1], device_id=peer, device_id_type=MESH).wait_recv()
    for cp in copies:
        cp.wait_send()


def _all_gather_small(p):
    R, C = p.shape

    def body(p_ref, out_ref, send_sems, recv_sems):
        _gather_vmem(p_ref, out_ref, send_sems, recv_sems)

    vm = pl.BlockSpec(memory_space=pltpu.VMEM)
    return pl.pallas_call(
        body, in_specs=[vm], out_specs=vm, out_shape=jax.ShapeDtypeStruct((N_DEV, R, C), F32),
        scratch_shapes=[pltpu.SemaphoreType.DMA((7,)), pltpu.SemaphoreType.DMA((7,))],
        name="all_gather_small")(p)


def _small_allreduce_adamw(part, w, m, v):
    R, C = part.shape

    def body(p_ref, w_ref, m_ref, v_ref, g_out, d_out, m_out, v_out, gath, send_sems, recv_sems):
        _gather_vmem(p_ref, gath, send_sems, recv_sems)
        g = gath[0]
        for s in range(1, N_DEV):
            g = g + gath[s]
        delta, mn, vn = _adamw(w_ref[...], g, m_ref[...], v_ref[...])
        g_out[...] = g
        d_out[...] = delta
        m_out[...] = mn
        v_out[...] = vn

    vm = pl.BlockSpec(memory_space=pltpu.VMEM)
    return pl.pallas_call(
        body, in_specs=[vm] * 4, out_specs=[vm] * 4, out_shape=[jax.ShapeDtypeStruct((R, C), F32)] * 4,
        scratch_shapes=[pltpu.VMEM((N_DEV, R, C), F32), pltpu.SemaphoreType.DMA((7,)), pltpu.SemaphoreType.DMA((7,))],
        name="small_allreduce_adamw")(part, w, m, v)


_BIG = ("w_in", "w_up", "w_down", "w_out", "w_pool_out", "w_attn_out")
_TRANSPOSED = ("w_in", "w_up", "w_pool_out", "w_attn_out")
_SMALL = (("g_mix", 2048), ("b_gate", 4096), ("pool_scale", 1024), ("g_ffn", 2048), ("conv_b", 11264), ("g_final", 2048))
CONV_ROWS = 8


def _to_comm(name, shard):
    return shard.T if name in _TRANSPOSED else shard


def _pack_small(vals):
    flat = [vals[n].reshape(-1) for n, _ in _SMALL]
    used = sum(f.shape[0] for f in flat)
    flat.append(jnp.zeros((SMALL_ROWS * PACK_COLS - used,), F32))
    return jnp.concatenate(flat).reshape(SMALL_ROWS, PACK_COLS)


def _unpack_small(packed, like):
    flat = packed.reshape(-1)
    out, off = {}, 0
    for name, n in _SMALL:
        out[name] = flat[off:off + n].reshape(like[name].shape)
        off += n
    return out


def _local_step(x, target, W, small, first_token=0.0, late_weights=None, emit=None):
    S = x.shape[0]
    o_qkv, o_gate = POOL_WIDTH, POOL_WIDTH + QKV_WIDTH
    if emit is None:
        emit = lambda grads: None

    def zero(token):
        return 0.0 if token is None else token[0, 0]
    w_in_t = W["w_in"]

    h1 = _rms_fwd("rms1_fwd", x, small["g_mix"] + first_token)
    u = _mm_nt("proj_u", h1, w_in_t, tm=1024, tn=512, tk=2048, o_dtype=F32, n_rows=POOL_WIDTH, b_row_off=0)
    qkv = _mm_nt("proj_qkv", h1, w_in_t, tm=1024, tn=512, tk=2048, o_dtype=BF16, n_rows=QKV_WIDTH, b_row_off=o_qkv)
    gates_pre = _mm_nt("proj_gates", h1, w_in_t, tm=1024, tn=512, tk=2048, o_dtype=F32, n_rows=2 * D_MODEL,
                       b_row_off=o_gate)
    pool_y = _pool_fwd(u, W["w_pool_lin"], small["pool_scale"])
    outs, lses = zip(*[_attn_fwd(gi, qkv) for gi in range(3)])
    attn_y = _merge_fwd(outs, lses)
    if late_weights is not None:
        W = dict(W, **late_weights(0, attn_y))
    w_up_t = W["w_up"]
    mixed, y_pool, y_attn = _mix_fwd(pool_y, attn_y, W["w_pool_out"], W["w_attn_out"], gates_pre, small["b_gate"])
    x1 = _mm_nn("out_proj", mixed, W["w_out"], tm=1024, tn=1024, tk=2048, o_dtype=F32, res=x)

    h2 = _rms_fwd("rms2_fwd", x1, small["g_ffn"])
    nj = D_FF // 512
    up_pre, act = _up_proj_convffn(h2, w_up_t, small["conv_w_full"], small["conv_b"])
    if late_weights is not None:
        W = dict(W, **late_weights(1, act))
    x2 = _mm_nn("down_proj", act, W["w_down"], tm=1024, tn=1024, tk=D_FF // 2, o_dtype=F32, res=x1)

    dx2, dx2_b, dg_final, sq_cols = _final_loss("final_loss", x2, small["g_final"], target)

    dw_down = _mm_tn("dw_down", act, dx2_b, tm=512, tn=2048, tk=TOKEN_TK)
    tok = emit(dict(w_down=dw_down))
    dup, dconv_w, dconv_b = _convffn_bwd_a(up_pre, dx2_b, W["w_down"], small["conv_w_full"],
                                           small["conv_b"] + zero(tok))
    dup_pre = _convffn_bwd_b(dup, small["conv_w_full"])
    tkh = D_FF // 2
    dh2 = _mm("d_h2", dup_pre, w_up_t, mode="nn", grid=(S // 1024, D_MODEL // 1024, 4),
              a_spec=pl.BlockSpec((None, 1024, tkh), lambda i, j, k: (k // 2, i, k % 2)),
              b_spec=pl.BlockSpec((tkh, 1024), lambda i, j, k: (k, j)),
              o_spec=pl.BlockSpec((1024, 1024), lambda i, j, k: (i, j)),
              o_shape=(S, D_MODEL), o_dtype=F32, acc_shape=(1024, 1024))
    dw_up = _mm("dw_up", dup_pre, h2, mode="tn", grid=(2 * nj, 1, S // TOKEN_TK),
                a_spec=pl.BlockSpec((None, TOKEN_TK, 512), lambda i, j, k: (i // nj, k, i % nj)),
                b_spec=pl.BlockSpec((TOKEN_TK, D_MODEL), lambda i, j, k: (k, 0)),
                o_spec=pl.BlockSpec((512, D_MODEL), lambda i, j, k: (i, 0)),
                o_shape=(2 * D_FF, D_MODEL), o_dtype=BF16, acc_shape=(512, D_MODEL))
    tok = emit(dict(w_up=dw_up, conv_w=dconv_w))
    dx1, dx1_b, dg_ffn = _rms_bwd("rms2_bwd", x1, small["g_ffn"] + zero(tok), dh2, dx2)

    dmixed = _mm_nt("d_mixed", dx1_b, W["w_out"], tm=1024, tn=1024, tk=2048, o_dtype=BF16)
    dw_out = _mm_tn("dw_out", mixed, dx1_b, tm=512, tn=2048, tk=TOKEN_TK)
    tok = emit(dict(w_out=dw_out))
    dy_pool, dy_attn, dgates, db_gate = _mix_bwd(dmixed, gates_pre, small["b_gate"] + zero(tok), y_pool, y_attn)
    dpool_y = _mm_nn("d_pool_y", dy_pool, W["w_pool_out"], tm=1024, tn=1024, tk=2048, o_dtype=F32)
    dw_pool_out = _mm_tn("dw_pool_out", dy_pool, pool_y, tm=1024, tn=1024, tk=TOKEN_TK)
    dattn_y = _mm_nn("d_attn_y", dy_attn, W["w_attn_out"], tm=1024, tn=512, tk=2048, o_dtype=F32)
    dw_attn_out = _mm_tn("dw_attn_out", dy_attn, attn_y, tm=1024, tn=512, tk=TOKEN_TK)
    e, dw_pool_lin, dpool_scale = _pool_bwd_a(u, dpool_y, W["w_pool_lin"], small["pool_scale"])
    du = _pool_bwd_b(e)
    dos, dps = _merge_bwd(dattn_y, outs, lses)
    dqkv = [_attn_bwd(gi, qkv, dos[gi], lses[gi], dps[gi]) for gi in range(3)]
    dproj = jnp.concatenate([du] + [dqkv[gi][sec] for sec in range(3) for gi in range(3)] + [dgates], axis=1)
    dw_in = _mm_tn("dw_in", dproj, h1, tm=512, tn=2048, tk=TOKEN_TK)
    tok = emit(dict(w_in=dw_in, w_pool_out=dw_pool_out, w_attn_out=dw_attn_out, w_pool_lin=dw_pool_lin))
    dh1 = _mm_nn("d_h1", dproj, w_in_t, tm=1024, tn=1024, tk=IN_WIDTH // 4, o_dtype=F32,
                 after=tok)
    grad_x, _, dg_mix = _rms_bwd("rms1_bwd", x, small["g_mix"] + zero(tok), dh1, dx1)

    grads = dict(w_in=dw_in, w_up=dw_up, w_down=dw_down, w_out=dw_out, w_pool_out=dw_pool_out,
                 w_attn_out=dw_attn_out, w_pool_lin=dw_pool_lin, conv_w=dconv_w)
    small_grads = dict(g_mix=dg_mix, b_gate=db_gate, pool_scale=dpool_scale, g_ffn=dg_ffn, conv_b=dconv_b,
                       g_final=dg_final)
    return sq_cols, grad_x, grads, small_grads


def kernel(x, g_mix, w_in, b_gate, w_pool_lin, pool_scale, w_pool_out, w_attn_out, w_out, g_ffn, w_up, conv_w, conv_b, w_down, g_final, loss_target, m_g_mix, m_w_in, m_b_gate, m_w_pool_lin, m_pool_scale, m_w_pool_out, m_w_attn_out, m_w_out, m_g_ffn, m_w_up, m_conv_w, m_conv_b, m_w_down, m_g_final, v_g_mix, v_w_in, v_b_gate, v_w_pool_lin, v_pool_scale, v_w_pool_out, v_w_attn_out, v_w_out, v_g_ffn, v_w_up, v_conv_w, v_conv_b, v_w_down, v_g_final):
    given = dict(g_mix=g_mix, w_in=w_in, b_gate=b_gate, w_pool_lin=w_pool_lin, pool_scale=pool_scale,
                 w_pool_out=w_pool_out, w_attn_out=w_attn_out, w_out=w_out, g_ffn=g_ffn, w_up=w_up, conv_w=conv_w,
                 conv_b=conv_b, w_down=w_down, g_final=g_final)
    mom_m = dict(g_mix=m_g_mix, w_in=m_w_in, b_gate=m_b_gate, w_pool_lin=m_w_pool_lin, pool_scale=m_pool_scale,
                 w_pool_out=m_w_pool_out, w_attn_out=m_w_attn_out, w_out=m_w_out, g_ffn=m_g_ffn, w_up=m_w_up,
                 conv_w=m_conv_w, conv_b=m_conv_b, w_down=m_w_down, g_final=m_g_final)
    mom_v = dict(g_mix=v_g_mix, w_in=v_w_in, b_gate=v_b_gate, w_pool_lin=v_w_pool_lin, pool_scale=v_pool_scale,
                 w_pool_out=v_w_pool_out, w_attn_out=v_w_attn_out, w_out=v_w_out, g_ffn=v_g_ffn, w_up=v_w_up,
                 conv_w=v_conv_w, conv_b=v_conv_b, w_down=v_w_down, g_final=v_g_final)
    def full(g):
        return g.reshape(N_DEV * g.shape[1], g.shape[2])

    shard = {n: _to_comm(n, given[n][0]).astype(BF16) for n in _BIG}
    W = dict(w_in=full(_all_gather_weights([shard["w_in"]])[0]))
    lin_all = _all_gather_small(w_pool_lin[0].reshape(4 * 32, POOL_GROUP_WIDTH))
    W["w_pool_lin"] = jnp.moveaxis(lin_all.reshape(N_DEV, 4, 32, POOL_GROUP_WIDTH), 0, 1).reshape(
        4, POOL_GROUP_WIDTH, POOL_GROUP_WIDTH).astype(BF16)
    conv_w_all = _all_gather_small(jnp.pad(conv_w[0], ((0, CONV_ROWS - 3), (0, 0))))
    conv_w_full = jnp.moveaxis(conv_w_all[:, :3, :], 0, 1).reshape(3, 2 * D_FF)

    stages = (("w_out", "w_pool_out", "w_attn_out", "w_up"), ("w_down",))
    stage_after = [None, None]
    stage_wait = []
    order_after = (W["w_in"], lin_all, conv_w_all)
    for k, names in enumerate(stages):
        srcs = [shard[n] for n in names]
        wait, token = _split_copies(f"gather_stage{k}", srcs, _place_own(srcs), 7 * len(names),
                                    _gather_direct_plan(len(names)), start_after=order_after,
                                    wait_after=lambda k=k: stage_after[k])
        stage_wait.append(wait)
        order_after = (token,)
    rest_token = token

    def late_weights(stage, after):
        stage_after[stage] = after
        return {n: full(l) for n, l in zip(stages[stage], stage_wait[stage]()[1])}

    small = dict(g_mix=g_mix, b_gate=b_gate, pool_scale=pool_scale, g_ffn=g_ffn, conv_b=conv_b,
                 g_final=g_final.reshape(1, D_MODEL), conv_w_full=conv_w_full)

    pending = []
    done = {}

    def emit(group):
        names = list(group)
        parts = []
        for n in names:
            g = group[n]
            if n == "w_pool_lin":
                g = jnp.moveaxis(g.reshape(4, N_DEV, 32, POOL_GROUP_WIDTH), 1, 0).reshape(
                    N_DEV, 4 * 32, POOL_GROUP_WIDTH).astype(BF16)
            elif n == "conv_w":
                g = jnp.pad(jnp.moveaxis(g.reshape(3, N_DEV, 2 * D_FF // N_DEV), 1, 0),
                            ((0, 0), (0, CONV_ROWS - 3), (0, 0))).astype(BF16)
            else:
                g = g.reshape(N_DEV, g.shape[0] // N_DEV, g.shape[1])
            parts.append(g)
        me = _index(_position())
        lands = [lax.dynamic_update_slice(lax.empty(p.shape, p.dtype), lax.dynamic_index_in_dim(p, me, 0),
                                          (me, 0, 0)) for p in parts]
        wait, token = _split_copies(
            "exchange_" + names[0], parts, lands, 7 * len(names), _exchange_direct_plan(len(names)),
            start_after=(), wait_after=lambda: done["grad_x"])
        pending.append((names, wait))
        return token

    sq_cols, grad_x, _, small_grads = _local_step(x[0], loss_target[0], W, small, first_token=rest_token[0, 0],
                                                  late_weights=late_weights, emit=emit)
    done["grad_x"] = grad_x
    loss = lax.psum(0.5 / D_MODEL * jnp.sum(sq_cols), AXES)

    res = {}
    for n, r in [(n, r) for names, wait in pending for n, r in zip(names, wait()[1])]:
        g = _sum_partials("sum_" + n, r)
        if n in _TRANSPOSED:
            g = g.T
        elif n == "conv_w":
            g = g[:3]
        shape = given[n].shape
        rows = math.prod(shape[:-1])
        w2, m2, v2 = (d[n].reshape(rows, shape[-1]) for d in (given, mom_m, mom_v))
        upd = _adamw_update("adamw_" + n, w2, g.reshape(rows, shape[-1]), m2, v2)
        res[n] = [a.reshape(shape) for a in (g,) + tuple(upd)]

    packed_small = [_pack_small(d) for d in (given, mom_m, mom_v)]
    res_small = [_unpack_small(p, given) for p in _small_allreduce_adamw(_pack_small(small_grads), *packed_small)]

    order = ["g_mix", "w_in", "b_gate", "w_pool_lin", "pool_scale", "w_pool_out", "w_attn_out", "w_out", "g_ffn",
             "w_up", "conv_w", "conv_b", "w_down", "g_final"]
    outs = [loss, grad_x[None]]
    for kind in range(4):
        for n in order:
            outs.append(res[n][kind] if n in res else res_small[kind][n])
    return tuple(outs)
```

```python
import math

import jax
import jax.numpy as jnp
from jax import lax
from jax.experimental import pallas as pl
from jax.experimental.pallas import tpu as pltpu

F32 = jnp.float32
BF16 = jnp.bfloat16
MESH = pl.DeviceIdType.MESH
AXES = ("x", "y", "c")
N_DEV = 8

D_MODEL = 2048
POOL_WINDOWS = (2, 4, 8, 16)
POOL_GROUP_WIDTH = 256
POOL_WIDTH = 1024
ATTN_GROUPS = ((128, 1), (512, 4), (2048, 16))
SPAN = 128
HEADS_PER_GROUP = 4
N_ATTN_HEADS = 12
HEAD_DIM = 128
ATTN_WIDTH = 1536
QKV_WIDTH = 3 * ATTN_WIDTH
ATTN_OUT_WIDTH = 512
IN_WIDTH = 9728
D_FF = 5632
RMS_EPS = 1e-6
ADAM_LR, ADAM_B1, ADAM_B2, ADAM_EPS, ADAM_WD, ADAM_STEP = 0.001, 0.9, 0.999, 1e-08, 0.01, 10

VMEM_LIMIT_BYTES = 48 * 1024 * 1024
HALO = 16
TOKEN_TK = 2048
PACK_COLS = 1024
SMALL_ROWS = 24
NEG_BIG = -1e30

_DN = {"nn": (((1,), (0,)), ((), ())), "nt": (((1,), (1,)), ((), ())), "tn": (((0,), (0,)), ((), ()))}


def _params(*sem):
    return pltpu.CompilerParams(dimension_semantics=sem, vmem_limit_bytes=VMEM_LIMIT_BYTES)


def _dot(a, b, mode):
    return lax.dot_general(a.astype(BF16), b.astype(BF16), _DN[mode], preferred_element_type=F32)


def _mm(name, a, b, *, mode, grid, a_spec, b_spec, o_spec, o_shape, o_dtype, acc_shape, res=None, res_spec=None,
        after=None):
    nk = grid[2]
    n_in = 2 + (res is not None) + (after is not None)

    def body(*refs):
        a_ref, b_ref = refs[:2]
        r_ref = refs[2] if res is not None else None
        o_ref = refs[n_in]
        scr = refs[n_in + 1:]
        prod = _dot(a_ref[...], b_ref[...], mode)

        def finish(val):
            if r_ref is not None:
                val = val + r_ref[...]
            o_ref[...] = val.astype(o_ref.dtype)

        if nk == 1:
            finish(prod)
        else:
            acc = scr[0]
            k = pl.program_id(2)

            @pl.when(k == 0)
            def _():
                acc[...] = prod

            @pl.when(k > 0)
            def _():
                acc[...] += prod

            @pl.when(k == nk - 1)
            def _():
                finish(acc[...])

    in_specs = [a_spec, b_spec]
    args = [a, b]
    if res is not None:
        in_specs.append(res_spec)
        args.append(res)
    if after is not None:
        in_specs.append(pl.BlockSpec((8, 128), lambda i, j, k: (0, 0)))
        args.append(after)
    return pl.pallas_call(
        body, grid=grid, in_specs=in_specs, out_specs=o_spec,
        out_shape=jax.ShapeDtypeStruct(o_shape, o_dtype),
        scratch_shapes=[] if nk == 1 else [pltpu.VMEM(acc_shape, F32)],
        compiler_params=_params("parallel", "parallel", "arbitrary"), name=name)(*args)


def _mm_nn(name, a, b, *, tm, tn, tk, o_dtype, n_cols=None, b_col_off=0, res=None, after=None):
    M, K = a.shape
    N = b.shape[1] if n_cols is None else n_cols
    off = b_col_off // tn
    return _mm(name, a, b, mode="nn", grid=(M // tm, N // tn, K // tk),
               a_spec=pl.BlockSpec((tm, tk), lambda i, j, k: (i, k)),
               b_spec=pl.BlockSpec((tk, tn), lambda i, j, k: (k, j + off)),
               o_spec=pl.BlockSpec((tm, tn), lambda i, j, k: (i, j)),
               o_shape=(M, N), o_dtype=o_dtype, acc_shape=(tm, tn), res=res,
               res_spec=pl.BlockSpec((tm, tn), lambda i, j, k: (i, j)), after=after)


def _mm_nt(name, a, b, *, tm, tn, tk, o_dtype, n_rows=None, b_row_off=0):
    M, K = a.shape
    N = b.shape[0] if n_rows is None else n_rows
    off = b_row_off // tn
    return _mm(name, a, b, mode="nt", grid=(M // tm, N // tn, K // tk),
               a_spec=pl.BlockSpec((tm, tk), lambda i, j, k: (i, k)),
               b_spec=pl.BlockSpec((tn, tk), lambda i, j, k: (j + off, k)),
               o_spec=pl.BlockSpec((tm, tn), lambda i, j, k: (i, j)),
               o_shape=(M, N), o_dtype=o_dtype, acc_shape=(tm, tn))


def _mm_tn(name, a, b, *, tm, tn, tk):
    K, M = a.shape
    N = b.shape[1]
    return _mm(name, a, b, mode="tn", grid=(M // tm, N // tn, K // tk),
               a_spec=pl.BlockSpec((tk, tm), lambda i, j, k: (k, i)),
               b_spec=pl.BlockSpec((tk, tn), lambda i, j, k: (k, j)),
               o_spec=pl.BlockSpec((tm, tn), lambda i, j, k: (i, j)),
               o_shape=(M, N), o_dtype=BF16, acc_shape=(tm, tn))


def _rms_fwd(name, x, g):
    S, D = x.shape
    T = 512

    def body(x_ref, g_ref, h_ref):
        xv = x_ref[...]
        r = lax.rsqrt(jnp.mean(xv * xv, axis=-1, keepdims=True) + RMS_EPS)
        h_ref[...] = (xv * r * g_ref[...]).astype(BF16)

    return pl.pallas_call(
        body, grid=(S // T,),
        in_specs=[pl.BlockSpec((T, D), lambda i: (i, 0)), pl.BlockSpec((1, D), lambda i: (0, 0))],
        out_specs=pl.BlockSpec((T, D), lambda i: (i, 0)),
        out_shape=jax.ShapeDtypeStruct((S, D), BF16), compiler_params=_params("parallel"), name=name)(x, g)


def _rms_bwd(name, x, g, dh, dres):
    S, D = x.shape
    T = 256

    def body(x_ref, g_ref, dh_ref, dres_ref, dx_ref, dxb_ref, dg_ref):
        i = pl.program_id(0)
        xv = x_ref[...]
        r = lax.rsqrt(jnp.mean(xv * xv, axis=-1, keepdims=True) + RMS_EPS)
        xhat = xv * r
        dhv = dh_ref[...].astype(F32)
        gdh = dhv * g_ref[...]
        c = jnp.mean(xhat * gdh, axis=-1, keepdims=True)
        dx = dres_ref[...] + r * (gdh - xhat * c)
        dx_ref[...] = dx
        dxb_ref[...] = dx.astype(BF16)
        part = jnp.sum(dhv * xhat, axis=0, keepdims=True)

        @pl.when(i == 0)
        def _():
            dg_ref[...] = part

        @pl.when(i > 0)
        def _():
            dg_ref[...] += part

    row = pl.BlockSpec((T, D), lambda i: (i, 0))
    vec = pl.BlockSpec((1, D), lambda i: (0, 0))
    return pl.pallas_call(
        body, grid=(S // T,), in_specs=[row, vec, row, row], out_specs=[row, row, vec],
        out_shape=[jax.ShapeDtypeStruct((S, D), F32), jax.ShapeDtypeStruct((S, D), BF16),
                   jax.ShapeDtypeStruct((1, D), F32)],
        compiler_params=_params("arbitrary"), name=name)(x, g, dh, dres)


def _final_loss(name, x, g, target):
    S, D = x.shape
    T = 256

    def body(x_ref, g_ref, t_ref, dx_ref, dxb_ref, dg_ref, sq_ref):
        i = pl.program_id(0)
        xv = x_ref[...]
        r = lax.rsqrt(jnp.mean(xv * xv, axis=-1, keepdims=True) + RMS_EPS)
        xhat = xv * r
        err = xhat * g_ref[...] - t_ref[...]
        dy = err * (1.0 / D)
        gdy = dy * g_ref[...]
        c = jnp.mean(xhat * gdy, axis=-1, keepdims=True)
        dx = r * (gdy - xhat * c)
        dx_ref[...] = dx
        dxb_ref[...] = dx.astype(BF16)
        dg_part = jnp.sum(dy * xhat, axis=0, keepdims=True)
        sq_part = jnp.sum(err * err, axis=0, keepdims=True)

        @pl.when(i == 0)
        def _():
            dg_ref[...] = dg_part
            sq_ref[...] = sq_part

        @pl.when(i > 0)
        def _():
            dg_ref[...] += dg_part
            sq_ref[...] += sq_part

    row = pl.BlockSpec((T, D), lambda i: (i, 0))
    vec = pl.BlockSpec((1, D), lambda i: (0, 0))
    return pl.pallas_call(
        body, grid=(S // T,), in_specs=[row, vec, row], out_specs=[row, row, vec, vec],
        out_shape=[jax.ShapeDtypeStruct((S, D), F32), jax.ShapeDtypeStruct((S, D), BF16),
                   jax.ShapeDtypeStruct((1, D), F32), jax.ShapeDtypeStruct((1, D), F32)],
        compiler_params=_params("arbitrary"), name=name)(x, g, target)


def _prev_halo_spec(T, C, col_map):
    per = T // HALO
    return pl.BlockSpec((HALO, C), lambda *g: (jnp.maximum(g[0] * per - 1, 0), col_map(*g)))


def _pool_window_sums(ext, T, i):
    t = i * T + lax.broadcasted_iota(jnp.int32, (T, 1), 0)
    outs = []
    cnts = []
    for gi, w in enumerate(POOL_WINDOWS):
        s = ext[:, gi * POOL_GROUP_WIDTH:(gi + 1) * POOL_GROUP_WIDTH]
        sh = 1
        while sh < w:
            s = s + pltpu.roll(s, sh, 0)
            sh *= 2
        cnt = jnp.minimum(t + 1, w).astype(F32)
        outs.append(s[HALO:] / cnt - ext[HALO:, gi * POOL_GROUP_WIDTH:(gi + 1) * POOL_GROUP_WIDTH])
        cnts.append(cnt)
    return outs, cnts


def _load_ext(ext_ref, halo_ref, blk_ref, first):
    ext_ref[0:HALO, :] = jnp.where(first, 0.0, halo_ref[...].astype(F32))
    ext_ref[HALO:, :] = blk_ref[...].astype(F32)


def _pool_fwd(u, w_lin, scale):
    S = u.shape[0]
    T = 512
    W = POOL_WIDTH

    def body(u_ref, halo_ref, wl_ref, sc_ref, y_ref, ext_ref):
        i = pl.program_id(0)
        _load_ext(ext_ref, halo_ref, u_ref, i == 0)
        pooled, _ = _pool_window_sums(ext_ref[...], T, i)
        for gi in range(4):
            z = _dot(pooled[gi], wl_ref[gi], "nn")
            cs = slice(gi * POOL_GROUP_WIDTH, (gi + 1) * POOL_GROUP_WIDTH)
            y_ref[:, cs] = (z * sc_ref[:, cs]).astype(BF16)

    return pl.pallas_call(
        body, grid=(S // T,),
        in_specs=[pl.BlockSpec((T, W), lambda i: (i, 0)), _prev_halo_spec(T, W, lambda i: 0),
                  pl.BlockSpec((4, 256, 256), lambda i: (0, 0, 0)), pl.BlockSpec((1, W), lambda i: (0, 0))],
        out_specs=pl.BlockSpec((T, W), lambda i: (i, 0)),
        out_shape=jax.ShapeDtypeStruct((S, W), BF16),
        scratch_shapes=[pltpu.VMEM((HALO + T, W), F32)],
        compiler_params=_params("parallel"), name="pool_fwd")(u, u, w_lin, scale)


def _pool_bwd_a(u, dy, w_lin, scale):
    S = u.shape[0]
    T = 512
    W = POOL_WIDTH

    def body(u_ref, halo_ref, dy_ref, wl_ref, sc_ref, e_ref, dwl_ref, dsc_ref, ext_ref):
        i = pl.program_id(0)
        _load_ext(ext_ref, halo_ref, u_ref, i == 0)
        pooled, cnts = _pool_window_sums(ext_ref[...], T, i)
        for gi in range(4):
            cs = slice(gi * POOL_GROUP_WIDTH, (gi + 1) * POOL_GROUP_WIDTH)
            z = _dot(pooled[gi], wl_ref[gi], "nn")
            dyg = dy_ref[:, cs]
            dsc = jnp.sum(dyg * z, axis=0, keepdims=True)
            dz = dyg * sc_ref[:, cs]
            dwl = _dot(pooled[gi], dz, "tn")
            dpooled = _dot(dz, wl_ref[gi], "nt")
            e_ref[:, cs] = dpooled / cnts[gi]

            @pl.when(i == 0)
            def _():
                dwl_ref[gi] = dwl
                dsc_ref[:, cs] = dsc

            @pl.when(i > 0)
            def _():
                dwl_ref[gi] += dwl
                dsc_ref[:, cs] += dsc

    return pl.pallas_call(
        body, grid=(S // T,),
        in_specs=[pl.BlockSpec((T, W), lambda i: (i, 0)), _prev_halo_spec(T, W, lambda i: 0),
                  pl.BlockSpec((T, W), lambda i: (i, 0)),
                  pl.BlockSpec((4, 256, 256), lambda i: (0, 0, 0)), pl.BlockSpec((1, W), lambda i: (0, 0))],
        out_specs=[pl.BlockSpec((T, W), lambda i: (i, 0)), pl.BlockSpec((4, 256, 256), lambda i: (0, 0, 0)),
                   pl.BlockSpec((1, W), lambda i: (0, 0))],
        out_shape=[jax.ShapeDtypeStruct((S, W), F32), jax.ShapeDtypeStruct((4, 256, 256), F32),
                   jax.ShapeDtypeStruct((1, W), F32)],
        scratch_shapes=[pltpu.VMEM((HALO + T, W), F32)],
        compiler_params=_params("arbitrary"), name="pool_bwd_a")(u, u, dy, w_lin, scale)


def _pool_bwd_b(e):
    S = e.shape[0]
    T = 512
    W = POOL_WIDTH
    per = T // HALO
    n_blk = S // T

    def body(e_ref, halo_ref, du_ref, ext_ref):
        i = pl.program_id(0)
        ext_ref[0:T, :] = e_ref[...]
        ext_ref[T:, :] = jnp.where(i == n_blk - 1, 0.0, halo_ref[...])
        ext = ext_ref[...]
        n = T + HALO
        t = i * T + lax.broadcasted_iota(jnp.int32, (T, 1), 0)
        for gi, w in enumerate(POOL_WINDOWS):
            cs = slice(gi * POOL_GROUP_WIDTH, (gi + 1) * POOL_GROUP_WIDTH)
            s = ext[:, cs]
            sh = 1
            while sh < w:
                s = s + pltpu.roll(s, n - sh, 0)
                sh *= 2
            cnt = jnp.minimum(t + 1, w).astype(F32)
            du_ref[:, cs] = (s[0:T] - ext[0:T, cs] * cnt).astype(BF16)

    return pl.pallas_call(
        body, grid=(n_blk,),
        in_specs=[pl.BlockSpec((T, W), lambda i: (i, 0)),
                  pl.BlockSpec((HALO, W), lambda i: (jnp.minimum((i + 1) * per, S // HALO - 1), 0))],
        out_specs=pl.BlockSpec((T, W), lambda i: (i, 0)),
        out_shape=jax.ShapeDtypeStruct((S, W), BF16),
        scratch_shapes=[pltpu.VMEM((T + HALO, W), F32)],
        compiler_params=_params("parallel"), name="pool_bwd_b")(e, e)


def _slope(head):
    return 2.0 ** (-8.0 * (head + 1) / N_ATTN_HEADS)


def _attn_masks(b):
    row = lax.broadcasted_iota(jnp.int32, (SPAN, SPAN), 0)
    col = lax.broadcasted_iota(jnp.int32, (SPAN, SPAN), 1)
    dist_cur = (row - col).astype(F32)
    valid_cur = col <= row
    valid_prev = jnp.logical_and(col >= row, b > 0)
    return dist_cur, valid_cur, valid_prev


ATTN_BATCH = 4


def _attn_tiling(gi, S):
    d = ATTN_GROUPS[gi][1]
    hp = HEADS_PER_GROUP if d < 16 else 1
    return d, SPAN * d, S // (SPAN * d), hp, hp * HEAD_DIM, ATTN_BATCH // hp


_BATCHED_DN = {"nt": (((2,), (2,)), ((0,), (0,))), "nn": (((2,), (1,)), ((0,), (0,))), "tn": (((1,), (1,)), ((0,), (0,)))}


def _bdot(a, b, mode):
    return lax.dot_general(a.astype(BF16), b.astype(BF16), _BATCHED_DN[mode], preferred_element_type=F32)


def _batch_rows(it, d, rb):
    return [pl.ds(it * rb + j, SPAN, stride=d) for j in range(rb)]


def _take(scr, hp, rows):
    return jnp.stack([scr[hi, rw, :] for hi in range(hp) for rw in rows])


def _put(scr, hp, rows, val):
    for n, (hi, rw) in enumerate((hi, rw) for hi in range(hp) for rw in rows):
        scr[hi, rw, :] = val[n]


def _batch_bias(gi, d, hp, rb, dist):
    return jnp.stack([(-(_head_slope(gi, hp, hi) * d)) * dist for hi in range(hp) for _ in range(rb)])


def _head_slope(gi, hp, hi):
    if hp == HEADS_PER_GROUP:
        return _slope(gi * HEADS_PER_GROUP + hi)
    first = pl.program_id(1) * hp
    sl = jnp.float32(_slope(gi * HEADS_PER_GROUP + hi))
    for h0 in range(hp, HEADS_PER_GROUP, hp):
        sl = jnp.where(first == h0, jnp.float32(_slope(gi * HEADS_PER_GROUP + h0 + hi)), sl)
    return sl


def _split_heads(src_ref, dst_ref, hp):
    for hi in range(hp):
        dst_ref[hi] = src_ref[:, hi * HEAD_DIM:(hi + 1) * HEAD_DIM].astype(F32)


def _attn_fwd(gi, qkv):
    S = qkv.shape[0]
    d, C, nb, hp, W, rb = _attn_tiling(gi, S)
    scale = HEAD_DIM ** -0.5

    def body(q_ref, kc_ref, kp_ref, vc_ref, vp_ref, o_ref, lse_ref, qs, kcs, kps, vcs, vps, os, ls):
        b = pl.program_id(0)
        for src, dst in ((q_ref, qs), (kc_ref, kcs), (kp_ref, kps), (vc_ref, vcs), (vp_ref, vps)):
            _split_heads(src, dst, hp)
        dist_cur, valid_cur, valid_prev = _attn_masks(b)
        bias_cur = _batch_bias(gi, d, hp, rb, dist_cur)
        bias_far = _batch_bias(gi, d, hp, rb, dist_cur + SPAN)

        def sub_sequences(it, carry):
            rows = _batch_rows(it, d, rb)
            q = _take(qs, hp, rows)
            s_cur = jnp.where(valid_cur, _bdot(q, _take(kcs, hp, rows), "nt") * scale + bias_cur, NEG_BIG)
            s_prev = jnp.where(valid_prev, _bdot(q, _take(kps, hp, rows), "nt") * scale + bias_far, NEG_BIG)
            m = jnp.maximum(jnp.max(s_cur, axis=-1, keepdims=True), jnp.max(s_prev, axis=-1, keepdims=True))
            p_cur = jnp.exp(s_cur - m)
            p_prev = jnp.exp(s_prev - m)
            l = jnp.sum(p_cur, axis=-1, keepdims=True) + jnp.sum(p_prev, axis=-1, keepdims=True)
            o = (_bdot(p_cur, _take(vcs, hp, rows), "nn") + _bdot(p_prev, _take(vps, hp, rows), "nn")) / l
            _put(os, hp, rows, o)
            _put(ls, hp, rows, jnp.broadcast_to(m + jnp.log(l), o.shape))
            return carry

        lax.fori_loop(0, d // rb, sub_sequences, 0)
        for hi in range(hp):
            hs = slice(hi * HEAD_DIM, (hi + 1) * HEAD_DIM)
            o_ref[:, hs] = os[hi]
            lse_ref[:, hs] = ls[hi]

    per_sec = ATTN_WIDTH // W
    per_grp = ATTN_OUT_WIDTH // W

    def spec(sec, shift):
        return pl.BlockSpec((C, W), lambda b, h: (jnp.maximum(b + shift, 0), sec * per_sec + gi * per_grp + h))

    out = pl.BlockSpec((C, W), lambda b, h: (b, h))
    scr = pltpu.VMEM((hp, C, HEAD_DIM), F32)
    return pl.pallas_call(
        body, grid=(nb, HEADS_PER_GROUP // hp),
        in_specs=[spec(0, 0), spec(1, 0), spec(1, -1), spec(2, 0), spec(2, -1)], out_specs=[out, out],
        out_shape=[jax.ShapeDtypeStruct((S, ATTN_OUT_WIDTH), F32)] * 2, scratch_shapes=[scr] * 7,
        compiler_params=_params("parallel", "parallel"), name=f"attn_fwd_g{gi}")(qkv, qkv, qkv, qkv, qkv)


def _attn_bwd(gi, qkv, do, lse, dp):
    S = qkv.shape[0]
    d, C, nb, hp, W, rb = _attn_tiling(gi, S)
    scale = HEAD_DIM ** -0.5
    n_in = 12

    def body(*refs):
        ins, (dq_ref, dk_ref, dv_ref) = refs[:n_in], refs[n_in:n_in + 3]
        scr = refs[n_in + 3:]
        for src, dst in zip(ins, scr[:n_in]):
            _split_heads(src, dst, hp)
        qs, kcs, vcs, kps, vps, qns, dos, dons, lses, lsens, dps, dpns = scr[:n_in]
        dqs, dks, dvs = scr[n_in:]
        b = pl.program_id(0)
        dist_cur, valid_cur, valid_prev = _attn_masks(b)
        row = lax.broadcasted_iota(jnp.int32, (SPAN, SPAN), 0)
        col = lax.broadcasted_iota(jnp.int32, (SPAN, SPAN), 1)
        valid_next = jnp.logical_and(col >= row, b < nb - 1)

        bias_cur = _batch_bias(gi, d, hp, rb, dist_cur)
        bias_far = _batch_bias(gi, d, hp, rb, dist_cur + SPAN)

        def sub_sequences(it, carry):
            rows = _batch_rows(it, d, rb)
            q, kc, vc, kp, vp, qn, dov, don = (_take(s, hp, rows) for s in (qs, kcs, vcs, kps, vps, qns, dos, dons))
            lse_b, lse_n, dp_b, dp_n = (_take(s, hp, rows) for s in (lses, lsens, dps, dpns))
            p_cur = jnp.where(valid_cur, jnp.exp(_bdot(q, kc, "nt") * scale + bias_cur - lse_b), 0.0)
            p_prev = jnp.where(valid_prev, jnp.exp(_bdot(q, kp, "nt") * scale + bias_far - lse_b), 0.0)
            ds_cur = p_cur * (_bdot(dov, vc, "nt") - dp_b)
            ds_prev = p_prev * (_bdot(dov, vp, "nt") - dp_b)
            _put(dqs, hp, rows, (_bdot(ds_cur, kc, "nn") + _bdot(ds_prev, kp, "nn")) * scale)
            p_next = jnp.where(valid_next, jnp.exp(_bdot(qn, kc, "nt") * scale + bias_far - lse_n), 0.0)
            ds_next = p_next * (_bdot(don, vc, "nt") - dp_n)
            _put(dvs, hp, rows, _bdot(p_cur, dov, "tn") + _bdot(p_next, don, "tn"))
            _put(dks, hp, rows, (_bdot(ds_cur, q, "tn") + _bdot(ds_next, qn, "tn")) * scale)
            return carry

        lax.fori_loop(0, d // rb, sub_sequences, 0)
        for hi in range(hp):
            hs = slice(hi * HEAD_DIM, (hi + 1) * HEAD_DIM)
            dq_ref[:, hs] = dqs[hi].astype(BF16)
            dk_ref[:, hs] = dks[hi].astype(BF16)
            dv_ref[:, hs] = dvs[hi].astype(BF16)

    per_sec = ATTN_WIDTH // W
    per_grp = ATTN_OUT_WIDTH // W

    def qkv_spec(sec, shift):
        return pl.BlockSpec((C, W), lambda b, h: (jnp.clip(b + shift, 0, nb - 1), sec * per_sec + gi * per_grp + h))

    def act_spec(shift):
        return pl.BlockSpec((C, W), lambda b, h: (jnp.clip(b + shift, 0, nb - 1), h))

    out = act_spec(0)
    scr = pltpu.VMEM((hp, C, HEAD_DIM), F32)
    return pl.pallas_call(
        body, grid=(nb, HEADS_PER_GROUP // hp),
        in_specs=[qkv_spec(0, 0), qkv_spec(1, 0), qkv_spec(2, 0), qkv_spec(1, -1), qkv_spec(2, -1), qkv_spec(0, 1),
                  act_spec(0), act_spec(1), act_spec(0), act_spec(1), act_spec(0), act_spec(1)],
        out_specs=[out, out, out], out_shape=[jax.ShapeDtypeStruct((S, ATTN_OUT_WIDTH), BF16)] * 3,
        scratch_shapes=[scr] * (n_in + 3),
        compiler_params=_params("parallel", "parallel"), name=f"attn_bwd_g{gi}")(
            qkv, qkv, qkv, qkv, qkv, qkv, do, do, lse, lse, dp, dp)


def _group_weights(l_refs):
    l0, l1, l2 = (r[...] for r in l_refs)
    m = jnp.maximum(jnp.maximum(l0, l1), l2)
    e = [jnp.exp(l0 - m), jnp.exp(l1 - m), jnp.exp(l2 - m)]
    tot = e[0] + e[1] + e[2]
    return [ei / tot for ei in e]


def _merge_fwd(outs, lses):
    S, W = outs[0].shape
    T = 512

    def body(o0, o1, o2, l0, l1, l2, y_ref):
        w = _group_weights((l0, l1, l2))
        y_ref[...] = (w[0] * o0[...] + w[1] * o1[...] + w[2] * o2[...]).astype(BF16)

    blk = pl.BlockSpec((T, W), lambda i: (i, 0))
    return pl.pallas_call(
        body, grid=(S // T,), in_specs=[blk] * 6, out_specs=blk, out_shape=jax.ShapeDtypeStruct((S, W), BF16),
        compiler_params=_params("parallel"), name="merge_fwd")(*outs, *lses)


def _merge_bwd(dy, outs, lses):
    S, W = outs[0].shape
    T = 512

    def body(dy_ref, o0, o1, o2, l0, l1, l2, do0, do1, do2, dp0, dp1, dp2):
        w = _group_weights((l0, l1, l2))
        dyv = dy_ref[...]
        y = w[0] * o0[...] + w[1] * o1[...] + w[2] * o2[...]
        prod = dyv * y
        dots = []
        for hi in range(HEADS_PER_GROUP):
            hs = slice(hi * HEAD_DIM, (hi + 1) * HEAD_DIM)
            dots.append(jnp.broadcast_to(jnp.sum(prod[:, hs], axis=-1, keepdims=True), (T, HEAD_DIM)))
        dot_b = jnp.concatenate(dots, axis=1)
        for wg, do_ref, dp_ref in zip(w, (do0, do1, do2), (dp0, dp1, dp2)):
            do_ref[...] = (wg * dyv).astype(BF16)
            dp_ref[...] = wg * dot_b

    blk = pl.BlockSpec((T, W), lambda i: (i, 0))
    res = pl.pallas_call(
        body, grid=(S // T,), in_specs=[blk] * 7, out_specs=[blk] * 6,
        out_shape=[jax.ShapeDtypeStruct((S, W), BF16)] * 3 + [jax.ShapeDtypeStruct((S, W), F32)] * 3,
        compiler_params=_params("parallel"), name="merge_bwd")(dy, *outs, *lses)
    return res[:3], res[3:]


def _mix_fwd(pool_y, attn_y, w_pool_out_t, w_attn_out_t, gates_pre, b_gate):
    S = pool_y.shape[0]
    D = D_MODEL
    tm, tn = 1024, 512
    nj = D // tn

    def body(p_ref, a_ref, wp_ref, wa_ref, g0_ref, g1_ref, b0_ref, b1_ref, mixed_ref, yp_ref, ya_ref):
        yp = _dot(p_ref[...], wp_ref[...], "nt")
        ya = _dot(a_ref[...], wa_ref[...], "nt")
        g0 = jax.nn.sigmoid(g0_ref[...] + b0_ref[...])
        g1 = jax.nn.sigmoid(g1_ref[...] + b1_ref[...])
        mixed_ref[...] = (g0 * yp + g1 * ya).astype(BF16)
        yp_ref[...] = yp.astype(BF16)
        ya_ref[...] = ya.astype(BF16)

    out = pl.BlockSpec((tm, tn), lambda i, j: (i, j))
    return pl.pallas_call(
        body, grid=(S // tm, nj),
        in_specs=[pl.BlockSpec((tm, POOL_WIDTH), lambda i, j: (i, 0)),
                  pl.BlockSpec((tm, ATTN_OUT_WIDTH), lambda i, j: (i, 0)),
                  pl.BlockSpec((tn, POOL_WIDTH), lambda i, j: (j, 0)),
                  pl.BlockSpec((tn, ATTN_OUT_WIDTH), lambda i, j: (j, 0)),
                  pl.BlockSpec((tm, tn), lambda i, j: (i, j)), pl.BlockSpec((tm, tn), lambda i, j: (i, j + nj)),
                  pl.BlockSpec((1, tn), lambda i, j: (0, j)), pl.BlockSpec((1, tn), lambda i, j: (0, j + nj))],
        out_specs=[out, out, out], out_shape=[jax.ShapeDtypeStruct((S, D), BF16)] * 3,
        compiler_params=_params("parallel", "parallel"), name="mix_fwd")(
            pool_y, attn_y, w_pool_out_t, w_attn_out_t, gates_pre, gates_pre, b_gate, b_gate)


def _mix_bwd(dmixed, gates_pre, b_gate, y_pool, y_attn):
    S = dmixed.shape[0]
    D = D_MODEL
    T = 256

    def body(dm_ref, gp_ref, b_ref, yp_ref, ya_ref, dyp_ref, dya_ref, dg_ref, db_ref):
        i = pl.program_id(0)
        dm = dm_ref[...].astype(F32)
        parts = []
        for br, (y_ref, dy_ref) in enumerate(((yp_ref, dyp_ref), (ya_ref, dya_ref))):
            cs = slice(br * D, (br + 1) * D)
            g = jax.nn.sigmoid(gp_ref[:, cs] + b_ref[:, cs])
            dy_ref[...] = (dm * g).astype(BF16)
            dpre = dm * y_ref[...].astype(F32) * g * (1.0 - g)
            dg_ref[:, cs] = dpre.astype(BF16)
            parts.append(jnp.sum(dpre, axis=0, keepdims=True))

        @pl.when(i == 0)
        def _():
            db_ref[:, 0:D] = parts[0]
            db_ref[:, D:2 * D] = parts[1]

        @pl.when(i > 0)
        def _():
            db_ref[:, 0:D] += parts[0]
            db_ref[:, D:2 * D] += parts[1]

    row = pl.BlockSpec((T, D), lambda i: (i, 0))
    row2 = pl.BlockSpec((T, 2 * D), lambda i: (i, 0))
    vec2 = pl.BlockSpec((1, 2 * D), lambda i: (0, 0))
    return pl.pallas_call(
        body, grid=(S // T,), in_specs=[row, row2, vec2, row, row], out_specs=[row, row, row2, vec2],
        out_shape=[jax.ShapeDtypeStruct((S, D), BF16), jax.ShapeDtypeStruct((S, D), BF16),
                   jax.ShapeDtypeStruct((S, 2 * D), BF16), jax.ShapeDtypeStruct((1, 2 * D), F32)],
        compiler_params=_params("arbitrary"), name="mix_bwd")(dmixed, gates_pre, b_gate, y_pool, y_attn)


_CONV_T, _CONV_C = 512, 512


def _conv_taps(ext_ref, h, w_ref, b_ref):
    ext = ext_ref[h]
    x2 = pltpu.roll(ext, 2, 0)[HALO:]
    x1 = pltpu.roll(ext, 1, 0)[HALO:]
    x0 = ext[HALO:]
    y = b_ref[...] + w_ref[0:1, :] * x2
    y = y + w_ref[1:2, :] * x1
    y = y + w_ref[2:3, :] * x0
    return y, (x2, x1, x0)


def _gelu_parts(a):
    cdf = 0.5 * (1.0 + lax.erf(a * (1.0 / math.sqrt(2.0))))
    return cdf, a * cdf


def _conv_specs(S):
    T, C = _CONV_T, _CONV_C
    nj = D_FF // C
    blk = pl.BlockSpec((2, T, C), lambda i, j: (0, i, j))
    per = T // HALO
    halo = pl.BlockSpec((2, HALO, C), lambda i, j: (0, jnp.maximum(i * per - 1, 0), j))
    w_a = pl.BlockSpec((3, C), lambda i, j: (0, j))
    w_b = pl.BlockSpec((3, C), lambda i, j: (0, j + nj))
    b_a = pl.BlockSpec((1, C), lambda i, j: (0, j))
    b_b = pl.BlockSpec((1, C), lambda i, j: (0, j + nj))
    return T, C, nj, blk, halo, w_a, w_b, b_a, b_b


def _conv_load(ext_ref, halo_ref, up_ref, first):
    for h in range(2):
        ext_ref[h, 0:HALO, :] = jnp.where(first, 0.0, halo_ref[h].astype(F32))
        ext_ref[h, HALO:, :] = up_ref[h].astype(F32)


def _up_proj_convffn(h, w_up_t, conv_w, conv_b):
    S = h.shape[0]
    tm, C = 1024, _CONV_C
    nj = D_FF // C

    def body(h_ref, wa_ref, wb_ref, cwa_ref, cwb_ref, cba_ref, cbb_ref, up_ref, act_ref, ext_ref, carry_ref):
        i, j = pl.program_id(0), pl.program_id(1)
        for half, w_ref in enumerate((wa_ref, wb_ref)):
            pre = _dot(h_ref[...], w_ref[...], "nt").astype(BF16)
            up_ref[half] = pre
            ext_ref[half, 0:HALO, :] = jnp.where(i == 0, 0.0, carry_ref[j, half])
            ext_ref[half, HALO:, :] = pre.astype(F32)
            carry_ref[j, half] = ext_ref[half, tm:tm + HALO, :]
        a, _ = _conv_taps(ext_ref, 0, cwa_ref, cba_ref)
        b, _ = _conv_taps(ext_ref, 1, cwb_ref, cbb_ref)
        _, gelu = _gelu_parts(a)
        act_ref[...] = (gelu * b).astype(BF16)

    def w_spec(half):
        return pl.BlockSpec((C, D_MODEL), lambda i, j: (j + half * nj, 0))

    def c_spec(rows, half):
        return pl.BlockSpec((rows, C), lambda i, j: (0, j + half * nj))

    return pl.pallas_call(
        body, grid=(S // tm, nj),
        in_specs=[pl.BlockSpec((tm, D_MODEL), lambda i, j: (i, 0)), w_spec(0), w_spec(1),
                  c_spec(3, 0), c_spec(3, 1), c_spec(1, 0), c_spec(1, 1)],
        out_specs=[pl.BlockSpec((2, tm, C), lambda i, j: (0, i, j)), pl.BlockSpec((tm, C), lambda i, j: (i, j))],
        out_shape=[jax.ShapeDtypeStruct((2, S, D_FF), BF16), jax.ShapeDtypeStruct((S, D_FF), BF16)],
        scratch_shapes=[pltpu.VMEM((2, HALO + tm, C), F32), pltpu.VMEM((nj, 2, HALO, C), F32)],
        compiler_params=_params("arbitrary", "arbitrary"), name="up_proj_convffn")(
            h, w_up_t, w_up_t, conv_w, conv_w, conv_b, conv_b)


def _convffn_bwd_a(up_pre, dx, w_down, conv_w, conv_b):
    S = up_pre.shape[1]
    T, C, nj, blk, halo, w_a, w_b, b_a, b_b = _conv_specs(S)
    def swap(spec):
        return pl.BlockSpec(spec.block_shape, lambda j, i, _m=spec.index_map: _m(i, j))

    def body(up_ref, halo_ref, dx_ref, wd_ref, wa_ref, wb_ref, ba_ref, bb_ref, dup_ref, dwa_ref, dwb_ref, dba_ref,
             dbb_ref, ext_ref):
        i = pl.program_id(1)
        dact = _dot(dx_ref[...], wd_ref[...], "nt")
        _conv_load(ext_ref, halo_ref, up_ref, i == 0)
        a, xa = _conv_taps(ext_ref, 0, wa_ref, ba_ref)
        b, xb = _conv_taps(ext_ref, 1, wb_ref, bb_ref)
        cdf, gelu = _gelu_parts(a)
        dgelu = cdf + a * (jnp.exp(-0.5 * a * a) * (1.0 / math.sqrt(2.0 * math.pi)))
        da = dact * b * dgelu
        db = dact * gelu
        dup_ref[0] = da.astype(BF16)
        dup_ref[1] = db.astype(BF16)
        for dval, xs, dw_ref, dbias_ref in ((da, xa, dwa_ref, dba_ref), (db, xb, dwb_ref, dbb_ref)):
            dw = jnp.concatenate([jnp.sum(dval * xk, axis=0, keepdims=True) for xk in xs], axis=0)
            dbias = jnp.sum(dval, axis=0, keepdims=True)

            @pl.when(i == 0)
            def _():
                dw_ref[...] = dw
                dbias_ref[...] = dbias

            @pl.when(i > 0)
            def _():
                dw_ref[...] += dw
                dbias_ref[...] += dbias

    dx_spec = pl.BlockSpec((T, D_MODEL), lambda j, i: (i, 0))
    wd_spec = pl.BlockSpec((C, D_MODEL), lambda j, i: (j, 0))
    dw_spec = pl.BlockSpec((3, C), lambda j, i: (0, j))
    db_spec = pl.BlockSpec((1, C), lambda j, i: (0, j))
    res = pl.pallas_call(
        body, grid=(nj, S // T),
        in_specs=[swap(blk), swap(halo), dx_spec, wd_spec, swap(w_a), swap(w_b), swap(b_a), swap(b_b)],
        out_specs=[swap(blk), dw_spec, dw_spec, db_spec, db_spec],
        out_shape=[jax.ShapeDtypeStruct((2, S, D_FF), BF16), jax.ShapeDtypeStruct((3, D_FF), F32),
                   jax.ShapeDtypeStruct((3, D_FF), F32), jax.ShapeDtypeStruct((1, D_FF), F32),
                   jax.ShapeDtypeStruct((1, D_FF), F32)],
        scratch_shapes=[pltpu.VMEM((2, HALO + T, C), F32)],
        compiler_params=_params("parallel", "arbitrary"), name="convffn_bwd_a")(
            up_pre, up_pre, dx, w_down, conv_w, conv_w, conv_b, conv_b)
    dup, dwa, dwb, dba, dbb = res
    dconv_w = jnp.concatenate([dwa, dwb], axis=1)
    dconv_b = jnp.concatenate([dba, dbb], axis=1)
    return dup, dconv_w, dconv_b


def _convffn_bwd_b(dup, conv_w, w_up_t):
    S = dup.shape[1]
    tm, tk = 512, D_FF // 4
    per_half = D_FF // tk
    nk = 2 * per_half
    per = tm // HALO
    n_blk = S // tm

    def body(d_ref, halo_ref, cw_ref, w_ref, dpre_ref, dh_ref, ext_ref, acc_ref):
        i, k = pl.program_id(0), pl.program_id(1)
        n = tm + HALO
        ext_ref[0:tm, :] = d_ref[...].astype(F32)
        ext_ref[tm:, :] = jnp.where(i == n_blk - 1, 0.0, halo_ref[...].astype(F32))
        ext = ext_ref[...]
        y = cw_ref[2:3, :] * ext[0:tm] + cw_ref[1:2, :] * pltpu.roll(ext, n - 1, 0)[0:tm]
        y = y + cw_ref[0:1, :] * pltpu.roll(ext, n - 2, 0)[0:tm]
        pre = y.astype(BF16)
        dpre_ref[...] = pre
        prod = _dot(pre, w_ref[...], "nn")

        @pl.when(k == 0)
        def _():
            acc_ref[...] = prod

        @pl.when(k > 0)
        def _():
            acc_ref[...] += prod

        @pl.when(k == nk - 1)
        def _():
            dh_ref[...] = acc_ref[...]

    blk = pl.BlockSpec((None, tm, tk), lambda i, k: (k // per_half, i, k % per_half))
    halo = pl.BlockSpec((None, HALO, tk),
                        lambda i, k: (k // per_half, jnp.minimum((i + 1) * per, S // HALO - 1), k % per_half))
    return pl.pallas_call(
        body, grid=(n_blk, nk),
        in_specs=[blk, halo, pl.BlockSpec((3, tk), lambda i, k: (0, k)), pl.BlockSpec((tk, D_MODEL), lambda i, k: (k, 0))],
        out_specs=[blk, pl.BlockSpec((tm, D_MODEL), lambda i, k: (i, 0))],
        out_shape=[jax.ShapeDtypeStruct((2, S, D_FF), BF16), jax.ShapeDtypeStruct((S, D_MODEL), F32)],
        scratch_shapes=[pltpu.VMEM((tm + HALO, tk), F32), pltpu.VMEM((tm, D_MODEL), F32)],
        compiler_params=_params("parallel", "arbitrary"), name="convffn_bwd_b")(dup, dup, conv_w, w_up_t)


def _position():
    return lax.axis_index("x"), lax.axis_index("y"), lax.axis_index("c")


def _flip(pos, k):
    x, y, c = pos
    return (1 - x if k & 4 else x, 1 - y if k & 2 else y, 1 - c if k & 1 else c)


def _index(pos):
    return 4 * pos[0] + 2 * pos[1] + pos[2]


def _all_gather_weights(shards):
    n = len(shards)

    def body(*refs):
        ins, outs = refs[:n], refs[n:2 * n]
        send_sems, recv_sems, local_sems = refs[2 * n:]
        me = _position()
        x, y, c = me
        sibling = (x, y, 1 - c)
        chips = [(1 - x, y), (x, 1 - y), (1 - x, 1 - y)]

        def copy(a, k, block, to, own=False):
            slot = outs[a].at[_index(block)]
            return pltpu.make_async_remote_copy(
                src_ref=ins[a] if own else slot, dst_ref=slot, send_sem=send_sems.at[7 * a + k],
                recv_sem=recv_sems.at[7 * a + k], device_id=to, device_id_type=MESH)

        mines = [pltpu.make_async_copy(ins[a], outs[a].at[_index(me)], local_sems.at[a]) for a in range(n)]
        started = []
        for a in range(n):
            mines[a].start()
            first = [copy(a, 0, me, sibling, own=True)]
            first += [copy(a, 1 + j, me, (*chip, c), own=True) for j, chip in enumerate(chips)]
            for cp in first:
                cp.start()
            started += first
        for j, chip in enumerate(chips):
            for a in range(n):
                copy(a, 1 + j, (*chip, c), me).wait_recv()
                passed = copy(a, 4 + j, (*chip, c), sibling)
                passed.start()
                started.append(passed)
        for a in range(n):
            copy(a, 0, sibling, me).wait_recv()
            for j, chip in enumerate(chips):
                copy(a, 4 + j, (*chip, 1 - c), me).wait_recv()
        for cp in started:
            cp.wait_send()
        for cp in mines:
            cp.wait()

    hbm = pl.BlockSpec(memory_space=pl.ANY)
    return pl.pallas_call(
        body, out_shape=[jax.ShapeDtypeStruct((N_DEV,) + s.shape, s.dtype) for s in shards],
        in_specs=[hbm] * n, out_specs=[hbm] * n,
        scratch_shapes=[pltpu.SemaphoreType.DMA((7 * n,)), pltpu.SemaphoreType.DMA((7 * n,)),
                        pltpu.SemaphoreType.DMA((n,))],
        name="all_gather_weights")(*shards)


_HBM = pl.BlockSpec(memory_space=pltpu.HBM)
_SEM = pl.BlockSpec(memory_space=pltpu.SEMAPHORE)
_ANY = pl.BlockSpec(memory_space=pl.ANY)
_EFFECT = pltpu.SideEffectType.DATAFLOW_SIDE_EFFECTING


def _in_hbm(a):
    return pltpu.with_memory_space_constraint(a, pltpu.HBM)


def _split_copies(name, srcs, lands, n_copies, plan, start_after, wait_after):
    ns, nl = len(srcs), len(lands)
    start_after = tuple(start_after)
    na = len(start_after)

    def start_body(*refs):
        src_refs, land_refs = refs[:ns], refs[ns:ns + nl]
        send_sems, recv_sems = refs[ns + nl + na], refs[ns + nl + na + 1]
        token = refs[-1]
        for k, (src, dst, _, peer) in enumerate(plan(src_refs, land_refs)):
            pltpu.make_async_remote_copy(src_ref=src, dst_ref=dst, send_sem=send_sems.at[k], recv_sem=recv_sems.at[k],
                                         device_id=peer, device_id_type=MESH).start()
        token[...] = jnp.zeros_like(token)

    thru = [pltpu.HBM(a.shape, a.dtype) for a in list(srcs) + list(lands)]
    res = pl.pallas_call(
        start_body, name=name + "_start",
        out_shape=[pltpu.SemaphoreType.DMA((n_copies,)), pltpu.SemaphoreType.DMA((n_copies,))] + thru
        + [jax.ShapeDtypeStruct((8, 128), F32)],
        in_specs=[_HBM] * (ns + nl) + [_ANY] * na,
        out_specs=[_SEM, _SEM] + [_HBM] * (ns + nl) + [pl.BlockSpec(memory_space=pltpu.VMEM)],
        input_output_aliases={i: 2 + i for i in range(ns + nl)},
        compiler_params=pltpu.CompilerParams(has_side_effects=_EFFECT),
    )(*[_in_hbm(a) for a in list(srcs) + list(lands)], *start_after)
    send_sems, recv_sems, token = res[0], res[1], res[-1]
    thru_vals = res[2:-1]

    def wait():
        def wait_body(*refs):
            src_refs, land_refs = refs[:ns], refs[ns:ns + nl]
            s_sems, r_sems = refs[ns + nl], refs[ns + nl + 1]
            for k, (src, _, dst, peer) in enumerate(plan(src_refs, land_refs)):
                cp = pltpu.make_async_remote_copy(src_ref=src, dst_ref=dst, send_sem=s_sems.at[k], recv_sem=r_sems.at[k],
                                                  device_id=peer, device_id_type=MESH)
                cp.wait_send()
                cp.wait_recv()

        out = pl.pallas_call(
            wait_body, name=name + "_wait", out_shape=thru,
            in_specs=[_HBM] * (ns + nl) + [_SEM, _SEM, _ANY], out_specs=[_HBM] * (ns + nl),
            input_output_aliases={i: i for i in range(ns + nl)},
            compiler_params=pltpu.CompilerParams(has_side_effects=_EFFECT),
        )(*thru_vals, send_sems, recv_sems, wait_after())
        return list(out[:ns]), list(out[ns:])

    return wait, token


def _gather_direct_plan(n):
    def plan(src_refs, land_refs):
        me = _position()
        peers = [_flip(me, k) for k in range(1, N_DEV)]
        return [(src_refs[a], land_refs[a].at[_index(me)], land_refs[a].at[_index(p)], p)
                for a in range(n) for p in peers]
    return plan


def _place_own(shards):
    me = _index(_position())
    return [lax.dynamic_update_slice(lax.empty((N_DEV,) + s.shape, s.dtype), s[None], (me, 0, 0)) for s in shards]


def _exchange_direct_plan(n):
    def plan(src_refs, land_refs):
        me = _position()
        peers = [_flip(me, k) for k in range(1, N_DEV)]
        return [(src_refs[a].at[_index(p)], land_refs[a].at[_index(me)], land_refs[a].at[_index(p)], p)
                for a in range(n) for p in peers]
    return plan


def _adamw(w, g, m, v):
    m = ADAM_B1 * m + (1.0 - ADAM_B1) * g
    v = ADAM_B2 * v + (1.0 - ADAM_B2) * (g * g)
    m_hat = m / (1.0 - ADAM_B1 ** ADAM_STEP)
    v_hat = v / (1.0 - ADAM_B2 ** ADAM_STEP)
    delta = -ADAM_LR * (m_hat / (jnp.sqrt(v_hat) + ADAM_EPS) + ADAM_WD * w)
    return delta, m, v


def _row_tile(rows, unit, cap=256):
    best = rows
    for t in range(unit, min(rows, cap) + 1, unit):
        if rows % t == 0:
            best = t
    return best


def _sum_partials(name, recv):
    n_src, R, C = recv.shape
    T = _row_tile(R, 16)

    def body(r_ref, g_out):
        g = r_ref[0].astype(F32)
        for s in range(1, n_src):
            g = g + r_ref[s].astype(F32)
        g_out[...] = g

    return pl.pallas_call(
        body, grid=(R // T,), in_specs=[pl.BlockSpec((n_src, T, C), lambda i: (0, i, 0))],
        out_specs=pl.BlockSpec((T, C), lambda i: (i, 0)), out_shape=jax.ShapeDtypeStruct((R, C), F32),
        compiler_params=_params("parallel"), name=name)(recv)


def _adamw_update(name, w, g, m, v):
    R, C = w.shape
    T = _row_tile(R, 8)

    def body(w_ref, g_ref, m_ref, v_ref, d_out, m_out, v_out):
        delta, mn, vn = _adamw(w_ref[...], g_ref[...], m_ref[...], v_ref[...])
        d_out[...] = delta
        m_out[...] = mn
        v_out[...] = vn

    blk = pl.BlockSpec((T, C), lambda i: (i, 0))
    return pl.pallas_call(
        body, grid=(R // T,), in_specs=[blk] * 4, out_specs=[blk] * 3,
        out_shape=[jax.ShapeDtypeStruct((R, C), F32)] * 3, compiler_params=_params("parallel"), name=name)(w, g, m, v)


def _gather_vmem(p_ref, gath, send_sems, recv_sems):
    me = _position()
    gath[_index(me)] = p_ref[...]
    copies = []
    for k in range(1, N_DEV):
        peer = _flip(me, k)
        copies.append(pltpu.make_async_remote_copy(
            src_ref=p_ref, dst_ref=gath.at[_index(me)], send_sem=send_sems.at[k - 1],
            recv_sem=recv_sems.at[k - 1], device_id=peer, device_id_type=MESH))
    for cp in copies:
        cp.start()
    for k in range(1, N_DEV):
        peer = _flip(me, k)
        pltpu.make_async_remote_copy(
            src_ref=p_ref, dst_ref=gath.at[_index(peer)], send_sem=send_sems.at[k - 1],
            recv_sem=recv_sems.at[k - 1], device_id=peer, device_id_type=MESH).wait_recv()
    for cp in copies:
        cp.wait_send()


def _all_gather_small(p):
    R, C = p.shape

    def body(p_ref, out_ref, send_sems, recv_sems):
        _gather_vmem(p_ref, out_ref, send_sems, recv_sems)

    vm = pl.BlockSpec(memory_space=pltpu.VMEM)
    return pl.pallas_call(
        body, in_specs=[vm], out_specs=vm, out_shape=jax.ShapeDtypeStruct((N_DEV, R, C), F32),
        scratch_shapes=[pltpu.SemaphoreType.DMA((7,)), pltpu.SemaphoreType.DMA((7,))],
        name="all_gather_small")(p)


def _small_allreduce_adamw(part, w, m, v):
    R, C = part.shape

    def body(p_ref, w_ref, m_ref, v_ref, g_out, d_out, m_out, v_out, gath, send_sems, recv_sems):
        _gather_vmem(p_ref, gath, send_sems, recv_sems)
        g = gath[0]
        for s in range(1, N_DEV):
            g = g + gath[s]
        delta, mn, vn = _adamw(w_ref[...], g, m_ref[...], v_ref[...])
        g_out[...] = g
        d_out[...] = delta
        m_out[...] = mn
        v_out[...] = vn

    vm = pl.BlockSpec(memory_space=pltpu.VMEM)
    return pl.pallas_call(
        body, in_specs=[vm] * 4, out_specs=[vm] * 4, out_shape=[jax.ShapeDtypeStruct((R, C), F32)] * 4,
        scratch_shapes=[pltpu.VMEM((N_DEV, R, C), F32), pltpu.SemaphoreType.DMA((7,)), pltpu.SemaphoreType.DMA((7,))],
        name="small_allreduce_adamw")(part, w, m, v)


_BIG = ("w_in", "w_up", "w_down", "w_out", "w_pool_out", "w_attn_out")
_TRANSPOSED = ("w_in", "w_up", "w_pool_out", "w_attn_out")
_SMALL = (("g_mix", 2048), ("b_gate", 4096), ("pool_scale", 1024), ("g_ffn", 2048), ("conv_b", 11264), ("g_final", 2048))
CONV_ROWS = 8


def _to_comm(name, shard):
    return shard.T if name in _TRANSPOSED else shard


def _pack_small(vals):
    flat = [vals[n].reshape(-1) for n, _ in _SMALL]
    used = sum(f.shape[0] for f in flat)
    flat.append(jnp.zeros((SMALL_ROWS * PACK_COLS - used,), F32))
    return jnp.concatenate(flat).reshape(SMALL_ROWS, PACK_COLS)


def _unpack_small(packed, like):
    flat = packed.reshape(-1)
    out, off = {}, 0
    for name, n in _SMALL:
        out[name] = flat[off:off + n].reshape(like[name].shape)
        off += n
    return out


def _local_step(x, target, W, small, first_token=0.0, late_weights=None, emit=None):
    S = x.shape[0]
    o_qkv, o_gate = POOL_WIDTH, POOL_WIDTH + QKV_WIDTH
    if emit is None:
        emit = lambda grads: None

    def zero(token):
        return 0.0 if token is None else token[0, 0]
    w_in_t = W["w_in"]

    h1 = _rms_fwd("rms1_fwd", x, small["g_mix"] + first_token)
    u = _mm_nt("proj_u", h1, w_in_t, tm=1024, tn=512, tk=2048, o_dtype=F32, n_rows=POOL_WIDTH, b_row_off=0)
    qkv = _mm_nt("proj_qkv", h1, w_in_t, tm=1024, tn=512, tk=2048, o_dtype=BF16, n_rows=QKV_WIDTH, b_row_off=o_qkv)
    gates_pre = _mm_nt("proj_gates", h1, w_in_t, tm=1024, tn=512, tk=2048, o_dtype=F32, n_rows=2 * D_MODEL,
                       b_row_off=o_gate)
    pool_y = _pool_fwd(u, W["w_pool_lin"], small["pool_scale"])
    outs, lses = zip(*[_attn_fwd(gi, qkv) for gi in range(3)])
    attn_y = _merge_fwd(outs, lses)
    if late_weights is not None:
        W = dict(W, **late_weights(0, attn_y))
    w_up_t = W["w_up"]
    mixed, y_pool, y_attn = _mix_fwd(pool_y, attn_y, W["w_pool_out"], W["w_attn_out"], gates_pre, small["b_gate"])
    x1 = _mm_nn("out_proj", mixed, W["w_out"], tm=1024, tn=1024, tk=2048, o_dtype=F32, res=x)

    h2 = _rms_fwd("rms2_fwd", x1, small["g_ffn"])
    nj = D_FF // 512
    up_pre, act = _up_proj_convffn(h2, w_up_t, small["conv_w_full"], small["conv_b"])
    if late_weights is not None:
        W = dict(W, **late_weights(1, act))
    x2 = _mm_nn("down_proj", act, W["w_down"], tm=1024, tn=1024, tk=D_FF // 2, o_dtype=F32, res=x1)

    dx2, dx2_b, dg_final, sq_cols = _final_loss("final_loss", x2, small["g_final"], target)

    dw_down = _mm_tn("dw_down", act, dx2_b, tm=512, tn=2048, tk=TOKEN_TK)
    tok = emit(dict(w_down=dw_down))
    dup, dconv_w, dconv_b = _convffn_bwd_a(up_pre, dx2_b, W["w_down"], small["conv_w_full"],
                                           small["conv_b"] + zero(tok))
    dup_pre, dh2 = _convffn_bwd_b(dup, small["conv_w_full"], w_up_t)
    dw_up = _mm("dw_up", dup_pre, h2, mode="tn", grid=(2 * nj, 1, S // TOKEN_TK),
                a_spec=pl.BlockSpec((None, TOKEN_TK, 512), lambda i, j, k: (i // nj, k, i % nj)),
                b_spec=pl.BlockSpec((TOKEN_TK, D_MODEL), lambda i, j, k: (k, 0)),
                o_spec=pl.BlockSpec((512, D_MODEL), lambda i, j, k: (i, 0)),
                o_shape=(2 * D_FF, D_MODEL), o_dtype=BF16, acc_shape=(512, D_MODEL))
    tok = emit(dict(w_up=dw_up, conv_w=dconv_w))
    dx1, dx1_b, dg_ffn = _rms_bwd("rms2_bwd", x1, small["g_ffn"] + zero(tok), dh2, dx2)

    dmixed = _mm_nt("d_mixed", dx1_b, W["w_out"], tm=1024, tn=1024, tk=2048, o_dtype=BF16)
    dw_out = _mm_tn("dw_out", mixed, dx1_b, tm=512, tn=2048, tk=TOKEN_TK)
    tok = emit(dict(w_out=dw_out))
    dy_pool, dy_attn, dgates, db_gate = _mix_bwd(dmixed, gates_pre, small["b_gate"] + zero(tok), y_pool, y_attn)
    dpool_y = _mm_nn("d_pool_y", dy_pool, W["w_pool_out"], tm=1024, tn=1024, tk=2048, o_dtype=F32)
    dw_pool_out = _mm_tn("dw_pool_out", dy_pool, pool_y, tm=1024, tn=1024, tk=TOKEN_TK)
    dattn_y = _mm_nn("d_attn_y", dy_attn, W["w_attn_out"], tm=1024, tn=512, tk=2048, o_dtype=F32)
    dw_attn_out = _mm_tn("dw_attn_out", dy_attn, attn_y, tm=1024, tn=512, tk=TOKEN_TK)
    e, dw_pool_lin, dpool_scale = _pool_bwd_a(u, dpool_y, W["w_pool_lin"], small["pool_scale"])
    du = _pool_bwd_b(e)
    dos, dps = _merge_bwd(dattn_y, outs, lses)
    dqkv = [_attn_bwd(gi, qkv, dos[gi], lses[gi], dps[gi]) for gi in range(3)]
    dproj = jnp.concatenate([du] + [dqkv[gi][sec] for sec in range(3) for gi in range(3)] + [dgates], axis=1)
    dw_in = _mm_tn("dw_in", dproj, h1, tm=512, tn=2048, tk=TOKEN_TK)
    tok = emit(dict(w_in=dw_in, w_pool_out=dw_pool_out, w_attn_out=dw_attn_out, w_pool_lin=dw_pool_lin))
    dh1 = _mm_nn("d_h1", dproj, w_in_t, tm=1024, tn=1024, tk=IN_WIDTH // 4, o_dtype=F32,
                 after=tok)
    grad_x, _, dg_mix = _rms_bwd("rms1_bwd", x, small["g_mix"] + zero(tok), dh1, dx1)

    grads = dict(w_in=dw_in, w_up=dw_up, w_down=dw_down, w_out=dw_out, w_pool_out=dw_pool_out,
                 w_attn_out=dw_attn_out, w_pool_lin=dw_pool_lin, conv_w=dconv_w)
    small_grads = dict(g_mix=dg_mix, b_gate=db_gate, pool_scale=dpool_scale, g_ffn=dg_ffn, conv_b=dconv_b,
                       g_final=dg_final)
    return sq_cols, grad_x, grads, small_grads


def kernel(x, g_mix, w_in, b_gate, w_pool_lin, pool_scale, w_pool_out, w_attn_out, w_out, g_ffn, w_up, conv_w, conv_b, w_down, g_final, loss_target, m_g_mix, m_w_in, m_b_gate, m_w_pool_lin, m_pool_scale, m_w_pool_out, m_w_attn_out, m_w_out, m_g_ffn, m_w_up, m_conv_w, m_conv_b, m_w_down, m_g_final, v_g_mix, v_w_in, v_b_gate, v_w_pool_lin, v_pool_scale, v_w_pool_out, v_w_attn_out, v_w_out, v_g_ffn, v_w_up, v_conv_w, v_conv_b, v_w_down, v_g_final):
    given = dict(g_mix=g_mix, w_in=w_in, b_gate=b_gate, w_pool_lin=w_pool_lin, pool_scale=pool_scale,
                 w_pool_out=w_pool_out, w_attn_out=w_attn_out, w_out=w_out, g_ffn=g_ffn, w_up=w_up, conv_w=conv_w,
                 conv_b=conv_b, w_down=w_down, g_final=g_final)
    mom_m = dict(g_mix=m_g_mix, w_in=m_w_in, b_gate=m_b_gate, w_pool_lin=m_w_pool_lin, pool_scale=m_pool_scale,
                 w_pool_out=m_w_pool_out, w_attn_out=m_w_attn_out, w_out=m_w_out, g_ffn=m_g_ffn, w_up=m_w_up,
                 conv_w=m_conv_w, conv_b=m_conv_b, w_down=m_w_down, g_final=m_g_final)
    mom_v = dict(g_mix=v_g_mix, w_in=v_w_in, b_gate=v_b_gate, w_pool_lin=v_w_pool_lin, pool_scale=v_pool_scale,
                 w_pool_out=v_w_pool_out, w_attn_out=v_w_attn_out, w_out=v_w_out, g_ffn=v_g_ffn, w_up=v_w_up,
                 conv_w=v_conv_w, conv_b=v_conv_b, w_down=v_w_down, g_final=v_g_final)
    def full(g):
        return g.reshape(N_DEV * g.shape[1], g.shape[2])

    shard = {n: _to_comm(n, given[n][0]).astype(BF16) for n in _BIG}
    W = dict(w_in=full(_all_gather_weights([shard["w_in"]])[0]))
    lin_all = _all_gather_small(w_pool_lin[0].reshape(4 * 32, POOL_GROUP_WIDTH))
    W["w_pool_lin"] = jnp.moveaxis(lin_all.reshape(N_DEV, 4, 32, POOL_GROUP_WIDTH), 0, 1).reshape(
        4, POOL_GROUP_WIDTH, POOL_GROUP_WIDTH).astype(BF16)
    conv_w_all = _all_gather_small(jnp.pad(conv_w[0], ((0, CONV_ROWS - 3), (0, 0))))
    conv_w_full = jnp.moveaxis(conv_w_all[:, :3, :], 0, 1).reshape(3, 2 * D_FF)

    stages = (("w_out", "w_pool_out", "w_attn_out", "w_up"), ("w_down",))
    stage_after = [None, None]
    stage_wait = []
    order_after = (W["w_in"], lin_all, conv_w_all)
    for k, names in enumerate(stages):
        srcs = [shard[n] for n in names]
        wait, token = _split_copies(f"gather_stage{k}", srcs, _place_own(srcs), 7 * len(names),
                                    _gather_direct_plan(len(names)), start_after=order_after,
                                    wait_after=lambda k=k: stage_after[k])
        stage_wait.append(wait)
        order_after = (token,)
    rest_token = token

    def late_weights(stage, after):
        stage_after[stage] = after
        return {n: full(l) for n, l in zip(stages[stage], stage_wait[stage]()[1])}

    small = dict(g_mix=g_mix, b_gate=b_gate, pool_scale=pool_scale, g_ffn=g_ffn, conv_b=conv_b,
                 g_final=g_final.reshape(1, D_MODEL), conv_w_full=conv_w_full)

    pending = []
    done = {}

    def emit(group):
        names = list(group)
        parts = []
        for n in names:
            g = group[n]
            if n == "w_pool_lin":
                g = jnp.moveaxis(g.reshape(4, N_DEV, 32, POOL_GROUP_WIDTH), 1, 0).reshape(
                    N_DEV, 4 * 32, POOL_GROUP_WIDTH).astype(BF16)
            elif n == "conv_w":
                g = jnp.pad(jnp.moveaxis(g.reshape(3, N_DEV, 2 * D_FF // N_DEV), 1, 0),
                            ((0, 0), (0, CONV_ROWS - 3), (0, 0))).astype(BF16)
            else:
                g = g.reshape(N_DEV, g.shape[0] // N_DEV, g.shape[1])
            parts.append(g)
        me = _index(_position())
        lands = [lax.dynamic_update_slice(lax.empty(p.shape, p.dtype), lax.dynamic_index_in_dim(p, me, 0),
                                          (me, 0, 0)) for p in parts]
        wait, token = _split_copies(
            "exchange_" + names[0], parts, lands, 7 * len(names), _exchange_direct_plan(len(names)),
            start_after=(), wait_after=lambda: done["grad_x"])
        pending.append((names, wait))
        return token

    sq_cols, grad_x, _, small_grads = _local_step(x[0], loss_target[0], W, small, first_token=rest_token[0, 0],
                                                  late_weights=late_weights, emit=emit)
    done["grad_x"] = grad_x
    loss = lax.psum(0.5 / D_MODEL * jnp.sum(sq_cols), AXES)

    res = {}
    for n, r in [(n, r) for names, wait in pending for n, r in zip(names, wait()[1])]:
        g = _sum_partials("sum_" + n, r)
        if n in _TRANSPOSED:
            g = g.T
        elif n == "conv_w":
            g = g[:3]
        shape = given[n].shape
        rows = math.prod(shape[:-1])
        w2, m2, v2 = (d[n].reshape(rows, shape[-1]) for d in (given, mom_m, mom_v))
        upd = _adamw_update("adamw_" + n, w2, g.reshape(rows, shape[-1]), m2, v2)
        res[n] = [a.reshape(shape) for a in (g,) + tuple(upd)]

    packed_small = [_pack_small(d) for d in (given, mom_m, mom_v)]
    res_small = [_unpack_small(p, given) for p in _small_allreduce_adamw(_pack_small(small_grads), *packed_small)]

    order = ["g_mix", "w_in", "b_gate", "w_pool_lin", "pool_scale", "w_pool_out", "w_attn_out", "w_out", "g_ffn",
             "w_up", "conv_w", "conv_b", "w_down", "g_final"]
    outs = [loss, grad_x[None]]
    for kind in range(4):
        for n in order:
            outs.append(res[n][kind] if n in res else res_small[kind][n])
    return tuple(outs)
```

```python
import math

import jax
import jax.numpy as jnp
from jax import lax
from jax.experimental import pallas as pl
from jax.experimental.pallas import tpu as pltpu

F32 = jnp.float32
BF16 = jnp.bfloat16
MESH = pl.DeviceIdType.MESH
AXES = ("x", "y", "c")
N_DEV = 8

D_MODEL = 2048
POOL_WINDOWS = (2, 4, 8, 16)
POOL_GROUP_WIDTH = 256
POOL_WIDTH = 1024
ATTN_GROUPS = ((128, 1), (512, 4), (2048, 16))
SPAN = 128
HEADS_PER_GROUP = 4
N_ATTN_HEADS = 12
HEAD_DIM = 128
ATTN_WIDTH = 1536
QKV_WIDTH = 3 * ATTN_WIDTH
ATTN_OUT_WIDTH = 512
IN_WIDTH = 9728
D_FF = 5632
RMS_EPS = 1e-6
ADAM_LR, ADAM_B1, ADAM_B2, ADAM_EPS, ADAM_WD, ADAM_STEP = 0.001, 0.9, 0.999, 1e-08, 0.01, 10

VMEM_LIMIT_BYTES = 48 * 1024 * 1024
HALO = 16
TOKEN_TK = 2048
MXU_DIM = 256
PACK_COLS = 1024
SMALL_ROWS = 24
NEG_BIG = -1e30

_DN = {"nn": (((1,), (0,)), ((), ())), "nt": (((1,), (1,)), ((), ())), "tn": (((0,), (0,)), ((), ()))}


def _params(*sem):
    return pltpu.CompilerParams(dimension_semantics=sem, vmem_limit_bytes=VMEM_LIMIT_BYTES)


def _dot(a, b, mode):
    return lax.dot_general(a.astype(BF16), b.astype(BF16), _DN[mode], preferred_element_type=F32)


def _mm(name, a, b, *, mode, grid, a_spec, b_spec, o_spec, o_shape, o_dtype, acc_shape, res=None, res_spec=None,
        after=None):
    nk = grid[2]
    n_in = 2 + (res is not None) + (after is not None)

    def body(*refs):
        a_ref, b_ref = refs[:2]
        r_ref = refs[2] if res is not None else None
        o_ref = refs[n_in]
        scr = refs[n_in + 1:]
        prod = _dot(a_ref[...], b_ref[...], mode)

        def finish(val):
            if r_ref is not None:
                val = val + r_ref[...]
            o_ref[...] = val.astype(o_ref.dtype)

        if nk == 1:
            finish(prod)
        else:
            acc = scr[0]
            k = pl.program_id(2)

            @pl.when(k == 0)
            def _():
                acc[...] = prod

            @pl.when(k > 0)
            def _():
                acc[...] += prod

            @pl.when(k == nk - 1)
            def _():
                finish(acc[...])

    in_specs = [a_spec, b_spec]
    args = [a, b]
    if res is not None:
        in_specs.append(res_spec)
        args.append(res)
    if after is not None:
        in_specs.append(pl.BlockSpec((8, 128), lambda i, j, k: (0, 0)))
        args.append(after)
    return pl.pallas_call(
        body, grid=grid, in_specs=in_specs, out_specs=o_spec,
        out_shape=jax.ShapeDtypeStruct(o_shape, o_dtype),
        scratch_shapes=[] if nk == 1 else [pltpu.VMEM(acc_shape, F32)],
        compiler_params=_params("parallel", "parallel", "arbitrary"), name=name)(*args)


def _mm_nn(name, a, b, *, tm, tn, tk, o_dtype, n_cols=None, b_col_off=0, res=None, after=None):
    M, K = a.shape
    N = b.shape[1] if n_cols is None else n_cols
    off = b_col_off // tn
    return _mm(name, a, b, mode="nn", grid=(M // tm, N // tn, K // tk),
               a_spec=pl.BlockSpec((tm, tk), lambda i, j, k: (i, k)),
               b_spec=pl.BlockSpec((tk, tn), lambda i, j, k: (k, j + off)),
               o_spec=pl.BlockSpec((tm, tn), lambda i, j, k: (i, j)),
               o_shape=(M, N), o_dtype=o_dtype, acc_shape=(tm, tn), res=res,
               res_spec=pl.BlockSpec((tm, tn), lambda i, j, k: (i, j)), after=after)


def _mm_nt(name, a, b, *, tm, tn, tk, o_dtype, n_rows=None, b_row_off=0):
    M, K = a.shape
    N = b.shape[0] if n_rows is None else n_rows
    off = b_row_off // tn
    return _mm(name, a, b, mode="nt", grid=(M // tm, N // tn, K // tk),
               a_spec=pl.BlockSpec((tm, tk), lambda i, j, k: (i, k)),
               b_spec=pl.BlockSpec((tn, tk), lambda i, j, k: (j + off, k)),
               o_spec=pl.BlockSpec((tm, tn), lambda i, j, k: (i, j)),
               o_shape=(M, N), o_dtype=o_dtype, acc_shape=(tm, tn))


def _mm_tn(name, a, b, *, tm, tn, tk):
    K, M = a.shape
    N = b.shape[1]
    return _mm(name, a, b, mode="tn", grid=(M // tm, N // tn, K // tk),
               a_spec=pl.BlockSpec((tk, tm), lambda i, j, k: (k, i)),
               b_spec=pl.BlockSpec((tk, tn), lambda i, j, k: (k, j)),
               o_spec=pl.BlockSpec((tm, tn), lambda i, j, k: (i, j)),
               o_shape=(M, N), o_dtype=BF16, acc_shape=(tm, tn))


def _rms_fwd(name, x, g):
    S, D = x.shape
    T = 512

    def body(x_ref, g_ref, h_ref):
        xv = x_ref[...]
        r = lax.rsqrt(jnp.mean(xv * xv, axis=-1, keepdims=True) + RMS_EPS)
        h_ref[...] = (xv * r * g_ref[...]).astype(BF16)

    return pl.pallas_call(
        body, grid=(S // T,),
        in_specs=[pl.BlockSpec((T, D), lambda i: (i, 0)), pl.BlockSpec((1, D), lambda i: (0, 0))],
        out_specs=pl.BlockSpec((T, D), lambda i: (i, 0)),
        out_shape=jax.ShapeDtypeStruct((S, D), BF16), compiler_params=_params("parallel"), name=name)(x, g)


def _rms_bwd(name, x, g, dh, dres):
    S, D = x.shape
    T = 256

    def body(x_ref, g_ref, dh_ref, dres_ref, dx_ref, dxb_ref, dg_ref):
        i = pl.program_id(0)
        xv = x_ref[...]
        r = lax.rsqrt(jnp.mean(xv * xv, axis=-1, keepdims=True) + RMS_EPS)
        xhat = xv * r
        dhv = dh_ref[...].astype(F32)
        gdh = dhv * g_ref[...]
        c = jnp.mean(xhat * gdh, axis=-1, keepdims=True)
        dx = dres_ref[...] + r * (gdh - xhat * c)
        dx_ref[...] = dx
        dxb_ref[...] = dx.astype(BF16)
        part = jnp.sum(dhv * xhat, axis=0, keepdims=True)

        @pl.when(i == 0)
        def _():
            dg_ref[...] = part

        @pl.when(i > 0)
        def _():
            dg_ref[...] += part

    row = pl.BlockSpec((T, D), lambda i: (i, 0))
    vec = pl.BlockSpec((1, D), lambda i: (0, 0))
    return pl.pallas_call(
        body, grid=(S // T,), in_specs=[row, vec, row, row], out_specs=[row, row, vec],
        out_shape=[jax.ShapeDtypeStruct((S, D), F32), jax.ShapeDtypeStruct((S, D), BF16),
                   jax.ShapeDtypeStruct((1, D), F32)],
        compiler_params=_params("arbitrary"), name=name)(x, g, dh, dres)


def _final_loss(name, x, g, target):
    S, D = x.shape
    T = 256

    def body(x_ref, g_ref, t_ref, dx_ref, dxb_ref, dg_ref, sq_ref):
        i = pl.program_id(0)
        xv = x_ref[...]
        r = lax.rsqrt(jnp.mean(xv * xv, axis=-1, keepdims=True) + RMS_EPS)
        xhat = xv * r
        err = xhat * g_ref[...] - t_ref[...]
        dy = err * (1.0 / D)
        gdy = dy * g_ref[...]
        c = jnp.mean(xhat * gdy, axis=-1, keepdims=True)
        dx = r * (gdy - xhat * c)
        dx_ref[...] = dx
        dxb_ref[...] = dx.astype(BF16)
        dg_part = jnp.sum(dy * xhat, axis=0, keepdims=True)
        sq_part = jnp.sum(err * err, axis=0, keepdims=True)

        @pl.when(i == 0)
        def _():
            dg_ref[...] = dg_part
            sq_ref[...] = sq_part

        @pl.when(i > 0)
        def _():
            dg_ref[...] += dg_part
            sq_ref[...] += sq_part

    row = pl.BlockSpec((T, D), lambda i: (i, 0))
    vec = pl.BlockSpec((1, D), lambda i: (0, 0))
    return pl.pallas_call(
        body, grid=(S // T,), in_specs=[row, vec, row], out_specs=[row, row, vec, vec],
        out_shape=[jax.ShapeDtypeStruct((S, D), F32), jax.ShapeDtypeStruct((S, D), BF16),
                   jax.ShapeDtypeStruct((1, D), F32), jax.ShapeDtypeStruct((1, D), F32)],
        compiler_params=_params("arbitrary"), name=name)(x, g, target)


def _prev_halo_spec(T, C, col_map):
    per = T // HALO
    return pl.BlockSpec((HALO, C), lambda *g: (jnp.maximum(g[0] * per - 1, 0), col_map(*g)))


def _pool_window_sums(ext, T, i):
    t = i * T + lax.broadcasted_iota(jnp.int32, (T, 1), 0)
    outs = []
    cnts = []
    for gi, w in enumerate(POOL_WINDOWS):
        s = ext[:, gi * POOL_GROUP_WIDTH:(gi + 1) * POOL_GROUP_WIDTH]
        sh = 1
        while sh < w:
            s = s + pltpu.roll(s, sh, 0)
            sh *= 2
        cnt = jnp.minimum(t + 1, w).astype(F32)
        outs.append(s[HALO:] / cnt - ext[HALO:, gi * POOL_GROUP_WIDTH:(gi + 1) * POOL_GROUP_WIDTH])
        cnts.append(cnt)
    return outs, cnts


def _load_ext(ext_ref, halo_ref, blk_ref, first):
    ext_ref[0:HALO, :] = jnp.where(first, 0.0, halo_ref[...].astype(F32))
    ext_ref[HALO:, :] = blk_ref[...].astype(F32)


def _pool_fwd(u, w_lin, scale):
    S = u.shape[0]
    T = 512
    W = POOL_WIDTH

    def body(u_ref, halo_ref, wl_ref, sc_ref, y_ref, ext_ref):
        i = pl.program_id(0)
        _load_ext(ext_ref, halo_ref, u_ref, i == 0)
        pooled, _ = _pool_window_sums(ext_ref[...], T, i)
        for gi in range(4):
            z = _dot(pooled[gi], wl_ref[gi], "nn")
            cs = slice(gi * POOL_GROUP_WIDTH, (gi + 1) * POOL_GROUP_WIDTH)
            y_ref[:, cs] = (z * sc_ref[:, cs]).astype(BF16)

    return pl.pallas_call(
        body, grid=(S // T,),
        in_specs=[pl.BlockSpec((T, W), lambda i: (i, 0)), _prev_halo_spec(T, W, lambda i: 0),
                  pl.BlockSpec((4, 256, 256), lambda i: (0, 0, 0)), pl.BlockSpec((1, W), lambda i: (0, 0))],
        out_specs=pl.BlockSpec((T, W), lambda i: (i, 0)),
        out_shape=jax.ShapeDtypeStruct((S, W), BF16),
        scratch_shapes=[pltpu.VMEM((HALO + T, W), F32)],
        compiler_params=_params("parallel"), name="pool_fwd")(u, u, w_lin, scale)


def _pool_bwd_a(u, dy, w_lin, scale):
    S = u.shape[0]
    T = 512
    W = POOL_WIDTH

    def body(u_ref, halo_ref, dy_ref, wl_ref, sc_ref, e_ref, dwl_ref, dsc_ref, ext_ref):
        i = pl.program_id(0)
        _load_ext(ext_ref, halo_ref, u_ref, i == 0)
        pooled, cnts = _pool_window_sums(ext_ref[...], T, i)
        for gi in range(4):
            cs = slice(gi * POOL_GROUP_WIDTH, (gi + 1) * POOL_GROUP_WIDTH)
            z = _dot(pooled[gi], wl_ref[gi], "nn")
            dyg = dy_ref[:, cs]
            dsc = jnp.sum(dyg * z, axis=0, keepdims=True)
            dz = dyg * sc_ref[:, cs]
            dwl = _dot(pooled[gi], dz, "tn")
            dpooled = _dot(dz, wl_ref[gi], "nt")
            e_ref[:, cs] = dpooled / cnts[gi]

            @pl.when(i == 0)
            def _():
                dwl_ref[gi] = dwl
                dsc_ref[:, cs] = dsc

            @pl.when(i > 0)
            def _():
                dwl_ref[gi] += dwl
                dsc_ref[:, cs] += dsc

    return pl.pallas_call(
        body, grid=(S // T,),
        in_specs=[pl.BlockSpec((T, W), lambda i: (i, 0)), _prev_halo_spec(T, W, lambda i: 0),
                  pl.BlockSpec((T, W), lambda i: (i, 0)),
                  pl.BlockSpec((4, 256, 256), lambda i: (0, 0, 0)), pl.BlockSpec((1, W), lambda i: (0, 0))],
        out_specs=[pl.BlockSpec((T, W), lambda i: (i, 0)), pl.BlockSpec((4, 256, 256), lambda i: (0, 0, 0)),
                   pl.BlockSpec((1, W), lambda i: (0, 0))],
        out_shape=[jax.ShapeDtypeStruct((S, W), F32), jax.ShapeDtypeStruct((4, 256, 256), F32),
                   jax.ShapeDtypeStruct((1, W), F32)],
        scratch_shapes=[pltpu.VMEM((HALO + T, W), F32)],
        compiler_params=_params("arbitrary"), name="pool_bwd_a")(u, u, dy, w_lin, scale)


def _pool_bwd_b(e):
    S = e.shape[0]
    T = 512
    W = POOL_WIDTH
    per = T // HALO
    n_blk = S // T

    def body(e_ref, halo_ref, du_ref, ext_ref):
        i = pl.program_id(0)
        ext_ref[0:T, :] = e_ref[...]
        ext_ref[T:, :] = jnp.where(i == n_blk - 1, 0.0, halo_ref[...])
        ext = ext_ref[...]
        n = T + HALO
        t = i * T + lax.broadcasted_iota(jnp.int32, (T, 1), 0)
        for gi, w in enumerate(POOL_WINDOWS):
            cs = slice(gi * POOL_GROUP_WIDTH, (gi + 1) * POOL_GROUP_WIDTH)
            s = ext[:, cs]
            sh = 1
            while sh < w:
                s = s + pltpu.roll(s, n - sh, 0)
                sh *= 2
            cnt = jnp.minimum(t + 1, w).astype(F32)
            du_ref[:, cs] = (s[0:T] - ext[0:T, cs] * cnt).astype(BF16)

    return pl.pallas_call(
        body, grid=(n_blk,),
        in_specs=[pl.BlockSpec((T, W), lambda i: (i, 0)),
                  pl.BlockSpec((HALO, W), lambda i: (jnp.minimum((i + 1) * per, S // HALO - 1), 0))],
        out_specs=pl.BlockSpec((T, W), lambda i: (i, 0)),
        out_shape=jax.ShapeDtypeStruct((S, W), BF16),
        scratch_shapes=[pltpu.VMEM((T + HALO, W), F32)],
        compiler_params=_params("parallel"), name="pool_bwd_b")(e, e)


def _slope(head):
    return 2.0 ** (-8.0 * (head + 1) / N_ATTN_HEADS)


def _attn_masks(b):
    row = lax.broadcasted_iota(jnp.int32, (SPAN, SPAN), 0)
    col = lax.broadcasted_iota(jnp.int32, (SPAN, SPAN), 1)
    dist_cur = (row - col).astype(F32)
    valid_cur = col <= row
    valid_prev = jnp.logical_and(col >= row, b > 0)
    return dist_cur, valid_cur, valid_prev


ATTN_BATCH = 4


def _attn_tiling(gi, S):
    d = ATTN_GROUPS[gi][1]
    hp = HEADS_PER_GROUP if d < 16 else 1
    return d, SPAN * d, S // (SPAN * d), hp, hp * HEAD_DIM, ATTN_BATCH // hp


_BATCHED_DN = {"nt": (((2,), (2,)), ((0,), (0,))), "nn": (((2,), (1,)), ((0,), (0,))), "tn": (((1,), (1,)), ((0,), (0,)))}


def _bdot(a, b, mode):
    return lax.dot_general(a.astype(BF16), b.astype(BF16), _BATCHED_DN[mode], preferred_element_type=F32)


def _batch_rows(it, d, rb):
    return [pl.ds(it * rb + j, SPAN, stride=d) for j in range(rb)]


def _take(scr, hp, rows):
    return jnp.stack([scr[hi, rw, :] for hi in range(hp) for rw in rows])


def _put(scr, hp, rows, val):
    for n, (hi, rw) in enumerate((hi, rw) for hi in range(hp) for rw in rows):
        scr[hi, rw, :] = val[n]


def _batch_bias(gi, d, hp, rb, dist):
    return jnp.stack([(-(_head_slope(gi, hp, hi) * d)) * dist for hi in range(hp) for _ in range(rb)])


def _head_slope(gi, hp, hi):
    if hp == HEADS_PER_GROUP:
        return _slope(gi * HEADS_PER_GROUP + hi)
    first = pl.program_id(1) * hp
    sl = jnp.float32(_slope(gi * HEADS_PER_GROUP + hi))
    for h0 in range(hp, HEADS_PER_GROUP, hp):
        sl = jnp.where(first == h0, jnp.float32(_slope(gi * HEADS_PER_GROUP + h0 + hi)), sl)
    return sl


def _split_heads(src_ref, dst_ref, hp):
    for hi in range(hp):
        dst_ref[hi] = src_ref[:, hi * HEAD_DIM:(hi + 1) * HEAD_DIM].astype(F32)


def _attn_fwd(gi, qkv):
    S = qkv.shape[0]
    d, C, nb, hp, W, rb = _attn_tiling(gi, S)
    scale = HEAD_DIM ** -0.5

    def body(q_ref, kc_ref, kp_ref, vc_ref, vp_ref, o_ref, lse_ref, qs, kcs, kps, vcs, vps, os, ls):
        b = pl.program_id(0)
        for src, dst in ((q_ref, qs), (kc_ref, kcs), (kp_ref, kps), (vc_ref, vcs), (vp_ref, vps)):
            _split_heads(src, dst, hp)
        dist_cur, valid_cur, valid_prev = _attn_masks(b)
        bias_cur = _batch_bias(gi, d, hp, rb, dist_cur)
        bias_far = _batch_bias(gi, d, hp, rb, dist_cur + SPAN)

        def sub_sequences(it, carry):
            rows = _batch_rows(it, d, rb)
            q = _take(qs, hp, rows)
            s_cur = jnp.where(valid_cur, _bdot(q, _take(kcs, hp, rows), "nt") * scale + bias_cur, NEG_BIG)
            s_prev = jnp.where(valid_prev, _bdot(q, _take(kps, hp, rows), "nt") * scale + bias_far, NEG_BIG)
            m = jnp.maximum(jnp.max(s_cur, axis=-1, keepdims=True), jnp.max(s_prev, axis=-1, keepdims=True))
            p_cur = jnp.exp(s_cur - m)
            p_prev = jnp.exp(s_prev - m)
            l = jnp.sum(p_cur, axis=-1, keepdims=True) + jnp.sum(p_prev, axis=-1, keepdims=True)
            o = (_bdot(p_cur, _take(vcs, hp, rows), "nn") + _bdot(p_prev, _take(vps, hp, rows), "nn")) / l
            _put(os, hp, rows, o)
            _put(ls, hp, rows, jnp.broadcast_to(m + jnp.log(l), o.shape))
            return carry

        lax.fori_loop(0, d // rb, sub_sequences, 0)
        for hi in range(hp):
            hs = slice(hi * HEAD_DIM, (hi + 1) * HEAD_DIM)
            o_ref[:, hs] = os[hi]
            lse_ref[:, hs] = ls[hi]

    per_sec = ATTN_WIDTH // W
    per_grp = ATTN_OUT_WIDTH // W

    def spec(sec, shift):
        return pl.BlockSpec((C, W), lambda b, h: (jnp.maximum(b + shift, 0), sec * per_sec + gi * per_grp + h))

    out = pl.BlockSpec((C, W), lambda b, h: (b, h))
    scr = pltpu.VMEM((hp, C, HEAD_DIM), F32)
    return pl.pallas_call(
        body, grid=(nb, HEADS_PER_GROUP // hp),
        in_specs=[spec(0, 0), spec(1, 0), spec(1, -1), spec(2, 0), spec(2, -1)], out_specs=[out, out],
        out_shape=[jax.ShapeDtypeStruct((S, ATTN_OUT_WIDTH), F32)] * 2, scratch_shapes=[scr] * 7,
        compiler_params=_params("parallel", "parallel"), name=f"attn_fwd_g{gi}")(qkv, qkv, qkv, qkv, qkv)


def _attn_bwd(gi, qkv, do, lse, dp):
    S = qkv.shape[0]
    d, C, nb, hp, W, rb = _attn_tiling(gi, S)
    scale = HEAD_DIM ** -0.5
    n_in = 12

    def body(*refs):
        ins, (dq_ref, dk_ref, dv_ref) = refs[:n_in], refs[n_in:n_in + 3]
        scr = refs[n_in + 3:]
        for src, dst in zip(ins, scr[:n_in]):
            _split_heads(src, dst, hp)
        qs, kcs, vcs, kps, vps, qns, dos, dons, lses, lsens, dps, dpns = scr[:n_in]
        dqs, dks, dvs = scr[n_in:]
        b = pl.program_id(0)
        dist_cur, valid_cur, valid_prev = _attn_masks(b)
        row = lax.broadcasted_iota(jnp.int32, (SPAN, SPAN), 0)
        col = lax.broadcasted_iota(jnp.int32, (SPAN, SPAN), 1)
        valid_next = jnp.logical_and(col >= row, b < nb - 1)

        bias_cur = _batch_bias(gi, d, hp, rb, dist_cur)
        bias_far = _batch_bias(gi, d, hp, rb, dist_cur + SPAN)

        def sub_sequences(it, carry):
            rows = _batch_rows(it, d, rb)
            q, kc, vc, kp, vp, qn, dov, don = (_take(s, hp, rows) for s in (qs, kcs, vcs, kps, vps, qns, dos, dons))
            lse_b, lse_n, dp_b, dp_n = (_take(s, hp, rows) for s in (lses, lsens, dps, dpns))
            p_cur = jnp.where(valid_cur, jnp.exp(_bdot(q, kc, "nt") * scale + bias_cur - lse_b), 0.0)
            p_prev = jnp.where(valid_prev, jnp.exp(_bdot(q, kp, "nt") * scale + bias_far - lse_b), 0.0)
            ds_cur = p_cur * (_bdot(dov, vc, "nt") - dp_b)
            ds_prev = p_prev * (_bdot(dov, vp, "nt") - dp_b)
            _put(dqs, hp, rows, (_bdot(ds_cur, kc, "nn") + _bdot(ds_prev, kp, "nn")) * scale)
            p_next = jnp.where(valid_next, jnp.exp(_bdot(qn, kc, "nt") * scale + bias_far - lse_n), 0.0)
            ds_next = p_next * (_bdot(don, vc, "nt") - dp_n)
            _put(dvs, hp, rows, _bdot(p_cur, dov, "tn") + _bdot(p_next, don, "tn"))
            _put(dks, hp, rows, (_bdot(ds_cur, q, "tn") + _bdot(ds_next, qn, "tn")) * scale)
            return carry

        lax.fori_loop(0, d // rb, sub_sequences, 0)
        for hi in range(hp):
            hs = slice(hi * HEAD_DIM, (hi + 1) * HEAD_DIM)
            dq_ref[:, hs] = dqs[hi].astype(BF16)
            dk_ref[:, hs] = dks[hi].astype(BF16)
            dv_ref[:, hs] = dvs[hi].astype(BF16)

    per_sec = ATTN_WIDTH // W
    per_grp = ATTN_OUT_WIDTH // W

    def qkv_spec(sec, shift):
        return pl.BlockSpec((C, W), lambda b, h: (jnp.clip(b + shift, 0, nb - 1), sec * per_sec + gi * per_grp + h))

    def act_spec(shift):
        return pl.BlockSpec((C, W), lambda b, h: (jnp.clip(b + shift, 0, nb - 1), h))

    out = act_spec(0)
    scr = pltpu.VMEM((hp, C, HEAD_DIM), F32)
    return pl.pallas_call(
        body, grid=(nb, HEADS_PER_GROUP // hp),
        in_specs=[qkv_spec(0, 0), qkv_spec(1, 0), qkv_spec(2, 0), qkv_spec(1, -1), qkv_spec(2, -1), qkv_spec(0, 1),
                  act_spec(0), act_spec(1), act_spec(0), act_spec(1), act_spec(0), act_spec(1)],
        out_specs=[out, out, out], out_shape=[jax.ShapeDtypeStruct((S, ATTN_OUT_WIDTH), BF16)] * 3,
        scratch_shapes=[scr] * (n_in + 3),
        compiler_params=_params("parallel", "parallel"), name=f"attn_bwd_g{gi}")(
            qkv, qkv, qkv, qkv, qkv, qkv, do, do, lse, lse, dp, dp)


def _group_weights(l_refs):
    l0, l1, l2 = (r[...] for r in l_refs)
    m = jnp.maximum(jnp.maximum(l0, l1), l2)
    e = [jnp.exp(l0 - m), jnp.exp(l1 - m), jnp.exp(l2 - m)]
    tot = e[0] + e[1] + e[2]
    return [ei / tot for ei in e]


def _merge_fwd(outs, lses):
    S, W = outs[0].shape
    T = 512

    def body(o0, o1, o2, l0, l1, l2, y_ref):
        w = _group_weights((l0, l1, l2))
        y_ref[...] = (w[0] * o0[...] + w[1] * o1[...] + w[2] * o2[...]).astype(BF16)

    blk = pl.BlockSpec((T, W), lambda i: (i, 0))
    return pl.pallas_call(
        body, grid=(S // T,), in_specs=[blk] * 6, out_specs=blk, out_shape=jax.ShapeDtypeStruct((S, W), BF16),
        compiler_params=_params("parallel"), name="merge_fwd")(*outs, *lses)


def _merge_bwd(dy, outs, lses):
    S, W = outs[0].shape
    T = 512

    def body(dy_ref, o0, o1, o2, l0, l1, l2, do0, do1, do2, dp0, dp1, dp2):
        w = _group_weights((l0, l1, l2))
        dyv = dy_ref[...]
        y = w[0] * o0[...] + w[1] * o1[...] + w[2] * o2[...]
        prod = dyv * y
        dots = []
        for hi in range(HEADS_PER_GROUP):
            hs = slice(hi * HEAD_DIM, (hi + 1) * HEAD_DIM)
            dots.append(jnp.broadcast_to(jnp.sum(prod[:, hs], axis=-1, keepdims=True), (T, HEAD_DIM)))
        dot_b = jnp.concatenate(dots, axis=1)
        for wg, do_ref, dp_ref in zip(w, (do0, do1, do2), (dp0, dp1, dp2)):
            do_ref[...] = (wg * dyv).astype(BF16)
            dp_ref[...] = wg * dot_b

    blk = pl.BlockSpec((T, W), lambda i: (i, 0))
    res = pl.pallas_call(
        body, grid=(S // T,), in_specs=[blk] * 7, out_specs=[blk] * 6,
        out_shape=[jax.ShapeDtypeStruct((S, W), BF16)] * 3 + [jax.ShapeDtypeStruct((S, W), F32)] * 3,
        compiler_params=_params("parallel"), name="merge_bwd")(dy, *outs, *lses)
    return res[:3], res[3:]


def _mix_fwd(pool_y, attn_y, w_pool_out_t, w_attn_out_t, gates_pre, b_gate):
    S = pool_y.shape[0]
    D = D_MODEL
    tm, tn = 1024, 512
    nj = D // tn

    def body(p_ref, a_ref, wp_ref, wa_ref, g0_ref, g1_ref, b0_ref, b1_ref, mixed_ref, yp_ref, ya_ref):
        yp = _dot(p_ref[...], wp_ref[...], "nt")
        ya = _dot(a_ref[...], wa_ref[...], "nt")
        g0 = jax.nn.sigmoid(g0_ref[...] + b0_ref[...])
        g1 = jax.nn.sigmoid(g1_ref[...] + b1_ref[...])
        mixed_ref[...] = (g0 * yp + g1 * ya).astype(BF16)
        yp_ref[...] = yp.astype(BF16)
        ya_ref[...] = ya.astype(BF16)

    out = pl.BlockSpec((tm, tn), lambda i, j: (i, j))
    return pl.pallas_call(
        body, grid=(S // tm, nj),
        in_specs=[pl.BlockSpec((tm, POOL_WIDTH), lambda i, j: (i, 0)),
                  pl.BlockSpec((tm, ATTN_OUT_WIDTH), lambda i, j: (i, 0)),
                  pl.BlockSpec((tn, POOL_WIDTH), lambda i, j: (j, 0)),
                  pl.BlockSpec((tn, ATTN_OUT_WIDTH), lambda i, j: (j, 0)),
                  pl.BlockSpec((tm, tn), lambda i, j: (i, j)), pl.BlockSpec((tm, tn), lambda i, j: (i, j + nj)),
                  pl.BlockSpec((1, tn), lambda i, j: (0, j)), pl.BlockSpec((1, tn), lambda i, j: (0, j + nj))],
        out_specs=[out, out, out], out_shape=[jax.ShapeDtypeStruct((S, D), BF16)] * 3,
        compiler_params=_params("parallel", "parallel"), name="mix_fwd")(
            pool_y, attn_y, w_pool_out_t, w_attn_out_t, gates_pre, gates_pre, b_gate, b_gate)


def _mix_bwd(dmixed, gates_pre, b_gate, y_pool, y_attn):
    S = dmixed.shape[0]
    D = D_MODEL
    T = 256

    def body(dm_ref, gp_ref, b_ref, yp_ref, ya_ref, dyp_ref, dya_ref, dg_ref, db_ref):
        i = pl.program_id(0)
        dm = dm_ref[...].astype(F32)
        parts = []
        for br, (y_ref, dy_ref) in enumerate(((yp_ref, dyp_ref), (ya_ref, dya_ref))):
            cs = slice(br * D, (br + 1) * D)
            g = jax.nn.sigmoid(gp_ref[:, cs] + b_ref[:, cs])
            dy_ref[...] = (dm * g).astype(BF16)
            dpre = dm * y_ref[...].astype(F32) * g * (1.0 - g)
            dg_ref[:, cs] = dpre.astype(BF16)
            parts.append(jnp.sum(dpre, axis=0, keepdims=True))

        @pl.when(i == 0)
        def _():
            db_ref[:, 0:D] = parts[0]
            db_ref[:, D:2 * D] = parts[1]

        @pl.when(i > 0)
        def _():
            db_ref[:, 0:D] += parts[0]
            db_ref[:, D:2 * D] += parts[1]

    row = pl.BlockSpec((T, D), lambda i: (i, 0))
    row2 = pl.BlockSpec((T, 2 * D), lambda i: (i, 0))
    vec2 = pl.BlockSpec((1, 2 * D), lambda i: (0, 0))
    return pl.pallas_call(
        body, grid=(S // T,), in_specs=[row, row2, vec2, row, row], out_specs=[row, row, row2, vec2],
        out_shape=[jax.ShapeDtypeStruct((S, D), BF16), jax.ShapeDtypeStruct((S, D), BF16),
                   jax.ShapeDtypeStruct((S, 2 * D), BF16), jax.ShapeDtypeStruct((1, 2 * D), F32)],
        compiler_params=_params("arbitrary"), name="mix_bwd")(dmixed, gates_pre, b_gate, y_pool, y_attn)


_CONV_T, _CONV_C = 512, 512


def _conv_taps(ext_ref, h, w_ref, b_ref, cols=slice(None)):
    ext = ext_ref[h, :, cols]
    x2 = pltpu.roll(ext, 2, 0)[HALO:]
    x1 = pltpu.roll(ext, 1, 0)[HALO:]
    x0 = ext[HALO:]
    y = b_ref[:, cols] + w_ref[0:1, cols] * x2
    y = y + w_ref[1:2, cols] * x1
    y = y + w_ref[2:3, cols] * x0
    return y, (x2, x1, x0)


def _gelu_parts(a):
    cdf = 0.5 * (1.0 + lax.erf(a * (1.0 / math.sqrt(2.0))))
    return cdf, a * cdf


def _conv_specs(S):
    T, C = _CONV_T, _CONV_C
    nj = D_FF // C
    blk = pl.BlockSpec((2, T, C), lambda i, j: (0, i, j))
    per = T // HALO
    halo = pl.BlockSpec((2, HALO, C), lambda i, j: (0, jnp.maximum(i * per - 1, 0), j))
    w_a = pl.BlockSpec((3, C), lambda i, j: (0, j))
    w_b = pl.BlockSpec((3, C), lambda i, j: (0, j + nj))
    b_a = pl.BlockSpec((1, C), lambda i, j: (0, j))
    b_b = pl.BlockSpec((1, C), lambda i, j: (0, j + nj))
    return T, C, nj, blk, halo, w_a, w_b, b_a, b_b


def _conv_load(ext_ref, halo_ref, up_ref, first):
    for h in range(2):
        ext_ref[h, 0:HALO, :] = jnp.where(first, 0.0, halo_ref[h].astype(F32))
        ext_ref[h, HALO:, :] = up_ref[h].astype(F32)


def _up_proj_convffn(h, w_up_t, conv_w, conv_b):
    S = h.shape[0]
    tm, C = 1024, _CONV_C
    nj = D_FF // C

    def body(h_ref, wa_ref, wb_ref, cwa_ref, cwb_ref, cba_ref, cbb_ref, up_ref, act_ref, ext_ref, carry_ref):
        i, j = pl.program_id(0), pl.program_id(1)
        for half, w_ref in enumerate((wa_ref, wb_ref)):
            pre = _dot(h_ref[...], w_ref[...], "nt").astype(BF16)
            up_ref[half] = pre
            ext_ref[half, 0:HALO, :] = jnp.where(i == 0, 0.0, carry_ref[j, half])
            ext_ref[half, HALO:, :] = pre.astype(F32)
            carry_ref[j, half] = ext_ref[half, tm:tm + HALO, :]
        a, _ = _conv_taps(ext_ref, 0, cwa_ref, cba_ref)
        b, _ = _conv_taps(ext_ref, 1, cwb_ref, cbb_ref)
        _, gelu = _gelu_parts(a)
        act_ref[...] = (gelu * b).astype(BF16)

    def w_spec(half):
        return pl.BlockSpec((C, D_MODEL), lambda i, j: (j + half * nj, 0))

    def c_spec(rows, half):
        return pl.BlockSpec((rows, C), lambda i, j: (0, j + half * nj))

    return pl.pallas_call(
        body, grid=(S // tm, nj),
        in_specs=[pl.BlockSpec((tm, D_MODEL), lambda i, j: (i, 0)), w_spec(0), w_spec(1),
                  c_spec(3, 0), c_spec(3, 1), c_spec(1, 0), c_spec(1, 1)],
        out_specs=[pl.BlockSpec((2, tm, C), lambda i, j: (0, i, j)), pl.BlockSpec((tm, C), lambda i, j: (i, j))],
        out_shape=[jax.ShapeDtypeStruct((2, S, D_FF), BF16), jax.ShapeDtypeStruct((S, D_FF), BF16)],
        scratch_shapes=[pltpu.VMEM((2, HALO + tm, C), F32), pltpu.VMEM((nj, 2, HALO, C), F32)],
        compiler_params=_params("arbitrary", "arbitrary"), name="up_proj_convffn")(
            h, w_up_t, w_up_t, conv_w, conv_w, conv_b, conv_b)


def _convffn_bwd_a(up_pre, dx, w_down, conv_w, conv_b):
    S = up_pre.shape[1]
    T, C, nj, blk, halo, w_a, w_b, b_a, b_b = _conv_specs(S)
    def swap(spec):
        return pl.BlockSpec(spec.block_shape, lambda j, i, _m=spec.index_map: _m(i, j))

    def body(up_ref, halo_ref, dx_ref, wd_ref, wa_ref, wb_ref, ba_ref, bb_ref, dup_ref, dwa_ref, dwb_ref, dba_ref,
             dbb_ref, ext_ref):
        i = pl.program_id(1)
        _conv_load(ext_ref, halo_ref, up_ref, i == 0)

        @pl.when(i == 0)
        def _():
            for acc_ref in (dwa_ref, dwb_ref, dba_ref, dbb_ref):
                acc_ref[...] = jnp.zeros_like(acc_ref)

        for c in range(C // MXU_DIM):
            cs = slice(c * MXU_DIM, (c + 1) * MXU_DIM)
            dact = _dot(dx_ref[...], wd_ref[cs, :], "nt")
            a, xa = _conv_taps(ext_ref, 0, wa_ref, ba_ref, cs)
            b, xb = _conv_taps(ext_ref, 1, wb_ref, bb_ref, cs)
            cdf, gelu = _gelu_parts(a)
            dgelu = cdf + a * (jnp.exp(-0.5 * a * a) * (1.0 / math.sqrt(2.0 * math.pi)))
            da = dact * b * dgelu
            db = dact * gelu
            dup_ref[0, :, cs] = da.astype(BF16)
            dup_ref[1, :, cs] = db.astype(BF16)
            for dval, xs, dw_ref, dbias_ref in ((da, xa, dwa_ref, dba_ref), (db, xb, dwb_ref, dbb_ref)):
                dw_ref[:, cs] += jnp.concatenate([jnp.sum(dval * xk, axis=0, keepdims=True) for xk in xs], axis=0)
                dbias_ref[:, cs] += jnp.sum(dval, axis=0, keepdims=True)

    dx_spec = pl.BlockSpec((T, D_MODEL), lambda j, i: (i, 0))
    wd_spec = pl.BlockSpec((C, D_MODEL), lambda j, i: (j, 0))
    dw_spec = pl.BlockSpec((3, C), lambda j, i: (0, j))
    db_spec = pl.BlockSpec((1, C), lambda j, i: (0, j))
    res = pl.pallas_call(
        body, grid=(nj, S // T),
        in_specs=[swap(blk), swap(halo), dx_spec, wd_spec, swap(w_a), swap(w_b), swap(b_a), swap(b_b)],
        out_specs=[swap(blk), dw_spec, dw_spec, db_spec, db_spec],
        out_shape=[jax.ShapeDtypeStruct((2, S, D_FF), BF16), jax.ShapeDtypeStruct((3, D_FF), F32),
                   jax.ShapeDtypeStruct((3, D_FF), F32), jax.ShapeDtypeStruct((1, D_FF), F32),
                   jax.ShapeDtypeStruct((1, D_FF), F32)],
        scratch_shapes=[pltpu.VMEM((2, HALO + T, C), F32)],
        compiler_params=_params("parallel", "arbitrary"), name="convffn_bwd_a")(
            up_pre, up_pre, dx, w_down, conv_w, conv_w, conv_b, conv_b)
    dup, dwa, dwb, dba, dbb = res
    dconv_w = jnp.concatenate([dwa, dwb], axis=1)
    dconv_b = jnp.concatenate([dba, dbb], axis=1)
    return dup, dconv_w, dconv_b


def _convffn_bwd_b(dup, conv_w, w_up_t):
    S = dup.shape[1]
    tm, tk = 512, D_FF // 4
    per_half = D_FF // tk
    nk = 2 * per_half
    per = tm // HALO
    n_blk = S // tm

    def body(d_ref, halo_ref, cw_ref, w_ref, dpre_ref, dh_ref, ext_ref, acc_ref):
        i, k = pl.program_id(0), pl.program_id(1)
        n = tm + HALO
        ext_ref[0:tm, :] = d_ref[...].astype(F32)
        ext_ref[tm:, :] = jnp.where(i == n_blk - 1, 0.0, halo_ref[...].astype(F32))
        ext = ext_ref[...]
        y = cw_ref[2:3, :] * ext[0:tm] + cw_ref[1:2, :] * pltpu.roll(ext, n - 1, 0)[0:tm]
        y = y + cw_ref[0:1, :] * pltpu.roll(ext, n - 2, 0)[0:tm]
        pre = y.astype(BF16)
        dpre_ref[...] = pre
        prod = _dot(pre, w_ref[...], "nn")

        @pl.when(k == 0)
        def _():
            acc_ref[...] = prod

        @pl.when(k > 0)
        def _():
            acc_ref[...] += prod

        @pl.when(k == nk - 1)
        def _():
            dh_ref[...] = acc_ref[...]

    blk = pl.BlockSpec((None, tm, tk), lambda i, k: (k // per_half, i, k % per_half))
    halo = pl.BlockSpec((None, HALO, tk),
                        lambda i, k: (k // per_half, jnp.minimum((i + 1) * per, S // HALO - 1), k % per_half))
    return pl.pallas_call(
        body, grid=(n_blk, nk),
        in_specs=[blk, halo, pl.BlockSpec((3, tk), lambda i, k: (0, k)), pl.BlockSpec((tk, D_MODEL), lambda i, k: (k, 0))],
        out_specs=[blk, pl.BlockSpec((tm, D_MODEL), lambda i, k: (i, 0))],
        out_shape=[jax.ShapeDtypeStruct((2, S, D_FF), BF16), jax.ShapeDtypeStruct((S, D_MODEL), F32)],
        scratch_shapes=[pltpu.VMEM((tm + HALO, tk), F32), pltpu.VMEM((tm, D_MODEL), F32)],
        compiler_params=_params("parallel", "arbitrary"), name="convffn_bwd_b")(dup, dup, conv_w, w_up_t)


def _position():
    return lax.axis_index("x"), lax.axis_index("y"), lax.axis_index("c")


def _flip(pos, k):
    x, y, c = pos
    return (1 - x if k & 4 else x, 1 - y if k & 2 else y, 1 - c if k & 1 else c)


def _index(pos):
    return 4 * pos[0] + 2 * pos[1] + pos[2]


def _all_gather_weights(shards):
    n = len(shards)

    def body(*refs):
        ins, outs = refs[:n], refs[n:2 * n]
        send_sems, recv_sems, local_sems = refs[2 * n:]
        me = _position()
        x, y, c = me
        sibling = (x, y, 1 - c)
        chips = [(1 - x, y), (x, 1 - y), (1 - x, 1 - y)]

        def copy(a, k, block, to, own=False):
            slot = outs[a].at[_index(block)]
            return pltpu.make_async_remote_copy(
                src_ref=ins[a] if own else slot, dst_ref=slot, send_sem=send_sems.at[7 * a + k],
                recv_sem=recv_sems.at[7 * a + k], device_id=to, device_id_type=MESH)

        mines = [pltpu.make_async_copy(ins[a], outs[a].at[_index(me)], local_sems.at[a]) for a in range(n)]
        started = []
        for a in range(n):
            mines[a].start()
            first = [copy(a, 0, me, sibling, own=True)]
            first += [copy(a, 1 + j, me, (*chip, c), own=True) for j, chip in enumerate(chips)]
            for cp in first:
                cp.start()
            started += first
        for j, chip in enumerate(chips):
            for a in range(n):
                copy(a, 1 + j, (*chip, c), me).wait_recv()
                passed = copy(a, 4 + j, (*chip, c), sibling)
                passed.start()
                started.append(passed)
        for a in range(n):
            copy(a, 0, sibling, me).wait_recv()
            for j, chip in enumerate(chips):
                copy(a, 4 + j, (*chip, 1 - c), me).wait_recv()
        for cp in started:
            cp.wait_send()
        for cp in mines:
            cp.wait()

    hbm = pl.BlockSpec(memory_space=pl.ANY)
    return pl.pallas_call(
        body, out_shape=[jax.ShapeDtypeStruct((N_DEV,) + s.shape, s.dtype) for s in shards],
        in_specs=[hbm] * n, out_specs=[hbm] * n,
        scratch_shapes=[pltpu.SemaphoreType.DMA((7 * n,)), pltpu.SemaphoreType.DMA((7 * n,)),
                        pltpu.SemaphoreType.DMA((n,))],
        name="all_gather_weights")(*shards)


_HBM = pl.BlockSpec(memory_space=pltpu.HBM)
_SEM = pl.BlockSpec(memory_space=pltpu.SEMAPHORE)
_ANY = pl.BlockSpec(memory_space=pl.ANY)
_EFFECT = pltpu.SideEffectType.DATAFLOW_SIDE_EFFECTING


def _in_hbm(a):
    return pltpu.with_memory_space_constraint(a, pltpu.HBM)


def _split_copies(name, srcs, lands, n_copies, plan, start_after, wait_after):
    ns, nl = len(srcs), len(lands)
    start_after = tuple(start_after)
    na = len(start_after)

    def start_body(*refs):
        src_refs, land_refs = refs[:ns], refs[ns:ns + nl]
        send_sems, recv_sems = refs[ns + nl + na], refs[ns + nl + na + 1]
        token = refs[-1]
        for k, (src, dst, _, peer) in enumerate(plan(src_refs, land_refs)):
            pltpu.make_async_remote_copy(src_ref=src, dst_ref=dst, send_sem=send_sems.at[k], recv_sem=recv_sems.at[k],
                                         device_id=peer, device_id_type=MESH).start()
        token[...] = jnp.zeros_like(token)

    thru = [pltpu.HBM(a.shape, a.dtype) for a in list(srcs) + list(lands)]
    res = pl.pallas_call(
        start_body, name=name + "_start",
        out_shape=[pltpu.SemaphoreType.DMA((n_copies,)), pltpu.SemaphoreType.DMA((n_copies,))] + thru
        + [jax.ShapeDtypeStruct((8, 128), F32)],
        in_specs=[_HBM] * (ns + nl) + [_ANY] * na,
        out_specs=[_SEM, _SEM] + [_HBM] * (ns + nl) + [pl.BlockSpec(memory_space=pltpu.VMEM)],
        input_output_aliases={i: 2 + i for i in range(ns + nl)},
        compiler_params=pltpu.CompilerParams(has_side_effects=_EFFECT),
    )(*[_in_hbm(a) for a in list(srcs) + list(lands)], *start_after)
    send_sems, recv_sems, token = res[0], res[1], res[-1]
    thru_vals = res[2:-1]

    def wait():
        def wait_body(*refs):
            src_refs, land_refs = refs[:ns], refs[ns:ns + nl]
            s_sems, r_sems = refs[ns + nl], refs[ns + nl + 1]
            for k, (src, _, dst, peer) in enumerate(plan(src_refs, land_refs)):
                cp = pltpu.make_async_remote_copy(src_ref=src, dst_ref=dst, send_sem=s_sems.at[k], recv_sem=r_sems.at[k],
                                                  device_id=peer, device_id_type=MESH)
                cp.wait_send()
                cp.wait_recv()

        out = pl.pallas_call(
            wait_body, name=name + "_wait", out_shape=thru,
            in_specs=[_HBM] * (ns + nl) + [_SEM, _SEM, _ANY], out_specs=[_HBM] * (ns + nl),
            input_output_aliases={i: i for i in range(ns + nl)},
            compiler_params=pltpu.CompilerParams(has_side_effects=_EFFECT),
        )(*thru_vals, send_sems, recv_sems, wait_after())
        return list(out[:ns]), list(out[ns:])

    return wait, token


def _gather_direct_plan(n):
    def plan(src_refs, land_refs):
        me = _position()
        peers = [_flip(me, k) for k in range(1, N_DEV)]
        return [(src_refs[a], land_refs[a].at[_index(me)], land_refs[a].at[_index(p)], p)
                for a in range(n) for p in peers]
    return plan


def _place_own(shards):
    me = _index(_position())
    return [lax.dynamic_update_slice(lax.empty((N_DEV,) + s.shape, s.dtype), s[None], (me, 0, 0)) for s in shards]


def _exchange_direct_plan(n):
    def plan(src_refs, land_refs):
        me = _position()
        peers = [_flip(me, k) for k in range(1, N_DEV)]
        return [(src_refs[a].at[_index(p)], land_refs[a].at[_index(me)], land_refs[a].at[_index(p)], p)
                for a in range(n) for p in peers]
    return plan


def _adamw(w, g, m, v):
    m = ADAM_B1 * m + (1.0 - ADAM_B1) * g
    v = ADAM_B2 * v + (1.0 - ADAM_B2) * (g * g)
    m_hat = m / (1.0 - ADAM_B1 ** ADAM_STEP)
    v_hat = v / (1.0 - ADAM_B2 ** ADAM_STEP)
    delta = -ADAM_LR * (m_hat / (jnp.sqrt(v_hat) + ADAM_EPS) + ADAM_WD * w)
    return delta, m, v


def _row_tile(rows, unit, cap=256):
    best = rows
    for t in range(unit, min(rows, cap) + 1, unit):
        if rows % t == 0:
            best = t
    return best


def _sum_partials(name, recv):
    n_src, R, C = recv.shape
    T = _row_tile(R, 16)

    def body(r_ref, g_out):
        g = r_ref[0].astype(F32)
        for s in range(1, n_src):
            g = g + r_ref[s].astype(F32)
        g_out[...] = g

    return pl.pallas_call(
        body, grid=(R // T,), in_specs=[pl.BlockSpec((n_src, T, C), lambda i: (0, i, 0))],
        out_specs=pl.BlockSpec((T, C), lambda i: (i, 0)), out_shape=jax.ShapeDtypeStruct((R, C), F32),
        compiler_params=_params("parallel"), name=name)(recv)


def _adamw_update(name, w, g, m, v):
    R, C = w.shape
    T = _row_tile(R, 8)

    def body(w_ref, g_ref, m_ref, v_ref, d_out, m_out, v_out):
        delta, mn, vn = _adamw(w_ref[...], g_ref[...], m_ref[...], v_ref[...])
        d_out[...] = delta
        m_out[...] = mn
        v_out[...] = vn

    blk = pl.BlockSpec((T, C), lambda i: (i, 0))
    return pl.pallas_call(
        body, grid=(R // T,), in_specs=[blk] * 4, out_specs=[blk] * 3,
        out_shape=[jax.ShapeDtypeStruct((R, C), F32)] * 3, compiler_params=_params("parallel"), name=name)(w, g, m, v)


def _gather_vmem(p_ref, gath, send_sems, recv_sems):
    me = _position()
    gath[_index(me)] = p_ref[...]
    copies = []
    for k in range(1, N_DEV):
        peer = _flip(me, k)
        copies.append(pltpu.make_async_remote_copy(
            src_ref=p_ref, dst_ref=gath.at[_index(me)], send_sem=send_sems.at[k - 1],
            recv_sem=recv_sems.at[k - 1], device_id=peer, device_id_type=MESH))
    for cp in copies:
        cp.start()
    for k in range(1, N_DEV):
        peer = _flip(me, k)
        pltpu.make_async_remote_copy(
            src_ref=p_ref, dst_ref=gath.at[_index(peer)], send_sem=send_sems.at[k - 1],
            recv_sem=recv_sems.at[k - 1], device_id=peer, device_id_type=MESH).wait_recv()
    for cp in copies:
        cp.wait_send()


def _all_gather_small(p):
    R, C = p.shape

    def body(p_ref, out_ref, send_sems, recv_sems):
        _gather_vmem(p_ref, out_ref, send_sems, recv_sems)

    vm = pl.BlockSpec(memory_space=pltpu.VMEM)
    return pl.pallas_call(
        body, in_specs=[vm], out_specs=vm, out_shape=jax.ShapeDtypeStruct((N_DEV, R, C), F32),
        scratch_shapes=[pltpu.SemaphoreType.DMA((7,)), pltpu.SemaphoreType.DMA((7,))],
        name="all_gather_small")(p)


def _small_allreduce_adamw(part, w, m, v):
    R, C = part.shape

    def body(p_ref, w_ref, m_ref, v_ref, g_out, d_out, m_out, v_out, gath, send_sems, recv_sems):
        _gather_vmem(p_ref, gath, send_sems, recv_sems)
        g = gath[0]
        for s in range(1, N_DEV):
            g = g + gath[s]
        delta, mn, vn = _adamw(w_ref[...], g, m_ref[...], v_ref[...])
        g_out[...] = g
        d_out[...] = delta
        m_out[...] = mn
        v_out[...] = vn

    vm = pl.BlockSpec(memory_space=pltpu.VMEM)
    return pl.pallas_call(
        body, in_specs=[vm] * 4, out_specs=[vm] * 4, out_shape=[jax.ShapeDtypeStruct((R, C), F32)] * 4,
        scratch_shapes=[pltpu.VMEM((N_DEV, R, C), F32), pltpu.SemaphoreType.DMA((7,)), pltpu.SemaphoreType.DMA((7,))],
        name="small_allreduce_adamw")(part, w, m, v)


_BIG = ("w_in", "w_up", "w_down", "w_out", "w_pool_out", "w_attn_out")
_TRANSPOSED = ("w_in", "w_up", "w_pool_out", "w_attn_out")
_SMALL = (("g_mix", 2048), ("b_gate", 4096), ("pool_scale", 1024), ("g_ffn", 2048), ("conv_b", 11264), ("g_final", 2048))
CONV_ROWS = 8


def _to_comm(name, shard):
    return shard.T if name in _TRANSPOSED else shard


def _pack_small(vals):
    flat = [vals[n].reshape(-1) for n, _ in _SMALL]
    used = sum(f.shape[0] for f in flat)
    flat.append(jnp.zeros((SMALL_ROWS * PACK_COLS - used,), F32))
    return jnp.concatenate(flat).reshape(SMALL_ROWS, PACK_COLS)


def _unpack_small(packed, like):
    flat = packed.reshape(-1)
    out, off = {}, 0
    for name, n in _SMALL:
        out[name] = flat[off:off + n].reshape(like[name].shape)
        off += n
    return out


def _local_step(x, target, W, small, first_token=0.0, late_weights=None, emit=None):
    S = x.shape[0]
    o_qkv, o_gate = POOL_WIDTH, POOL_WIDTH + QKV_WIDTH
    if emit is None:
        emit = lambda grads: None

    def zero(token):
        return 0.0 if token is None else token[0, 0]
    w_in_t = W["w_in"]

    h1 = _rms_fwd("rms1_fwd", x, small["g_mix"] + first_token)
    u = _mm_nt("proj_u", h1, w_in_t, tm=2048, tn=512, tk=2048, o_dtype=F32, n_rows=POOL_WIDTH, b_row_off=0)
    qkv = _mm_nt("proj_qkv", h1, w_in_t, tm=2048, tn=512, tk=2048, o_dtype=BF16, n_rows=QKV_WIDTH, b_row_off=o_qkv)
    gates_pre = _mm_nt("proj_gates", h1, w_in_t, tm=2048, tn=512, tk=2048, o_dtype=F32, n_rows=2 * D_MODEL,
                       b_row_off=o_gate)
    pool_y = _pool_fwd(u, W["w_pool_lin"], small["pool_scale"])
    outs, lses = zip(*[_attn_fwd(gi, qkv) for gi in range(3)])
    attn_y = _merge_fwd(outs, lses)
    if late_weights is not None:
        W = dict(W, **late_weights(0, attn_y))
    w_up_t = W["w_up"]
    mixed, y_pool, y_attn = _mix_fwd(pool_y, attn_y, W["w_pool_out"], W["w_attn_out"], gates_pre, small["b_gate"])
    x1 = _mm_nn("out_proj", mixed, W["w_out"], tm=1024, tn=1024, tk=2048, o_dtype=F32, res=x)

    h2 = _rms_fwd("rms2_fwd", x1, small["g_ffn"])
    nj = D_FF // 512
    up_pre, act = _up_proj_convffn(h2, w_up_t, small["conv_w_full"], small["conv_b"])
    if late_weights is not None:
        W = dict(W, **late_weights(1, act))
    x2 = _mm_nn("down_proj", act, W["w_down"], tm=1024, tn=1024, tk=D_FF // 2, o_dtype=F32, res=x1)

    dx2, dx2_b, dg_final, sq_cols = _final_loss("final_loss", x2, small["g_final"], target)

    dw_down = _mm_tn("dw_down", act, dx2_b, tm=512, tn=2048, tk=TOKEN_TK)
    tok = emit(dict(w_down=dw_down))
    dup, dconv_w, dconv_b = _convffn_bwd_a(up_pre, dx2_b, W["w_down"], small["conv_w_full"],
                                           small["conv_b"] + zero(tok))
    dup_pre, dh2 = _convffn_bwd_b(dup, small["conv_w_full"], w_up_t)
    dw_up = _mm("dw_up", dup_pre, h2, mode="tn", grid=(2 * nj, 1, S // TOKEN_TK),
                a_spec=pl.BlockSpec((None, TOKEN_TK, 512), lambda i, j, k: (i // nj, k, i % nj)),
                b_spec=pl.BlockSpec((TOKEN_TK, D_MODEL), lambda i, j, k: (k, 0)),
                o_spec=pl.BlockSpec((512, D_MODEL), lambda i, j, k: (i, 0)),
                o_shape=(2 * D_FF, D_MODEL), o_dtype=BF16, acc_shape=(512, D_MODEL))
    tok = emit(dict(w_up=dw_up, conv_w=dconv_w))
    dx1, dx1_b, dg_ffn = _rms_bwd("rms2_bwd", x1, small["g_ffn"] + zero(tok), dh2, dx2)

    dmixed = _mm_nt("d_mixed", dx1_b, W["w_out"], tm=1024, tn=1024, tk=2048, o_dtype=BF16)
    dw_out = _mm_tn("dw_out", mixed, dx1_b, tm=512, tn=2048, tk=TOKEN_TK)
    tok = emit(dict(w_out=dw_out))
    dy_pool, dy_attn, dgates, db_gate = _mix_bwd(dmixed, gates_pre, small["b_gate"] + zero(tok), y_pool, y_attn)
    dpool_y = _mm_nn("d_pool_y", dy_pool, W["w_pool_out"], tm=1024, tn=1024, tk=2048, o_dtype=F32)
    dw_pool_out = _mm_tn("dw_pool_out", dy_pool, pool_y, tm=1024, tn=1024, tk=TOKEN_TK)
    dattn_y = _mm_nn("d_attn_y", dy_attn, W["w_attn_out"], tm=1024, tn=512, tk=2048, o_dtype=F32)
    dw_attn_out = _mm_tn("dw_attn_out", dy_attn, attn_y, tm=1024, tn=512, tk=TOKEN_TK)
    e, dw_pool_lin, dpool_scale = _pool_bwd_a(u, dpool_y, W["w_pool_lin"], small["pool_scale"])
    du = _pool_bwd_b(e)
    dos, dps = _merge_bwd(dattn_y, outs, lses)
    dqkv = [_attn_bwd(gi, qkv, dos[gi], lses[gi], dps[gi]) for gi in range(3)]
    dproj = jnp.concatenate([du] + [dqkv[gi][sec] for sec in range(3) for gi in range(3)] + [dgates], axis=1)
    dw_in = _mm_tn("dw_in", dproj, h1, tm=512, tn=2048, tk=TOKEN_TK)
    tok = emit(dict(w_in=dw_in, w_pool_out=dw_pool_out, w_attn_out=dw_attn_out, w_pool_lin=dw_pool_lin))
    dh1 = _mm_nn("d_h1", dproj, w_in_t, tm=1024, tn=1024, tk=IN_WIDTH // 4, o_dtype=F32,
                 after=tok)
    grad_x, _, dg_mix = _rms_bwd("rms1_bwd", x, small["g_mix"] + zero(tok), dh1, dx1)

    grads = dict(w_in=dw_in, w_up=dw_up, w_down=dw_down, w_out=dw_out, w_pool_out=dw_pool_out,
                 w_attn_out=dw_attn_out, w_pool_lin=dw_pool_lin, conv_w=dconv_w)
    small_grads = dict(g_mix=dg_mix, b_gate=db_gate, pool_scale=dpool_scale, g_ffn=dg_ffn, conv_b=dconv_b,
                       g_final=dg_final)
    return sq_cols, grad_x, grads, small_grads


def kernel(x, g_mix, w_in, b_gate, w_pool_lin, pool_scale, w_pool_out, w_attn_out, w_out, g_ffn, w_up, conv_w, conv_b, w_down, g_final, loss_target, m_g_mix, m_w_in, m_b_gate, m_w_pool_lin, m_pool_scale, m_w_pool_out, m_w_attn_out, m_w_out, m_g_ffn, m_w_up, m_conv_w, m_conv_b, m_w_down, m_g_final, v_g_mix, v_w_in, v_b_gate, v_w_pool_lin, v_pool_scale, v_w_pool_out, v_w_attn_out, v_w_out, v_g_ffn, v_w_up, v_conv_w, v_conv_b, v_w_down, v_g_final):
    given = dict(g_mix=g_mix, w_in=w_in, b_gate=b_gate, w_pool_lin=w_pool_lin, pool_scale=pool_scale,
                 w_pool_out=w_pool_out, w_attn_out=w_attn_out, w_out=w_out, g_ffn=g_ffn, w_up=w_up, conv_w=conv_w,
                 conv_b=conv_b, w_down=w_down, g_final=g_final)
    mom_m = dict(g_mix=m_g_mix, w_in=m_w_in, b_gate=m_b_gate, w_pool_lin=m_w_pool_lin, pool_scale=m_pool_scale,
                 w_pool_out=m_w_pool_out, w_attn_out=m_w_attn_out, w_out=m_w_out, g_ffn=m_g_ffn, w_up=m_w_up,
                 conv_w=m_conv_w, conv_b=m_conv_b, w_down=m_w_down, g_final=m_g_final)
    mom_v = dict(g_mix=v_g_mix, w_in=v_w_in, b_gate=v_b_gate, w_pool_lin=v_w_pool_lin, pool_scale=v_pool_scale,
                 w_pool_out=v_w_pool_out, w_attn_out=v_w_attn_out, w_out=v_w_out, g_ffn=v_g_ffn, w_up=v_w_up,
                 conv_w=v_conv_w, conv_b=v_conv_b, w_down=v_w_down, g_final=v_g_final)
    def full(g):
        return g.reshape(N_DEV * g.shape[1], g.shape[2])

    shard = {n: _to_comm(n, given[n][0]).astype(BF16) for n in _BIG}
    W = dict(w_in=full(_all_gather_weights([shard["w_in"]])[0]))
    lin_all = _all_gather_small(w_pool_lin[0].reshape(4 * 32, POOL_GROUP_WIDTH))
    W["w_pool_lin"] = jnp.moveaxis(lin_all.reshape(N_DEV, 4, 32, POOL_GROUP_WIDTH), 0, 1).reshape(
        4, POOL_GROUP_WIDTH, POOL_GROUP_WIDTH).astype(BF16)
    conv_w_all = _all_gather_small(jnp.pad(conv_w[0], ((0, CONV_ROWS - 3), (0, 0))))
    conv_w_full = jnp.moveaxis(conv_w_all[:, :3, :], 0, 1).reshape(3, 2 * D_FF)

    stages = (("w_out", "w_pool_out", "w_attn_out", "w_up"), ("w_down",))
    stage_after = [None, None]
    stage_wait = []
    order_after = (W["w_in"], lin_all, conv_w_all)
    for k, names in enumerate(stages):
        srcs = [shard[n] for n in names]
        wait, token = _split_copies(f"gather_stage{k}", srcs, _place_own(srcs), 7 * len(names),
                                    _gather_direct_plan(len(names)), start_after=order_after,
                                    wait_after=lambda k=k: stage_after[k])
        stage_wait.append(wait)
        order_after = (token,)
    rest_token = token

    def late_weights(stage, after):
        stage_after[stage] = after
        return {n: full(l) for n, l in zip(stages[stage], stage_wait[stage]()[1])}

    small = dict(g_mix=g_mix, b_gate=b_gate, pool_scale=pool_scale, g_ffn=g_ffn, conv_b=conv_b,
                 g_final=g_final.reshape(1, D_MODEL), conv_w_full=conv_w_full)

    pending = []
    done = {}

    def emit(group):
        names = list(group)
        parts = []
        for n in names:
            g = group[n]
            if n == "w_pool_lin":
                g = jnp.moveaxis(g.reshape(4, N_DEV, 32, POOL_GROUP_WIDTH), 1, 0).reshape(
                    N_DEV, 4 * 32, POOL_GROUP_WIDTH).astype(BF16)
            elif n == "conv_w":
                g = jnp.pad(jnp.moveaxis(g.reshape(3, N_DEV, 2 * D_FF // N_DEV), 1, 0),
                            ((0, 0), (0, CONV_ROWS - 3), (0, 0))).astype(BF16)
            else:
                g = g.reshape(N_DEV, g.shape[0] // N_DEV, g.shape[1])
            parts.append(g)
        me = _index(_position())
        lands = [lax.dynamic_update_slice(lax.empty(p.shape, p.dtype), lax.dynamic_index_in_dim(p, me, 0),
                                          (me, 0, 0)) for p in parts]
        wait, token = _split_copies(
            "exchange_" + names[0], parts, lands, 7 * len(names), _exchange_direct_plan(len(names)),
            start_after=(), wait_after=lambda: done["grad_x"])
        pending.append((names, wait))
        return token

    sq_cols, grad_x, _, small_grads = _local_step(x[0], loss_target[0], W, small, first_token=rest_token[0, 0],
                                                  late_weights=late_weights, emit=emit)
    done["grad_x"] = grad_x
    loss = lax.psum(0.5 / D_MODEL * jnp.sum(sq_cols), AXES)

    res = {}
    for n, r in [(n, r) for names, wait in pending for n, r in zip(names, wait()[1])]:
        g = _sum_partials("sum_" + n, r)
        if n in _TRANSPOSED:
            g = g.T
        elif n == "conv_w":
            g = g[:3]
        shape = given[n].shape
        rows = math.prod(shape[:-1])
        w2, m2, v2 = (d[n].reshape(rows, shape[-1]) for d in (given, mom_m, mom_v))
        upd = _adamw_update("adamw_" + n, w2, g.reshape(rows, shape[-1]), m2, v2)
        res[n] = [a.reshape(shape) for a in (g,) + tuple(upd)]

    packed_small = [_pack_small(d) for d in (given, mom_m, mom_v)]
    res_small = [_unpack_small(p, given) for p in _small_allreduce_adamw(_pack_small(small_grads), *packed_small)]

    order = ["g_mix", "w_in", "b_gate", "w_pool_lin", "pool_scale", "w_pool_out", "w_attn_out", "w_out", "g_ffn",
             "w_up", "conv_w", "conv_b", "w_down", "g_final"]
    outs = [loss, grad_x[None]]
    for kind in range(4):
        for n in order:
            outs.append(res[n][kind] if n in res else res_small[kind][n])
    return tuple(outs)
```

```python
import math

import jax
import jax.numpy as jnp
from jax import lax
from jax.experimental import pallas as pl
from jax.experimental.pallas import tpu as pltpu

F32 = jnp.float32
BF16 = jnp.bfloat16
MESH = pl.DeviceIdType.MESH
AXES = ("x", "y", "c")
N_DEV = 8

D_MODEL = 2048
POOL_WINDOWS = (2, 4, 8, 16)
POOL_GROUP_WIDTH = 256
POOL_WIDTH = 1024
ATTN_GROUPS = ((128, 1), (512, 4), (2048, 16))
SPAN = 128
HEADS_PER_GROUP = 4
N_ATTN_HEADS = 12
HEAD_DIM = 128
ATTN_WIDTH = 1536
QKV_WIDTH = 3 * ATTN_WIDTH
ATTN_OUT_WIDTH = 512
IN_WIDTH = 9728
D_FF = 5632
RMS_EPS = 1e-6
ADAM_LR, ADAM_B1, ADAM_B2, ADAM_EPS, ADAM_WD, ADAM_STEP = 0.001, 0.9, 0.999, 1e-08, 0.01, 10

VMEM_LIMIT_BYTES = 48 * 1024 * 1024
HALO = 16
TOKEN_TK = 2048
MXU_DIM = 256
PACK_COLS = 1024
SMALL_ROWS = 24
NEG_BIG = -1e30

_DN = {"nn": (((1,), (0,)), ((), ())), "nt": (((1,), (1,)), ((), ())), "tn": (((0,), (0,)), ((), ()))}


def _params(*sem):
    return pltpu.CompilerParams(dimension_semantics=sem, vmem_limit_bytes=VMEM_LIMIT_BYTES)


def _dot(a, b, mode):
    return lax.dot_general(a.astype(BF16), b.astype(BF16), _DN[mode], preferred_element_type=F32)


def _mm(name, a, b, *, mode, grid, a_spec, b_spec, o_spec, o_shape, o_dtype, acc_shape, res=None, res_spec=None,
        after=None):
    nk = grid[2]
    n_in = 2 + (res is not None) + (after is not None)

    def body(*refs):
        a_ref, b_ref = refs[:2]
        r_ref = refs[2] if res is not None else None
        o_ref = refs[n_in]
        scr = refs[n_in + 1:]
        prod = _dot(a_ref[...], b_ref[...], mode)

        def finish(val):
            if r_ref is not None:
                val = val + r_ref[...]
            o_ref[...] = val.astype(o_ref.dtype)

        if nk == 1:
            finish(prod)
        else:
            acc = scr[0]
            k = pl.program_id(2)

            @pl.when(k == 0)
            def _():
                acc[...] = prod

            @pl.when(k > 0)
            def _():
                acc[...] += prod

            @pl.when(k == nk - 1)
            def _():
                finish(acc[...])

    in_specs = [a_spec, b_spec]
    args = [a, b]
    if res is not None:
        in_specs.append(res_spec)
        args.append(res)
    if after is not None:
        in_specs.append(pl.BlockSpec((8, 128), lambda i, j, k: (0, 0)))
        args.append(after)
    return pl.pallas_call(
        body, grid=grid, in_specs=in_specs, out_specs=o_spec,
        out_shape=jax.ShapeDtypeStruct(o_shape, o_dtype),
        scratch_shapes=[] if nk == 1 else [pltpu.VMEM(acc_shape, F32)],
        compiler_params=_params("parallel", "parallel", "arbitrary"), name=name)(*args)


def _mm_nn(name, a, b, *, tm, tn, tk, o_dtype, n_cols=None, b_col_off=0, res=None, after=None):
    M, K = a.shape
    N = b.shape[1] if n_cols is None else n_cols
    off = b_col_off // tn
    return _mm(name, a, b, mode="nn", grid=(M // tm, N // tn, K // tk),
               a_spec=pl.BlockSpec((tm, tk), lambda i, j, k: (i, k)),
               b_spec=pl.BlockSpec((tk, tn), lambda i, j, k: (k, j + off)),
               o_spec=pl.BlockSpec((tm, tn), lambda i, j, k: (i, j)),
               o_shape=(M, N), o_dtype=o_dtype, acc_shape=(tm, tn), res=res,
               res_spec=pl.BlockSpec((tm, tn), lambda i, j, k: (i, j)), after=after)


def _mm_nt(name, a, b, *, tm, tn, tk, o_dtype, n_rows=None, b_row_off=0, after=None):
    M, K = a.shape
    N = b.shape[0] if n_rows is None else n_rows
    off = b_row_off // tn
    return _mm(name, a, b, mode="nt", grid=(M // tm, N // tn, K // tk),
               a_spec=pl.BlockSpec((tm, tk), lambda i, j, k: (i, k)),
               b_spec=pl.BlockSpec((tn, tk), lambda i, j, k: (j + off, k)),
               o_spec=pl.BlockSpec((tm, tn), lambda i, j, k: (i, j)),
               o_shape=(M, N), o_dtype=o_dtype, acc_shape=(tm, tn), after=after)


def _mm_tn(name, a, b, *, tm, tn, tk):
    K, M = a.shape
    N = b.shape[1]
    return _mm(name, a, b, mode="tn", grid=(M // tm, N // tn, K // tk),
               a_spec=pl.BlockSpec((tk, tm), lambda i, j, k: (k, i)),
               b_spec=pl.BlockSpec((tk, tn), lambda i, j, k: (k, j)),
               o_spec=pl.BlockSpec((tm, tn), lambda i, j, k: (i, j)),
               o_shape=(M, N), o_dtype=BF16, acc_shape=(tm, tn))


def _rms_fwd(name, x, g):
    S, D = x.shape
    T = 512

    def body(x_ref, g_ref, h_ref):
        xv = x_ref[...]
        r = lax.rsqrt(jnp.mean(xv * xv, axis=-1, keepdims=True) + RMS_EPS)
        h_ref[...] = (xv * r * g_ref[...]).astype(BF16)

    return pl.pallas_call(
        body, grid=(S // T,),
        in_specs=[pl.BlockSpec((T, D), lambda i: (i, 0)), pl.BlockSpec((1, D), lambda i: (0, 0))],
        out_specs=pl.BlockSpec((T, D), lambda i: (i, 0)),
        out_shape=jax.ShapeDtypeStruct((S, D), BF16), compiler_params=_params("parallel"), name=name)(x, g)


def _rms_bwd(name, x, g, dh, dres):
    S, D = x.shape
    T = 256

    def body(x_ref, g_ref, dh_ref, dres_ref, dx_ref, dxb_ref, dg_ref):
        i = pl.program_id(0)
        xv = x_ref[...]
        r = lax.rsqrt(jnp.mean(xv * xv, axis=-1, keepdims=True) + RMS_EPS)
        xhat = xv * r
        dhv = dh_ref[...].astype(F32)
        gdh = dhv * g_ref[...]
        c = jnp.mean(xhat * gdh, axis=-1, keepdims=True)
        dx = dres_ref[...] + r * (gdh - xhat * c)
        dx_ref[...] = dx
        dxb_ref[...] = dx.astype(BF16)
        part = jnp.sum(dhv * xhat, axis=0, keepdims=True)

        @pl.when(i == 0)
        def _():
            dg_ref[...] = part

        @pl.when(i > 0)
        def _():
            dg_ref[...] += part

    row = pl.BlockSpec((T, D), lambda i: (i, 0))
    vec = pl.BlockSpec((1, D), lambda i: (0, 0))
    return pl.pallas_call(
        body, grid=(S // T,), in_specs=[row, vec, row, row], out_specs=[row, row, vec],
        out_shape=[jax.ShapeDtypeStruct((S, D), F32), jax.ShapeDtypeStruct((S, D), BF16),
                   jax.ShapeDtypeStruct((1, D), F32)],
        compiler_params=_params("arbitrary"), name=name)(x, g, dh, dres)


def _final_loss(name, x, g, target):
    S, D = x.shape
    T = 256

    def body(x_ref, g_ref, t_ref, dx_ref, dxb_ref, dg_ref, sq_ref):
        i = pl.program_id(0)
        xv = x_ref[...]
        r = lax.rsqrt(jnp.mean(xv * xv, axis=-1, keepdims=True) + RMS_EPS)
        xhat = xv * r
        err = xhat * g_ref[...] - t_ref[...]
        dy = err * (1.0 / D)
        gdy = dy * g_ref[...]
        c = jnp.mean(xhat * gdy, axis=-1, keepdims=True)
        dx = r * (gdy - xhat * c)
        dx_ref[...] = dx
        dxb_ref[...] = dx.astype(BF16)
        dg_part = jnp.sum(dy * xhat, axis=0, keepdims=True)
        sq_part = jnp.sum(err * err, axis=0, keepdims=True)

        @pl.when(i == 0)
        def _():
            dg_ref[...] = dg_part
            sq_ref[...] = sq_part

        @pl.when(i > 0)
        def _():
            dg_ref[...] += dg_part
            sq_ref[...] += sq_part

    row = pl.BlockSpec((T, D), lambda i: (i, 0))
    vec = pl.BlockSpec((1, D), lambda i: (0, 0))
    return pl.pallas_call(
        body, grid=(S // T,), in_specs=[row, vec, row], out_specs=[row, row, vec, vec],
        out_shape=[jax.ShapeDtypeStruct((S, D), F32), jax.ShapeDtypeStruct((S, D), BF16),
                   jax.ShapeDtypeStruct((1, D), F32), jax.ShapeDtypeStruct((1, D), F32)],
        compiler_params=_params("arbitrary"), name=name)(x, g, target)


def _prev_halo_spec(T, C, col_map):
    per = T // HALO
    return pl.BlockSpec((HALO, C), lambda *g: (jnp.maximum(g[0] * per - 1, 0), col_map(*g)))


def _pool_window_sums(ext, T, i):
    t = i * T + lax.broadcasted_iota(jnp.int32, (T, 1), 0)
    outs = []
    cnts = []
    for gi, w in enumerate(POOL_WINDOWS):
        s = ext[:, gi * POOL_GROUP_WIDTH:(gi + 1) * POOL_GROUP_WIDTH]
        sh = 1
        while sh < w:
            s = s + pltpu.roll(s, sh, 0)
            sh *= 2
        cnt = jnp.minimum(t + 1, w).astype(F32)
        outs.append(s[HALO:] / cnt - ext[HALO:, gi * POOL_GROUP_WIDTH:(gi + 1) * POOL_GROUP_WIDTH])
        cnts.append(cnt)
    return outs, cnts


def _load_ext(ext_ref, halo_ref, blk_ref, first):
    ext_ref[0:HALO, :] = jnp.where(first, 0.0, halo_ref[...].astype(F32))
    ext_ref[HALO:, :] = blk_ref[...].astype(F32)


def _pool_fwd(u, w_lin, scale):
    S = u.shape[0]
    T = 512
    W = POOL_WIDTH

    def body(u_ref, halo_ref, wl_ref, sc_ref, y_ref, ext_ref):
        i = pl.program_id(0)
        _load_ext(ext_ref, halo_ref, u_ref, i == 0)
        pooled, _ = _pool_window_sums(ext_ref[...], T, i)
        for gi in range(4):
            z = _dot(pooled[gi], wl_ref[gi], "nn")
            cs = slice(gi * POOL_GROUP_WIDTH, (gi + 1) * POOL_GROUP_WIDTH)
            y_ref[:, cs] = (z * sc_ref[:, cs]).astype(BF16)

    return pl.pallas_call(
        body, grid=(S // T,),
        in_specs=[pl.BlockSpec((T, W), lambda i: (i, 0)), _prev_halo_spec(T, W, lambda i: 0),
                  pl.BlockSpec((4, 256, 256), lambda i: (0, 0, 0)), pl.BlockSpec((1, W), lambda i: (0, 0))],
        out_specs=pl.BlockSpec((T, W), lambda i: (i, 0)),
        out_shape=jax.ShapeDtypeStruct((S, W), BF16),
        scratch_shapes=[pltpu.VMEM((HALO + T, W), F32)],
        compiler_params=_params("parallel"), name="pool_fwd")(u, u, w_lin, scale)


def _pool_bwd_a(u, dy, w_lin, scale):
    S = u.shape[0]
    T = 512
    W = POOL_WIDTH

    def body(u_ref, halo_ref, dy_ref, wl_ref, sc_ref, e_ref, dwl_ref, dsc_ref, ext_ref):
        i = pl.program_id(0)
        _load_ext(ext_ref, halo_ref, u_ref, i == 0)
        pooled, cnts = _pool_window_sums(ext_ref[...], T, i)
        for gi in range(4):
            cs = slice(gi * POOL_GROUP_WIDTH, (gi + 1) * POOL_GROUP_WIDTH)
            z = _dot(pooled[gi], wl_ref[gi], "nn")
            dyg = dy_ref[:, cs]
            dsc = jnp.sum(dyg * z, axis=0, keepdims=True)
            dz = dyg * sc_ref[:, cs]
            dwl = _dot(pooled[gi], dz, "tn")
            dpooled = _dot(dz, wl_ref[gi], "nt")
            e_ref[:, cs] = dpooled / cnts[gi]

            @pl.when(i == 0)
            def _():
                dwl_ref[gi] = dwl
                dsc_ref[:, cs] = dsc

            @pl.when(i > 0)
            def _():
                dwl_ref[gi] += dwl
                dsc_ref[:, cs] += dsc

    return pl.pallas_call(
        body, grid=(S // T,),
        in_specs=[pl.BlockSpec((T, W), lambda i: (i, 0)), _prev_halo_spec(T, W, lambda i: 0),
                  pl.BlockSpec((T, W), lambda i: (i, 0)),
                  pl.BlockSpec((4, 256, 256), lambda i: (0, 0, 0)), pl.BlockSpec((1, W), lambda i: (0, 0))],
        out_specs=[pl.BlockSpec((T, W), lambda i: (i, 0)), pl.BlockSpec((4, 256, 256), lambda i: (0, 0, 0)),
                   pl.BlockSpec((1, W), lambda i: (0, 0))],
        out_shape=[jax.ShapeDtypeStruct((S, W), F32), jax.ShapeDtypeStruct((4, 256, 256), F32),
                   jax.ShapeDtypeStruct((1, W), F32)],
        scratch_shapes=[pltpu.VMEM((HALO + T, W), F32)],
        compiler_params=_params("arbitrary"), name="pool_bwd_a")(u, u, dy, w_lin, scale)


def _pool_bwd_b(e):
    S = e.shape[0]
    T = 512
    W = POOL_WIDTH
    per = T // HALO
    n_blk = S // T

    def body(e_ref, halo_ref, du_ref, ext_ref):
        i = pl.program_id(0)
        ext_ref[0:T, :] = e_ref[...]
        ext_ref[T:, :] = jnp.where(i == n_blk - 1, 0.0, halo_ref[...])
        ext = ext_ref[...]
        n = T + HALO
        t = i * T + lax.broadcasted_iota(jnp.int32, (T, 1), 0)
        for gi, w in enumerate(POOL_WINDOWS):
            cs = slice(gi * POOL_GROUP_WIDTH, (gi + 1) * POOL_GROUP_WIDTH)
            s = ext[:, cs]
            sh = 1
            while sh < w:
                s = s + pltpu.roll(s, n - sh, 0)
                sh *= 2
            cnt = jnp.minimum(t + 1, w).astype(F32)
            du_ref[:, cs] = (s[0:T] - ext[0:T, cs] * cnt).astype(BF16)

    return pl.pallas_call(
        body, grid=(n_blk,),
        in_specs=[pl.BlockSpec((T, W), lambda i: (i, 0)),
                  pl.BlockSpec((HALO, W), lambda i: (jnp.minimum((i + 1) * per, S // HALO - 1), 0))],
        out_specs=pl.BlockSpec((T, W), lambda i: (i, 0)),
        out_shape=jax.ShapeDtypeStruct((S, W), BF16),
        scratch_shapes=[pltpu.VMEM((T + HALO, W), F32)],
        compiler_params=_params("parallel"), name="pool_bwd_b")(e, e)


def _slope(head):
    return 2.0 ** (-8.0 * (head + 1) / N_ATTN_HEADS)


def _attn_masks(b):
    row = lax.broadcasted_iota(jnp.int32, (SPAN, SPAN), 0)
    col = lax.broadcasted_iota(jnp.int32, (SPAN, SPAN), 1)
    dist_cur = (row - col).astype(F32)
    valid_cur = col <= row
    valid_prev = jnp.logical_and(col >= row, b > 0)
    return dist_cur, valid_cur, valid_prev


ATTN_BATCH = 4


def _attn_tiling(gi, S):
    d = ATTN_GROUPS[gi][1]
    hp = HEADS_PER_GROUP if d < 16 else 1
    return d, SPAN * d, S // (SPAN * d), hp, hp * HEAD_DIM, ATTN_BATCH // hp


_BATCHED_DN = {"nt": (((2,), (2,)), ((0,), (0,))), "nn": (((2,), (1,)), ((0,), (0,))), "tn": (((1,), (1,)), ((0,), (0,)))}


def _bdot(a, b, mode):
    return lax.dot_general(a.astype(BF16), b.astype(BF16), _BATCHED_DN[mode], preferred_element_type=F32)


def _batch_rows(it, d, rb):
    return [pl.ds(it * rb + j, SPAN, stride=d) for j in range(rb)]


def _take(scr, hp, rows):
    return jnp.stack([scr[hi, rw, :] for hi in range(hp) for rw in rows])


def _put(scr, hp, rows, val):
    for n, (hi, rw) in enumerate((hi, rw) for hi in range(hp) for rw in rows):
        scr[hi, rw, :] = val[n]


def _batch_bias(gi, d, hp, rb, dist):
    return jnp.stack([(-(_head_slope(gi, hp, hi) * d)) * dist for hi in range(hp) for _ in range(rb)])


def _head_slope(gi, hp, hi):
    if hp == HEADS_PER_GROUP:
        return _slope(gi * HEADS_PER_GROUP + hi)
    first = pl.program_id(1) * hp
    sl = jnp.float32(_slope(gi * HEADS_PER_GROUP + hi))
    for h0 in range(hp, HEADS_PER_GROUP, hp):
        sl = jnp.where(first == h0, jnp.float32(_slope(gi * HEADS_PER_GROUP + h0 + hi)), sl)
    return sl


def _split_heads(src_ref, dst_ref, hp):
    for hi in range(hp):
        dst_ref[hi] = src_ref[:, hi * HEAD_DIM:(hi + 1) * HEAD_DIM].astype(F32)


def _attn_fwd(gi, qkv):
    S = qkv.shape[0]
    d, C, nb, hp, W, rb = _attn_tiling(gi, S)
    scale = HEAD_DIM ** -0.5

    def body(q_ref, kc_ref, kp_ref, vc_ref, vp_ref, o_ref, lse_ref, qs, kcs, kps, vcs, vps, os, ls):
        b = pl.program_id(0)
        for src, dst in ((q_ref, qs), (kc_ref, kcs), (kp_ref, kps), (vc_ref, vcs), (vp_ref, vps)):
            _split_heads(src, dst, hp)
        dist_cur, valid_cur, valid_prev = _attn_masks(b)
        bias_cur = _batch_bias(gi, d, hp, rb, dist_cur)
        bias_far = _batch_bias(gi, d, hp, rb, dist_cur + SPAN)

        def sub_sequences(it, carry):
            rows = _batch_rows(it, d, rb)
            q = _take(qs, hp, rows)
            s_cur = jnp.where(valid_cur, _bdot(q, _take(kcs, hp, rows), "nt") * scale + bias_cur, NEG_BIG)
            s_prev = jnp.where(valid_prev, _bdot(q, _take(kps, hp, rows), "nt") * scale + bias_far, NEG_BIG)
            m = jnp.maximum(jnp.max(s_cur, axis=-1, keepdims=True), jnp.max(s_prev, axis=-1, keepdims=True))
            p_cur = jnp.exp(s_cur - m)
            p_prev = jnp.exp(s_prev - m)
            l = jnp.sum(p_cur, axis=-1, keepdims=True) + jnp.sum(p_prev, axis=-1, keepdims=True)
            o = (_bdot(p_cur, _take(vcs, hp, rows), "nn") + _bdot(p_prev, _take(vps, hp, rows), "nn")) / l
            _put(os, hp, rows, o)
            _put(ls, hp, rows, jnp.broadcast_to(m + jnp.log(l), o.shape))
            return carry

        lax.fori_loop(0, d // rb, sub_sequences, 0)
        for hi in range(hp):
            hs = slice(hi * HEAD_DIM, (hi + 1) * HEAD_DIM)
            o_ref[:, hs] = os[hi]
            lse_ref[:, hs] = ls[hi]

    per_sec = ATTN_WIDTH // W
    per_grp = ATTN_OUT_WIDTH // W

    def spec(sec, shift):
        return pl.BlockSpec((C, W), lambda b, h: (jnp.maximum(b + shift, 0), sec * per_sec + gi * per_grp + h))

    out = pl.BlockSpec((C, W), lambda b, h: (b, h))
    scr = pltpu.VMEM((hp, C, HEAD_DIM), F32)
    return pl.pallas_call(
        body, grid=(nb, HEADS_PER_GROUP // hp),
        in_specs=[spec(0, 0), spec(1, 0), spec(1, -1), spec(2, 0), spec(2, -1)], out_specs=[out, out],
        out_shape=[jax.ShapeDtypeStruct((S, ATTN_OUT_WIDTH), F32)] * 2, scratch_shapes=[scr] * 7,
        compiler_params=_params("parallel", "parallel"), name=f"attn_fwd_g{gi}")(qkv, qkv, qkv, qkv, qkv)


def _attn_bwd(gi, qkv, do, lse, dp):
    S = qkv.shape[0]
    d, C, nb, hp, W, rb = _attn_tiling(gi, S)
    scale = HEAD_DIM ** -0.5
    n_in = 12

    def body(*refs):
        ins, (dq_ref, dk_ref, dv_ref) = refs[:n_in], refs[n_in:n_in + 3]
        scr = refs[n_in + 3:]
        for src, dst in zip(ins, scr[:n_in]):
            _split_heads(src, dst, hp)
        qs, kcs, vcs, kps, vps, qns, dos, dons, lses, lsens, dps, dpns = scr[:n_in]
        dqs, dks, dvs = scr[n_in:]
        b = pl.program_id(0)
        dist_cur, valid_cur, valid_prev = _attn_masks(b)
        row = lax.broadcasted_iota(jnp.int32, (SPAN, SPAN), 0)
        col = lax.broadcasted_iota(jnp.int32, (SPAN, SPAN), 1)
        valid_next = jnp.logical_and(col >= row, b < nb - 1)

        bias_cur = _batch_bias(gi, d, hp, rb, dist_cur)
        bias_far = _batch_bias(gi, d, hp, rb, dist_cur + SPAN)

        def sub_sequences(it, carry):
            rows = _batch_rows(it, d, rb)
            q, kc, vc, kp, vp, qn, dov, don = (_take(s, hp, rows) for s in (qs, kcs, vcs, kps, vps, qns, dos, dons))
            lse_b, lse_n, dp_b, dp_n = (_take(s, hp, rows) for s in (lses, lsens, dps, dpns))
            p_cur = jnp.where(valid_cur, jnp.exp(_bdot(q, kc, "nt") * scale + bias_cur - lse_b), 0.0)
            p_prev = jnp.where(valid_prev, jnp.exp(_bdot(q, kp, "nt") * scale + bias_far - lse_b), 0.0)
            ds_cur = p_cur * (_bdot(dov, vc, "nt") - dp_b)
            ds_prev = p_prev * (_bdot(dov, vp, "nt") - dp_b)
            _put(dqs, hp, rows, (_bdot(ds_cur, kc, "nn") + _bdot(ds_prev, kp, "nn")) * scale)
            p_next = jnp.where(valid_next, jnp.exp(_bdot(qn, kc, "nt") * scale + bias_far - lse_n), 0.0)
            ds_next = p_next * (_bdot(don, vc, "nt") - dp_n)
            _put(dvs, hp, rows, _bdot(p_cur, dov, "tn") + _bdot(p_next, don, "tn"))
            _put(dks, hp, rows, (_bdot(ds_cur, q, "tn") + _bdot(ds_next, qn, "tn")) * scale)
            return carry

        lax.fori_loop(0, d // rb, sub_sequences, 0)
        for hi in range(hp):
            hs = slice(hi * HEAD_DIM, (hi + 1) * HEAD_DIM)
            dq_ref[:, hs] = dqs[hi].astype(BF16)
            dk_ref[:, hs] = dks[hi].astype(BF16)
            dv_ref[:, hs] = dvs[hi].astype(BF16)

    per_sec = ATTN_WIDTH // W
    per_grp = ATTN_OUT_WIDTH // W

    def qkv_spec(sec, shift):
        return pl.BlockSpec((C, W), lambda b, h: (jnp.clip(b + shift, 0, nb - 1), sec * per_sec + gi * per_grp + h))

    def act_spec(shift):
        return pl.BlockSpec((C, W), lambda b, h: (jnp.clip(b + shift, 0, nb - 1), h))

    out = act_spec(0)
    scr = pltpu.VMEM((hp, C, HEAD_DIM), F32)
    return pl.pallas_call(
        body, grid=(nb, HEADS_PER_GROUP // hp),
        in_specs=[qkv_spec(0, 0), qkv_spec(1, 0), qkv_spec(2, 0), qkv_spec(1, -1), qkv_spec(2, -1), qkv_spec(0, 1),
                  act_spec(0), act_spec(1), act_spec(0), act_spec(1), act_spec(0), act_spec(1)],
        out_specs=[out, out, out], out_shape=[jax.ShapeDtypeStruct((S, ATTN_OUT_WIDTH), BF16)] * 3,
        scratch_shapes=[scr] * (n_in + 3),
        compiler_params=_params("parallel", "parallel"), name=f"attn_bwd_g{gi}")(
            qkv, qkv, qkv, qkv, qkv, qkv, do, do, lse, lse, dp, dp)


def _group_weights(l_refs):
    l0, l1, l2 = (r[...] for r in l_refs)
    m = jnp.maximum(jnp.maximum(l0, l1), l2)
    e = [jnp.exp(l0 - m), jnp.exp(l1 - m), jnp.exp(l2 - m)]
    tot = e[0] + e[1] + e[2]
    return [ei / tot for ei in e]


def _merge_fwd(outs, lses):
    S, W = outs[0].shape
    T = 512

    def body(o0, o1, o2, l0, l1, l2, y_ref):
        w = _group_weights((l0, l1, l2))
        y_ref[...] = (w[0] * o0[...] + w[1] * o1[...] + w[2] * o2[...]).astype(BF16)

    blk = pl.BlockSpec((T, W), lambda i: (i, 0))
    return pl.pallas_call(
        body, grid=(S // T,), in_specs=[blk] * 6, out_specs=blk, out_shape=jax.ShapeDtypeStruct((S, W), BF16),
        compiler_params=_params("parallel"), name="merge_fwd")(*outs, *lses)


def _merge_bwd(dy, outs, lses):
    S, W = outs[0].shape
    T = 512

    def body(dy_ref, o0, o1, o2, l0, l1, l2, do0, do1, do2, dp0, dp1, dp2):
        w = _group_weights((l0, l1, l2))
        dyv = dy_ref[...]
        y = w[0] * o0[...] + w[1] * o1[...] + w[2] * o2[...]
        prod = dyv * y
        dots = []
        for hi in range(HEADS_PER_GROUP):
            hs = slice(hi * HEAD_DIM, (hi + 1) * HEAD_DIM)
            dots.append(jnp.broadcast_to(jnp.sum(prod[:, hs], axis=-1, keepdims=True), (T, HEAD_DIM)))
        dot_b = jnp.concatenate(dots, axis=1)
        for wg, do_ref, dp_ref in zip(w, (do0, do1, do2), (dp0, dp1, dp2)):
            do_ref[...] = (wg * dyv).astype(BF16)
            dp_ref[...] = wg * dot_b

    blk = pl.BlockSpec((T, W), lambda i: (i, 0))
    res = pl.pallas_call(
        body, grid=(S // T,), in_specs=[blk] * 7, out_specs=[blk] * 6,
        out_shape=[jax.ShapeDtypeStruct((S, W), BF16)] * 3 + [jax.ShapeDtypeStruct((S, W), F32)] * 3,
        compiler_params=_params("parallel"), name="merge_bwd")(dy, *outs, *lses)
    return res[:3], res[3:]


def _mix_fwd(pool_y, attn_y, w_pool_out_t, w_attn_out_t, gates_pre, b_gate):
    S = pool_y.shape[0]
    D = D_MODEL
    tm, tn = 1024, 512
    nj = D // tn

    def body(p_ref, a_ref, wp_ref, wa_ref, g0_ref, g1_ref, b0_ref, b1_ref, mixed_ref, yp_ref, ya_ref):
        yp = _dot(p_ref[...], wp_ref[...], "nt")
        ya = _dot(a_ref[...], wa_ref[...], "nt")
        g0 = jax.nn.sigmoid(g0_ref[...] + b0_ref[...])
        g1 = jax.nn.sigmoid(g1_ref[...] + b1_ref[...])
        mixed_ref[...] = (g0 * yp + g1 * ya).astype(BF16)
        yp_ref[...] = yp.astype(BF16)
        ya_ref[...] = ya.astype(BF16)

    out = pl.BlockSpec((tm, tn), lambda i, j: (i, j))
    return pl.pallas_call(
        body, grid=(S // tm, nj),
        in_specs=[pl.BlockSpec((tm, POOL_WIDTH), lambda i, j: (i, 0)),
                  pl.BlockSpec((tm, ATTN_OUT_WIDTH), lambda i, j: (i, 0)),
                  pl.BlockSpec((tn, POOL_WIDTH), lambda i, j: (j, 0)),
                  pl.BlockSpec((tn, ATTN_OUT_WIDTH), lambda i, j: (j, 0)),
                  pl.BlockSpec((tm, tn), lambda i, j: (i, j)), pl.BlockSpec((tm, tn), lambda i, j: (i, j + nj)),
                  pl.BlockSpec((1, tn), lambda i, j: (0, j)), pl.BlockSpec((1, tn), lambda i, j: (0, j + nj))],
        out_specs=[out, out, out], out_shape=[jax.ShapeDtypeStruct((S, D), BF16)] * 3,
        compiler_params=_params("parallel", "parallel"), name="mix_fwd")(
            pool_y, attn_y, w_pool_out_t, w_attn_out_t, gates_pre, gates_pre, b_gate, b_gate)


def _mix_bwd(dmixed, gates_pre, b_gate, y_pool, y_attn):
    S = dmixed.shape[0]
    D = D_MODEL
    T = 256

    def body(dm_ref, gp_ref, b_ref, yp_ref, ya_ref, dyp_ref, dya_ref, dg_ref, db_ref):
        i = pl.program_id(0)
        dm = dm_ref[...].astype(F32)
        parts = []
        for br, (y_ref, dy_ref) in enumerate(((yp_ref, dyp_ref), (ya_ref, dya_ref))):
            cs = slice(br * D, (br + 1) * D)
            g = jax.nn.sigmoid(gp_ref[:, cs] + b_ref[:, cs])
            dy_ref[...] = (dm * g).astype(BF16)
            dpre = dm * y_ref[...].astype(F32) * g * (1.0 - g)
            dg_ref[:, cs] = dpre.astype(BF16)
            parts.append(jnp.sum(dpre, axis=0, keepdims=True))

        @pl.when(i == 0)
        def _():
            db_ref[:, 0:D] = parts[0]
            db_ref[:, D:2 * D] = parts[1]

        @pl.when(i > 0)
        def _():
            db_ref[:, 0:D] += parts[0]
            db_ref[:, D:2 * D] += parts[1]

    row = pl.BlockSpec((T, D), lambda i: (i, 0))
    row2 = pl.BlockSpec((T, 2 * D), lambda i: (i, 0))
    vec2 = pl.BlockSpec((1, 2 * D), lambda i: (0, 0))
    return pl.pallas_call(
        body, grid=(S // T,), in_specs=[row, row2, vec2, row, row], out_specs=[row, row, row2, vec2],
        out_shape=[jax.ShapeDtypeStruct((S, D), BF16), jax.ShapeDtypeStruct((S, D), BF16),
                   jax.ShapeDtypeStruct((S, 2 * D), BF16), jax.ShapeDtypeStruct((1, 2 * D), F32)],
        compiler_params=_params("arbitrary"), name="mix_bwd")(dmixed, gates_pre, b_gate, y_pool, y_attn)


_CONV_T, _CONV_C = 512, 512


def _conv_taps(ext_ref, h, w_ref, b_ref, cols=slice(None)):
    ext = ext_ref[h, :, cols]
    x2 = pltpu.roll(ext, 2, 0)[HALO:]
    x1 = pltpu.roll(ext, 1, 0)[HALO:]
    x0 = ext[HALO:]
    y = b_ref[:, cols] + w_ref[0:1, cols] * x2
    y = y + w_ref[1:2, cols] * x1
    y = y + w_ref[2:3, cols] * x0
    return y, (x2, x1, x0)


def _gelu_parts(a):
    cdf = 0.5 * (1.0 + lax.erf(a * (1.0 / math.sqrt(2.0))))
    return cdf, a * cdf


def _conv_specs(S):
    T, C = _CONV_T, _CONV_C
    nj = D_FF // C
    blk = pl.BlockSpec((2, T, C), lambda i, j: (0, i, j))
    per = T // HALO
    halo = pl.BlockSpec((2, HALO, C), lambda i, j: (0, jnp.maximum(i * per - 1, 0), j))
    w_a = pl.BlockSpec((3, C), lambda i, j: (0, j))
    w_b = pl.BlockSpec((3, C), lambda i, j: (0, j + nj))
    b_a = pl.BlockSpec((1, C), lambda i, j: (0, j))
    b_b = pl.BlockSpec((1, C), lambda i, j: (0, j + nj))
    return T, C, nj, blk, halo, w_a, w_b, b_a, b_b


def _conv_load(ext_ref, halo_ref, up_ref, first):
    for h in range(2):
        ext_ref[h, 0:HALO, :] = jnp.where(first, 0.0, halo_ref[h].astype(F32))
        ext_ref[h, HALO:, :] = up_ref[h].astype(F32)


def _up_proj_convffn(h, w_up_t, conv_w, conv_b):
    S = h.shape[0]
    tm, C = 1024, _CONV_C
    nj = D_FF // C

    def body(h_ref, wa_ref, wb_ref, cwa_ref, cwb_ref, cba_ref, cbb_ref, up_ref, act_ref, ext_ref, carry_ref):
        i, j = pl.program_id(0), pl.program_id(1)
        for half, w_ref in enumerate((wa_ref, wb_ref)):
            pre = _dot(h_ref[...], w_ref[...], "nt").astype(BF16)
            up_ref[half] = pre
            ext_ref[half, 0:HALO, :] = jnp.where(i == 0, 0.0, carry_ref[j, half])
            ext_ref[half, HALO:, :] = pre.astype(F32)
            carry_ref[j, half] = ext_ref[half, tm:tm + HALO, :]
        a, _ = _conv_taps(ext_ref, 0, cwa_ref, cba_ref)
        b, _ = _conv_taps(ext_ref, 1, cwb_ref, cbb_ref)
        _, gelu = _gelu_parts(a)
        act_ref[...] = (gelu * b).astype(BF16)

    def w_spec(half):
        return pl.BlockSpec((C, D_MODEL), lambda i, j: (j + half * nj, 0))

    def c_spec(rows, half):
        return pl.BlockSpec((rows, C), lambda i, j: (0, j + half * nj))

    return pl.pallas_call(
        body, grid=(S // tm, nj),
        in_specs=[pl.BlockSpec((tm, D_MODEL), lambda i, j: (i, 0)), w_spec(0), w_spec(1),
                  c_spec(3, 0), c_spec(3, 1), c_spec(1, 0), c_spec(1, 1)],
        out_specs=[pl.BlockSpec((2, tm, C), lambda i, j: (0, i, j)), pl.BlockSpec((tm, C), lambda i, j: (i, j))],
        out_shape=[jax.ShapeDtypeStruct((2, S, D_FF), BF16), jax.ShapeDtypeStruct((S, D_FF), BF16)],
        scratch_shapes=[pltpu.VMEM((2, HALO + tm, C), F32), pltpu.VMEM((nj, 2, HALO, C), F32)],
        compiler_params=_params("arbitrary", "arbitrary"), name="up_proj_convffn")(
            h, w_up_t, w_up_t, conv_w, conv_w, conv_b, conv_b)


def _convffn_bwd_a(up_pre, dx, w_down, conv_w, conv_b):
    S = up_pre.shape[1]
    T, C, nj, blk, halo, w_a, w_b, b_a, b_b = _conv_specs(S)
    def swap(spec):
        return pl.BlockSpec(spec.block_shape, lambda j, i, _m=spec.index_map: _m(i, j))

    def body(up_ref, halo_ref, dx_ref, wd_ref, wa_ref, wb_ref, ba_ref, bb_ref, dup_ref, dwa_ref, dwb_ref, dba_ref,
             dbb_ref, ext_ref):
        i = pl.program_id(1)
        _conv_load(ext_ref, halo_ref, up_ref, i == 0)

        @pl.when(i == 0)
        def _():
            for acc_ref in (dwa_ref, dwb_ref, dba_ref, dbb_ref):
                acc_ref[...] = jnp.zeros_like(acc_ref)

        for c in range(C // MXU_DIM):
            cs = slice(c * MXU_DIM, (c + 1) * MXU_DIM)
            dact = _dot(dx_ref[...], wd_ref[cs, :], "nt")
            a, xa = _conv_taps(ext_ref, 0, wa_ref, ba_ref, cs)
            b, xb = _conv_taps(ext_ref, 1, wb_ref, bb_ref, cs)
            cdf, gelu = _gelu_parts(a)
            dgelu = cdf + a * (jnp.exp(-0.5 * a * a) * (1.0 / math.sqrt(2.0 * math.pi)))
            da = dact * b * dgelu
            db = dact * gelu
            dup_ref[0, :, cs] = da.astype(BF16)
            dup_ref[1, :, cs] = db.astype(BF16)
            for dval, xs, dw_ref, dbias_ref in ((da, xa, dwa_ref, dba_ref), (db, xb, dwb_ref, dbb_ref)):
                dw_ref[:, cs] += jnp.concatenate([jnp.sum(dval * xk, axis=0, keepdims=True) for xk in xs], axis=0)
                dbias_ref[:, cs] += jnp.sum(dval, axis=0, keepdims=True)

    dx_spec = pl.BlockSpec((T, D_MODEL), lambda j, i: (i, 0))
    wd_spec = pl.BlockSpec((C, D_MODEL), lambda j, i: (j, 0))
    dw_spec = pl.BlockSpec((3, C), lambda j, i: (0, j))
    db_spec = pl.BlockSpec((1, C), lambda j, i: (0, j))
    res = pl.pallas_call(
        body, grid=(nj, S // T),
        in_specs=[swap(blk), swap(halo), dx_spec, wd_spec, swap(w_a), swap(w_b), swap(b_a), swap(b_b)],
        out_specs=[swap(blk), dw_spec, dw_spec, db_spec, db_spec],
        out_shape=[jax.ShapeDtypeStruct((2, S, D_FF), BF16), jax.ShapeDtypeStruct((3, D_FF), F32),
                   jax.ShapeDtypeStruct((3, D_FF), F32), jax.ShapeDtypeStruct((1, D_FF), F32),
                   jax.ShapeDtypeStruct((1, D_FF), F32)],
        scratch_shapes=[pltpu.VMEM((2, HALO + T, C), F32)],
        compiler_params=_params("parallel", "arbitrary"), name="convffn_bwd_a")(
            up_pre, up_pre, dx, w_down, conv_w, conv_w, conv_b, conv_b)
    dup, dwa, dwb, dba, dbb = res
    dconv_w = jnp.concatenate([dwa, dwb], axis=1)
    dconv_b = jnp.concatenate([dba, dbb], axis=1)
    return dup, dconv_w, dconv_b


def _convffn_bwd_b(dup, conv_w, w_up_t):
    S = dup.shape[1]
    tm, tk = 512, D_FF // 4
    per_half = D_FF // tk
    nk = 2 * per_half
    per = tm // HALO
    n_blk = S // tm

    def body(d_ref, halo_ref, cw_ref, w_ref, dpre_ref, dh_ref, ext_ref, acc_ref):
        i, k = pl.program_id(0), pl.program_id(1)
        n = tm + HALO
        ext_ref[0:tm, :] = d_ref[...].astype(F32)
        ext_ref[tm:, :] = jnp.where(i == n_blk - 1, 0.0, halo_ref[...].astype(F32))
        ext = ext_ref[...]
        y = cw_ref[2:3, :] * ext[0:tm] + cw_ref[1:2, :] * pltpu.roll(ext, n - 1, 0)[0:tm]
        y = y + cw_ref[0:1, :] * pltpu.roll(ext, n - 2, 0)[0:tm]
        pre = y.astype(BF16)
        dpre_ref[...] = pre
        prod = _dot(pre, w_ref[...], "nn")

        @pl.when(k == 0)
        def _():
            acc_ref[...] = prod

        @pl.when(k > 0)
        def _():
            acc_ref[...] += prod

        @pl.when(k == nk - 1)
        def _():
            dh_ref[...] = acc_ref[...]

    blk = pl.BlockSpec((None, tm, tk), lambda i, k: (k // per_half, i, k % per_half))
    halo = pl.BlockSpec((None, HALO, tk),
                        lambda i, k: (k // per_half, jnp.minimum((i + 1) * per, S // HALO - 1), k % per_half))
    return pl.pallas_call(
        body, grid=(n_blk, nk),
        in_specs=[blk, halo, pl.BlockSpec((3, tk), lambda i, k: (0, k)), pl.BlockSpec((tk, D_MODEL), lambda i, k: (k, 0))],
        out_specs=[blk, pl.BlockSpec((tm, D_MODEL), lambda i, k: (i, 0))],
        out_shape=[jax.ShapeDtypeStruct((2, S, D_FF), BF16), jax.ShapeDtypeStruct((S, D_MODEL), F32)],
        scratch_shapes=[pltpu.VMEM((tm + HALO, tk), F32), pltpu.VMEM((tm, D_MODEL), F32)],
        compiler_params=_params("parallel", "arbitrary"), name="convffn_bwd_b")(dup, dup, conv_w, w_up_t)


def _position():
    return lax.axis_index("x"), lax.axis_index("y"), lax.axis_index("c")


def _flip(pos, k):
    x, y, c = pos
    return (1 - x if k & 4 else x, 1 - y if k & 2 else y, 1 - c if k & 1 else c)


def _index(pos):
    return 4 * pos[0] + 2 * pos[1] + pos[2]


_HBM = pl.BlockSpec(memory_space=pltpu.HBM)
_SEM = pl.BlockSpec(memory_space=pltpu.SEMAPHORE)
_ANY = pl.BlockSpec(memory_space=pl.ANY)
_EFFECT = pltpu.SideEffectType.DATAFLOW_SIDE_EFFECTING


def _in_hbm(a):
    return pltpu.with_memory_space_constraint(a, pltpu.HBM)


def _split_copies(name, srcs, lands, n_copies, plan, start_after, wait_after):
    ns, nl = len(srcs), len(lands)
    start_after = tuple(start_after)
    na = len(start_after)

    def start_body(*refs):
        src_refs, land_refs = refs[:ns], refs[ns:ns + nl]
        send_sems, recv_sems = refs[ns + nl + na], refs[ns + nl + na + 1]
        token = refs[-1]
        for k, (src, dst, _, peer) in enumerate(plan(src_refs, land_refs)):
            pltpu.make_async_remote_copy(src_ref=src, dst_ref=dst, send_sem=send_sems.at[k], recv_sem=recv_sems.at[k],
                                         device_id=peer, device_id_type=MESH).start()
        token[...] = jnp.zeros_like(token)

    thru = [pltpu.HBM(a.shape, a.dtype) for a in list(srcs) + list(lands)]
    res = pl.pallas_call(
        start_body, name=name + "_start",
        out_shape=[pltpu.SemaphoreType.DMA((n_copies,)), pltpu.SemaphoreType.DMA((n_copies,))] + thru
        + [jax.ShapeDtypeStruct((8, 128), F32)],
        in_specs=[_HBM] * (ns + nl) + [_ANY] * na,
        out_specs=[_SEM, _SEM] + [_HBM] * (ns + nl) + [pl.BlockSpec(memory_space=pltpu.VMEM)],
        input_output_aliases={i: 2 + i for i in range(ns + nl)},
        compiler_params=pltpu.CompilerParams(has_side_effects=_EFFECT),
    )(*[_in_hbm(a) for a in list(srcs) + list(lands)], *start_after)
    send_sems, recv_sems, token = res[0], res[1], res[-1]
    thru_vals = res[2:-1]

    def wait():
        def wait_body(*refs):
            src_refs, land_refs = refs[:ns], refs[ns:ns + nl]
            s_sems, r_sems = refs[ns + nl], refs[ns + nl + 1]
            for k, (src, _, dst, peer) in enumerate(plan(src_refs, land_refs)):
                cp = pltpu.make_async_remote_copy(src_ref=src, dst_ref=dst, send_sem=s_sems.at[k], recv_sem=r_sems.at[k],
                                                  device_id=peer, device_id_type=MESH)
                cp.wait_send()
                cp.wait_recv()

        out = pl.pallas_call(
            wait_body, name=name + "_wait", out_shape=thru,
            in_specs=[_HBM] * (ns + nl) + [_SEM, _SEM, _ANY], out_specs=[_HBM] * (ns + nl),
            input_output_aliases={i: i for i in range(ns + nl)},
            compiler_params=pltpu.CompilerParams(has_side_effects=_EFFECT),
        )(*thru_vals, send_sems, recv_sems, wait_after())
        return list(out[:ns]), list(out[ns:])

    return wait, token


def _gather_level1_plan(n):
    def plan(src_refs, land_refs):
        me = _position()
        x, y, c = me
        peers = [(x, y, 1 - c), (1 - x, y, c), (x, 1 - y, c), (1 - x, 1 - y, c)]
        return [(src_refs[a], land_refs[a].at[_index(me)], land_refs[a].at[_index(p)], p)
                for a in range(n) for p in peers]
    return plan


def _gather_level2(lands):
    n = len(lands)

    def body(*refs):
        lin = refs[:n]
        send_sems, recv_sems = refs[2 * n:]
        x, y, c = _position()
        sibling = (x, y, 1 - c)
        chips = [(1 - x, y), (x, 1 - y), (1 - x, 1 - y)]

        def copy(a, j, core):
            slot = lin[a].at[_index((*chips[j], core))]
            return pltpu.make_async_remote_copy(
                src_ref=slot, dst_ref=slot, send_sem=send_sems.at[3 * a + j], recv_sem=recv_sems.at[3 * a + j],
                device_id=sibling, device_id_type=MESH)

        sent = [copy(a, j, c) for a in range(n) for j in range(3)]
        for cp in sent:
            cp.start()
        for a in range(n):
            for j in range(3):
                copy(a, j, 1 - c).wait_recv()
        for cp in sent:
            cp.wait_send()

    return pl.pallas_call(
        body, out_shape=[jax.ShapeDtypeStruct(l.shape, l.dtype) for l in lands],
        in_specs=[_ANY] * n, out_specs=[_ANY] * n, input_output_aliases={a: a for a in range(n)},
        scratch_shapes=[pltpu.SemaphoreType.DMA((3 * n,)), pltpu.SemaphoreType.DMA((3 * n,))],
        name="gather_level2")(*lands)


def _gather_direct_plan(n):
    def plan(src_refs, land_refs):
        me = _position()
        peers = [_flip(me, k) for k in range(1, N_DEV)]
        return [(src_refs[a], land_refs[a].at[_index(me)], land_refs[a].at[_index(p)], p)
                for a in range(n) for p in peers]
    return plan


def _place_own(shards):
    me = _index(_position())
    return [lax.dynamic_update_slice(lax.empty((N_DEV,) + s.shape, s.dtype), s[None], (me, 0, 0)) for s in shards]


def _exchange_direct_plan(n):
    def plan(src_refs, land_refs):
        me = _position()
        peers = [_flip(me, k) for k in range(1, N_DEV)]
        return [(src_refs[a].at[_index(p)], land_refs[a].at[_index(me)], land_refs[a].at[_index(p)], p)
                for a in range(n) for p in peers]
    return plan


def _adamw(w, g, m, v):
    m = ADAM_B1 * m + (1.0 - ADAM_B1) * g
    v = ADAM_B2 * v + (1.0 - ADAM_B2) * (g * g)
    m_hat = m / (1.0 - ADAM_B1 ** ADAM_STEP)
    v_hat = v / (1.0 - ADAM_B2 ** ADAM_STEP)
    delta = -ADAM_LR * (m_hat / (jnp.sqrt(v_hat) + ADAM_EPS) + ADAM_WD * w)
    return delta, m, v


def _row_tile(rows, unit, cap=256):
    best = rows
    for t in range(unit, min(rows, cap) + 1, unit):
        if rows % t == 0:
            best = t
    return best


def _sum_partials(name, recv):
    n_src, R, C = recv.shape
    T = _row_tile(R, 16)

    def body(r_ref, g_out):
        g = r_ref[0].astype(F32)
        for s in range(1, n_src):
            g = g + r_ref[s].astype(F32)
        g_out[...] = g

    return pl.pallas_call(
        body, grid=(R // T,), in_specs=[pl.BlockSpec((n_src, T, C), lambda i: (0, i, 0))],
        out_specs=pl.BlockSpec((T, C), lambda i: (i, 0)), out_shape=jax.ShapeDtypeStruct((R, C), F32),
        compiler_params=_params("parallel"), name=name)(recv)


def _adamw_update(name, w, g, m, v):
    R, C = w.shape
    T = _row_tile(R, 8)

    def body(w_ref, g_ref, m_ref, v_ref, d_out, m_out, v_out):
        delta, mn, vn = _adamw(w_ref[...], g_ref[...], m_ref[...], v_ref[...])
        d_out[...] = delta
        m_out[...] = mn
        v_out[...] = vn

    blk = pl.BlockSpec((T, C), lambda i: (i, 0))
    return pl.pallas_call(
        body, grid=(R // T,), in_specs=[blk] * 4, out_specs=[blk] * 3,
        out_shape=[jax.ShapeDtypeStruct((R, C), F32)] * 3, compiler_params=_params("parallel"), name=name)(w, g, m, v)


def _gather_vmem(p_ref, gath, send_sems, recv_sems):
    me = _position()
    gath[_index(me)] = p_ref[...]
    copies = []
    for k in range(1, N_DEV):
        peer = _flip(me, k)
        copies.append(pltpu.make_async_remote_copy(
            src_ref=p_ref, dst_ref=gath.at[_index(me)], send_sem=send_sems.at[k - 1],
            recv_sem=recv_sems.at[k - 1], device_id=peer, device_id_type=MESH))
    for cp in copies:
        cp.start()
    for k in range(1, N_DEV):
        peer = _flip(me, k)
        pltpu.make_async_remote_copy(
            src_ref=p_ref, dst_ref=gath.at[_index(peer)], send_sem=send_sems.at[k - 1],
            recv_sem=recv_sems.at[k - 1], device_id=peer, device_id_type=MESH).wait_recv()
    for cp in copies:
        cp.wait_send()


def _all_gather_small(p):
    R, C = p.shape

    def body(p_ref, out_ref, send_sems, recv_sems):
        _gather_vmem(p_ref, out_ref, send_sems, recv_sems)

    vm = pl.BlockSpec(memory_space=pltpu.VMEM)
    return pl.pallas_call(
        body, in_specs=[vm], out_specs=vm, out_shape=jax.ShapeDtypeStruct((N_DEV, R, C), F32),
        scratch_shapes=[pltpu.SemaphoreType.DMA((7,)), pltpu.SemaphoreType.DMA((7,))],
        name="all_gather_small")(p)


def _small_allreduce_adamw(part, w, m, v):
    R, C = part.shape

    def body(p_ref, w_ref, m_ref, v_ref, g_out, d_out, m_out, v_out, gath, send_sems, recv_sems):
        _gather_vmem(p_ref, gath, send_sems, recv_sems)
        g = gath[0]
        for s in range(1, N_DEV):
            g = g + gath[s]
        delta, mn, vn = _adamw(w_ref[...], g, m_ref[...], v_ref[...])
        g_out[...] = g
        d_out[...] = delta
        m_out[...] = mn
        v_out[...] = vn

    vm = pl.BlockSpec(memory_space=pltpu.VMEM)
    return pl.pallas_call(
        body, in_specs=[vm] * 4, out_specs=[vm] * 4, out_shape=[jax.ShapeDtypeStruct((R, C), F32)] * 4,
        scratch_shapes=[pltpu.VMEM((N_DEV, R, C), F32), pltpu.SemaphoreType.DMA((7,)), pltpu.SemaphoreType.DMA((7,))],
        name="small_allreduce_adamw")(part, w, m, v)


_BIG = ("w_in", "w_up", "w_down", "w_out", "w_pool_out", "w_attn_out")
_TRANSPOSED = ("w_in", "w_up", "w_pool_out", "w_attn_out")
_SMALL = (("g_mix", 2048), ("b_gate", 4096), ("pool_scale", 1024), ("g_ffn", 2048), ("conv_b", 11264), ("g_final", 2048))
CONV_ROWS = 8


def _to_comm(name, shard):
    return shard.T if name in _TRANSPOSED else shard


def _pack_small(vals):
    flat = [vals[n].reshape(-1) for n, _ in _SMALL]
    used = sum(f.shape[0] for f in flat)
    flat.append(jnp.zeros((SMALL_ROWS * PACK_COLS - used,), F32))
    return jnp.concatenate(flat).reshape(SMALL_ROWS, PACK_COLS)


def _unpack_small(packed, like):
    flat = packed.reshape(-1)
    out, off = {}, 0
    for name, n in _SMALL:
        out[name] = flat[off:off + n].reshape(like[name].shape)
        off += n
    return out


def _local_step(x, target, W, small, first_token=0.0, first_weights=None, late_weights=None, emit=None):
    S = x.shape[0]
    o_qkv, o_gate = POOL_WIDTH, POOL_WIDTH + QKV_WIDTH
    if emit is None:
        emit = lambda grads: None

    def zero(token):
        return 0.0 if token is None else token[0, 0]

    h1 = _rms_fwd("rms1_fwd", x, small["g_mix"] + first_token)
    tok = None
    if first_weights is not None:
        w_first, tok = first_weights(h1)
        W = dict(W, **w_first)
    w_in_t = W["w_in"]
    u = _mm_nt("proj_u", h1, w_in_t, tm=2048, tn=512, tk=2048, o_dtype=F32, n_rows=POOL_WIDTH, b_row_off=0,
               after=tok)
    qkv = _mm_nt("proj_qkv", h1, w_in_t, tm=2048, tn=512, tk=2048, o_dtype=BF16, n_rows=QKV_WIDTH, b_row_off=o_qkv)
    gates_pre = _mm_nt("proj_gates", h1, w_in_t, tm=2048, tn=512, tk=2048, o_dtype=F32, n_rows=2 * D_MODEL,
                       b_row_off=o_gate)
    pool_y = _pool_fwd(u, W["w_pool_lin"], small["pool_scale"])
    outs, lses = zip(*[_attn_fwd(gi, qkv) for gi in range(3)])
    attn_y = _merge_fwd(outs, lses)
    if late_weights is not None:
        W = dict(W, **late_weights(0, attn_y))
    w_up_t = W["w_up"]
    mixed, y_pool, y_attn = _mix_fwd(pool_y, attn_y, W["w_pool_out"], W["w_attn_out"], gates_pre, small["b_gate"])
    x1 = _mm_nn("out_proj", mixed, W["w_out"], tm=1024, tn=1024, tk=2048, o_dtype=F32, res=x)

    h2 = _rms_fwd("rms2_fwd", x1, small["g_ffn"])
    nj = D_FF // 512
    up_pre, act = _up_proj_convffn(h2, w_up_t, small["conv_w_full"], small["conv_b"])
    if late_weights is not None:
        W = dict(W, **late_weights(1, act))
    x2 = _mm_nn("down_proj", act, W["w_down"], tm=1024, tn=1024, tk=D_FF // 2, o_dtype=F32, res=x1)

    dx2, dx2_b, dg_final, sq_cols = _final_loss("final_loss", x2, small["g_final"], target)

    dw_down = _mm_tn("dw_down", act, dx2_b, tm=512, tn=2048, tk=TOKEN_TK)
    tok = emit(dict(w_down=dw_down))
    dup, dconv_w, dconv_b = _convffn_bwd_a(up_pre, dx2_b, W["w_down"], small["conv_w_full"],
                                           small["conv_b"] + zero(tok))
    dup_pre, dh2 = _convffn_bwd_b(dup, small["conv_w_full"], w_up_t)
    dw_up = _mm("dw_up", dup_pre, h2, mode="tn", grid=(2 * nj, 1, S // TOKEN_TK),
                a_spec=pl.BlockSpec((None, TOKEN_TK, 512), lambda i, j, k: (i // nj, k, i % nj)),
                b_spec=pl.BlockSpec((TOKEN_TK, D_MODEL), lambda i, j, k: (k, 0)),
                o_spec=pl.BlockSpec((512, D_MODEL), lambda i, j, k: (i, 0)),
                o_shape=(2 * D_FF, D_MODEL), o_dtype=BF16, acc_shape=(512, D_MODEL))
    tok = emit(dict(w_up=dw_up, conv_w=dconv_w))
    dx1, dx1_b, dg_ffn = _rms_bwd("rms2_bwd", x1, small["g_ffn"] + zero(tok), dh2, dx2)

    dmixed = _mm_nt("d_mixed", dx1_b, W["w_out"], tm=1024, tn=1024, tk=2048, o_dtype=BF16)
    dw_out = _mm_tn("dw_out", mixed, dx1_b, tm=512, tn=2048, tk=TOKEN_TK)
    tok = emit(dict(w_out=dw_out))
    dy_pool, dy_attn, dgates, db_gate = _mix_bwd(dmixed, gates_pre, small["b_gate"] + zero(tok), y_pool, y_attn)
    dpool_y = _mm_nn("d_pool_y", dy_pool, W["w_pool_out"], tm=1024, tn=1024, tk=2048, o_dtype=F32)
    dw_pool_out = _mm_tn("dw_pool_out", dy_pool, pool_y, tm=1024, tn=1024, tk=TOKEN_TK)
    dattn_y = _mm_nn("d_attn_y", dy_attn, W["w_attn_out"], tm=1024, tn=512, tk=2048, o_dtype=F32)
    dw_attn_out = _mm_tn("dw_attn_out", dy_attn, attn_y, tm=1024, tn=512, tk=TOKEN_TK)
    e, dw_pool_lin, dpool_scale = _pool_bwd_a(u, dpool_y, W["w_pool_lin"], small["pool_scale"])
    du = _pool_bwd_b(e)
    dos, dps = _merge_bwd(dattn_y, outs, lses)
    dqkv = [_attn_bwd(gi, qkv, dos[gi], lses[gi], dps[gi]) for gi in range(3)]
    dproj = jnp.concatenate([du] + [dqkv[gi][sec] for sec in range(3) for gi in range(3)] + [dgates], axis=1)
    dw_in = _mm_tn("dw_in", dproj, h1, tm=512, tn=2048, tk=TOKEN_TK)
    tok = emit(dict(w_in=dw_in, w_pool_out=dw_pool_out, w_attn_out=dw_attn_out, w_pool_lin=dw_pool_lin))
    dh1 = _mm_nn("d_h1", dproj, w_in_t, tm=1024, tn=1024, tk=IN_WIDTH // 4, o_dtype=F32,
                 after=tok)
    grad_x, _, dg_mix = _rms_bwd("rms1_bwd", x, small["g_mix"] + zero(tok), dh1, dx1)

    grads = dict(w_in=dw_in, w_up=dw_up, w_down=dw_down, w_out=dw_out, w_pool_out=dw_pool_out,
                 w_attn_out=dw_attn_out, w_pool_lin=dw_pool_lin, conv_w=dconv_w)
    small_grads = dict(g_mix=dg_mix, b_gate=db_gate, pool_scale=dpool_scale, g_ffn=dg_ffn, conv_b=dconv_b,
                       g_final=dg_final)
    return sq_cols, grad_x, grads, small_grads


def kernel(x, g_mix, w_in, b_gate, w_pool_lin, pool_scale, w_pool_out, w_attn_out, w_out, g_ffn, w_up, conv_w, conv_b, w_down, g_final, loss_target, m_g_mix, m_w_in, m_b_gate, m_w_pool_lin, m_pool_scale, m_w_pool_out, m_w_attn_out, m_w_out, m_g_ffn, m_w_up, m_conv_w, m_conv_b, m_w_down, m_g_final, v_g_mix, v_w_in, v_b_gate, v_w_pool_lin, v_pool_scale, v_w_pool_out, v_w_attn_out, v_w_out, v_g_ffn, v_w_up, v_conv_w, v_conv_b, v_w_down, v_g_final):
    given = dict(g_mix=g_mix, w_in=w_in, b_gate=b_gate, w_pool_lin=w_pool_lin, pool_scale=pool_scale,
                 w_pool_out=w_pool_out, w_attn_out=w_attn_out, w_out=w_out, g_ffn=g_ffn, w_up=w_up, conv_w=conv_w,
                 conv_b=conv_b, w_down=w_down, g_final=g_final)
    mom_m = dict(g_mix=m_g_mix, w_in=m_w_in, b_gate=m_b_gate, w_pool_lin=m_w_pool_lin, pool_scale=m_pool_scale,
                 w_pool_out=m_w_pool_out, w_attn_out=m_w_attn_out, w_out=m_w_out, g_ffn=m_g_ffn, w_up=m_w_up,
                 conv_w=m_conv_w, conv_b=m_conv_b, w_down=m_w_down, g_final=m_g_final)
    mom_v = dict(g_mix=v_g_mix, w_in=v_w_in, b_gate=v_b_gate, w_pool_lin=v_w_pool_lin, pool_scale=v_pool_scale,
                 w_pool_out=v_w_pool_out, w_attn_out=v_w_attn_out, w_out=v_w_out, g_ffn=v_g_ffn, w_up=v_w_up,
                 conv_w=v_conv_w, conv_b=v_conv_b, w_down=v_w_down, g_final=v_g_final)
    def full(g):
        return g.reshape(N_DEV * g.shape[1], g.shape[2])

    shard = {n: _to_comm(n, given[n][0]).astype(BF16) for n in _BIG}
    W = {}
    lin_all = _all_gather_small(w_pool_lin[0].reshape(4 * 32, POOL_GROUP_WIDTH))
    W["w_pool_lin"] = jnp.moveaxis(lin_all.reshape(N_DEV, 4, 32, POOL_GROUP_WIDTH), 0, 1).reshape(
        4, POOL_GROUP_WIDTH, POOL_GROUP_WIDTH).astype(BF16)
    conv_w_all = _all_gather_small(jnp.pad(conv_w[0], ((0, CONV_ROWS - 3), (0, 0))))
    conv_w_full = jnp.moveaxis(conv_w_all[:, :3, :], 0, 1).reshape(3, 2 * D_FF)

    hold = {}
    w_in_wait, first_token = _split_copies(
        "gather_w_in", [shard["w_in"]], _place_own([shard["w_in"]]), 4, _gather_level1_plan(1),
        start_after=(lin_all, conv_w_all), wait_after=lambda: hold["h1"])

    stages = (("w_out", "w_pool_out", "w_attn_out", "w_up"), ("w_down",))
    stage_after = [None, None]
    stage_wait = []

    def first_weights(h1):
        hold["h1"] = h1
        w_in_full = full(_gather_level2(w_in_wait()[1])[0])
        order_after = (w_in_full,)
        for k, names in enumerate(stages):
            srcs = [shard[n] for n in names]
            wait, token = _split_copies(f"gather_stage{k}", srcs, _place_own(srcs), 7 * len(names),
                                        _gather_direct_plan(len(names)), start_after=order_after,
                                        wait_after=lambda k=k: stage_after[k])
            stage_wait.append(wait)
            order_after = (token,)
        return dict(w_in=w_in_full), token

    def late_weights(stage, after):
        stage_after[stage] = after
        return {n: full(l) for n, l in zip(stages[stage], stage_wait[stage]()[1])}

    small = dict(g_mix=g_mix, b_gate=b_gate, pool_scale=pool_scale, g_ffn=g_ffn, conv_b=conv_b,
                 g_final=g_final.reshape(1, D_MODEL), conv_w_full=conv_w_full)

    pending = []
    done = {}

    def emit(group):
        names = list(group)
        parts = []
        for n in names:
            g = group[n]
            if n == "w_pool_lin":
                g = jnp.moveaxis(g.reshape(4, N_DEV, 32, POOL_GROUP_WIDTH), 1, 0).reshape(
                    N_DEV, 4 * 32, POOL_GROUP_WIDTH).astype(BF16)
            elif n == "conv_w":
                g = jnp.pad(jnp.moveaxis(g.reshape(3, N_DEV, 2 * D_FF // N_DEV), 1, 0),
                            ((0, 0), (0, CONV_ROWS - 3), (0, 0))).astype(BF16)
            else:
                g = g.reshape(N_DEV, g.shape[0] // N_DEV, g.shape[1])
            parts.append(g)
        me = _index(_position())
        lands = [lax.dynamic_update_slice(lax.empty(p.shape, p.dtype), lax.dynamic_index_in_dim(p, me, 0),
                                          (me, 0, 0)) for p in parts]
        wait, token = _split_copies(
            "exchange_" + names[0], parts, lands, 7 * len(names), _exchange_direct_plan(len(names)),
            start_after=(), wait_after=lambda: done["grad_x"])
        pending.append((names, wait))
        return token

    sq_cols, grad_x, _, small_grads = _local_step(x[0], loss_target[0], W, small, first_token=first_token[0, 0],
                                                  first_weights=first_weights, late_weights=late_weights, emit=emit)
    done["grad_x"] = grad_x
    loss = lax.psum(0.5 / D_MODEL * jnp.sum(sq_cols), AXES)

    res = {}
    for n, r in [(n, r) for names, wait in pending for n, r in zip(names, wait()[1])]:
        g = _sum_partials("sum_" + n, r)
        if n in _TRANSPOSED:
            g = g.T
        elif n == "conv_w":
            g = g[:3]
        shape = given[n].shape
        rows = math.prod(shape[:-1])
        w2, m2, v2 = (d[n].reshape(rows, shape[-1]) for d in (given, mom_m, mom_v))
        upd = _adamw_update("adamw_" + n, w2, g.reshape(rows, shape[-1]), m2, v2)
        res[n] = [a.reshape(shape) for a in (g,) + tuple(upd)]

    packed_small = [_pack_small(d) for d in (given, mom_m, mom_v)]
    res_small = [_unpack_small(p, given) for p in _small_allreduce_adamw(_pack_small(small_grads), *packed_small)]

    order = ["g_mix", "w_in", "b_gate", "w_pool_lin", "pool_scale", "w_pool_out", "w_attn_out", "w_out", "g_ffn",
             "w_up", "conv_w", "conv_b", "w_down", "g_final"]
    outs = [loss, grad_x[None]]
    for kind in range(4):
        for n in order:
            outs.append(res[n][kind] if n in res else res_small[kind][n])
    return tuple(outs)
```

```python
import math

import jax
import jax.numpy as jnp
from jax import lax
from jax.experimental import pallas as pl
from jax.experimental.pallas import tpu as pltpu

F32 = jnp.float32
BF16 = jnp.bfloat16
MESH = pl.DeviceIdType.MESH
AXES = ("x", "y", "c")
N_DEV = 8

D_MODEL = 2048
POOL_WINDOWS = (2, 4, 8, 16)
POOL_GROUP_WIDTH = 256
POOL_WIDTH = 1024
ATTN_GROUPS = ((128, 1), (512, 4), (2048, 16))
SPAN = 128
HEADS_PER_GROUP = 4
N_ATTN_HEADS = 12
HEAD_DIM = 128
ATTN_WIDTH = 1536
QKV_WIDTH = 3 * ATTN_WIDTH
ATTN_OUT_WIDTH = 512
IN_WIDTH = 9728
D_FF = 5632
RMS_EPS = 1e-6
ADAM_LR, ADAM_B1, ADAM_B2, ADAM_EPS, ADAM_WD, ADAM_STEP = 0.001, 0.9, 0.999, 1e-08, 0.01, 10

VMEM_LIMIT_BYTES = 48 * 1024 * 1024
HALO = 16
TOKEN_TK = 2048
MXU_DIM = 256
PACK_COLS = 1024
SMALL_ROWS = 24
NEG_BIG = -1e30

_DN = {"nn": (((1,), (0,)), ((), ())), "nt": (((1,), (1,)), ((), ())), "tn": (((0,), (0,)), ((), ()))}


def _params(*sem):
    return pltpu.CompilerParams(dimension_semantics=sem, vmem_limit_bytes=VMEM_LIMIT_BYTES)


def _dot(a, b, mode):
    return lax.dot_general(a.astype(BF16), b.astype(BF16), _DN[mode], preferred_element_type=F32)


def _mm(name, a, b, *, mode, grid, a_spec, b_spec, o_spec, o_shape, o_dtype, acc_shape, res=None, res_spec=None,
        after=None):
    nk = grid[2]
    n_in = 2 + (res is not None) + (after is not None)

    def body(*refs):
        a_ref, b_ref = refs[:2]
        r_ref = refs[2] if res is not None else None
        o_ref = refs[n_in]
        scr = refs[n_in + 1:]
        prod = _dot(a_ref[...], b_ref[...], mode)

        def finish(val):
            if r_ref is not None:
                val = val + r_ref[...]
            o_ref[...] = val.astype(o_ref.dtype)

        if nk == 1:
            finish(prod)
        else:
            acc = scr[0]
            k = pl.program_id(2)

            @pl.when(k == 0)
            def _():
                acc[...] = prod

            @pl.when(k > 0)
            def _():
                acc[...] += prod

            @pl.when(k == nk - 1)
            def _():
                finish(acc[...])

    in_specs = [a_spec, b_spec]
    args = [a, b]
    if res is not None:
        in_specs.append(res_spec)
        args.append(res)
    if after is not None:
        in_specs.append(pl.BlockSpec((8, 128), lambda i, j, k: (0, 0)))
        args.append(after)
    return pl.pallas_call(
        body, grid=grid, in_specs=in_specs, out_specs=o_spec,
        out_shape=jax.ShapeDtypeStruct(o_shape, o_dtype),
        scratch_shapes=[] if nk == 1 else [pltpu.VMEM(acc_shape, F32)],
        compiler_params=_params("parallel", "parallel", "arbitrary"), name=name)(*args)


def _mm_nn(name, a, b, *, tm, tn, tk, o_dtype, n_cols=None, b_col_off=0, res=None, after=None):
    M, K = a.shape
    N = b.shape[1] if n_cols is None else n_cols
    off = b_col_off // tn
    return _mm(name, a, b, mode="nn", grid=(M // tm, N // tn, K // tk),
               a_spec=pl.BlockSpec((tm, tk), lambda i, j, k: (i, k)),
               b_spec=pl.BlockSpec((tk, tn), lambda i, j, k: (k, j + off)),
               o_spec=pl.BlockSpec((tm, tn), lambda i, j, k: (i, j)),
               o_shape=(M, N), o_dtype=o_dtype, acc_shape=(tm, tn), res=res,
               res_spec=pl.BlockSpec((tm, tn), lambda i, j, k: (i, j)), after=after)


def _mm_nt(name, a, b, *, tm, tn, tk, o_dtype, n_rows=None, b_row_off=0, after=None):
    M, K = a.shape
    N = b.shape[0] if n_rows is None else n_rows
    off = b_row_off // tn
    return _mm(name, a, b, mode="nt", grid=(M // tm, N // tn, K // tk),
               a_spec=pl.BlockSpec((tm, tk), lambda i, j, k: (i, k)),
               b_spec=pl.BlockSpec((tn, tk), lambda i, j, k: (j + off, k)),
               o_spec=pl.BlockSpec((tm, tn), lambda i, j, k: (i, j)),
               o_shape=(M, N), o_dtype=o_dtype, acc_shape=(tm, tn), after=after)


def _mm_tn(name, a, b, *, tm, tn, tk):
    K, M = a.shape
    N = b.shape[1]
    return _mm(name, a, b, mode="tn", grid=(M // tm, N // tn, K // tk),
               a_spec=pl.BlockSpec((tk, tm), lambda i, j, k: (k, i)),
               b_spec=pl.BlockSpec((tk, tn), lambda i, j, k: (k, j)),
               o_spec=pl.BlockSpec((tm, tn), lambda i, j, k: (i, j)),
               o_shape=(M, N), o_dtype=BF16, acc_shape=(tm, tn))


def _rms_fwd(name, x, g):
    S, D = x.shape
    T = 512

    def body(x_ref, g_ref, h_ref):
        xv = x_ref[...]
        r = lax.rsqrt(jnp.mean(xv * xv, axis=-1, keepdims=True) + RMS_EPS)
        h_ref[...] = (xv * r * g_ref[...]).astype(BF16)

    return pl.pallas_call(
        body, grid=(S // T,),
        in_specs=[pl.BlockSpec((T, D), lambda i: (i, 0)), pl.BlockSpec((1, D), lambda i: (0, 0))],
        out_specs=pl.BlockSpec((T, D), lambda i: (i, 0)),
        out_shape=jax.ShapeDtypeStruct((S, D), BF16), compiler_params=_params("parallel"), name=name)(x, g)


def _rms_bwd(name, x, g, dh, dres):
    S, D = x.shape
    T = 256

    def body(x_ref, g_ref, dh_ref, dres_ref, dx_ref, dxb_ref, dg_ref):
        i = pl.program_id(0)
        xv = x_ref[...]
        r = lax.rsqrt(jnp.mean(xv * xv, axis=-1, keepdims=True) + RMS_EPS)
        xhat = xv * r
        dhv = dh_ref[...].astype(F32)
        gdh = dhv * g_ref[...]
        c = jnp.mean(xhat * gdh, axis=-1, keepdims=True)
        dx = dres_ref[...] + r * (gdh - xhat * c)
        dx_ref[...] = dx
        dxb_ref[...] = dx.astype(BF16)
        part = jnp.sum(dhv * xhat, axis=0, keepdims=True)

        @pl.when(i == 0)
        def _():
            dg_ref[...] = part

        @pl.when(i > 0)
        def _():
            dg_ref[...] += part

    row = pl.BlockSpec((T, D), lambda i: (i, 0))
    vec = pl.BlockSpec((1, D), lambda i: (0, 0))
    return pl.pallas_call(
        body, grid=(S // T,), in_specs=[row, vec, row, row], out_specs=[row, row, vec],
        out_shape=[jax.ShapeDtypeStruct((S, D), F32), jax.ShapeDtypeStruct((S, D), BF16),
                   jax.ShapeDtypeStruct((1, D), F32)],
        compiler_params=_params("arbitrary"), name=name)(x, g, dh, dres)


def _final_loss(name, x, g, target):
    S, D = x.shape
    T = 256

    def body(x_ref, g_ref, t_ref, dx_ref, dxb_ref, dg_ref, sq_ref):
        i = pl.program_id(0)
        xv = x_ref[...]
        r = lax.rsqrt(jnp.mean(xv * xv, axis=-1, keepdims=True) + RMS_EPS)
        xhat = xv * r
        err = xhat * g_ref[...] - t_ref[...]
        dy = err * (1.0 / D)
        gdy = dy * g_ref[...]
        c = jnp.mean(xhat * gdy, axis=-1, keepdims=True)
        dx = r * (gdy - xhat * c)
        dx_ref[...] = dx
        dxb_ref[...] = dx.astype(BF16)
        dg_part = jnp.sum(dy * xhat, axis=0, keepdims=True)
        sq_part = jnp.sum(err * err, axis=0, keepdims=True)

        @pl.when(i == 0)
        def _():
            dg_ref[...] = dg_part
            sq_ref[...] = sq_part

        @pl.when(i > 0)
        def _():
            dg_ref[...] += dg_part
            sq_ref[...] += sq_part

    row = pl.BlockSpec((T, D), lambda i: (i, 0))
    vec = pl.BlockSpec((1, D), lambda i: (0, 0))
    return pl.pallas_call(
        body, grid=(S // T,), in_specs=[row, vec, row], out_specs=[row, row, vec, vec],
        out_shape=[jax.ShapeDtypeStruct((S, D), F32), jax.ShapeDtypeStruct((S, D), BF16),
                   jax.ShapeDtypeStruct((1, D), F32), jax.ShapeDtypeStruct((1, D), F32)],
        compiler_params=_params("arbitrary"), name=name)(x, g, target)


def _prev_halo_spec(T, C, col_map):
    per = T // HALO
    return pl.BlockSpec((HALO, C), lambda *g: (jnp.maximum(g[0] * per - 1, 0), col_map(*g)))


def _pool_window_sums(ext, T, i):
    t = i * T + lax.broadcasted_iota(jnp.int32, (T, 1), 0)
    outs = []
    cnts = []
    for gi, w in enumerate(POOL_WINDOWS):
        s = ext[:, gi * POOL_GROUP_WIDTH:(gi + 1) * POOL_GROUP_WIDTH]
        sh = 1
        while sh < w:
            s = s + pltpu.roll(s, sh, 0)
            sh *= 2
        cnt = jnp.minimum(t + 1, w).astype(F32)
        outs.append(s[HALO:] / cnt - ext[HALO:, gi * POOL_GROUP_WIDTH:(gi + 1) * POOL_GROUP_WIDTH])
        cnts.append(cnt)
    return outs, cnts


def _load_ext(ext_ref, halo_ref, blk_ref, first):
    ext_ref[0:HALO, :] = jnp.where(first, 0.0, halo_ref[...].astype(F32))
    ext_ref[HALO:, :] = blk_ref[...].astype(F32)


def _pool_fwd(u, w_lin, scale):
    S = u.shape[0]
    T = 512
    W = POOL_WIDTH

    def body(u_ref, halo_ref, wl_ref, sc_ref, y_ref, ext_ref):
        i = pl.program_id(0)
        _load_ext(ext_ref, halo_ref, u_ref, i == 0)
        pooled, _ = _pool_window_sums(ext_ref[...], T, i)
        for gi in range(4):
            z = _dot(pooled[gi], wl_ref[gi], "nn")
            cs = slice(gi * POOL_GROUP_WIDTH, (gi + 1) * POOL_GROUP_WIDTH)
            y_ref[:, cs] = (z * sc_ref[:, cs]).astype(BF16)

    return pl.pallas_call(
        body, grid=(S // T,),
        in_specs=[pl.BlockSpec((T, W), lambda i: (i, 0)), _prev_halo_spec(T, W, lambda i: 0),
                  pl.BlockSpec((4, 256, 256), lambda i: (0, 0, 0)), pl.BlockSpec((1, W), lambda i: (0, 0))],
        out_specs=pl.BlockSpec((T, W), lambda i: (i, 0)),
        out_shape=jax.ShapeDtypeStruct((S, W), BF16),
        scratch_shapes=[pltpu.VMEM((HALO + T, W), F32)],
        compiler_params=_params("parallel"), name="pool_fwd")(u, u, w_lin, scale)


def _pool_bwd_a(u, dy, w_lin, scale):
    S = u.shape[0]
    T = 512
    W = POOL_WIDTH

    def body(u_ref, halo_ref, dy_ref, wl_ref, sc_ref, e_ref, dwl_ref, dsc_ref, ext_ref):
        i = pl.program_id(0)
        _load_ext(ext_ref, halo_ref, u_ref, i == 0)
        pooled, cnts = _pool_window_sums(ext_ref[...], T, i)
        for gi in range(4):
            cs = slice(gi * POOL_GROUP_WIDTH, (gi + 1) * POOL_GROUP_WIDTH)
            z = _dot(pooled[gi], wl_ref[gi], "nn")
            dyg = dy_ref[:, cs]
            dsc = jnp.sum(dyg * z, axis=0, keepdims=True)
            dz = dyg * sc_ref[:, cs]
            dwl = _dot(pooled[gi], dz, "tn")
            dpooled = _dot(dz, wl_ref[gi], "nt")
            e_ref[:, cs] = dpooled / cnts[gi]

            @pl.when(i == 0)
            def _():
                dwl_ref[gi] = dwl
                dsc_ref[:, cs] = dsc

            @pl.when(i > 0)
            def _():
                dwl_ref[gi] += dwl
                dsc_ref[:, cs] += dsc

    return pl.pallas_call(
        body, grid=(S // T,),
        in_specs=[pl.BlockSpec((T, W), lambda i: (i, 0)), _prev_halo_spec(T, W, lambda i: 0),
                  pl.BlockSpec((T, W), lambda i: (i, 0)),
                  pl.BlockSpec((4, 256, 256), lambda i: (0, 0, 0)), pl.BlockSpec((1, W), lambda i: (0, 0))],
        out_specs=[pl.BlockSpec((T, W), lambda i: (i, 0)), pl.BlockSpec((4, 256, 256), lambda i: (0, 0, 0)),
                   pl.BlockSpec((1, W), lambda i: (0, 0))],
        out_shape=[jax.ShapeDtypeStruct((S, W), F32), jax.ShapeDtypeStruct((4, 256, 256), F32),
                   jax.ShapeDtypeStruct((1, W), F32)],
        scratch_shapes=[pltpu.VMEM((HALO + T, W), F32)],
        compiler_params=_params("arbitrary"), name="pool_bwd_a")(u, u, dy, w_lin, scale)


def _pool_bwd_b(e):
    S = e.shape[0]
    T = 512
    W = POOL_WIDTH
    per = T // HALO
    n_blk = S // T

    def body(e_ref, halo_ref, du_ref, ext_ref):
        i = pl.program_id(0)
        ext_ref[0:T, :] = e_ref[...]
        ext_ref[T:, :] = jnp.where(i == n_blk - 1, 0.0, halo_ref[...])
        ext = ext_ref[...]
        n = T + HALO
        t = i * T + lax.broadcasted_iota(jnp.int32, (T, 1), 0)
        for gi, w in enumerate(POOL_WINDOWS):
            cs = slice(gi * POOL_GROUP_WIDTH, (gi + 1) * POOL_GROUP_WIDTH)
            s = ext[:, cs]
            sh = 1
            while sh < w:
                s = s + pltpu.roll(s, n - sh, 0)
                sh *= 2
            cnt = jnp.minimum(t + 1, w).astype(F32)
            du_ref[:, cs] = (s[0:T] - ext[0:T, cs] * cnt).astype(BF16)

    return pl.pallas_call(
        body, grid=(n_blk,),
        in_specs=[pl.BlockSpec((T, W), lambda i: (i, 0)),
                  pl.BlockSpec((HALO, W), lambda i: (jnp.minimum((i + 1) * per, S // HALO - 1), 0))],
        out_specs=pl.BlockSpec((T, W), lambda i: (i, 0)),
        out_shape=jax.ShapeDtypeStruct((S, W), BF16),
        scratch_shapes=[pltpu.VMEM((T + HALO, W), F32)],
        compiler_params=_params("parallel"), name="pool_bwd_b")(e, e)


def _slope(head):
    return 2.0 ** (-8.0 * (head + 1) / N_ATTN_HEADS)


def _attn_masks(b):
    row = lax.broadcasted_iota(jnp.int32, (SPAN, SPAN), 0)
    col = lax.broadcasted_iota(jnp.int32, (SPAN, SPAN), 1)
    dist_cur = (row - col).astype(F32)
    valid_cur = col <= row
    valid_prev = jnp.logical_and(col >= row, b > 0)
    return dist_cur, valid_cur, valid_prev


ATTN_BATCH = 4


def _attn_tiling(gi, S):
    d = ATTN_GROUPS[gi][1]
    hp = HEADS_PER_GROUP if d < 16 else 1
    return d, SPAN * d, S // (SPAN * d), hp, hp * HEAD_DIM, ATTN_BATCH // hp


_BATCHED_DN = {"nt": (((2,), (2,)), ((0,), (0,))), "nn": (((2,), (1,)), ((0,), (0,))), "tn": (((1,), (1,)), ((0,), (0,)))}


def _bdot(a, b, mode):
    return lax.dot_general(a.astype(BF16), b.astype(BF16), _BATCHED_DN[mode], preferred_element_type=F32)


def _batch_rows(it, d, rb):
    return [pl.ds(it * rb + j, SPAN, stride=d) for j in range(rb)]


def _take(scr, hp, rows):
    return jnp.stack([scr[hi, rw, :] for hi in range(hp) for rw in rows])


def _put(scr, hp, rows, val):
    for n, (hi, rw) in enumerate((hi, rw) for hi in range(hp) for rw in rows):
        scr[hi, rw, :] = val[n]


def _batch_bias(gi, d, hp, rb, dist):
    return jnp.stack([(-(_head_slope(gi, hp, hi) * d)) * dist for hi in range(hp) for _ in range(rb)])


def _head_slope(gi, hp, hi):
    if hp == HEADS_PER_GROUP:
        return _slope(gi * HEADS_PER_GROUP + hi)
    first = pl.program_id(1) * hp
    sl = jnp.float32(_slope(gi * HEADS_PER_GROUP + hi))
    for h0 in range(hp, HEADS_PER_GROUP, hp):
        sl = jnp.where(first == h0, jnp.float32(_slope(gi * HEADS_PER_GROUP + h0 + hi)), sl)
    return sl


def _split_heads(src_ref, dst_ref, hp):
    for hi in range(hp):
        dst_ref[hi] = src_ref[:, hi * HEAD_DIM:(hi + 1) * HEAD_DIM].astype(F32)


def _attn_fwd(gi, qkv):
    S = qkv.shape[0]
    d, C, nb, hp, W, rb = _attn_tiling(gi, S)
    scale = HEAD_DIM ** -0.5

    def body(q_ref, kc_ref, kp_ref, vc_ref, vp_ref, o_ref, lse_ref, qs, kcs, kps, vcs, vps, os, ls):
        b = pl.program_id(0)
        for src, dst in ((q_ref, qs), (kc_ref, kcs), (kp_ref, kps), (vc_ref, vcs), (vp_ref, vps)):
            _split_heads(src, dst, hp)
        dist_cur, valid_cur, valid_prev = _attn_masks(b)
        bias_cur = _batch_bias(gi, d, hp, rb, dist_cur)
        bias_far = _batch_bias(gi, d, hp, rb, dist_cur + SPAN)

        def sub_sequences(it, carry):
            rows = _batch_rows(it, d, rb)
            q = _take(qs, hp, rows)
            s_cur = jnp.where(valid_cur, _bdot(q, _take(kcs, hp, rows), "nt") * scale + bias_cur, NEG_BIG)
            s_prev = jnp.where(valid_prev, _bdot(q, _take(kps, hp, rows), "nt") * scale + bias_far, NEG_BIG)
            m = jnp.maximum(jnp.max(s_cur, axis=-1, keepdims=True), jnp.max(s_prev, axis=-1, keepdims=True))
            p_cur = jnp.exp(s_cur - m)
            p_prev = jnp.exp(s_prev - m)
            l = jnp.sum(p_cur, axis=-1, keepdims=True) + jnp.sum(p_prev, axis=-1, keepdims=True)
            o = (_bdot(p_cur, _take(vcs, hp, rows), "nn") + _bdot(p_prev, _take(vps, hp, rows), "nn")) / l
            _put(os, hp, rows, o)
            _put(ls, hp, rows, jnp.broadcast_to(m + jnp.log(l), o.shape))
            return carry

        lax.fori_loop(0, d // rb, sub_sequences, 0)
        for hi in range(hp):
            hs = slice(hi * HEAD_DIM, (hi + 1) * HEAD_DIM)
            o_ref[:, hs] = os[hi]
            lse_ref[:, hs] = ls[hi]

    per_sec = ATTN_WIDTH // W
    per_grp = ATTN_OUT_WIDTH // W

    def spec(sec, shift):
        return pl.BlockSpec((C, W), lambda b, h: (jnp.maximum(b + shift, 0), sec * per_sec + gi * per_grp + h))

    out = pl.BlockSpec((C, W), lambda b, h: (b, h))
    scr = pltpu.VMEM((hp, C, HEAD_DIM), F32)
    return pl.pallas_call(
        body, grid=(nb, HEADS_PER_GROUP // hp),
        in_specs=[spec(0, 0), spec(1, 0), spec(1, -1), spec(2, 0), spec(2, -1)], out_specs=[out, out],
        out_shape=[jax.ShapeDtypeStruct((S, ATTN_OUT_WIDTH), F32)] * 2, scratch_shapes=[scr] * 7,
        compiler_params=_params("parallel", "parallel"), name=f"attn_fwd_g{gi}")(qkv, qkv, qkv, qkv, qkv)


def _attn_bwd(gi, qkv, do, lse, dp):
    S = qkv.shape[0]
    d, C, nb, hp, W, rb = _attn_tiling(gi, S)
    scale = HEAD_DIM ** -0.5
    n_in = 12

    def body(*refs):
        ins, (dq_ref, dk_ref, dv_ref) = refs[:n_in], refs[n_in:n_in + 3]
        scr = refs[n_in + 3:]
        for src, dst in zip(ins, scr[:n_in]):
            _split_heads(src, dst, hp)
        qs, kcs, vcs, kps, vps, qns, dos, dons, lses, lsens, dps, dpns = scr[:n_in]
        dqs, dks, dvs = scr[n_in:]
        b = pl.program_id(0)
        dist_cur, valid_cur, valid_prev = _attn_masks(b)
        row = lax.broadcasted_iota(jnp.int32, (SPAN, SPAN), 0)
        col = lax.broadcasted_iota(jnp.int32, (SPAN, SPAN), 1)
        valid_next = jnp.logical_and(col >= row, b < nb - 1)

        bias_cur = _batch_bias(gi, d, hp, rb, dist_cur)
        bias_far = _batch_bias(gi, d, hp, rb, dist_cur + SPAN)

        def sub_sequences(it, carry):
            rows = _batch_rows(it, d, rb)
            q, kc, vc, kp, vp, qn, dov, don = (_take(s, hp, rows) for s in (qs, kcs, vcs, kps, vps, qns, dos, dons))
            lse_b, lse_n, dp_b, dp_n = (_take(s, hp, rows) for s in (lses, lsens, dps, dpns))
            p_cur = jnp.where(valid_cur, jnp.exp(_bdot(q, kc, "nt") * scale + bias_cur - lse_b), 0.0)
            p_prev = jnp.where(valid_prev, jnp.exp(_bdot(q, kp, "nt") * scale + bias_far - lse_b), 0.0)
            ds_cur = p_cur * (_bdot(dov, vc, "nt") - dp_b)
            ds_prev = p_prev * (_bdot(dov, vp, "nt") - dp_b)
            _put(dqs, hp, rows, (_bdot(ds_cur, kc, "nn") + _bdot(ds_prev, kp, "nn")) * scale)
            p_next = jnp.where(valid_next, jnp.exp(_bdot(qn, kc, "nt") * scale + bias_far - lse_n), 0.0)
            ds_next = p_next * (_bdot(don, vc, "nt") - dp_n)
            _put(dvs, hp, rows, _bdot(p_cur, dov, "tn") + _bdot(p_next, don, "tn"))
            _put(dks, hp, rows, (_bdot(ds_cur, q, "tn") + _bdot(ds_next, qn, "tn")) * scale)
            return carry

        lax.fori_loop(0, d // rb, sub_sequences, 0)
        for hi in range(hp):
            hs = slice(hi * HEAD_DIM, (hi + 1) * HEAD_DIM)
            dq_ref[:, hs] = dqs[hi].astype(BF16)
            dk_ref[:, hs] = dks[hi].astype(BF16)
            dv_ref[:, hs] = dvs[hi].astype(BF16)

    per_sec = ATTN_WIDTH // W
    per_grp = ATTN_OUT_WIDTH // W

    def qkv_spec(sec, shift):
        return pl.BlockSpec((C, W), lambda b, h: (jnp.clip(b + shift, 0, nb - 1), sec * per_sec + gi * per_grp + h))

    def act_spec(shift):
        return pl.BlockSpec((C, W), lambda b, h: (jnp.clip(b + shift, 0, nb - 1), h))

    out = act_spec(0)
    scr = pltpu.VMEM((hp, C, HEAD_DIM), F32)
    return pl.pallas_call(
        body, grid=(nb, HEADS_PER_GROUP // hp),
        in_specs=[qkv_spec(0, 0), qkv_spec(1, 0), qkv_spec(2, 0), qkv_spec(1, -1), qkv_spec(2, -1), qkv_spec(0, 1),
                  act_spec(0), act_spec(1), act_spec(0), act_spec(1), act_spec(0), act_spec(1)],
        out_specs=[out, out, out], out_shape=[jax.ShapeDtypeStruct((S, ATTN_OUT_WIDTH), BF16)] * 3,
        scratch_shapes=[scr] * (n_in + 3),
        compiler_params=_params("parallel", "parallel"), name=f"attn_bwd_g{gi}")(
            qkv, qkv, qkv, qkv, qkv, qkv, do, do, lse, lse, dp, dp)


def _group_weights(l_refs):
    l0, l1, l2 = (r[...] for r in l_refs)
    m = jnp.maximum(jnp.maximum(l0, l1), l2)
    e = [jnp.exp(l0 - m), jnp.exp(l1 - m), jnp.exp(l2 - m)]
    tot = e[0] + e[1] + e[2]
    return [ei / tot for ei in e]


def _merge_fwd(outs, lses):
    S, W = outs[0].shape
    T = 512

    def body(o0, o1, o2, l0, l1, l2, y_ref):
        w = _group_weights((l0, l1, l2))
        y_ref[...] = (w[0] * o0[...] + w[1] * o1[...] + w[2] * o2[...]).astype(BF16)

    blk = pl.BlockSpec((T, W), lambda i: (i, 0))
    return pl.pallas_call(
        body, grid=(S // T,), in_specs=[blk] * 6, out_specs=blk, out_shape=jax.ShapeDtypeStruct((S, W), BF16),
        compiler_params=_params("parallel"), name="merge_fwd")(*outs, *lses)


def _merge_bwd(dy, outs, lses):
    S, W = outs[0].shape
    T = 512

    def body(dy_ref, o0, o1, o2, l0, l1, l2, do0, do1, do2, dp0, dp1, dp2):
        w = _group_weights((l0, l1, l2))
        dyv = dy_ref[...]
        y = w[0] * o0[...] + w[1] * o1[...] + w[2] * o2[...]
        prod = dyv * y
        dots = []
        for hi in range(HEADS_PER_GROUP):
            hs = slice(hi * HEAD_DIM, (hi + 1) * HEAD_DIM)
            dots.append(jnp.broadcast_to(jnp.sum(prod[:, hs], axis=-1, keepdims=True), (T, HEAD_DIM)))
        dot_b = jnp.concatenate(dots, axis=1)
        for wg, do_ref, dp_ref in zip(w, (do0, do1, do2), (dp0, dp1, dp2)):
            do_ref[...] = (wg * dyv).astype(BF16)
            dp_ref[...] = wg * dot_b

    blk = pl.BlockSpec((T, W), lambda i: (i, 0))
    res = pl.pallas_call(
        body, grid=(S // T,), in_specs=[blk] * 7, out_specs=[blk] * 6,
        out_shape=[jax.ShapeDtypeStruct((S, W), BF16)] * 3 + [jax.ShapeDtypeStruct((S, W), F32)] * 3,
        compiler_params=_params("parallel"), name="merge_bwd")(dy, *outs, *lses)
    return res[:3], res[3:]


def _mix_fwd(pool_y, attn_y, w_pool_out_t, w_attn_out_t, gates_pre, b_gate):
    S = pool_y.shape[0]
    D = D_MODEL
    tm, tn = 1024, 512
    nj = D // tn

    def body(p_ref, a_ref, wp_ref, wa_ref, g0_ref, g1_ref, b0_ref, b1_ref, mixed_ref, yp_ref, ya_ref):
        yp = _dot(p_ref[...], wp_ref[...], "nt")
        ya = _dot(a_ref[...], wa_ref[...], "nt")
        g0 = jax.nn.sigmoid(g0_ref[...] + b0_ref[...])
        g1 = jax.nn.sigmoid(g1_ref[...] + b1_ref[...])
        mixed_ref[...] = (g0 * yp + g1 * ya).astype(BF16)
        yp_ref[...] = yp.astype(BF16)
        ya_ref[...] = ya.astype(BF16)

    out = pl.BlockSpec((tm, tn), lambda i, j: (i, j))
    return pl.pallas_call(
        body, grid=(S // tm, nj),
        in_specs=[pl.BlockSpec((tm, POOL_WIDTH), lambda i, j: (i, 0)),
                  pl.BlockSpec((tm, ATTN_OUT_WIDTH), lambda i, j: (i, 0)),
                  pl.BlockSpec((tn, POOL_WIDTH), lambda i, j: (j, 0)),
                  pl.BlockSpec((tn, ATTN_OUT_WIDTH), lambda i, j: (j, 0)),
                  pl.BlockSpec((tm, tn), lambda i, j: (i, j)), pl.BlockSpec((tm, tn), lambda i, j: (i, j + nj)),
                  pl.BlockSpec((1, tn), lambda i, j: (0, j)), pl.BlockSpec((1, tn), lambda i, j: (0, j + nj))],
        out_specs=[out, out, out], out_shape=[jax.ShapeDtypeStruct((S, D), BF16)] * 3,
        compiler_params=_params("parallel", "parallel"), name="mix_fwd")(
            pool_y, attn_y, w_pool_out_t, w_attn_out_t, gates_pre, gates_pre, b_gate, b_gate)


def _mix_bwd(dmixed, gates_pre, b_gate, y_pool, y_attn):
    S = dmixed.shape[0]
    D = D_MODEL
    T = 256

    def body(dm_ref, gp_ref, b_ref, yp_ref, ya_ref, dyp_ref, dya_ref, dg_ref, db_ref):
        i = pl.program_id(0)
        dm = dm_ref[...].astype(F32)
        parts = []
        for br, (y_ref, dy_ref) in enumerate(((yp_ref, dyp_ref), (ya_ref, dya_ref))):
            cs = slice(br * D, (br + 1) * D)
            g = jax.nn.sigmoid(gp_ref[:, cs] + b_ref[:, cs])
            dy_ref[...] = (dm * g).astype(BF16)
            dpre = dm * y_ref[...].astype(F32) * g * (1.0 - g)
            dg_ref[:, cs] = dpre.astype(BF16)
            parts.append(jnp.sum(dpre, axis=0, keepdims=True))

        @pl.when(i == 0)
        def _():
            db_ref[:, 0:D] = parts[0]
            db_ref[:, D:2 * D] = parts[1]

        @pl.when(i > 0)
        def _():
            db_ref[:, 0:D] += parts[0]
            db_ref[:, D:2 * D] += parts[1]

    row = pl.BlockSpec((T, D), lambda i: (i, 0))
    row2 = pl.BlockSpec((T, 2 * D), lambda i: (i, 0))
    vec2 = pl.BlockSpec((1, 2 * D), lambda i: (0, 0))
    return pl.pallas_call(
        body, grid=(S // T,), in_specs=[row, row2, vec2, row, row], out_specs=[row, row, row2, vec2],
        out_shape=[jax.ShapeDtypeStruct((S, D), BF16), jax.ShapeDtypeStruct((S, D), BF16),
                   jax.ShapeDtypeStruct((S, 2 * D), BF16), jax.ShapeDtypeStruct((1, 2 * D), F32)],
        compiler_params=_params("arbitrary"), name="mix_bwd")(dmixed, gates_pre, b_gate, y_pool, y_attn)


_CONV_T, _CONV_C = 512, 512


def _conv_taps(ext_ref, h, w_ref, b_ref, cols=slice(None)):
    ext = ext_ref[h, :, cols]
    x2 = pltpu.roll(ext, 2, 0)[HALO:]
    x1 = pltpu.roll(ext, 1, 0)[HALO:]
    x0 = ext[HALO:]
    y = b_ref[:, cols] + w_ref[0:1, cols] * x2
    y = y + w_ref[1:2, cols] * x1
    y = y + w_ref[2:3, cols] * x0
    return y, (x2, x1, x0)


def _gelu_parts(a):
    cdf = 0.5 * (1.0 + lax.erf(a * (1.0 / math.sqrt(2.0))))
    return cdf, a * cdf


def _conv_specs(S):
    T, C = _CONV_T, _CONV_C
    nj = D_FF // C
    blk = pl.BlockSpec((2, T, C), lambda i, j: (0, i, j))
    per = T // HALO
    halo = pl.BlockSpec((2, HALO, C), lambda i, j: (0, jnp.maximum(i * per - 1, 0), j))
    w_a = pl.BlockSpec((3, C), lambda i, j: (0, j))
    w_b = pl.BlockSpec((3, C), lambda i, j: (0, j + nj))
    b_a = pl.BlockSpec((1, C), lambda i, j: (0, j))
    b_b = pl.BlockSpec((1, C), lambda i, j: (0, j + nj))
    return T, C, nj, blk, halo, w_a, w_b, b_a, b_b


def _conv_load(ext_ref, halo_ref, up_ref, first):
    for h in range(2):
        ext_ref[h, 0:HALO, :] = jnp.where(first, 0.0, halo_ref[h].astype(F32))
        ext_ref[h, HALO:, :] = up_ref[h].astype(F32)


def _up_proj_convffn(h, w_up_t, conv_w, conv_b):
    S = h.shape[0]
    tm, C = 1024, _CONV_C
    nj = D_FF // C

    def body(h_ref, wa_ref, wb_ref, cwa_ref, cwb_ref, cba_ref, cbb_ref, up_ref, act_ref, ext_ref, carry_ref):
        i, j = pl.program_id(0), pl.program_id(1)
        for half, w_ref in enumerate((wa_ref, wb_ref)):
            pre = _dot(h_ref[...], w_ref[...], "nt").astype(BF16)
            up_ref[half] = pre
            ext_ref[half, 0:HALO, :] = jnp.where(i == 0, 0.0, carry_ref[j, half])
            ext_ref[half, HALO:, :] = pre.astype(F32)
            carry_ref[j, half] = ext_ref[half, tm:tm + HALO, :]
        a, _ = _conv_taps(ext_ref, 0, cwa_ref, cba_ref)
        b, _ = _conv_taps(ext_ref, 1, cwb_ref, cbb_ref)
        _, gelu = _gelu_parts(a)
        act_ref[...] = (gelu * b).astype(BF16)

    def w_spec(half):
        return pl.BlockSpec((C, D_MODEL), lambda i, j: (j + half * nj, 0))

    def c_spec(rows, half):
        return pl.BlockSpec((rows, C), lambda i, j: (0, j + half * nj))

    return pl.pallas_call(
        body, grid=(S // tm, nj),
        in_specs=[pl.BlockSpec((tm, D_MODEL), lambda i, j: (i, 0)), w_spec(0), w_spec(1),
                  c_spec(3, 0), c_spec(3, 1), c_spec(1, 0), c_spec(1, 1)],
        out_specs=[pl.BlockSpec((2, tm, C), lambda i, j: (0, i, j)), pl.BlockSpec((tm, C), lambda i, j: (i, j))],
        out_shape=[jax.ShapeDtypeStruct((2, S, D_FF), BF16), jax.ShapeDtypeStruct((S, D_FF), BF16)],
        scratch_shapes=[pltpu.VMEM((2, HALO + tm, C), F32), pltpu.VMEM((nj, 2, HALO, C), F32)],
        compiler_params=_params("arbitrary", "arbitrary"), name="up_proj_convffn")(
            h, w_up_t, w_up_t, conv_w, conv_w, conv_b, conv_b)


def _convffn_bwd_a(up_pre, dx, w_down, conv_w, conv_b):
    S = up_pre.shape[1]
    T, C, nj, blk, halo, w_a, w_b, b_a, b_b = _conv_specs(S)
    def swap(spec):
        return pl.BlockSpec(spec.block_shape, lambda j, i, _m=spec.index_map: _m(i, j))

    def body(up_ref, halo_ref, dx_ref, wd_ref, wa_ref, wb_ref, ba_ref, bb_ref, dup_ref, dwa_ref, dwb_ref, dba_ref,
             dbb_ref, ext_ref):
        i = pl.program_id(1)
        _conv_load(ext_ref, halo_ref, up_ref, i == 0)

        @pl.when(i == 0)
        def _():
            for acc_ref in (dwa_ref, dwb_ref, dba_ref, dbb_ref):
                acc_ref[...] = jnp.zeros_like(acc_ref)

        for c in range(C // MXU_DIM):
            cs = slice(c * MXU_DIM, (c + 1) * MXU_DIM)
            dact = _dot(dx_ref[...], wd_ref[cs, :], "nt")
            a, xa = _conv_taps(ext_ref, 0, wa_ref, ba_ref, cs)
            b, xb = _conv_taps(ext_ref, 1, wb_ref, bb_ref, cs)
            cdf, gelu = _gelu_parts(a)
            dgelu = cdf + a * (jnp.exp(-0.5 * a * a) * (1.0 / math.sqrt(2.0 * math.pi)))
            da = dact * b * dgelu
            db = dact * gelu
            dup_ref[0, :, cs] = da.astype(BF16)
            dup_ref[1, :, cs] = db.astype(BF16)
            for dval, xs, dw_ref, dbias_ref in ((da, xa, dwa_ref, dba_ref), (db, xb, dwb_ref, dbb_ref)):
                dw_ref[:, cs] += jnp.concatenate([jnp.sum(dval * xk, axis=0, keepdims=True) for xk in xs], axis=0)
                dbias_ref[:, cs] += jnp.sum(dval, axis=0, keepdims=True)

    dx_spec = pl.BlockSpec((T, D_MODEL), lambda j, i: (i, 0))
    wd_spec = pl.BlockSpec((C, D_MODEL), lambda j, i: (j, 0))
    dw_spec = pl.BlockSpec((3, C), lambda j, i: (0, j))
    db_spec = pl.BlockSpec((1, C), lambda j, i: (0, j))
    res = pl.pallas_call(
        body, grid=(nj, S // T),
        in_specs=[swap(blk), swap(halo), dx_spec, wd_spec, swap(w_a), swap(w_b), swap(b_a), swap(b_b)],
        out_specs=[swap(blk), dw_spec, dw_spec, db_spec, db_spec],
        out_shape=[jax.ShapeDtypeStruct((2, S, D_FF), BF16), jax.ShapeDtypeStruct((3, D_FF), F32),
                   jax.ShapeDtypeStruct((3, D_FF), F32), jax.ShapeDtypeStruct((1, D_FF), F32),
                   jax.ShapeDtypeStruct((1, D_FF), F32)],
        scratch_shapes=[pltpu.VMEM((2, HALO + T, C), F32)],
        compiler_params=_params("parallel", "arbitrary"), name="convffn_bwd_a")(
            up_pre, up_pre, dx, w_down, conv_w, conv_w, conv_b, conv_b)
    dup, dwa, dwb, dba, dbb = res
    dconv_w = jnp.concatenate([dwa, dwb], axis=1)
    dconv_b = jnp.concatenate([dba, dbb], axis=1)
    return dup, dconv_w, dconv_b


def _convffn_bwd_b(dup, conv_w, w_up_t):
    S = dup.shape[1]
    tm, tk = 512, D_FF // 4
    per_half = D_FF // tk
    nk = 2 * per_half
    per = tm // HALO
    n_blk = S // tm

    def body(d_ref, halo_ref, cw_ref, w_ref, dpre_ref, dh_ref, ext_ref, acc_ref):
        i, k = pl.program_id(0), pl.program_id(1)
        n = tm + HALO
        ext_ref[0:tm, :] = d_ref[...].astype(F32)
        ext_ref[tm:, :] = jnp.where(i == n_blk - 1, 0.0, halo_ref[...].astype(F32))
        ext = ext_ref[...]
        y = cw_ref[2:3, :] * ext[0:tm] + cw_ref[1:2, :] * pltpu.roll(ext, n - 1, 0)[0:tm]
        y = y + cw_ref[0:1, :] * pltpu.roll(ext, n - 2, 0)[0:tm]
        pre = y.astype(BF16)
        dpre_ref[...] = pre
        prod = _dot(pre, w_ref[...], "nn")

        @pl.when(k == 0)
        def _():
            acc_ref[...] = prod

        @pl.when(k > 0)
        def _():
            acc_ref[...] += prod

        @pl.when(k == nk - 1)
        def _():
            dh_ref[...] = acc_ref[...]

    blk = pl.BlockSpec((None, tm, tk), lambda i, k: (k // per_half, i, k % per_half))
    halo = pl.BlockSpec((None, HALO, tk),
                        lambda i, k: (k // per_half, jnp.minimum((i + 1) * per, S // HALO - 1), k % per_half))
    return pl.pallas_call(
        body, grid=(n_blk, nk),
        in_specs=[blk, halo, pl.BlockSpec((3, tk), lambda i, k: (0, k)), pl.BlockSpec((tk, D_MODEL), lambda i, k: (k, 0))],
        out_specs=[blk, pl.BlockSpec((tm, D_MODEL), lambda i, k: (i, 0))],
        out_shape=[jax.ShapeDtypeStruct((2, S, D_FF), BF16), jax.ShapeDtypeStruct((S, D_MODEL), F32)],
        scratch_shapes=[pltpu.VMEM((tm + HALO, tk), F32), pltpu.VMEM((tm, D_MODEL), F32)],
        compiler_params=_params("parallel", "arbitrary"), name="convffn_bwd_b")(dup, dup, conv_w, w_up_t)


def _position():
    return lax.axis_index("x"), lax.axis_index("y"), lax.axis_index("c")


def _flip(pos, k):
    x, y, c = pos
    return (1 - x if k & 4 else x, 1 - y if k & 2 else y, 1 - c if k & 1 else c)


def _index(pos):
    return 4 * pos[0] + 2 * pos[1] + pos[2]


_HBM = pl.BlockSpec(memory_space=pltpu.HBM)
_SEM = pl.BlockSpec(memory_space=pltpu.SEMAPHORE)
_ANY = pl.BlockSpec(memory_space=pl.ANY)
_EFFECT = pltpu.SideEffectType.DATAFLOW_SIDE_EFFECTING


def _in_hbm(a):
    return pltpu.with_memory_space_constraint(a, pltpu.HBM)


def _split_copies(name, srcs, lands, n_copies, plan, start_after, wait_after):
    ns, nl = len(srcs), len(lands)
    start_after = tuple(start_after)
    na = len(start_after)

    def start_body(*refs):
        src_refs, land_refs = refs[:ns], refs[ns:ns + nl]
        send_sems, recv_sems = refs[ns + nl + na], refs[ns + nl + na + 1]
        token = refs[-1]
        for k, (src, dst, _, peer) in enumerate(plan(src_refs, land_refs)):
            pltpu.make_async_remote_copy(src_ref=src, dst_ref=dst, send_sem=send_sems.at[k], recv_sem=recv_sems.at[k],
                                         device_id=peer, device_id_type=MESH).start()
        token[...] = jnp.zeros_like(token)

    thru = [pltpu.HBM(a.shape, a.dtype) for a in list(srcs) + list(lands)]
    res = pl.pallas_call(
        start_body, name=name + "_start",
        out_shape=[pltpu.SemaphoreType.DMA((n_copies,)), pltpu.SemaphoreType.DMA((n_copies,))] + thru
        + [jax.ShapeDtypeStruct((8, 128), F32)],
        in_specs=[_HBM] * (ns + nl) + [_ANY] * na,
        out_specs=[_SEM, _SEM] + [_HBM] * (ns + nl) + [pl.BlockSpec(memory_space=pltpu.VMEM)],
        input_output_aliases={i: 2 + i for i in range(ns + nl)},
        compiler_params=pltpu.CompilerParams(has_side_effects=_EFFECT),
    )(*[_in_hbm(a) for a in list(srcs) + list(lands)], *start_after)
    send_sems, recv_sems, token = res[0], res[1], res[-1]
    thru_vals = res[2:-1]

    def wait():
        def wait_body(*refs):
            src_refs, land_refs = refs[:ns], refs[ns:ns + nl]
            s_sems, r_sems = refs[ns + nl], refs[ns + nl + 1]
            for k, (src, _, dst, peer) in enumerate(plan(src_refs, land_refs)):
                cp = pltpu.make_async_remote_copy(src_ref=src, dst_ref=dst, send_sem=s_sems.at[k], recv_sem=r_sems.at[k],
                                                  device_id=peer, device_id_type=MESH)
                cp.wait_send()
                cp.wait_recv()

        out = pl.pallas_call(
            wait_body, name=name + "_wait", out_shape=thru,
            in_specs=[_HBM] * (ns + nl) + [_SEM, _SEM, _ANY], out_specs=[_HBM] * (ns + nl),
            input_output_aliases={i: i for i in range(ns + nl)},
            compiler_params=pltpu.CompilerParams(has_side_effects=_EFFECT),
        )(*thru_vals, send_sems, recv_sems, wait_after())
        return list(out[:ns]), list(out[ns:])

    return wait, token


def _gather_level1_plan(n):
    def plan(src_refs, land_refs):
        me = _position()
        x, y, c = me
        peers = [(x, y, 1 - c), (1 - x, y, c), (x, 1 - y, c), (1 - x, 1 - y, c)]
        return [(src_refs[a], land_refs[a].at[_index(me)], land_refs[a].at[_index(p)], p)
                for a in range(n) for p in peers]
    return plan


def _gather_level2(lands):
    n = len(lands)

    def body(*refs):
        lin = refs[:n]
        send_sems, recv_sems = refs[2 * n:]
        x, y, c = _position()
        sibling = (x, y, 1 - c)
        chips = [(1 - x, y), (x, 1 - y), (1 - x, 1 - y)]

        def copy(a, j, core):
            slot = lin[a].at[_index((*chips[j], core))]
            return pltpu.make_async_remote_copy(
                src_ref=slot, dst_ref=slot, send_sem=send_sems.at[3 * a + j], recv_sem=recv_sems.at[3 * a + j],
                device_id=sibling, device_id_type=MESH)

        sent = [copy(a, j, c) for a in range(n) for j in range(3)]
        for cp in sent:
            cp.start()
        for a in range(n):
            for j in range(3):
                copy(a, j, 1 - c).wait_recv()
        for cp in sent:
            cp.wait_send()

    return pl.pallas_call(
        body, out_shape=[jax.ShapeDtypeStruct(l.shape, l.dtype) for l in lands],
        in_specs=[_ANY] * n, out_specs=[_ANY] * n, input_output_aliases={a: a for a in range(n)},
        scratch_shapes=[pltpu.SemaphoreType.DMA((3 * n,)), pltpu.SemaphoreType.DMA((3 * n,))],
        name="gather_level2")(*lands)


def _gather_direct_plan(n):
    def plan(src_refs, land_refs):
        me = _position()
        peers = [_flip(me, k) for k in range(1, N_DEV)]
        return [(src_refs[a], land_refs[a].at[_index(me)], land_refs[a].at[_index(p)], p)
                for a in range(n) for p in peers]
    return plan


def _place_own(shards):
    me = _index(_position())
    return [lax.dynamic_update_slice(lax.empty((N_DEV,) + s.shape, s.dtype), s[None], (me, 0, 0)) for s in shards]


def _exchange_direct_plan(n):
    def plan(src_refs, land_refs):
        me = _position()
        peers = [_flip(me, k) for k in range(1, N_DEV)]
        return [(src_refs[a].at[_index(p)], land_refs[a].at[_index(me)], land_refs[a].at[_index(p)], p)
                for a in range(n) for p in peers]
    return plan


def _adamw(w, g, m, v):
    m = ADAM_B1 * m + (1.0 - ADAM_B1) * g
    v = ADAM_B2 * v + (1.0 - ADAM_B2) * (g * g)
    m_hat = m / (1.0 - ADAM_B1 ** ADAM_STEP)
    v_hat = v / (1.0 - ADAM_B2 ** ADAM_STEP)
    delta = -ADAM_LR * (m_hat / (jnp.sqrt(v_hat) + ADAM_EPS) + ADAM_WD * w)
    return delta, m, v


def _row_tile(rows, unit, cap=256):
    best = rows
    for t in range(unit, min(rows, cap) + 1, unit):
        if rows % t == 0:
            best = t
    return best


def _sum_partials(name, recv):
    n_src, R, C = recv.shape
    T = _row_tile(R, 16)

    def body(r_ref, g_out):
        g = r_ref[0].astype(F32)
        for s in range(1, n_src):
            g = g + r_ref[s].astype(F32)
        g_out[...] = g

    return pl.pallas_call(
        body, grid=(R // T,), in_specs=[pl.BlockSpec((n_src, T, C), lambda i: (0, i, 0))],
        out_specs=pl.BlockSpec((T, C), lambda i: (i, 0)), out_shape=jax.ShapeDtypeStruct((R, C), F32),
        compiler_params=_params("parallel"), name=name)(recv)


def _adamw_update(name, w, g, m, v):
    R, C = w.shape
    T = _row_tile(R, 8)

    def body(w_ref, g_ref, m_ref, v_ref, d_out, m_out, v_out):
        delta, mn, vn = _adamw(w_ref[...], g_ref[...], m_ref[...], v_ref[...])
        d_out[...] = delta
        m_out[...] = mn
        v_out[...] = vn

    blk = pl.BlockSpec((T, C), lambda i: (i, 0))
    return pl.pallas_call(
        body, grid=(R // T,), in_specs=[blk] * 4, out_specs=[blk] * 3,
        out_shape=[jax.ShapeDtypeStruct((R, C), F32)] * 3, compiler_params=_params("parallel"), name=name)(w, g, m, v)


def _gather_vmem(p_ref, gath, send_sems, recv_sems):
    me = _position()
    gath[_index(me)] = p_ref[...]
    copies = []
    for k in range(1, N_DEV):
        peer = _flip(me, k)
        copies.append(pltpu.make_async_remote_copy(
            src_ref=p_ref, dst_ref=gath.at[_index(me)], send_sem=send_sems.at[k - 1],
            recv_sem=recv_sems.at[k - 1], device_id=peer, device_id_type=MESH))
    for cp in copies:
        cp.start()
    for k in range(1, N_DEV):
        peer = _flip(me, k)
        pltpu.make_async_remote_copy(
            src_ref=p_ref, dst_ref=gath.at[_index(peer)], send_sem=send_sems.at[k - 1],
            recv_sem=recv_sems.at[k - 1], device_id=peer, device_id_type=MESH).wait_recv()
    for cp in copies:
        cp.wait_send()


def _all_gather_small(p):
    R, C = p.shape

    def body(p_ref, out_ref, send_sems, recv_sems):
        _gather_vmem(p_ref, out_ref, send_sems, recv_sems)

    vm = pl.BlockSpec(memory_space=pltpu.VMEM)
    return pl.pallas_call(
        body, in_specs=[vm], out_specs=vm, out_shape=jax.ShapeDtypeStruct((N_DEV, R, C), F32),
        scratch_shapes=[pltpu.SemaphoreType.DMA((7,)), pltpu.SemaphoreType.DMA((7,))],
        name="all_gather_small")(p)


def _small_allreduce_adamw(part, w, m, v):
    R, C = part.shape

    def body(p_ref, w_ref, m_ref, v_ref, g_out, d_out, m_out, v_out, gath, send_sems, recv_sems):
        _gather_vmem(p_ref, gath, send_sems, recv_sems)
        g = gath[0]
        for s in range(1, N_DEV):
            g = g + gath[s]
        delta, mn, vn = _adamw(w_ref[...], g, m_ref[...], v_ref[...])
        g_out[...] = g
        d_out[...] = delta
        m_out[...] = mn
        v_out[...] = vn

    vm = pl.BlockSpec(memory_space=pltpu.VMEM)
    return pl.pallas_call(
        body, in_specs=[vm] * 4, out_specs=[vm] * 4, out_shape=[jax.ShapeDtypeStruct((R, C), F32)] * 4,
        scratch_shapes=[pltpu.VMEM((N_DEV, R, C), F32), pltpu.SemaphoreType.DMA((7,)), pltpu.SemaphoreType.DMA((7,))],
        name="small_allreduce_adamw")(part, w, m, v)


_BIG = ("w_in", "w_up", "w_down", "w_out", "w_pool_out", "w_attn_out")
_TRANSPOSED = ("w_in", "w_up", "w_pool_out", "w_attn_out")
_SMALL = (("g_mix", 2048), ("b_gate", 4096), ("pool_scale", 1024), ("g_ffn", 2048), ("conv_b", 11264), ("g_final", 2048))
CONV_ROWS = 8


def _to_comm(name, shard):
    return shard.T if name in _TRANSPOSED else shard


def _pack_small(vals):
    flat = [vals[n].reshape(-1) for n, _ in _SMALL]
    used = sum(f.shape[0] for f in flat)
    flat.append(jnp.zeros((SMALL_ROWS * PACK_COLS - used,), F32))
    return jnp.concatenate(flat).reshape(SMALL_ROWS, PACK_COLS)


def _unpack_small(packed, like):
    flat = packed.reshape(-1)
    out, off = {}, 0
    for name, n in _SMALL:
        out[name] = flat[off:off + n].reshape(like[name].shape)
        off += n
    return out


def _local_step(x, target, W, small, first_token=0.0, first_weights=None, late_weights=None, emit=None):
    S = x.shape[0]
    o_qkv, o_gate = POOL_WIDTH, POOL_WIDTH + QKV_WIDTH
    if emit is None:
        emit = lambda grads: None

    def zero(token):
        return 0.0 if token is None else token[0, 0]

    h1 = _rms_fwd("rms1_fwd", x, small["g_mix"] + first_token)
    tok = None
    if first_weights is not None:
        w_first, tok = first_weights(h1)
        W = dict(W, **w_first)
    w_in_t = W["w_in"]
    u = _mm_nt("proj_u", h1, w_in_t, tm=2048, tn=512, tk=2048, o_dtype=F32, n_rows=POOL_WIDTH, b_row_off=0,
               after=tok)
    qkv = _mm_nt("proj_qkv", h1, w_in_t, tm=2048, tn=512, tk=2048, o_dtype=BF16, n_rows=QKV_WIDTH, b_row_off=o_qkv,
                 after=tok)
    gates_pre = _mm_nt("proj_gates", h1, w_in_t, tm=2048, tn=512, tk=2048, o_dtype=F32, n_rows=2 * D_MODEL,
                       b_row_off=o_gate, after=tok)
    pool_y = _pool_fwd(u, W["w_pool_lin"], small["pool_scale"])
    outs, lses = zip(*[_attn_fwd(gi, qkv) for gi in range(3)])
    attn_y = _merge_fwd(outs, lses)
    if late_weights is not None:
        W = dict(W, **late_weights(0, attn_y))
    w_up_t = W["w_up"]
    mixed, y_pool, y_attn = _mix_fwd(pool_y, attn_y, W["w_pool_out"], W["w_attn_out"], gates_pre, small["b_gate"])
    x1 = _mm_nn("out_proj", mixed, W["w_out"], tm=1024, tn=1024, tk=2048, o_dtype=F32, res=x)

    h2 = _rms_fwd("rms2_fwd", x1, small["g_ffn"])
    nj = D_FF // 512
    up_pre, act = _up_proj_convffn(h2, w_up_t, small["conv_w_full"], small["conv_b"])
    if late_weights is not None:
        W = dict(W, **late_weights(1, act))
    x2 = _mm_nn("down_proj", act, W["w_down"], tm=1024, tn=1024, tk=D_FF // 2, o_dtype=F32, res=x1)

    dx2, dx2_b, dg_final, sq_cols = _final_loss("final_loss", x2, small["g_final"], target)

    dw_down = _mm_tn("dw_down", act, dx2_b, tm=512, tn=2048, tk=TOKEN_TK)
    tok = emit(dict(w_down=dw_down))
    dup, dconv_w, dconv_b = _convffn_bwd_a(up_pre, dx2_b, W["w_down"], small["conv_w_full"],
                                           small["conv_b"] + zero(tok))
    dup_pre, dh2 = _convffn_bwd_b(dup, small["conv_w_full"], w_up_t)
    dw_up = _mm("dw_up", dup_pre, h2, mode="tn", grid=(2 * nj, 1, S // TOKEN_TK),
                a_spec=pl.BlockSpec((None, TOKEN_TK, 512), lambda i, j, k: (i // nj, k, i % nj)),
                b_spec=pl.BlockSpec((TOKEN_TK, D_MODEL), lambda i, j, k: (k, 0)),
                o_spec=pl.BlockSpec((512, D_MODEL), lambda i, j, k: (i, 0)),
                o_shape=(2 * D_FF, D_MODEL), o_dtype=BF16, acc_shape=(512, D_MODEL))
    tok = emit(dict(w_up=dw_up, conv_w=dconv_w))
    dx1, dx1_b, dg_ffn = _rms_bwd("rms2_bwd", x1, small["g_ffn"] + zero(tok), dh2, dx2)

    dmixed = _mm_nt("d_mixed", dx1_b, W["w_out"], tm=1024, tn=1024, tk=2048, o_dtype=BF16)
    dw_out = _mm_tn("dw_out", mixed, dx1_b, tm=512, tn=2048, tk=TOKEN_TK)
    tok = emit(dict(w_out=dw_out))
    dy_pool, dy_attn, dgates, db_gate = _mix_bwd(dmixed, gates_pre, small["b_gate"] + zero(tok), y_pool, y_attn)
    dpool_y = _mm_nn("d_pool_y", dy_pool, W["w_pool_out"], tm=1024, tn=1024, tk=2048, o_dtype=F32)
    dw_pool_out = _mm_tn("dw_pool_out", dy_pool, pool_y, tm=1024, tn=1024, tk=TOKEN_TK)
    dattn_y = _mm_nn("d_attn_y", dy_attn, W["w_attn_out"], tm=1024, tn=512, tk=2048, o_dtype=F32)
    dw_attn_out = _mm_tn("dw_attn_out", dy_attn, attn_y, tm=1024, tn=512, tk=TOKEN_TK)
    e, dw_pool_lin, dpool_scale = _pool_bwd_a(u, dpool_y, W["w_pool_lin"], small["pool_scale"])
    du = _pool_bwd_b(e)
    dos, dps = _merge_bwd(dattn_y, outs, lses)
    dqkv = [_attn_bwd(gi, qkv, dos[gi], lses[gi], dps[gi]) for gi in range(3)]
    dproj = jnp.concatenate([du] + [dqkv[gi][sec] for sec in range(3) for gi in range(3)] + [dgates], axis=1)
    dw_in = _mm_tn("dw_in", dproj, h1, tm=512, tn=2048, tk=TOKEN_TK)
    tok = emit(dict(w_in=dw_in, w_pool_out=dw_pool_out, w_attn_out=dw_attn_out, w_pool_lin=dw_pool_lin))
    dh1 = _mm_nn("d_h1", dproj, w_in_t, tm=1024, tn=1024, tk=IN_WIDTH // 4, o_dtype=F32,
                 after=tok)
    grad_x, _, dg_mix = _rms_bwd("rms1_bwd", x, small["g_mix"] + zero(tok), dh1, dx1)

    grads = dict(w_in=dw_in, w_up=dw_up, w_down=dw_down, w_out=dw_out, w_pool_out=dw_pool_out,
                 w_attn_out=dw_attn_out, w_pool_lin=dw_pool_lin, conv_w=dconv_w)
    small_grads = dict(g_mix=dg_mix, b_gate=db_gate, pool_scale=dpool_scale, g_ffn=dg_ffn, conv_b=dconv_b,
                       g_final=dg_final)
    return sq_cols, grad_x, grads, small_grads


def kernel(x, g_mix, w_in, b_gate, w_pool_lin, pool_scale, w_pool_out, w_attn_out, w_out, g_ffn, w_up, conv_w, conv_b, w_down, g_final, loss_target, m_g_mix, m_w_in, m_b_gate, m_w_pool_lin, m_pool_scale, m_w_pool_out, m_w_attn_out, m_w_out, m_g_ffn, m_w_up, m_conv_w, m_conv_b, m_w_down, m_g_final, v_g_mix, v_w_in, v_b_gate, v_w_pool_lin, v_pool_scale, v_w_pool_out, v_w_attn_out, v_w_out, v_g_ffn, v_w_up, v_conv_w, v_conv_b, v_w_down, v_g_final):
    given = dict(g_mix=g_mix, w_in=w_in, b_gate=b_gate, w_pool_lin=w_pool_lin, pool_scale=pool_scale,
                 w_pool_out=w_pool_out, w_attn_out=w_attn_out, w_out=w_out, g_ffn=g_ffn, w_up=w_up, conv_w=conv_w,
                 conv_b=conv_b, w_down=w_down, g_final=g_final)
    mom_m = dict(g_mix=m_g_mix, w_in=m_w_in, b_gate=m_b_gate, w_pool_lin=m_w_pool_lin, pool_scale=m_pool_scale,
                 w_pool_out=m_w_pool_out, w_attn_out=m_w_attn_out, w_out=m_w_out, g_ffn=m_g_ffn, w_up=m_w_up,
                 conv_w=m_conv_w, conv_b=m_conv_b, w_down=m_w_down, g_final=m_g_final)
    mom_v = dict(g_mix=v_g_mix, w_in=v_w_in, b_gate=v_b_gate, w_pool_lin=v_w_pool_lin, pool_scale=v_pool_scale,
                 w_pool_out=v_w_pool_out, w_attn_out=v_w_attn_out, w_out=v_w_out, g_ffn=v_g_ffn, w_up=v_w_up,
                 conv_w=v_conv_w, conv_b=v_conv_b, w_down=v_w_down, g_final=v_g_final)
    def full(g):
        return g.reshape(N_DEV * g.shape[1], g.shape[2])

    shard = {n: _to_comm(n, given[n][0]).astype(BF16) for n in _BIG}
    W = {}
    lin_all = _all_gather_small(w_pool_lin[0].reshape(4 * 32, POOL_GROUP_WIDTH))
    W["w_pool_lin"] = jnp.moveaxis(lin_all.reshape(N_DEV, 4, 32, POOL_GROUP_WIDTH), 0, 1).reshape(
        4, POOL_GROUP_WIDTH, POOL_GROUP_WIDTH).astype(BF16)
    conv_w_all = _all_gather_small(jnp.pad(conv_w[0], ((0, CONV_ROWS - 3), (0, 0))))
    conv_w_full = jnp.moveaxis(conv_w_all[:, :3, :], 0, 1).reshape(3, 2 * D_FF)

    hold = {}
    w_in_wait, first_token = _split_copies(
        "gather_w_in", [shard["w_in"]], _place_own([shard["w_in"]]), 4, _gather_level1_plan(1),
        start_after=(lin_all, conv_w_all), wait_after=lambda: hold["h1"])

    stages = (("w_out", "w_pool_out", "w_attn_out", "w_up"), ("w_down",))
    stage_after = [None, None]
    stage_wait = []

    def first_weights(h1):
        hold["h1"] = h1
        w_in_full = full(_gather_level2(w_in_wait()[1])[0])
        order_after = (w_in_full,)
        for k, names in enumerate(stages):
            srcs = [shard[n] for n in names]
            wait, token = _split_copies(f"gather_stage{k}", srcs, _place_own(srcs), 7 * len(names),
                                        _gather_direct_plan(len(names)), start_after=order_after,
                                        wait_after=lambda k=k: stage_after[k])
            stage_wait.append(wait)
            order_after = (token,)
        return dict(w_in=w_in_full), token

    def late_weights(stage, after):
        stage_after[stage] = after
        return {n: full(l) for n, l in zip(stages[stage], stage_wait[stage]()[1])}

    small = dict(g_mix=g_mix, b_gate=b_gate, pool_scale=pool_scale, g_ffn=g_ffn, conv_b=conv_b,
                 g_final=g_final.reshape(1, D_MODEL), conv_w_full=conv_w_full)

    pending = []
    done = {}

    def emit(group):
        names = list(group)
        parts = []
        for n in names:
            g = group[n]
            if n == "w_pool_lin":
                g = jnp.moveaxis(g.reshape(4, N_DEV, 32, POOL_GROUP_WIDTH), 1, 0).reshape(
                    N_DEV, 4 * 32, POOL_GROUP_WIDTH).astype(BF16)
            elif n == "conv_w":
                g = jnp.pad(jnp.moveaxis(g.reshape(3, N_DEV, 2 * D_FF // N_DEV), 1, 0),
                            ((0, 0), (0, CONV_ROWS - 3), (0, 0))).astype(BF16)
            else:
                g = g.reshape(N_DEV, g.shape[0] // N_DEV, g.shape[1])
            parts.append(g)
        me = _index(_position())
        lands = [lax.dynamic_update_slice(lax.empty(p.shape, p.dtype), lax.dynamic_index_in_dim(p, me, 0),
                                          (me, 0, 0)) for p in parts]
        wait, token = _split_copies(
            "exchange_" + names[0], parts, lands, 7 * len(names), _exchange_direct_plan(len(names)),
            start_after=(), wait_after=lambda: done["grad_x"])
        pending.append((names, wait))
        return token

    sq_cols, grad_x, _, small_grads = _local_step(x[0], loss_target[0], W, small, first_token=first_token[0, 0],
                                                  first_weights=first_weights, late_weights=late_weights, emit=emit)
    done["grad_x"] = grad_x
    loss = lax.psum(0.5 / D_MODEL * jnp.sum(sq_cols), AXES)

    res = {}
    for n, r in [(n, r) for names, wait in pending for n, r in zip(names, wait()[1])]:
        g = _sum_partials("sum_" + n, r)
        if n in _TRANSPOSED:
            g = g.T
        elif n == "conv_w":
            g = g[:3]
        shape = given[n].shape
        rows = math.prod(shape[:-1])
        w2, m2, v2 = (d[n].reshape(rows, shape[-1]) for d in (given, mom_m, mom_v))
        upd = _adamw_update("adamw_" + n, w2, g.reshape(rows, shape[-1]), m2, v2)
        res[n] = [a.reshape(shape) for a in (g,) + tuple(upd)]

    packed_small = [_pack_small(d) for d in (given, mom_m, mom_v)]
    res_small = [_unpack_small(p, given) for p in _small_allreduce_adamw(_pack_small(small_grads), *packed_small)]

    order = ["g_mix", "w_in", "b_gate", "w_pool_lin", "pool_scale", "w_pool_out", "w_attn_out", "w_out", "g_ffn",
             "w_up", "conv_w", "conv_b", "w_down", "g_final"]
    outs = [loss, grad_x[None]]
    for kind in range(4):
        for n in order:
            outs.append(res[n][kind] if n in res else res_small[kind][n])
    return tuple(outs)
```

```python
import math

import jax
import jax.numpy as jnp
from jax import lax
from jax.experimental import pallas as pl
from jax.experimental.pallas import tpu as pltpu

F32 = jnp.float32
BF16 = jnp.bfloat16
MESH = pl.DeviceIdType.MESH
AXES = ("x", "y", "c")
N_DEV = 8

D_MODEL = 2048
POOL_WINDOWS = (2, 4, 8, 16)
POOL_GROUP_WIDTH = 256
POOL_WIDTH = 1024
ATTN_GROUPS = ((128, 1), (512, 4), (2048, 16))
SPAN = 128
HEADS_PER_GROUP = 4
N_ATTN_HEADS = 12
HEAD_DIM = 128
ATTN_WIDTH = 1536
QKV_WIDTH = 3 * ATTN_WIDTH
ATTN_OUT_WIDTH = 512
IN_WIDTH = 9728
D_FF = 5632
RMS_EPS = 1e-6
ADAM_LR, ADAM_B1, ADAM_B2, ADAM_EPS, ADAM_WD, ADAM_STEP = 0.001, 0.9, 0.999, 1e-08, 0.01, 10

VMEM_LIMIT_BYTES = 48 * 1024 * 1024
HALO = 16
TOKEN_TK = 2048
MXU_DIM = 256
PACK_COLS = 1024
SMALL_ROWS = 24
NEG_BIG = -1e30

_DN = {"nn": (((1,), (0,)), ((), ())), "nt": (((1,), (1,)), ((), ())), "tn": (((0,), (0,)), ((), ()))}


def _params(*sem):
    return pltpu.CompilerParams(dimension_semantics=sem, vmem_limit_bytes=VMEM_LIMIT_BYTES)


def _dot(a, b, mode):
    return lax.dot_general(a.astype(BF16), b.astype(BF16), _DN[mode], preferred_element_type=F32)


def _mm(name, a, b, *, mode, grid, a_spec, b_spec, o_spec, o_shape, o_dtype, acc_shape, res=None, res_spec=None,
        after=None):
    nk = grid[2]
    n_in = 2 + (res is not None) + (after is not None)

    def body(*refs):
        a_ref, b_ref = refs[:2]
        r_ref = refs[2] if res is not None else None
        o_ref = refs[n_in]
        scr = refs[n_in + 1:]
        prod = _dot(a_ref[...], b_ref[...], mode)

        def finish(val):
            if r_ref is not None:
                val = val + r_ref[...]
            o_ref[...] = val.astype(o_ref.dtype)

        if nk == 1:
            finish(prod)
        else:
            acc = scr[0]
            k = pl.program_id(2)

            @pl.when(k == 0)
            def _():
                acc[...] = prod

            @pl.when(k > 0)
            def _():
                acc[...] += prod

            @pl.when(k == nk - 1)
            def _():
                finish(acc[...])

    in_specs = [a_spec, b_spec]
    args = [a, b]
    if res is not None:
        in_specs.append(res_spec)
        args.append(res)
    if after is not None:
        in_specs.append(pl.BlockSpec((8, 128), lambda i, j, k: (0, 0)))
        args.append(after)
    return pl.pallas_call(
        body, grid=grid, in_specs=in_specs, out_specs=o_spec,
        out_shape=jax.ShapeDtypeStruct(o_shape, o_dtype),
        scratch_shapes=[] if nk == 1 else [pltpu.VMEM(acc_shape, F32)],
        compiler_params=_params("parallel", "parallel", "arbitrary"), name=name)(*args)


def _mm_nn(name, a, b, *, tm, tn, tk, o_dtype, n_cols=None, b_col_off=0, res=None, after=None):
    M, K = a.shape
    N = b.shape[1] if n_cols is None else n_cols
    off = b_col_off // tn
    return _mm(name, a, b, mode="nn", grid=(M // tm, N // tn, K // tk),
               a_spec=pl.BlockSpec((tm, tk), lambda i, j, k: (i, k)),
               b_spec=pl.BlockSpec((tk, tn), lambda i, j, k: (k, j + off)),
               o_spec=pl.BlockSpec((tm, tn), lambda i, j, k: (i, j)),
               o_shape=(M, N), o_dtype=o_dtype, acc_shape=(tm, tn), res=res,
               res_spec=pl.BlockSpec((tm, tn), lambda i, j, k: (i, j)), after=after)


def _mm_nt(name, a, b, *, tm, tn, tk, o_dtype, n_rows=None, b_row_off=0, after=None):
    M, K = a.shape
    N = b.shape[0] if n_rows is None else n_rows
    off = b_row_off // tn
    return _mm(name, a, b, mode="nt", grid=(M // tm, N // tn, K // tk),
               a_spec=pl.BlockSpec((tm, tk), lambda i, j, k: (i, k)),
               b_spec=pl.BlockSpec((tn, tk), lambda i, j, k: (j + off, k)),
               o_spec=pl.BlockSpec((tm, tn), lambda i, j, k: (i, j)),
               o_shape=(M, N), o_dtype=o_dtype, acc_shape=(tm, tn), after=after)


def _mm_tn(name, a, b, *, tm, tn, tk):
    K, M = a.shape
    N = b.shape[1]
    return _mm(name, a, b, mode="tn", grid=(M // tm, N // tn, K // tk),
               a_spec=pl.BlockSpec((tk, tm), lambda i, j, k: (k, i)),
               b_spec=pl.BlockSpec((tk, tn), lambda i, j, k: (k, j)),
               o_spec=pl.BlockSpec((tm, tn), lambda i, j, k: (i, j)),
               o_shape=(M, N), o_dtype=BF16, acc_shape=(tm, tn))


def _rms_fwd(name, x, g):
    S, D = x.shape
    T = 512

    def body(x_ref, g_ref, h_ref):
        xv = x_ref[...]
        r = lax.rsqrt(jnp.mean(xv * xv, axis=-1, keepdims=True) + RMS_EPS)
        h_ref[...] = (xv * r * g_ref[...]).astype(BF16)

    return pl.pallas_call(
        body, grid=(S // T,),
        in_specs=[pl.BlockSpec((T, D), lambda i: (i, 0)), pl.BlockSpec((1, D), lambda i: (0, 0))],
        out_specs=pl.BlockSpec((T, D), lambda i: (i, 0)),
        out_shape=jax.ShapeDtypeStruct((S, D), BF16), compiler_params=_params("parallel"), name=name)(x, g)


def _rms_bwd(name, x, g, dh, dres):
    S, D = x.shape
    T = 256

    def body(x_ref, g_ref, dh_ref, dres_ref, dx_ref, dxb_ref, dg_ref):
        i = pl.program_id(0)
        xv = x_ref[...]
        r = lax.rsqrt(jnp.mean(xv * xv, axis=-1, keepdims=True) + RMS_EPS)
        xhat = xv * r
        dhv = dh_ref[...].astype(F32)
        gdh = dhv * g_ref[...]
        c = jnp.mean(xhat * gdh, axis=-1, keepdims=True)
        dx = dres_ref[...] + r * (gdh - xhat * c)
        dx_ref[...] = dx
        dxb_ref[...] = dx.astype(BF16)
        part = jnp.sum(dhv * xhat, axis=0, keepdims=True)

        @pl.when(i == 0)
        def _():
            dg_ref[...] = part

        @pl.when(i > 0)
        def _():
            dg_ref[...] += part

    row = pl.BlockSpec((T, D), lambda i: (i, 0))
    vec = pl.BlockSpec((1, D), lambda i: (0, 0))
    return pl.pallas_call(
        body, grid=(S // T,), in_specs=[row, vec, row, row], out_specs=[row, row, vec],
        out_shape=[jax.ShapeDtypeStruct((S, D), F32), jax.ShapeDtypeStruct((S, D), BF16),
                   jax.ShapeDtypeStruct((1, D), F32)],
        compiler_params=_params("arbitrary"), name=name)(x, g, dh, dres)


def _final_loss(name, x, g, target):
    S, D = x.shape
    T = 256

    def body(x_ref, g_ref, t_ref, dx_ref, dxb_ref, dg_ref, sq_ref):
        i = pl.program_id(0)
        xv = x_ref[...]
        r = lax.rsqrt(jnp.mean(xv * xv, axis=-1, keepdims=True) + RMS_EPS)
        xhat = xv * r
        err = xhat * g_ref[...] - t_ref[...]
        dy = err * (1.0 / D)
        gdy = dy * g_ref[...]
        c = jnp.mean(xhat * gdy, axis=-1, keepdims=True)
        dx = r * (gdy - xhat * c)
        dx_ref[...] = dx
        dxb_ref[...] = dx.astype(BF16)
        dg_part = jnp.sum(dy * xhat, axis=0, keepdims=True)
        sq_part = jnp.sum(err * err, axis=0, keepdims=True)

        @pl.when(i == 0)
        def _():
            dg_ref[...] = dg_part
            sq_ref[...] = sq_part

        @pl.when(i > 0)
        def _():
            dg_ref[...] += dg_part
            sq_ref[...] += sq_part

    row = pl.BlockSpec((T, D), lambda i: (i, 0))
    vec = pl.BlockSpec((1, D), lambda i: (0, 0))
    return pl.pallas_call(
        body, grid=(S // T,), in_specs=[row, vec, row], out_specs=[row, row, vec, vec],
        out_shape=[jax.ShapeDtypeStruct((S, D), F32), jax.ShapeDtypeStruct((S, D), BF16),
                   jax.ShapeDtypeStruct((1, D), F32), jax.ShapeDtypeStruct((1, D), F32)],
        compiler_params=_params("arbitrary"), name=name)(x, g, target)


def _prev_halo_spec(T, C, col_map):
    per = T // HALO
    return pl.BlockSpec((HALO, C), lambda *g: (jnp.maximum(g[0] * per - 1, 0), col_map(*g)))


def _pool_window_sums(ext, T, i):
    t = i * T + lax.broadcasted_iota(jnp.int32, (T, 1), 0)
    outs = []
    cnts = []
    for gi, w in enumerate(POOL_WINDOWS):
        s = ext[:, gi * POOL_GROUP_WIDTH:(gi + 1) * POOL_GROUP_WIDTH]
        sh = 1
        while sh < w:
            s = s + pltpu.roll(s, sh, 0)
            sh *= 2
        cnt = jnp.minimum(t + 1, w).astype(F32)
        outs.append(s[HALO:] / cnt - ext[HALO:, gi * POOL_GROUP_WIDTH:(gi + 1) * POOL_GROUP_WIDTH])
        cnts.append(cnt)
    return outs, cnts


def _load_ext(ext_ref, halo_ref, blk_ref, first):
    ext_ref[0:HALO, :] = jnp.where(first, 0.0, halo_ref[...].astype(F32))
    ext_ref[HALO:, :] = blk_ref[...].astype(F32)


def _pool_fwd(u, w_lin, scale):
    S = u.shape[0]
    T = 512
    W = POOL_WIDTH

    def body(u_ref, halo_ref, wl_ref, sc_ref, y_ref, ext_ref):
        i = pl.program_id(0)
        _load_ext(ext_ref, halo_ref, u_ref, i == 0)
        pooled, _ = _pool_window_sums(ext_ref[...], T, i)
        for gi in range(4):
            z = _dot(pooled[gi], wl_ref[gi], "nn")
            cs = slice(gi * POOL_GROUP_WIDTH, (gi + 1) * POOL_GROUP_WIDTH)
            y_ref[:, cs] = (z * sc_ref[:, cs]).astype(BF16)

    return pl.pallas_call(
        body, grid=(S // T,),
        in_specs=[pl.BlockSpec((T, W), lambda i: (i, 0)), _prev_halo_spec(T, W, lambda i: 0),
                  pl.BlockSpec((4, 256, 256), lambda i: (0, 0, 0)), pl.BlockSpec((1, W), lambda i: (0, 0))],
        out_specs=pl.BlockSpec((T, W), lambda i: (i, 0)),
        out_shape=jax.ShapeDtypeStruct((S, W), BF16),
        scratch_shapes=[pltpu.VMEM((HALO + T, W), F32)],
        compiler_params=_params("parallel"), name="pool_fwd")(u, u, w_lin, scale)


def _pool_bwd_a(u, dy, w_lin, scale):
    S = u.shape[0]
    T = 512
    W = POOL_WIDTH

    def body(u_ref, halo_ref, dy_ref, wl_ref, sc_ref, e_ref, dwl_ref, dsc_ref, ext_ref):
        i = pl.program_id(0)
        _load_ext(ext_ref, halo_ref, u_ref, i == 0)
        pooled, cnts = _pool_window_sums(ext_ref[...], T, i)
        for gi in range(4):
            cs = slice(gi * POOL_GROUP_WIDTH, (gi + 1) * POOL_GROUP_WIDTH)
            z = _dot(pooled[gi], wl_ref[gi], "nn")
            dyg = dy_ref[:, cs]
            dsc = jnp.sum(dyg * z, axis=0, keepdims=True)
            dz = dyg * sc_ref[:, cs]
            dwl = _dot(pooled[gi], dz, "tn")
            dpooled = _dot(dz, wl_ref[gi], "nt")
            e_ref[:, cs] = dpooled / cnts[gi]

            @pl.when(i == 0)
            def _():
                dwl_ref[gi] = dwl
                dsc_ref[:, cs] = dsc

            @pl.when(i > 0)
            def _():
                dwl_ref[gi] += dwl
                dsc_ref[:, cs] += dsc

    return pl.pallas_call(
        body, grid=(S // T,),
        in_specs=[pl.BlockSpec((T, W), lambda i: (i, 0)), _prev_halo_spec(T, W, lambda i: 0),
                  pl.BlockSpec((T, W), lambda i: (i, 0)),
                  pl.BlockSpec((4, 256, 256), lambda i: (0, 0, 0)), pl.BlockSpec((1, W), lambda i: (0, 0))],
        out_specs=[pl.BlockSpec((T, W), lambda i: (i, 0)), pl.BlockSpec((4, 256, 256), lambda i: (0, 0, 0)),
                   pl.BlockSpec((1, W), lambda i: (0, 0))],
        out_shape=[jax.ShapeDtypeStruct((S, W), F32), jax.ShapeDtypeStruct((4, 256, 256), F32),
                   jax.ShapeDtypeStruct((1, W), F32)],
        scratch_shapes=[pltpu.VMEM((HALO + T, W), F32)],
        compiler_params=_params("arbitrary"), name="pool_bwd_a")(u, u, dy, w_lin, scale)


def _pool_bwd_b(e):
    S = e.shape[0]
    T = 512
    W = POOL_WIDTH
    per = T // HALO
    n_blk = S // T

    def body(e_ref, halo_ref, du_ref, ext_ref):
        i = pl.program_id(0)
        ext_ref[0:T, :] = e_ref[...]
        ext_ref[T:, :] = jnp.where(i == n_blk - 1, 0.0, halo_ref[...])
        ext = ext_ref[...]
        n = T + HALO
        t = i * T + lax.broadcasted_iota(jnp.int32, (T, 1), 0)
        for gi, w in enumerate(POOL_WINDOWS):
            cs = slice(gi * POOL_GROUP_WIDTH, (gi + 1) * POOL_GROUP_WIDTH)
            s = ext[:, cs]
            sh = 1
            while sh < w:
                s = s + pltpu.roll(s, n - sh, 0)
                sh *= 2
            cnt = jnp.minimum(t + 1, w).astype(F32)
            du_ref[:, cs] = (s[0:T] - ext[0:T, cs] * cnt).astype(BF16)

    return pl.pallas_call(
        body, grid=(n_blk,),
        in_specs=[pl.BlockSpec((T, W), lambda i: (i, 0)),
                  pl.BlockSpec((HALO, W), lambda i: (jnp.minimum((i + 1) * per, S // HALO - 1), 0))],
        out_specs=pl.BlockSpec((T, W), lambda i: (i, 0)),
        out_shape=jax.ShapeDtypeStruct((S, W), BF16),
        scratch_shapes=[pltpu.VMEM((T + HALO, W), F32)],
        compiler_params=_params("parallel"), name="pool_bwd_b")(e, e)


def _slope(head):
    return 2.0 ** (-8.0 * (head + 1) / N_ATTN_HEADS)


def _attn_masks(b):
    row = lax.broadcasted_iota(jnp.int32, (SPAN, SPAN), 0)
    col = lax.broadcasted_iota(jnp.int32, (SPAN, SPAN), 1)
    dist_cur = (row - col).astype(F32)
    valid_cur = col <= row
    valid_prev = jnp.logical_and(col >= row, b > 0)
    return dist_cur, valid_cur, valid_prev


ATTN_BATCH = 4


def _attn_tiling(gi, S):
    d = ATTN_GROUPS[gi][1]
    hp = HEADS_PER_GROUP if d < 16 else 1
    return d, SPAN * d, S // (SPAN * d), hp, hp * HEAD_DIM, ATTN_BATCH // hp


_BATCHED_DN = {"nt": (((2,), (2,)), ((0,), (0,))), "nn": (((2,), (1,)), ((0,), (0,))), "tn": (((1,), (1,)), ((0,), (0,)))}


def _bdot(a, b, mode):
    return lax.dot_general(a.astype(BF16), b.astype(BF16), _BATCHED_DN[mode], preferred_element_type=F32)


def _batch_rows(it, d, rb):
    return [pl.ds(it * rb + j, SPAN, stride=d) for j in range(rb)]


def _take(scr, hp, rows):
    return jnp.stack([scr[hi, rw, :] for hi in range(hp) for rw in rows])


def _put(scr, hp, rows, val):
    for n, (hi, rw) in enumerate((hi, rw) for hi in range(hp) for rw in rows):
        scr[hi, rw, :] = val[n]


def _batch_bias(gi, d, hp, rb, dist):
    return jnp.stack([(-(_head_slope(gi, hp, hi) * d)) * dist for hi in range(hp) for _ in range(rb)])


def _head_slope(gi, hp, hi):
    if hp == HEADS_PER_GROUP:
        return _slope(gi * HEADS_PER_GROUP + hi)
    first = pl.program_id(1) * hp
    sl = jnp.float32(_slope(gi * HEADS_PER_GROUP + hi))
    for h0 in range(hp, HEADS_PER_GROUP, hp):
        sl = jnp.where(first == h0, jnp.float32(_slope(gi * HEADS_PER_GROUP + h0 + hi)), sl)
    return sl


def _split_heads(src_ref, dst_ref, hp):
    for hi in range(hp):
        dst_ref[hi] = src_ref[:, hi * HEAD_DIM:(hi + 1) * HEAD_DIM].astype(F32)


def _attn_fwd(gi, qkv):
    S = qkv.shape[0]
    d, C, nb, hp, W, rb = _attn_tiling(gi, S)
    scale = HEAD_DIM ** -0.5

    def body(q_ref, kc_ref, kp_ref, vc_ref, vp_ref, o_ref, lse_ref, qs, kcs, kps, vcs, vps, os, ls):
        b = pl.program_id(0)
        for src, dst in ((q_ref, qs), (kc_ref, kcs), (kp_ref, kps), (vc_ref, vcs), (vp_ref, vps)):
            _split_heads(src, dst, hp)
        dist_cur, valid_cur, valid_prev = _attn_masks(b)
        bias_cur = _batch_bias(gi, d, hp, rb, dist_cur)
        bias_far = _batch_bias(gi, d, hp, rb, dist_cur + SPAN)

        def sub_sequences(it, carry):
            rows = _batch_rows(it, d, rb)
            q = _take(qs, hp, rows)
            s_cur = jnp.where(valid_cur, _bdot(q, _take(kcs, hp, rows), "nt") * scale + bias_cur, NEG_BIG)
            s_prev = jnp.where(valid_prev, _bdot(q, _take(kps, hp, rows), "nt") * scale + bias_far, NEG_BIG)
            m = jnp.maximum(jnp.max(s_cur, axis=-1, keepdims=True), jnp.max(s_prev, axis=-1, keepdims=True))
            p_cur = jnp.exp(s_cur - m)
            p_prev = jnp.exp(s_prev - m)
            l = jnp.sum(p_cur, axis=-1, keepdims=True) + jnp.sum(p_prev, axis=-1, keepdims=True)
            o = (_bdot(p_cur, _take(vcs, hp, rows), "nn") + _bdot(p_prev, _take(vps, hp, rows), "nn")) / l
            _put(os, hp, rows, o)
            _put(ls, hp, rows, jnp.broadcast_to(m + jnp.log(l), o.shape))
            return carry

        lax.fori_loop(0, d // rb, sub_sequences, 0)
        for hi in range(hp):
            hs = slice(hi * HEAD_DIM, (hi + 1) * HEAD_DIM)
            o_ref[:, hs] = os[hi]
            lse_ref[:, hs] = ls[hi]

    per_sec = ATTN_WIDTH // W
    per_grp = ATTN_OUT_WIDTH // W

    def spec(sec, shift):
        return pl.BlockSpec((C, W), lambda b, h: (jnp.maximum(b + shift, 0), sec * per_sec + gi * per_grp + h))

    out = pl.BlockSpec((C, W), lambda b, h: (b, h))
    scr = pltpu.VMEM((hp, C, HEAD_DIM), F32)
    return pl.pallas_call(
        body, grid=(nb, HEADS_PER_GROUP // hp),
        in_specs=[spec(0, 0), spec(1, 0), spec(1, -1), spec(2, 0), spec(2, -1)], out_specs=[out, out],
        out_shape=[jax.ShapeDtypeStruct((S, ATTN_OUT_WIDTH), F32)] * 2, scratch_shapes=[scr] * 7,
        compiler_params=_params("parallel", "parallel"), name=f"attn_fwd_g{gi}")(qkv, qkv, qkv, qkv, qkv)


def _attn_bwd(gi, qkv, do, lse, dp):
    S = qkv.shape[0]
    d, C, nb, hp, W, rb = _attn_tiling(gi, S)
    scale = HEAD_DIM ** -0.5
    n_in = 12

    def body(*refs):
        ins, (dq_ref, dk_ref, dv_ref) = refs[:n_in], refs[n_in:n_in + 3]
        scr = refs[n_in + 3:]
        for src, dst in zip(ins, scr[:n_in]):
            _split_heads(src, dst, hp)
        qs, kcs, vcs, kps, vps, qns, dos, dons, lses, lsens, dps, dpns = scr[:n_in]
        dqs, dks, dvs = scr[n_in:]
        b = pl.program_id(0)
        dist_cur, valid_cur, valid_prev = _attn_masks(b)
        row = lax.broadcasted_iota(jnp.int32, (SPAN, SPAN), 0)
        col = lax.broadcasted_iota(jnp.int32, (SPAN, SPAN), 1)
        valid_next = jnp.logical_and(col >= row, b < nb - 1)

        bias_cur = _batch_bias(gi, d, hp, rb, dist_cur)
        bias_far = _batch_bias(gi, d, hp, rb, dist_cur + SPAN)

        def sub_sequences(it, carry):
            rows = _batch_rows(it, d, rb)
            q, kc, vc, kp, vp, qn, dov, don = (_take(s, hp, rows) for s in (qs, kcs, vcs, kps, vps, qns, dos, dons))
            lse_b, lse_n, dp_b, dp_n = (_take(s, hp, rows) for s in (lses, lsens, dps, dpns))
            p_cur = jnp.where(valid_cur, jnp.exp(_bdot(q, kc, "nt") * scale + bias_cur - lse_b), 0.0)
            p_prev = jnp.where(valid_prev, jnp.exp(_bdot(q, kp, "nt") * scale + bias_far - lse_b), 0.0)
            ds_cur = p_cur * (_bdot(dov, vc, "nt") - dp_b)
            ds_prev = p_prev * (_bdot(dov, vp, "nt") - dp_b)
            _put(dqs, hp, rows, (_bdot(ds_cur, kc, "nn") + _bdot(ds_prev, kp, "nn")) * scale)
            p_next = jnp.where(valid_next, jnp.exp(_bdot(qn, kc, "nt") * scale + bias_far - lse_n), 0.0)
            ds_next = p_next * (_bdot(don, vc, "nt") - dp_n)
            _put(dvs, hp, rows, _bdot(p_cur, dov, "tn") + _bdot(p_next, don, "tn"))
            _put(dks, hp, rows, (_bdot(ds_cur, q, "tn") + _bdot(ds_next, qn, "tn")) * scale)
            return carry

        lax.fori_loop(0, d // rb, sub_sequences, 0)
        for hi in range(hp):
            hs = slice(hi * HEAD_DIM, (hi + 1) * HEAD_DIM)
            dq_ref[:, hs] = dqs[hi].astype(BF16)
            dk_ref[:, hs] = dks[hi].astype(BF16)
            dv_ref[:, hs] = dvs[hi].astype(BF16)

    per_sec = ATTN_WIDTH // W
    per_grp = ATTN_OUT_WIDTH // W

    def qkv_spec(sec, shift):
        return pl.BlockSpec((C, W), lambda b, h: (jnp.clip(b + shift, 0, nb - 1), sec * per_sec + gi * per_grp + h))

    def act_spec(shift):
        return pl.BlockSpec((C, W), lambda b, h: (jnp.clip(b + shift, 0, nb - 1), h))

    out = act_spec(0)
    scr = pltpu.VMEM((hp, C, HEAD_DIM), F32)
    return pl.pallas_call(
        body, grid=(nb, HEADS_PER_GROUP // hp),
        in_specs=[qkv_spec(0, 0), qkv_spec(1, 0), qkv_spec(2, 0), qkv_spec(1, -1), qkv_spec(2, -1), qkv_spec(0, 1),
                  act_spec(0), act_spec(1), act_spec(0), act_spec(1), act_spec(0), act_spec(1)],
        out_specs=[out, out, out], out_shape=[jax.ShapeDtypeStruct((S, ATTN_OUT_WIDTH), BF16)] * 3,
        scratch_shapes=[scr] * (n_in + 3),
        compiler_params=_params("parallel", "parallel"), name=f"attn_bwd_g{gi}")(
            qkv, qkv, qkv, qkv, qkv, qkv, do, do, lse, lse, dp, dp)


def _group_weights(l_refs):
    l0, l1, l2 = (r[...] for r in l_refs)
    m = jnp.maximum(jnp.maximum(l0, l1), l2)
    e = [jnp.exp(l0 - m), jnp.exp(l1 - m), jnp.exp(l2 - m)]
    tot = e[0] + e[1] + e[2]
    return [ei / tot for ei in e]


def _merge_fwd(outs, lses):
    S, W = outs[0].shape
    T = 512

    def body(o0, o1, o2, l0, l1, l2, y_ref):
        w = _group_weights((l0, l1, l2))
        y_ref[...] = (w[0] * o0[...] + w[1] * o1[...] + w[2] * o2[...]).astype(BF16)

    blk = pl.BlockSpec((T, W), lambda i: (i, 0))
    return pl.pallas_call(
        body, grid=(S // T,), in_specs=[blk] * 6, out_specs=blk, out_shape=jax.ShapeDtypeStruct((S, W), BF16),
        compiler_params=_params("parallel"), name="merge_fwd")(*outs, *lses)


def _merge_bwd(dy, outs, lses):
    S, W = outs[0].shape
    T = 512

    def body(dy_ref, o0, o1, o2, l0, l1, l2, do0, do1, do2, dp0, dp1, dp2):
        w = _group_weights((l0, l1, l2))
        dyv = dy_ref[...]
        y = w[0] * o0[...] + w[1] * o1[...] + w[2] * o2[...]
        prod = dyv * y
        dots = []
        for hi in range(HEADS_PER_GROUP):
            hs = slice(hi * HEAD_DIM, (hi + 1) * HEAD_DIM)
            dots.append(jnp.broadcast_to(jnp.sum(prod[:, hs], axis=-1, keepdims=True), (T, HEAD_DIM)))
        dot_b = jnp.concatenate(dots, axis=1)
        for wg, do_ref, dp_ref in zip(w, (do0, do1, do2), (dp0, dp1, dp2)):
            do_ref[...] = (wg * dyv).astype(BF16)
            dp_ref[...] = wg * dot_b

    blk = pl.BlockSpec((T, W), lambda i: (i, 0))
    res = pl.pallas_call(
        body, grid=(S // T,), in_specs=[blk] * 7, out_specs=[blk] * 6,
        out_shape=[jax.ShapeDtypeStruct((S, W), BF16)] * 3 + [jax.ShapeDtypeStruct((S, W), F32)] * 3,
        compiler_params=_params("parallel"), name="merge_bwd")(dy, *outs, *lses)
    return res[:3], res[3:]


def _mix_fwd(pool_y, attn_y, w_pool_out_t, w_attn_out_t, gates_pre, b_gate):
    S = pool_y.shape[0]
    D = D_MODEL
    tm, tn = 1024, 512
    nj = D // tn

    def body(p_ref, a_ref, wp_ref, wa_ref, g0_ref, g1_ref, b0_ref, b1_ref, mixed_ref, yp_ref, ya_ref):
        yp = _dot(p_ref[...], wp_ref[...], "nt")
        ya = _dot(a_ref[...], wa_ref[...], "nt")
        g0 = jax.nn.sigmoid(g0_ref[...] + b0_ref[...])
        g1 = jax.nn.sigmoid(g1_ref[...] + b1_ref[...])
        mixed_ref[...] = (g0 * yp + g1 * ya).astype(BF16)
        yp_ref[...] = yp.astype(BF16)
        ya_ref[...] = ya.astype(BF16)

    out = pl.BlockSpec((tm, tn), lambda i, j: (i, j))
    return pl.pallas_call(
        body, grid=(S // tm, nj),
        in_specs=[pl.BlockSpec((tm, POOL_WIDTH), lambda i, j: (i, 0)),
                  pl.BlockSpec((tm, ATTN_OUT_WIDTH), lambda i, j: (i, 0)),
                  pl.BlockSpec((tn, POOL_WIDTH), lambda i, j: (j, 0)),
                  pl.BlockSpec((tn, ATTN_OUT_WIDTH), lambda i, j: (j, 0)),
                  pl.BlockSpec((tm, tn), lambda i, j: (i, j)), pl.BlockSpec((tm, tn), lambda i, j: (i, j + nj)),
                  pl.BlockSpec((1, tn), lambda i, j: (0, j)), pl.BlockSpec((1, tn), lambda i, j: (0, j + nj))],
        out_specs=[out, out, out], out_shape=[jax.ShapeDtypeStruct((S, D), BF16)] * 3,
        compiler_params=_params("parallel", "parallel"), name="mix_fwd")(
            pool_y, attn_y, w_pool_out_t, w_attn_out_t, gates_pre, gates_pre, b_gate, b_gate)


def _mix_bwd(dmixed, gates_pre, b_gate, y_pool, y_attn):
    S = dmixed.shape[0]
    D = D_MODEL
    T = 256

    def body(dm_ref, gp_ref, b_ref, yp_ref, ya_ref, dyp_ref, dya_ref, dg_ref, db_ref):
        i = pl.program_id(0)
        dm = dm_ref[...].astype(F32)
        parts = []
        for br, (y_ref, dy_ref) in enumerate(((yp_ref, dyp_ref), (ya_ref, dya_ref))):
            cs = slice(br * D, (br + 1) * D)
            g = jax.nn.sigmoid(gp_ref[:, cs] + b_ref[:, cs])
            dy_ref[...] = (dm * g).astype(BF16)
            dpre = dm * y_ref[...].astype(F32) * g * (1.0 - g)
            dg_ref[:, cs] = dpre.astype(BF16)
            parts.append(jnp.sum(dpre, axis=0, keepdims=True))

        @pl.when(i == 0)
        def _():
            db_ref[:, 0:D] = parts[0]
            db_ref[:, D:2 * D] = parts[1]

        @pl.when(i > 0)
        def _():
            db_ref[:, 0:D] += parts[0]
            db_ref[:, D:2 * D] += parts[1]

    row = pl.BlockSpec((T, D), lambda i: (i, 0))
    row2 = pl.BlockSpec((T, 2 * D), lambda i: (i, 0))
    vec2 = pl.BlockSpec((1, 2 * D), lambda i: (0, 0))
    return pl.pallas_call(
        body, grid=(S // T,), in_specs=[row, row2, vec2, row, row], out_specs=[row, row, row2, vec2],
        out_shape=[jax.ShapeDtypeStruct((S, D), BF16), jax.ShapeDtypeStruct((S, D), BF16),
                   jax.ShapeDtypeStruct((S, 2 * D), BF16), jax.ShapeDtypeStruct((1, 2 * D), F32)],
        compiler_params=_params("arbitrary"), name="mix_bwd")(dmixed, gates_pre, b_gate, y_pool, y_attn)


_CONV_T, _CONV_C = 512, 512


def _conv_taps(ext_ref, h, w_ref, b_ref, cols=slice(None)):
    ext = ext_ref[h, :, cols]
    x2 = pltpu.roll(ext, 2, 0)[HALO:]
    x1 = pltpu.roll(ext, 1, 0)[HALO:]
    x0 = ext[HALO:]
    y = b_ref[:, cols] + w_ref[0:1, cols] * x2
    y = y + w_ref[1:2, cols] * x1
    y = y + w_ref[2:3, cols] * x0
    return y, (x2, x1, x0)


def _gelu_parts(a):
    cdf = 0.5 * (1.0 + lax.erf(a * (1.0 / math.sqrt(2.0))))
    return cdf, a * cdf


def _conv_specs(S):
    T, C = _CONV_T, _CONV_C
    nj = D_FF // C
    blk = pl.BlockSpec((2, T, C), lambda i, j: (0, i, j))
    per = T // HALO
    halo = pl.BlockSpec((2, HALO, C), lambda i, j: (0, jnp.maximum(i * per - 1, 0), j))
    w_a = pl.BlockSpec((3, C), lambda i, j: (0, j))
    w_b = pl.BlockSpec((3, C), lambda i, j: (0, j + nj))
    b_a = pl.BlockSpec((1, C), lambda i, j: (0, j))
    b_b = pl.BlockSpec((1, C), lambda i, j: (0, j + nj))
    return T, C, nj, blk, halo, w_a, w_b, b_a, b_b


def _conv_load(ext_ref, halo_ref, up_ref, first):
    for h in range(2):
        ext_ref[h, 0:HALO, :] = jnp.where(first, 0.0, halo_ref[h].astype(F32))
        ext_ref[h, HALO:, :] = up_ref[h].astype(F32)


def _up_proj_convffn(h, w_up_t, conv_w, conv_b):
    S = h.shape[0]
    tm, C = 1024, _CONV_C
    nj = D_FF // C

    def body(h_ref, wa_ref, wb_ref, cwa_ref, cwb_ref, cba_ref, cbb_ref, up_ref, act_ref, ext_ref, carry_ref):
        i, j = pl.program_id(0), pl.program_id(1)
        for half, w_ref in enumerate((wa_ref, wb_ref)):
            pre = _dot(h_ref[...], w_ref[...], "nt").astype(BF16)
            up_ref[half] = pre
            ext_ref[half, 0:HALO, :] = jnp.where(i == 0, 0.0, carry_ref[j, half])
            ext_ref[half, HALO:, :] = pre.astype(F32)
            carry_ref[j, half] = ext_ref[half, tm:tm + HALO, :]
        a, _ = _conv_taps(ext_ref, 0, cwa_ref, cba_ref)
        b, _ = _conv_taps(ext_ref, 1, cwb_ref, cbb_ref)
        _, gelu = _gelu_parts(a)
        act_ref[...] = (gelu * b).astype(BF16)

    def w_spec(half):
        return pl.BlockSpec((C, D_MODEL), lambda i, j: (j + half * nj, 0))

    def c_spec(rows, half):
        return pl.BlockSpec((rows, C), lambda i, j: (0, j + half * nj))

    return pl.pallas_call(
        body, grid=(S // tm, nj),
        in_specs=[pl.BlockSpec((tm, D_MODEL), lambda i, j: (i, 0)), w_spec(0), w_spec(1),
                  c_spec(3, 0), c_spec(3, 1), c_spec(1, 0), c_spec(1, 1)],
        out_specs=[pl.BlockSpec((2, tm, C), lambda i, j: (0, i, j)), pl.BlockSpec((tm, C), lambda i, j: (i, j))],
        out_shape=[jax.ShapeDtypeStruct((2, S, D_FF), BF16), jax.ShapeDtypeStruct((S, D_FF), BF16)],
        scratch_shapes=[pltpu.VMEM((2, HALO + tm, C), F32), pltpu.VMEM((nj, 2, HALO, C), F32)],
        compiler_params=_params("arbitrary", "arbitrary"), name="up_proj_convffn")(
            h, w_up_t, w_up_t, conv_w, conv_w, conv_b, conv_b)


def _convffn_bwd_a(up_pre, dx, w_down, conv_w, conv_b):
    S = up_pre.shape[1]
    T, C, nj, blk, halo, w_a, w_b, b_a, b_b = _conv_specs(S)
    def swap(spec):
        return pl.BlockSpec(spec.block_shape, lambda j, i, _m=spec.index_map: _m(i, j))

    def body(up_ref, halo_ref, dx_ref, wd_ref, wa_ref, wb_ref, ba_ref, bb_ref, dup_ref, dwa_ref, dwb_ref, dba_ref,
             dbb_ref, ext_ref):
        i = pl.program_id(1)
        _conv_load(ext_ref, halo_ref, up_ref, i == 0)

        @pl.when(i == 0)
        def _():
            for acc_ref in (dwa_ref, dwb_ref, dba_ref, dbb_ref):
                acc_ref[...] = jnp.zeros_like(acc_ref)

        for c in range(C // MXU_DIM):
            cs = slice(c * MXU_DIM, (c + 1) * MXU_DIM)
            dact = _dot(dx_ref[...], wd_ref[cs, :], "nt")
            a, xa = _conv_taps(ext_ref, 0, wa_ref, ba_ref, cs)
            b, xb = _conv_taps(ext_ref, 1, wb_ref, bb_ref, cs)
            cdf, gelu = _gelu_parts(a)
            dgelu = cdf + a * (jnp.exp(-0.5 * a * a) * (1.0 / math.sqrt(2.0 * math.pi)))
            da = dact * b * dgelu
            db = dact * gelu
            dup_ref[0, :, cs] = da.astype(BF16)
            dup_ref[1, :, cs] = db.astype(BF16)
            for dval, xs, dw_ref, dbias_ref in ((da, xa, dwa_ref, dba_ref), (db, xb, dwb_ref, dbb_ref)):
                dw_ref[:, cs] += jnp.concatenate([jnp.sum(dval * xk, axis=0, keepdims=True) for xk in xs], axis=0)
                dbias_ref[:, cs] += jnp.sum(dval, axis=0, keepdims=True)

    dx_spec = pl.BlockSpec((T, D_MODEL), lambda j, i: (i, 0))
    wd_spec = pl.BlockSpec((C, D_MODEL), lambda j, i: (j, 0))
    dw_spec = pl.BlockSpec((3, C), lambda j, i: (0, j))
    db_spec = pl.BlockSpec((1, C), lambda j, i: (0, j))
    res = pl.pallas_call(
        body, grid=(nj, S // T),
        in_specs=[swap(blk), swap(halo), dx_spec, wd_spec, swap(w_a), swap(w_b), swap(b_a), swap(b_b)],
        out_specs=[swap(blk), dw_spec, dw_spec, db_spec, db_spec],
        out_shape=[jax.ShapeDtypeStruct((2, S, D_FF), BF16), jax.ShapeDtypeStruct((3, D_FF), F32),
                   jax.ShapeDtypeStruct((3, D_FF), F32), jax.ShapeDtypeStruct((1, D_FF), F32),
                   jax.ShapeDtypeStruct((1, D_FF), F32)],
        scratch_shapes=[pltpu.VMEM((2, HALO + T, C), F32)],
        compiler_params=_params("parallel", "arbitrary"), name="convffn_bwd_a")(
            up_pre, up_pre, dx, w_down, conv_w, conv_w, conv_b, conv_b)
    dup, dwa, dwb, dba, dbb = res
    dconv_w = jnp.concatenate([dwa, dwb], axis=1)
    dconv_b = jnp.concatenate([dba, dbb], axis=1)
    return dup, dconv_w, dconv_b


def _convffn_bwd_b(dup, conv_w, w_up_t):
    S = dup.shape[1]
    tm, tk = 512, D_FF // 4
    per_half = D_FF // tk
    nk = 2 * per_half
    per = tm // HALO
    n_blk = S // tm

    def body(d_ref, halo_ref, cw_ref, w_ref, dpre_ref, dh_ref, ext_ref, acc_ref):
        i, k = pl.program_id(0), pl.program_id(1)
        n = tm + HALO
        ext_ref[0:tm, :] = d_ref[...].astype(F32)
        ext_ref[tm:, :] = jnp.where(i == n_blk - 1, 0.0, halo_ref[...].astype(F32))
        ext = ext_ref[...]
        y = cw_ref[2:3, :] * ext[0:tm] + cw_ref[1:2, :] * pltpu.roll(ext, n - 1, 0)[0:tm]
        y = y + cw_ref[0:1, :] * pltpu.roll(ext, n - 2, 0)[0:tm]
        pre = y.astype(BF16)
        dpre_ref[...] = pre
        prod = _dot(pre, w_ref[...], "nn")

        @pl.when(k == 0)
        def _():
            acc_ref[...] = prod

        @pl.when(k > 0)
        def _():
            acc_ref[...] += prod

        @pl.when(k == nk - 1)
        def _():
            dh_ref[...] = acc_ref[...].astype(BF16)

    blk = pl.BlockSpec((None, tm, tk), lambda i, k: (k // per_half, i, k % per_half))
    halo = pl.BlockSpec((None, HALO, tk),
                        lambda i, k: (k // per_half, jnp.minimum((i + 1) * per, S // HALO - 1), k % per_half))
    return pl.pallas_call(
        body, grid=(n_blk, nk),
        in_specs=[blk, halo, pl.BlockSpec((3, tk), lambda i, k: (0, k)), pl.BlockSpec((tk, D_MODEL), lambda i, k: (k, 0))],
        out_specs=[blk, pl.BlockSpec((tm, D_MODEL), lambda i, k: (i, 0))],
        out_shape=[jax.ShapeDtypeStruct((2, S, D_FF), BF16), jax.ShapeDtypeStruct((S, D_MODEL), BF16)],
        scratch_shapes=[pltpu.VMEM((tm + HALO, tk), F32), pltpu.VMEM((tm, D_MODEL), F32)],
        compiler_params=_params("parallel", "arbitrary"), name="convffn_bwd_b")(dup, dup, conv_w, w_up_t)


def _position():
    return lax.axis_index("x"), lax.axis_index("y"), lax.axis_index("c")


def _flip(pos, k):
    x, y, c = pos
    return (1 - x if k & 4 else x, 1 - y if k & 2 else y, 1 - c if k & 1 else c)


def _index(pos):
    return 4 * pos[0] + 2 * pos[1] + pos[2]


_HBM = pl.BlockSpec(memory_space=pltpu.HBM)
_SEM = pl.BlockSpec(memory_space=pltpu.SEMAPHORE)
_ANY = pl.BlockSpec(memory_space=pl.ANY)
_EFFECT = pltpu.SideEffectType.DATAFLOW_SIDE_EFFECTING


def _in_hbm(a):
    return pltpu.with_memory_space_constraint(a, pltpu.HBM)


def _split_copies(name, srcs, lands, n_copies, plan, start_after, wait_after):
    ns, nl = len(srcs), len(lands)
    start_after = tuple(start_after)
    na = len(start_after)

    def start_body(*refs):
        src_refs, land_refs = refs[:ns], refs[ns:ns + nl]
        send_sems, recv_sems = refs[ns + nl + na], refs[ns + nl + na + 1]
        token = refs[-1]
        for k, (src, dst, _, peer) in enumerate(plan(src_refs, land_refs)):
            pltpu.make_async_remote_copy(src_ref=src, dst_ref=dst, send_sem=send_sems.at[k], recv_sem=recv_sems.at[k],
                                         device_id=peer, device_id_type=MESH).start()
        token[...] = jnp.zeros_like(token)

    thru = [pltpu.HBM(a.shape, a.dtype) for a in list(srcs) + list(lands)]
    res = pl.pallas_call(
        start_body, name=name + "_start",
        out_shape=[pltpu.SemaphoreType.DMA((n_copies,)), pltpu.SemaphoreType.DMA((n_copies,))] + thru
        + [jax.ShapeDtypeStruct((8, 128), F32)],
        in_specs=[_HBM] * (ns + nl) + [_ANY] * na,
        out_specs=[_SEM, _SEM] + [_HBM] * (ns + nl) + [pl.BlockSpec(memory_space=pltpu.VMEM)],
        input_output_aliases={i: 2 + i for i in range(ns + nl)},
        compiler_params=pltpu.CompilerParams(has_side_effects=_EFFECT),
    )(*[_in_hbm(a) for a in list(srcs) + list(lands)], *start_after)
    send_sems, recv_sems, token = res[0], res[1], res[-1]
    thru_vals = res[2:-1]

    def wait():
        def wait_body(*refs):
            src_refs, land_refs = refs[:ns], refs[ns:ns + nl]
            s_sems, r_sems = refs[ns + nl], refs[ns + nl + 1]
            for k, (src, _, dst, peer) in enumerate(plan(src_refs, land_refs)):
                cp = pltpu.make_async_remote_copy(src_ref=src, dst_ref=dst, send_sem=s_sems.at[k], recv_sem=r_sems.at[k],
                                                  device_id=peer, device_id_type=MESH)
                cp.wait_send()
                cp.wait_recv()

        out = pl.pallas_call(
            wait_body, name=name + "_wait", out_shape=thru,
            in_specs=[_HBM] * (ns + nl) + [_SEM, _SEM, _ANY], out_specs=[_HBM] * (ns + nl),
            input_output_aliases={i: i for i in range(ns + nl)},
            compiler_params=pltpu.CompilerParams(has_side_effects=_EFFECT),
        )(*thru_vals, send_sems, recv_sems, wait_after())
        return list(out[:ns]), list(out[ns:])

    return wait, token


def _gather_level1_plan(n):
    def plan(src_refs, land_refs):
        me = _position()
        x, y, c = me
        peers = [(x, y, 1 - c), (1 - x, y, c), (x, 1 - y, c), (1 - x, 1 - y, c)]
        return [(src_refs[a], land_refs[a].at[_index(me)], land_refs[a].at[_index(p)], p)
                for a in range(n) for p in peers]
    return plan


def _gather_level2(lands):
    n = len(lands)

    def body(*refs):
        lin = refs[:n]
        send_sems, recv_sems = refs[2 * n:]
        x, y, c = _position()
        sibling = (x, y, 1 - c)
        chips = [(1 - x, y), (x, 1 - y), (1 - x, 1 - y)]

        def copy(a, j, core):
            slot = lin[a].at[_index((*chips[j], core))]
            return pltpu.make_async_remote_copy(
                src_ref=slot, dst_ref=slot, send_sem=send_sems.at[3 * a + j], recv_sem=recv_sems.at[3 * a + j],
                device_id=sibling, device_id_type=MESH)

        sent = [copy(a, j, c) for a in range(n) for j in range(3)]
        for cp in sent:
            cp.start()
        for a in range(n):
            for j in range(3):
                copy(a, j, 1 - c).wait_recv()
        for cp in sent:
            cp.wait_send()

    return pl.pallas_call(
        body, out_shape=[jax.ShapeDtypeStruct(l.shape, l.dtype) for l in lands],
        in_specs=[_ANY] * n, out_specs=[_ANY] * n, input_output_aliases={a: a for a in range(n)},
        scratch_shapes=[pltpu.SemaphoreType.DMA((3 * n,)), pltpu.SemaphoreType.DMA((3 * n,))],
        name="gather_level2")(*lands)


def _gather_direct_plan(n):
    def plan(src_refs, land_refs):
        me = _position()
        peers = [_flip(me, k) for k in range(1, N_DEV)]
        return [(src_refs[a], land_refs[a].at[_index(me)], land_refs[a].at[_index(p)], p)
                for a in range(n) for p in peers]
    return plan


def _place_own(shards):
    me = _index(_position())
    return [lax.dynamic_update_slice(lax.empty((N_DEV,) + s.shape, s.dtype), s[None], (me, 0, 0)) for s in shards]


def _exchange_direct_plan(n):
    def plan(src_refs, land_refs):
        me = _position()
        peers = [_flip(me, k) for k in range(1, N_DEV)]
        return [(src_refs[a].at[_index(p)], land_refs[a].at[_index(me)], land_refs[a].at[_index(p)], p)
                for a in range(n) for p in peers]
    return plan


def _adamw(w, g, m, v):
    m = ADAM_B1 * m + (1.0 - ADAM_B1) * g
    v = ADAM_B2 * v + (1.0 - ADAM_B2) * (g * g)
    m_hat = m / (1.0 - ADAM_B1 ** ADAM_STEP)
    v_hat = v / (1.0 - ADAM_B2 ** ADAM_STEP)
    delta = -ADAM_LR * (m_hat / (jnp.sqrt(v_hat) + ADAM_EPS) + ADAM_WD * w)
    return delta, m, v


def _row_tile(rows, unit, cap=256):
    best = rows
    for t in range(unit, min(rows, cap) + 1, unit):
        if rows % t == 0:
            best = t
    return best


def _sum_partials(name, recv):
    n_src, R, C = recv.shape
    T = _row_tile(R, 16)

    def body(r_ref, g_out):
        g = r_ref[0].astype(F32)
        for s in range(1, n_src):
            g = g + r_ref[s].astype(F32)
        g_out[...] = g

    return pl.pallas_call(
        body, grid=(R // T,), in_specs=[pl.BlockSpec((n_src, T, C), lambda i: (0, i, 0))],
        out_specs=pl.BlockSpec((T, C), lambda i: (i, 0)), out_shape=jax.ShapeDtypeStruct((R, C), F32),
        compiler_params=_params("parallel"), name=name)(recv)


def _adamw_update(name, w, g, m, v):
    R, C = w.shape
    T = _row_tile(R, 8)

    def body(w_ref, g_ref, m_ref, v_ref, d_out, m_out, v_out):
        delta, mn, vn = _adamw(w_ref[...], g_ref[...], m_ref[...], v_ref[...])
        d_out[...] = delta
        m_out[...] = mn
        v_out[...] = vn

    blk = pl.BlockSpec((T, C), lambda i: (i, 0))
    return pl.pallas_call(
        body, grid=(R // T,), in_specs=[blk] * 4, out_specs=[blk] * 3,
        out_shape=[jax.ShapeDtypeStruct((R, C), F32)] * 3, compiler_params=_params("parallel"), name=name)(w, g, m, v)


def _gather_vmem(p_ref, gath, send_sems, recv_sems):
    me = _position()
    gath[_index(me)] = p_ref[...]
    copies = []
    for k in range(1, N_DEV):
        peer = _flip(me, k)
        copies.append(pltpu.make_async_remote_copy(
            src_ref=p_ref, dst_ref=gath.at[_index(me)], send_sem=send_sems.at[k - 1],
            recv_sem=recv_sems.at[k - 1], device_id=peer, device_id_type=MESH))
    for cp in copies:
        cp.start()
    for k in range(1, N_DEV):
        peer = _flip(me, k)
        pltpu.make_async_remote_copy(
            src_ref=p_ref, dst_ref=gath.at[_index(peer)], send_sem=send_sems.at[k - 1],
            recv_sem=recv_sems.at[k - 1], device_id=peer, device_id_type=MESH).wait_recv()
    for cp in copies:
        cp.wait_send()


def _all_gather_small(p):
    R, C = p.shape

    def body(p_ref, out_ref, send_sems, recv_sems):
        _gather_vmem(p_ref, out_ref, send_sems, recv_sems)

    vm = pl.BlockSpec(memory_space=pltpu.VMEM)
    return pl.pallas_call(
        body, in_specs=[vm], out_specs=vm, out_shape=jax.ShapeDtypeStruct((N_DEV, R, C), F32),
        scratch_shapes=[pltpu.SemaphoreType.DMA((7,)), pltpu.SemaphoreType.DMA((7,))],
        name="all_gather_small")(p)


def _small_allreduce_adamw(part, w, m, v):
    R, C = part.shape

    def body(p_ref, w_ref, m_ref, v_ref, g_out, d_out, m_out, v_out, gath, send_sems, recv_sems):
        _gather_vmem(p_ref, gath, send_sems, recv_sems)
        g = gath[0]
        for s in range(1, N_DEV):
            g = g + gath[s]
        delta, mn, vn = _adamw(w_ref[...], g, m_ref[...], v_ref[...])
        g_out[...] = g
        d_out[...] = delta
        m_out[...] = mn
        v_out[...] = vn

    vm = pl.BlockSpec(memory_space=pltpu.VMEM)
    return pl.pallas_call(
        body, in_specs=[vm] * 4, out_specs=[vm] * 4, out_shape=[jax.ShapeDtypeStruct((R, C), F32)] * 4,
        scratch_shapes=[pltpu.VMEM((N_DEV, R, C), F32), pltpu.SemaphoreType.DMA((7,)), pltpu.SemaphoreType.DMA((7,))],
        name="small_allreduce_adamw")(part, w, m, v)


_BIG = ("w_in", "w_up", "w_down", "w_out", "w_pool_out", "w_attn_out")
_TRANSPOSED = ("w_in", "w_up", "w_pool_out", "w_attn_out")
_SMALL = (("g_mix", 2048), ("b_gate", 4096), ("pool_scale", 1024), ("g_ffn", 2048), ("conv_b", 11264), ("g_final", 2048))
CONV_ROWS = 8


def _to_comm(name, shard):
    return shard.T if name in _TRANSPOSED else shard


def _pack_small(vals):
    flat = [vals[n].reshape(-1) for n, _ in _SMALL]
    used = sum(f.shape[0] for f in flat)
    flat.append(jnp.zeros((SMALL_ROWS * PACK_COLS - used,), F32))
    return jnp.concatenate(flat).reshape(SMALL_ROWS, PACK_COLS)


def _unpack_small(packed, like):
    flat = packed.reshape(-1)
    out, off = {}, 0
    for name, n in _SMALL:
        out[name] = flat[off:off + n].reshape(like[name].shape)
        off += n
    return out


def _local_step(x, target, W, small, first_token=0.0, first_weights=None, late_weights=None, emit=None):
    S = x.shape[0]
    o_qkv, o_gate = POOL_WIDTH, POOL_WIDTH + QKV_WIDTH
    if emit is None:
        emit = lambda grads: None

    def zero(token):
        return 0.0 if token is None else token[0, 0]

    h1 = _rms_fwd("rms1_fwd", x, small["g_mix"] + first_token)
    tok = None
    if first_weights is not None:
        w_first, tok = first_weights(h1)
        W = dict(W, **w_first)
    w_in_t = W["w_in"]
    u = _mm_nt("proj_u", h1, w_in_t, tm=2048, tn=512, tk=2048, o_dtype=F32, n_rows=POOL_WIDTH, b_row_off=0,
               after=tok)
    qkv = _mm_nt("proj_qkv", h1, w_in_t, tm=2048, tn=512, tk=2048, o_dtype=BF16, n_rows=QKV_WIDTH, b_row_off=o_qkv,
                 after=tok)
    gates_pre = _mm_nt("proj_gates", h1, w_in_t, tm=2048, tn=512, tk=2048, o_dtype=BF16, n_rows=2 * D_MODEL,
                       b_row_off=o_gate, after=tok)
    pool_y = _pool_fwd(u, W["w_pool_lin"], small["pool_scale"])
    outs, lses = zip(*[_attn_fwd(gi, qkv) for gi in range(3)])
    attn_y = _merge_fwd(outs, lses)
    if late_weights is not None:
        W = dict(W, **late_weights(0, attn_y))
    w_up_t = W["w_up"]
    mixed, y_pool, y_attn = _mix_fwd(pool_y, attn_y, W["w_pool_out"], W["w_attn_out"], gates_pre, small["b_gate"])
    x1 = _mm_nn("out_proj", mixed, W["w_out"], tm=1024, tn=1024, tk=2048, o_dtype=F32, res=x)

    h2 = _rms_fwd("rms2_fwd", x1, small["g_ffn"])
    nj = D_FF // 512
    up_pre, act = _up_proj_convffn(h2, w_up_t, small["conv_w_full"], small["conv_b"])
    if late_weights is not None:
        W = dict(W, **late_weights(1, act))
    x2 = _mm_nn("down_proj", act, W["w_down"], tm=1024, tn=1024, tk=D_FF // 2, o_dtype=F32, res=x1)

    dx2, dx2_b, dg_final, sq_cols = _final_loss("final_loss", x2, small["g_final"], target)

    dw_down = _mm_tn("dw_down", act, dx2_b, tm=512, tn=2048, tk=TOKEN_TK)
    tok = emit(dict(w_down=dw_down))
    dup, dconv_w, dconv_b = _convffn_bwd_a(up_pre, dx2_b, W["w_down"], small["conv_w_full"],
                                           small["conv_b"] + zero(tok))
    dup_pre, dh2 = _convffn_bwd_b(dup, small["conv_w_full"], w_up_t)
    dw_up = _mm("dw_up", dup_pre, h2, mode="tn", grid=(2 * nj, 1, S // TOKEN_TK),
                a_spec=pl.BlockSpec((None, TOKEN_TK, 512), lambda i, j, k: (i // nj, k, i % nj)),
                b_spec=pl.BlockSpec((TOKEN_TK, D_MODEL), lambda i, j, k: (k, 0)),
                o_spec=pl.BlockSpec((512, D_MODEL), lambda i, j, k: (i, 0)),
                o_shape=(2 * D_FF, D_MODEL), o_dtype=BF16, acc_shape=(512, D_MODEL))
    tok = emit(dict(w_up=dw_up, conv_w=dconv_w))
    dx1, dx1_b, dg_ffn = _rms_bwd("rms2_bwd", x1, small["g_ffn"] + zero(tok), dh2, dx2)

    dmixed = _mm_nt("d_mixed", dx1_b, W["w_out"], tm=1024, tn=1024, tk=2048, o_dtype=BF16)
    dw_out = _mm_tn("dw_out", mixed, dx1_b, tm=512, tn=2048, tk=TOKEN_TK)
    tok = emit(dict(w_out=dw_out))
    dy_pool, dy_attn, dgates, db_gate = _mix_bwd(dmixed, gates_pre, small["b_gate"] + zero(tok), y_pool, y_attn)
    dpool_y = _mm_nn("d_pool_y", dy_pool, W["w_pool_out"], tm=1024, tn=1024, tk=2048, o_dtype=F32)
    dw_pool_out = _mm_tn("dw_pool_out", dy_pool, pool_y, tm=1024, tn=1024, tk=TOKEN_TK)
    dattn_y = _mm_nn("d_attn_y", dy_attn, W["w_attn_out"], tm=1024, tn=512, tk=2048, o_dtype=F32)
    dw_attn_out = _mm_tn("dw_attn_out", dy_attn, attn_y, tm=1024, tn=512, tk=TOKEN_TK)
    e, dw_pool_lin, dpool_scale = _pool_bwd_a(u, dpool_y, W["w_pool_lin"], small["pool_scale"])
    du = _pool_bwd_b(e)
    dos, dps = _merge_bwd(dattn_y, outs, lses)
    dqkv = [_attn_bwd(gi, qkv, dos[gi], lses[gi], dps[gi]) for gi in range(3)]
    dproj = jnp.concatenate([du] + [dqkv[gi][sec] for sec in range(3) for gi in range(3)] + [dgates], axis=1)
    dw_in = _mm_tn("dw_in", dproj, h1, tm=512, tn=2048, tk=TOKEN_TK)
    tok = emit(dict(w_in=dw_in, w_pool_out=dw_pool_out, w_attn_out=dw_attn_out, w_pool_lin=dw_pool_lin))
    dh1 = _mm_nn("d_h1", dproj, w_in_t, tm=1024, tn=1024, tk=IN_WIDTH // 4, o_dtype=BF16,
                 after=tok)
    grad_x, _, dg_mix = _rms_bwd("rms1_bwd", x, small["g_mix"] + zero(tok), dh1, dx1)

    grads = dict(w_in=dw_in, w_up=dw_up, w_down=dw_down, w_out=dw_out, w_pool_out=dw_pool_out,
                 w_attn_out=dw_attn_out, w_pool_lin=dw_pool_lin, conv_w=dconv_w)
    small_grads = dict(g_mix=dg_mix, b_gate=db_gate, pool_scale=dpool_scale, g_ffn=dg_ffn, conv_b=dconv_b,
                       g_final=dg_final)
    return sq_cols, grad_x, grads, small_grads


def kernel(x, g_mix, w_in, b_gate, w_pool_lin, pool_scale, w_pool_out, w_attn_out, w_out, g_ffn, w_up, conv_w, conv_b, w_down, g_final, loss_target, m_g_mix, m_w_in, m_b_gate, m_w_pool_lin, m_pool_scale, m_w_pool_out, m_w_attn_out, m_w_out, m_g_ffn, m_w_up, m_conv_w, m_conv_b, m_w_down, m_g_final, v_g_mix, v_w_in, v_b_gate, v_w_pool_lin, v_pool_scale, v_w_pool_out, v_w_attn_out, v_w_out, v_g_ffn, v_w_up, v_conv_w, v_conv_b, v_w_down, v_g_final):
    given = dict(g_mix=g_mix, w_in=w_in, b_gate=b_gate, w_pool_lin=w_pool_lin, pool_scale=pool_scale,
                 w_pool_out=w_pool_out, w_attn_out=w_attn_out, w_out=w_out, g_ffn=g_ffn, w_up=w_up, conv_w=conv_w,
                 conv_b=conv_b, w_down=w_down, g_final=g_final)
    mom_m = dict(g_mix=m_g_mix, w_in=m_w_in, b_gate=m_b_gate, w_pool_lin=m_w_pool_lin, pool_scale=m_pool_scale,
                 w_pool_out=m_w_pool_out, w_attn_out=m_w_attn_out, w_out=m_w_out, g_ffn=m_g_ffn, w_up=m_w_up,
                 conv_w=m_conv_w, conv_b=m_conv_b, w_down=m_w_down, g_final=m_g_final)
    mom_v = dict(g_mix=v_g_mix, w_in=v_w_in, b_gate=v_b_gate, w_pool_lin=v_w_pool_lin, pool_scale=v_pool_scale,
                 w_pool_out=v_w_pool_out, w_attn_out=v_w_attn_out, w_out=v_w_out, g_ffn=v_g_ffn, w_up=v_w_up,
                 conv_w=v_conv_w, conv_b=v_conv_b, w_down=v_w_down, g_final=v_g_final)
    def full(g):
        return g.reshape(N_DEV * g.shape[1], g.shape[2])

    shard = dict(w_in=_to_comm("w_in", w_in[0]).astype(BF16))
    W = {}
    lin_all = _all_gather_small(w_pool_lin[0].reshape(4 * 32, POOL_GROUP_WIDTH))
    W["w_pool_lin"] = jnp.moveaxis(lin_all.reshape(N_DEV, 4, 32, POOL_GROUP_WIDTH), 0, 1).reshape(
        4, POOL_GROUP_WIDTH, POOL_GROUP_WIDTH).astype(BF16)
    conv_w_all = _all_gather_small(jnp.pad(conv_w[0], ((0, CONV_ROWS - 3), (0, 0))))
    conv_w_full = jnp.moveaxis(conv_w_all[:, :3, :], 0, 1).reshape(3, 2 * D_FF)

    hold = {}
    w_in_wait, first_token = _split_copies(
        "gather_w_in", [shard["w_in"]], _place_own([shard["w_in"]]), 4, _gather_level1_plan(1),
        start_after=(lin_all, conv_w_all), wait_after=lambda: hold["h1"])

    stages = (("w_out", "w_pool_out", "w_attn_out", "w_up"), ("w_down",))
    stage_after = [None, None]
    stage_wait = []
    for n in (n for names in stages for n in names):
        shard[n] = _to_comm(n, given[n][0] + first_token[0, 0]).astype(BF16)

    def first_weights(h1):
        hold["h1"] = h1
        w_in_full = full(_gather_level2(w_in_wait()[1])[0])
        order_after = (w_in_full,)
        for k, names in enumerate(stages):
            srcs = [shard[n] for n in names]
            wait, token = _split_copies(f"gather_stage{k}", srcs, _place_own(srcs), 7 * len(names),
                                        _gather_direct_plan(len(names)), start_after=order_after,
                                        wait_after=lambda k=k: stage_after[k])
            stage_wait.append(wait)
            order_after = (token,)
        return dict(w_in=w_in_full), token

    def late_weights(stage, after):
        stage_after[stage] = after
        return {n: full(l) for n, l in zip(stages[stage], stage_wait[stage]()[1])}

    small = dict(g_mix=g_mix, b_gate=b_gate, pool_scale=pool_scale, g_ffn=g_ffn, conv_b=conv_b,
                 g_final=g_final.reshape(1, D_MODEL), conv_w_full=conv_w_full)

    pending = []
    done = {}

    def emit(group):
        names = list(group)
        parts = []
        for n in names:
            g = group[n]
            if n == "w_pool_lin":
                g = jnp.moveaxis(g.reshape(4, N_DEV, 32, POOL_GROUP_WIDTH), 1, 0).reshape(
                    N_DEV, 4 * 32, POOL_GROUP_WIDTH).astype(BF16)
            elif n == "conv_w":
                g = jnp.pad(jnp.moveaxis(g.reshape(3, N_DEV, 2 * D_FF // N_DEV), 1, 0),
                            ((0, 0), (0, CONV_ROWS - 3), (0, 0))).astype(BF16)
            else:
                g = g.reshape(N_DEV, g.shape[0] // N_DEV, g.shape[1])
            parts.append(g)
        me = _index(_position())
        lands = [lax.dynamic_update_slice(lax.empty(p.shape, p.dtype), lax.dynamic_index_in_dim(p, me, 0),
                                          (me, 0, 0)) for p in parts]
        wait, token = _split_copies(
            "exchange_" + names[0], parts, lands, 7 * len(names), _exchange_direct_plan(len(names)),
            start_after=(), wait_after=lambda: done["grad_x"])
        pending.append((names, wait))
        return token

    prepared = sum(shard[n][0, 0].astype(F32) for names in stages for n in names) * 0.0
    sq_cols, grad_x, _, small_grads = _local_step(x[0], loss_target[0], W, small,
                                                  first_token=first_token[0, 0] + prepared,
                                                  first_weights=first_weights, late_weights=late_weights, emit=emit)
    done["grad_x"] = grad_x
    loss = lax.psum(0.5 / D_MODEL * jnp.sum(sq_cols), AXES)

    res = {}
    for n, r in [(n, r) for names, wait in pending for n, r in zip(names, wait()[1])]:
        g = _sum_partials("sum_" + n, r)
        if n in _TRANSPOSED:
            g = g.T
        elif n == "conv_w":
            g = g[:3]
        shape = given[n].shape
        rows = math.prod(shape[:-1])
        w2, m2, v2 = (d[n].reshape(rows, shape[-1]) for d in (given, mom_m, mom_v))
        upd = _adamw_update("adamw_" + n, w2, g.reshape(rows, shape[-1]), m2, v2)
        res[n] = [a.reshape(shape) for a in (g,) + tuple(upd)]

    packed_small = [_pack_small(d) for d in (given, mom_m, mom_v)]
    res_small = [_unpack_small(p, given) for p in _small_allreduce_adamw(_pack_small(small_grads), *packed_small)]

    order = ["g_mix", "w_in", "b_gate", "w_pool_lin", "pool_scale", "w_pool_out", "w_attn_out", "w_out", "g_ffn",
             "w_up", "conv_w", "conv_b", "w_down", "g_final"]
    outs = [loss, grad_x[None]]
    for kind in range(4):
        for n in order:
            outs.append(res[n][kind] if n in res else res_small[kind][n])
    return tuple(outs)
```

```python
import math

import jax
import jax.numpy as jnp
from jax import lax
from jax.experimental import pallas as pl
from jax.experimental.pallas import tpu as pltpu

F32 = jnp.float32
BF16 = jnp.bfloat16
MESH = pl.DeviceIdType.MESH
AXES = ("x", "y", "c")
N_DEV = 8

D_MODEL = 2048
POOL_WINDOWS = (2, 4, 8, 16)
POOL_GROUP_WIDTH = 256
POOL_WIDTH = 1024
ATTN_GROUPS = ((128, 1), (512, 4), (2048, 16))
SPAN = 128
HEADS_PER_GROUP = 4
N_ATTN_HEADS = 12
HEAD_DIM = 128
ATTN_WIDTH = 1536
QKV_WIDTH = 3 * ATTN_WIDTH
ATTN_OUT_WIDTH = 512
IN_WIDTH = 9728
D_FF = 5632
RMS_EPS = 1e-6
ADAM_LR, ADAM_B1, ADAM_B2, ADAM_EPS, ADAM_WD, ADAM_STEP = 0.001, 0.9, 0.999, 1e-08, 0.01, 10

VMEM_LIMIT_BYTES = 48 * 1024 * 1024
HALO = 16
TOKEN_TK = 2048
MXU_DIM = 256
PACK_COLS = 1024
SMALL_ROWS = 24
NEG_BIG = -1e30

_DN = {"nn": (((1,), (0,)), ((), ())), "nt": (((1,), (1,)), ((), ())), "tn": (((0,), (0,)), ((), ()))}


def _params(*sem):
    return pltpu.CompilerParams(dimension_semantics=sem, vmem_limit_bytes=VMEM_LIMIT_BYTES)


def _dot(a, b, mode):
    return lax.dot_general(a.astype(BF16), b.astype(BF16), _DN[mode], preferred_element_type=F32)


def _mm(name, a, b, *, mode, grid, a_spec, b_spec, o_spec, o_shape, o_dtype, acc_shape, res=None, res_spec=None,
        after=None):
    nk = grid[2]
    n_in = 2 + (res is not None) + (after is not None)

    def body(*refs):
        a_ref, b_ref = refs[:2]
        r_ref = refs[2] if res is not None else None
        o_ref = refs[n_in]
        scr = refs[n_in + 1:]
        prod = _dot(a_ref[...], b_ref[...], mode)

        def finish(val):
            if r_ref is not None:
                val = val + r_ref[...]
            o_ref[...] = val.astype(o_ref.dtype)

        if nk == 1:
            finish(prod)
        else:
            acc = scr[0]
            k = pl.program_id(2)

            @pl.when(k == 0)
            def _():
                acc[...] = prod

            @pl.when(k > 0)
            def _():
                acc[...] += prod

            @pl.when(k == nk - 1)
            def _():
                finish(acc[...])

    in_specs = [a_spec, b_spec]
    args = [a, b]
    if res is not None:
        in_specs.append(res_spec)
        args.append(res)
    if after is not None:
        in_specs.append(pl.BlockSpec((8, 128), lambda i, j, k: (0, 0)))
        args.append(after)
    return pl.pallas_call(
        body, grid=grid, in_specs=in_specs, out_specs=o_spec,
        out_shape=jax.ShapeDtypeStruct(o_shape, o_dtype),
        scratch_shapes=[] if nk == 1 else [pltpu.VMEM(acc_shape, F32)],
        compiler_params=_params("parallel", "parallel", "arbitrary"), name=name)(*args)


def _mm_nn(name, a, b, *, tm, tn, tk, o_dtype, n_cols=None, b_col_off=0, res=None, after=None):
    M, K = a.shape
    N = b.shape[1] if n_cols is None else n_cols
    off = b_col_off // tn
    return _mm(name, a, b, mode="nn", grid=(M // tm, N // tn, K // tk),
               a_spec=pl.BlockSpec((tm, tk), lambda i, j, k: (i, k)),
               b_spec=pl.BlockSpec((tk, tn), lambda i, j, k: (k, j + off)),
               o_spec=pl.BlockSpec((tm, tn), lambda i, j, k: (i, j)),
               o_shape=(M, N), o_dtype=o_dtype, acc_shape=(tm, tn), res=res,
               res_spec=pl.BlockSpec((tm, tn), lambda i, j, k: (i, j)), after=after)


def _mm_nt(name, a, b, *, tm, tn, tk, o_dtype, n_rows=None, b_row_off=0, after=None):
    M, K = a.shape
    N = b.shape[0] if n_rows is None else n_rows
    off = b_row_off // tn
    return _mm(name, a, b, mode="nt", grid=(M // tm, N // tn, K // tk),
               a_spec=pl.BlockSpec((tm, tk), lambda i, j, k: (i, k)),
               b_spec=pl.BlockSpec((tn, tk), lambda i, j, k: (j + off, k)),
               o_spec=pl.BlockSpec((tm, tn), lambda i, j, k: (i, j)),
               o_shape=(M, N), o_dtype=o_dtype, acc_shape=(tm, tn), after=after)


def _mm_tn(name, a, b, *, tm, tn, tk):
    K, M = a.shape
    N = b.shape[1]
    return _mm(name, a, b, mode="tn", grid=(M // tm, N // tn, K // tk),
               a_spec=pl.BlockSpec((tk, tm), lambda i, j, k: (k, i)),
               b_spec=pl.BlockSpec((tk, tn), lambda i, j, k: (k, j)),
               o_spec=pl.BlockSpec((tm, tn), lambda i, j, k: (i, j)),
               o_shape=(M, N), o_dtype=BF16, acc_shape=(tm, tn))


def _rms_fwd(name, x, g):
    S, D = x.shape
    T = 512

    def body(x_ref, g_ref, h_ref):
        xv = x_ref[...]
        r = lax.rsqrt(jnp.mean(xv * xv, axis=-1, keepdims=True) + RMS_EPS)
        h_ref[...] = (xv * r * g_ref[...]).astype(BF16)

    return pl.pallas_call(
        body, grid=(S // T,),
        in_specs=[pl.BlockSpec((T, D), lambda i: (i, 0)), pl.BlockSpec((1, D), lambda i: (0, 0))],
        out_specs=pl.BlockSpec((T, D), lambda i: (i, 0)),
        out_shape=jax.ShapeDtypeStruct((S, D), BF16), compiler_params=_params("parallel"), name=name)(x, g)


def _rms_bwd(name, x, g, dh, dres):
    S, D = x.shape
    T = 256

    def body(x_ref, g_ref, dh_ref, dres_ref, dx_ref, dxb_ref, dg_ref):
        i = pl.program_id(0)
        xv = x_ref[...]
        r = lax.rsqrt(jnp.mean(xv * xv, axis=-1, keepdims=True) + RMS_EPS)
        xhat = xv * r
        dhv = dh_ref[...].astype(F32)
        gdh = dhv * g_ref[...]
        c = jnp.mean(xhat * gdh, axis=-1, keepdims=True)
        dx = dres_ref[...] + r * (gdh - xhat * c)
        dx_ref[...] = dx
        dxb_ref[...] = dx.astype(BF16)
        part = jnp.sum(dhv * xhat, axis=0, keepdims=True)

        @pl.when(i == 0)
        def _():
            dg_ref[...] = part

        @pl.when(i > 0)
        def _():
            dg_ref[...] += part

    row = pl.BlockSpec((T, D), lambda i: (i, 0))
    vec = pl.BlockSpec((1, D), lambda i: (0, 0))
    return pl.pallas_call(
        body, grid=(S // T,), in_specs=[row, vec, row, row], out_specs=[row, row, vec],
        out_shape=[jax.ShapeDtypeStruct((S, D), F32), jax.ShapeDtypeStruct((S, D), BF16),
                   jax.ShapeDtypeStruct((1, D), F32)],
        compiler_params=_params("arbitrary"), name=name)(x, g, dh, dres)


def _final_loss(name, x, g, target):
    S, D = x.shape
    T = 256

    def body(x_ref, g_ref, t_ref, dx_ref, dxb_ref, dg_ref, sq_ref):
        i = pl.program_id(0)
        xv = x_ref[...]
        r = lax.rsqrt(jnp.mean(xv * xv, axis=-1, keepdims=True) + RMS_EPS)
        xhat = xv * r
        err = xhat * g_ref[...] - t_ref[...]
        dy = err * (1.0 / D)
        gdy = dy * g_ref[...]
        c = jnp.mean(xhat * gdy, axis=-1, keepdims=True)
        dx = r * (gdy - xhat * c)
        dx_ref[...] = dx
        dxb_ref[...] = dx.astype(BF16)
        dg_part = jnp.sum(dy * xhat, axis=0, keepdims=True)
        sq_part = jnp.sum(err * err, axis=0, keepdims=True)

        @pl.when(i == 0)
        def _():
            dg_ref[...] = dg_part
            sq_ref[...] = sq_part

        @pl.when(i > 0)
        def _():
            dg_ref[...] += dg_part
            sq_ref[...] += sq_part

    row = pl.BlockSpec((T, D), lambda i: (i, 0))
    vec = pl.BlockSpec((1, D), lambda i: (0, 0))
    return pl.pallas_call(
        body, grid=(S // T,), in_specs=[row, vec, row], out_specs=[row, row, vec, vec],
        out_shape=[jax.ShapeDtypeStruct((S, D), F32), jax.ShapeDtypeStruct((S, D), BF16),
                   jax.ShapeDtypeStruct((1, D), F32), jax.ShapeDtypeStruct((1, D), F32)],
        compiler_params=_params("arbitrary"), name=name)(x, g, target)


def _prev_halo_spec(T, C, col_map):
    per = T // HALO
    return pl.BlockSpec((HALO, C), lambda *g: (jnp.maximum(g[0] * per - 1, 0), col_map(*g)))


def _pool_window_sums(ext, T, i):
    t = i * T + lax.broadcasted_iota(jnp.int32, (T, 1), 0)
    outs = []
    cnts = []
    for gi, w in enumerate(POOL_WINDOWS):
        s = ext[:, gi * POOL_GROUP_WIDTH:(gi + 1) * POOL_GROUP_WIDTH]
        sh = 1
        while sh < w:
            s = s + pltpu.roll(s, sh, 0)
            sh *= 2
        cnt = jnp.minimum(t + 1, w).astype(F32)
        outs.append(s[HALO:] / cnt - ext[HALO:, gi * POOL_GROUP_WIDTH:(gi + 1) * POOL_GROUP_WIDTH])
        cnts.append(cnt)
    return outs, cnts


def _load_ext(ext_ref, halo_ref, blk_ref, first):
    ext_ref[0:HALO, :] = jnp.where(first, 0.0, halo_ref[...].astype(F32))
    ext_ref[HALO:, :] = blk_ref[...].astype(F32)


def _pool_fwd(u, w_lin, scale):
    S = u.shape[0]
    T = 512
    W = POOL_WIDTH

    def body(u_ref, halo_ref, wl_ref, sc_ref, y_ref, ext_ref):
        i = pl.program_id(0)
        _load_ext(ext_ref, halo_ref, u_ref, i == 0)
        pooled, _ = _pool_window_sums(ext_ref[...], T, i)
        for gi in range(4):
            z = _dot(pooled[gi], wl_ref[gi], "nn")
            cs = slice(gi * POOL_GROUP_WIDTH, (gi + 1) * POOL_GROUP_WIDTH)
            y_ref[:, cs] = (z * sc_ref[:, cs]).astype(BF16)

    return pl.pallas_call(
        body, grid=(S // T,),
        in_specs=[pl.BlockSpec((T, W), lambda i: (i, 0)), _prev_halo_spec(T, W, lambda i: 0),
                  pl.BlockSpec((4, 256, 256), lambda i: (0, 0, 0)), pl.BlockSpec((1, W), lambda i: (0, 0))],
        out_specs=pl.BlockSpec((T, W), lambda i: (i, 0)),
        out_shape=jax.ShapeDtypeStruct((S, W), BF16),
        scratch_shapes=[pltpu.VMEM((HALO + T, W), F32)],
        compiler_params=_params("parallel"), name="pool_fwd")(u, u, w_lin, scale)


def _pool_bwd_a(u, dy, w_lin, scale):
    S = u.shape[0]
    T = 512
    W = POOL_WIDTH

    def body(u_ref, halo_ref, dy_ref, wl_ref, sc_ref, e_ref, dwl_ref, dsc_ref, ext_ref):
        i = pl.program_id(0)
        _load_ext(ext_ref, halo_ref, u_ref, i == 0)
        pooled, cnts = _pool_window_sums(ext_ref[...], T, i)
        for gi in range(4):
            cs = slice(gi * POOL_GROUP_WIDTH, (gi + 1) * POOL_GROUP_WIDTH)
            z = _dot(pooled[gi], wl_ref[gi], "nn")
            dyg = dy_ref[:, cs]
            dsc = jnp.sum(dyg * z, axis=0, keepdims=True)
            dz = dyg * sc_ref[:, cs]
            dwl = _dot(pooled[gi], dz, "tn")
            dpooled = _dot(dz, wl_ref[gi], "nt")
            e_ref[:, cs] = dpooled / cnts[gi]

            @pl.when(i == 0)
            def _():
                dwl_ref[gi] = dwl
                dsc_ref[:, cs] = dsc

            @pl.when(i > 0)
            def _():
                dwl_ref[gi] += dwl
                dsc_ref[:, cs] += dsc

    return pl.pallas_call(
        body, grid=(S // T,),
        in_specs=[pl.BlockSpec((T, W), lambda i: (i, 0)), _prev_halo_spec(T, W, lambda i: 0),
                  pl.BlockSpec((T, W), lambda i: (i, 0)),
                  pl.BlockSpec((4, 256, 256), lambda i: (0, 0, 0)), pl.BlockSpec((1, W), lambda i: (0, 0))],
        out_specs=[pl.BlockSpec((T, W), lambda i: (i, 0)), pl.BlockSpec((4, 256, 256), lambda i: (0, 0, 0)),
                   pl.BlockSpec((1, W), lambda i: (0, 0))],
        out_shape=[jax.ShapeDtypeStruct((S, W), F32), jax.ShapeDtypeStruct((4, 256, 256), F32),
                   jax.ShapeDtypeStruct((1, W), F32)],
        scratch_shapes=[pltpu.VMEM((HALO + T, W), F32)],
        compiler_params=_params("arbitrary"), name="pool_bwd_a")(u, u, dy, w_lin, scale)


def _pool_bwd_b(e):
    S = e.shape[0]
    T = 512
    W = POOL_WIDTH
    per = T // HALO
    n_blk = S // T

    def body(e_ref, halo_ref, du_ref, ext_ref):
        i = pl.program_id(0)
        ext_ref[0:T, :] = e_ref[...]
        ext_ref[T:, :] = jnp.where(i == n_blk - 1, 0.0, halo_ref[...])
        ext = ext_ref[...]
        n = T + HALO
        t = i * T + lax.broadcasted_iota(jnp.int32, (T, 1), 0)
        for gi, w in enumerate(POOL_WINDOWS):
            cs = slice(gi * POOL_GROUP_WIDTH, (gi + 1) * POOL_GROUP_WIDTH)
            s = ext[:, cs]
            sh = 1
            while sh < w:
                s = s + pltpu.roll(s, n - sh, 0)
                sh *= 2
            cnt = jnp.minimum(t + 1, w).astype(F32)
            du_ref[:, cs] = (s[0:T] - ext[0:T, cs] * cnt).astype(BF16)

    return pl.pallas_call(
        body, grid=(n_blk,),
        in_specs=[pl.BlockSpec((T, W), lambda i: (i, 0)),
                  pl.BlockSpec((HALO, W), lambda i: (jnp.minimum((i + 1) * per, S // HALO - 1), 0))],
        out_specs=pl.BlockSpec((T, W), lambda i: (i, 0)),
        out_shape=jax.ShapeDtypeStruct((S, W), BF16),
        scratch_shapes=[pltpu.VMEM((T + HALO, W), F32)],
        compiler_params=_params("parallel"), name="pool_bwd_b")(e, e)


def _slope(head):
    return 2.0 ** (-8.0 * (head + 1) / N_ATTN_HEADS)


def _attn_masks(b):
    row = lax.broadcasted_iota(jnp.int32, (SPAN, SPAN), 0)
    col = lax.broadcasted_iota(jnp.int32, (SPAN, SPAN), 1)
    dist_cur = (row - col).astype(F32)
    valid_cur = col <= row
    valid_prev = jnp.logical_and(col >= row, b > 0)
    return dist_cur, valid_cur, valid_prev


ATTN_BATCH = 4


def _attn_tiling(gi, S):
    d = ATTN_GROUPS[gi][1]
    hp = HEADS_PER_GROUP if d < 16 else 1
    return d, SPAN * d, S // (SPAN * d), hp, hp * HEAD_DIM, ATTN_BATCH // hp


_BATCHED_DN = {"nt": (((2,), (2,)), ((0,), (0,))), "nn": (((2,), (1,)), ((0,), (0,))), "tn": (((1,), (1,)), ((0,), (0,)))}


def _bdot(a, b, mode):
    return lax.dot_general(a.astype(BF16), b.astype(BF16), _BATCHED_DN[mode], preferred_element_type=F32)


def _batch_rows(it, d, rb):
    return [pl.ds(it * rb + j, SPAN, stride=d) for j in range(rb)]


def _take(scr, hp, rows):
    return jnp.stack([scr[hi, rw, :] for hi in range(hp) for rw in rows])


def _put(scr, hp, rows, val):
    for n, (hi, rw) in enumerate((hi, rw) for hi in range(hp) for rw in rows):
        scr[hi, rw, :] = val[n]


def _batch_bias(gi, d, hp, rb, dist):
    return jnp.stack([(-(_head_slope(gi, hp, hi) * d)) * dist for hi in range(hp) for _ in range(rb)])


def _head_slope(gi, hp, hi):
    if hp == HEADS_PER_GROUP:
        return _slope(gi * HEADS_PER_GROUP + hi)
    first = pl.program_id(1) * hp
    sl = jnp.float32(_slope(gi * HEADS_PER_GROUP + hi))
    for h0 in range(hp, HEADS_PER_GROUP, hp):
        sl = jnp.where(first == h0, jnp.float32(_slope(gi * HEADS_PER_GROUP + h0 + hi)), sl)
    return sl


def _split_heads(src_ref, dst_ref, hp):
    for hi in range(hp):
        dst_ref[hi] = src_ref[:, hi * HEAD_DIM:(hi + 1) * HEAD_DIM].astype(F32)


def _attn_fwd(gi, qkv):
    S = qkv.shape[0]
    d, C, nb, hp, W, rb = _attn_tiling(gi, S)
    scale = HEAD_DIM ** -0.5

    def body(q_ref, kc_ref, kp_ref, vc_ref, vp_ref, o_ref, lse_ref, qs, kcs, kps, vcs, vps, os, ls):
        b = pl.program_id(0)
        for src, dst in ((q_ref, qs), (kc_ref, kcs), (kp_ref, kps), (vc_ref, vcs), (vp_ref, vps)):
            _split_heads(src, dst, hp)
        dist_cur, valid_cur, valid_prev = _attn_masks(b)
        bias_cur = _batch_bias(gi, d, hp, rb, dist_cur)
        bias_far = _batch_bias(gi, d, hp, rb, dist_cur + SPAN)

        def sub_sequences(it, carry):
            rows = _batch_rows(it, d, rb)
            q = _take(qs, hp, rows)
            s_cur = jnp.where(valid_cur, _bdot(q, _take(kcs, hp, rows), "nt") * scale + bias_cur, NEG_BIG)
            s_prev = jnp.where(valid_prev, _bdot(q, _take(kps, hp, rows), "nt") * scale + bias_far, NEG_BIG)
            m = jnp.maximum(jnp.max(s_cur, axis=-1, keepdims=True), jnp.max(s_prev, axis=-1, keepdims=True))
            p_cur = jnp.exp(s_cur - m)
            p_prev = jnp.exp(s_prev - m)
            l = jnp.sum(p_cur, axis=-1, keepdims=True) + jnp.sum(p_prev, axis=-1, keepdims=True)
            o = (_bdot(p_cur, _take(vcs, hp, rows), "nn") + _bdot(p_prev, _take(vps, hp, rows), "nn")) / l
            _put(os, hp, rows, o)
            _put(ls, hp, rows, jnp.broadcast_to(m + jnp.log(l), o.shape))
            return carry

        lax.fori_loop(0, d // rb, sub_sequences, 0)
        for hi in range(hp):
            hs = slice(hi * HEAD_DIM, (hi + 1) * HEAD_DIM)
            o_ref[:, hs] = os[hi]
            lse_ref[:, hs] = ls[hi]

    per_sec = ATTN_WIDTH // W
    per_grp = ATTN_OUT_WIDTH // W

    def spec(sec, shift):
        return pl.BlockSpec((C, W), lambda b, h: (jnp.maximum(b + shift, 0), sec * per_sec + gi * per_grp + h))

    out = pl.BlockSpec((C, W), lambda b, h: (b, h))
    scr = pltpu.VMEM((hp, C, HEAD_DIM), F32)
    return pl.pallas_call(
        body, grid=(nb, HEADS_PER_GROUP // hp),
        in_specs=[spec(0, 0), spec(1, 0), spec(1, -1), spec(2, 0), spec(2, -1)], out_specs=[out, out],
        out_shape=[jax.ShapeDtypeStruct((S, ATTN_OUT_WIDTH), F32)] * 2, scratch_shapes=[scr] * 7,
        compiler_params=_params("parallel", "parallel"), name=f"attn_fwd_g{gi}")(qkv, qkv, qkv, qkv, qkv)


def _attn_bwd(gi, qkv, do, lse, dp):
    S = qkv.shape[0]
    d, C, nb, hp, W, rb = _attn_tiling(gi, S)
    scale = HEAD_DIM ** -0.5
    n_in = 12

    def body(*refs):
        ins, (dq_ref, dk_ref, dv_ref) = refs[:n_in], refs[n_in:n_in + 3]
        scr = refs[n_in + 3:]
        for src, dst in zip(ins, scr[:n_in]):
            _split_heads(src, dst, hp)
        qs, kcs, vcs, kps, vps, qns, dos, dons, lses, lsens, dps, dpns = scr[:n_in]
        dqs, dks, dvs = scr[n_in:]
        b = pl.program_id(0)
        dist_cur, valid_cur, valid_prev = _attn_masks(b)
        row = lax.broadcasted_iota(jnp.int32, (SPAN, SPAN), 0)
        col = lax.broadcasted_iota(jnp.int32, (SPAN, SPAN), 1)
        valid_next = jnp.logical_and(col >= row, b < nb - 1)

        bias_cur = _batch_bias(gi, d, hp, rb, dist_cur)
        bias_far = _batch_bias(gi, d, hp, rb, dist_cur + SPAN)

        def sub_sequences(it, carry):
            rows = _batch_rows(it, d, rb)
            q, kc, vc, kp, vp, qn, dov, don = (_take(s, hp, rows) for s in (qs, kcs, vcs, kps, vps, qns, dos, dons))
            lse_b, lse_n, dp_b, dp_n = (_take(s, hp, rows) for s in (lses, lsens, dps, dpns))
            p_cur = jnp.where(valid_cur, jnp.exp(_bdot(q, kc, "nt") * scale + bias_cur - lse_b), 0.0)
            p_prev = jnp.where(valid_prev, jnp.exp(_bdot(q, kp, "nt") * scale + bias_far - lse_b), 0.0)
            ds_cur = p_cur * (_bdot(dov, vc, "nt") - dp_b)
            ds_prev = p_prev * (_bdot(dov, vp, "nt") - dp_b)
            _put(dqs, hp, rows, (_bdot(ds_cur, kc, "nn") + _bdot(ds_prev, kp, "nn")) * scale)
            p_next = jnp.where(valid_next, jnp.exp(_bdot(qn, kc, "nt") * scale + bias_far - lse_n), 0.0)
            ds_next = p_next * (_bdot(don, vc, "nt") - dp_n)
            _put(dvs, hp, rows, _bdot(p_cur, dov, "tn") + _bdot(p_next, don, "tn"))
            _put(dks, hp, rows, (_bdot(ds_cur, q, "tn") + _bdot(ds_next, qn, "tn")) * scale)
            return carry

        lax.fori_loop(0, d // rb, sub_sequences, 0)
        for hi in range(hp):
            hs = slice(hi * HEAD_DIM, (hi + 1) * HEAD_DIM)
            dq_ref[:, hs] = dqs[hi].astype(BF16)
            dk_ref[:, hs] = dks[hi].astype(BF16)
            dv_ref[:, hs] = dvs[hi].astype(BF16)

    per_sec = ATTN_WIDTH // W
    per_grp = ATTN_OUT_WIDTH // W

    def qkv_spec(sec, shift):
        return pl.BlockSpec((C, W), lambda b, h: (jnp.clip(b + shift, 0, nb - 1), sec * per_sec + gi * per_grp + h))

    def act_spec(shift):
        return pl.BlockSpec((C, W), lambda b, h: (jnp.clip(b + shift, 0, nb - 1), h))

    out = act_spec(0)
    scr = pltpu.VMEM((hp, C, HEAD_DIM), F32)
    return pl.pallas_call(
        body, grid=(nb, HEADS_PER_GROUP // hp),
        in_specs=[qkv_spec(0, 0), qkv_spec(1, 0), qkv_spec(2, 0), qkv_spec(1, -1), qkv_spec(2, -1), qkv_spec(0, 1),
                  act_spec(0), act_spec(1), act_spec(0), act_spec(1), act_spec(0), act_spec(1)],
        out_specs=[out, out, out], out_shape=[jax.ShapeDtypeStruct((S, ATTN_OUT_WIDTH), BF16)] * 3,
        scratch_shapes=[scr] * (n_in + 3),
        compiler_params=_params("parallel", "parallel"), name=f"attn_bwd_g{gi}")(
            qkv, qkv, qkv, qkv, qkv, qkv, do, do, lse, lse, dp, dp)


def _group_weights(l_refs):
    l0, l1, l2 = (r[...] for r in l_refs)
    m = jnp.maximum(jnp.maximum(l0, l1), l2)
    e = [jnp.exp(l0 - m), jnp.exp(l1 - m), jnp.exp(l2 - m)]
    tot = e[0] + e[1] + e[2]
    return [ei / tot for ei in e]


def _merge_fwd(outs, lses):
    S, W = outs[0].shape
    T = 512

    def body(o0, o1, o2, l0, l1, l2, y_ref):
        w = _group_weights((l0, l1, l2))
        y_ref[...] = (w[0] * o0[...] + w[1] * o1[...] + w[2] * o2[...]).astype(BF16)

    blk = pl.BlockSpec((T, W), lambda i: (i, 0))
    return pl.pallas_call(
        body, grid=(S // T,), in_specs=[blk] * 6, out_specs=blk, out_shape=jax.ShapeDtypeStruct((S, W), BF16),
        compiler_params=_params("parallel"), name="merge_fwd")(*outs, *lses)


def _merge_bwd(dy, outs, lses):
    S, W = outs[0].shape
    T = 512

    def body(dy_ref, o0, o1, o2, l0, l1, l2, do0, do1, do2, dp0, dp1, dp2):
        w = _group_weights((l0, l1, l2))
        dyv = dy_ref[...]
        y = w[0] * o0[...] + w[1] * o1[...] + w[2] * o2[...]
        prod = dyv * y
        dots = []
        for hi in range(HEADS_PER_GROUP):
            hs = slice(hi * HEAD_DIM, (hi + 1) * HEAD_DIM)
            dots.append(jnp.broadcast_to(jnp.sum(prod[:, hs], axis=-1, keepdims=True), (T, HEAD_DIM)))
        dot_b = jnp.concatenate(dots, axis=1)
        for wg, do_ref, dp_ref in zip(w, (do0, do1, do2), (dp0, dp1, dp2)):
            do_ref[...] = (wg * dyv).astype(BF16)
            dp_ref[...] = wg * dot_b

    blk = pl.BlockSpec((T, W), lambda i: (i, 0))
    res = pl.pallas_call(
        body, grid=(S // T,), in_specs=[blk] * 7, out_specs=[blk] * 6,
        out_shape=[jax.ShapeDtypeStruct((S, W), BF16)] * 3 + [jax.ShapeDtypeStruct((S, W), F32)] * 3,
        compiler_params=_params("parallel"), name="merge_bwd")(dy, *outs, *lses)
    return res[:3], res[3:]


def _mix_fwd(pool_y, attn_y, w_pool_out_t, w_attn_out_t, gates_pre, b_gate):
    S = pool_y.shape[0]
    D = D_MODEL
    tm, tn = 1024, 512
    nj = D // tn

    def body(p_ref, a_ref, wp_ref, wa_ref, g0_ref, g1_ref, b0_ref, b1_ref, mixed_ref, yp_ref, ya_ref):
        yp = _dot(p_ref[...], wp_ref[...], "nt")
        ya = _dot(a_ref[...], wa_ref[...], "nt")
        g0 = jax.nn.sigmoid(g0_ref[...] + b0_ref[...])
        g1 = jax.nn.sigmoid(g1_ref[...] + b1_ref[...])
        mixed_ref[...] = (g0 * yp + g1 * ya).astype(BF16)
        yp_ref[...] = yp.astype(BF16)
        ya_ref[...] = ya.astype(BF16)

    out = pl.BlockSpec((tm, tn), lambda i, j: (i, j))
    return pl.pallas_call(
        body, grid=(S // tm, nj),
        in_specs=[pl.BlockSpec((tm, POOL_WIDTH), lambda i, j: (i, 0)),
                  pl.BlockSpec((tm, ATTN_OUT_WIDTH), lambda i, j: (i, 0)),
                  pl.BlockSpec((tn, POOL_WIDTH), lambda i, j: (j, 0)),
                  pl.BlockSpec((tn, ATTN_OUT_WIDTH), lambda i, j: (j, 0)),
                  pl.BlockSpec((tm, tn), lambda i, j: (i, j)), pl.BlockSpec((tm, tn), lambda i, j: (i, j + nj)),
                  pl.BlockSpec((1, tn), lambda i, j: (0, j)), pl.BlockSpec((1, tn), lambda i, j: (0, j + nj))],
        out_specs=[out, out, out], out_shape=[jax.ShapeDtypeStruct((S, D), BF16)] * 3,
        compiler_params=_params("parallel", "parallel"), name="mix_fwd")(
            pool_y, attn_y, w_pool_out_t, w_attn_out_t, gates_pre, gates_pre, b_gate, b_gate)


def _mix_bwd(dmixed, gates_pre, b_gate, y_pool, y_attn):
    S = dmixed.shape[0]
    D = D_MODEL
    T = 256

    def body(dm_ref, gp_ref, b_ref, yp_ref, ya_ref, dyp_ref, dya_ref, dg_ref, db_ref):
        i = pl.program_id(0)
        dm = dm_ref[...].astype(F32)
        parts = []
        for br, (y_ref, dy_ref) in enumerate(((yp_ref, dyp_ref), (ya_ref, dya_ref))):
            cs = slice(br * D, (br + 1) * D)
            g = jax.nn.sigmoid(gp_ref[:, cs] + b_ref[:, cs])
            dy_ref[...] = (dm * g).astype(BF16)
            dpre = dm * y_ref[...].astype(F32) * g * (1.0 - g)
            dg_ref[:, cs] = dpre.astype(BF16)
            parts.append(jnp.sum(dpre, axis=0, keepdims=True))

        @pl.when(i == 0)
        def _():
            db_ref[:, 0:D] = parts[0]
            db_ref[:, D:2 * D] = parts[1]

        @pl.when(i > 0)
        def _():
            db_ref[:, 0:D] += parts[0]
            db_ref[:, D:2 * D] += parts[1]

    row = pl.BlockSpec((T, D), lambda i: (i, 0))
    row2 = pl.BlockSpec((T, 2 * D), lambda i: (i, 0))
    vec2 = pl.BlockSpec((1, 2 * D), lambda i: (0, 0))
    return pl.pallas_call(
        body, grid=(S // T,), in_specs=[row, row2, vec2, row, row], out_specs=[row, row, row2, vec2],
        out_shape=[jax.ShapeDtypeStruct((S, D), BF16), jax.ShapeDtypeStruct((S, D), BF16),
                   jax.ShapeDtypeStruct((S, 2 * D), BF16), jax.ShapeDtypeStruct((1, 2 * D), F32)],
        compiler_params=_params("arbitrary"), name="mix_bwd")(dmixed, gates_pre, b_gate, y_pool, y_attn)


_CONV_T, _CONV_C = 512, 512


def _conv_taps(ext_ref, h, w_ref, b_ref, cols=slice(None)):
    ext = ext_ref[h, :, cols]
    x2 = pltpu.roll(ext, 2, 0)[HALO:]
    x1 = pltpu.roll(ext, 1, 0)[HALO:]
    x0 = ext[HALO:]
    y = b_ref[:, cols] + w_ref[0:1, cols] * x2
    y = y + w_ref[1:2, cols] * x1
    y = y + w_ref[2:3, cols] * x0
    return y, (x2, x1, x0)


def _gelu_parts(a):
    cdf = 0.5 * (1.0 + lax.erf(a * (1.0 / math.sqrt(2.0))))
    return cdf, a * cdf


def _conv_specs(S):
    T, C = _CONV_T, _CONV_C
    nj = D_FF // C
    blk = pl.BlockSpec((2, T, C), lambda i, j: (0, i, j))
    per = T // HALO
    halo = pl.BlockSpec((2, HALO, C), lambda i, j: (0, jnp.maximum(i * per - 1, 0), j))
    w_a = pl.BlockSpec((3, C), lambda i, j: (0, j))
    w_b = pl.BlockSpec((3, C), lambda i, j: (0, j + nj))
    b_a = pl.BlockSpec((1, C), lambda i, j: (0, j))
    b_b = pl.BlockSpec((1, C), lambda i, j: (0, j + nj))
    return T, C, nj, blk, halo, w_a, w_b, b_a, b_b


def _conv_load(ext_ref, halo_ref, up_ref, first):
    for h in range(2):
        ext_ref[h, 0:HALO, :] = jnp.where(first, 0.0, halo_ref[h].astype(F32))
        ext_ref[h, HALO:, :] = up_ref[h].astype(F32)


def _up_proj_convffn(h, w_up_t, conv_w, conv_b):
    S = h.shape[0]
    tm, C = 1024, _CONV_C
    nj = D_FF // C

    def body(h_ref, wa_ref, wb_ref, cwa_ref, cwb_ref, cba_ref, cbb_ref, up_ref, act_ref, ext_ref, carry_ref):
        i, j = pl.program_id(0), pl.program_id(1)
        for half, w_ref in enumerate((wa_ref, wb_ref)):
            pre = _dot(h_ref[...], w_ref[...], "nt").astype(BF16)
            up_ref[half] = pre
            ext_ref[half, 0:HALO, :] = jnp.where(i == 0, 0.0, carry_ref[j, half])
            ext_ref[half, HALO:, :] = pre.astype(F32)
            carry_ref[j, half] = ext_ref[half, tm:tm + HALO, :]
        a, _ = _conv_taps(ext_ref, 0, cwa_ref, cba_ref)
        b, _ = _conv_taps(ext_ref, 1, cwb_ref, cbb_ref)
        _, gelu = _gelu_parts(a)
        act_ref[...] = (gelu * b).astype(BF16)

    def w_spec(half):
        return pl.BlockSpec((C, D_MODEL), lambda i, j: (j + half * nj, 0))

    def c_spec(rows, half):
        return pl.BlockSpec((rows, C), lambda i, j: (0, j + half * nj))

    return pl.pallas_call(
        body, grid=(S // tm, nj),
        in_specs=[pl.BlockSpec((tm, D_MODEL), lambda i, j: (i, 0)), w_spec(0), w_spec(1),
                  c_spec(3, 0), c_spec(3, 1), c_spec(1, 0), c_spec(1, 1)],
        out_specs=[pl.BlockSpec((2, tm, C), lambda i, j: (0, i, j)), pl.BlockSpec((tm, C), lambda i, j: (i, j))],
        out_shape=[jax.ShapeDtypeStruct((2, S, D_FF), BF16), jax.ShapeDtypeStruct((S, D_FF), BF16)],
        scratch_shapes=[pltpu.VMEM((2, HALO + tm, C), F32), pltpu.VMEM((nj, 2, HALO, C), F32)],
        compiler_params=_params("arbitrary", "arbitrary"), name="up_proj_convffn")(
            h, w_up_t, w_up_t, conv_w, conv_w, conv_b, conv_b)


def _convffn_bwd_a(up_pre, dx, w_down, conv_w, conv_b):
    S = up_pre.shape[1]
    T, C, nj, blk, halo, w_a, w_b, b_a, b_b = _conv_specs(S)
    def swap(spec):
        return pl.BlockSpec(spec.block_shape, lambda j, i, _m=spec.index_map: _m(i, j))

    def body(up_ref, halo_ref, dx_ref, wd_ref, wa_ref, wb_ref, ba_ref, bb_ref, dup_ref, dwa_ref, dwb_ref, dba_ref,
             dbb_ref, ext_ref):
        i = pl.program_id(1)
        _conv_load(ext_ref, halo_ref, up_ref, i == 0)

        @pl.when(i == 0)
        def _():
            for acc_ref in (dwa_ref, dwb_ref, dba_ref, dbb_ref):
                acc_ref[...] = jnp.zeros_like(acc_ref)

        for c in range(C // MXU_DIM):
            cs = slice(c * MXU_DIM, (c + 1) * MXU_DIM)
            dact = _dot(dx_ref[...], wd_ref[cs, :], "nt")
            a, xa = _conv_taps(ext_ref, 0, wa_ref, ba_ref, cs)
            b, xb = _conv_taps(ext_ref, 1, wb_ref, bb_ref, cs)
            cdf, gelu = _gelu_parts(a)
            dgelu = cdf + a * (jnp.exp(-0.5 * a * a) * (1.0 / math.sqrt(2.0 * math.pi)))
            da = dact * b * dgelu
            db = dact * gelu
            dup_ref[0, :, cs] = da.astype(BF16)
            dup_ref[1, :, cs] = db.astype(BF16)
            for dval, xs, dw_ref, dbias_ref in ((da, xa, dwa_ref, dba_ref), (db, xb, dwb_ref, dbb_ref)):
                dw_ref[:, cs] += jnp.concatenate([jnp.sum(dval * xk, axis=0, keepdims=True) for xk in xs], axis=0)
                dbias_ref[:, cs] += jnp.sum(dval, axis=0, keepdims=True)

    dx_spec = pl.BlockSpec((T, D_MODEL), lambda j, i: (i, 0))
    wd_spec = pl.BlockSpec((C, D_MODEL), lambda j, i: (j, 0))
    dw_spec = pl.BlockSpec((3, C), lambda j, i: (0, j))
    db_spec = pl.BlockSpec((1, C), lambda j, i: (0, j))
    res = pl.pallas_call(
        body, grid=(nj, S // T),
        in_specs=[swap(blk), swap(halo), dx_spec, wd_spec, swap(w_a), swap(w_b), swap(b_a), swap(b_b)],
        out_specs=[swap(blk), dw_spec, dw_spec, db_spec, db_spec],
        out_shape=[jax.ShapeDtypeStruct((2, S, D_FF), BF16), jax.ShapeDtypeStruct((3, D_FF), F32),
                   jax.ShapeDtypeStruct((3, D_FF), F32), jax.ShapeDtypeStruct((1, D_FF), F32),
                   jax.ShapeDtypeStruct((1, D_FF), F32)],
        scratch_shapes=[pltpu.VMEM((2, HALO + T, C), F32)],
        compiler_params=_params("parallel", "arbitrary"), name="convffn_bwd_a")(
            up_pre, up_pre, dx, w_down, conv_w, conv_w, conv_b, conv_b)
    dup, dwa, dwb, dba, dbb = res
    dconv_w = jnp.concatenate([dwa, dwb], axis=1)
    dconv_b = jnp.concatenate([dba, dbb], axis=1)
    return dup, dconv_w, dconv_b


def _convffn_bwd_b(dup, conv_w, w_up_t):
    S = dup.shape[1]
    tm, tk = 512, D_FF // 4
    per_half = D_FF // tk
    nk = 2 * per_half
    per = tm // HALO
    n_blk = S // tm

    def body(d_ref, halo_ref, cw_ref, w_ref, dpre_ref, dh_ref, ext_ref, acc_ref):
        i, k = pl.program_id(0), pl.program_id(1)
        n = tm + HALO
        ext_ref[0:tm, :] = d_ref[...].astype(F32)
        ext_ref[tm:, :] = jnp.where(i == n_blk - 1, 0.0, halo_ref[...].astype(F32))
        ext = ext_ref[...]
        y = cw_ref[2:3, :] * ext[0:tm] + cw_ref[1:2, :] * pltpu.roll(ext, n - 1, 0)[0:tm]
        y = y + cw_ref[0:1, :] * pltpu.roll(ext, n - 2, 0)[0:tm]
        pre = y.astype(BF16)
        dpre_ref[...] = pre
        prod = _dot(pre, w_ref[...], "nn")

        @pl.when(k == 0)
        def _():
            acc_ref[...] = prod

        @pl.when(k > 0)
        def _():
            acc_ref[...] += prod

        @pl.when(k == nk - 1)
        def _():
            dh_ref[...] = acc_ref[...].astype(BF16)

    blk = pl.BlockSpec((None, tm, tk), lambda i, k: (k // per_half, i, k % per_half))
    halo = pl.BlockSpec((None, HALO, tk),
                        lambda i, k: (k // per_half, jnp.minimum((i + 1) * per, S // HALO - 1), k % per_half))
    return pl.pallas_call(
        body, grid=(n_blk, nk),
        in_specs=[blk, halo, pl.BlockSpec((3, tk), lambda i, k: (0, k)), pl.BlockSpec((tk, D_MODEL), lambda i, k: (k, 0))],
        out_specs=[blk, pl.BlockSpec((tm, D_MODEL), lambda i, k: (i, 0))],
        out_shape=[jax.ShapeDtypeStruct((2, S, D_FF), BF16), jax.ShapeDtypeStruct((S, D_MODEL), BF16)],
        scratch_shapes=[pltpu.VMEM((tm + HALO, tk), F32), pltpu.VMEM((tm, D_MODEL), F32)],
        compiler_params=_params("parallel", "arbitrary"), name="convffn_bwd_b")(dup, dup, conv_w, w_up_t)


def _position():
    return lax.axis_index("x"), lax.axis_index("y"), lax.axis_index("c")


def _flip(pos, k):
    x, y, c = pos
    return (1 - x if k & 4 else x, 1 - y if k & 2 else y, 1 - c if k & 1 else c)


def _index(pos):
    return 4 * pos[0] + 2 * pos[1] + pos[2]


_HBM = pl.BlockSpec(memory_space=pltpu.HBM)
_SEM = pl.BlockSpec(memory_space=pltpu.SEMAPHORE)
_ANY = pl.BlockSpec(memory_space=pl.ANY)
_EFFECT = pltpu.SideEffectType.DATAFLOW_SIDE_EFFECTING


def _in_hbm(a):
    return pltpu.with_memory_space_constraint(a, pltpu.HBM)


def _split_copies(name, srcs, lands, n_copies, plan, start_after, wait_after):
    ns, nl = len(srcs), len(lands)
    start_after = tuple(start_after)
    na = len(start_after)

    def start_body(*refs):
        src_refs, land_refs = refs[:ns], refs[ns:ns + nl]
        send_sems, recv_sems = refs[ns + nl + na], refs[ns + nl + na + 1]
        token = refs[-1]
        for k, (src, dst, _, peer) in enumerate(plan(src_refs, land_refs)):
            pltpu.make_async_remote_copy(src_ref=src, dst_ref=dst, send_sem=send_sems.at[k], recv_sem=recv_sems.at[k],
                                         device_id=peer, device_id_type=MESH).start()
        token[...] = jnp.zeros_like(token)

    thru = [pltpu.HBM(a.shape, a.dtype) for a in list(srcs) + list(lands)]
    res = pl.pallas_call(
        start_body, name=name + "_start",
        out_shape=[pltpu.SemaphoreType.DMA((n_copies,)), pltpu.SemaphoreType.DMA((n_copies,))] + thru
        + [jax.ShapeDtypeStruct((8, 128), F32)],
        in_specs=[_HBM] * (ns + nl) + [_ANY] * na,
        out_specs=[_SEM, _SEM] + [_HBM] * (ns + nl) + [pl.BlockSpec(memory_space=pltpu.VMEM)],
        input_output_aliases={i: 2 + i for i in range(ns + nl)},
        compiler_params=pltpu.CompilerParams(has_side_effects=_EFFECT),
    )(*[_in_hbm(a) for a in list(srcs) + list(lands)], *start_after)
    send_sems, recv_sems, token = res[0], res[1], res[-1]
    thru_vals = res[2:-1]

    def wait():
        def wait_body(*refs):
            src_refs, land_refs = refs[:ns], refs[ns:ns + nl]
            s_sems, r_sems = refs[ns + nl], refs[ns + nl + 1]
            for k, (src, _, dst, peer) in enumerate(plan(src_refs, land_refs)):
                cp = pltpu.make_async_remote_copy(src_ref=src, dst_ref=dst, send_sem=s_sems.at[k], recv_sem=r_sems.at[k],
                                                  device_id=peer, device_id_type=MESH)
                cp.wait_send()
                cp.wait_recv()

        out = pl.pallas_call(
            wait_body, name=name + "_wait", out_shape=thru,
            in_specs=[_HBM] * (ns + nl) + [_SEM, _SEM, _ANY], out_specs=[_HBM] * (ns + nl),
            input_output_aliases={i: i for i in range(ns + nl)},
            compiler_params=pltpu.CompilerParams(has_side_effects=_EFFECT),
        )(*thru_vals, send_sems, recv_sems, wait_after())
        return list(out[:ns]), list(out[ns:])

    return wait, token


def _gather_level1_plan(n):
    def plan(src_refs, land_refs):
        me = _position()
        x, y, c = me
        peers = [(x, y, 1 - c), (1 - x, y, c), (x, 1 - y, c), (1 - x, 1 - y, c)]
        return [(src_refs[a], land_refs[a].at[_index(me)], land_refs[a].at[_index(p)], p)
                for a in range(n) for p in peers]
    return plan


def _gather_level2(lands):
    n = len(lands)

    def body(*refs):
        lin = refs[:n]
        send_sems, recv_sems = refs[2 * n:]
        x, y, c = _position()
        sibling = (x, y, 1 - c)
        chips = [(1 - x, y), (x, 1 - y), (1 - x, 1 - y)]

        def copy(a, j, core):
            slot = lin[a].at[_index((*chips[j], core))]
            return pltpu.make_async_remote_copy(
                src_ref=slot, dst_ref=slot, send_sem=send_sems.at[3 * a + j], recv_sem=recv_sems.at[3 * a + j],
                device_id=sibling, device_id_type=MESH)

        sent = [copy(a, j, c) for a in range(n) for j in range(3)]
        for cp in sent:
            cp.start()
        for a in range(n):
            for j in range(3):
                copy(a, j, 1 - c).wait_recv()
        for cp in sent:
            cp.wait_send()

    return pl.pallas_call(
        body, out_shape=[jax.ShapeDtypeStruct(l.shape, l.dtype) for l in lands],
        in_specs=[_ANY] * n, out_specs=[_ANY] * n, input_output_aliases={a: a for a in range(n)},
        scratch_shapes=[pltpu.SemaphoreType.DMA((3 * n,)), pltpu.SemaphoreType.DMA((3 * n,))],
        name="gather_level2")(*lands)


def _gather_direct_plan(n):
    def plan(src_refs, land_refs):
        me = _position()
        peers = [_flip(me, k) for k in range(1, N_DEV)]
        return [(src_refs[a], land_refs[a].at[_index(me)], land_refs[a].at[_index(p)], p)
                for a in range(n) for p in peers]
    return plan


def _place_own(shards):
    me = _index(_position())
    return [lax.dynamic_update_slice(lax.empty((N_DEV,) + s.shape, s.dtype), s[None], (me, 0, 0)) for s in shards]


def _exchange_direct_plan(n):
    def plan(src_refs, land_refs):
        me = _position()
        peers = [_flip(me, k) for k in range(1, N_DEV)]
        return [(src_refs[a].at[_index(p)], land_refs[a].at[_index(me)], land_refs[a].at[_index(p)], p)
                for a in range(n) for p in peers]
    return plan


def _adamw(w, g, m, v):
    m = ADAM_B1 * m + (1.0 - ADAM_B1) * g
    v = ADAM_B2 * v + (1.0 - ADAM_B2) * (g * g)
    m_hat = m / (1.0 - ADAM_B1 ** ADAM_STEP)
    v_hat = v / (1.0 - ADAM_B2 ** ADAM_STEP)
    delta = -ADAM_LR * (m_hat / (jnp.sqrt(v_hat) + ADAM_EPS) + ADAM_WD * w)
    return delta, m, v


def _row_tile(rows, unit, cap=256):
    best = rows
    for t in range(unit, min(rows, cap) + 1, unit):
        if rows % t == 0:
            best = t
    return best


def _sum_partials(name, recv):
    n_src, R, C = recv.shape
    T = _row_tile(R, 16)

    def body(r_ref, g_out):
        g = r_ref[0].astype(F32)
        for s in range(1, n_src):
            g = g + r_ref[s].astype(F32)
        g_out[...] = g

    return pl.pallas_call(
        body, grid=(R // T,), in_specs=[pl.BlockSpec((n_src, T, C), lambda i: (0, i, 0))],
        out_specs=pl.BlockSpec((T, C), lambda i: (i, 0)), out_shape=jax.ShapeDtypeStruct((R, C), F32),
        compiler_params=_params("parallel"), name=name)(recv)


def _adamw_update(name, w, g, m, v):
    R, C = w.shape
    T = _row_tile(R, 8)

    def body(w_ref, g_ref, m_ref, v_ref, d_out, m_out, v_out):
        delta, mn, vn = _adamw(w_ref[...], g_ref[...], m_ref[...], v_ref[...])
        d_out[...] = delta
        m_out[...] = mn
        v_out[...] = vn

    blk = pl.BlockSpec((T, C), lambda i: (i, 0))
    return pl.pallas_call(
        body, grid=(R // T,), in_specs=[blk] * 4, out_specs=[blk] * 3,
        out_shape=[jax.ShapeDtypeStruct((R, C), F32)] * 3, compiler_params=_params("parallel"), name=name)(w, g, m, v)


def _gather_vmem(p_ref, gath, send_sems, recv_sems):
    me = _position()
    gath[_index(me)] = p_ref[...]
    copies = []
    for k in range(1, N_DEV):
        peer = _flip(me, k)
        copies.append(pltpu.make_async_remote_copy(
            src_ref=p_ref, dst_ref=gath.at[_index(me)], send_sem=send_sems.at[k - 1],
            recv_sem=recv_sems.at[k - 1], device_id=peer, device_id_type=MESH))
    for cp in copies:
        cp.start()
    for k in range(1, N_DEV):
        peer = _flip(me, k)
        pltpu.make_async_remote_copy(
            src_ref=p_ref, dst_ref=gath.at[_index(peer)], send_sem=send_sems.at[k - 1],
            recv_sem=recv_sems.at[k - 1], device_id=peer, device_id_type=MESH).wait_recv()
    for cp in copies:
        cp.wait_send()


def _small_allreduce_adamw(part, w, m, v):
    R, C = part.shape

    def body(p_ref, w_ref, m_ref, v_ref, g_out, d_out, m_out, v_out, gath, send_sems, recv_sems):
        _gather_vmem(p_ref, gath, send_sems, recv_sems)
        g = gath[0]
        for s in range(1, N_DEV):
            g = g + gath[s]
        delta, mn, vn = _adamw(w_ref[...], g, m_ref[...], v_ref[...])
        g_out[...] = g
        d_out[...] = delta
        m_out[...] = mn
        v_out[...] = vn

    vm = pl.BlockSpec(memory_space=pltpu.VMEM)
    return pl.pallas_call(
        body, in_specs=[vm] * 4, out_specs=[vm] * 4, out_shape=[jax.ShapeDtypeStruct((R, C), F32)] * 4,
        scratch_shapes=[pltpu.VMEM((N_DEV, R, C), F32), pltpu.SemaphoreType.DMA((7,)), pltpu.SemaphoreType.DMA((7,))],
        name="small_allreduce_adamw")(part, w, m, v)


_BIG = ("w_in", "w_up", "w_down", "w_out", "w_pool_out", "w_attn_out")
_TRANSPOSED = ("w_in", "w_up", "w_pool_out", "w_attn_out")
_SMALL = (("g_mix", 2048), ("b_gate", 4096), ("pool_scale", 1024), ("g_ffn", 2048), ("conv_b", 11264), ("g_final", 2048))
CONV_ROWS = 8


def _to_comm(name, shard):
    return shard.T if name in _TRANSPOSED else shard


def _pack_small(vals):
    flat = [vals[n].reshape(-1) for n, _ in _SMALL]
    used = sum(f.shape[0] for f in flat)
    flat.append(jnp.zeros((SMALL_ROWS * PACK_COLS - used,), F32))
    return jnp.concatenate(flat).reshape(SMALL_ROWS, PACK_COLS)


def _unpack_small(packed, like):
    flat = packed.reshape(-1)
    out, off = {}, 0
    for name, n in _SMALL:
        out[name] = flat[off:off + n].reshape(like[name].shape)
        off += n
    return out


def _local_step(x, target, W, small, first_token=0.0, first_weights=None, late_weights=None, emit=None):
    S = x.shape[0]
    o_qkv, o_gate = POOL_WIDTH, POOL_WIDTH + QKV_WIDTH
    if emit is None:
        emit = lambda grads: None

    def zero(token):
        return 0.0 if token is None else token[0, 0]

    h1 = _rms_fwd("rms1_fwd", x, small["g_mix"] + first_token)
    tok = None
    if first_weights is not None:
        w_first, tok = first_weights(h1)
        small = dict(small, conv_w_full=w_first.pop("conv_w_full"))
        W = dict(W, **w_first)
    w_in_t = W["w_in"]
    u = _mm_nt("proj_u", h1, w_in_t, tm=2048, tn=512, tk=2048, o_dtype=F32, n_rows=POOL_WIDTH, b_row_off=0,
               after=tok)
    qkv = _mm_nt("proj_qkv", h1, w_in_t, tm=2048, tn=512, tk=2048, o_dtype=BF16, n_rows=QKV_WIDTH, b_row_off=o_qkv,
                 after=tok)
    gates_pre = _mm_nt("proj_gates", h1, w_in_t, tm=2048, tn=512, tk=2048, o_dtype=BF16, n_rows=2 * D_MODEL,
                       b_row_off=o_gate, after=tok)
    pool_y = _pool_fwd(u, W["w_pool_lin"], small["pool_scale"])
    outs, lses = zip(*[_attn_fwd(gi, qkv) for gi in range(3)])
    attn_y = _merge_fwd(outs, lses)
    if late_weights is not None:
        W = dict(W, **late_weights(0, attn_y))
    w_up_t = W["w_up"]
    mixed, y_pool, y_attn = _mix_fwd(pool_y, attn_y, W["w_pool_out"], W["w_attn_out"], gates_pre, small["b_gate"])
    x1 = _mm_nn("out_proj", mixed, W["w_out"], tm=1024, tn=1024, tk=2048, o_dtype=F32, res=x)

    h2 = _rms_fwd("rms2_fwd", x1, small["g_ffn"])
    nj = D_FF // 512
    up_pre, act = _up_proj_convffn(h2, w_up_t, small["conv_w_full"], small["conv_b"])
    if late_weights is not None:
        W = dict(W, **late_weights(1, act))
    x2 = _mm_nn("down_proj", act, W["w_down"], tm=1024, tn=1024, tk=D_FF // 2, o_dtype=F32, res=x1)

    dx2, dx2_b, dg_final, sq_cols = _final_loss("final_loss", x2, small["g_final"], target)

    dw_down = _mm_tn("dw_down", act, dx2_b, tm=512, tn=2048, tk=TOKEN_TK)
    tok = emit(dict(w_down=dw_down))
    dup, dconv_w, dconv_b = _convffn_bwd_a(up_pre, dx2_b, W["w_down"], small["conv_w_full"],
                                           small["conv_b"] + zero(tok))
    dup_pre, dh2 = _convffn_bwd_b(dup, small["conv_w_full"], w_up_t)
    dw_up = _mm("dw_up", dup_pre, h2, mode="tn", grid=(2 * nj, 1, S // TOKEN_TK),
                a_spec=pl.BlockSpec((None, TOKEN_TK, 512), lambda i, j, k: (i // nj, k, i % nj)),
                b_spec=pl.BlockSpec((TOKEN_TK, D_MODEL), lambda i, j, k: (k, 0)),
                o_spec=pl.BlockSpec((512, D_MODEL), lambda i, j, k: (i, 0)),
                o_shape=(2 * D_FF, D_MODEL), o_dtype=BF16, acc_shape=(512, D_MODEL))
    tok = emit(dict(w_up=dw_up, conv_w=dconv_w))
    dx1, dx1_b, dg_ffn = _rms_bwd("rms2_bwd", x1, small["g_ffn"] + zero(tok), dh2, dx2)

    dmixed = _mm_nt("d_mixed", dx1_b, W["w_out"], tm=1024, tn=1024, tk=2048, o_dtype=BF16)
    dw_out = _mm_tn("dw_out", mixed, dx1_b, tm=512, tn=2048, tk=TOKEN_TK)
    tok = emit(dict(w_out=dw_out))
    dy_pool, dy_attn, dgates, db_gate = _mix_bwd(dmixed, gates_pre, small["b_gate"] + zero(tok), y_pool, y_attn)
    dpool_y = _mm_nn("d_pool_y", dy_pool, W["w_pool_out"], tm=1024, tn=1024, tk=2048, o_dtype=F32)
    dw_pool_out = _mm_tn("dw_pool_out", dy_pool, pool_y, tm=1024, tn=1024, tk=TOKEN_TK)
    dattn_y = _mm_nn("d_attn_y", dy_attn, W["w_attn_out"], tm=1024, tn=512, tk=2048, o_dtype=F32)
    dw_attn_out = _mm_tn("dw_attn_out", dy_attn, attn_y, tm=1024, tn=512, tk=TOKEN_TK)
    e, dw_pool_lin, dpool_scale = _pool_bwd_a(u, dpool_y, W["w_pool_lin"], small["pool_scale"])
    du = _pool_bwd_b(e)
    dos, dps = _merge_bwd(dattn_y, outs, lses)
    dqkv = [_attn_bwd(gi, qkv, dos[gi], lses[gi], dps[gi]) for gi in range(3)]
    dproj = jnp.concatenate([du] + [dqkv[gi][sec] for sec in range(3) for gi in range(3)] + [dgates], axis=1)
    dw_in = _mm_tn("dw_in", dproj, h1, tm=512, tn=2048, tk=TOKEN_TK)
    tok = emit(dict(w_in=dw_in, w_pool_out=dw_pool_out, w_attn_out=dw_attn_out, w_pool_lin=dw_pool_lin))
    dh1 = _mm_nn("d_h1", dproj, w_in_t, tm=1024, tn=1024, tk=IN_WIDTH // 4, o_dtype=BF16,
                 after=tok)
    grad_x, _, dg_mix = _rms_bwd("rms1_bwd", x, small["g_mix"] + zero(tok), dh1, dx1)

    grads = dict(w_in=dw_in, w_up=dw_up, w_down=dw_down, w_out=dw_out, w_pool_out=dw_pool_out,
                 w_attn_out=dw_attn_out, w_pool_lin=dw_pool_lin, conv_w=dconv_w)
    small_grads = dict(g_mix=dg_mix, b_gate=db_gate, pool_scale=dpool_scale, g_ffn=dg_ffn, conv_b=dconv_b,
                       g_final=dg_final)
    return sq_cols, grad_x, grads, small_grads


def kernel(x, g_mix, w_in, b_gate, w_pool_lin, pool_scale, w_pool_out, w_attn_out, w_out, g_ffn, w_up, conv_w, conv_b, w_down, g_final, loss_target, m_g_mix, m_w_in, m_b_gate, m_w_pool_lin, m_pool_scale, m_w_pool_out, m_w_attn_out, m_w_out, m_g_ffn, m_w_up, m_conv_w, m_conv_b, m_w_down, m_g_final, v_g_mix, v_w_in, v_b_gate, v_w_pool_lin, v_pool_scale, v_w_pool_out, v_w_attn_out, v_w_out, v_g_ffn, v_w_up, v_conv_w, v_conv_b, v_w_down, v_g_final):
    given = dict(g_mix=g_mix, w_in=w_in, b_gate=b_gate, w_pool_lin=w_pool_lin, pool_scale=pool_scale,
                 w_pool_out=w_pool_out, w_attn_out=w_attn_out, w_out=w_out, g_ffn=g_ffn, w_up=w_up, conv_w=conv_w,
                 conv_b=conv_b, w_down=w_down, g_final=g_final)
    mom_m = dict(g_mix=m_g_mix, w_in=m_w_in, b_gate=m_b_gate, w_pool_lin=m_w_pool_lin, pool_scale=m_pool_scale,
                 w_pool_out=m_w_pool_out, w_attn_out=m_w_attn_out, w_out=m_w_out, g_ffn=m_g_ffn, w_up=m_w_up,
                 conv_w=m_conv_w, conv_b=m_conv_b, w_down=m_w_down, g_final=m_g_final)
    mom_v = dict(g_mix=v_g_mix, w_in=v_w_in, b_gate=v_b_gate, w_pool_lin=v_w_pool_lin, pool_scale=v_pool_scale,
                 w_pool_out=v_w_pool_out, w_attn_out=v_w_attn_out, w_out=v_w_out, g_ffn=v_g_ffn, w_up=v_w_up,
                 conv_w=v_conv_w, conv_b=v_conv_b, w_down=v_w_down, g_final=v_g_final)
    def full(g):
        return g.reshape(N_DEV * g.shape[1], g.shape[2])

    shard = dict(w_in=_to_comm("w_in", w_in[0]).astype(BF16))
    W = {}
    first = [shard["w_in"], w_pool_lin[0].reshape(4 * 32, POOL_GROUP_WIDTH),
             jnp.pad(conv_w[0], ((0, CONV_ROWS - 3), (0, 0)))]
    hold = {}
    w_in_wait, first_token = _split_copies(
        "gather_w_in", first, _place_own(first), 4 * len(first), _gather_level1_plan(len(first)),
        start_after=(), wait_after=lambda: hold["h1"])

    stages = (("w_out", "w_pool_out", "w_attn_out", "w_up"), ("w_down",))
    stage_after = [None, None]
    stage_wait = []
    for n in (n for names in stages for n in names):
        shard[n] = _to_comm(n, given[n][0] + first_token[0, 0]).astype(BF16)

    def first_weights(h1):
        hold["h1"] = h1
        w_in_all, lin_all, conv_w_all = _gather_level2(w_in_wait()[1])
        w_in_full = full(w_in_all)
        lin_full = jnp.moveaxis(lin_all.reshape(N_DEV, 4, 32, POOL_GROUP_WIDTH), 0, 1).reshape(
            4, POOL_GROUP_WIDTH, POOL_GROUP_WIDTH).astype(BF16)
        conv_w_full = jnp.moveaxis(conv_w_all[:, :3, :], 0, 1).reshape(3, 2 * D_FF)
        order_after = (w_in_full,)
        for k, names in enumerate(stages):
            srcs = [shard[n] for n in names]
            wait, token = _split_copies(f"gather_stage{k}", srcs, _place_own(srcs), 7 * len(names),
                                        _gather_direct_plan(len(names)), start_after=order_after,
                                        wait_after=lambda k=k: stage_after[k])
            stage_wait.append(wait)
            order_after = (token,)
        return dict(w_in=w_in_full, w_pool_lin=lin_full, conv_w_full=conv_w_full), token

    def late_weights(stage, after):
        stage_after[stage] = after
        return {n: full(l) for n, l in zip(stages[stage], stage_wait[stage]()[1])}

    small = dict(g_mix=g_mix, b_gate=b_gate, pool_scale=pool_scale, g_ffn=g_ffn, conv_b=conv_b,
                 g_final=g_final.reshape(1, D_MODEL))

    pending = []
    done = {}

    def emit(group):
        names = list(group)
        parts = []
        for n in names:
            g = group[n]
            if n == "w_pool_lin":
                g = jnp.moveaxis(g.reshape(4, N_DEV, 32, POOL_GROUP_WIDTH), 1, 0).reshape(
                    N_DEV, 4 * 32, POOL_GROUP_WIDTH).astype(BF16)
            elif n == "conv_w":
                g = jnp.pad(jnp.moveaxis(g.reshape(3, N_DEV, 2 * D_FF // N_DEV), 1, 0),
                            ((0, 0), (0, CONV_ROWS - 3), (0, 0))).astype(BF16)
            else:
                g = g.reshape(N_DEV, g.shape[0] // N_DEV, g.shape[1])
            parts.append(g)
        me = _index(_position())
        lands = [lax.dynamic_update_slice(lax.empty(p.shape, p.dtype), lax.dynamic_index_in_dim(p, me, 0),
                                          (me, 0, 0)) for p in parts]
        wait, token = _split_copies(
            "exchange_" + names[0], parts, lands, 7 * len(names), _exchange_direct_plan(len(names)),
            start_after=(), wait_after=lambda: done["grad_x"])
        pending.append((names, wait))
        return token

    prepared = sum(shard[n][0, 0].astype(F32) for names in stages for n in names) * 0.0
    sq_cols, grad_x, _, small_grads = _local_step(x[0], loss_target[0], W, small,
                                                  first_token=first_token[0, 0] + prepared,
                                                  first_weights=first_weights, late_weights=late_weights, emit=emit)
    done["grad_x"] = grad_x
    loss = lax.psum(0.5 / D_MODEL * jnp.sum(sq_cols), AXES)

    res = {}
    for n, r in [(n, r) for names, wait in pending for n, r in zip(names, wait()[1])]:
        g = _sum_partials("sum_" + n, r)
        if n in _TRANSPOSED:
            g = g.T
        elif n == "conv_w":
            g = g[:3]
        shape = given[n].shape
        rows = math.prod(shape[:-1])
        w2, m2, v2 = (d[n].reshape(rows, shape[-1]) for d in (given, mom_m, mom_v))
        upd = _adamw_update("adamw_" + n, w2, g.reshape(rows, shape[-1]), m2, v2)
        res[n] = [a.reshape(shape) for a in (g,) + tuple(upd)]

    packed_small = [_pack_small(d) for d in (given, mom_m, mom_v)]
    res_small = [_unpack_small(p, given) for p in _small_allreduce_adamw(_pack_small(small_grads), *packed_small)]

    order = ["g_mix", "w_in", "b_gate", "w_pool_lin", "pool_scale", "w_pool_out", "w_attn_out", "w_out", "g_ffn",
             "w_up", "conv_w", "conv_b", "w_down", "g_final"]
    outs = [loss, grad_x[None]]
    for kind in range(4):
        for n in order:
            outs.append(res[n][kind] if n in res else res_small[kind][n])
    return tuple(outs)
```

```python
import math

import jax
import jax.numpy as jnp
from jax import lax
from jax.experimental import pallas as pl
from jax.experimental.pallas import tpu as pltpu

F32 = jnp.float32
BF16 = jnp.bfloat16
MESH = pl.DeviceIdType.MESH
AXES = ("x", "y", "c")
N_DEV = 8

D_MODEL = 2048
POOL_WINDOWS = (2, 4, 8, 16)
POOL_GROUP_WIDTH = 256
POOL_WIDTH = 1024
ATTN_GROUPS = ((128, 1), (512, 4), (2048, 16))
SPAN = 128
HEADS_PER_GROUP = 4
N_ATTN_HEADS = 12
HEAD_DIM = 128
ATTN_WIDTH = 1536
QKV_WIDTH = 3 * ATTN_WIDTH
ATTN_OUT_WIDTH = 512
IN_WIDTH = 9728
D_FF = 5632
RMS_EPS = 1e-6
ADAM_LR, ADAM_B1, ADAM_B2, ADAM_EPS, ADAM_WD, ADAM_STEP = 0.001, 0.9, 0.999, 1e-08, 0.01, 10

VMEM_LIMIT_BYTES = 48 * 1024 * 1024
HALO = 16
TOKEN_TK = 2048
MXU_DIM = 256
PACK_COLS = 1024
SMALL_ROWS = 24
NEG_BIG = -1e30

_DN = {"nn": (((1,), (0,)), ((), ())), "nt": (((1,), (1,)), ((), ())), "tn": (((0,), (0,)), ((), ()))}


def _params(*sem):
    return pltpu.CompilerParams(dimension_semantics=sem, vmem_limit_bytes=VMEM_LIMIT_BYTES)


def _dot(a, b, mode):
    return lax.dot_general(a.astype(BF16), b.astype(BF16), _DN[mode], preferred_element_type=F32)


def _mm(name, a, b, *, mode, grid, a_spec, b_spec, o_spec, o_shape, o_dtype, acc_shape, res=None, res_spec=None,
        after=None):
    nk = grid[2]
    n_in = 2 + (res is not None) + (after is not None)

    def body(*refs):
        a_ref, b_ref = refs[:2]
        r_ref = refs[2] if res is not None else None
        o_ref = refs[n_in]
        scr = refs[n_in + 1:]
        prod = _dot(a_ref[...], b_ref[...], mode)

        def finish(val):
            if r_ref is not None:
                val = val + r_ref[...]
            o_ref[...] = val.astype(o_ref.dtype)

        if nk == 1:
            finish(prod)
        else:
            acc = scr[0]
            k = pl.program_id(2)

            @pl.when(k == 0)
            def _():
                acc[...] = prod

            @pl.when(k > 0)
            def _():
                acc[...] += prod

            @pl.when(k == nk - 1)
            def _():
                finish(acc[...])

    in_specs = [a_spec, b_spec]
    args = [a, b]
    if res is not None:
        in_specs.append(res_spec)
        args.append(res)
    if after is not None:
        in_specs.append(pl.BlockSpec((8, 128), lambda i, j, k: (0, 0)))
        args.append(after)
    return pl.pallas_call(
        body, grid=grid, in_specs=in_specs, out_specs=o_spec,
        out_shape=jax.ShapeDtypeStruct(o_shape, o_dtype),
        scratch_shapes=[] if nk == 1 else [pltpu.VMEM(acc_shape, F32)],
        compiler_params=_params("parallel", "parallel", "arbitrary"), name=name)(*args)


def _mm_nn(name, a, b, *, tm, tn, tk, o_dtype, res=None, after=None):
    M, K = a.shape
    N = b.shape[1]
    return _mm(name, a, b, mode="nn", grid=(M // tm, N // tn, K // tk),
               a_spec=pl.BlockSpec((tm, tk), lambda i, j, k: (i, k)),
               b_spec=pl.BlockSpec((tk, tn), lambda i, j, k: (k, j)),
               o_spec=pl.BlockSpec((tm, tn), lambda i, j, k: (i, j)),
               o_shape=(M, N), o_dtype=o_dtype, acc_shape=(tm, tn), res=res,
               res_spec=pl.BlockSpec((tm, tn), lambda i, j, k: (i, j)), after=after)


def _mm_nt(name, a, b, *, tm, tn, tk, o_dtype, n_rows=None, b_row_off=0, after=None):
    M, K = a.shape
    N = b.shape[0] if n_rows is None else n_rows
    off = b_row_off // tn
    return _mm(name, a, b, mode="nt", grid=(M // tm, N // tn, K // tk),
               a_spec=pl.BlockSpec((tm, tk), lambda i, j, k: (i, k)),
               b_spec=pl.BlockSpec((tn, tk), lambda i, j, k: (j + off, k)),
               o_spec=pl.BlockSpec((tm, tn), lambda i, j, k: (i, j)),
               o_shape=(M, N), o_dtype=o_dtype, acc_shape=(tm, tn), after=after)


def _mm_tn(name, a, b, *, tm, tn, tk):
    K, M = a.shape
    N = b.shape[1]
    return _mm(name, a, b, mode="tn", grid=(M // tm, N // tn, K // tk),
               a_spec=pl.BlockSpec((tk, tm), lambda i, j, k: (k, i)),
               b_spec=pl.BlockSpec((tk, tn), lambda i, j, k: (k, j)),
               o_spec=pl.BlockSpec((tm, tn), lambda i, j, k: (i, j)),
               o_shape=(M, N), o_dtype=BF16, acc_shape=(tm, tn))


def _rms_fwd(name, x, g):
    S, D = x.shape
    T = 512

    def body(x_ref, g_ref, h_ref):
        xv = x_ref[...]
        r = lax.rsqrt(jnp.mean(xv * xv, axis=-1, keepdims=True) + RMS_EPS)
        h_ref[...] = (xv * r * g_ref[...]).astype(BF16)

    return pl.pallas_call(
        body, grid=(S // T,),
        in_specs=[pl.BlockSpec((T, D), lambda i: (i, 0)), pl.BlockSpec((1, D), lambda i: (0, 0))],
        out_specs=pl.BlockSpec((T, D), lambda i: (i, 0)),
        out_shape=jax.ShapeDtypeStruct((S, D), BF16), compiler_params=_params("parallel"), name=name)(x, g)


def _out_proj_norm(mixed, w_out, x, g):
    S, D = x.shape
    tm = 512

    def body(m_ref, w_ref, x_ref, g_ref, x1_ref, h_ref):
        x1 = x_ref[...] + _dot(m_ref[...], w_ref[...], "nn")
        x1_ref[...] = x1
        r = lax.rsqrt(jnp.mean(x1 * x1, axis=-1, keepdims=True) + RMS_EPS)
        h_ref[...] = (x1 * r * g_ref[...]).astype(BF16)

    row = pl.BlockSpec((tm, D), lambda i: (i, 0))
    return pl.pallas_call(
        body, grid=(S // tm,),
        in_specs=[row, pl.BlockSpec((D, D), lambda i: (0, 0)), row, pl.BlockSpec((1, D), lambda i: (0, 0))],
        out_specs=[row, row], out_shape=[jax.ShapeDtypeStruct((S, D), F32), jax.ShapeDtypeStruct((S, D), BF16)],
        compiler_params=_params("parallel"), name="out_proj_norm")(mixed, w_out, x, g)


def _rms_bwd(name, x, g, dh, dres, with_bf16):
    S, D = x.shape
    T = 256

    def body(x_ref, g_ref, dh_ref, dres_ref, dx_ref, *rest):
        dg_ref = rest[-1]
        i = pl.program_id(0)
        xv = x_ref[...]
        r = lax.rsqrt(jnp.mean(xv * xv, axis=-1, keepdims=True) + RMS_EPS)
        xhat = xv * r
        dhv = dh_ref[...].astype(F32)
        gdh = dhv * g_ref[...]
        c = jnp.mean(xhat * gdh, axis=-1, keepdims=True)
        dx = dres_ref[...] + r * (gdh - xhat * c)
        dx_ref[...] = dx
        if with_bf16:
            rest[0][...] = dx.astype(BF16)
        part = jnp.sum(dhv * xhat, axis=0, keepdims=True)

        @pl.when(i == 0)
        def _():
            dg_ref[...] = part

        @pl.when(i > 0)
        def _():
            dg_ref[...] += part

    row = pl.BlockSpec((T, D), lambda i: (i, 0))
    vec = pl.BlockSpec((1, D), lambda i: (0, 0))
    narrow = [jax.ShapeDtypeStruct((S, D), BF16)] if with_bf16 else []
    return pl.pallas_call(
        body, grid=(S // T,), in_specs=[row, vec, row, row], out_specs=[row] * (1 + len(narrow)) + [vec],
        out_shape=[jax.ShapeDtypeStruct((S, D), F32)] + narrow + [jax.ShapeDtypeStruct((1, D), F32)],
        compiler_params=_params("arbitrary"), name=name)(x, g, dh, dres)


def _final_loss(name, x, g, target):
    S, D = x.shape
    T = 256

    def body(x_ref, g_ref, t_ref, dx_ref, dxb_ref, dg_ref, sq_ref):
        i = pl.program_id(0)
        xv = x_ref[...]
        r = lax.rsqrt(jnp.mean(xv * xv, axis=-1, keepdims=True) + RMS_EPS)
        xhat = xv * r
        err = xhat * g_ref[...] - t_ref[...]
        dy = err * (1.0 / D)
        gdy = dy * g_ref[...]
        c = jnp.mean(xhat * gdy, axis=-1, keepdims=True)
        dx = r * (gdy - xhat * c)
        dx_ref[...] = dx
        dxb_ref[...] = dx.astype(BF16)
        dg_part = jnp.sum(dy * xhat, axis=0, keepdims=True)
        sq_part = jnp.sum(err * err, axis=0, keepdims=True)

        @pl.when(i == 0)
        def _():
            dg_ref[...] = dg_part
            sq_ref[...] = sq_part

        @pl.when(i > 0)
        def _():
            dg_ref[...] += dg_part
            sq_ref[...] += sq_part

    row = pl.BlockSpec((T, D), lambda i: (i, 0))
    vec = pl.BlockSpec((1, D), lambda i: (0, 0))
    return pl.pallas_call(
        body, grid=(S // T,), in_specs=[row, vec, row], out_specs=[row, row, vec, vec],
        out_shape=[jax.ShapeDtypeStruct((S, D), F32), jax.ShapeDtypeStruct((S, D), BF16),
                   jax.ShapeDtypeStruct((1, D), F32), jax.ShapeDtypeStruct((1, D), F32)],
        compiler_params=_params("arbitrary"), name=name)(x, g, target)


def _prev_halo_spec(T, C, col_map):
    per = T // HALO
    return pl.BlockSpec((HALO, C), lambda *g: (jnp.maximum(g[0] * per - 1, 0), col_map(*g)))


def _pool_window_sums(ext, T, i):
    t = i * T + lax.broadcasted_iota(jnp.int32, (T, 1), 0)
    outs = []
    cnts = []
    for gi, w in enumerate(POOL_WINDOWS):
        s = ext[:, gi * POOL_GROUP_WIDTH:(gi + 1) * POOL_GROUP_WIDTH]
        sh = 1
        while sh < w:
            s = s + pltpu.roll(s, sh, 0)
            sh *= 2
        cnt = jnp.minimum(t + 1, w).astype(F32)
        outs.append(s[HALO:] / cnt - ext[HALO:, gi * POOL_GROUP_WIDTH:(gi + 1) * POOL_GROUP_WIDTH])
        cnts.append(cnt)
    return outs, cnts


def _load_ext(ext_ref, halo_ref, blk_ref, first):
    ext_ref[0:HALO, :] = jnp.where(first, 0.0, halo_ref[...].astype(F32))
    ext_ref[HALO:, :] = blk_ref[...].astype(F32)


def _pool_fwd(u, w_lin, scale):
    S = u.shape[0]
    T = 512
    W = POOL_WIDTH

    def body(u_ref, halo_ref, wl_ref, sc_ref, y_ref, ext_ref):
        i = pl.program_id(0)
        _load_ext(ext_ref, halo_ref, u_ref, i == 0)
        pooled, _ = _pool_window_sums(ext_ref[...], T, i)
        for gi in range(4):
            z = _dot(pooled[gi], wl_ref[gi], "nn")
            cs = slice(gi * POOL_GROUP_WIDTH, (gi + 1) * POOL_GROUP_WIDTH)
            y_ref[:, cs] = (z * sc_ref[:, cs]).astype(BF16)

    return pl.pallas_call(
        body, grid=(S // T,),
        in_specs=[pl.BlockSpec((T, W), lambda i: (i, 0)), _prev_halo_spec(T, W, lambda i: 0),
                  pl.BlockSpec((4, 256, 256), lambda i: (0, 0, 0)), pl.BlockSpec((1, W), lambda i: (0, 0))],
        out_specs=pl.BlockSpec((T, W), lambda i: (i, 0)),
        out_shape=jax.ShapeDtypeStruct((S, W), BF16),
        scratch_shapes=[pltpu.VMEM((HALO + T, W), F32)],
        compiler_params=_params("parallel"), name="pool_fwd")(u, u, w_lin, scale)


def _pool_bwd_a(u, dy, w_lin, scale):
    S = u.shape[0]
    T = 512
    W = POOL_WIDTH

    def body(u_ref, halo_ref, dy_ref, wl_ref, sc_ref, e_ref, dwl_ref, dsc_ref, ext_ref):
        i = pl.program_id(0)
        _load_ext(ext_ref, halo_ref, u_ref, i == 0)
        pooled, cnts = _pool_window_sums(ext_ref[...], T, i)
        for gi in range(4):
            cs = slice(gi * POOL_GROUP_WIDTH, (gi + 1) * POOL_GROUP_WIDTH)
            z = _dot(pooled[gi], wl_ref[gi], "nn")
            dyg = dy_ref[:, cs]
            dsc = jnp.sum(dyg * z, axis=0, keepdims=True)
            dz = dyg * sc_ref[:, cs]
            dwl = _dot(pooled[gi], dz, "tn")
            dpooled = _dot(dz, wl_ref[gi], "nt")
            e_ref[:, cs] = dpooled / cnts[gi]

            @pl.when(i == 0)
            def _():
                dwl_ref[gi] = dwl
                dsc_ref[:, cs] = dsc

            @pl.when(i > 0)
            def _():
                dwl_ref[gi] += dwl
                dsc_ref[:, cs] += dsc

    return pl.pallas_call(
        body, grid=(S // T,),
        in_specs=[pl.BlockSpec((T, W), lambda i: (i, 0)), _prev_halo_spec(T, W, lambda i: 0),
                  pl.BlockSpec((T, W), lambda i: (i, 0)),
                  pl.BlockSpec((4, 256, 256), lambda i: (0, 0, 0)), pl.BlockSpec((1, W), lambda i: (0, 0))],
        out_specs=[pl.BlockSpec((T, W), lambda i: (i, 0)), pl.BlockSpec((4, 256, 256), lambda i: (0, 0, 0)),
                   pl.BlockSpec((1, W), lambda i: (0, 0))],
        out_shape=[jax.ShapeDtypeStruct((S, W), F32), jax.ShapeDtypeStruct((4, 256, 256), F32),
                   jax.ShapeDtypeStruct((1, W), F32)],
        scratch_shapes=[pltpu.VMEM((HALO + T, W), F32)],
        compiler_params=_params("arbitrary"), name="pool_bwd_a")(u, u, dy, w_lin, scale)


def _pool_bwd_b(e):
    S = e.shape[0]
    T = 512
    W = POOL_WIDTH
    per = T // HALO
    n_blk = S // T

    def body(e_ref, halo_ref, du_ref, ext_ref):
        i = pl.program_id(0)
        ext_ref[0:T, :] = e_ref[...]
        ext_ref[T:, :] = jnp.where(i == n_blk - 1, 0.0, halo_ref[...])
        ext = ext_ref[...]
        n = T + HALO
        t = i * T + lax.broadcasted_iota(jnp.int32, (T, 1), 0)
        for gi, w in enumerate(POOL_WINDOWS):
            cs = slice(gi * POOL_GROUP_WIDTH, (gi + 1) * POOL_GROUP_WIDTH)
            s = ext[:, cs]
            sh = 1
            while sh < w:
                s = s + pltpu.roll(s, n - sh, 0)
                sh *= 2
            cnt = jnp.minimum(t + 1, w).astype(F32)
            du_ref[:, cs] = (s[0:T] - ext[0:T, cs] * cnt).astype(BF16)

    return pl.pallas_call(
        body, grid=(n_blk,),
        in_specs=[pl.BlockSpec((T, W), lambda i: (i, 0)),
                  pl.BlockSpec((HALO, W), lambda i: (jnp.minimum((i + 1) * per, S // HALO - 1), 0))],
        out_specs=pl.BlockSpec((T, W), lambda i: (i, 0)),
        out_shape=jax.ShapeDtypeStruct((S, W), BF16),
        scratch_shapes=[pltpu.VMEM((T + HALO, W), F32)],
        compiler_params=_params("parallel"), name="pool_bwd_b")(e, e)


def _slope(head):
    return 2.0 ** (-8.0 * (head + 1) / N_ATTN_HEADS)


def _attn_masks(b):
    row = lax.broadcasted_iota(jnp.int32, (SPAN, SPAN), 0)
    col = lax.broadcasted_iota(jnp.int32, (SPAN, SPAN), 1)
    dist_cur = (row - col).astype(F32)
    valid_cur = col <= row
    valid_prev = jnp.logical_and(col >= row, b > 0)
    return dist_cur, valid_cur, valid_prev


ATTN_BATCH = 4


def _attn_tiling(gi, S):
    d = ATTN_GROUPS[gi][1]
    hp = HEADS_PER_GROUP if d < 16 else 1
    return d, SPAN * d, S // (SPAN * d), hp, hp * HEAD_DIM, ATTN_BATCH // hp


_BATCHED_DN = {"nt": (((2,), (2,)), ((0,), (0,))), "nn": (((2,), (1,)), ((0,), (0,))), "tn": (((1,), (1,)), ((0,), (0,)))}


def _bdot(a, b, mode):
    return lax.dot_general(a.astype(BF16), b.astype(BF16), _BATCHED_DN[mode], preferred_element_type=F32)


def _batch_rows(it, d, rb):
    return [pl.ds(it * rb + j, SPAN, stride=d) for j in range(rb)]


def _take(scr, hp, rows):
    return jnp.stack([scr[hi, rw, :] for hi in range(hp) for rw in rows])


def _put(scr, hp, rows, val):
    for n, (hi, rw) in enumerate((hi, rw) for hi in range(hp) for rw in rows):
        scr[hi, rw, :] = val[n]


def _batch_bias(gi, d, hp, rb, dist):
    return jnp.stack([(-(_head_slope(gi, hp, hi) * d)) * dist for hi in range(hp) for _ in range(rb)])


def _head_slope(gi, hp, hi):
    if hp == HEADS_PER_GROUP:
        return _slope(gi * HEADS_PER_GROUP + hi)
    first = pl.program_id(1) * hp
    sl = jnp.float32(_slope(gi * HEADS_PER_GROUP + hi))
    for h0 in range(hp, HEADS_PER_GROUP, hp):
        sl = jnp.where(first == h0, jnp.float32(_slope(gi * HEADS_PER_GROUP + h0 + hi)), sl)
    return sl


def _split_heads(src_ref, dst_ref, hp):
    for hi in range(hp):
        dst_ref[hi] = src_ref[:, hi * HEAD_DIM:(hi + 1) * HEAD_DIM].astype(F32)


def _attn_fwd(gi, qkv):
    S = qkv.shape[0]
    d, C, nb, hp, W, rb = _attn_tiling(gi, S)
    scale = HEAD_DIM ** -0.5

    def body(q_ref, kc_ref, kp_ref, vc_ref, vp_ref, o_ref, lse_ref, qs, kcs, kps, vcs, vps, os, ls):
        b = pl.program_id(0)
        for src, dst in ((q_ref, qs), (kc_ref, kcs), (kp_ref, kps), (vc_ref, vcs), (vp_ref, vps)):
            _split_heads(src, dst, hp)
        dist_cur, valid_cur, valid_prev = _attn_masks(b)
        bias_cur = _batch_bias(gi, d, hp, rb, dist_cur)
        bias_far = _batch_bias(gi, d, hp, rb, dist_cur + SPAN)

        def sub_sequences(it, carry):
            rows = _batch_rows(it, d, rb)
            q = _take(qs, hp, rows)
            s_cur = jnp.where(valid_cur, _bdot(q, _take(kcs, hp, rows), "nt") * scale + bias_cur, NEG_BIG)
            s_prev = jnp.where(valid_prev, _bdot(q, _take(kps, hp, rows), "nt") * scale + bias_far, NEG_BIG)
            m = jnp.maximum(jnp.max(s_cur, axis=-1, keepdims=True), jnp.max(s_prev, axis=-1, keepdims=True))
            p_cur = jnp.exp(s_cur - m)
            p_prev = jnp.exp(s_prev - m)
            l = jnp.sum(p_cur, axis=-1, keepdims=True) + jnp.sum(p_prev, axis=-1, keepdims=True)
            o = (_bdot(p_cur, _take(vcs, hp, rows), "nn") + _bdot(p_prev, _take(vps, hp, rows), "nn")) / l
            _put(os, hp, rows, o)
            _put(ls, hp, rows, jnp.broadcast_to(m + jnp.log(l), o.shape))
            return carry

        lax.fori_loop(0, d // rb, sub_sequences, 0)
        for hi in range(hp):
            hs = slice(hi * HEAD_DIM, (hi + 1) * HEAD_DIM)
            o_ref[:, hs] = os[hi]
            lse_ref[:, hs] = ls[hi]

    per_sec = ATTN_WIDTH // W
    per_grp = ATTN_OUT_WIDTH // W

    def spec(sec, shift):
        return pl.BlockSpec((C, W), lambda b, h: (jnp.maximum(b + shift, 0), sec * per_sec + gi * per_grp + h))

    out = pl.BlockSpec((C, W), lambda b, h: (b, h))
    scr = pltpu.VMEM((hp, C, HEAD_DIM), F32)
    return pl.pallas_call(
        body, grid=(nb, HEADS_PER_GROUP // hp),
        in_specs=[spec(0, 0), spec(1, 0), spec(1, -1), spec(2, 0), spec(2, -1)], out_specs=[out, out],
        out_shape=[jax.ShapeDtypeStruct((S, ATTN_OUT_WIDTH), F32)] * 2, scratch_shapes=[scr] * 7,
        compiler_params=_params("parallel", "parallel"), name=f"attn_fwd_g{gi}")(qkv, qkv, qkv, qkv, qkv)


def _attn_bwd(gi, qkv, do, lse, dp):
    S = qkv.shape[0]
    d, C, nb, hp, W, rb = _attn_tiling(gi, S)
    scale = HEAD_DIM ** -0.5
    n_in = 12

    def body(*refs):
        ins, (dq_ref, dk_ref, dv_ref) = refs[:n_in], refs[n_in:n_in + 3]
        scr = refs[n_in + 3:]
        for src, dst in zip(ins, scr[:n_in]):
            _split_heads(src, dst, hp)
        qs, kcs, vcs, kps, vps, qns, dos, dons, lses, lsens, dps, dpns = scr[:n_in]
        dqs, dks, dvs = scr[n_in:]
        b = pl.program_id(0)
        dist_cur, valid_cur, valid_prev = _attn_masks(b)
        row = lax.broadcasted_iota(jnp.int32, (SPAN, SPAN), 0)
        col = lax.broadcasted_iota(jnp.int32, (SPAN, SPAN), 1)
        valid_next = jnp.logical_and(col >= row, b < nb - 1)

        bias_cur = _batch_bias(gi, d, hp, rb, dist_cur)
        bias_far = _batch_bias(gi, d, hp, rb, dist_cur + SPAN)

        def sub_sequences(it, carry):
            rows = _batch_rows(it, d, rb)
            q, kc, vc, kp, vp, qn, dov, don = (_take(s, hp, rows) for s in (qs, kcs, vcs, kps, vps, qns, dos, dons))
            lse_b, lse_n, dp_b, dp_n = (_take(s, hp, rows) for s in (lses, lsens, dps, dpns))
            p_cur = jnp.where(valid_cur, jnp.exp(_bdot(q, kc, "nt") * scale + bias_cur - lse_b), 0.0)
            p_prev = jnp.where(valid_prev, jnp.exp(_bdot(q, kp, "nt") * scale + bias_far - lse_b), 0.0)
            ds_cur = p_cur * (_bdot(dov, vc, "nt") - dp_b)
            ds_prev = p_prev * (_bdot(dov, vp, "nt") - dp_b)
            _put(dqs, hp, rows, (_bdot(ds_cur, kc, "nn") + _bdot(ds_prev, kp, "nn")) * scale)
            p_next = jnp.where(valid_next, jnp.exp(_bdot(qn, kc, "nt") * scale + bias_far - lse_n), 0.0)
            ds_next = p_next * (_bdot(don, vc, "nt") - dp_n)
            _put(dvs, hp, rows, _bdot(p_cur, dov, "tn") + _bdot(p_next, don, "tn"))
            _put(dks, hp, rows, (_bdot(ds_cur, q, "tn") + _bdot(ds_next, qn, "tn")) * scale)
            return carry

        lax.fori_loop(0, d // rb, sub_sequences, 0)
        for hi in range(hp):
            hs = slice(hi * HEAD_DIM, (hi + 1) * HEAD_DIM)
            dq_ref[:, hs] = dqs[hi].astype(BF16)
            dk_ref[:, hs] = dks[hi].astype(BF16)
            dv_ref[:, hs] = dvs[hi].astype(BF16)

    per_sec = ATTN_WIDTH // W
    per_grp = ATTN_OUT_WIDTH // W

    def qkv_spec(sec, shift):
        return pl.BlockSpec((C, W), lambda b, h: (jnp.clip(b + shift, 0, nb - 1), sec * per_sec + gi * per_grp + h))

    def act_spec(shift):
        return pl.BlockSpec((C, W), lambda b, h: (jnp.clip(b + shift, 0, nb - 1), h))

    out = act_spec(0)
    scr = pltpu.VMEM((hp, C, HEAD_DIM), F32)
    return pl.pallas_call(
        body, grid=(nb, HEADS_PER_GROUP // hp),
        in_specs=[qkv_spec(0, 0), qkv_spec(1, 0), qkv_spec(2, 0), qkv_spec(1, -1), qkv_spec(2, -1), qkv_spec(0, 1),
                  act_spec(0), act_spec(1), act_spec(0), act_spec(1), act_spec(0), act_spec(1)],
        out_specs=[out, out, out], out_shape=[jax.ShapeDtypeStruct((S, ATTN_OUT_WIDTH), BF16)] * 3,
        scratch_shapes=[scr] * (n_in + 3),
        compiler_params=_params("parallel", "parallel"), name=f"attn_bwd_g{gi}")(
            qkv, qkv, qkv, qkv, qkv, qkv, do, do, lse, lse, dp, dp)


def _group_weights(l_refs):
    l0, l1, l2 = (r[...] for r in l_refs)
    m = jnp.maximum(jnp.maximum(l0, l1), l2)
    e = [jnp.exp(l0 - m), jnp.exp(l1 - m), jnp.exp(l2 - m)]
    tot = e[0] + e[1] + e[2]
    return [ei / tot for ei in e]


def _merge_fwd(outs, lses):
    S, W = outs[0].shape
    T = 512

    def body(o0, o1, o2, l0, l1, l2, y_ref):
        w = _group_weights((l0, l1, l2))
        y_ref[...] = (w[0] * o0[...] + w[1] * o1[...] + w[2] * o2[...]).astype(BF16)

    blk = pl.BlockSpec((T, W), lambda i: (i, 0))
    return pl.pallas_call(
        body, grid=(S // T,), in_specs=[blk] * 6, out_specs=blk, out_shape=jax.ShapeDtypeStruct((S, W), BF16),
        compiler_params=_params("parallel"), name="merge_fwd")(*outs, *lses)


def _merge_bwd(dy, outs, lses):
    S, W = outs[0].shape
    T = 512

    def body(dy_ref, o0, o1, o2, l0, l1, l2, do0, do1, do2, dp0, dp1, dp2):
        w = _group_weights((l0, l1, l2))
        dyv = dy_ref[...]
        y = w[0] * o0[...] + w[1] * o1[...] + w[2] * o2[...]
        prod = dyv * y
        dots = []
        for hi in range(HEADS_PER_GROUP):
            hs = slice(hi * HEAD_DIM, (hi + 1) * HEAD_DIM)
            dots.append(jnp.broadcast_to(jnp.sum(prod[:, hs], axis=-1, keepdims=True), (T, HEAD_DIM)))
        dot_b = jnp.concatenate(dots, axis=1)
        for wg, do_ref, dp_ref in zip(w, (do0, do1, do2), (dp0, dp1, dp2)):
            do_ref[...] = (wg * dyv).astype(BF16)
            dp_ref[...] = wg * dot_b

    blk = pl.BlockSpec((T, W), lambda i: (i, 0))
    res = pl.pallas_call(
        body, grid=(S // T,), in_specs=[blk] * 7, out_specs=[blk] * 6,
        out_shape=[jax.ShapeDtypeStruct((S, W), BF16)] * 3 + [jax.ShapeDtypeStruct((S, W), F32)] * 3,
        compiler_params=_params("parallel"), name="merge_bwd")(dy, *outs, *lses)
    return res[:3], res[3:]


def _mix_fwd(pool_y, attn_y, w_pool_out_t, w_attn_out_t, gates_pre, b_gate):
    S = pool_y.shape[0]
    D = D_MODEL
    tm, tn = 1024, 512
    nj = D // tn

    def body(p_ref, a_ref, wp_ref, wa_ref, g0_ref, g1_ref, b0_ref, b1_ref, mixed_ref, yp_ref, ya_ref):
        yp = _dot(p_ref[...], wp_ref[...], "nt")
        ya = _dot(a_ref[...], wa_ref[...], "nt")
        g0 = jax.nn.sigmoid(g0_ref[...] + b0_ref[...])
        g1 = jax.nn.sigmoid(g1_ref[...] + b1_ref[...])
        mixed_ref[...] = (g0 * yp + g1 * ya).astype(BF16)
        yp_ref[...] = yp.astype(BF16)
        ya_ref[...] = ya.astype(BF16)

    out = pl.BlockSpec((tm, tn), lambda i, j: (i, j))
    return pl.pallas_call(
        body, grid=(S // tm, nj),
        in_specs=[pl.BlockSpec((tm, POOL_WIDTH), lambda i, j: (i, 0)),
                  pl.BlockSpec((tm, ATTN_OUT_WIDTH), lambda i, j: (i, 0)),
                  pl.BlockSpec((tn, POOL_WIDTH), lambda i, j: (j, 0)),
                  pl.BlockSpec((tn, ATTN_OUT_WIDTH), lambda i, j: (j, 0)),
                  pl.BlockSpec((tm, tn), lambda i, j: (i, j)), pl.BlockSpec((tm, tn), lambda i, j: (i, j + nj)),
                  pl.BlockSpec((1, tn), lambda i, j: (0, j)), pl.BlockSpec((1, tn), lambda i, j: (0, j + nj))],
        out_specs=[out, out, out], out_shape=[jax.ShapeDtypeStruct((S, D), BF16)] * 3,
        compiler_params=_params("parallel", "parallel"), name="mix_fwd")(
            pool_y, attn_y, w_pool_out_t, w_attn_out_t, gates_pre, gates_pre, b_gate, b_gate)


def _mix_bwd(dmixed, gates_pre, b_gate, y_pool, y_attn):
    S = dmixed.shape[0]
    D = D_MODEL
    T = 256

    def body(dm_ref, gp_ref, b_ref, yp_ref, ya_ref, dyp_ref, dya_ref, dg_ref, db_ref):
        i = pl.program_id(0)
        dm = dm_ref[...].astype(F32)
        parts = []
        for br, (y_ref, dy_ref) in enumerate(((yp_ref, dyp_ref), (ya_ref, dya_ref))):
            cs = slice(br * D, (br + 1) * D)
            g = jax.nn.sigmoid(gp_ref[:, cs] + b_ref[:, cs])
            dy_ref[...] = (dm * g).astype(BF16)
            dpre = dm * y_ref[...].astype(F32) * g * (1.0 - g)
            dg_ref[:, cs] = dpre.astype(BF16)
            parts.append(jnp.sum(dpre, axis=0, keepdims=True))

        @pl.when(i == 0)
        def _():
            db_ref[:, 0:D] = parts[0]
            db_ref[:, D:2 * D] = parts[1]

        @pl.when(i > 0)
        def _():
            db_ref[:, 0:D] += parts[0]
            db_ref[:, D:2 * D] += parts[1]

    row = pl.BlockSpec((T, D), lambda i: (i, 0))
    row2 = pl.BlockSpec((T, 2 * D), lambda i: (i, 0))
    vec2 = pl.BlockSpec((1, 2 * D), lambda i: (0, 0))
    return pl.pallas_call(
        body, grid=(S // T,), in_specs=[row, row2, vec2, row, row], out_specs=[row, row, row2, vec2],
        out_shape=[jax.ShapeDtypeStruct((S, D), BF16), jax.ShapeDtypeStruct((S, D), BF16),
                   jax.ShapeDtypeStruct((S, 2 * D), BF16), jax.ShapeDtypeStruct((1, 2 * D), F32)],
        compiler_params=_params("arbitrary"), name="mix_bwd")(dmixed, gates_pre, b_gate, y_pool, y_attn)


_CONV_T, _CONV_C = 512, 512


def _conv_taps(ext_ref, h, w_ref, b_ref, cols=slice(None)):
    ext = ext_ref[h, :, cols]
    x2 = pltpu.roll(ext, 2, 0)[HALO:]
    x1 = pltpu.roll(ext, 1, 0)[HALO:]
    x0 = ext[HALO:]
    y = b_ref[:, cols] + w_ref[0:1, cols] * x2
    y = y + w_ref[1:2, cols] * x1
    y = y + w_ref[2:3, cols] * x0
    return y, (x2, x1, x0)


def _gelu_parts(a):
    cdf = 0.5 * (1.0 + lax.erf(a * (1.0 / math.sqrt(2.0))))
    return cdf, a * cdf


def _conv_specs(S):
    T, C = _CONV_T, _CONV_C
    nj = D_FF // C
    blk = pl.BlockSpec((2, T, C), lambda i, j: (0, i, j))
    per = T // HALO
    halo = pl.BlockSpec((2, HALO, C), lambda i, j: (0, jnp.maximum(i * per - 1, 0), j))
    w_a = pl.BlockSpec((3, C), lambda i, j: (0, j))
    w_b = pl.BlockSpec((3, C), lambda i, j: (0, j + nj))
    b_a = pl.BlockSpec((1, C), lambda i, j: (0, j))
    b_b = pl.BlockSpec((1, C), lambda i, j: (0, j + nj))
    return T, C, nj, blk, halo, w_a, w_b, b_a, b_b


def _conv_load(ext_ref, halo_ref, up_ref, first):
    for h in range(2):
        ext_ref[h, 0:HALO, :] = jnp.where(first, 0.0, halo_ref[h].astype(F32))
        ext_ref[h, HALO:, :] = up_ref[h].astype(F32)


def _up_proj_convffn(h, w_up_t, conv_w, conv_b):
    S = h.shape[0]
    tm, C = 1024, _CONV_C
    nj = D_FF // C

    def body(h_ref, wa_ref, wb_ref, cwa_ref, cwb_ref, cba_ref, cbb_ref, up_ref, act_ref, ext_ref, carry_ref):
        i, j = pl.program_id(0), pl.program_id(1)
        for half, w_ref in enumerate((wa_ref, wb_ref)):
            pre = _dot(h_ref[...], w_ref[...], "nt").astype(BF16)
            up_ref[half] = pre
            ext_ref[half, 0:HALO, :] = jnp.where(i == 0, 0.0, carry_ref[j, half])
            ext_ref[half, HALO:, :] = pre.astype(F32)
            carry_ref[j, half] = ext_ref[half, tm:tm + HALO, :]
        a, _ = _conv_taps(ext_ref, 0, cwa_ref, cba_ref)
        b, _ = _conv_taps(ext_ref, 1, cwb_ref, cbb_ref)
        _, gelu = _gelu_parts(a)
        act_ref[...] = (gelu * b).astype(BF16)

    def w_spec(half):
        return pl.BlockSpec((C, D_MODEL), lambda i, j: (j + half * nj, 0))

    def c_spec(rows, half):
        return pl.BlockSpec((rows, C), lambda i, j: (0, j + half * nj))

    return pl.pallas_call(
        body, grid=(S // tm, nj),
        in_specs=[pl.BlockSpec((tm, D_MODEL), lambda i, j: (i, 0)), w_spec(0), w_spec(1),
                  c_spec(3, 0), c_spec(3, 1), c_spec(1, 0), c_spec(1, 1)],
        out_specs=[pl.BlockSpec((2, tm, C), lambda i, j: (0, i, j)), pl.BlockSpec((tm, C), lambda i, j: (i, j))],
        out_shape=[jax.ShapeDtypeStruct((2, S, D_FF), BF16), jax.ShapeDtypeStruct((S, D_FF), BF16)],
        scratch_shapes=[pltpu.VMEM((2, HALO + tm, C), F32), pltpu.VMEM((nj, 2, HALO, C), F32)],
        compiler_params=_params("arbitrary", "arbitrary"), name="up_proj_convffn")(
            h, w_up_t, w_up_t, conv_w, conv_w, conv_b, conv_b)


def _convffn_bwd_a(up_pre, dx, w_down, conv_w, conv_b):
    S = up_pre.shape[1]
    T, C, nj, blk, halo, w_a, w_b, b_a, b_b = _conv_specs(S)
    def swap(spec):
        return pl.BlockSpec(spec.block_shape, lambda j, i, _m=spec.index_map: _m(i, j))

    def body(up_ref, halo_ref, dx_ref, wd_ref, wa_ref, wb_ref, ba_ref, bb_ref, dup_ref, dwa_ref, dwb_ref, dba_ref,
             dbb_ref, ext_ref):
        i = pl.program_id(1)
        _conv_load(ext_ref, halo_ref, up_ref, i == 0)

        @pl.when(i == 0)
        def _():
            for acc_ref in (dwa_ref, dwb_ref, dba_ref, dbb_ref):
                acc_ref[...] = jnp.zeros_like(acc_ref)

        for c in range(C // MXU_DIM):
            cs = slice(c * MXU_DIM, (c + 1) * MXU_DIM)
            dact = _dot(dx_ref[...], wd_ref[cs, :], "nt")
            a, xa = _conv_taps(ext_ref, 0, wa_ref, ba_ref, cs)
            b, xb = _conv_taps(ext_ref, 1, wb_ref, bb_ref, cs)
            cdf, gelu = _gelu_parts(a)
            dgelu = cdf + a * (jnp.exp(-0.5 * a * a) * (1.0 / math.sqrt(2.0 * math.pi)))
            da = dact * b * dgelu
            db = dact * gelu
            dup_ref[0, :, cs] = da.astype(BF16)
            dup_ref[1, :, cs] = db.astype(BF16)
            for dval, xs, dw_ref, dbias_ref in ((da, xa, dwa_ref, dba_ref), (db, xb, dwb_ref, dbb_ref)):
                dw_ref[:, cs] += jnp.concatenate([jnp.sum(dval * xk, axis=0, keepdims=True) for xk in xs], axis=0)
                dbias_ref[:, cs] += jnp.sum(dval, axis=0, keepdims=True)

    dx_spec = pl.BlockSpec((T, D_MODEL), lambda j, i: (i, 0))
    wd_spec = pl.BlockSpec((C, D_MODEL), lambda j, i: (j, 0))
    dw_spec = pl.BlockSpec((3, C), lambda j, i: (0, j))
    db_spec = pl.BlockSpec((1, C), lambda j, i: (0, j))
    res = pl.pallas_call(
        body, grid=(nj, S // T),
        in_specs=[swap(blk), swap(halo), dx_spec, wd_spec, swap(w_a), swap(w_b), swap(b_a), swap(b_b)],
        out_specs=[swap(blk), dw_spec, dw_spec, db_spec, db_spec],
        out_shape=[jax.ShapeDtypeStruct((2, S, D_FF), BF16), jax.ShapeDtypeStruct((3, D_FF), F32),
                   jax.ShapeDtypeStruct((3, D_FF), F32), jax.ShapeDtypeStruct((1, D_FF), F32),
                   jax.ShapeDtypeStruct((1, D_FF), F32)],
        scratch_shapes=[pltpu.VMEM((2, HALO + T, C), F32)],
        compiler_params=_params("parallel", "arbitrary"), name="convffn_bwd_a")(
            up_pre, up_pre, dx, w_down, conv_w, conv_w, conv_b, conv_b)
    dup, dwa, dwb, dba, dbb = res
    dconv_w = jnp.concatenate([dwa, dwb], axis=1)
    dconv_b = jnp.concatenate([dba, dbb], axis=1)
    return dup, dconv_w, dconv_b


def _convffn_bwd_b(dup, conv_w, w_up_t):
    S = dup.shape[1]
    tm, tk = 1024, D_FF // 4
    per_half = D_FF // tk
    nk = 2 * per_half
    per = tm // HALO
    n_blk = S // tm

    def body(d_ref, halo_ref, cw_ref, w_ref, dpre_ref, dh_ref, ext_ref, acc_ref):
        i, k = pl.program_id(0), pl.program_id(1)
        n = tm + HALO
        ext_ref[0:tm, :] = d_ref[...].astype(F32)
        ext_ref[tm:, :] = jnp.where(i == n_blk - 1, 0.0, halo_ref[...].astype(F32))
        ext = ext_ref[...]
        y = cw_ref[2:3, :] * ext[0:tm] + cw_ref[1:2, :] * pltpu.roll(ext, n - 1, 0)[0:tm]
        y = y + cw_ref[0:1, :] * pltpu.roll(ext, n - 2, 0)[0:tm]
        pre = y.astype(BF16)
        dpre_ref[...] = pre
        prod = _dot(pre, w_ref[...], "nn")

        @pl.when(k == 0)
        def _():
            acc_ref[...] = prod

        @pl.when(k > 0)
        def _():
            acc_ref[...] += prod

        @pl.when(k == nk - 1)
        def _():
            dh_ref[...] = acc_ref[...].astype(BF16)

    blk = pl.BlockSpec((None, tm, tk), lambda i, k: (k // per_half, i, k % per_half))
    halo = pl.BlockSpec((None, HALO, tk),
                        lambda i, k: (k // per_half, jnp.minimum((i + 1) * per, S // HALO - 1), k % per_half))
    return pl.pallas_call(
        body, grid=(n_blk, nk),
        in_specs=[blk, halo, pl.BlockSpec((3, tk), lambda i, k: (0, k)), pl.BlockSpec((tk, D_MODEL), lambda i, k: (k, 0))],
        out_specs=[blk, pl.BlockSpec((tm, D_MODEL), lambda i, k: (i, 0))],
        out_shape=[jax.ShapeDtypeStruct((2, S, D_FF), BF16), jax.ShapeDtypeStruct((S, D_MODEL), BF16)],
        scratch_shapes=[pltpu.VMEM((tm + HALO, tk), F32), pltpu.VMEM((tm, D_MODEL), F32)],
        compiler_params=_params("parallel", "arbitrary"), name="convffn_bwd_b")(dup, dup, conv_w, w_up_t)


def _position():
    return lax.axis_index("x"), lax.axis_index("y"), lax.axis_index("c")


def _flip(pos, k):
    x, y, c = pos
    return (1 - x if k & 4 else x, 1 - y if k & 2 else y, 1 - c if k & 1 else c)


def _index(pos):
    return 4 * pos[0] + 2 * pos[1] + pos[2]


_HBM = pl.BlockSpec(memory_space=pltpu.HBM)
_SEM = pl.BlockSpec(memory_space=pltpu.SEMAPHORE)
_ANY = pl.BlockSpec(memory_space=pl.ANY)
_EFFECT = pltpu.SideEffectType.DATAFLOW_SIDE_EFFECTING


def _in_hbm(a):
    return pltpu.with_memory_space_constraint(a, pltpu.HBM)


def _split_copies(name, srcs, lands, n_copies, plan, start_after, wait_after):
    ns, nl = len(srcs), len(lands)
    start_after = tuple(start_after)
    na = len(start_after)

    def start_body(*refs):
        src_refs, land_refs = refs[:ns], refs[ns:ns + nl]
        send_sems, recv_sems = refs[ns + nl + na], refs[ns + nl + na + 1]
        token = refs[-1]
        for k, (src, dst, _, peer) in enumerate(plan(src_refs, land_refs)):
            pltpu.make_async_remote_copy(src_ref=src, dst_ref=dst, send_sem=send_sems.at[k], recv_sem=recv_sems.at[k],
                                         device_id=peer, device_id_type=MESH).start()
        token[...] = jnp.zeros_like(token)

    thru = [pltpu.HBM(a.shape, a.dtype) for a in list(srcs) + list(lands)]
    res = pl.pallas_call(
        start_body, name=name + "_start",
        out_shape=[pltpu.SemaphoreType.DMA((n_copies,)), pltpu.SemaphoreType.DMA((n_copies,))] + thru
        + [jax.ShapeDtypeStruct((8, 128), F32)],
        in_specs=[_HBM] * (ns + nl) + [_ANY] * na,
        out_specs=[_SEM, _SEM] + [_HBM] * (ns + nl) + [pl.BlockSpec(memory_space=pltpu.VMEM)],
        input_output_aliases={i: 2 + i for i in range(ns + nl)},
        compiler_params=pltpu.CompilerParams(has_side_effects=_EFFECT),
    )(*[_in_hbm(a) for a in list(srcs) + list(lands)], *start_after)
    send_sems, recv_sems, token = res[0], res[1], res[-1]
    thru_vals = res[2:-1]

    def wait():
        def wait_body(*refs):
            src_refs, land_refs = refs[:ns], refs[ns:ns + nl]
            s_sems, r_sems = refs[ns + nl], refs[ns + nl + 1]
            for k, (src, _, dst, peer) in enumerate(plan(src_refs, land_refs)):
                cp = pltpu.make_async_remote_copy(src_ref=src, dst_ref=dst, send_sem=s_sems.at[k], recv_sem=r_sems.at[k],
                                                  device_id=peer, device_id_type=MESH)
                cp.wait_send()
                cp.wait_recv()

        out = pl.pallas_call(
            wait_body, name=name + "_wait", out_shape=thru,
            in_specs=[_HBM] * (ns + nl) + [_SEM, _SEM, _ANY], out_specs=[_HBM] * (ns + nl),
            input_output_aliases={i: i for i in range(ns + nl)},
            compiler_params=pltpu.CompilerParams(has_side_effects=_EFFECT),
        )(*thru_vals, send_sems, recv_sems, wait_after())
        return list(out[:ns]), list(out[ns:])

    return wait, token


def _gather_level1_plan(n):
    def plan(src_refs, land_refs):
        me = _position()
        x, y, c = me
        peers = [(x, y, 1 - c), (1 - x, y, c), (x, 1 - y, c), (1 - x, 1 - y, c)]
        return [(src_refs[a], land_refs[a].at[_index(me)], land_refs[a].at[_index(p)], p)
                for a in range(n) for p in peers]
    return plan


def _gather_level2(lands):
    n = len(lands)

    def body(*refs):
        lin = refs[:n]
        send_sems, recv_sems = refs[2 * n:]
        x, y, c = _position()
        sibling = (x, y, 1 - c)
        chips = [(1 - x, y), (x, 1 - y), (1 - x, 1 - y)]

        def copy(a, j, core):
            slot = lin[a].at[_index((*chips[j], core))]
            return pltpu.make_async_remote_copy(
                src_ref=slot, dst_ref=slot, send_sem=send_sems.at[3 * a + j], recv_sem=recv_sems.at[3 * a + j],
                device_id=sibling, device_id_type=MESH)

        sent = [copy(a, j, c) for a in range(n) for j in range(3)]
        for cp in sent:
            cp.start()
        for a in range(n):
            for j in range(3):
                copy(a, j, 1 - c).wait_recv()
        for cp in sent:
            cp.wait_send()

    return pl.pallas_call(
        body, out_shape=[jax.ShapeDtypeStruct(l.shape, l.dtype) for l in lands],
        in_specs=[_ANY] * n, out_specs=[_ANY] * n, input_output_aliases={a: a for a in range(n)},
        scratch_shapes=[pltpu.SemaphoreType.DMA((3 * n,)), pltpu.SemaphoreType.DMA((3 * n,))],
        name="gather_level2")(*lands)


def _gather_direct_plan(n):
    def plan(src_refs, land_refs):
        me = _position()
        peers = [_flip(me, k) for k in range(1, N_DEV)]
        return [(src_refs[a], land_refs[a].at[_index(me)], land_refs[a].at[_index(p)], p)
                for a in range(n) for p in peers]
    return plan


def _place_own(shards):
    me = _index(_position())
    return [lax.dynamic_update_slice(lax.empty((N_DEV,) + s.shape, s.dtype), s[None], (me, 0, 0)) for s in shards]


def _exchange_direct_plan(n):
    def plan(src_refs, land_refs):
        me = _position()
        peers = [_flip(me, k) for k in range(1, N_DEV)]
        return [(src_refs[a].at[_index(p)], land_refs[a].at[_index(me)], land_refs[a].at[_index(p)], p)
                for a in range(n) for p in peers]
    return plan


def _adamw(w, g, m, v):
    m = ADAM_B1 * m + (1.0 - ADAM_B1) * g
    v = ADAM_B2 * v + (1.0 - ADAM_B2) * (g * g)
    m_hat = m / (1.0 - ADAM_B1 ** ADAM_STEP)
    v_hat = v / (1.0 - ADAM_B2 ** ADAM_STEP)
    delta = -ADAM_LR * (m_hat / (jnp.sqrt(v_hat) + ADAM_EPS) + ADAM_WD * w)
    return delta, m, v


def _row_tile(rows, unit, cap=256):
    best = rows
    for t in range(unit, min(rows, cap) + 1, unit):
        if rows % t == 0:
            best = t
    return best


def _sum_partials(name, recv):
    n_src, R, C = recv.shape
    T = _row_tile(R, 16)

    def body(r_ref, g_out):
        g = r_ref[0].astype(F32)
        for s in range(1, n_src):
            g = g + r_ref[s].astype(F32)
        g_out[...] = g

    return pl.pallas_call(
        body, grid=(R // T,), in_specs=[pl.BlockSpec((n_src, T, C), lambda i: (0, i, 0))],
        out_specs=pl.BlockSpec((T, C), lambda i: (i, 0)), out_shape=jax.ShapeDtypeStruct((R, C), F32),
        compiler_params=_params("parallel"), name=name)(recv)


def _adamw_update(name, w, g, m, v):
    R, C = w.shape
    T = _row_tile(R, 8)

    def body(w_ref, g_ref, m_ref, v_ref, d_out, m_out, v_out):
        delta, mn, vn = _adamw(w_ref[...], g_ref[...], m_ref[...], v_ref[...])
        d_out[...] = delta
        m_out[...] = mn
        v_out[...] = vn

    blk = pl.BlockSpec((T, C), lambda i: (i, 0))
    return pl.pallas_call(
        body, grid=(R // T,), in_specs=[blk] * 4, out_specs=[blk] * 3,
        out_shape=[jax.ShapeDtypeStruct((R, C), F32)] * 3, compiler_params=_params("parallel"), name=name)(w, g, m, v)


def _gather_vmem(p_ref, gath, send_sems, recv_sems):
    me = _position()
    gath[_index(me)] = p_ref[...]
    copies = []
    for k in range(1, N_DEV):
        peer = _flip(me, k)
        copies.append(pltpu.make_async_remote_copy(
            src_ref=p_ref, dst_ref=gath.at[_index(me)], send_sem=send_sems.at[k - 1],
            recv_sem=recv_sems.at[k - 1], device_id=peer, device_id_type=MESH))
    for cp in copies:
        cp.start()
    for k in range(1, N_DEV):
        peer = _flip(me, k)
        pltpu.make_async_remote_copy(
            src_ref=p_ref, dst_ref=gath.at[_index(peer)], send_sem=send_sems.at[k - 1],
            recv_sem=recv_sems.at[k - 1], device_id=peer, device_id_type=MESH).wait_recv()
    for cp in copies:
        cp.wait_send()


def _small_allreduce_adamw(part, w, m, v):
    R, C = part.shape

    def body(p_ref, w_ref, m_ref, v_ref, g_out, d_out, m_out, v_out, gath, send_sems, recv_sems):
        _gather_vmem(p_ref, gath, send_sems, recv_sems)
        g = gath[0]
        for s in range(1, N_DEV):
            g = g + gath[s]
        delta, mn, vn = _adamw(w_ref[...], g, m_ref[...], v_ref[...])
        g_out[...] = g
        d_out[...] = delta
        m_out[...] = mn
        v_out[...] = vn

    vm = pl.BlockSpec(memory_space=pltpu.VMEM)
    return pl.pallas_call(
        body, in_specs=[vm] * 4, out_specs=[vm] * 4, out_shape=[jax.ShapeDtypeStruct((R, C), F32)] * 4,
        scratch_shapes=[pltpu.VMEM((N_DEV, R, C), F32), pltpu.SemaphoreType.DMA((7,)), pltpu.SemaphoreType.DMA((7,))],
        name="small_allreduce_adamw")(part, w, m, v)


_TRANSPOSED = ("w_in", "w_up", "w_pool_out", "w_attn_out")
_SMALL = (("g_mix", 2048), ("b_gate", 4096), ("pool_scale", 1024), ("g_ffn", 2048), ("conv_b", 11264), ("g_final", 2048))
CONV_ROWS = 8


def _to_comm(name, shard):
    return shard.T if name in _TRANSPOSED else shard


def _pack_small(vals):
    flat = [vals[n].reshape(-1) for n, _ in _SMALL]
    used = sum(f.shape[0] for f in flat)
    flat.append(jnp.zeros((SMALL_ROWS * PACK_COLS - used,), F32))
    return jnp.concatenate(flat).reshape(SMALL_ROWS, PACK_COLS)


def _unpack_small(packed, like):
    flat = packed.reshape(-1)
    out, off = {}, 0
    for name, n in _SMALL:
        out[name] = flat[off:off + n].reshape(like[name].shape)
        off += n
    return out


def _local_step(x, target, W, small, first_token=0.0, first_weights=None, late_weights=None, emit=None):
    S = x.shape[0]
    o_qkv, o_gate = POOL_WIDTH, POOL_WIDTH + QKV_WIDTH
    if emit is None:
        emit = lambda grads: None

    def zero(token):
        return 0.0 if token is None else token[0, 0]

    h1 = _rms_fwd("rms1_fwd", x, small["g_mix"] + first_token)
    tok = None
    if first_weights is not None:
        w_first, tok = first_weights(h1)
        small = dict(small, conv_w_full=w_first.pop("conv_w_full"))
        W = dict(W, **w_first)
    w_in_t = W["w_in"]
    u = _mm_nt("proj_u", h1, w_in_t, tm=2048, tn=512, tk=2048, o_dtype=F32, n_rows=POOL_WIDTH, b_row_off=0,
               after=tok)
    qkv = _mm_nt("proj_qkv", h1, w_in_t, tm=2048, tn=512, tk=2048, o_dtype=BF16, n_rows=QKV_WIDTH, b_row_off=o_qkv,
                 after=tok)
    gates_pre = _mm_nt("proj_gates", h1, w_in_t, tm=2048, tn=512, tk=2048, o_dtype=BF16, n_rows=2 * D_MODEL,
                       b_row_off=o_gate, after=tok)
    pool_y = _pool_fwd(u, W["w_pool_lin"], small["pool_scale"])
    outs, lses = zip(*[_attn_fwd(gi, qkv) for gi in range(3)])
    attn_y = _merge_fwd(outs, lses)
    if late_weights is not None:
        W = dict(W, **late_weights(0, attn_y))
    w_up_t = W["w_up"]
    mixed, y_pool, y_attn = _mix_fwd(pool_y, attn_y, W["w_pool_out"], W["w_attn_out"], gates_pre, small["b_gate"])
    x1, h2 = _out_proj_norm(mixed, W["w_out"], x, small["g_ffn"])

    nj = D_FF // 512
    up_pre, act = _up_proj_convffn(h2, w_up_t, small["conv_w_full"], small["conv_b"])
    if late_weights is not None:
        W = dict(W, **late_weights(1, act))
    x2 = _mm_nn("down_proj", act, W["w_down"], tm=1024, tn=1024, tk=D_FF // 2, o_dtype=F32, res=x1)

    dx2, dx2_b, dg_final, sq_cols = _final_loss("final_loss", x2, small["g_final"], target)

    dw_down = _mm_tn("dw_down", act, dx2_b, tm=512, tn=2048, tk=TOKEN_TK)
    tok = emit(dict(w_down=dw_down))
    dup, dconv_w, dconv_b = _convffn_bwd_a(up_pre, dx2_b, W["w_down"], small["conv_w_full"],
                                           small["conv_b"] + zero(tok))
    dup_pre, dh2 = _convffn_bwd_b(dup, small["conv_w_full"], w_up_t)
    dw_up = _mm("dw_up", dup_pre, h2, mode="tn", grid=(2 * nj, 1, S // TOKEN_TK),
                a_spec=pl.BlockSpec((None, TOKEN_TK, 512), lambda i, j, k: (i // nj, k, i % nj)),
                b_spec=pl.BlockSpec((TOKEN_TK, D_MODEL), lambda i, j, k: (k, 0)),
                o_spec=pl.BlockSpec((512, D_MODEL), lambda i, j, k: (i, 0)),
                o_shape=(2 * D_FF, D_MODEL), o_dtype=BF16, acc_shape=(512, D_MODEL))
    tok = emit(dict(w_up=dw_up, conv_w=dconv_w))
    dx1, dx1_b, dg_ffn = _rms_bwd("rms2_bwd", x1, small["g_ffn"] + zero(tok), dh2, dx2, with_bf16=True)

    dmixed = _mm_nt("d_mixed", dx1_b, W["w_out"], tm=1024, tn=1024, tk=2048, o_dtype=BF16)
    dw_out = _mm_tn("dw_out", mixed, dx1_b, tm=512, tn=2048, tk=TOKEN_TK)
    tok = emit(dict(w_out=dw_out))
    dy_pool, dy_attn, dgates, db_gate = _mix_bwd(dmixed, gates_pre, small["b_gate"] + zero(tok), y_pool, y_attn)
    dpool_y = _mm_nn("d_pool_y", dy_pool, W["w_pool_out"], tm=1024, tn=1024, tk=2048, o_dtype=F32)
    dw_pool_out = _mm_tn("dw_pool_out", dy_pool, pool_y, tm=1024, tn=1024, tk=TOKEN_TK)
    dattn_y = _mm_nn("d_attn_y", dy_attn, W["w_attn_out"], tm=1024, tn=512, tk=2048, o_dtype=F32)
    dw_attn_out = _mm_tn("dw_attn_out", dy_attn, attn_y, tm=1024, tn=512, tk=TOKEN_TK)
    e, dw_pool_lin, dpool_scale = _pool_bwd_a(u, dpool_y, W["w_pool_lin"], small["pool_scale"])
    du = _pool_bwd_b(e)
    dos, dps = _merge_bwd(dattn_y, outs, lses)
    dqkv = [_attn_bwd(gi, qkv, dos[gi], lses[gi], dps[gi]) for gi in range(3)]
    dproj = jnp.concatenate([du] + [dqkv[gi][sec] for sec in range(3) for gi in range(3)] + [dgates], axis=1)
    dw_in = _mm_tn("dw_in", dproj, h1, tm=512, tn=2048, tk=TOKEN_TK)
    tok = emit(dict(w_in=dw_in, w_pool_out=dw_pool_out, w_attn_out=dw_attn_out, w_pool_lin=dw_pool_lin))
    dh1 = _mm_nn("d_h1", dproj, w_in_t, tm=1024, tn=1024, tk=IN_WIDTH // 4, o_dtype=BF16,
                 after=tok)
    grad_x, dg_mix = _rms_bwd("rms1_bwd", x, small["g_mix"] + zero(tok), dh1, dx1, with_bf16=False)

    grads = dict(w_in=dw_in, w_up=dw_up, w_down=dw_down, w_out=dw_out, w_pool_out=dw_pool_out,
                 w_attn_out=dw_attn_out, w_pool_lin=dw_pool_lin, conv_w=dconv_w)
    small_grads = dict(g_mix=dg_mix, b_gate=db_gate, pool_scale=dpool_scale, g_ffn=dg_ffn, conv_b=dconv_b,
                       g_final=dg_final)
    return sq_cols, grad_x, grads, small_grads


def kernel(x, g_mix, w_in, b_gate, w_pool_lin, pool_scale, w_pool_out, w_attn_out, w_out, g_ffn, w_up, conv_w, conv_b, w_down, g_final, loss_target, m_g_mix, m_w_in, m_b_gate, m_w_pool_lin, m_pool_scale, m_w_pool_out, m_w_attn_out, m_w_out, m_g_ffn, m_w_up, m_conv_w, m_conv_b, m_w_down, m_g_final, v_g_mix, v_w_in, v_b_gate, v_w_pool_lin, v_pool_scale, v_w_pool_out, v_w_attn_out, v_w_out, v_g_ffn, v_w_up, v_conv_w, v_conv_b, v_w_down, v_g_final):
    given = dict(g_mix=g_mix, w_in=w_in, b_gate=b_gate, w_pool_lin=w_pool_lin, pool_scale=pool_scale,
                 w_pool_out=w_pool_out, w_attn_out=w_attn_out, w_out=w_out, g_ffn=g_ffn, w_up=w_up, conv_w=conv_w,
                 conv_b=conv_b, w_down=w_down, g_final=g_final)
    mom_m = dict(g_mix=m_g_mix, w_in=m_w_in, b_gate=m_b_gate, w_pool_lin=m_w_pool_lin, pool_scale=m_pool_scale,
                 w_pool_out=m_w_pool_out, w_attn_out=m_w_attn_out, w_out=m_w_out, g_ffn=m_g_ffn, w_up=m_w_up,
                 conv_w=m_conv_w, conv_b=m_conv_b, w_down=m_w_down, g_final=m_g_final)
    mom_v = dict(g_mix=v_g_mix, w_in=v_w_in, b_gate=v_b_gate, w_pool_lin=v_w_pool_lin, pool_scale=v_pool_scale,
                 w_pool_out=v_w_pool_out, w_attn_out=v_w_attn_out, w_out=v_w_out, g_ffn=v_g_ffn, w_up=v_w_up,
                 conv_w=v_conv_w, conv_b=v_conv_b, w_down=v_w_down, g_final=v_g_final)
    def full(g):
        return g.reshape(N_DEV * g.shape[1], g.shape[2])

    shard = dict(w_in=_to_comm("w_in", w_in[0]).astype(BF16))
    W = {}
    first = [shard["w_in"], w_pool_lin[0].reshape(4 * 32, POOL_GROUP_WIDTH),
             jnp.pad(conv_w[0], ((0, CONV_ROWS - 3), (0, 0)))]
    hold = {}
    w_in_wait, first_token = _split_copies(
        "gather_w_in", first, _place_own(first), 4 * len(first), _gather_level1_plan(len(first)),
        start_after=(), wait_after=lambda: hold["h1"])

    stages = (("w_out", "w_pool_out", "w_attn_out", "w_up"), ("w_down",))
    stage_after = [None, None]
    stage_wait = []
    for n in (n for names in stages for n in names):
        shard[n] = _to_comm(n, given[n][0] + first_token[0, 0]).astype(BF16)

    def first_weights(h1):
        hold["h1"] = h1
        w_in_all, lin_all, conv_w_all = _gather_level2(w_in_wait()[1])
        w_in_full = full(w_in_all)
        lin_full = jnp.moveaxis(lin_all.reshape(N_DEV, 4, 32, POOL_GROUP_WIDTH), 0, 1).reshape(
            4, POOL_GROUP_WIDTH, POOL_GROUP_WIDTH).astype(BF16)
        conv_w_full = jnp.moveaxis(conv_w_all[:, :3, :], 0, 1).reshape(3, 2 * D_FF)
        order_after = (w_in_full,)
        for k, names in enumerate(stages):
            srcs = [shard[n] for n in names]
            wait, token = _split_copies(f"gather_stage{k}", srcs, _place_own(srcs), 7 * len(names),
                                        _gather_direct_plan(len(names)), start_after=order_after,
                                        wait_after=lambda k=k: stage_after[k])
            stage_wait.append(wait)
            order_after = (token,)
        return dict(w_in=w_in_full, w_pool_lin=lin_full, conv_w_full=conv_w_full), token

    def late_weights(stage, after):
        stage_after[stage] = after
        return {n: full(l) for n, l in zip(stages[stage], stage_wait[stage]()[1])}

    small = dict(g_mix=g_mix, b_gate=b_gate, pool_scale=pool_scale, g_ffn=g_ffn, conv_b=conv_b,
                 g_final=g_final.reshape(1, D_MODEL))

    pending = []
    done = {}

    def emit(group):
        names = list(group)
        parts = []
        for n in names:
            g = group[n]
            if n == "w_pool_lin":
                g = jnp.moveaxis(g.reshape(4, N_DEV, 32, POOL_GROUP_WIDTH), 1, 0).reshape(
                    N_DEV, 4 * 32, POOL_GROUP_WIDTH).astype(BF16)
            elif n == "conv_w":
                g = jnp.pad(jnp.moveaxis(g.reshape(3, N_DEV, 2 * D_FF // N_DEV), 1, 0),
                            ((0, 0), (0, CONV_ROWS - 3), (0, 0))).astype(BF16)
            else:
                g = g.reshape(N_DEV, g.shape[0] // N_DEV, g.shape[1])
            parts.append(g)
        me = _index(_position())
        lands = [lax.dynamic_update_slice(lax.empty(p.shape, p.dtype), lax.dynamic_index_in_dim(p, me, 0),
                                          (me, 0, 0)) for p in parts]
        wait, token = _split_copies(
            "exchange_" + names[0], parts, lands, 7 * len(names), _exchange_direct_plan(len(names)),
            start_after=(), wait_after=lambda: done["grad_x"])
        pending.append((names, wait))
        return token

    prepared = sum(shard[n][0, 0].astype(F32) for names in stages for n in names) * 0.0
    sq_cols, grad_x, _, small_grads = _local_step(x[0], loss_target[0], W, small,
                                                  first_token=first_token[0, 0] + prepared,
                                                  first_weights=first_weights, late_weights=late_weights, emit=emit)
    done["grad_x"] = grad_x
    loss = lax.psum(0.5 / D_MODEL * jnp.sum(sq_cols), AXES)

    res = {}
    for n, r in [(n, r) for names, wait in pending for n, r in zip(names, wait()[1])]:
        g = _sum_partials("sum_" + n, r)
        if n in _TRANSPOSED:
            g = g.T
        elif n == "conv_w":
            g = g[:3]
        shape = given[n].shape
        rows = math.prod(shape[:-1])
        w2, m2, v2 = (d[n].reshape(rows, shape[-1]) for d in (given, mom_m, mom_v))
        upd = _adamw_update("adamw_" + n, w2, g.reshape(rows, shape[-1]), m2, v2)
        res[n] = [a.reshape(shape) for a in (g,) + tuple(upd)]

    packed_small = [_pack_small(d) for d in (given, mom_m, mom_v)]
    res_small = [_unpack_small(p, given) for p in _small_allreduce_adamw(_pack_small(small_grads), *packed_small)]

    order = ["g_mix", "w_in", "b_gate", "w_pool_lin", "pool_scale", "w_pool_out", "w_attn_out", "w_out", "g_ffn",
             "w_up", "conv_w", "conv_b", "w_down", "g_final"]
    outs = [loss, grad_x[None]]
    for kind in range(4):
        for n in order:
            outs.append(res[n][kind] if n in res else res_small[kind][n])
    return tuple(outs)
```

```python
import math

import jax
import jax.numpy as jnp
from jax import lax
from jax.experimental import pallas as pl
from jax.experimental.pallas import tpu as pltpu

F32 = jnp.float32
BF16 = jnp.bfloat16
MESH = pl.DeviceIdType.MESH
AXES = ("x", "y", "c")
N_DEV = 8

D_MODEL = 2048
POOL_WINDOWS = (2, 4, 8, 16)
POOL_GROUP_WIDTH = 256
POOL_WIDTH = 1024
ATTN_GROUPS = ((128, 1), (512, 4), (2048, 16))
SPAN = 128
HEADS_PER_GROUP = 4
N_ATTN_HEADS = 12
HEAD_DIM = 128
ATTN_WIDTH = 1536
QKV_WIDTH = 3 * ATTN_WIDTH
ATTN_OUT_WIDTH = 512
IN_WIDTH = 9728
D_FF = 5632
RMS_EPS = 1e-6
ADAM_LR, ADAM_B1, ADAM_B2, ADAM_EPS, ADAM_WD, ADAM_STEP = 0.001, 0.9, 0.999, 1e-08, 0.01, 10

VMEM_LIMIT_BYTES = 48 * 1024 * 1024
HALO = 16
TOKEN_TK = 2048
MXU_DIM = 256
PACK_COLS = 1024
SMALL_ROWS = 24
NEG_BIG = -1e30

_DN = {"nn": (((1,), (0,)), ((), ())), "nt": (((1,), (1,)), ((), ())), "tn": (((0,), (0,)), ((), ()))}


def _params(*sem):
    return pltpu.CompilerParams(dimension_semantics=sem, vmem_limit_bytes=VMEM_LIMIT_BYTES)


def _dot(a, b, mode):
    return lax.dot_general(a.astype(BF16), b.astype(BF16), _DN[mode], preferred_element_type=F32)


def _mm(name, a, b, *, mode, grid, a_spec, b_spec, o_spec, o_shape, o_dtype, acc_shape, res=None, res_spec=None,
        after=None):
    nk = grid[2]
    n_in = 2 + (res is not None) + (after is not None)

    def body(*refs):
        a_ref, b_ref = refs[:2]
        r_ref = refs[2] if res is not None else None
        o_ref = refs[n_in]
        scr = refs[n_in + 1:]
        prod = _dot(a_ref[...], b_ref[...], mode)

        def finish(val):
            if r_ref is not None:
                val = val + r_ref[...]
            o_ref[...] = val.astype(o_ref.dtype)

        if nk == 1:
            finish(prod)
        else:
            acc = scr[0]
            k = pl.program_id(2)

            @pl.when(k == 0)
            def _():
                acc[...] = prod

            @pl.when(k > 0)
            def _():
                acc[...] += prod

            @pl.when(k == nk - 1)
            def _():
                finish(acc[...])

    in_specs = [a_spec, b_spec]
    args = [a, b]
    if res is not None:
        in_specs.append(res_spec)
        args.append(res)
    if after is not None:
        in_specs.append(pl.BlockSpec((8, 128), lambda i, j, k: (0, 0)))
        args.append(after)
    return pl.pallas_call(
        body, grid=grid, in_specs=in_specs, out_specs=o_spec,
        out_shape=jax.ShapeDtypeStruct(o_shape, o_dtype),
        scratch_shapes=[] if nk == 1 else [pltpu.VMEM(acc_shape, F32)],
        compiler_params=_params("parallel", "parallel", "arbitrary"), name=name)(*args)


def _mm_nn(name, a, b, *, tm, tn, tk, o_dtype, res=None, after=None):
    M, K = a.shape
    N = b.shape[1]
    return _mm(name, a, b, mode="nn", grid=(M // tm, N // tn, K // tk),
               a_spec=pl.BlockSpec((tm, tk), lambda i, j, k: (i, k)),
               b_spec=pl.BlockSpec((tk, tn), lambda i, j, k: (k, j)),
               o_spec=pl.BlockSpec((tm, tn), lambda i, j, k: (i, j)),
               o_shape=(M, N), o_dtype=o_dtype, acc_shape=(tm, tn), res=res,
               res_spec=pl.BlockSpec((tm, tn), lambda i, j, k: (i, j)), after=after)


def _mm_nt(name, a, b, *, tm, tn, tk, o_dtype, n_rows=None, b_row_off=0, after=None):
    M, K = a.shape
    N = b.shape[0] if n_rows is None else n_rows
    off = b_row_off // tn
    return _mm(name, a, b, mode="nt", grid=(M // tm, N // tn, K // tk),
               a_spec=pl.BlockSpec((tm, tk), lambda i, j, k: (i, k)),
               b_spec=pl.BlockSpec((tn, tk), lambda i, j, k: (j + off, k)),
               o_spec=pl.BlockSpec((tm, tn), lambda i, j, k: (i, j)),
               o_shape=(M, N), o_dtype=o_dtype, acc_shape=(tm, tn), after=after)


def _mm_tn(name, a, b, *, tm, tn, tk):
    K, M = a.shape
    N = b.shape[1]
    return _mm(name, a, b, mode="tn", grid=(M // tm, N // tn, K // tk),
               a_spec=pl.BlockSpec((tk, tm), lambda i, j, k: (k, i)),
               b_spec=pl.BlockSpec((tk, tn), lambda i, j, k: (k, j)),
               o_spec=pl.BlockSpec((tm, tn), lambda i, j, k: (i, j)),
               o_shape=(M, N), o_dtype=BF16, acc_shape=(tm, tn))


def _rms_fwd(name, x, g):
    S, D = x.shape
    T = 512

    def body(x_ref, g_ref, h_ref):
        xv = x_ref[...]
        r = lax.rsqrt(jnp.mean(xv * xv, axis=-1, keepdims=True) + RMS_EPS)
        h_ref[...] = (xv * r * g_ref[...]).astype(BF16)

    return pl.pallas_call(
        body, grid=(S // T,),
        in_specs=[pl.BlockSpec((T, D), lambda i: (i, 0)), pl.BlockSpec((1, D), lambda i: (0, 0))],
        out_specs=pl.BlockSpec((T, D), lambda i: (i, 0)),
        out_shape=jax.ShapeDtypeStruct((S, D), BF16), compiler_params=_params("parallel"), name=name)(x, g)


def _out_proj_norm(mixed, w_out, x, g):
    S, D = x.shape
    tm = 512

    def body(m_ref, w_ref, x_ref, g_ref, x1_ref, h_ref):
        x1 = x_ref[...] + _dot(m_ref[...], w_ref[...], "nn")
        x1_ref[...] = x1
        r = lax.rsqrt(jnp.mean(x1 * x1, axis=-1, keepdims=True) + RMS_EPS)
        h_ref[...] = (x1 * r * g_ref[...]).astype(BF16)

    row = pl.BlockSpec((tm, D), lambda i: (i, 0))
    return pl.pallas_call(
        body, grid=(S // tm,),
        in_specs=[row, pl.BlockSpec((D, D), lambda i: (0, 0)), row, pl.BlockSpec((1, D), lambda i: (0, 0))],
        out_specs=[row, row], out_shape=[jax.ShapeDtypeStruct((S, D), F32), jax.ShapeDtypeStruct((S, D), BF16)],
        compiler_params=_params("parallel"), name="out_proj_norm")(mixed, w_out, x, g)


def _rms_bwd(name, x, g, dh, dres, with_bf16):
    S, D = x.shape
    T = 256

    def body(x_ref, g_ref, dh_ref, dres_ref, dx_ref, *rest):
        dg_ref = rest[-1]
        i = pl.program_id(0)
        xv = x_ref[...]
        r = lax.rsqrt(jnp.mean(xv * xv, axis=-1, keepdims=True) + RMS_EPS)
        xhat = xv * r
        dhv = dh_ref[...].astype(F32)
        gdh = dhv * g_ref[...]
        c = jnp.mean(xhat * gdh, axis=-1, keepdims=True)
        dx = dres_ref[...] + r * (gdh - xhat * c)
        dx_ref[...] = dx
        if with_bf16:
            rest[0][...] = dx.astype(BF16)
        part = jnp.sum(dhv * xhat, axis=0, keepdims=True)

        @pl.when(i == 0)
        def _():
            dg_ref[...] = part

        @pl.when(i > 0)
        def _():
            dg_ref[...] += part

    row = pl.BlockSpec((T, D), lambda i: (i, 0))
    vec = pl.BlockSpec((1, D), lambda i: (0, 0))
    narrow = [jax.ShapeDtypeStruct((S, D), BF16)] if with_bf16 else []
    return pl.pallas_call(
        body, grid=(S // T,), in_specs=[row, vec, row, row], out_specs=[row] * (1 + len(narrow)) + [vec],
        out_shape=[jax.ShapeDtypeStruct((S, D), F32)] + narrow + [jax.ShapeDtypeStruct((1, D), F32)],
        compiler_params=_params("arbitrary"), name=name)(x, g, dh, dres)


def _final_loss(name, x, g, target):
    S, D = x.shape
    T = 256

    def body(x_ref, g_ref, t_ref, dx_ref, dxb_ref, dg_ref, sq_ref):
        i = pl.program_id(0)
        xv = x_ref[...]
        r = lax.rsqrt(jnp.mean(xv * xv, axis=-1, keepdims=True) + RMS_EPS)
        xhat = xv * r
        err = xhat * g_ref[...] - t_ref[...]
        dy = err * (1.0 / D)
        gdy = dy * g_ref[...]
        c = jnp.mean(xhat * gdy, axis=-1, keepdims=True)
        dx = r * (gdy - xhat * c)
        dx_ref[...] = dx
        dxb_ref[...] = dx.astype(BF16)
        dg_part = jnp.sum(dy * xhat, axis=0, keepdims=True)
        sq_part = jnp.sum(err * err, axis=0, keepdims=True)

        @pl.when(i == 0)
        def _():
            dg_ref[...] = dg_part
            sq_ref[...] = sq_part

        @pl.when(i > 0)
        def _():
            dg_ref[...] += dg_part
            sq_ref[...] += sq_part

    row = pl.BlockSpec((T, D), lambda i: (i, 0))
    vec = pl.BlockSpec((1, D), lambda i: (0, 0))
    return pl.pallas_call(
        body, grid=(S // T,), in_specs=[row, vec, row], out_specs=[row, row, vec, vec],
        out_shape=[jax.ShapeDtypeStruct((S, D), F32), jax.ShapeDtypeStruct((S, D), BF16),
                   jax.ShapeDtypeStruct((1, D), F32), jax.ShapeDtypeStruct((1, D), F32)],
        compiler_params=_params("arbitrary"), name=name)(x, g, target)


def _prev_halo_spec(T, C, col_map):
    per = T // HALO
    return pl.BlockSpec((HALO, C), lambda *g: (jnp.maximum(g[0] * per - 1, 0), col_map(*g)))


def _pool_window_sums(ext, T, i):
    t = i * T + lax.broadcasted_iota(jnp.int32, (T, 1), 0)
    outs = []
    cnts = []
    for gi, w in enumerate(POOL_WINDOWS):
        s = ext[:, gi * POOL_GROUP_WIDTH:(gi + 1) * POOL_GROUP_WIDTH]
        sh = 1
        while sh < w:
            s = s + pltpu.roll(s, sh, 0)
            sh *= 2
        cnt = jnp.minimum(t + 1, w).astype(F32)
        outs.append(s[HALO:] / cnt - ext[HALO:, gi * POOL_GROUP_WIDTH:(gi + 1) * POOL_GROUP_WIDTH])
        cnts.append(cnt)
    return outs, cnts


def _load_ext(ext_ref, halo_ref, blk_ref, first):
    ext_ref[0:HALO, :] = jnp.where(first, 0.0, halo_ref[...].astype(F32))
    ext_ref[HALO:, :] = blk_ref[...].astype(F32)


def _pool_fwd(u, w_lin, scale):
    S = u.shape[0]
    T = 512
    W = POOL_WIDTH

    def body(u_ref, halo_ref, wl_ref, sc_ref, y_ref, ext_ref):
        i = pl.program_id(0)
        _load_ext(ext_ref, halo_ref, u_ref, i == 0)
        pooled, _ = _pool_window_sums(ext_ref[...], T, i)
        for gi in range(4):
            z = _dot(pooled[gi], wl_ref[gi], "nn")
            cs = slice(gi * POOL_GROUP_WIDTH, (gi + 1) * POOL_GROUP_WIDTH)
            y_ref[:, cs] = (z * sc_ref[:, cs]).astype(BF16)

    return pl.pallas_call(
        body, grid=(S // T,),
        in_specs=[pl.BlockSpec((T, W), lambda i: (i, 0)), _prev_halo_spec(T, W, lambda i: 0),
                  pl.BlockSpec((4, 256, 256), lambda i: (0, 0, 0)), pl.BlockSpec((1, W), lambda i: (0, 0))],
        out_specs=pl.BlockSpec((T, W), lambda i: (i, 0)),
        out_shape=jax.ShapeDtypeStruct((S, W), BF16),
        scratch_shapes=[pltpu.VMEM((HALO + T, W), F32)],
        compiler_params=_params("parallel"), name="pool_fwd")(u, u, w_lin, scale)


def _pool_bwd_a(u, dy, w_lin, scale):
    S = u.shape[0]
    T = 512
    W = POOL_WIDTH

    def body(u_ref, halo_ref, dy_ref, wl_ref, sc_ref, e_ref, dwl_ref, dsc_ref, ext_ref):
        i = pl.program_id(0)
        _load_ext(ext_ref, halo_ref, u_ref, i == 0)
        pooled, cnts = _pool_window_sums(ext_ref[...], T, i)
        for gi in range(4):
            cs = slice(gi * POOL_GROUP_WIDTH, (gi + 1) * POOL_GROUP_WIDTH)
            z = _dot(pooled[gi], wl_ref[gi], "nn")
            dyg = dy_ref[:, cs]
            dsc = jnp.sum(dyg * z, axis=0, keepdims=True)
            dz = dyg * sc_ref[:, cs]
            dwl = _dot(pooled[gi], dz, "tn")
            dpooled = _dot(dz, wl_ref[gi], "nt")
            e_ref[:, cs] = dpooled / cnts[gi]

            @pl.when(i == 0)
            def _():
                dwl_ref[gi] = dwl
                dsc_ref[:, cs] = dsc

            @pl.when(i > 0)
            def _():
                dwl_ref[gi] += dwl
                dsc_ref[:, cs] += dsc

    return pl.pallas_call(
        body, grid=(S // T,),
        in_specs=[pl.BlockSpec((T, W), lambda i: (i, 0)), _prev_halo_spec(T, W, lambda i: 0),
                  pl.BlockSpec((T, W), lambda i: (i, 0)),
                  pl.BlockSpec((4, 256, 256), lambda i: (0, 0, 0)), pl.BlockSpec((1, W), lambda i: (0, 0))],
        out_specs=[pl.BlockSpec((T, W), lambda i: (i, 0)), pl.BlockSpec((4, 256, 256), lambda i: (0, 0, 0)),
                   pl.BlockSpec((1, W), lambda i: (0, 0))],
        out_shape=[jax.ShapeDtypeStruct((S, W), F32), jax.ShapeDtypeStruct((4, 256, 256), F32),
                   jax.ShapeDtypeStruct((1, W), F32)],
        scratch_shapes=[pltpu.VMEM((HALO + T, W), F32)],
        compiler_params=_params("arbitrary"), name="pool_bwd_a")(u, u, dy, w_lin, scale)


def _pool_bwd_b(e):
    S = e.shape[0]
    T = 512
    W = POOL_WIDTH
    per = T // HALO
    n_blk = S // T

    def body(e_ref, halo_ref, du_ref, ext_ref):
        i = pl.program_id(0)
        ext_ref[0:T, :] = e_ref[...]
        ext_ref[T:, :] = jnp.where(i == n_blk - 1, 0.0, halo_ref[...])
        ext = ext_ref[...]
        n = T + HALO
        t = i * T + lax.broadcasted_iota(jnp.int32, (T, 1), 0)
        for gi, w in enumerate(POOL_WINDOWS):
            cs = slice(gi * POOL_GROUP_WIDTH, (gi + 1) * POOL_GROUP_WIDTH)
            s = ext[:, cs]
            sh = 1
            while sh < w:
                s = s + pltpu.roll(s, n - sh, 0)
                sh *= 2
            cnt = jnp.minimum(t + 1, w).astype(F32)
            du_ref[:, cs] = (s[0:T] - ext[0:T, cs] * cnt).astype(BF16)

    return pl.pallas_call(
        body, grid=(n_blk,),
        in_specs=[pl.BlockSpec((T, W), lambda i: (i, 0)),
                  pl.BlockSpec((HALO, W), lambda i: (jnp.minimum((i + 1) * per, S // HALO - 1), 0))],
        out_specs=pl.BlockSpec((T, W), lambda i: (i, 0)),
        out_shape=jax.ShapeDtypeStruct((S, W), BF16),
        scratch_shapes=[pltpu.VMEM((T + HALO, W), F32)],
        compiler_params=_params("parallel"), name="pool_bwd_b")(e, e)


def _slope(head):
    return 2.0 ** (-8.0 * (head + 1) / N_ATTN_HEADS)


def _attn_masks(b):
    row = lax.broadcasted_iota(jnp.int32, (SPAN, SPAN), 0)
    col = lax.broadcasted_iota(jnp.int32, (SPAN, SPAN), 1)
    dist_cur = (row - col).astype(F32)
    valid_cur = col <= row
    valid_prev = jnp.logical_and(col >= row, b > 0)
    return dist_cur, valid_cur, valid_prev


ATTN_BATCH = 4


def _attn_tiling(gi, S):
    d = ATTN_GROUPS[gi][1]
    hp = HEADS_PER_GROUP if d < 16 else 1
    return d, SPAN * d, S // (SPAN * d), hp, hp * HEAD_DIM, ATTN_BATCH // hp


_BATCHED_DN = {"nt": (((2,), (2,)), ((0,), (0,))), "nn": (((2,), (1,)), ((0,), (0,))), "tn": (((1,), (1,)), ((0,), (0,)))}


def _bdot(a, b, mode):
    return lax.dot_general(a.astype(BF16), b.astype(BF16), _BATCHED_DN[mode], preferred_element_type=F32)


def _batch_rows(it, d, rb):
    return [pl.ds(it * rb + j, SPAN, stride=d) for j in range(rb)]


def _take(scr, hp, rows):
    return jnp.stack([scr[hi, rw, :] for hi in range(hp) for rw in rows])


def _put(scr, hp, rows, val):
    for n, (hi, rw) in enumerate((hi, rw) for hi in range(hp) for rw in rows):
        scr[hi, rw, :] = val[n]


def _batch_bias(gi, d, hp, rb, dist):
    return jnp.stack([(-(_head_slope(gi, hp, hi) * d)) * dist for hi in range(hp) for _ in range(rb)])


def _head_slope(gi, hp, hi):
    if hp == HEADS_PER_GROUP:
        return _slope(gi * HEADS_PER_GROUP + hi)
    first = pl.program_id(1) * hp
    sl = jnp.float32(_slope(gi * HEADS_PER_GROUP + hi))
    for h0 in range(hp, HEADS_PER_GROUP, hp):
        sl = jnp.where(first == h0, jnp.float32(_slope(gi * HEADS_PER_GROUP + h0 + hi)), sl)
    return sl


def _split_heads(src_ref, dst_ref, hp):
    for hi in range(hp):
        dst_ref[hi] = src_ref[:, hi * HEAD_DIM:(hi + 1) * HEAD_DIM].astype(F32)


def _attn_fwd(gi, qkv):
    S = qkv.shape[0]
    d, C, nb, hp, W, rb = _attn_tiling(gi, S)
    scale = HEAD_DIM ** -0.5

    def body(q_ref, kc_ref, kp_ref, vc_ref, vp_ref, o_ref, lse_ref, qs, kcs, kps, vcs, vps, os, ls):
        b = pl.program_id(0)
        for src, dst in ((q_ref, qs), (kc_ref, kcs), (kp_ref, kps), (vc_ref, vcs), (vp_ref, vps)):
            _split_heads(src, dst, hp)
        dist_cur, valid_cur, valid_prev = _attn_masks(b)
        bias_cur = _batch_bias(gi, d, hp, rb, dist_cur)
        bias_far = _batch_bias(gi, d, hp, rb, dist_cur + SPAN)

        def sub_sequences(it, carry):
            rows = _batch_rows(it, d, rb)
            q = _take(qs, hp, rows)
            s_cur = jnp.where(valid_cur, _bdot(q, _take(kcs, hp, rows), "nt") * scale + bias_cur, NEG_BIG)
            s_prev = jnp.where(valid_prev, _bdot(q, _take(kps, hp, rows), "nt") * scale + bias_far, NEG_BIG)
            m = jnp.maximum(jnp.max(s_cur, axis=-1, keepdims=True), jnp.max(s_prev, axis=-1, keepdims=True))
            p_cur = jnp.exp(s_cur - m)
            p_prev = jnp.exp(s_prev - m)
            l = jnp.sum(p_cur, axis=-1, keepdims=True) + jnp.sum(p_prev, axis=-1, keepdims=True)
            o = (_bdot(p_cur, _take(vcs, hp, rows), "nn") + _bdot(p_prev, _take(vps, hp, rows), "nn")) / l
            _put(os, hp, rows, o)
            _put(ls, hp, rows, jnp.broadcast_to(m + jnp.log(l), o.shape))
            return carry

        lax.fori_loop(0, d // rb, sub_sequences, 0)
        for hi in range(hp):
            hs = slice(hi * HEAD_DIM, (hi + 1) * HEAD_DIM)
            o_ref[:, hs] = os[hi]
            lse_ref[:, hs] = ls[hi]

    per_sec = ATTN_WIDTH // W
    per_grp = ATTN_OUT_WIDTH // W

    def spec(sec, shift):
        return pl.BlockSpec((C, W), lambda b, h: (jnp.maximum(b + shift, 0), sec * per_sec + gi * per_grp + h))

    out = pl.BlockSpec((C, W), lambda b, h: (b, h))
    scr = pltpu.VMEM((hp, C, HEAD_DIM), F32)
    return pl.pallas_call(
        body, grid=(nb, HEADS_PER_GROUP // hp),
        in_specs=[spec(0, 0), spec(1, 0), spec(1, -1), spec(2, 0), spec(2, -1)], out_specs=[out, out],
        out_shape=[jax.ShapeDtypeStruct((S, ATTN_OUT_WIDTH), F32)] * 2, scratch_shapes=[scr] * 7,
        compiler_params=_params("parallel", "parallel"), name=f"attn_fwd_g{gi}")(qkv, qkv, qkv, qkv, qkv)


def _attn_bwd(gi, qkv, do, lse, dp):
    S = qkv.shape[0]
    d, C, nb, hp, W, rb = _attn_tiling(gi, S)
    scale = HEAD_DIM ** -0.5
    n_in = 12

    def body(*refs):
        ins, (dq_ref, dk_ref, dv_ref) = refs[:n_in], refs[n_in:n_in + 3]
        scr = refs[n_in + 3:]
        for src, dst in zip(ins, scr[:n_in]):
            _split_heads(src, dst, hp)
        qs, kcs, vcs, kps, vps, qns, dos, dons, lses, lsens, dps, dpns = scr[:n_in]
        dqs, dks, dvs = scr[n_in:]
        b = pl.program_id(0)
        dist_cur, valid_cur, valid_prev = _attn_masks(b)
        row = lax.broadcasted_iota(jnp.int32, (SPAN, SPAN), 0)
        col = lax.broadcasted_iota(jnp.int32, (SPAN, SPAN), 1)
        valid_next = jnp.logical_and(col >= row, b < nb - 1)

        bias_cur = _batch_bias(gi, d, hp, rb, dist_cur)
        bias_far = _batch_bias(gi, d, hp, rb, dist_cur + SPAN)

        def sub_sequences(it, carry):
            rows = _batch_rows(it, d, rb)
            q, kc, vc, kp, vp, qn, dov, don = (_take(s, hp, rows) for s in (qs, kcs, vcs, kps, vps, qns, dos, dons))
            lse_b, lse_n, dp_b, dp_n = (_take(s, hp, rows) for s in (lses, lsens, dps, dpns))
            p_cur = jnp.where(valid_cur, jnp.exp(_bdot(q, kc, "nt") * scale + bias_cur - lse_b), 0.0)
            p_prev = jnp.where(valid_prev, jnp.exp(_bdot(q, kp, "nt") * scale + bias_far - lse_b), 0.0)
            ds_cur = p_cur * (_bdot(dov, vc, "nt") - dp_b)
            ds_prev = p_prev * (_bdot(dov, vp, "nt") - dp_b)
            _put(dqs, hp, rows, (_bdot(ds_cur, kc, "nn") + _bdot(ds_prev, kp, "nn")) * scale)
            p_next = jnp.where(valid_next, jnp.exp(_bdot(qn, kc, "nt") * scale + bias_far - lse_n), 0.0)
            ds_next = p_next * (_bdot(don, vc, "nt") - dp_n)
            _put(dvs, hp, rows, _bdot(p_cur, dov, "tn") + _bdot(p_next, don, "tn"))
            _put(dks, hp, rows, (_bdot(ds_cur, q, "tn") + _bdot(ds_next, qn, "tn")) * scale)
            return carry

        lax.fori_loop(0, d // rb, sub_sequences, 0)
        for hi in range(hp):
            hs = slice(hi * HEAD_DIM, (hi + 1) * HEAD_DIM)
            dq_ref[:, hs] = dqs[hi].astype(BF16)
            dk_ref[:, hs] = dks[hi].astype(BF16)
            dv_ref[:, hs] = dvs[hi].astype(BF16)

    per_sec = ATTN_WIDTH // W
    per_grp = ATTN_OUT_WIDTH // W

    def qkv_spec(sec, shift):
        return pl.BlockSpec((C, W), lambda b, h: (jnp.clip(b + shift, 0, nb - 1), sec * per_sec + gi * per_grp + h))

    def act_spec(shift):
        return pl.BlockSpec((C, W), lambda b, h: (jnp.clip(b + shift, 0, nb - 1), h))

    out = act_spec(0)
    scr = pltpu.VMEM((hp, C, HEAD_DIM), F32)
    return pl.pallas_call(
        body, grid=(nb, HEADS_PER_GROUP // hp),
        in_specs=[qkv_spec(0, 0), qkv_spec(1, 0), qkv_spec(2, 0), qkv_spec(1, -1), qkv_spec(2, -1), qkv_spec(0, 1),
                  act_spec(0), act_spec(1), act_spec(0), act_spec(1), act_spec(0), act_spec(1)],
        out_specs=[out, out, out], out_shape=[jax.ShapeDtypeStruct((S, ATTN_OUT_WIDTH), BF16)] * 3,
        scratch_shapes=[scr] * (n_in + 3),
        compiler_params=_params("parallel", "parallel"), name=f"attn_bwd_g{gi}")(
            qkv, qkv, qkv, qkv, qkv, qkv, do, do, lse, lse, dp, dp)


def _group_weights(l_refs):
    l0, l1, l2 = (r[...] for r in l_refs)
    m = jnp.maximum(jnp.maximum(l0, l1), l2)
    e = [jnp.exp(l0 - m), jnp.exp(l1 - m), jnp.exp(l2 - m)]
    tot = e[0] + e[1] + e[2]
    return [ei / tot for ei in e]


def _merge_fwd(outs, lses):
    S, W = outs[0].shape
    T = 512

    def body(o0, o1, o2, l0, l1, l2, y_ref):
        w = _group_weights((l0, l1, l2))
        y_ref[...] = (w[0] * o0[...] + w[1] * o1[...] + w[2] * o2[...]).astype(BF16)

    blk = pl.BlockSpec((T, W), lambda i: (i, 0))
    return pl.pallas_call(
        body, grid=(S // T,), in_specs=[blk] * 6, out_specs=blk, out_shape=jax.ShapeDtypeStruct((S, W), BF16),
        compiler_params=_params("parallel"), name="merge_fwd")(*outs, *lses)


def _merge_bwd(dy, outs, lses):
    S, W = outs[0].shape
    T = 512

    def body(dy_ref, o0, o1, o2, l0, l1, l2, do0, do1, do2, dp0, dp1, dp2):
        w = _group_weights((l0, l1, l2))
        dyv = dy_ref[...]
        y = w[0] * o0[...] + w[1] * o1[...] + w[2] * o2[...]
        prod = dyv * y
        dots = []
        for hi in range(HEADS_PER_GROUP):
            hs = slice(hi * HEAD_DIM, (hi + 1) * HEAD_DIM)
            dots.append(jnp.broadcast_to(jnp.sum(prod[:, hs], axis=-1, keepdims=True), (T, HEAD_DIM)))
        dot_b = jnp.concatenate(dots, axis=1)
        for wg, do_ref, dp_ref in zip(w, (do0, do1, do2), (dp0, dp1, dp2)):
            do_ref[...] = (wg * dyv).astype(BF16)
            dp_ref[...] = wg * dot_b

    blk = pl.BlockSpec((T, W), lambda i: (i, 0))
    res = pl.pallas_call(
        body, grid=(S // T,), in_specs=[blk] * 7, out_specs=[blk] * 6,
        out_shape=[jax.ShapeDtypeStruct((S, W), BF16)] * 3 + [jax.ShapeDtypeStruct((S, W), F32)] * 3,
        compiler_params=_params("parallel"), name="merge_bwd")(dy, *outs, *lses)
    return res[:3], res[3:]


def _mix_fwd(pool_y, attn_y, w_pool_out_t, w_attn_out_t, gates_pre, b_gate):
    S = pool_y.shape[0]
    D = D_MODEL
    tm, tn = 1024, 512
    nj = D // tn

    def body(p_ref, a_ref, wp_ref, wa_ref, g0_ref, g1_ref, b0_ref, b1_ref, mixed_ref, yp_ref, ya_ref):
        yp = _dot(p_ref[...], wp_ref[...], "nt")
        ya = _dot(a_ref[...], wa_ref[...], "nt")
        g0 = jax.nn.sigmoid(g0_ref[...] + b0_ref[...])
        g1 = jax.nn.sigmoid(g1_ref[...] + b1_ref[...])
        mixed_ref[...] = (g0 * yp + g1 * ya).astype(BF16)
        yp_ref[...] = yp.astype(BF16)
        ya_ref[...] = ya.astype(BF16)

    out = pl.BlockSpec((tm, tn), lambda i, j: (i, j))
    return pl.pallas_call(
        body, grid=(S // tm, nj),
        in_specs=[pl.BlockSpec((tm, POOL_WIDTH), lambda i, j: (i, 0)),
                  pl.BlockSpec((tm, ATTN_OUT_WIDTH), lambda i, j: (i, 0)),
                  pl.BlockSpec((tn, POOL_WIDTH), lambda i, j: (j, 0)),
                  pl.BlockSpec((tn, ATTN_OUT_WIDTH), lambda i, j: (j, 0)),
                  pl.BlockSpec((tm, tn), lambda i, j: (i, j)), pl.BlockSpec((tm, tn), lambda i, j: (i, j + nj)),
                  pl.BlockSpec((1, tn), lambda i, j: (0, j)), pl.BlockSpec((1, tn), lambda i, j: (0, j + nj))],
        out_specs=[out, out, out], out_shape=[jax.ShapeDtypeStruct((S, D), BF16)] * 3,
        compiler_params=_params("parallel", "parallel"), name="mix_fwd")(
            pool_y, attn_y, w_pool_out_t, w_attn_out_t, gates_pre, gates_pre, b_gate, b_gate)


def _mix_bwd(dx, w_out, gates_pre, b_gate, y_pool, y_attn):
    S = dx.shape[0]
    D = D_MODEL
    T = 256

    def body(dx_ref, w_ref, gp_ref, b_ref, yp_ref, ya_ref, dyp_ref, dya_ref, dg_ref, db_ref):
        i = pl.program_id(0)
        dm = _dot(dx_ref[...], w_ref[...], "nt")
        parts = []
        for br, (y_ref, dy_ref) in enumerate(((yp_ref, dyp_ref), (ya_ref, dya_ref))):
            cs = slice(br * D, (br + 1) * D)
            g = jax.nn.sigmoid(gp_ref[:, cs] + b_ref[:, cs])
            dy_ref[...] = (dm * g).astype(BF16)
            dpre = dm * y_ref[...].astype(F32) * g * (1.0 - g)
            dg_ref[:, cs] = dpre.astype(BF16)
            parts.append(jnp.sum(dpre, axis=0, keepdims=True))

        @pl.when(i == 0)
        def _():
            db_ref[:, 0:D] = parts[0]
            db_ref[:, D:2 * D] = parts[1]

        @pl.when(i > 0)
        def _():
            db_ref[:, 0:D] += parts[0]
            db_ref[:, D:2 * D] += parts[1]

    row = pl.BlockSpec((T, D), lambda i: (i, 0))
    row2 = pl.BlockSpec((T, 2 * D), lambda i: (i, 0))
    vec2 = pl.BlockSpec((1, 2 * D), lambda i: (0, 0))
    return pl.pallas_call(
        body, grid=(S // T,), in_specs=[row, pl.BlockSpec((D, D), lambda i: (0, 0)), row2, vec2, row, row],
        out_specs=[row, row, row2, vec2],
        out_shape=[jax.ShapeDtypeStruct((S, D), BF16), jax.ShapeDtypeStruct((S, D), BF16),
                   jax.ShapeDtypeStruct((S, 2 * D), BF16), jax.ShapeDtypeStruct((1, 2 * D), F32)],
        compiler_params=_params("arbitrary"), name="mix_bwd")(dx, w_out, gates_pre, b_gate, y_pool, y_attn)


_CONV_T, _CONV_C = 512, 512


def _conv_taps(ext_ref, h, w_ref, b_ref, cols=slice(None)):
    ext = ext_ref[h, :, cols]
    x2 = pltpu.roll(ext, 2, 0)[HALO:]
    x1 = pltpu.roll(ext, 1, 0)[HALO:]
    x0 = ext[HALO:]
    y = b_ref[:, cols] + w_ref[0:1, cols] * x2
    y = y + w_ref[1:2, cols] * x1
    y = y + w_ref[2:3, cols] * x0
    return y, (x2, x1, x0)


def _gelu_parts(a):
    cdf = 0.5 * (1.0 + lax.erf(a * (1.0 / math.sqrt(2.0))))
    return cdf, a * cdf


def _conv_specs(S):
    T, C = _CONV_T, _CONV_C
    nj = D_FF // C
    blk = pl.BlockSpec((2, T, C), lambda i, j: (0, i, j))
    per = T // HALO
    halo = pl.BlockSpec((2, HALO, C), lambda i, j: (0, jnp.maximum(i * per - 1, 0), j))
    w_a = pl.BlockSpec((3, C), lambda i, j: (0, j))
    w_b = pl.BlockSpec((3, C), lambda i, j: (0, j + nj))
    b_a = pl.BlockSpec((1, C), lambda i, j: (0, j))
    b_b = pl.BlockSpec((1, C), lambda i, j: (0, j + nj))
    return T, C, nj, blk, halo, w_a, w_b, b_a, b_b


def _conv_load(ext_ref, halo_ref, up_ref, first):
    for h in range(2):
        ext_ref[h, 0:HALO, :] = jnp.where(first, 0.0, halo_ref[h].astype(F32))
        ext_ref[h, HALO:, :] = up_ref[h].astype(F32)


def _up_proj_convffn(h, w_up_t, conv_w, conv_b):
    S = h.shape[0]
    tm, C = 1024, _CONV_C
    nj = D_FF // C

    def body(h_ref, wa_ref, wb_ref, cwa_ref, cwb_ref, cba_ref, cbb_ref, up_ref, act_ref, ext_ref, carry_ref):
        i, j = pl.program_id(0), pl.program_id(1)
        for half, w_ref in enumerate((wa_ref, wb_ref)):
            pre = _dot(h_ref[...], w_ref[...], "nt").astype(BF16)
            up_ref[half] = pre
            ext_ref[half, 0:HALO, :] = jnp.where(i == 0, 0.0, carry_ref[j, half])
            ext_ref[half, HALO:, :] = pre.astype(F32)
            carry_ref[j, half] = ext_ref[half, tm:tm + HALO, :]
        a, _ = _conv_taps(ext_ref, 0, cwa_ref, cba_ref)
        b, _ = _conv_taps(ext_ref, 1, cwb_ref, cbb_ref)
        _, gelu = _gelu_parts(a)
        act_ref[...] = (gelu * b).astype(BF16)

    def w_spec(half):
        return pl.BlockSpec((C, D_MODEL), lambda i, j: (j + half * nj, 0))

    def c_spec(rows, half):
        return pl.BlockSpec((rows, C), lambda i, j: (0, j + half * nj))

    return pl.pallas_call(
        body, grid=(S // tm, nj),
        in_specs=[pl.BlockSpec((tm, D_MODEL), lambda i, j: (i, 0)), w_spec(0), w_spec(1),
                  c_spec(3, 0), c_spec(3, 1), c_spec(1, 0), c_spec(1, 1)],
        out_specs=[pl.BlockSpec((2, tm, C), lambda i, j: (0, i, j)), pl.BlockSpec((tm, C), lambda i, j: (i, j))],
        out_shape=[jax.ShapeDtypeStruct((2, S, D_FF), BF16), jax.ShapeDtypeStruct((S, D_FF), BF16)],
        scratch_shapes=[pltpu.VMEM((2, HALO + tm, C), F32), pltpu.VMEM((nj, 2, HALO, C), F32)],
        compiler_params=_params("arbitrary", "arbitrary"), name="up_proj_convffn")(
            h, w_up_t, w_up_t, conv_w, conv_w, conv_b, conv_b)


def _convffn_bwd_a(up_pre, dx, w_down, conv_w, conv_b):
    S = up_pre.shape[1]
    T, C, nj, blk, halo, w_a, w_b, b_a, b_b = _conv_specs(S)
    def swap(spec):
        return pl.BlockSpec(spec.block_shape, lambda j, i, _m=spec.index_map: _m(i, j))

    def body(up_ref, halo_ref, dx_ref, wd_ref, wa_ref, wb_ref, ba_ref, bb_ref, dup_ref, dwa_ref, dwb_ref, dba_ref,
             dbb_ref, ext_ref):
        i = pl.program_id(1)
        _conv_load(ext_ref, halo_ref, up_ref, i == 0)

        @pl.when(i == 0)
        def _():
            for acc_ref in (dwa_ref, dwb_ref, dba_ref, dbb_ref):
                acc_ref[...] = jnp.zeros_like(acc_ref)

        for c in range(C // MXU_DIM):
            cs = slice(c * MXU_DIM, (c + 1) * MXU_DIM)
            dact = _dot(dx_ref[...], wd_ref[cs, :], "nt")
            a, xa = _conv_taps(ext_ref, 0, wa_ref, ba_ref, cs)
            b, xb = _conv_taps(ext_ref, 1, wb_ref, bb_ref, cs)
            cdf, gelu = _gelu_parts(a)
            dgelu = cdf + a * (jnp.exp(-0.5 * a * a) * (1.0 / math.sqrt(2.0 * math.pi)))
            da = dact * b * dgelu
            db = dact * gelu
            dup_ref[0, :, cs] = da.astype(BF16)
            dup_ref[1, :, cs] = db.astype(BF16)
            for dval, xs, dw_ref, dbias_ref in ((da, xa, dwa_ref, dba_ref), (db, xb, dwb_ref, dbb_ref)):
                dw_ref[:, cs] += jnp.concatenate([jnp.sum(dval * xk, axis=0, keepdims=True) for xk in xs], axis=0)
                dbias_ref[:, cs] += jnp.sum(dval, axis=0, keepdims=True)

    dx_spec = pl.BlockSpec((T, D_MODEL), lambda j, i: (i, 0))
    wd_spec = pl.BlockSpec((C, D_MODEL), lambda j, i: (j, 0))
    dw_spec = pl.BlockSpec((3, C), lambda j, i: (0, j))
    db_spec = pl.BlockSpec((1, C), lambda j, i: (0, j))
    res = pl.pallas_call(
        body, grid=(nj, S // T),
        in_specs=[swap(blk), swap(halo), dx_spec, wd_spec, swap(w_a), swap(w_b), swap(b_a), swap(b_b)],
        out_specs=[swap(blk), dw_spec, dw_spec, db_spec, db_spec],
        out_shape=[jax.ShapeDtypeStruct((2, S, D_FF), BF16), jax.ShapeDtypeStruct((3, D_FF), F32),
                   jax.ShapeDtypeStruct((3, D_FF), F32), jax.ShapeDtypeStruct((1, D_FF), F32),
                   jax.ShapeDtypeStruct((1, D_FF), F32)],
        scratch_shapes=[pltpu.VMEM((2, HALO + T, C), F32)],
        compiler_params=_params("parallel", "arbitrary"), name="convffn_bwd_a")(
            up_pre, up_pre, dx, w_down, conv_w, conv_w, conv_b, conv_b)
    dup, dwa, dwb, dba, dbb = res
    dconv_w = jnp.concatenate([dwa, dwb], axis=1)
    dconv_b = jnp.concatenate([dba, dbb], axis=1)
    return dup, dconv_w, dconv_b


def _convffn_bwd_b(dup, conv_w, w_up_t):
    S = dup.shape[1]
    tm, tk = 1024, D_FF // 4
    per_half = D_FF // tk
    nk = 2 * per_half
    per = tm // HALO
    n_blk = S // tm

    def body(d_ref, halo_ref, cw_ref, w_ref, dpre_ref, dh_ref, ext_ref, acc_ref):
        i, k = pl.program_id(0), pl.program_id(1)
        n = tm + HALO
        ext_ref[0:tm, :] = d_ref[...].astype(F32)
        ext_ref[tm:, :] = jnp.where(i == n_blk - 1, 0.0, halo_ref[...].astype(F32))
        ext = ext_ref[...]
        y = cw_ref[2:3, :] * ext[0:tm] + cw_ref[1:2, :] * pltpu.roll(ext, n - 1, 0)[0:tm]
        y = y + cw_ref[0:1, :] * pltpu.roll(ext, n - 2, 0)[0:tm]
        pre = y.astype(BF16)
        dpre_ref[...] = pre
        prod = _dot(pre, w_ref[...], "nn")

        @pl.when(k == 0)
        def _():
            acc_ref[...] = prod

        @pl.when(k > 0)
        def _():
            acc_ref[...] += prod

        @pl.when(k == nk - 1)
        def _():
            dh_ref[...] = acc_ref[...].astype(BF16)

    blk = pl.BlockSpec((None, tm, tk), lambda i, k: (k // per_half, i, k % per_half))
    halo = pl.BlockSpec((None, HALO, tk),
                        lambda i, k: (k // per_half, jnp.minimum((i + 1) * per, S // HALO - 1), k % per_half))
    return pl.pallas_call(
        body, grid=(n_blk, nk),
        in_specs=[blk, halo, pl.BlockSpec((3, tk), lambda i, k: (0, k)), pl.BlockSpec((tk, D_MODEL), lambda i, k: (k, 0))],
        out_specs=[blk, pl.BlockSpec((tm, D_MODEL), lambda i, k: (i, 0))],
        out_shape=[jax.ShapeDtypeStruct((2, S, D_FF), BF16), jax.ShapeDtypeStruct((S, D_MODEL), BF16)],
        scratch_shapes=[pltpu.VMEM((tm + HALO, tk), F32), pltpu.VMEM((tm, D_MODEL), F32)],
        compiler_params=_params("parallel", "arbitrary"), name="convffn_bwd_b")(dup, dup, conv_w, w_up_t)


def _position():
    return lax.axis_index("x"), lax.axis_index("y"), lax.axis_index("c")


def _flip(pos, k):
    x, y, c = pos
    return (1 - x if k & 4 else x, 1 - y if k & 2 else y, 1 - c if k & 1 else c)


def _index(pos):
    return 4 * pos[0] + 2 * pos[1] + pos[2]


_HBM = pl.BlockSpec(memory_space=pltpu.HBM)
_SEM = pl.BlockSpec(memory_space=pltpu.SEMAPHORE)
_ANY = pl.BlockSpec(memory_space=pl.ANY)
_EFFECT = pltpu.SideEffectType.DATAFLOW_SIDE_EFFECTING


def _in_hbm(a):
    return pltpu.with_memory_space_constraint(a, pltpu.HBM)


def _split_copies(name, srcs, lands, n_copies, plan, start_after, wait_after):
    ns, nl = len(srcs), len(lands)
    start_after = tuple(start_after)
    na = len(start_after)

    def start_body(*refs):
        src_refs, land_refs = refs[:ns], refs[ns:ns + nl]
        send_sems, recv_sems = refs[ns + nl + na], refs[ns + nl + na + 1]
        token = refs[-1]
        for k, (src, dst, _, peer) in enumerate(plan(src_refs, land_refs)):
            pltpu.make_async_remote_copy(src_ref=src, dst_ref=dst, send_sem=send_sems.at[k], recv_sem=recv_sems.at[k],
                                         device_id=peer, device_id_type=MESH).start()
        token[...] = jnp.zeros_like(token)

    thru = [pltpu.HBM(a.shape, a.dtype) for a in list(srcs) + list(lands)]
    res = pl.pallas_call(
        start_body, name=name + "_start",
        out_shape=[pltpu.SemaphoreType.DMA((n_copies,)), pltpu.SemaphoreType.DMA((n_copies,))] + thru
        + [jax.ShapeDtypeStruct((8, 128), F32)],
        in_specs=[_HBM] * (ns + nl) + [_ANY] * na,
        out_specs=[_SEM, _SEM] + [_HBM] * (ns + nl) + [pl.BlockSpec(memory_space=pltpu.VMEM)],
        input_output_aliases={i: 2 + i for i in range(ns + nl)},
        compiler_params=pltpu.CompilerParams(has_side_effects=_EFFECT),
    )(*[_in_hbm(a) for a in list(srcs) + list(lands)], *start_after)
    send_sems, recv_sems, token = res[0], res[1], res[-1]
    thru_vals = res[2:-1]

    def wait():
        def wait_body(*refs):
            src_refs, land_refs = refs[:ns], refs[ns:ns + nl]
            s_sems, r_sems = refs[ns + nl], refs[ns + nl + 1]
            for k, (src, _, dst, peer) in enumerate(plan(src_refs, land_refs)):
                cp = pltpu.make_async_remote_copy(src_ref=src, dst_ref=dst, send_sem=s_sems.at[k], recv_sem=r_sems.at[k],
                                                  device_id=peer, device_id_type=MESH)
                cp.wait_send()
                cp.wait_recv()

        out = pl.pallas_call(
            wait_body, name=name + "_wait", out_shape=thru,
            in_specs=[_HBM] * (ns + nl) + [_SEM, _SEM, _ANY], out_specs=[_HBM] * (ns + nl),
            input_output_aliases={i: i for i in range(ns + nl)},
            compiler_params=pltpu.CompilerParams(has_side_effects=_EFFECT),
        )(*thru_vals, send_sems, recv_sems, wait_after())
        return list(out[:ns]), list(out[ns:])

    return wait, token


def _gather_level1_plan(n):
    def plan(src_refs, land_refs):
        me = _position()
        x, y, c = me
        peers = [(x, y, 1 - c), (1 - x, y, c), (x, 1 - y, c), (1 - x, 1 - y, c)]
        return [(src_refs[a], land_refs[a].at[_index(me)], land_refs[a].at[_index(p)], p)
                for a in range(n) for p in peers]
    return plan


def _gather_level2(lands):
    n = len(lands)

    def body(*refs):
        lin = refs[:n]
        send_sems, recv_sems = refs[2 * n:]
        x, y, c = _position()
        sibling = (x, y, 1 - c)
        chips = [(1 - x, y), (x, 1 - y), (1 - x, 1 - y)]

        def copy(a, j, core):
            slot = lin[a].at[_index((*chips[j], core))]
            return pltpu.make_async_remote_copy(
                src_ref=slot, dst_ref=slot, send_sem=send_sems.at[3 * a + j], recv_sem=recv_sems.at[3 * a + j],
                device_id=sibling, device_id_type=MESH)

        sent = [copy(a, j, c) for a in range(n) for j in range(3)]
        for cp in sent:
            cp.start()
        for a in range(n):
            for j in range(3):
                copy(a, j, 1 - c).wait_recv()
        for cp in sent:
            cp.wait_send()

    return pl.pallas_call(
        body, out_shape=[jax.ShapeDtypeStruct(l.shape, l.dtype) for l in lands],
        in_specs=[_ANY] * n, out_specs=[_ANY] * n, input_output_aliases={a: a for a in range(n)},
        scratch_shapes=[pltpu.SemaphoreType.DMA((3 * n,)), pltpu.SemaphoreType.DMA((3 * n,))],
        name="gather_level2")(*lands)


def _gather_direct_plan(n):
    def plan(src_refs, land_refs):
        me = _position()
        peers = [_flip(me, k) for k in range(1, N_DEV)]
        return [(src_refs[a], land_refs[a].at[_index(me)], land_refs[a].at[_index(p)], p)
                for a in range(n) for p in peers]
    return plan


def _place_own(shards):
    me = _index(_position())
    return [lax.dynamic_update_slice(lax.empty((N_DEV,) + s.shape, s.dtype), s[None], (me, 0, 0)) for s in shards]


def _exchange_direct_plan(n):
    def plan(src_refs, land_refs):
        me = _position()
        peers = [_flip(me, k) for k in range(1, N_DEV)]
        return [(src_refs[a].at[_index(p)], land_refs[a].at[_index(me)], land_refs[a].at[_index(p)], p)
                for a in range(n) for p in peers]
    return plan


def _adamw(w, g, m, v):
    m = ADAM_B1 * m + (1.0 - ADAM_B1) * g
    v = ADAM_B2 * v + (1.0 - ADAM_B2) * (g * g)
    m_hat = m / (1.0 - ADAM_B1 ** ADAM_STEP)
    v_hat = v / (1.0 - ADAM_B2 ** ADAM_STEP)
    delta = -ADAM_LR * (m_hat / (jnp.sqrt(v_hat) + ADAM_EPS) + ADAM_WD * w)
    return delta, m, v


def _row_tile(rows, unit, cap=256):
    best = rows
    for t in range(unit, min(rows, cap) + 1, unit):
        if rows % t == 0:
            best = t
    return best


def _sum_partials(name, recv):
    n_src, R, C = recv.shape
    T = _row_tile(R, 16)

    def body(r_ref, g_out):
        g = r_ref[0].astype(F32)
        for s in range(1, n_src):
            g = g + r_ref[s].astype(F32)
        g_out[...] = g

    return pl.pallas_call(
        body, grid=(R // T,), in_specs=[pl.BlockSpec((n_src, T, C), lambda i: (0, i, 0))],
        out_specs=pl.BlockSpec((T, C), lambda i: (i, 0)), out_shape=jax.ShapeDtypeStruct((R, C), F32),
        compiler_params=_params("parallel"), name=name)(recv)


def _adamw_update(name, w, g, m, v):
    R, C = w.shape
    T = _row_tile(R, 8)

    def body(w_ref, g_ref, m_ref, v_ref, d_out, m_out, v_out):
        delta, mn, vn = _adamw(w_ref[...], g_ref[...], m_ref[...], v_ref[...])
        d_out[...] = delta
        m_out[...] = mn
        v_out[...] = vn

    blk = pl.BlockSpec((T, C), lambda i: (i, 0))
    return pl.pallas_call(
        body, grid=(R // T,), in_specs=[blk] * 4, out_specs=[blk] * 3,
        out_shape=[jax.ShapeDtypeStruct((R, C), F32)] * 3, compiler_params=_params("parallel"), name=name)(w, g, m, v)


def _gather_vmem(p_ref, gath, send_sems, recv_sems):
    me = _position()
    gath[_index(me)] = p_ref[...]
    copies = []
    for k in range(1, N_DEV):
        peer = _flip(me, k)
        copies.append(pltpu.make_async_remote_copy(
            src_ref=p_ref, dst_ref=gath.at[_index(me)], send_sem=send_sems.at[k - 1],
            recv_sem=recv_sems.at[k - 1], device_id=peer, device_id_type=MESH))
    for cp in copies:
        cp.start()
    for k in range(1, N_DEV):
        peer = _flip(me, k)
        pltpu.make_async_remote_copy(
            src_ref=p_ref, dst_ref=gath.at[_index(peer)], send_sem=send_sems.at[k - 1],
            recv_sem=recv_sems.at[k - 1], device_id=peer, device_id_type=MESH).wait_recv()
    for cp in copies:
        cp.wait_send()


def _small_allreduce_adamw(part, w, m, v):
    R, C = part.shape

    def body(p_ref, w_ref, m_ref, v_ref, g_out, d_out, m_out, v_out, gath, send_sems, recv_sems):
        _gather_vmem(p_ref, gath, send_sems, recv_sems)
        g = gath[0]
        for s in range(1, N_DEV):
            g = g + gath[s]
        delta, mn, vn = _adamw(w_ref[...], g, m_ref[...], v_ref[...])
        g_out[...] = g
        d_out[...] = delta
        m_out[...] = mn
        v_out[...] = vn

    vm = pl.BlockSpec(memory_space=pltpu.VMEM)
    return pl.pallas_call(
        body, in_specs=[vm] * 4, out_specs=[vm] * 4, out_shape=[jax.ShapeDtypeStruct((R, C), F32)] * 4,
        scratch_shapes=[pltpu.VMEM((N_DEV, R, C), F32), pltpu.SemaphoreType.DMA((7,)), pltpu.SemaphoreType.DMA((7,))],
        name="small_allreduce_adamw")(part, w, m, v)


_TRANSPOSED = ("w_in", "w_up", "w_pool_out", "w_attn_out")
_SMALL = (("g_mix", 2048), ("b_gate", 4096), ("pool_scale", 1024), ("g_ffn", 2048), ("conv_b", 11264), ("g_final", 2048))
CONV_ROWS = 8


def _to_comm(name, shard):
    return shard.T if name in _TRANSPOSED else shard


def _pack_small(vals):
    flat = [vals[n].reshape(-1) for n, _ in _SMALL]
    used = sum(f.shape[0] for f in flat)
    flat.append(jnp.zeros((SMALL_ROWS * PACK_COLS - used,), F32))
    return jnp.concatenate(flat).reshape(SMALL_ROWS, PACK_COLS)


def _unpack_small(packed, like):
    flat = packed.reshape(-1)
    out, off = {}, 0
    for name, n in _SMALL:
        out[name] = flat[off:off + n].reshape(like[name].shape)
        off += n
    return out


def _local_step(x, target, W, small, first_token=0.0, first_weights=None, late_weights=None, emit=None):
    S = x.shape[0]
    o_qkv, o_gate = POOL_WIDTH, POOL_WIDTH + QKV_WIDTH
    if emit is None:
        emit = lambda grads: None

    def zero(token):
        return 0.0 if token is None else token[0, 0]

    h1 = _rms_fwd("rms1_fwd", x, small["g_mix"] + first_token)
    tok = None
    if first_weights is not None:
        w_first, tok = first_weights(h1)
        small = dict(small, conv_w_full=w_first.pop("conv_w_full"))
        W = dict(W, **w_first)
    w_in_t = W["w_in"]
    u = _mm_nt("proj_u", h1, w_in_t, tm=2048, tn=512, tk=2048, o_dtype=F32, n_rows=POOL_WIDTH, b_row_off=0,
               after=tok)
    qkv = _mm_nt("proj_qkv", h1, w_in_t, tm=2048, tn=512, tk=2048, o_dtype=BF16, n_rows=QKV_WIDTH, b_row_off=o_qkv,
                 after=tok)
    gates_pre = _mm_nt("proj_gates", h1, w_in_t, tm=2048, tn=512, tk=2048, o_dtype=BF16, n_rows=2 * D_MODEL,
                       b_row_off=o_gate, after=tok)
    pool_y = _pool_fwd(u, W["w_pool_lin"], small["pool_scale"])
    outs, lses = zip(*[_attn_fwd(gi, qkv) for gi in range(3)])
    attn_y = _merge_fwd(outs, lses)
    if late_weights is not None:
        W = dict(W, **late_weights(0, attn_y))
    w_up_t = W["w_up"]
    mixed, y_pool, y_attn = _mix_fwd(pool_y, attn_y, W["w_pool_out"], W["w_attn_out"], gates_pre, small["b_gate"])
    x1, h2 = _out_proj_norm(mixed, W["w_out"], x, small["g_ffn"])

    nj = D_FF // 512
    up_pre, act = _up_proj_convffn(h2, w_up_t, small["conv_w_full"], small["conv_b"])
    if late_weights is not None:
        W = dict(W, **late_weights(1, act))
    x2 = _mm_nn("down_proj", act, W["w_down"], tm=1024, tn=1024, tk=D_FF // 2, o_dtype=F32, res=x1)

    dx2, dx2_b, dg_final, sq_cols = _final_loss("final_loss", x2, small["g_final"], target)

    dw_down = _mm_tn("dw_down", act, dx2_b, tm=512, tn=2048, tk=TOKEN_TK)
    tok = emit(dict(w_down=dw_down))
    dup, dconv_w, dconv_b = _convffn_bwd_a(up_pre, dx2_b, W["w_down"], small["conv_w_full"],
                                           small["conv_b"] + zero(tok))
    dup_pre, dh2 = _convffn_bwd_b(dup, small["conv_w_full"], w_up_t)
    dw_up = _mm("dw_up", dup_pre, h2, mode="tn", grid=(2 * nj, 1, S // TOKEN_TK),
                a_spec=pl.BlockSpec((None, TOKEN_TK, 512), lambda i, j, k: (i // nj, k, i % nj)),
                b_spec=pl.BlockSpec((TOKEN_TK, D_MODEL), lambda i, j, k: (k, 0)),
                o_spec=pl.BlockSpec((512, D_MODEL), lambda i, j, k: (i, 0)),
                o_shape=(2 * D_FF, D_MODEL), o_dtype=BF16, acc_shape=(512, D_MODEL))
    tok = emit(dict(w_up=dw_up, conv_w=dconv_w))
    dx1, dx1_b, dg_ffn = _rms_bwd("rms2_bwd", x1, small["g_ffn"] + zero(tok), dh2, dx2, with_bf16=True)

    dw_out = _mm_tn("dw_out", mixed, dx1_b, tm=512, tn=2048, tk=TOKEN_TK)
    tok = emit(dict(w_out=dw_out))
    dy_pool, dy_attn, dgates, db_gate = _mix_bwd(dx1_b, W["w_out"], gates_pre, small["b_gate"] + zero(tok),
                                                 y_pool, y_attn)
    dpool_y = _mm_nn("d_pool_y", dy_pool, W["w_pool_out"], tm=1024, tn=1024, tk=2048, o_dtype=F32)
    dw_pool_out = _mm_tn("dw_pool_out", dy_pool, pool_y, tm=1024, tn=1024, tk=TOKEN_TK)
    dattn_y = _mm_nn("d_attn_y", dy_attn, W["w_attn_out"], tm=1024, tn=512, tk=2048, o_dtype=F32)
    dw_attn_out = _mm_tn("dw_attn_out", dy_attn, attn_y, tm=1024, tn=512, tk=TOKEN_TK)
    e, dw_pool_lin, dpool_scale = _pool_bwd_a(u, dpool_y, W["w_pool_lin"], small["pool_scale"])
    du = _pool_bwd_b(e)
    dos, dps = _merge_bwd(dattn_y, outs, lses)
    dqkv = [_attn_bwd(gi, qkv, dos[gi], lses[gi], dps[gi]) for gi in range(3)]
    dproj = jnp.concatenate([du] + [dqkv[gi][sec] for sec in range(3) for gi in range(3)] + [dgates], axis=1)
    dw_in = _mm_tn("dw_in", dproj, h1, tm=512, tn=2048, tk=TOKEN_TK)
    tok = emit(dict(w_in=dw_in, w_pool_out=dw_pool_out, w_attn_out=dw_attn_out, w_pool_lin=dw_pool_lin))
    dh1 = _mm_nn("d_h1", dproj, w_in_t, tm=1024, tn=1024, tk=IN_WIDTH // 4, o_dtype=BF16,
                 after=tok)
    grad_x, dg_mix = _rms_bwd("rms1_bwd", x, small["g_mix"] + zero(tok), dh1, dx1, with_bf16=False)

    grads = dict(w_in=dw_in, w_up=dw_up, w_down=dw_down, w_out=dw_out, w_pool_out=dw_pool_out,
                 w_attn_out=dw_attn_out, w_pool_lin=dw_pool_lin, conv_w=dconv_w)
    small_grads = dict(g_mix=dg_mix, b_gate=db_gate, pool_scale=dpool_scale, g_ffn=dg_ffn, conv_b=dconv_b,
                       g_final=dg_final)
    return sq_cols, grad_x, grads, small_grads


def kernel(x, g_mix, w_in, b_gate, w_pool_lin, pool_scale, w_pool_out, w_attn_out, w_out, g_ffn, w_up, conv_w, conv_b, w_down, g_final, loss_target, m_g_mix, m_w_in, m_b_gate, m_w_pool_lin, m_pool_scale, m_w_pool_out, m_w_attn_out, m_w_out, m_g_ffn, m_w_up, m_conv_w, m_conv_b, m_w_down, m_g_final, v_g_mix, v_w_in, v_b_gate, v_w_pool_lin, v_pool_scale, v_w_pool_out, v_w_attn_out, v_w_out, v_g_ffn, v_w_up, v_conv_w, v_conv_b, v_w_down, v_g_final):
    given = dict(g_mix=g_mix, w_in=w_in, b_gate=b_gate, w_pool_lin=w_pool_lin, pool_scale=pool_scale,
                 w_pool_out=w_pool_out, w_attn_out=w_attn_out, w_out=w_out, g_ffn=g_ffn, w_up=w_up, conv_w=conv_w,
                 conv_b=conv_b, w_down=w_down, g_final=g_final)
    mom_m = dict(g_mix=m_g_mix, w_in=m_w_in, b_gate=m_b_gate, w_pool_lin=m_w_pool_lin, pool_scale=m_pool_scale,
                 w_pool_out=m_w_pool_out, w_attn_out=m_w_attn_out, w_out=m_w_out, g_ffn=m_g_ffn, w_up=m_w_up,
                 conv_w=m_conv_w, conv_b=m_conv_b, w_down=m_w_down, g_final=m_g_final)
    mom_v = dict(g_mix=v_g_mix, w_in=v_w_in, b_gate=v_b_gate, w_pool_lin=v_w_pool_lin, pool_scale=v_pool_scale,
                 w_pool_out=v_w_pool_out, w_attn_out=v_w_attn_out, w_out=v_w_out, g_ffn=v_g_ffn, w_up=v_w_up,
                 conv_w=v_conv_w, conv_b=v_conv_b, w_down=v_w_down, g_final=v_g_final)
    def full(g):
        return g.reshape(N_DEV * g.shape[1], g.shape[2])

    shard = dict(w_in=_to_comm("w_in", w_in[0]).astype(BF16))
    W = {}
    first = [shard["w_in"], w_pool_lin[0].reshape(4 * 32, POOL_GROUP_WIDTH),
             jnp.pad(conv_w[0], ((0, CONV_ROWS - 3), (0, 0)))]
    hold = {}
    w_in_wait, first_token = _split_copies(
        "gather_w_in", first, _place_own(first), 4 * len(first), _gather_level1_plan(len(first)),
        start_after=(), wait_after=lambda: hold["h1"])

    stages = (("w_out", "w_pool_out", "w_attn_out", "w_up"), ("w_down",))
    stage_after = [None, None]
    stage_wait = []
    for n in (n for names in stages for n in names):
        shard[n] = _to_comm(n, given[n][0] + first_token[0, 0]).astype(BF16)

    def first_weights(h1):
        hold["h1"] = h1
        w_in_all, lin_all, conv_w_all = _gather_level2(w_in_wait()[1])
        w_in_full = full(w_in_all)
        lin_full = jnp.moveaxis(lin_all.reshape(N_DEV, 4, 32, POOL_GROUP_WIDTH), 0, 1).reshape(
            4, POOL_GROUP_WIDTH, POOL_GROUP_WIDTH).astype(BF16)
        conv_w_full = jnp.moveaxis(conv_w_all[:, :3, :], 0, 1).reshape(3, 2 * D_FF)
        order_after = (w_in_full,)
        for k, names in enumerate(stages):
            srcs = [shard[n] for n in names]
            wait, token = _split_copies(f"gather_stage{k}", srcs, _place_own(srcs), 7 * len(names),
                                        _gather_direct_plan(len(names)), start_after=order_after,
                                        wait_after=lambda k=k: stage_after[k])
            stage_wait.append(wait)
            order_after = (token,)
        return dict(w_in=w_in_full, w_pool_lin=lin_full, conv_w_full=conv_w_full), token

    def late_weights(stage, after):
        stage_after[stage] = after
        return {n: full(l) for n, l in zip(stages[stage], stage_wait[stage]()[1])}

    small = dict(g_mix=g_mix, b_gate=b_gate, pool_scale=pool_scale, g_ffn=g_ffn, conv_b=conv_b,
                 g_final=g_final.reshape(1, D_MODEL))

    pending = []
    done = {}

    def emit(group):
        names = list(group)
        parts = []
        for n in names:
            g = group[n]
            if n == "w_pool_lin":
                g = jnp.moveaxis(g.reshape(4, N_DEV, 32, POOL_GROUP_WIDTH), 1, 0).reshape(
                    N_DEV, 4 * 32, POOL_GROUP_WIDTH).astype(BF16)
            elif n == "conv_w":
                g = jnp.pad(jnp.moveaxis(g.reshape(3, N_DEV, 2 * D_FF // N_DEV), 1, 0),
                            ((0, 0), (0, CONV_ROWS - 3), (0, 0))).astype(BF16)
            else:
                g = g.reshape(N_DEV, g.shape[0] // N_DEV, g.shape[1])
            parts.append(g)
        me = _index(_position())
        lands = [lax.dynamic_update_slice(lax.empty(p.shape, p.dtype), lax.dynamic_index_in_dim(p, me, 0),
                                          (me, 0, 0)) for p in parts]
        wait, token = _split_copies(
            "exchange_" + names[0], parts, lands, 7 * len(names), _exchange_direct_plan(len(names)),
            start_after=(), wait_after=lambda: done["grad_x"])
        pending.append((names, wait))
        return token

    prepared = sum(shard[n][0, 0].astype(F32) for names in stages for n in names) * 0.0
    sq_cols, grad_x, _, small_grads = _local_step(x[0], loss_target[0], W, small,
                                                  first_token=first_token[0, 0] + prepared,
                                                  first_weights=first_weights, late_weights=late_weights, emit=emit)
    done["grad_x"] = grad_x
    loss = lax.psum(0.5 / D_MODEL * jnp.sum(sq_cols), AXES)

    res = {}
    for n, r in [(n, r) for names, wait in pending for n, r in zip(names, wait()[1])]:
        g = _sum_partials("sum_" + n, r)
        if n in _TRANSPOSED:
            g = g.T
        elif n == "conv_w":
            g = g[:3]
        shape = given[n].shape
        rows = math.prod(shape[:-1])
        w2, m2, v2 = (d[n].reshape(rows, shape[-1]) for d in (given, mom_m, mom_v))
        upd = _adamw_update("adamw_" + n, w2, g.reshape(rows, shape[-1]), m2, v2)
        res[n] = [a.reshape(shape) for a in (g,) + tuple(upd)]

    packed_small = [_pack_small(d) for d in (given, mom_m, mom_v)]
    res_small = [_unpack_small(p, given) for p in _small_allreduce_adamw(_pack_small(small_grads), *packed_small)]

    order = ["g_mix", "w_in", "b_gate", "w_pool_lin", "pool_scale", "w_pool_out", "w_attn_out", "w_out", "g_ffn",
             "w_up", "conv_w", "conv_b", "w_down", "g_final"]
    outs = [loss, grad_x[None]]
    for kind in range(4):
        for n in order:
            outs.append(res[n][kind] if n in res else res_small[kind][n])
    return tuple(outs)
```

```python
import math

import jax
import jax.numpy as jnp
from jax import lax
from jax.experimental import pallas as pl
from jax.experimental.pallas import tpu as pltpu

F32 = jnp.float32
BF16 = jnp.bfloat16
MESH = pl.DeviceIdType.MESH
AXES = ("x", "y", "c")
N_DEV = 8

D_MODEL = 2048
POOL_WINDOWS = (2, 4, 8, 16)
POOL_GROUP_WIDTH = 256
POOL_WIDTH = 1024
ATTN_GROUPS = ((128, 1), (512, 4), (2048, 16))
SPAN = 128
HEADS_PER_GROUP = 4
N_ATTN_HEADS = 12
HEAD_DIM = 128
ATTN_WIDTH = 1536
QKV_WIDTH = 3 * ATTN_WIDTH
ATTN_OUT_WIDTH = 512
IN_WIDTH = 9728
D_FF = 5632
RMS_EPS = 1e-6
ADAM_LR, ADAM_B1, ADAM_B2, ADAM_EPS, ADAM_WD, ADAM_STEP = 0.001, 0.9, 0.999, 1e-08, 0.01, 10

VMEM_LIMIT_BYTES = 48 * 1024 * 1024
HALO = 16
TOKEN_TK = 2048
MXU_DIM = 256
PACK_COLS = 1024
SMALL_ROWS = 24
NEG_BIG = -1e30

_DN = {"nn": (((1,), (0,)), ((), ())), "nt": (((1,), (1,)), ((), ())), "tn": (((0,), (0,)), ((), ()))}


def _params(*sem):
    return pltpu.CompilerParams(dimension_semantics=sem, vmem_limit_bytes=VMEM_LIMIT_BYTES)


def _dot(a, b, mode):
    return lax.dot_general(a.astype(BF16), b.astype(BF16), _DN[mode], preferred_element_type=F32)


def _mm(name, a, b, *, mode, grid, a_spec, b_spec, o_spec, o_shape, o_dtype, acc_shape, res=None, res_spec=None,
        after=None):
    nk = grid[2]
    n_in = 2 + (res is not None) + (after is not None)

    def body(*refs):
        a_ref, b_ref = refs[:2]
        r_ref = refs[2] if res is not None else None
        o_ref = refs[n_in]
        scr = refs[n_in + 1:]
        prod = _dot(a_ref[...], b_ref[...], mode)

        def finish(val):
            if r_ref is not None:
                val = val + r_ref[...]
            o_ref[...] = val.astype(o_ref.dtype)

        if nk == 1:
            finish(prod)
        else:
            acc = scr[0]
            k = pl.program_id(2)

            @pl.when(k == 0)
            def _():
                acc[...] = prod

            @pl.when(k > 0)
            def _():
                acc[...] += prod

            @pl.when(k == nk - 1)
            def _():
                finish(acc[...])

    in_specs = [a_spec, b_spec]
    args = [a, b]
    if res is not None:
        in_specs.append(res_spec)
        args.append(res)
    if after is not None:
        in_specs.append(pl.BlockSpec((8, 128), lambda i, j, k: (0, 0)))
        args.append(after)
    return pl.pallas_call(
        body, grid=grid, in_specs=in_specs, out_specs=o_spec,
        out_shape=jax.ShapeDtypeStruct(o_shape, o_dtype),
        scratch_shapes=[] if nk == 1 else [pltpu.VMEM(acc_shape, F32)],
        compiler_params=_params("parallel", "parallel", "arbitrary"), name=name)(*args)


def _mm_nn(name, a, b, *, tm, tn, tk, o_dtype, res=None, after=None):
    M, K = a.shape
    N = b.shape[1]
    return _mm(name, a, b, mode="nn", grid=(M // tm, N // tn, K // tk),
               a_spec=pl.BlockSpec((tm, tk), lambda i, j, k: (i, k)),
               b_spec=pl.BlockSpec((tk, tn), lambda i, j, k: (k, j)),
               o_spec=pl.BlockSpec((tm, tn), lambda i, j, k: (i, j)),
               o_shape=(M, N), o_dtype=o_dtype, acc_shape=(tm, tn), res=res,
               res_spec=pl.BlockSpec((tm, tn), lambda i, j, k: (i, j)), after=after)


def _mm_nt(name, a, b, *, tm, tn, tk, o_dtype, n_rows=None, b_row_off=0, after=None):
    M, K = a.shape
    N = b.shape[0] if n_rows is None else n_rows
    off = b_row_off // tn
    return _mm(name, a, b, mode="nt", grid=(M // tm, N // tn, K // tk),
               a_spec=pl.BlockSpec((tm, tk), lambda i, j, k: (i, k)),
               b_spec=pl.BlockSpec((tn, tk), lambda i, j, k: (j + off, k)),
               o_spec=pl.BlockSpec((tm, tn), lambda i, j, k: (i, j)),
               o_shape=(M, N), o_dtype=o_dtype, acc_shape=(tm, tn), after=after)


def _mm_tn(name, a, b, *, tm, tn, tk):
    K, M = a.shape
    N = b.shape[1]
    return _mm(name, a, b, mode="tn", grid=(M // tm, N // tn, K // tk),
               a_spec=pl.BlockSpec((tk, tm), lambda i, j, k: (k, i)),
               b_spec=pl.BlockSpec((tk, tn), lambda i, j, k: (k, j)),
               o_spec=pl.BlockSpec((tm, tn), lambda i, j, k: (i, j)),
               o_shape=(M, N), o_dtype=BF16, acc_shape=(tm, tn))


def _rms_fwd(name, x, g):
    S, D = x.shape
    T = 512

    def body(x_ref, g_ref, h_ref):
        xv = x_ref[...]
        r = lax.rsqrt(jnp.mean(xv * xv, axis=-1, keepdims=True) + RMS_EPS)
        h_ref[...] = (xv * r * g_ref[...]).astype(BF16)

    return pl.pallas_call(
        body, grid=(S // T,),
        in_specs=[pl.BlockSpec((T, D), lambda i: (i, 0)), pl.BlockSpec((1, D), lambda i: (0, 0))],
        out_specs=pl.BlockSpec((T, D), lambda i: (i, 0)),
        out_shape=jax.ShapeDtypeStruct((S, D), BF16), compiler_params=_params("parallel"), name=name)(x, g)


def _out_proj_norm(mixed, w_out, x, g):
    S, D = x.shape
    tm = 512

    def body(m_ref, w_ref, x_ref, g_ref, x1_ref, h_ref):
        x1 = x_ref[...] + _dot(m_ref[...], w_ref[...], "nn")
        x1_ref[...] = x1
        r = lax.rsqrt(jnp.mean(x1 * x1, axis=-1, keepdims=True) + RMS_EPS)
        h_ref[...] = (x1 * r * g_ref[...]).astype(BF16)

    row = pl.BlockSpec((tm, D), lambda i: (i, 0))
    return pl.pallas_call(
        body, grid=(S // tm,),
        in_specs=[row, pl.BlockSpec((D, D), lambda i: (0, 0)), row, pl.BlockSpec((1, D), lambda i: (0, 0))],
        out_specs=[row, row], out_shape=[jax.ShapeDtypeStruct((S, D), F32), jax.ShapeDtypeStruct((S, D), BF16)],
        compiler_params=_params("parallel"), name="out_proj_norm")(mixed, w_out, x, g)


def _rms_bwd(name, x, g, dh, dres, with_bf16):
    S, D = x.shape
    T = 256

    def body(x_ref, g_ref, dh_ref, dres_ref, dx_ref, *rest):
        dg_ref = rest[-1]
        i = pl.program_id(0)
        xv = x_ref[...]
        r = lax.rsqrt(jnp.mean(xv * xv, axis=-1, keepdims=True) + RMS_EPS)
        xhat = xv * r
        dhv = dh_ref[...].astype(F32)
        gdh = dhv * g_ref[...]
        c = jnp.mean(xhat * gdh, axis=-1, keepdims=True)
        dx = dres_ref[...] + r * (gdh - xhat * c)
        dx_ref[...] = dx
        if with_bf16:
            rest[0][...] = dx.astype(BF16)
        part = jnp.sum(dhv * xhat, axis=0, keepdims=True)

        @pl.when(i == 0)
        def _():
            dg_ref[...] = part

        @pl.when(i > 0)
        def _():
            dg_ref[...] += part

    row = pl.BlockSpec((T, D), lambda i: (i, 0))
    vec = pl.BlockSpec((1, D), lambda i: (0, 0))
    narrow = [jax.ShapeDtypeStruct((S, D), BF16)] if with_bf16 else []
    return pl.pallas_call(
        body, grid=(S // T,), in_specs=[row, vec, row, row], out_specs=[row] * (1 + len(narrow)) + [vec],
        out_shape=[jax.ShapeDtypeStruct((S, D), F32)] + narrow + [jax.ShapeDtypeStruct((1, D), F32)],
        compiler_params=_params("arbitrary"), name=name)(x, g, dh, dres)


def _final_loss(name, x, g, target):
    S, D = x.shape
    T = 256

    def body(x_ref, g_ref, t_ref, dx_ref, dxb_ref, dg_ref, sq_ref):
        i = pl.program_id(0)
        xv = x_ref[...]
        r = lax.rsqrt(jnp.mean(xv * xv, axis=-1, keepdims=True) + RMS_EPS)
        xhat = xv * r
        err = xhat * g_ref[...] - t_ref[...]
        dy = err * (1.0 / D)
        gdy = dy * g_ref[...]
        c = jnp.mean(xhat * gdy, axis=-1, keepdims=True)
        dx = r * (gdy - xhat * c)
        dx_ref[...] = dx
        dxb_ref[...] = dx.astype(BF16)
        dg_part = jnp.sum(dy * xhat, axis=0, keepdims=True)
        sq_part = jnp.sum(err * err, axis=0, keepdims=True)

        @pl.when(i == 0)
        def _():
            dg_ref[...] = dg_part
            sq_ref[...] = sq_part

        @pl.when(i > 0)
        def _():
            dg_ref[...] += dg_part
            sq_ref[...] += sq_part

    row = pl.BlockSpec((T, D), lambda i: (i, 0))
    vec = pl.BlockSpec((1, D), lambda i: (0, 0))
    return pl.pallas_call(
        body, grid=(S // T,), in_specs=[row, vec, row], out_specs=[row, row, vec, vec],
        out_shape=[jax.ShapeDtypeStruct((S, D), F32), jax.ShapeDtypeStruct((S, D), BF16),
                   jax.ShapeDtypeStruct((1, D), F32), jax.ShapeDtypeStruct((1, D), F32)],
        compiler_params=_params("arbitrary"), name=name)(x, g, target)


def _prev_halo_spec(T, C, col_map):
    per = T // HALO
    return pl.BlockSpec((HALO, C), lambda *g: (jnp.maximum(g[0] * per - 1, 0), col_map(*g)))


def _pool_window_sums(ext, T, i):
    t = i * T + lax.broadcasted_iota(jnp.int32, (T, 1), 0)
    outs = []
    cnts = []
    for gi, w in enumerate(POOL_WINDOWS):
        s = ext[:, gi * POOL_GROUP_WIDTH:(gi + 1) * POOL_GROUP_WIDTH]
        sh = 1
        while sh < w:
            s = s + pltpu.roll(s, sh, 0)
            sh *= 2
        cnt = jnp.minimum(t + 1, w).astype(F32)
        outs.append(s[HALO:] / cnt - ext[HALO:, gi * POOL_GROUP_WIDTH:(gi + 1) * POOL_GROUP_WIDTH])
        cnts.append(cnt)
    return outs, cnts


def _load_ext(ext_ref, halo_ref, blk_ref, first):
    ext_ref[0:HALO, :] = jnp.where(first, 0.0, halo_ref[...].astype(F32))
    ext_ref[HALO:, :] = blk_ref[...].astype(F32)


def _pool_fwd(u, w_lin, scale):
    S = u.shape[0]
    T = 512
    W = POOL_WIDTH

    def body(u_ref, halo_ref, wl_ref, sc_ref, y_ref, ext_ref):
        i = pl.program_id(0)
        _load_ext(ext_ref, halo_ref, u_ref, i == 0)
        pooled, _ = _pool_window_sums(ext_ref[...], T, i)
        for gi in range(4):
            z = _dot(pooled[gi], wl_ref[gi], "nn")
            cs = slice(gi * POOL_GROUP_WIDTH, (gi + 1) * POOL_GROUP_WIDTH)
            y_ref[:, cs] = (z * sc_ref[:, cs]).astype(BF16)

    return pl.pallas_call(
        body, grid=(S // T,),
        in_specs=[pl.BlockSpec((T, W), lambda i: (i, 0)), _prev_halo_spec(T, W, lambda i: 0),
                  pl.BlockSpec((4, 256, 256), lambda i: (0, 0, 0)), pl.BlockSpec((1, W), lambda i: (0, 0))],
        out_specs=pl.BlockSpec((T, W), lambda i: (i, 0)),
        out_shape=jax.ShapeDtypeStruct((S, W), BF16),
        scratch_shapes=[pltpu.VMEM((HALO + T, W), F32)],
        compiler_params=_params("parallel"), name="pool_fwd")(u, u, w_lin, scale)


def _pool_bwd_a(u, dy_out, w_pool_out_t, w_lin, scale):
    S = u.shape[0]
    T = 512
    W = POOL_WIDTH

    def body(u_ref, halo_ref, dyo_ref, wpo_ref, wl_ref, sc_ref, e_ref, dwl_ref, dsc_ref, ext_ref):
        i = pl.program_id(0)
        dy = _dot(dyo_ref[...], wpo_ref[...], "nn")
        _load_ext(ext_ref, halo_ref, u_ref, i == 0)
        pooled, cnts = _pool_window_sums(ext_ref[...], T, i)
        for gi in range(4):
            cs = slice(gi * POOL_GROUP_WIDTH, (gi + 1) * POOL_GROUP_WIDTH)
            z = _dot(pooled[gi], wl_ref[gi], "nn")
            dyg = dy[:, cs]
            dsc = jnp.sum(dyg * z, axis=0, keepdims=True)
            dz = dyg * sc_ref[:, cs]
            dwl = _dot(pooled[gi], dz, "tn")
            dpooled = _dot(dz, wl_ref[gi], "nt")
            e_ref[:, cs] = dpooled / cnts[gi]

            @pl.when(i == 0)
            def _():
                dwl_ref[gi] = dwl
                dsc_ref[:, cs] = dsc

            @pl.when(i > 0)
            def _():
                dwl_ref[gi] += dwl
                dsc_ref[:, cs] += dsc

    return pl.pallas_call(
        body, grid=(S // T,),
        in_specs=[pl.BlockSpec((T, W), lambda i: (i, 0)), _prev_halo_spec(T, W, lambda i: 0),
                  pl.BlockSpec((T, D_MODEL), lambda i: (i, 0)), pl.BlockSpec((D_MODEL, W), lambda i: (0, 0)),
                  pl.BlockSpec((4, 256, 256), lambda i: (0, 0, 0)), pl.BlockSpec((1, W), lambda i: (0, 0))],
        out_specs=[pl.BlockSpec((T, W), lambda i: (i, 0)), pl.BlockSpec((4, 256, 256), lambda i: (0, 0, 0)),
                   pl.BlockSpec((1, W), lambda i: (0, 0))],
        out_shape=[jax.ShapeDtypeStruct((S, W), F32), jax.ShapeDtypeStruct((4, 256, 256), F32),
                   jax.ShapeDtypeStruct((1, W), F32)],
        scratch_shapes=[pltpu.VMEM((HALO + T, W), F32)],
        compiler_params=_params("arbitrary"), name="pool_bwd_a")(u, u, dy_out, w_pool_out_t, w_lin, scale)


def _pool_bwd_b(e):
    S = e.shape[0]
    T = 512
    W = POOL_WIDTH
    per = T // HALO
    n_blk = S // T

    def body(e_ref, halo_ref, du_ref, ext_ref):
        i = pl.program_id(0)
        ext_ref[0:T, :] = e_ref[...]
        ext_ref[T:, :] = jnp.where(i == n_blk - 1, 0.0, halo_ref[...])
        ext = ext_ref[...]
        n = T + HALO
        t = i * T + lax.broadcasted_iota(jnp.int32, (T, 1), 0)
        for gi, w in enumerate(POOL_WINDOWS):
            cs = slice(gi * POOL_GROUP_WIDTH, (gi + 1) * POOL_GROUP_WIDTH)
            s = ext[:, cs]
            sh = 1
            while sh < w:
                s = s + pltpu.roll(s, n - sh, 0)
                sh *= 2
            cnt = jnp.minimum(t + 1, w).astype(F32)
            du_ref[:, cs] = (s[0:T] - ext[0:T, cs] * cnt).astype(BF16)

    return pl.pallas_call(
        body, grid=(n_blk,),
        in_specs=[pl.BlockSpec((T, W), lambda i: (i, 0)),
                  pl.BlockSpec((HALO, W), lambda i: (jnp.minimum((i + 1) * per, S // HALO - 1), 0))],
        out_specs=pl.BlockSpec((T, W), lambda i: (i, 0)),
        out_shape=jax.ShapeDtypeStruct((S, W), BF16),
        scratch_shapes=[pltpu.VMEM((T + HALO, W), F32)],
        compiler_params=_params("parallel"), name="pool_bwd_b")(e, e)


def _slope(head):
    return 2.0 ** (-8.0 * (head + 1) / N_ATTN_HEADS)


def _attn_masks(b):
    row = lax.broadcasted_iota(jnp.int32, (SPAN, SPAN), 0)
    col = lax.broadcasted_iota(jnp.int32, (SPAN, SPAN), 1)
    dist_cur = (row - col).astype(F32)
    valid_cur = col <= row
    valid_prev = jnp.logical_and(col >= row, b > 0)
    return dist_cur, valid_cur, valid_prev


ATTN_BATCH = 4


def _attn_tiling(gi, S):
    d = ATTN_GROUPS[gi][1]
    hp = HEADS_PER_GROUP if d < 16 else 1
    return d, SPAN * d, S // (SPAN * d), hp, hp * HEAD_DIM, ATTN_BATCH // hp


_BATCHED_DN = {"nt": (((2,), (2,)), ((0,), (0,))), "nn": (((2,), (1,)), ((0,), (0,))), "tn": (((1,), (1,)), ((0,), (0,)))}


def _bdot(a, b, mode):
    return lax.dot_general(a.astype(BF16), b.astype(BF16), _BATCHED_DN[mode], preferred_element_type=F32)


def _batch_rows(it, d, rb):
    return [pl.ds(it * rb + j, SPAN, stride=d) for j in range(rb)]


def _take(scr, hp, rows):
    return jnp.stack([scr[hi, rw, :] for hi in range(hp) for rw in rows])


def _put(scr, hp, rows, val):
    for n, (hi, rw) in enumerate((hi, rw) for hi in range(hp) for rw in rows):
        scr[hi, rw, :] = val[n]


def _batch_bias(gi, d, hp, rb, dist):
    return jnp.stack([(-(_head_slope(gi, hp, hi) * d)) * dist for hi in range(hp) for _ in range(rb)])


def _head_slope(gi, hp, hi):
    if hp == HEADS_PER_GROUP:
        return _slope(gi * HEADS_PER_GROUP + hi)
    first = pl.program_id(1) * hp
    sl = jnp.float32(_slope(gi * HEADS_PER_GROUP + hi))
    for h0 in range(hp, HEADS_PER_GROUP, hp):
        sl = jnp.where(first == h0, jnp.float32(_slope(gi * HEADS_PER_GROUP + h0 + hi)), sl)
    return sl


def _split_heads(src_ref, dst_ref, hp):
    for hi in range(hp):
        dst_ref[hi] = src_ref[:, hi * HEAD_DIM:(hi + 1) * HEAD_DIM].astype(F32)


def _attn_fwd(gi, qkv):
    S = qkv.shape[0]
    d, C, nb, hp, W, rb = _attn_tiling(gi, S)
    scale = HEAD_DIM ** -0.5

    def body(q_ref, kc_ref, kp_ref, vc_ref, vp_ref, o_ref, lse_ref, qs, kcs, kps, vcs, vps, os, ls):
        b = pl.program_id(0)
        for src, dst in ((q_ref, qs), (kc_ref, kcs), (kp_ref, kps), (vc_ref, vcs), (vp_ref, vps)):
            _split_heads(src, dst, hp)
        dist_cur, valid_cur, valid_prev = _attn_masks(b)
        bias_cur = _batch_bias(gi, d, hp, rb, dist_cur)
        bias_far = _batch_bias(gi, d, hp, rb, dist_cur + SPAN)

        def sub_sequences(it, carry):
            rows = _batch_rows(it, d, rb)
            q = _take(qs, hp, rows)
            s_cur = jnp.where(valid_cur, _bdot(q, _take(kcs, hp, rows), "nt") * scale + bias_cur, NEG_BIG)
            s_prev = jnp.where(valid_prev, _bdot(q, _take(kps, hp, rows), "nt") * scale + bias_far, NEG_BIG)
            m = jnp.maximum(jnp.max(s_cur, axis=-1, keepdims=True), jnp.max(s_prev, axis=-1, keepdims=True))
            p_cur = jnp.exp(s_cur - m)
            p_prev = jnp.exp(s_prev - m)
            l = jnp.sum(p_cur, axis=-1, keepdims=True) + jnp.sum(p_prev, axis=-1, keepdims=True)
            o = (_bdot(p_cur, _take(vcs, hp, rows), "nn") + _bdot(p_prev, _take(vps, hp, rows), "nn")) / l
            _put(os, hp, rows, o)
            _put(ls, hp, rows, jnp.broadcast_to(m + jnp.log(l), o.shape))
            return carry

        lax.fori_loop(0, d // rb, sub_sequences, 0)
        for hi in range(hp):
            hs = slice(hi * HEAD_DIM, (hi + 1) * HEAD_DIM)
            o_ref[:, hs] = os[hi]
            lse_ref[:, hs] = ls[hi]

    per_sec = ATTN_WIDTH // W
    per_grp = ATTN_OUT_WIDTH // W

    def spec(sec, shift):
        return pl.BlockSpec((C, W), lambda b, h: (jnp.maximum(b + shift, 0), sec * per_sec + gi * per_grp + h))

    out = pl.BlockSpec((C, W), lambda b, h: (b, h))
    scr = pltpu.VMEM((hp, C, HEAD_DIM), F32)
    return pl.pallas_call(
        body, grid=(nb, HEADS_PER_GROUP // hp),
        in_specs=[spec(0, 0), spec(1, 0), spec(1, -1), spec(2, 0), spec(2, -1)], out_specs=[out, out],
        out_shape=[jax.ShapeDtypeStruct((S, ATTN_OUT_WIDTH), F32)] * 2, scratch_shapes=[scr] * 7,
        compiler_params=_params("parallel", "parallel"), name=f"attn_fwd_g{gi}")(qkv, qkv, qkv, qkv, qkv)


def _attn_bwd(gi, qkv, do, lse, dp):
    S = qkv.shape[0]
    d, C, nb, hp, W, rb = _attn_tiling(gi, S)
    scale = HEAD_DIM ** -0.5
    n_in = 12

    def body(*refs):
        ins, (dq_ref, dk_ref, dv_ref) = refs[:n_in], refs[n_in:n_in + 3]
        scr = refs[n_in + 3:]
        for src, dst in zip(ins, scr[:n_in]):
            _split_heads(src, dst, hp)
        qs, kcs, vcs, kps, vps, qns, dos, dons, lses, lsens, dps, dpns = scr[:n_in]
        dqs, dks, dvs = scr[n_in:]
        b = pl.program_id(0)
        dist_cur, valid_cur, valid_prev = _attn_masks(b)
        row = lax.broadcasted_iota(jnp.int32, (SPAN, SPAN), 0)
        col = lax.broadcasted_iota(jnp.int32, (SPAN, SPAN), 1)
        valid_next = jnp.logical_and(col >= row, b < nb - 1)

        bias_cur = _batch_bias(gi, d, hp, rb, dist_cur)
        bias_far = _batch_bias(gi, d, hp, rb, dist_cur + SPAN)

        def sub_sequences(it, carry):
            rows = _batch_rows(it, d, rb)
            q, kc, vc, kp, vp, qn, dov, don = (_take(s, hp, rows) for s in (qs, kcs, vcs, kps, vps, qns, dos, dons))
            lse_b, lse_n, dp_b, dp_n = (_take(s, hp, rows) for s in (lses, lsens, dps, dpns))
            p_cur = jnp.where(valid_cur, jnp.exp(_bdot(q, kc, "nt") * scale + bias_cur - lse_b), 0.0)
            p_prev = jnp.where(valid_prev, jnp.exp(_bdot(q, kp, "nt") * scale + bias_far - lse_b), 0.0)
            ds_cur = p_cur * (_bdot(dov, vc, "nt") - dp_b)
            ds_prev = p_prev * (_bdot(dov, vp, "nt") - dp_b)
            _put(dqs, hp, rows, (_bdot(ds_cur, kc, "nn") + _bdot(ds_prev, kp, "nn")) * scale)
            p_next = jnp.where(valid_next, jnp.exp(_bdot(qn, kc, "nt") * scale + bias_far - lse_n), 0.0)
            ds_next = p_next * (_bdot(don, vc, "nt") - dp_n)
            _put(dvs, hp, rows, _bdot(p_cur, dov, "tn") + _bdot(p_next, don, "tn"))
            _put(dks, hp, rows, (_bdot(ds_cur, q, "tn") + _bdot(ds_next, qn, "tn")) * scale)
            return carry

        lax.fori_loop(0, d // rb, sub_sequences, 0)
        for hi in range(hp):
            hs = slice(hi * HEAD_DIM, (hi + 1) * HEAD_DIM)
            dq_ref[:, hs] = dqs[hi].astype(BF16)
            dk_ref[:, hs] = dks[hi].astype(BF16)
            dv_ref[:, hs] = dvs[hi].astype(BF16)

    per_sec = ATTN_WIDTH // W
    per_grp = ATTN_OUT_WIDTH // W

    def qkv_spec(sec, shift):
        return pl.BlockSpec((C, W), lambda b, h: (jnp.clip(b + shift, 0, nb - 1), sec * per_sec + gi * per_grp + h))

    def act_spec(shift):
        return pl.BlockSpec((C, W), lambda b, h: (jnp.clip(b + shift, 0, nb - 1), h))

    out = act_spec(0)
    scr = pltpu.VMEM((hp, C, HEAD_DIM), F32)
    return pl.pallas_call(
        body, grid=(nb, HEADS_PER_GROUP // hp),
        in_specs=[qkv_spec(0, 0), qkv_spec(1, 0), qkv_spec(2, 0), qkv_spec(1, -1), qkv_spec(2, -1), qkv_spec(0, 1),
                  act_spec(0), act_spec(1), act_spec(0), act_spec(1), act_spec(0), act_spec(1)],
        out_specs=[out, out, out], out_shape=[jax.ShapeDtypeStruct((S, ATTN_OUT_WIDTH), BF16)] * 3,
        scratch_shapes=[scr] * (n_in + 3),
        compiler_params=_params("parallel", "parallel"), name=f"attn_bwd_g{gi}")(
            qkv, qkv, qkv, qkv, qkv, qkv, do, do, lse, lse, dp, dp)


def _group_weights(l_refs):
    l0, l1, l2 = (r[...] for r in l_refs)
    m = jnp.maximum(jnp.maximum(l0, l1), l2)
    e = [jnp.exp(l0 - m), jnp.exp(l1 - m), jnp.exp(l2 - m)]
    tot = e[0] + e[1] + e[2]
    return [ei / tot for ei in e]


def _merge_fwd(outs, lses):
    S, W = outs[0].shape
    T = 512

    def body(o0, o1, o2, l0, l1, l2, y_ref):
        w = _group_weights((l0, l1, l2))
        y_ref[...] = (w[0] * o0[...] + w[1] * o1[...] + w[2] * o2[...]).astype(BF16)

    blk = pl.BlockSpec((T, W), lambda i: (i, 0))
    return pl.pallas_call(
        body, grid=(S // T,), in_specs=[blk] * 6, out_specs=blk, out_shape=jax.ShapeDtypeStruct((S, W), BF16),
        compiler_params=_params("parallel"), name="merge_fwd")(*outs, *lses)


def _merge_bwd(dy_out, w_attn_out_t, outs, lses):
    S, W = outs[0].shape
    T = 512

    def body(dyo_ref, wao_ref, o0, o1, o2, l0, l1, l2, do0, do1, do2, dp0, dp1, dp2):
        w = _group_weights((l0, l1, l2))
        dyv = _dot(dyo_ref[...], wao_ref[...], "nn")
        y = w[0] * o0[...] + w[1] * o1[...] + w[2] * o2[...]
        prod = dyv * y
        dots = []
        for hi in range(HEADS_PER_GROUP):
            hs = slice(hi * HEAD_DIM, (hi + 1) * HEAD_DIM)
            dots.append(jnp.broadcast_to(jnp.sum(prod[:, hs], axis=-1, keepdims=True), (T, HEAD_DIM)))
        dot_b = jnp.concatenate(dots, axis=1)
        for wg, do_ref, dp_ref in zip(w, (do0, do1, do2), (dp0, dp1, dp2)):
            do_ref[...] = (wg * dyv).astype(BF16)
            dp_ref[...] = wg * dot_b

    blk = pl.BlockSpec((T, W), lambda i: (i, 0))
    res = pl.pallas_call(
        body, grid=(S // T,),
        in_specs=[pl.BlockSpec((T, D_MODEL), lambda i: (i, 0)), pl.BlockSpec((D_MODEL, W), lambda i: (0, 0))] + [blk] * 6,
        out_specs=[blk] * 6,
        out_shape=[jax.ShapeDtypeStruct((S, W), BF16)] * 3 + [jax.ShapeDtypeStruct((S, W), F32)] * 3,
        compiler_params=_params("parallel"), name="merge_bwd")(dy_out, w_attn_out_t, *outs, *lses)
    return res[:3], res[3:]


def _mix_fwd(pool_y, attn_y, w_pool_out_t, w_attn_out_t, gates_pre, b_gate):
    S = pool_y.shape[0]
    D = D_MODEL
    tm, tn = 1024, 512
    nj = D // tn

    def body(p_ref, a_ref, wp_ref, wa_ref, g0_ref, g1_ref, b0_ref, b1_ref, mixed_ref, yp_ref, ya_ref):
        yp = _dot(p_ref[...], wp_ref[...], "nt")
        ya = _dot(a_ref[...], wa_ref[...], "nt")
        g0 = jax.nn.sigmoid(g0_ref[...] + b0_ref[...])
        g1 = jax.nn.sigmoid(g1_ref[...] + b1_ref[...])
        mixed_ref[...] = (g0 * yp + g1 * ya).astype(BF16)
        yp_ref[...] = yp.astype(BF16)
        ya_ref[...] = ya.astype(BF16)

    out = pl.BlockSpec((tm, tn), lambda i, j: (i, j))
    return pl.pallas_call(
        body, grid=(S // tm, nj),
        in_specs=[pl.BlockSpec((tm, POOL_WIDTH), lambda i, j: (i, 0)),
                  pl.BlockSpec((tm, ATTN_OUT_WIDTH), lambda i, j: (i, 0)),
                  pl.BlockSpec((tn, POOL_WIDTH), lambda i, j: (j, 0)),
                  pl.BlockSpec((tn, ATTN_OUT_WIDTH), lambda i, j: (j, 0)),
                  pl.BlockSpec((tm, tn), lambda i, j: (i, j)), pl.BlockSpec((tm, tn), lambda i, j: (i, j + nj)),
                  pl.BlockSpec((1, tn), lambda i, j: (0, j)), pl.BlockSpec((1, tn), lambda i, j: (0, j + nj))],
        out_specs=[out, out, out], out_shape=[jax.ShapeDtypeStruct((S, D), BF16)] * 3,
        compiler_params=_params("parallel", "parallel"), name="mix_fwd")(
            pool_y, attn_y, w_pool_out_t, w_attn_out_t, gates_pre, gates_pre, b_gate, b_gate)


def _mix_bwd(dx, w_out, gates_pre, b_gate, y_pool, y_attn):
    S = dx.shape[0]
    D = D_MODEL
    T = 256

    def body(dx_ref, w_ref, gp_ref, b_ref, yp_ref, ya_ref, dyp_ref, dya_ref, dg_ref, db_ref):
        i = pl.program_id(0)
        dm = _dot(dx_ref[...], w_ref[...], "nt")
        parts = []
        for br, (y_ref, dy_ref) in enumerate(((yp_ref, dyp_ref), (ya_ref, dya_ref))):
            cs = slice(br * D, (br + 1) * D)
            g = jax.nn.sigmoid(gp_ref[:, cs] + b_ref[:, cs])
            dy_ref[...] = (dm * g).astype(BF16)
            dpre = dm * y_ref[...].astype(F32) * g * (1.0 - g)
            dg_ref[:, cs] = dpre.astype(BF16)
            parts.append(jnp.sum(dpre, axis=0, keepdims=True))

        @pl.when(i == 0)
        def _():
            db_ref[:, 0:D] = parts[0]
            db_ref[:, D:2 * D] = parts[1]

        @pl.when(i > 0)
        def _():
            db_ref[:, 0:D] += parts[0]
            db_ref[:, D:2 * D] += parts[1]

    row = pl.BlockSpec((T, D), lambda i: (i, 0))
    row2 = pl.BlockSpec((T, 2 * D), lambda i: (i, 0))
    vec2 = pl.BlockSpec((1, 2 * D), lambda i: (0, 0))
    return pl.pallas_call(
        body, grid=(S // T,), in_specs=[row, pl.BlockSpec((D, D), lambda i: (0, 0)), row2, vec2, row, row],
        out_specs=[row, row, row2, vec2],
        out_shape=[jax.ShapeDtypeStruct((S, D), BF16), jax.ShapeDtypeStruct((S, D), BF16),
                   jax.ShapeDtypeStruct((S, 2 * D), BF16), jax.ShapeDtypeStruct((1, 2 * D), F32)],
        compiler_params=_params("arbitrary"), name="mix_bwd")(dx, w_out, gates_pre, b_gate, y_pool, y_attn)


_CONV_T, _CONV_C = 512, 512


def _conv_taps(ext_ref, h, w_ref, b_ref, cols=slice(None)):
    ext = ext_ref[h, :, cols]
    x2 = pltpu.roll(ext, 2, 0)[HALO:]
    x1 = pltpu.roll(ext, 1, 0)[HALO:]
    x0 = ext[HALO:]
    y = b_ref[:, cols] + w_ref[0:1, cols] * x2
    y = y + w_ref[1:2, cols] * x1
    y = y + w_ref[2:3, cols] * x0
    return y, (x2, x1, x0)


def _gelu_parts(a):
    cdf = 0.5 * (1.0 + lax.erf(a * (1.0 / math.sqrt(2.0))))
    return cdf, a * cdf


def _conv_specs(S):
    T, C = _CONV_T, _CONV_C
    nj = D_FF // C
    blk = pl.BlockSpec((2, T, C), lambda i, j: (0, i, j))
    per = T // HALO
    halo = pl.BlockSpec((2, HALO, C), lambda i, j: (0, jnp.maximum(i * per - 1, 0), j))
    w_a = pl.BlockSpec((3, C), lambda i, j: (0, j))
    w_b = pl.BlockSpec((3, C), lambda i, j: (0, j + nj))
    b_a = pl.BlockSpec((1, C), lambda i, j: (0, j))
    b_b = pl.BlockSpec((1, C), lambda i, j: (0, j + nj))
    return T, C, nj, blk, halo, w_a, w_b, b_a, b_b


def _conv_load(ext_ref, halo_ref, up_ref, first):
    for h in range(2):
        ext_ref[h, 0:HALO, :] = jnp.where(first, 0.0, halo_ref[h].astype(F32))
        ext_ref[h, HALO:, :] = up_ref[h].astype(F32)


def _up_proj_convffn(h, w_up_t, conv_w, conv_b):
    S = h.shape[0]
    tm, C = 1024, _CONV_C
    nj = D_FF // C

    def body(h_ref, wa_ref, wb_ref, cwa_ref, cwb_ref, cba_ref, cbb_ref, up_ref, act_ref, ext_ref, carry_ref):
        i, j = pl.program_id(0), pl.program_id(1)
        for half, w_ref in enumerate((wa_ref, wb_ref)):
            pre = _dot(h_ref[...], w_ref[...], "nt").astype(BF16)
            up_ref[half] = pre
            ext_ref[half, 0:HALO, :] = jnp.where(i == 0, 0.0, carry_ref[j, half])
            ext_ref[half, HALO:, :] = pre.astype(F32)
            carry_ref[j, half] = ext_ref[half, tm:tm + HALO, :]
        a, _ = _conv_taps(ext_ref, 0, cwa_ref, cba_ref)
        b, _ = _conv_taps(ext_ref, 1, cwb_ref, cbb_ref)
        _, gelu = _gelu_parts(a)
        act_ref[...] = (gelu * b).astype(BF16)

    def w_spec(half):
        return pl.BlockSpec((C, D_MODEL), lambda i, j: (j + half * nj, 0))

    def c_spec(rows, half):
        return pl.BlockSpec((rows, C), lambda i, j: (0, j + half * nj))

    return pl.pallas_call(
        body, grid=(S // tm, nj),
        in_specs=[pl.BlockSpec((tm, D_MODEL), lambda i, j: (i, 0)), w_spec(0), w_spec(1),
                  c_spec(3, 0), c_spec(3, 1), c_spec(1, 0), c_spec(1, 1)],
        out_specs=[pl.BlockSpec((2, tm, C), lambda i, j: (0, i, j)), pl.BlockSpec((tm, C), lambda i, j: (i, j))],
        out_shape=[jax.ShapeDtypeStruct((2, S, D_FF), BF16), jax.ShapeDtypeStruct((S, D_FF), BF16)],
        scratch_shapes=[pltpu.VMEM((2, HALO + tm, C), F32), pltpu.VMEM((nj, 2, HALO, C), F32)],
        compiler_params=_params("arbitrary", "arbitrary"), name="up_proj_convffn")(
            h, w_up_t, w_up_t, conv_w, conv_w, conv_b, conv_b)


def _convffn_bwd_a(up_pre, dx, w_down, conv_w, conv_b):
    S = up_pre.shape[1]
    T, C, nj, blk, halo, w_a, w_b, b_a, b_b = _conv_specs(S)
    def swap(spec):
        return pl.BlockSpec(spec.block_shape, lambda j, i, _m=spec.index_map: _m(i, j))

    def body(up_ref, halo_ref, dx_ref, wd_ref, wa_ref, wb_ref, ba_ref, bb_ref, dup_ref, dwa_ref, dwb_ref, dba_ref,
             dbb_ref, ext_ref):
        i = pl.program_id(1)
        _conv_load(ext_ref, halo_ref, up_ref, i == 0)

        @pl.when(i == 0)
        def _():
            for acc_ref in (dwa_ref, dwb_ref, dba_ref, dbb_ref):
                acc_ref[...] = jnp.zeros_like(acc_ref)

        for c in range(C // MXU_DIM):
            cs = slice(c * MXU_DIM, (c + 1) * MXU_DIM)
            dact = _dot(dx_ref[...], wd_ref[cs, :], "nt")
            a, xa = _conv_taps(ext_ref, 0, wa_ref, ba_ref, cs)
            b, xb = _conv_taps(ext_ref, 1, wb_ref, bb_ref, cs)
            cdf, gelu = _gelu_parts(a)
            dgelu = cdf + a * (jnp.exp(-0.5 * a * a) * (1.0 / math.sqrt(2.0 * math.pi)))
            da = dact * b * dgelu
            db = dact * gelu
            dup_ref[0, :, cs] = da.astype(BF16)
            dup_ref[1, :, cs] = db.astype(BF16)
            for dval, xs, dw_ref, dbias_ref in ((da, xa, dwa_ref, dba_ref), (db, xb, dwb_ref, dbb_ref)):
                dw_ref[:, cs] += jnp.concatenate([jnp.sum(dval * xk, axis=0, keepdims=True) for xk in xs], axis=0)
                dbias_ref[:, cs] += jnp.sum(dval, axis=0, keepdims=True)

    dx_spec = pl.BlockSpec((T, D_MODEL), lambda j, i: (i, 0))
    wd_spec = pl.BlockSpec((C, D_MODEL), lambda j, i: (j, 0))
    dw_spec = pl.BlockSpec((3, C), lambda j, i: (0, j))
    db_spec = pl.BlockSpec((1, C), lambda j, i: (0, j))
    res = pl.pallas_call(
        body, grid=(nj, S // T),
        in_specs=[swap(blk), swap(halo), dx_spec, wd_spec, swap(w_a), swap(w_b), swap(b_a), swap(b_b)],
        out_specs=[swap(blk), dw_spec, dw_spec, db_spec, db_spec],
        out_shape=[jax.ShapeDtypeStruct((2, S, D_FF), BF16), jax.ShapeDtypeStruct((3, D_FF), F32),
                   jax.ShapeDtypeStruct((3, D_FF), F32), jax.ShapeDtypeStruct((1, D_FF), F32),
                   jax.ShapeDtypeStruct((1, D_FF), F32)],
        scratch_shapes=[pltpu.VMEM((2, HALO + T, C), F32)],
        compiler_params=_params("parallel", "arbitrary"), name="convffn_bwd_a")(
            up_pre, up_pre, dx, w_down, conv_w, conv_w, conv_b, conv_b)
    dup, dwa, dwb, dba, dbb = res
    dconv_w = jnp.concatenate([dwa, dwb], axis=1)
    dconv_b = jnp.concatenate([dba, dbb], axis=1)
    return dup, dconv_w, dconv_b


def _convffn_bwd_b(dup, conv_w, w_up_t):
    S = dup.shape[1]
    tm, tk = 1024, D_FF // 4
    per_half = D_FF // tk
    nk = 2 * per_half
    per = tm // HALO
    n_blk = S // tm

    def body(d_ref, halo_ref, cw_ref, w_ref, dpre_ref, dh_ref, ext_ref, acc_ref):
        i, k = pl.program_id(0), pl.program_id(1)
        n = tm + HALO
        ext_ref[0:tm, :] = d_ref[...].astype(F32)
        ext_ref[tm:, :] = jnp.where(i == n_blk - 1, 0.0, halo_ref[...].astype(F32))
        ext = ext_ref[...]
        y = cw_ref[2:3, :] * ext[0:tm] + cw_ref[1:2, :] * pltpu.roll(ext, n - 1, 0)[0:tm]
        y = y + cw_ref[0:1, :] * pltpu.roll(ext, n - 2, 0)[0:tm]
        pre = y.astype(BF16)
        dpre_ref[...] = pre
        prod = _dot(pre, w_ref[...], "nn")

        @pl.when(k == 0)
        def _():
            acc_ref[...] = prod

        @pl.when(k > 0)
        def _():
            acc_ref[...] += prod

        @pl.when(k == nk - 1)
        def _():
            dh_ref[...] = acc_ref[...].astype(BF16)

    blk = pl.BlockSpec((None, tm, tk), lambda i, k: (k // per_half, i, k % per_half))
    halo = pl.BlockSpec((None, HALO, tk),
                        lambda i, k: (k // per_half, jnp.minimum((i + 1) * per, S // HALO - 1), k % per_half))
    return pl.pallas_call(
        body, grid=(n_blk, nk),
        in_specs=[blk, halo, pl.BlockSpec((3, tk), lambda i, k: (0, k)), pl.BlockSpec((tk, D_MODEL), lambda i, k: (k, 0))],
        out_specs=[blk, pl.BlockSpec((tm, D_MODEL), lambda i, k: (i, 0))],
        out_shape=[jax.ShapeDtypeStruct((2, S, D_FF), BF16), jax.ShapeDtypeStruct((S, D_MODEL), BF16)],
        scratch_shapes=[pltpu.VMEM((tm + HALO, tk), F32), pltpu.VMEM((tm, D_MODEL), F32)],
        compiler_params=_params("parallel", "arbitrary"), name="convffn_bwd_b")(dup, dup, conv_w, w_up_t)


def _position():
    return lax.axis_index("x"), lax.axis_index("y"), lax.axis_index("c")


def _flip(pos, k):
    x, y, c = pos
    return (1 - x if k & 4 else x, 1 - y if k & 2 else y, 1 - c if k & 1 else c)


def _index(pos):
    return 4 * pos[0] + 2 * pos[1] + pos[2]


_HBM = pl.BlockSpec(memory_space=pltpu.HBM)
_SEM = pl.BlockSpec(memory_space=pltpu.SEMAPHORE)
_ANY = pl.BlockSpec(memory_space=pl.ANY)
_EFFECT = pltpu.SideEffectType.DATAFLOW_SIDE_EFFECTING


def _in_hbm(a):
    return pltpu.with_memory_space_constraint(a, pltpu.HBM)


def _split_copies(name, srcs, lands, n_copies, plan, start_after, wait_after):
    ns, nl = len(srcs), len(lands)
    start_after = tuple(start_after)
    na = len(start_after)

    def start_body(*refs):
        src_refs, land_refs = refs[:ns], refs[ns:ns + nl]
        send_sems, recv_sems = refs[ns + nl + na], refs[ns + nl + na + 1]
        token = refs[-1]
        for k, (src, dst, _, peer) in enumerate(plan(src_refs, land_refs)):
            pltpu.make_async_remote_copy(src_ref=src, dst_ref=dst, send_sem=send_sems.at[k], recv_sem=recv_sems.at[k],
                                         device_id=peer, device_id_type=MESH).start()
        token[...] = jnp.zeros_like(token)

    thru = [pltpu.HBM(a.shape, a.dtype) for a in list(srcs) + list(lands)]
    res = pl.pallas_call(
        start_body, name=name + "_start",
        out_shape=[pltpu.SemaphoreType.DMA((n_copies,)), pltpu.SemaphoreType.DMA((n_copies,))] + thru
        + [jax.ShapeDtypeStruct((8, 128), F32)],
        in_specs=[_HBM] * (ns + nl) + [_ANY] * na,
        out_specs=[_SEM, _SEM] + [_HBM] * (ns + nl) + [pl.BlockSpec(memory_space=pltpu.VMEM)],
        input_output_aliases={i: 2 + i for i in range(ns + nl)},
        compiler_params=pltpu.CompilerParams(has_side_effects=_EFFECT),
    )(*[_in_hbm(a) for a in list(srcs) + list(lands)], *start_after)
    send_sems, recv_sems, token = res[0], res[1], res[-1]
    thru_vals = res[2:-1]

    def wait():
        def wait_body(*refs):
            src_refs, land_refs = refs[:ns], refs[ns:ns + nl]
            s_sems, r_sems = refs[ns + nl], refs[ns + nl + 1]
            for k, (src, _, dst, peer) in enumerate(plan(src_refs, land_refs)):
                cp = pltpu.make_async_remote_copy(src_ref=src, dst_ref=dst, send_sem=s_sems.at[k], recv_sem=r_sems.at[k],
                                                  device_id=peer, device_id_type=MESH)
                cp.wait_send()
                cp.wait_recv()

        out = pl.pallas_call(
            wait_body, name=name + "_wait", out_shape=thru,
            in_specs=[_HBM] * (ns + nl) + [_SEM, _SEM, _ANY], out_specs=[_HBM] * (ns + nl),
            input_output_aliases={i: i for i in range(ns + nl)},
            compiler_params=pltpu.CompilerParams(has_side_effects=_EFFECT),
        )(*thru_vals, send_sems, recv_sems, wait_after())
        return list(out[:ns]), list(out[ns:])

    return wait, token


def _gather_level1_plan(n):
    def plan(src_refs, land_refs):
        me = _position()
        x, y, c = me
        peers = [(x, y, 1 - c), (1 - x, y, c), (x, 1 - y, c), (1 - x, 1 - y, c)]
        return [(src_refs[a], land_refs[a].at[_index(me)], land_refs[a].at[_index(p)], p)
                for a in range(n) for p in peers]
    return plan


def _gather_level2(lands):
    n = len(lands)

    def body(*refs):
        lin = refs[:n]
        send_sems, recv_sems = refs[2 * n:]
        x, y, c = _position()
        sibling = (x, y, 1 - c)
        chips = [(1 - x, y), (x, 1 - y), (1 - x, 1 - y)]

        def copy(a, j, core):
            slot = lin[a].at[_index((*chips[j], core))]
            return pltpu.make_async_remote_copy(
                src_ref=slot, dst_ref=slot, send_sem=send_sems.at[3 * a + j], recv_sem=recv_sems.at[3 * a + j],
                device_id=sibling, device_id_type=MESH)

        sent = [copy(a, j, c) for a in range(n) for j in range(3)]
        for cp in sent:
            cp.start()
        for a in range(n):
            for j in range(3):
                copy(a, j, 1 - c).wait_recv()
        for cp in sent:
            cp.wait_send()

    return pl.pallas_call(
        body, out_shape=[jax.ShapeDtypeStruct(l.shape, l.dtype) for l in lands],
        in_specs=[_ANY] * n, out_specs=[_ANY] * n, input_output_aliases={a: a for a in range(n)},
        scratch_shapes=[pltpu.SemaphoreType.DMA((3 * n,)), pltpu.SemaphoreType.DMA((3 * n,))],
        name="gather_level2")(*lands)


def _gather_direct_plan(n):
    def plan(src_refs, land_refs):
        me = _position()
        peers = [_flip(me, k) for k in range(1, N_DEV)]
        return [(src_refs[a], land_refs[a].at[_index(me)], land_refs[a].at[_index(p)], p)
                for a in range(n) for p in peers]
    return plan


def _place_own(shards):
    me = _index(_position())
    return [lax.dynamic_update_slice(lax.empty((N_DEV,) + s.shape, s.dtype), s[None], (me, 0, 0)) for s in shards]


def _exchange_direct_plan(n):
    def plan(src_refs, land_refs):
        me = _position()
        peers = [_flip(me, k) for k in range(1, N_DEV)]
        return [(src_refs[a].at[_index(p)], land_refs[a].at[_index(me)], land_refs[a].at[_index(p)], p)
                for a in range(n) for p in peers]
    return plan


def _adamw(w, g, m, v):
    m = ADAM_B1 * m + (1.0 - ADAM_B1) * g
    v = ADAM_B2 * v + (1.0 - ADAM_B2) * (g * g)
    m_hat = m / (1.0 - ADAM_B1 ** ADAM_STEP)
    v_hat = v / (1.0 - ADAM_B2 ** ADAM_STEP)
    delta = -ADAM_LR * (m_hat / (jnp.sqrt(v_hat) + ADAM_EPS) + ADAM_WD * w)
    return delta, m, v


def _row_tile(rows, unit, cap=256):
    best = rows
    for t in range(unit, min(rows, cap) + 1, unit):
        if rows % t == 0:
            best = t
    return best


def _sum_partials(name, recv):
    n_src, R, C = recv.shape
    T = _row_tile(R, 16)

    def body(r_ref, g_out):
        g = r_ref[0].astype(F32)
        for s in range(1, n_src):
            g = g + r_ref[s].astype(F32)
        g_out[...] = g

    return pl.pallas_call(
        body, grid=(R // T,), in_specs=[pl.BlockSpec((n_src, T, C), lambda i: (0, i, 0))],
        out_specs=pl.BlockSpec((T, C), lambda i: (i, 0)), out_shape=jax.ShapeDtypeStruct((R, C), F32),
        compiler_params=_params("parallel"), name=name)(recv)


def _adamw_update(name, w, g, m, v):
    R, C = w.shape
    T = _row_tile(R, 8)

    def body(w_ref, g_ref, m_ref, v_ref, d_out, m_out, v_out):
        delta, mn, vn = _adamw(w_ref[...], g_ref[...], m_ref[...], v_ref[...])
        d_out[...] = delta
        m_out[...] = mn
        v_out[...] = vn

    blk = pl.BlockSpec((T, C), lambda i: (i, 0))
    return pl.pallas_call(
        body, grid=(R // T,), in_specs=[blk] * 4, out_specs=[blk] * 3,
        out_shape=[jax.ShapeDtypeStruct((R, C), F32)] * 3, compiler_params=_params("parallel"), name=name)(w, g, m, v)


def _gather_vmem(p_ref, gath, send_sems, recv_sems):
    me = _position()
    gath[_index(me)] = p_ref[...]
    copies = []
    for k in range(1, N_DEV):
        peer = _flip(me, k)
        copies.append(pltpu.make_async_remote_copy(
            src_ref=p_ref, dst_ref=gath.at[_index(me)], send_sem=send_sems.at[k - 1],
            recv_sem=recv_sems.at[k - 1], device_id=peer, device_id_type=MESH))
    for cp in copies:
        cp.start()
    for k in range(1, N_DEV):
        peer = _flip(me, k)
        pltpu.make_async_remote_copy(
            src_ref=p_ref, dst_ref=gath.at[_index(peer)], send_sem=send_sems.at[k - 1],
            recv_sem=recv_sems.at[k - 1], device_id=peer, device_id_type=MESH).wait_recv()
    for cp in copies:
        cp.wait_send()


def _small_allreduce_adamw(part, w, m, v):
    R, C = part.shape

    def body(p_ref, w_ref, m_ref, v_ref, g_out, d_out, m_out, v_out, gath, send_sems, recv_sems):
        _gather_vmem(p_ref, gath, send_sems, recv_sems)
        g = gath[0]
        for s in range(1, N_DEV):
            g = g + gath[s]
        delta, mn, vn = _adamw(w_ref[...], g, m_ref[...], v_ref[...])
        g_out[...] = g
        d_out[...] = delta
        m_out[...] = mn
        v_out[...] = vn

    vm = pl.BlockSpec(memory_space=pltpu.VMEM)
    return pl.pallas_call(
        body, in_specs=[vm] * 4, out_specs=[vm] * 4, out_shape=[jax.ShapeDtypeStruct((R, C), F32)] * 4,
        scratch_shapes=[pltpu.VMEM((N_DEV, R, C), F32), pltpu.SemaphoreType.DMA((7,)), pltpu.SemaphoreType.DMA((7,))],
        name="small_allreduce_adamw")(part, w, m, v)


_TRANSPOSED = ("w_in", "w_up", "w_pool_out", "w_attn_out")
_SMALL = (("g_mix", 2048), ("b_gate", 4096), ("pool_scale", 1024), ("g_ffn", 2048), ("conv_b", 11264), ("g_final", 2048))
CONV_ROWS = 8


def _to_comm(name, shard):
    return shard.T if name in _TRANSPOSED else shard


def _pack_small(vals):
    flat = [vals[n].reshape(-1) for n, _ in _SMALL]
    used = sum(f.shape[0] for f in flat)
    flat.append(jnp.zeros((SMALL_ROWS * PACK_COLS - used,), F32))
    return jnp.concatenate(flat).reshape(SMALL_ROWS, PACK_COLS)


def _unpack_small(packed, like):
    flat = packed.reshape(-1)
    out, off = {}, 0
    for name, n in _SMALL:
        out[name] = flat[off:off + n].reshape(like[name].shape)
        off += n
    return out


def _local_step(x, target, W, small, first_token=0.0, first_weights=None, late_weights=None, emit=None):
    S = x.shape[0]
    o_qkv, o_gate = POOL_WIDTH, POOL_WIDTH + QKV_WIDTH
    if emit is None:
        emit = lambda grads: None

    def zero(token):
        return 0.0 if token is None else token[0, 0]

    h1 = _rms_fwd("rms1_fwd", x, small["g_mix"] + first_token)
    tok = None
    if first_weights is not None:
        w_first, tok = first_weights(h1)
        small = dict(small, conv_w_full=w_first.pop("conv_w_full"))
        W = dict(W, **w_first)
    w_in_t = W["w_in"]
    u = _mm_nt("proj_u", h1, w_in_t, tm=2048, tn=512, tk=2048, o_dtype=F32, n_rows=POOL_WIDTH, b_row_off=0,
               after=tok)
    qkv = _mm_nt("proj_qkv", h1, w_in_t, tm=2048, tn=512, tk=2048, o_dtype=BF16, n_rows=QKV_WIDTH, b_row_off=o_qkv,
                 after=tok)
    gates_pre = _mm_nt("proj_gates", h1, w_in_t, tm=2048, tn=512, tk=2048, o_dtype=BF16, n_rows=2 * D_MODEL,
                       b_row_off=o_gate, after=tok)
    pool_y = _pool_fwd(u, W["w_pool_lin"], small["pool_scale"])
    outs, lses = zip(*[_attn_fwd(gi, qkv) for gi in range(3)])
    attn_y = _merge_fwd(outs, lses)
    if late_weights is not None:
        W = dict(W, **late_weights(0, attn_y))
    w_up_t = W["w_up"]
    mixed, y_pool, y_attn = _mix_fwd(pool_y, attn_y, W["w_pool_out"], W["w_attn_out"], gates_pre, small["b_gate"])
    x1, h2 = _out_proj_norm(mixed, W["w_out"], x, small["g_ffn"])

    nj = D_FF // 512
    up_pre, act = _up_proj_convffn(h2, w_up_t, small["conv_w_full"], small["conv_b"])
    if late_weights is not None:
        W = dict(W, **late_weights(1, act))
    x2 = _mm_nn("down_proj", act, W["w_down"], tm=1024, tn=1024, tk=D_FF // 2, o_dtype=F32, res=x1)

    dx2, dx2_b, dg_final, sq_cols = _final_loss("final_loss", x2, small["g_final"], target)

    dw_down = _mm_tn("dw_down", act, dx2_b, tm=512, tn=2048, tk=TOKEN_TK)
    tok = emit(dict(w_down=dw_down))
    dup, dconv_w, dconv_b = _convffn_bwd_a(up_pre, dx2_b, W["w_down"], small["conv_w_full"],
                                           small["conv_b"] + zero(tok))
    dup_pre, dh2 = _convffn_bwd_b(dup, small["conv_w_full"], w_up_t)
    dw_up = _mm("dw_up", dup_pre, h2, mode="tn", grid=(2 * nj, 1, S // TOKEN_TK),
                a_spec=pl.BlockSpec((None, TOKEN_TK, 512), lambda i, j, k: (i // nj, k, i % nj)),
                b_spec=pl.BlockSpec((TOKEN_TK, D_MODEL), lambda i, j, k: (k, 0)),
                o_spec=pl.BlockSpec((512, D_MODEL), lambda i, j, k: (i, 0)),
                o_shape=(2 * D_FF, D_MODEL), o_dtype=BF16, acc_shape=(512, D_MODEL))
    tok = emit(dict(w_up=dw_up, conv_w=dconv_w))
    dx1, dx1_b, dg_ffn = _rms_bwd("rms2_bwd", x1, small["g_ffn"] + zero(tok), dh2, dx2, with_bf16=True)

    dw_out = _mm_tn("dw_out", mixed, dx1_b, tm=512, tn=2048, tk=TOKEN_TK)
    tok = emit(dict(w_out=dw_out))
    dy_pool, dy_attn, dgates, db_gate = _mix_bwd(dx1_b, W["w_out"], gates_pre, small["b_gate"] + zero(tok),
                                                 y_pool, y_attn)
    dw_pool_out = _mm_tn("dw_pool_out", dy_pool, pool_y, tm=1024, tn=1024, tk=TOKEN_TK)
    dw_attn_out = _mm_tn("dw_attn_out", dy_attn, attn_y, tm=1024, tn=512, tk=TOKEN_TK)
    e, dw_pool_lin, dpool_scale = _pool_bwd_a(u, dy_pool, W["w_pool_out"], W["w_pool_lin"], small["pool_scale"])
    du = _pool_bwd_b(e)
    dos, dps = _merge_bwd(dy_attn, W["w_attn_out"], outs, lses)
    dqkv = [_attn_bwd(gi, qkv, dos[gi], lses[gi], dps[gi]) for gi in range(3)]
    dproj = jnp.concatenate([du] + [dqkv[gi][sec] for sec in range(3) for gi in range(3)] + [dgates], axis=1)
    dw_in = _mm_tn("dw_in", dproj, h1, tm=512, tn=2048, tk=TOKEN_TK)
    tok = emit(dict(w_in=dw_in, w_pool_out=dw_pool_out, w_attn_out=dw_attn_out, w_pool_lin=dw_pool_lin))
    dh1 = _mm_nn("d_h1", dproj, w_in_t, tm=1024, tn=1024, tk=IN_WIDTH // 4, o_dtype=BF16,
                 after=tok)
    grad_x, dg_mix = _rms_bwd("rms1_bwd", x, small["g_mix"] + zero(tok), dh1, dx1, with_bf16=False)

    grads = dict(w_in=dw_in, w_up=dw_up, w_down=dw_down, w_out=dw_out, w_pool_out=dw_pool_out,
                 w_attn_out=dw_attn_out, w_pool_lin=dw_pool_lin, conv_w=dconv_w)
    small_grads = dict(g_mix=dg_mix, b_gate=db_gate, pool_scale=dpool_scale, g_ffn=dg_ffn, conv_b=dconv_b,
                       g_final=dg_final)
    return sq_cols, grad_x, grads, small_grads


def kernel(x, g_mix, w_in, b_gate, w_pool_lin, pool_scale, w_pool_out, w_attn_out, w_out, g_ffn, w_up, conv_w, conv_b, w_down, g_final, loss_target, m_g_mix, m_w_in, m_b_gate, m_w_pool_lin, m_pool_scale, m_w_pool_out, m_w_attn_out, m_w_out, m_g_ffn, m_w_up, m_conv_w, m_conv_b, m_w_down, m_g_final, v_g_mix, v_w_in, v_b_gate, v_w_pool_lin, v_pool_scale, v_w_pool_out, v_w_attn_out, v_w_out, v_g_ffn, v_w_up, v_conv_w, v_conv_b, v_w_down, v_g_final):
    given = dict(g_mix=g_mix, w_in=w_in, b_gate=b_gate, w_pool_lin=w_pool_lin, pool_scale=pool_scale,
                 w_pool_out=w_pool_out, w_attn_out=w_attn_out, w_out=w_out, g_ffn=g_ffn, w_up=w_up, conv_w=conv_w,
                 conv_b=conv_b, w_down=w_down, g_final=g_final)
    mom_m = dict(g_mix=m_g_mix, w_in=m_w_in, b_gate=m_b_gate, w_pool_lin=m_w_pool_lin, pool_scale=m_pool_scale,
                 w_pool_out=m_w_pool_out, w_attn_out=m_w_attn_out, w_out=m_w_out, g_ffn=m_g_ffn, w_up=m_w_up,
                 conv_w=m_conv_w, conv_b=m_conv_b, w_down=m_w_down, g_final=m_g_final)
    mom_v = dict(g_mix=v_g_mix, w_in=v_w_in, b_gate=v_b_gate, w_pool_lin=v_w_pool_lin, pool_scale=v_pool_scale,
                 w_pool_out=v_w_pool_out, w_attn_out=v_w_attn_out, w_out=v_w_out, g_ffn=v_g_ffn, w_up=v_w_up,
                 conv_w=v_conv_w, conv_b=v_conv_b, w_down=v_w_down, g_final=v_g_final)
    def full(g):
        return g.reshape(N_DEV * g.shape[1], g.shape[2])

    shard = dict(w_in=_to_comm("w_in", w_in[0]).astype(BF16))
    W = {}
    first = [shard["w_in"], w_pool_lin[0].reshape(4 * 32, POOL_GROUP_WIDTH),
             jnp.pad(conv_w[0], ((0, CONV_ROWS - 3), (0, 0)))]
    hold = {}
    w_in_wait, first_token = _split_copies(
        "gather_w_in", first, _place_own(first), 4 * len(first), _gather_level1_plan(len(first)),
        start_after=(), wait_after=lambda: hold["h1"])

    stages = (("w_out", "w_pool_out", "w_attn_out", "w_up"), ("w_down",))
    stage_after = [None, None]
    stage_wait = []
    for n in (n for names in stages for n in names):
        shard[n] = _to_comm(n, given[n][0] + first_token[0, 0]).astype(BF16)

    def first_weights(h1):
        hold["h1"] = h1
        w_in_all, lin_all, conv_w_all = _gather_level2(w_in_wait()[1])
        w_in_full = full(w_in_all)
        lin_full = jnp.moveaxis(lin_all.reshape(N_DEV, 4, 32, POOL_GROUP_WIDTH), 0, 1).reshape(
            4, POOL_GROUP_WIDTH, POOL_GROUP_WIDTH).astype(BF16)
        conv_w_full = jnp.moveaxis(conv_w_all[:, :3, :], 0, 1).reshape(3, 2 * D_FF)
        order_after = (w_in_full,)
        for k, names in enumerate(stages):
            srcs = [shard[n] for n in names]
            wait, token = _split_copies(f"gather_stage{k}", srcs, _place_own(srcs), 7 * len(names),
                                        _gather_direct_plan(len(names)), start_after=order_after,
                                        wait_after=lambda k=k: stage_after[k])
            stage_wait.append(wait)
            order_after = (token,)
        return dict(w_in=w_in_full, w_pool_lin=lin_full, conv_w_full=conv_w_full), token

    def late_weights(stage, after):
        stage_after[stage] = after
        return {n: full(l) for n, l in zip(stages[stage], stage_wait[stage]()[1])}

    small = dict(g_mix=g_mix, b_gate=b_gate, pool_scale=pool_scale, g_ffn=g_ffn, conv_b=conv_b,
                 g_final=g_final.reshape(1, D_MODEL))

    pending = []
    done = {}

    def emit(group):
        names = list(group)
        parts = []
        for n in names:
            g = group[n]
            if n == "w_pool_lin":
                g = jnp.moveaxis(g.reshape(4, N_DEV, 32, POOL_GROUP_WIDTH), 1, 0).reshape(
                    N_DEV, 4 * 32, POOL_GROUP_WIDTH).astype(BF16)
            elif n == "conv_w":
                g = jnp.pad(jnp.moveaxis(g.reshape(3, N_DEV, 2 * D_FF // N_DEV), 1, 0),
                            ((0, 0), (0, CONV_ROWS - 3), (0, 0))).astype(BF16)
            else:
                g = g.reshape(N_DEV, g.shape[0] // N_DEV, g.shape[1])
            parts.append(g)
        me = _index(_position())
        lands = [lax.dynamic_update_slice(lax.empty(p.shape, p.dtype), lax.dynamic_index_in_dim(p, me, 0),
                                          (me, 0, 0)) for p in parts]
        wait, token = _split_copies(
            "exchange_" + names[0], parts, lands, 7 * len(names), _exchange_direct_plan(len(names)),
            start_after=(), wait_after=lambda: done["grad_x"])
        pending.append((names, wait))
        return token

    prepared = sum(shard[n][0, 0].astype(F32) for names in stages for n in names) * 0.0
    sq_cols, grad_x, _, small_grads = _local_step(x[0], loss_target[0], W, small,
                                                  first_token=first_token[0, 0] + prepared,
                                                  first_weights=first_weights, late_weights=late_weights, emit=emit)
    done["grad_x"] = grad_x
    loss = lax.psum(0.5 / D_MODEL * jnp.sum(sq_cols), AXES)

    res = {}
    for n, r in [(n, r) for names, wait in pending for n, r in zip(names, wait()[1])]:
        g = _sum_partials("sum_" + n, r)
        if n in _TRANSPOSED:
            g = g.T
        elif n == "conv_w":
            g = g[:3]
        shape = given[n].shape
        rows = math.prod(shape[:-1])
        w2, m2, v2 = (d[n].reshape(rows, shape[-1]) for d in (given, mom_m, mom_v))
        upd = _adamw_update("adamw_" + n, w2, g.reshape(rows, shape[-1]), m2, v2)
        res[n] = [a.reshape(shape) for a in (g,) + tuple(upd)]

    packed_small = [_pack_small(d) for d in (given, mom_m, mom_v)]
    res_small = [_unpack_small(p, given) for p in _small_allreduce_adamw(_pack_small(small_grads), *packed_small)]

    order = ["g_mix", "w_in", "b_gate", "w_pool_lin", "pool_scale", "w_pool_out", "w_attn_out", "w_out", "g_ffn",
             "w_up", "conv_w", "conv_b", "w_down", "g_final"]
    outs = [loss, grad_x[None]]
    for kind in range(4):
        for n in order:
            outs.append(res[n][kind] if n in res else res_small[kind][n])
    return tuple(outs)
```

```python
import math

import jax
import jax.numpy as jnp
from jax import lax
from jax.experimental import pallas as pl
from jax.experimental.pallas import tpu as pltpu

F32 = jnp.float32
BF16 = jnp.bfloat16
MESH = pl.DeviceIdType.MESH
AXES = ("x", "y", "c")
N_DEV = 8

D_MODEL = 2048
POOL_WINDOWS = (2, 4, 8, 16)
POOL_GROUP_WIDTH = 256
POOL_WIDTH = 1024
ATTN_GROUPS = ((128, 1), (512, 4), (2048, 16))
SPAN = 128
HEADS_PER_GROUP = 4
N_ATTN_HEADS = 12
HEAD_DIM = 128
ATTN_WIDTH = 1536
QKV_WIDTH = 3 * ATTN_WIDTH
ATTN_OUT_WIDTH = 512
IN_WIDTH = 9728
D_FF = 5632
RMS_EPS = 1e-6
ADAM_LR, ADAM_B1, ADAM_B2, ADAM_EPS, ADAM_WD, ADAM_STEP = 0.001, 0.9, 0.999, 1e-08, 0.01, 10

VMEM_LIMIT_BYTES = 48 * 1024 * 1024
HALO = 16
TOKEN_TK = 2048
MXU_DIM = 256
PACK_COLS = 1024
SMALL_ROWS = 24
NEG_BIG = -1e30

_DN = {"nn": (((1,), (0,)), ((), ())), "nt": (((1,), (1,)), ((), ())), "tn": (((0,), (0,)), ((), ()))}


def _params(*sem):
    return pltpu.CompilerParams(dimension_semantics=sem, vmem_limit_bytes=VMEM_LIMIT_BYTES)


def _dot(a, b, mode):
    return lax.dot_general(a.astype(BF16), b.astype(BF16), _DN[mode], preferred_element_type=F32)


def _mm(name, a, b, *, mode, grid, a_spec, b_spec, o_spec, o_shape, o_dtype, acc_shape, res=None, res_spec=None,
        after=None):
    nk = grid[2]
    n_in = 2 + (res is not None) + (after is not None)

    def body(*refs):
        a_ref, b_ref = refs[:2]
        r_ref = refs[2] if res is not None else None
        o_ref = refs[n_in]
        scr = refs[n_in + 1:]
        prod = _dot(a_ref[...], b_ref[...], mode)

        def finish(val):
            if r_ref is not None:
                val = val + r_ref[...]
            o_ref[...] = val.astype(o_ref.dtype)

        if nk == 1:
            finish(prod)
        else:
            acc = scr[0]
            k = pl.program_id(2)

            @pl.when(k == 0)
            def _():
                acc[...] = prod

            @pl.when(k > 0)
            def _():
                acc[...] += prod

            @pl.when(k == nk - 1)
            def _():
                finish(acc[...])

    in_specs = [a_spec, b_spec]
    args = [a, b]
    if res is not None:
        in_specs.append(res_spec)
        args.append(res)
    if after is not None:
        in_specs.append(pl.BlockSpec((8, 128), lambda i, j, k: (0, 0)))
        args.append(after)
    return pl.pallas_call(
        body, grid=grid, in_specs=in_specs, out_specs=o_spec,
        out_shape=jax.ShapeDtypeStruct(o_shape, o_dtype),
        scratch_shapes=[] if nk == 1 else [pltpu.VMEM(acc_shape, F32)],
        compiler_params=_params("parallel", "parallel", "arbitrary"), name=name)(*args)


def _mm_nn(name, a, b, *, tm, tn, tk, o_dtype, res=None, after=None):
    M, K = a.shape
    N = b.shape[1]
    return _mm(name, a, b, mode="nn", grid=(M // tm, N // tn, K // tk),
               a_spec=pl.BlockSpec((tm, tk), lambda i, j, k: (i, k)),
               b_spec=pl.BlockSpec((tk, tn), lambda i, j, k: (k, j)),
               o_spec=pl.BlockSpec((tm, tn), lambda i, j, k: (i, j)),
               o_shape=(M, N), o_dtype=o_dtype, acc_shape=(tm, tn), res=res,
               res_spec=pl.BlockSpec((tm, tn), lambda i, j, k: (i, j)), after=after)


def _mm_nt(name, a, b, *, tm, tn, tk, o_dtype, n_rows=None, b_row_off=0, after=None):
    M, K = a.shape
    N = b.shape[0] if n_rows is None else n_rows
    off = b_row_off // tn
    return _mm(name, a, b, mode="nt", grid=(M // tm, N // tn, K // tk),
               a_spec=pl.BlockSpec((tm, tk), lambda i, j, k: (i, k)),
               b_spec=pl.BlockSpec((tn, tk), lambda i, j, k: (j + off, k)),
               o_spec=pl.BlockSpec((tm, tn), lambda i, j, k: (i, j)),
               o_shape=(M, N), o_dtype=o_dtype, acc_shape=(tm, tn), after=after)


def _mm_tn(name, a, b, *, tm, tn, tk):
    K, M = a.shape
    N = b.shape[1]
    return _mm(name, a, b, mode="tn", grid=(M // tm, N // tn, K // tk),
               a_spec=pl.BlockSpec((tk, tm), lambda i, j, k: (k, i)),
               b_spec=pl.BlockSpec((tk, tn), lambda i, j, k: (k, j)),
               o_spec=pl.BlockSpec((tm, tn), lambda i, j, k: (i, j)),
               o_shape=(M, N), o_dtype=BF16, acc_shape=(tm, tn))


def _rms_fwd(name, x, g):
    S, D = x.shape
    T = 512

    def body(x_ref, g_ref, h_ref):
        xv = x_ref[...]
        r = lax.rsqrt(jnp.mean(xv * xv, axis=-1, keepdims=True) + RMS_EPS)
        h_ref[...] = (xv * r * g_ref[...]).astype(BF16)

    return pl.pallas_call(
        body, grid=(S // T,),
        in_specs=[pl.BlockSpec((T, D), lambda i: (i, 0)), pl.BlockSpec((1, D), lambda i: (0, 0))],
        out_specs=pl.BlockSpec((T, D), lambda i: (i, 0)),
        out_shape=jax.ShapeDtypeStruct((S, D), BF16), compiler_params=_params("parallel"), name=name)(x, g)


def _out_proj_norm(mixed, w_out, x, g):
    S, D = x.shape
    tm = 512

    def body(m_ref, w_ref, x_ref, g_ref, x1_ref, h_ref):
        x1 = x_ref[...] + _dot(m_ref[...], w_ref[...], "nn")
        x1_ref[...] = x1
        r = lax.rsqrt(jnp.mean(x1 * x1, axis=-1, keepdims=True) + RMS_EPS)
        h_ref[...] = (x1 * r * g_ref[...]).astype(BF16)

    row = pl.BlockSpec((tm, D), lambda i: (i, 0))
    return pl.pallas_call(
        body, grid=(S // tm,),
        in_specs=[row, pl.BlockSpec((D, D), lambda i: (0, 0)), row, pl.BlockSpec((1, D), lambda i: (0, 0))],
        out_specs=[row, row], out_shape=[jax.ShapeDtypeStruct((S, D), F32), jax.ShapeDtypeStruct((S, D), BF16)],
        compiler_params=_params("parallel"), name="out_proj_norm")(mixed, w_out, x, g)


def _rms_bwd(name, x, g, dh, dres, with_bf16):
    S, D = x.shape
    T = 256

    def body(x_ref, g_ref, dh_ref, dres_ref, dx_ref, *rest):
        dg_ref = rest[-1]
        i = pl.program_id(0)
        xv = x_ref[...]
        r = lax.rsqrt(jnp.mean(xv * xv, axis=-1, keepdims=True) + RMS_EPS)
        xhat = xv * r
        dhv = dh_ref[...].astype(F32)
        gdh = dhv * g_ref[...]
        c = jnp.mean(xhat * gdh, axis=-1, keepdims=True)
        dx = dres_ref[...] + r * (gdh - xhat * c)
        dx_ref[...] = dx
        if with_bf16:
            rest[0][...] = dx.astype(BF16)
        part = jnp.sum(dhv * xhat, axis=0, keepdims=True)

        @pl.when(i == 0)
        def _():
            dg_ref[...] = part

        @pl.when(i > 0)
        def _():
            dg_ref[...] += part

    row = pl.BlockSpec((T, D), lambda i: (i, 0))
    vec = pl.BlockSpec((1, D), lambda i: (0, 0))
    narrow = [jax.ShapeDtypeStruct((S, D), BF16)] if with_bf16 else []
    return pl.pallas_call(
        body, grid=(S // T,), in_specs=[row, vec, row, row], out_specs=[row] * (1 + len(narrow)) + [vec],
        out_shape=[jax.ShapeDtypeStruct((S, D), F32)] + narrow + [jax.ShapeDtypeStruct((1, D), F32)],
        compiler_params=_params("arbitrary"), name=name)(x, g, dh, dres)


def _final_loss(name, x, g, target):
    S, D = x.shape
    T = 256

    def body(x_ref, g_ref, t_ref, dx_ref, dxb_ref, dg_ref, sq_ref):
        i = pl.program_id(0)
        xv = x_ref[...]
        r = lax.rsqrt(jnp.mean(xv * xv, axis=-1, keepdims=True) + RMS_EPS)
        xhat = xv * r
        err = xhat * g_ref[...] - t_ref[...]
        dy = err * (1.0 / D)
        gdy = dy * g_ref[...]
        c = jnp.mean(xhat * gdy, axis=-1, keepdims=True)
        dx = r * (gdy - xhat * c)
        dx_ref[...] = dx
        dxb_ref[...] = dx.astype(BF16)
        dg_part = jnp.sum(dy * xhat, axis=0, keepdims=True)
        sq_part = jnp.sum(err * err, axis=0, keepdims=True)

        @pl.when(i == 0)
        def _():
            dg_ref[...] = dg_part
            sq_ref[...] = sq_part

        @pl.when(i > 0)
        def _():
            dg_ref[...] += dg_part
            sq_ref[...] += sq_part

    row = pl.BlockSpec((T, D), lambda i: (i, 0))
    vec = pl.BlockSpec((1, D), lambda i: (0, 0))
    return pl.pallas_call(
        body, grid=(S // T,), in_specs=[row, vec, row], out_specs=[row, row, vec, vec],
        out_shape=[jax.ShapeDtypeStruct((S, D), F32), jax.ShapeDtypeStruct((S, D), BF16),
                   jax.ShapeDtypeStruct((1, D), F32), jax.ShapeDtypeStruct((1, D), F32)],
        compiler_params=_params("arbitrary"), name=name)(x, g, target)


def _prev_halo_spec(T, C, col_map):
    per = T // HALO
    return pl.BlockSpec((HALO, C), lambda *g: (jnp.maximum(g[0] * per - 1, 0), col_map(*g)))


def _pool_window_sums(ext, T, i):
    t = i * T + lax.broadcasted_iota(jnp.int32, (T, 1), 0)
    outs = []
    cnts = []
    for gi, w in enumerate(POOL_WINDOWS):
        s = ext[:, gi * POOL_GROUP_WIDTH:(gi + 1) * POOL_GROUP_WIDTH]
        sh = 1
        while sh < w:
            s = s + pltpu.roll(s, sh, 0)
            sh *= 2
        cnt = jnp.minimum(t + 1, w).astype(F32)
        outs.append(s[HALO:] / cnt - ext[HALO:, gi * POOL_GROUP_WIDTH:(gi + 1) * POOL_GROUP_WIDTH])
        cnts.append(cnt)
    return outs, cnts


def _load_ext(ext_ref, halo_ref, blk_ref, first):
    ext_ref[0:HALO, :] = jnp.where(first, 0.0, halo_ref[...].astype(F32))
    ext_ref[HALO:, :] = blk_ref[...].astype(F32)


def _pool_fwd(u, w_lin, scale):
    S = u.shape[0]
    T = 512
    W = POOL_WIDTH

    def body(u_ref, halo_ref, wl_ref, sc_ref, y_ref, ext_ref):
        i = pl.program_id(0)
        _load_ext(ext_ref, halo_ref, u_ref, i == 0)
        pooled, _ = _pool_window_sums(ext_ref[...], T, i)
        for gi in range(4):
            z = _dot(pooled[gi], wl_ref[gi], "nn")
            cs = slice(gi * POOL_GROUP_WIDTH, (gi + 1) * POOL_GROUP_WIDTH)
            y_ref[:, cs] = (z * sc_ref[:, cs]).astype(BF16)

    return pl.pallas_call(
        body, grid=(S // T,),
        in_specs=[pl.BlockSpec((T, W), lambda i: (i, 0)), _prev_halo_spec(T, W, lambda i: 0),
                  pl.BlockSpec((4, 256, 256), lambda i: (0, 0, 0)), pl.BlockSpec((1, W), lambda i: (0, 0))],
        out_specs=pl.BlockSpec((T, W), lambda i: (i, 0)),
        out_shape=jax.ShapeDtypeStruct((S, W), BF16),
        scratch_shapes=[pltpu.VMEM((HALO + T, W), F32)],
        compiler_params=_params("parallel"), name="pool_fwd")(u, u, w_lin, scale)


def _pool_bwd_a(u, dy_out, w_pool_out_t, w_lin, scale):
    S = u.shape[0]
    T = 512
    W = POOL_WIDTH

    def body(u_ref, halo_ref, dyo_ref, wpo_ref, wl_ref, sc_ref, e_ref, dwl_ref, dsc_ref, ext_ref):
        i = pl.program_id(0)
        dy = _dot(dyo_ref[...], wpo_ref[...], "nn")
        _load_ext(ext_ref, halo_ref, u_ref, i == 0)
        pooled, cnts = _pool_window_sums(ext_ref[...], T, i)
        for gi in range(4):
            cs = slice(gi * POOL_GROUP_WIDTH, (gi + 1) * POOL_GROUP_WIDTH)
            z = _dot(pooled[gi], wl_ref[gi], "nn")
            dyg = dy[:, cs]
            dsc = jnp.sum(dyg * z, axis=0, keepdims=True)
            dz = dyg * sc_ref[:, cs]
            dwl = _dot(pooled[gi], dz, "tn")
            dpooled = _dot(dz, wl_ref[gi], "nt")
            e_ref[:, cs] = dpooled / cnts[gi]

            @pl.when(i == 0)
            def _():
                dwl_ref[gi] = dwl
                dsc_ref[:, cs] = dsc

            @pl.when(i > 0)
            def _():
                dwl_ref[gi] += dwl
                dsc_ref[:, cs] += dsc

    return pl.pallas_call(
        body, grid=(S // T,),
        in_specs=[pl.BlockSpec((T, W), lambda i: (i, 0)), _prev_halo_spec(T, W, lambda i: 0),
                  pl.BlockSpec((T, D_MODEL), lambda i: (i, 0)), pl.BlockSpec((D_MODEL, W), lambda i: (0, 0)),
                  pl.BlockSpec((4, 256, 256), lambda i: (0, 0, 0)), pl.BlockSpec((1, W), lambda i: (0, 0))],
        out_specs=[pl.BlockSpec((T, W), lambda i: (i, 0)), pl.BlockSpec((4, 256, 256), lambda i: (0, 0, 0)),
                   pl.BlockSpec((1, W), lambda i: (0, 0))],
        out_shape=[jax.ShapeDtypeStruct((S, W), F32), jax.ShapeDtypeStruct((4, 256, 256), F32),
                   jax.ShapeDtypeStruct((1, W), F32)],
        scratch_shapes=[pltpu.VMEM((HALO + T, W), F32)],
        compiler_params=_params("arbitrary"), name="pool_bwd_a")(u, u, dy_out, w_pool_out_t, w_lin, scale)


def _pool_bwd_b(e):
    S = e.shape[0]
    T = 512
    W = POOL_WIDTH
    per = T // HALO
    n_blk = S // T

    def body(e_ref, halo_ref, du_ref, ext_ref):
        i = pl.program_id(0)
        ext_ref[0:T, :] = e_ref[...]
        ext_ref[T:, :] = jnp.where(i == n_blk - 1, 0.0, halo_ref[...])
        ext = ext_ref[...]
        n = T + HALO
        t = i * T + lax.broadcasted_iota(jnp.int32, (T, 1), 0)
        for gi, w in enumerate(POOL_WINDOWS):
            cs = slice(gi * POOL_GROUP_WIDTH, (gi + 1) * POOL_GROUP_WIDTH)
            s = ext[:, cs]
            sh = 1
            while sh < w:
                s = s + pltpu.roll(s, n - sh, 0)
                sh *= 2
            cnt = jnp.minimum(t + 1, w).astype(F32)
            du_ref[:, cs] = (s[0:T] - ext[0:T, cs] * cnt).astype(BF16)

    return pl.pallas_call(
        body, grid=(n_blk,),
        in_specs=[pl.BlockSpec((T, W), lambda i: (i, 0)),
                  pl.BlockSpec((HALO, W), lambda i: (jnp.minimum((i + 1) * per, S // HALO - 1), 0))],
        out_specs=pl.BlockSpec((T, W), lambda i: (i, 0)),
        out_shape=jax.ShapeDtypeStruct((S, W), BF16),
        scratch_shapes=[pltpu.VMEM((T + HALO, W), F32)],
        compiler_params=_params("parallel"), name="pool_bwd_b")(e, e)


def _slope(head):
    return 2.0 ** (-8.0 * (head + 1) / N_ATTN_HEADS)


def _attn_masks(b):
    row = lax.broadcasted_iota(jnp.int32, (SPAN, SPAN), 0)
    col = lax.broadcasted_iota(jnp.int32, (SPAN, SPAN), 1)
    dist_cur = (row - col).astype(F32)
    valid_cur = col <= row
    valid_prev = jnp.logical_and(col >= row, b > 0)
    return dist_cur, valid_cur, valid_prev


ATTN_BATCH = 4


def _attn_tiling(gi, S):
    d = ATTN_GROUPS[gi][1]
    hp = HEADS_PER_GROUP if d < 16 else 1
    return d, SPAN * d, S // (SPAN * d), hp, hp * HEAD_DIM, ATTN_BATCH // hp


_BATCHED_DN = {"nt": (((2,), (2,)), ((0,), (0,))), "nn": (((2,), (1,)), ((0,), (0,))), "tn": (((1,), (1,)), ((0,), (0,)))}


def _bdot(a, b, mode):
    return lax.dot_general(a.astype(BF16), b.astype(BF16), _BATCHED_DN[mode], preferred_element_type=F32)


def _batch_rows(it, d, rb):
    return [pl.ds(it * rb + j, SPAN, stride=d) for j in range(rb)]


def _take(scr, hp, rows):
    return jnp.stack([scr[hi, rw, :] for hi in range(hp) for rw in rows])


def _put(scr, hp, rows, val):
    for n, (hi, rw) in enumerate((hi, rw) for hi in range(hp) for rw in rows):
        scr[hi, rw, :] = val[n]


def _batch_bias(gi, d, hp, rb, dist):
    return jnp.stack([(-(_head_slope(gi, hp, hi) * d)) * dist for hi in range(hp) for _ in range(rb)])


def _head_slope(gi, hp, hi):
    if hp == HEADS_PER_GROUP:
        return _slope(gi * HEADS_PER_GROUP + hi)
    first = pl.program_id(1) * hp
    sl = jnp.float32(_slope(gi * HEADS_PER_GROUP + hi))
    for h0 in range(hp, HEADS_PER_GROUP, hp):
        sl = jnp.where(first == h0, jnp.float32(_slope(gi * HEADS_PER_GROUP + h0 + hi)), sl)
    return sl


def _split_heads(src_ref, dst_ref, hp):
    for hi in range(hp):
        dst_ref[hi] = src_ref[:, hi * HEAD_DIM:(hi + 1) * HEAD_DIM].astype(F32)


def _attn_fwd(gi, qkv):
    S = qkv.shape[0]
    d, C, nb, hp, W, rb = _attn_tiling(gi, S)
    scale = HEAD_DIM ** -0.5

    def body(q_ref, kc_ref, kp_ref, vc_ref, vp_ref, o_ref, lse_ref, qs, kcs, kps, vcs, vps, os, ls):
        b = pl.program_id(0)
        for src, dst in ((q_ref, qs), (kc_ref, kcs), (kp_ref, kps), (vc_ref, vcs), (vp_ref, vps)):
            _split_heads(src, dst, hp)
        dist_cur, valid_cur, valid_prev = _attn_masks(b)
        bias_cur = _batch_bias(gi, d, hp, rb, dist_cur)
        bias_far = _batch_bias(gi, d, hp, rb, dist_cur + SPAN)

        def sub_sequences(it, carry):
            rows = _batch_rows(it, d, rb)
            q = _take(qs, hp, rows)
            s_cur = jnp.where(valid_cur, _bdot(q, _take(kcs, hp, rows), "nt") * scale + bias_cur, NEG_BIG)
            s_prev = jnp.where(valid_prev, _bdot(q, _take(kps, hp, rows), "nt") * scale + bias_far, NEG_BIG)
            m = jnp.maximum(jnp.max(s_cur, axis=-1, keepdims=True), jnp.max(s_prev, axis=-1, keepdims=True))
            p_cur = jnp.exp(s_cur - m)
            p_prev = jnp.exp(s_prev - m)
            l = jnp.sum(p_cur, axis=-1, keepdims=True) + jnp.sum(p_prev, axis=-1, keepdims=True)
            o = (_bdot(p_cur, _take(vcs, hp, rows), "nn") + _bdot(p_prev, _take(vps, hp, rows), "nn")) / l
            _put(os, hp, rows, o)
            _put(ls, hp, rows, jnp.broadcast_to(m + jnp.log(l), o.shape))
            return carry

        lax.fori_loop(0, d // rb, sub_sequences, 0)
        for hi in range(hp):
            hs = slice(hi * HEAD_DIM, (hi + 1) * HEAD_DIM)
            o_ref[:, hs] = os[hi]
            lse_ref[:, hs] = ls[hi]

    per_sec = ATTN_WIDTH // W
    per_grp = ATTN_OUT_WIDTH // W

    def spec(sec, shift):
        return pl.BlockSpec((C, W), lambda b, h: (jnp.maximum(b + shift, 0), sec * per_sec + gi * per_grp + h))

    out = pl.BlockSpec((C, W), lambda b, h: (b, h))
    scr = pltpu.VMEM((hp, C, HEAD_DIM), F32)
    return pl.pallas_call(
        body, grid=(nb, HEADS_PER_GROUP // hp),
        in_specs=[spec(0, 0), spec(1, 0), spec(1, -1), spec(2, 0), spec(2, -1)], out_specs=[out, out],
        out_shape=[jax.ShapeDtypeStruct((S, ATTN_OUT_WIDTH), F32)] * 2, scratch_shapes=[scr] * 7,
        compiler_params=_params("parallel", "parallel"), name=f"attn_fwd_g{gi}")(qkv, qkv, qkv, qkv, qkv)


def _attn_bwd(gi, qkv, do, lse, dp):
    S = qkv.shape[0]
    d, C, nb, hp, W, rb = _attn_tiling(gi, S)
    scale = HEAD_DIM ** -0.5
    n_in = 12

    def body(*refs):
        ins, (dq_ref, dk_ref, dv_ref) = refs[:n_in], refs[n_in:n_in + 3]
        scr = refs[n_in + 3:]
        for src, dst in zip(ins, scr[:n_in]):
            _split_heads(src, dst, hp)
        qs, kcs, vcs, kps, vps, qns, dos, dons, lses, lsens, dps, dpns = scr[:n_in]
        dqs, dks, dvs = scr[n_in:]
        b = pl.program_id(0)
        dist_cur, valid_cur, valid_prev = _attn_masks(b)
        row = lax.broadcasted_iota(jnp.int32, (SPAN, SPAN), 0)
        col = lax.broadcasted_iota(jnp.int32, (SPAN, SPAN), 1)
        valid_next = jnp.logical_and(col >= row, b < nb - 1)

        bias_cur = _batch_bias(gi, d, hp, rb, dist_cur)
        bias_far = _batch_bias(gi, d, hp, rb, dist_cur + SPAN)

        def sub_sequences(it, carry):
            rows = _batch_rows(it, d, rb)
            q, kc, vc, kp, vp, qn, dov, don = (_take(s, hp, rows) for s in (qs, kcs, vcs, kps, vps, qns, dos, dons))
            lse_b, lse_n, dp_b, dp_n = (_take(s, hp, rows) for s in (lses, lsens, dps, dpns))
            p_cur = jnp.where(valid_cur, jnp.exp(_bdot(q, kc, "nt") * scale + bias_cur - lse_b), 0.0)
            p_prev = jnp.where(valid_prev, jnp.exp(_bdot(q, kp, "nt") * scale + bias_far - lse_b), 0.0)
            ds_cur = p_cur * (_bdot(dov, vc, "nt") - dp_b)
            ds_prev = p_prev * (_bdot(dov, vp, "nt") - dp_b)
            _put(dqs, hp, rows, (_bdot(ds_cur, kc, "nn") + _bdot(ds_prev, kp, "nn")) * scale)
            p_next = jnp.where(valid_next, jnp.exp(_bdot(qn, kc, "nt") * scale + bias_far - lse_n), 0.0)
            ds_next = p_next * (_bdot(don, vc, "nt") - dp_n)
            _put(dvs, hp, rows, _bdot(p_cur, dov, "tn") + _bdot(p_next, don, "tn"))
            _put(dks, hp, rows, (_bdot(ds_cur, q, "tn") + _bdot(ds_next, qn, "tn")) * scale)
            return carry

        lax.fori_loop(0, d // rb, sub_sequences, 0)
        for hi in range(hp):
            hs = slice(hi * HEAD_DIM, (hi + 1) * HEAD_DIM)
            dq_ref[:, hs] = dqs[hi].astype(BF16)
            dk_ref[:, hs] = dks[hi].astype(BF16)
            dv_ref[:, hs] = dvs[hi].astype(BF16)

    per_sec = ATTN_WIDTH // W
    per_grp = ATTN_OUT_WIDTH // W

    def qkv_spec(sec, shift):
        return pl.BlockSpec((C, W), lambda b, h: (jnp.clip(b + shift, 0, nb - 1), sec * per_sec + gi * per_grp + h))

    def act_spec(shift):
        return pl.BlockSpec((C, W), lambda b, h: (jnp.clip(b + shift, 0, nb - 1), h))

    out = act_spec(0)
    scr = pltpu.VMEM((hp, C, HEAD_DIM), F32)
    return pl.pallas_call(
        body, grid=(nb, HEADS_PER_GROUP // hp),
        in_specs=[qkv_spec(0, 0), qkv_spec(1, 0), qkv_spec(2, 0), qkv_spec(1, -1), qkv_spec(2, -1), qkv_spec(0, 1),
                  act_spec(0), act_spec(1), act_spec(0), act_spec(1), act_spec(0), act_spec(1)],
        out_specs=[out, out, out], out_shape=[jax.ShapeDtypeStruct((S, ATTN_OUT_WIDTH), BF16)] * 3,
        scratch_shapes=[scr] * (n_in + 3),
        compiler_params=_params("parallel", "parallel"), name=f"attn_bwd_g{gi}")(
            qkv, qkv, qkv, qkv, qkv, qkv, do, do, lse, lse, dp, dp)


def _group_weights(l_refs):
    l0, l1, l2 = (r[...] for r in l_refs)
    m = jnp.maximum(jnp.maximum(l0, l1), l2)
    e = [jnp.exp(l0 - m), jnp.exp(l1 - m), jnp.exp(l2 - m)]
    tot = e[0] + e[1] + e[2]
    return [ei / tot for ei in e]


def _merge_fwd(outs, lses):
    S, W = outs[0].shape
    T = 512

    def body(o0, o1, o2, l0, l1, l2, y_ref):
        w = _group_weights((l0, l1, l2))
        y_ref[...] = (w[0] * o0[...] + w[1] * o1[...] + w[2] * o2[...]).astype(BF16)

    blk = pl.BlockSpec((T, W), lambda i: (i, 0))
    return pl.pallas_call(
        body, grid=(S // T,), in_specs=[blk] * 6, out_specs=blk, out_shape=jax.ShapeDtypeStruct((S, W), BF16),
        compiler_params=_params("parallel"), name="merge_fwd")(*outs, *lses)


def _merge_bwd(dy_out, w_attn_out_t, outs, lses):
    S, W = outs[0].shape
    T = 512

    def body(dyo_ref, wao_ref, o0, o1, o2, l0, l1, l2, do0, do1, do2, dp0, dp1, dp2):
        w = _group_weights((l0, l1, l2))
        dyv = _dot(dyo_ref[...], wao_ref[...], "nn")
        y = w[0] * o0[...] + w[1] * o1[...] + w[2] * o2[...]
        prod = dyv * y
        dots = []
        for hi in range(HEADS_PER_GROUP):
            hs = slice(hi * HEAD_DIM, (hi + 1) * HEAD_DIM)
            dots.append(jnp.broadcast_to(jnp.sum(prod[:, hs], axis=-1, keepdims=True), (T, HEAD_DIM)))
        dot_b = jnp.concatenate(dots, axis=1)
        for wg, do_ref, dp_ref in zip(w, (do0, do1, do2), (dp0, dp1, dp2)):
            do_ref[...] = (wg * dyv).astype(BF16)
            dp_ref[...] = wg * dot_b

    blk = pl.BlockSpec((T, W), lambda i: (i, 0))
    res = pl.pallas_call(
        body, grid=(S // T,),
        in_specs=[pl.BlockSpec((T, D_MODEL), lambda i: (i, 0)), pl.BlockSpec((D_MODEL, W), lambda i: (0, 0))] + [blk] * 6,
        out_specs=[blk] * 6,
        out_shape=[jax.ShapeDtypeStruct((S, W), BF16)] * 3 + [jax.ShapeDtypeStruct((S, W), F32)] * 3,
        compiler_params=_params("parallel"), name="merge_bwd")(dy_out, w_attn_out_t, *outs, *lses)
    return res[:3], res[3:]


def _mix_fwd(pool_y, attn_y, w_pool_out_t, w_attn_out_t, gates_pre, b_gate):
    S = pool_y.shape[0]
    D = D_MODEL
    tm, tn = 1024, 512
    nj = D // tn

    def body(p_ref, a_ref, wp_ref, wa_ref, g0_ref, g1_ref, b0_ref, b1_ref, mixed_ref, yp_ref, ya_ref):
        yp = _dot(p_ref[...], wp_ref[...], "nt")
        ya = _dot(a_ref[...], wa_ref[...], "nt")
        g0 = jax.nn.sigmoid(g0_ref[...] + b0_ref[...])
        g1 = jax.nn.sigmoid(g1_ref[...] + b1_ref[...])
        mixed_ref[...] = (g0 * yp + g1 * ya).astype(BF16)
        yp_ref[...] = yp.astype(BF16)
        ya_ref[...] = ya.astype(BF16)

    out = pl.BlockSpec((tm, tn), lambda i, j: (i, j))
    return pl.pallas_call(
        body, grid=(S // tm, nj),
        in_specs=[pl.BlockSpec((tm, POOL_WIDTH), lambda i, j: (i, 0)),
                  pl.BlockSpec((tm, ATTN_OUT_WIDTH), lambda i, j: (i, 0)),
                  pl.BlockSpec((tn, POOL_WIDTH), lambda i, j: (j, 0)),
                  pl.BlockSpec((tn, ATTN_OUT_WIDTH), lambda i, j: (j, 0)),
                  pl.BlockSpec((tm, tn), lambda i, j: (i, j)), pl.BlockSpec((tm, tn), lambda i, j: (i, j + nj)),
                  pl.BlockSpec((1, tn), lambda i, j: (0, j)), pl.BlockSpec((1, tn), lambda i, j: (0, j + nj))],
        out_specs=[out, out, out], out_shape=[jax.ShapeDtypeStruct((S, D), BF16)] * 3,
        compiler_params=_params("parallel", "parallel"), name="mix_fwd")(
            pool_y, attn_y, w_pool_out_t, w_attn_out_t, gates_pre, gates_pre, b_gate, b_gate)


def _mix_bwd(dx, w_out, gates_pre, b_gate, y_pool, y_attn):
    S = dx.shape[0]
    D = D_MODEL
    T = 256

    def body(dx_ref, w_ref, gp_ref, b_ref, yp_ref, ya_ref, dyp_ref, dya_ref, dg_ref, db_ref):
        i = pl.program_id(0)
        dm = _dot(dx_ref[...], w_ref[...], "nt")
        parts = []
        for br, (y_ref, dy_ref) in enumerate(((yp_ref, dyp_ref), (ya_ref, dya_ref))):
            cs = slice(br * D, (br + 1) * D)
            g = jax.nn.sigmoid(gp_ref[:, cs] + b_ref[:, cs])
            dy_ref[...] = (dm * g).astype(BF16)
            dpre = dm * y_ref[...].astype(F32) * g * (1.0 - g)
            dg_ref[:, cs] = dpre.astype(BF16)
            parts.append(jnp.sum(dpre, axis=0, keepdims=True))

        @pl.when(i == 0)
        def _():
            db_ref[:, 0:D] = parts[0]
            db_ref[:, D:2 * D] = parts[1]

        @pl.when(i > 0)
        def _():
            db_ref[:, 0:D] += parts[0]
            db_ref[:, D:2 * D] += parts[1]

    row = pl.BlockSpec((T, D), lambda i: (i, 0))
    row2 = pl.BlockSpec((T, 2 * D), lambda i: (i, 0))
    vec2 = pl.BlockSpec((1, 2 * D), lambda i: (0, 0))
    return pl.pallas_call(
        body, grid=(S // T,), in_specs=[row, pl.BlockSpec((D, D), lambda i: (0, 0)), row2, vec2, row, row],
        out_specs=[row, row, row2, vec2],
        out_shape=[jax.ShapeDtypeStruct((S, D), BF16), jax.ShapeDtypeStruct((S, D), BF16),
                   jax.ShapeDtypeStruct((S, 2 * D), BF16), jax.ShapeDtypeStruct((1, 2 * D), F32)],
        compiler_params=_params("arbitrary"), name="mix_bwd")(dx, w_out, gates_pre, b_gate, y_pool, y_attn)


_CONV_T, _CONV_C = 512, 512


def _conv_taps(ext_ref, h, w_ref, b_ref, cols=slice(None)):
    ext = ext_ref[h, :, cols]
    x2 = pltpu.roll(ext, 2, 0)[HALO:]
    x1 = pltpu.roll(ext, 1, 0)[HALO:]
    x0 = ext[HALO:]
    y = b_ref[:, cols] + w_ref[0:1, cols] * x2
    y = y + w_ref[1:2, cols] * x1
    y = y + w_ref[2:3, cols] * x0
    return y, (x2, x1, x0)


def _gelu_parts(a):
    cdf = 0.5 * (1.0 + lax.erf(a * (1.0 / math.sqrt(2.0))))
    return cdf, a * cdf


def _conv_specs(S):
    T, C = _CONV_T, _CONV_C
    nj = D_FF // C
    blk = pl.BlockSpec((2, T, C), lambda i, j: (0, i, j))
    per = T // HALO
    halo = pl.BlockSpec((2, HALO, C), lambda i, j: (0, jnp.maximum(i * per - 1, 0), j))
    w_a = pl.BlockSpec((3, C), lambda i, j: (0, j))
    w_b = pl.BlockSpec((3, C), lambda i, j: (0, j + nj))
    b_a = pl.BlockSpec((1, C), lambda i, j: (0, j))
    b_b = pl.BlockSpec((1, C), lambda i, j: (0, j + nj))
    return T, C, nj, blk, halo, w_a, w_b, b_a, b_b


def _conv_load(ext_ref, halo_ref, up_ref, first):
    for h in range(2):
        ext_ref[h, 0:HALO, :] = jnp.where(first, 0.0, halo_ref[h].astype(F32))
        ext_ref[h, HALO:, :] = up_ref[h].astype(F32)


def _up_proj_convffn(h, w_up_t, conv_w, conv_b):
    S = h.shape[0]
    tm, C = 1024, _CONV_C
    nj = D_FF // C

    def body(h_ref, wa_ref, wb_ref, cwa_ref, cwb_ref, cba_ref, cbb_ref, up_ref, act_ref, ext_ref, carry_ref):
        i, j = pl.program_id(0), pl.program_id(1)
        for half, w_ref in enumerate((wa_ref, wb_ref)):
            pre = _dot(h_ref[...], w_ref[...], "nt").astype(BF16)
            up_ref[half] = pre
            ext_ref[half, 0:HALO, :] = jnp.where(i == 0, 0.0, carry_ref[j, half])
            ext_ref[half, HALO:, :] = pre.astype(F32)
            carry_ref[j, half] = ext_ref[half, tm:tm + HALO, :]
        a, _ = _conv_taps(ext_ref, 0, cwa_ref, cba_ref)
        b, _ = _conv_taps(ext_ref, 1, cwb_ref, cbb_ref)
        _, gelu = _gelu_parts(a)
        act_ref[...] = (gelu * b).astype(BF16)

    def w_spec(half):
        return pl.BlockSpec((C, D_MODEL), lambda i, j: (j + half * nj, 0))

    def c_spec(rows, half):
        return pl.BlockSpec((rows, C), lambda i, j: (0, j + half * nj))

    return pl.pallas_call(
        body, grid=(S // tm, nj),
        in_specs=[pl.BlockSpec((tm, D_MODEL), lambda i, j: (i, 0)), w_spec(0), w_spec(1),
                  c_spec(3, 0), c_spec(3, 1), c_spec(1, 0), c_spec(1, 1)],
        out_specs=[pl.BlockSpec((2, tm, C), lambda i, j: (0, i, j)), pl.BlockSpec((tm, C), lambda i, j: (i, j))],
        out_shape=[jax.ShapeDtypeStruct((2, S, D_FF), BF16), jax.ShapeDtypeStruct((S, D_FF), BF16)],
        scratch_shapes=[pltpu.VMEM((2, HALO + tm, C), F32), pltpu.VMEM((nj, 2, HALO, C), F32)],
        compiler_params=_params("arbitrary", "arbitrary"), name="up_proj_convffn")(
            h, w_up_t, w_up_t, conv_w, conv_w, conv_b, conv_b)


def _convffn_bwd_a(up_pre, dx, w_down, conv_w, conv_b):
    S = up_pre.shape[1]
    T, C, nj, blk, halo, w_a, w_b, b_a, b_b = _conv_specs(S)
    def swap(spec):
        return pl.BlockSpec(spec.block_shape, lambda j, i, _m=spec.index_map: _m(i, j))

    def body(up_ref, halo_ref, dx_ref, wd_ref, wa_ref, wb_ref, ba_ref, bb_ref, dup_ref, dwa_ref, dwb_ref, dba_ref,
             dbb_ref, ext_ref):
        i = pl.program_id(1)
        _conv_load(ext_ref, halo_ref, up_ref, i == 0)

        @pl.when(i == 0)
        def _():
            for acc_ref in (dwa_ref, dwb_ref, dba_ref, dbb_ref):
                acc_ref[...] = jnp.zeros_like(acc_ref)

        for c in range(C // MXU_DIM):
            cs = slice(c * MXU_DIM, (c + 1) * MXU_DIM)
            dact = _dot(dx_ref[...], wd_ref[cs, :], "nt")
            a, xa = _conv_taps(ext_ref, 0, wa_ref, ba_ref, cs)
            b, xb = _conv_taps(ext_ref, 1, wb_ref, bb_ref, cs)
            cdf, gelu = _gelu_parts(a)
            dgelu = cdf + a * (jnp.exp(-0.5 * a * a) * (1.0 / math.sqrt(2.0 * math.pi)))
            da = dact * b * dgelu
            db = dact * gelu
            dup_ref[0, :, cs] = da.astype(BF16)
            dup_ref[1, :, cs] = db.astype(BF16)
            for dval, xs, dw_ref, dbias_ref in ((da, xa, dwa_ref, dba_ref), (db, xb, dwb_ref, dbb_ref)):
                dw_ref[:, cs] += jnp.concatenate([jnp.sum(dval * xk, axis=0, keepdims=True) for xk in xs], axis=0)
                dbias_ref[:, cs] += jnp.sum(dval, axis=0, keepdims=True)

    dx_spec = pl.BlockSpec((T, D_MODEL), lambda j, i: (i, 0))
    wd_spec = pl.BlockSpec((C, D_MODEL), lambda j, i: (j, 0))
    dw_spec = pl.BlockSpec((3, C), lambda j, i: (0, j))
    db_spec = pl.BlockSpec((1, C), lambda j, i: (0, j))
    res = pl.pallas_call(
        body, grid=(nj, S // T),
        in_specs=[swap(blk), swap(halo), dx_spec, wd_spec, swap(w_a), swap(w_b), swap(b_a), swap(b_b)],
        out_specs=[swap(blk), dw_spec, dw_spec, db_spec, db_spec],
        out_shape=[jax.ShapeDtypeStruct((2, S, D_FF), BF16), jax.ShapeDtypeStruct((3, D_FF), F32),
                   jax.ShapeDtypeStruct((3, D_FF), F32), jax.ShapeDtypeStruct((1, D_FF), F32),
                   jax.ShapeDtypeStruct((1, D_FF), F32)],
        scratch_shapes=[pltpu.VMEM((2, HALO + T, C), F32)],
        compiler_params=_params("parallel", "arbitrary"), name="convffn_bwd_a")(
            up_pre, up_pre, dx, w_down, conv_w, conv_w, conv_b, conv_b)
    dup, dwa, dwb, dba, dbb = res
    dconv_w = jnp.concatenate([dwa, dwb], axis=1)
    dconv_b = jnp.concatenate([dba, dbb], axis=1)
    return dup, dconv_w, dconv_b


def _convffn_bwd_b(dup, conv_w, w_up_t):
    S = dup.shape[1]
    tm, tk = 1024, D_FF // 4
    per_half = D_FF // tk
    nk = 2 * per_half
    per = tm // HALO
    n_blk = S // tm

    def body(d_ref, halo_ref, cw_ref, w_ref, dpre_ref, dh_ref, ext_ref, acc_ref):
        i, k = pl.program_id(0), pl.program_id(1)
        n = tm + HALO
        ext_ref[0:tm, :] = d_ref[...].astype(F32)
        ext_ref[tm:, :] = jnp.where(i == n_blk - 1, 0.0, halo_ref[...].astype(F32))
        ext = ext_ref[...]
        y = cw_ref[2:3, :] * ext[0:tm] + cw_ref[1:2, :] * pltpu.roll(ext, n - 1, 0)[0:tm]
        y = y + cw_ref[0:1, :] * pltpu.roll(ext, n - 2, 0)[0:tm]
        pre = y.astype(BF16)
        dpre_ref[...] = pre
        prod = _dot(pre, w_ref[...], "nn")

        @pl.when(k == 0)
        def _():
            acc_ref[...] = prod

        @pl.when(k > 0)
        def _():
            acc_ref[...] += prod

        @pl.when(k == nk - 1)
        def _():
            dh_ref[...] = acc_ref[...].astype(BF16)

    blk = pl.BlockSpec((None, tm, tk), lambda i, k: (k // per_half, i, k % per_half))
    halo = pl.BlockSpec((None, HALO, tk),
                        lambda i, k: (k // per_half, jnp.minimum((i + 1) * per, S // HALO - 1), k % per_half))
    return pl.pallas_call(
        body, grid=(n_blk, nk),
        in_specs=[blk, halo, pl.BlockSpec((3, tk), lambda i, k: (0, k)), pl.BlockSpec((tk, D_MODEL), lambda i, k: (k, 0))],
        out_specs=[blk, pl.BlockSpec((tm, D_MODEL), lambda i, k: (i, 0))],
        out_shape=[jax.ShapeDtypeStruct((2, S, D_FF), BF16), jax.ShapeDtypeStruct((S, D_MODEL), BF16)],
        scratch_shapes=[pltpu.VMEM((tm + HALO, tk), F32), pltpu.VMEM((tm, D_MODEL), F32)],
        compiler_params=_params("parallel", "arbitrary"), name="convffn_bwd_b")(dup, dup, conv_w, w_up_t)


def _position():
    return lax.axis_index("x"), lax.axis_index("y"), lax.axis_index("c")


def _flip(pos, k):
    x, y, c = pos
    return (1 - x if k & 4 else x, 1 - y if k & 2 else y, 1 - c if k & 1 else c)


def _index(pos):
    return 4 * pos[0] + 2 * pos[1] + pos[2]


_HBM = pl.BlockSpec(memory_space=pltpu.HBM)
_SEM = pl.BlockSpec(memory_space=pltpu.SEMAPHORE)
_ANY = pl.BlockSpec(memory_space=pl.ANY)
_EFFECT = pltpu.SideEffectType.DATAFLOW_SIDE_EFFECTING


def _in_hbm(a):
    return pltpu.with_memory_space_constraint(a, pltpu.HBM)


def _split_copies(name, srcs, lands, n_copies, plan, start_after, wait_after):
    ns, nl = len(srcs), len(lands)
    start_after = tuple(start_after)
    na = len(start_after)

    def start_body(*refs):
        src_refs, land_refs = refs[:ns], refs[ns:ns + nl]
        send_sems, recv_sems = refs[ns + nl + na], refs[ns + nl + na + 1]
        token = refs[-1]
        for k, (src, dst, _, peer) in enumerate(plan(src_refs, land_refs)):
            pltpu.make_async_remote_copy(src_ref=src, dst_ref=dst, send_sem=send_sems.at[k], recv_sem=recv_sems.at[k],
                                         device_id=peer, device_id_type=MESH).start()
        token[...] = jnp.zeros_like(token)

    thru = [pltpu.HBM(a.shape, a.dtype) for a in list(srcs) + list(lands)]
    res = pl.pallas_call(
        start_body, name=name + "_start",
        out_shape=[pltpu.SemaphoreType.DMA((n_copies,)), pltpu.SemaphoreType.DMA((n_copies,))] + thru
        + [jax.ShapeDtypeStruct((8, 128), F32)],
        in_specs=[_HBM] * (ns + nl) + [_ANY] * na,
        out_specs=[_SEM, _SEM] + [_HBM] * (ns + nl) + [pl.BlockSpec(memory_space=pltpu.VMEM)],
        input_output_aliases={i: 2 + i for i in range(ns + nl)},
        compiler_params=pltpu.CompilerParams(has_side_effects=_EFFECT),
    )(*[_in_hbm(a) for a in list(srcs) + list(lands)], *start_after)
    send_sems, recv_sems, token = res[0], res[1], res[-1]
    thru_vals = res[2:-1]

    def wait():
        def wait_body(*refs):
            src_refs, land_refs = refs[:ns], refs[ns:ns + nl]
            s_sems, r_sems = refs[ns + nl], refs[ns + nl + 1]
            for k, (src, _, dst, peer) in enumerate(plan(src_refs, land_refs)):
                cp = pltpu.make_async_remote_copy(src_ref=src, dst_ref=dst, send_sem=s_sems.at[k], recv_sem=r_sems.at[k],
                                                  device_id=peer, device_id_type=MESH)
                cp.wait_send()
                cp.wait_recv()

        out = pl.pallas_call(
            wait_body, name=name + "_wait", out_shape=thru,
            in_specs=[_HBM] * (ns + nl) + [_SEM, _SEM, _ANY], out_specs=[_HBM] * (ns + nl),
            input_output_aliases={i: i for i in range(ns + nl)},
            compiler_params=pltpu.CompilerParams(has_side_effects=_EFFECT),
        )(*thru_vals, send_sems, recv_sems, wait_after())
        return list(out[:ns]), list(out[ns:])

    return wait, token


def _gather_level1_plan(n):
    def plan(src_refs, land_refs):
        me = _position()
        x, y, c = me
        peers = [(x, y, 1 - c), (1 - x, y, c), (x, 1 - y, c), (1 - x, 1 - y, c)]
        return [(src_refs[a], land_refs[a].at[_index(me)], land_refs[a].at[_index(p)], p)
                for a in range(n) for p in peers]
    return plan


def _gather_level2(lands):
    n = len(lands)

    def body(*refs):
        lin = refs[:n]
        send_sems, recv_sems = refs[2 * n:]
        x, y, c = _position()
        sibling = (x, y, 1 - c)
        chips = [(1 - x, y), (x, 1 - y), (1 - x, 1 - y)]

        def copy(a, j, core):
            slot = lin[a].at[_index((*chips[j], core))]
            return pltpu.make_async_remote_copy(
                src_ref=slot, dst_ref=slot, send_sem=send_sems.at[3 * a + j], recv_sem=recv_sems.at[3 * a + j],
                device_id=sibling, device_id_type=MESH)

        sent = [copy(a, j, c) for a in range(n) for j in range(3)]
        for cp in sent:
            cp.start()
        for a in range(n):
            for j in range(3):
                copy(a, j, 1 - c).wait_recv()
        for cp in sent:
            cp.wait_send()

    return pl.pallas_call(
        body, out_shape=[jax.ShapeDtypeStruct(l.shape, l.dtype) for l in lands],
        in_specs=[_ANY] * n, out_specs=[_ANY] * n, input_output_aliases={a: a for a in range(n)},
        scratch_shapes=[pltpu.SemaphoreType.DMA((3 * n,)), pltpu.SemaphoreType.DMA((3 * n,))],
        name="gather_level2")(*lands)


def _gather_direct_plan(n):
    def plan(src_refs, land_refs):
        me = _position()
        peers = [_flip(me, k) for k in range(1, N_DEV)]
        return [(src_refs[a], land_refs[a].at[_index(me)], land_refs[a].at[_index(p)], p)
                for a in range(n) for p in peers]
    return plan


def _place_own(shards):
    me = _index(_position())
    return [lax.dynamic_update_slice(lax.empty((N_DEV,) + s.shape, s.dtype), s[None], (me, 0, 0)) for s in shards]


def _exchange_direct_plan(n):
    def plan(src_refs, land_refs):
        me = _position()
        peers = [_flip(me, k) for k in range(1, N_DEV)]
        return [(src_refs[a].at[_index(p)], land_refs[a].at[_index(me)], land_refs[a].at[_index(p)], p)
                for a in range(n) for p in peers]
    return plan


def _adamw(w, g, m, v):
    m = ADAM_B1 * m + (1.0 - ADAM_B1) * g
    v = ADAM_B2 * v + (1.0 - ADAM_B2) * (g * g)
    m_hat = m / (1.0 - ADAM_B1 ** ADAM_STEP)
    v_hat = v / (1.0 - ADAM_B2 ** ADAM_STEP)
    delta = -ADAM_LR * (m_hat / (jnp.sqrt(v_hat) + ADAM_EPS) + ADAM_WD * w)
    return delta, m, v


def _row_tile(rows, unit, cap=256):
    best = rows
    for t in range(unit, min(rows, cap) + 1, unit):
        if rows % t == 0:
            best = t
    return best


def _sum_partials(name, recv):
    n_src, R, C = recv.shape
    T = _row_tile(R, 16)

    def body(r_ref, g_out):
        g = r_ref[0].astype(F32)
        for s in range(1, n_src):
            g = g + r_ref[s].astype(F32)
        g_out[...] = g

    return pl.pallas_call(
        body, grid=(R // T,), in_specs=[pl.BlockSpec((n_src, T, C), lambda i: (0, i, 0))],
        out_specs=pl.BlockSpec((T, C), lambda i: (i, 0)), out_shape=jax.ShapeDtypeStruct((R, C), F32),
        compiler_params=_params("parallel"), name=name)(recv)


def _sum_adamw(name, recv, w, m, v):
    n_src, R, C = recv.shape
    T = _row_tile(R, 16)

    def body(r_ref, w_ref, m_ref, v_ref, g_out, d_out, m_out, v_out):
        g = r_ref[0].astype(F32)
        for s in range(1, n_src):
            g = g + r_ref[s].astype(F32)
        delta, mn, vn = _adamw(w_ref[...], g, m_ref[...], v_ref[...])
        g_out[...] = g
        d_out[...] = delta
        m_out[...] = mn
        v_out[...] = vn

    blk = pl.BlockSpec((T, C), lambda i: (i, 0))
    return pl.pallas_call(
        body, grid=(R // T,), in_specs=[pl.BlockSpec((n_src, T, C), lambda i: (0, i, 0)), blk, blk, blk],
        out_specs=[blk] * 4, out_shape=[jax.ShapeDtypeStruct((R, C), F32)] * 4,
        compiler_params=_params("parallel"), name=name)(recv, w, m, v)


def _adamw_update(name, w, g, m, v):
    R, C = w.shape
    T = _row_tile(R, 8)

    def body(w_ref, g_ref, m_ref, v_ref, d_out, m_out, v_out):
        delta, mn, vn = _adamw(w_ref[...], g_ref[...], m_ref[...], v_ref[...])
        d_out[...] = delta
        m_out[...] = mn
        v_out[...] = vn

    blk = pl.BlockSpec((T, C), lambda i: (i, 0))
    return pl.pallas_call(
        body, grid=(R // T,), in_specs=[blk] * 4, out_specs=[blk] * 3,
        out_shape=[jax.ShapeDtypeStruct((R, C), F32)] * 3, compiler_params=_params("parallel"), name=name)(w, g, m, v)


def _gather_vmem(p_ref, gath, send_sems, recv_sems):
    me = _position()
    gath[_index(me)] = p_ref[...]
    copies = []
    for k in range(1, N_DEV):
        peer = _flip(me, k)
        copies.append(pltpu.make_async_remote_copy(
            src_ref=p_ref, dst_ref=gath.at[_index(me)], send_sem=send_sems.at[k - 1],
            recv_sem=recv_sems.at[k - 1], device_id=peer, device_id_type=MESH))
    for cp in copies:
        cp.start()
    for k in range(1, N_DEV):
        peer = _flip(me, k)
        pltpu.make_async_remote_copy(
            src_ref=p_ref, dst_ref=gath.at[_index(peer)], send_sem=send_sems.at[k - 1],
            recv_sem=recv_sems.at[k - 1], device_id=peer, device_id_type=MESH).wait_recv()
    for cp in copies:
        cp.wait_send()


def _small_allreduce_adamw(part, w, m, v):
    R, C = part.shape

    def body(p_ref, w_ref, m_ref, v_ref, g_out, d_out, m_out, v_out, gath, send_sems, recv_sems):
        _gather_vmem(p_ref, gath, send_sems, recv_sems)
        g = gath[0]
        for s in range(1, N_DEV):
            g = g + gath[s]
        delta, mn, vn = _adamw(w_ref[...], g, m_ref[...], v_ref[...])
        g_out[...] = g
        d_out[...] = delta
        m_out[...] = mn
        v_out[...] = vn

    vm = pl.BlockSpec(memory_space=pltpu.VMEM)
    return pl.pallas_call(
        body, in_specs=[vm] * 4, out_specs=[vm] * 4, out_shape=[jax.ShapeDtypeStruct((R, C), F32)] * 4,
        scratch_shapes=[pltpu.VMEM((N_DEV, R, C), F32), pltpu.SemaphoreType.DMA((7,)), pltpu.SemaphoreType.DMA((7,))],
        name="small_allreduce_adamw")(part, w, m, v)


_TRANSPOSED = ("w_in", "w_up", "w_pool_out", "w_attn_out")
_SMALL = (("g_mix", 2048), ("b_gate", 4096), ("pool_scale", 1024), ("g_ffn", 2048), ("conv_b", 11264), ("g_final", 2048))
CONV_ROWS = 8


def _to_comm(name, shard):
    return shard.T if name in _TRANSPOSED else shard


def _pack_small(vals):
    flat = [vals[n].reshape(-1) for n, _ in _SMALL]
    used = sum(f.shape[0] for f in flat)
    flat.append(jnp.zeros((SMALL_ROWS * PACK_COLS - used,), F32))
    return jnp.concatenate(flat).reshape(SMALL_ROWS, PACK_COLS)


def _unpack_small(packed, like):
    flat = packed.reshape(-1)
    out, off = {}, 0
    for name, n in _SMALL:
        out[name] = flat[off:off + n].reshape(like[name].shape)
        off += n
    return out


def _local_step(x, target, W, small, first_token=0.0, first_weights=None, late_weights=None, emit=None):
    S = x.shape[0]
    o_qkv, o_gate = POOL_WIDTH, POOL_WIDTH + QKV_WIDTH
    if emit is None:
        emit = lambda grads: None

    def zero(token):
        return 0.0 if token is None else token[0, 0]

    h1 = _rms_fwd("rms1_fwd", x, small["g_mix"] + first_token)
    tok = None
    if first_weights is not None:
        w_first, tok = first_weights(h1)
        small = dict(small, conv_w_full=w_first.pop("conv_w_full"))
        W = dict(W, **w_first)
    w_in_t = W["w_in"]
    u = _mm_nt("proj_u", h1, w_in_t, tm=2048, tn=512, tk=2048, o_dtype=F32, n_rows=POOL_WIDTH, b_row_off=0,
               after=tok)
    qkv = _mm_nt("proj_qkv", h1, w_in_t, tm=2048, tn=512, tk=2048, o_dtype=BF16, n_rows=QKV_WIDTH, b_row_off=o_qkv,
                 after=tok)
    gates_pre = _mm_nt("proj_gates", h1, w_in_t, tm=2048, tn=512, tk=2048, o_dtype=BF16, n_rows=2 * D_MODEL,
                       b_row_off=o_gate, after=tok)
    pool_y = _pool_fwd(u, W["w_pool_lin"], small["pool_scale"])
    outs, lses = zip(*[_attn_fwd(gi, qkv) for gi in range(3)])
    attn_y = _merge_fwd(outs, lses)
    if late_weights is not None:
        W = dict(W, **late_weights(0, attn_y))
    w_up_t = W["w_up"]
    mixed, y_pool, y_attn = _mix_fwd(pool_y, attn_y, W["w_pool_out"], W["w_attn_out"], gates_pre, small["b_gate"])
    x1, h2 = _out_proj_norm(mixed, W["w_out"], x, small["g_ffn"])

    nj = D_FF // 512
    up_pre, act = _up_proj_convffn(h2, w_up_t, small["conv_w_full"], small["conv_b"])
    if late_weights is not None:
        W = dict(W, **late_weights(1, act))
    x2 = _mm_nn("down_proj", act, W["w_down"], tm=1024, tn=1024, tk=D_FF // 2, o_dtype=F32, res=x1)

    dx2, dx2_b, dg_final, sq_cols = _final_loss("final_loss", x2, small["g_final"], target)

    dw_down = _mm_tn("dw_down", act, dx2_b, tm=512, tn=2048, tk=TOKEN_TK)
    tok = emit(dict(w_down=dw_down))
    dup, dconv_w, dconv_b = _convffn_bwd_a(up_pre, dx2_b, W["w_down"], small["conv_w_full"],
                                           small["conv_b"] + zero(tok))
    dup_pre, dh2 = _convffn_bwd_b(dup, small["conv_w_full"], w_up_t)
    dw_up = _mm("dw_up", dup_pre, h2, mode="tn", grid=(2 * nj, 1, S // TOKEN_TK),
                a_spec=pl.BlockSpec((None, TOKEN_TK, 512), lambda i, j, k: (i // nj, k, i % nj)),
                b_spec=pl.BlockSpec((TOKEN_TK, D_MODEL), lambda i, j, k: (k, 0)),
                o_spec=pl.BlockSpec((512, D_MODEL), lambda i, j, k: (i, 0)),
                o_shape=(2 * D_FF, D_MODEL), o_dtype=BF16, acc_shape=(512, D_MODEL))
    tok = emit(dict(w_up=dw_up, conv_w=dconv_w))
    dx1, dx1_b, dg_ffn = _rms_bwd("rms2_bwd", x1, small["g_ffn"] + zero(tok), dh2, dx2, with_bf16=True)

    dw_out = _mm_tn("dw_out", mixed, dx1_b, tm=512, tn=2048, tk=TOKEN_TK)
    tok = emit(dict(w_out=dw_out))
    dy_pool, dy_attn, dgates, db_gate = _mix_bwd(dx1_b, W["w_out"], gates_pre, small["b_gate"] + zero(tok),
                                                 y_pool, y_attn)
    dw_pool_out = _mm_tn("dw_pool_out", dy_pool, pool_y, tm=1024, tn=1024, tk=TOKEN_TK)
    dw_attn_out = _mm_tn("dw_attn_out", dy_attn, attn_y, tm=1024, tn=512, tk=TOKEN_TK)
    e, dw_pool_lin, dpool_scale = _pool_bwd_a(u, dy_pool, W["w_pool_out"], W["w_pool_lin"], small["pool_scale"])
    du = _pool_bwd_b(e)
    dos, dps = _merge_bwd(dy_attn, W["w_attn_out"], outs, lses)
    dqkv = [_attn_bwd(gi, qkv, dos[gi], lses[gi], dps[gi]) for gi in range(3)]
    dproj = jnp.concatenate([du] + [dqkv[gi][sec] for sec in range(3) for gi in range(3)] + [dgates], axis=1)
    dw_in = _mm_tn("dw_in", dproj, h1, tm=512, tn=2048, tk=TOKEN_TK)
    tok = emit(dict(w_in=dw_in, w_pool_out=dw_pool_out, w_attn_out=dw_attn_out, w_pool_lin=dw_pool_lin))
    dh1 = _mm_nn("d_h1", dproj, w_in_t, tm=1024, tn=1024, tk=IN_WIDTH // 4, o_dtype=BF16,
                 after=tok)
    grad_x, dg_mix = _rms_bwd("rms1_bwd", x, small["g_mix"] + zero(tok), dh1, dx1, with_bf16=False)

    grads = dict(w_in=dw_in, w_up=dw_up, w_down=dw_down, w_out=dw_out, w_pool_out=dw_pool_out,
                 w_attn_out=dw_attn_out, w_pool_lin=dw_pool_lin, conv_w=dconv_w)
    small_grads = dict(g_mix=dg_mix, b_gate=db_gate, pool_scale=dpool_scale, g_ffn=dg_ffn, conv_b=dconv_b,
                       g_final=dg_final)
    return sq_cols, grad_x, grads, small_grads


def kernel(x, g_mix, w_in, b_gate, w_pool_lin, pool_scale, w_pool_out, w_attn_out, w_out, g_ffn, w_up, conv_w, conv_b, w_down, g_final, loss_target, m_g_mix, m_w_in, m_b_gate, m_w_pool_lin, m_pool_scale, m_w_pool_out, m_w_attn_out, m_w_out, m_g_ffn, m_w_up, m_conv_w, m_conv_b, m_w_down, m_g_final, v_g_mix, v_w_in, v_b_gate, v_w_pool_lin, v_pool_scale, v_w_pool_out, v_w_attn_out, v_w_out, v_g_ffn, v_w_up, v_conv_w, v_conv_b, v_w_down, v_g_final):
    given = dict(g_mix=g_mix, w_in=w_in, b_gate=b_gate, w_pool_lin=w_pool_lin, pool_scale=pool_scale,
                 w_pool_out=w_pool_out, w_attn_out=w_attn_out, w_out=w_out, g_ffn=g_ffn, w_up=w_up, conv_w=conv_w,
                 conv_b=conv_b, w_down=w_down, g_final=g_final)
    mom_m = dict(g_mix=m_g_mix, w_in=m_w_in, b_gate=m_b_gate, w_pool_lin=m_w_pool_lin, pool_scale=m_pool_scale,
                 w_pool_out=m_w_pool_out, w_attn_out=m_w_attn_out, w_out=m_w_out, g_ffn=m_g_ffn, w_up=m_w_up,
                 conv_w=m_conv_w, conv_b=m_conv_b, w_down=m_w_down, g_final=m_g_final)
    mom_v = dict(g_mix=v_g_mix, w_in=v_w_in, b_gate=v_b_gate, w_pool_lin=v_w_pool_lin, pool_scale=v_pool_scale,
                 w_pool_out=v_w_pool_out, w_attn_out=v_w_attn_out, w_out=v_w_out, g_ffn=v_g_ffn, w_up=v_w_up,
                 conv_w=v_conv_w, conv_b=v_conv_b, w_down=v_w_down, g_final=v_g_final)
    def full(g):
        return g.reshape(N_DEV * g.shape[1], g.shape[2])

    shard = dict(w_in=_to_comm("w_in", w_in[0]).astype(BF16))
    W = {}
    first = [shard["w_in"], w_pool_lin[0].reshape(4 * 32, POOL_GROUP_WIDTH),
             jnp.pad(conv_w[0], ((0, CONV_ROWS - 3), (0, 0)))]
    hold = {}
    w_in_wait, first_token = _split_copies(
        "gather_w_in", first, _place_own(first), 4 * len(first), _gather_level1_plan(len(first)),
        start_after=(), wait_after=lambda: hold["h1"])

    stages = (("w_out", "w_pool_out", "w_attn_out", "w_up"), ("w_down",))
    stage_after = [None, None]
    stage_wait = []
    for n in (n for names in stages for n in names):
        shard[n] = _to_comm(n, given[n][0] + first_token[0, 0]).astype(BF16)

    def first_weights(h1):
        hold["h1"] = h1
        w_in_all, lin_all, conv_w_all = _gather_level2(w_in_wait()[1])
        w_in_full = full(w_in_all)
        lin_full = jnp.moveaxis(lin_all.reshape(N_DEV, 4, 32, POOL_GROUP_WIDTH), 0, 1).reshape(
            4, POOL_GROUP_WIDTH, POOL_GROUP_WIDTH).astype(BF16)
        conv_w_full = jnp.moveaxis(conv_w_all[:, :3, :], 0, 1).reshape(3, 2 * D_FF)
        order_after = (w_in_full,)
        for k, names in enumerate(stages):
            srcs = [shard[n] for n in names]
            wait, token = _split_copies(f"gather_stage{k}", srcs, _place_own(srcs), 7 * len(names),
                                        _gather_direct_plan(len(names)), start_after=order_after,
                                        wait_after=lambda k=k: stage_after[k])
            stage_wait.append(wait)
            order_after = (token,)
        return dict(w_in=w_in_full, w_pool_lin=lin_full, conv_w_full=conv_w_full), token

    def late_weights(stage, after):
        stage_after[stage] = after
        return {n: full(l) for n, l in zip(stages[stage], stage_wait[stage]()[1])}

    small = dict(g_mix=g_mix, b_gate=b_gate, pool_scale=pool_scale, g_ffn=g_ffn, conv_b=conv_b,
                 g_final=g_final.reshape(1, D_MODEL))

    pending = []
    done = {}

    def emit(group):
        names = list(group)
        parts = []
        for n in names:
            g = group[n]
            if n == "w_pool_lin":
                g = jnp.moveaxis(g.reshape(4, N_DEV, 32, POOL_GROUP_WIDTH), 1, 0).reshape(
                    N_DEV, 4 * 32, POOL_GROUP_WIDTH).astype(BF16)
            elif n == "conv_w":
                g = jnp.pad(jnp.moveaxis(g.reshape(3, N_DEV, 2 * D_FF // N_DEV), 1, 0),
                            ((0, 0), (0, CONV_ROWS - 3), (0, 0))).astype(BF16)
            else:
                g = g.reshape(N_DEV, g.shape[0] // N_DEV, g.shape[1])
            parts.append(g)
        me = _index(_position())
        lands = [lax.dynamic_update_slice(lax.empty(p.shape, p.dtype), lax.dynamic_index_in_dim(p, me, 0),
                                          (me, 0, 0)) for p in parts]
        wait, token = _split_copies(
            "exchange_" + names[0], parts, lands, 7 * len(names), _exchange_direct_plan(len(names)),
            start_after=(), wait_after=lambda: done["grad_x"])
        pending.append((names, wait))
        return token

    prepared = sum(shard[n][0, 0].astype(F32) for names in stages for n in names) * 0.0
    sq_cols, grad_x, _, small_grads = _local_step(x[0], loss_target[0], W, small,
                                                  first_token=first_token[0, 0] + prepared,
                                                  first_weights=first_weights, late_weights=late_weights, emit=emit)
    done["grad_x"] = grad_x
    loss = lax.psum(0.5 / D_MODEL * jnp.sum(sq_cols), AXES)

    res = {}
    for n, r in [(n, r) for names, wait in pending for n, r in zip(names, wait()[1])]:
        shape = given[n].shape
        rows = math.prod(shape[:-1])
        w2, m2, v2 = (d[n].reshape(rows, shape[-1]) for d in (given, mom_m, mom_v))
        if n in _TRANSPOSED or n == "conv_w":
            g = _sum_partials("sum_" + n, r)
            g = g.T if n in _TRANSPOSED else g[:3]
            upd = (g,) + tuple(_adamw_update("adamw_" + n, w2, g, m2, v2))
        else:
            upd = _sum_adamw("adamw_" + n, r, w2, m2, v2)
        res[n] = [a.reshape(shape) for a in upd]

    packed_small = [_pack_small(d) for d in (given, mom_m, mom_v)]
    res_small = [_unpack_small(p, given) for p in _small_allreduce_adamw(_pack_small(small_grads), *packed_small)]

    order = ["g_mix", "w_in", "b_gate", "w_pool_lin", "pool_scale", "w_pool_out", "w_attn_out", "w_out", "g_ffn",
             "w_up", "conv_w", "conv_b", "w_down", "g_final"]
    outs = [loss, grad_x[None]]
    for kind in range(4):
        for n in order:
            outs.append(res[n][kind] if n in res else res_small[kind][n])
    return tuple(outs)
```

```python
import math

import jax
import jax.numpy as jnp
from jax import lax
from jax.experimental import pallas as pl
from jax.experimental.pallas import tpu as pltpu

F32 = jnp.float32
BF16 = jnp.bfloat16
MESH = pl.DeviceIdType.MESH
AXES = ("x", "y", "c")
N_DEV = 8

D_MODEL = 2048
POOL_WINDOWS = (2, 4, 8, 16)
POOL_GROUP_WIDTH = 256
POOL_WIDTH = 1024
ATTN_GROUPS = ((128, 1), (512, 4), (2048, 16))
SPAN = 128
HEADS_PER_GROUP = 4
N_ATTN_HEADS = 12
HEAD_DIM = 128
ATTN_WIDTH = 1536
QKV_WIDTH = 3 * ATTN_WIDTH
ATTN_OUT_WIDTH = 512
IN_WIDTH = 9728
D_FF = 5632
RMS_EPS = 1e-6
ADAM_LR, ADAM_B1, ADAM_B2, ADAM_EPS, ADAM_WD, ADAM_STEP = 0.001, 0.9, 0.999, 1e-08, 0.01, 10

VMEM_LIMIT_BYTES = 48 * 1024 * 1024
HALO = 16
TOKEN_TK = 2048
MXU_DIM = 256
PACK_COLS = 1024
SMALL_ROWS = 24
NEG_BIG = -1e30

_DN = {"nn": (((1,), (0,)), ((), ())), "nt": (((1,), (1,)), ((), ())), "tn": (((0,), (0,)), ((), ()))}


def _params(*sem):
    return pltpu.CompilerParams(dimension_semantics=sem, vmem_limit_bytes=VMEM_LIMIT_BYTES)


def _dot(a, b, mode):
    return lax.dot_general(a.astype(BF16), b.astype(BF16), _DN[mode], preferred_element_type=F32)


def _mm(name, a, b, *, mode, grid, a_spec, b_spec, o_spec, o_shape, o_dtype, acc_shape, res=None, res_spec=None,
        after=None):
    nk = grid[2]
    n_in = 2 + (res is not None) + (after is not None)

    def body(*refs):
        a_ref, b_ref = refs[:2]
        r_ref = refs[2] if res is not None else None
        o_ref = refs[n_in]
        scr = refs[n_in + 1:]
        prod = _dot(a_ref[...], b_ref[...], mode)

        def finish(val):
            if r_ref is not None:
                val = val + r_ref[...]
            o_ref[...] = val.astype(o_ref.dtype)

        if nk == 1:
            finish(prod)
        else:
            acc = scr[0]
            k = pl.program_id(2)

            @pl.when(k == 0)
            def _():
                acc[...] = prod

            @pl.when(k > 0)
            def _():
                acc[...] += prod

            @pl.when(k == nk - 1)
            def _():
                finish(acc[...])

    in_specs = [a_spec, b_spec]
    args = [a, b]
    if res is not None:
        in_specs.append(res_spec)
        args.append(res)
    if after is not None:
        in_specs.append(pl.BlockSpec((8, 128), lambda i, j, k: (0, 0)))
        args.append(after)
    return pl.pallas_call(
        body, grid=grid, in_specs=in_specs, out_specs=o_spec,
        out_shape=jax.ShapeDtypeStruct(o_shape, o_dtype),
        scratch_shapes=[] if nk == 1 else [pltpu.VMEM(acc_shape, F32)],
        compiler_params=_params("parallel", "parallel", "arbitrary"), name=name)(*args)


def _mm_nn(name, a, b, *, tm, tn, tk, o_dtype, res=None, after=None):
    M, K = a.shape
    N = b.shape[1]
    return _mm(name, a, b, mode="nn", grid=(M // tm, N // tn, K // tk),
               a_spec=pl.BlockSpec((tm, tk), lambda i, j, k: (i, k)),
               b_spec=pl.BlockSpec((tk, tn), lambda i, j, k: (k, j)),
               o_spec=pl.BlockSpec((tm, tn), lambda i, j, k: (i, j)),
               o_shape=(M, N), o_dtype=o_dtype, acc_shape=(tm, tn), res=res,
               res_spec=pl.BlockSpec((tm, tn), lambda i, j, k: (i, j)), after=after)


def _mm_nt(name, a, b, *, tm, tn, tk, o_dtype, n_rows=None, b_row_off=0, after=None):
    M, K = a.shape
    N = b.shape[0] if n_rows is None else n_rows
    off = b_row_off // tn
    return _mm(name, a, b, mode="nt", grid=(M // tm, N // tn, K // tk),
               a_spec=pl.BlockSpec((tm, tk), lambda i, j, k: (i, k)),
               b_spec=pl.BlockSpec((tn, tk), lambda i, j, k: (j + off, k)),
               o_spec=pl.BlockSpec((tm, tn), lambda i, j, k: (i, j)),
               o_shape=(M, N), o_dtype=o_dtype, acc_shape=(tm, tn), after=after)


def _mm_tn(name, a, b, *, tm, tn, tk):
    K, M = a.shape
    N = b.shape[1]
    return _mm(name, a, b, mode="tn", grid=(M // tm, N // tn, K // tk),
               a_spec=pl.BlockSpec((tk, tm), lambda i, j, k: (k, i)),
               b_spec=pl.BlockSpec((tk, tn), lambda i, j, k: (k, j)),
               o_spec=pl.BlockSpec((tm, tn), lambda i, j, k: (i, j)),
               o_shape=(M, N), o_dtype=BF16, acc_shape=(tm, tn))


def _rms_fwd(name, x, g):
    S, D = x.shape
    T = 512

    def body(x_ref, g_ref, h_ref):
        xv = x_ref[...]
        r = lax.rsqrt(jnp.mean(xv * xv, axis=-1, keepdims=True) + RMS_EPS)
        h_ref[...] = (xv * r * g_ref[...]).astype(BF16)

    return pl.pallas_call(
        body, grid=(S // T,),
        in_specs=[pl.BlockSpec((T, D), lambda i: (i, 0)), pl.BlockSpec((1, D), lambda i: (0, 0))],
        out_specs=pl.BlockSpec((T, D), lambda i: (i, 0)),
        out_shape=jax.ShapeDtypeStruct((S, D), BF16), compiler_params=_params("parallel"), name=name)(x, g)


def _out_proj_norm(mixed, w_out, x, g):
    S, D = x.shape
    tm = 512

    def body(m_ref, w_ref, x_ref, g_ref, x1_ref, h_ref):
        x1 = x_ref[...] + _dot(m_ref[...], w_ref[...], "nn")
        x1_ref[...] = x1
        r = lax.rsqrt(jnp.mean(x1 * x1, axis=-1, keepdims=True) + RMS_EPS)
        h_ref[...] = (x1 * r * g_ref[...]).astype(BF16)

    row = pl.BlockSpec((tm, D), lambda i: (i, 0))
    return pl.pallas_call(
        body, grid=(S // tm,),
        in_specs=[row, pl.BlockSpec((D, D), lambda i: (0, 0)), row, pl.BlockSpec((1, D), lambda i: (0, 0))],
        out_specs=[row, row], out_shape=[jax.ShapeDtypeStruct((S, D), F32), jax.ShapeDtypeStruct((S, D), BF16)],
        compiler_params=_params("parallel"), name="out_proj_norm")(mixed, w_out, x, g)


def _rms_bwd(name, x, g, dh, dres, with_bf16):
    S, D = x.shape
    T = 256

    def body(x_ref, g_ref, dh_ref, dres_ref, dx_ref, *rest):
        dg_ref = rest[-1]
        i = pl.program_id(0)
        xv = x_ref[...]
        r = lax.rsqrt(jnp.mean(xv * xv, axis=-1, keepdims=True) + RMS_EPS)
        xhat = xv * r
        dhv = dh_ref[...].astype(F32)
        gdh = dhv * g_ref[...]
        c = jnp.mean(xhat * gdh, axis=-1, keepdims=True)
        dx = dres_ref[...] + r * (gdh - xhat * c)
        dx_ref[...] = dx
        if with_bf16:
            rest[0][...] = dx.astype(BF16)
        part = jnp.sum(dhv * xhat, axis=0, keepdims=True)

        @pl.when(i == 0)
        def _():
            dg_ref[...] = part

        @pl.when(i > 0)
        def _():
            dg_ref[...] += part

    row = pl.BlockSpec((T, D), lambda i: (i, 0))
    vec = pl.BlockSpec((1, D), lambda i: (0, 0))
    narrow = [jax.ShapeDtypeStruct((S, D), BF16)] if with_bf16 else []
    return pl.pallas_call(
        body, grid=(S // T,), in_specs=[row, vec, row, row], out_specs=[row] * (1 + len(narrow)) + [vec],
        out_shape=[jax.ShapeDtypeStruct((S, D), F32)] + narrow + [jax.ShapeDtypeStruct((1, D), F32)],
        compiler_params=_params("arbitrary"), name=name)(x, g, dh, dres)


def _final_loss(name, x, g, target):
    S, D = x.shape
    T = 256

    def body(x_ref, g_ref, t_ref, dx_ref, dxb_ref, dg_ref, sq_ref):
        i = pl.program_id(0)
        xv = x_ref[...]
        r = lax.rsqrt(jnp.mean(xv * xv, axis=-1, keepdims=True) + RMS_EPS)
        xhat = xv * r
        err = xhat * g_ref[...] - t_ref[...]
        dy = err * (1.0 / D)
        gdy = dy * g_ref[...]
        c = jnp.mean(xhat * gdy, axis=-1, keepdims=True)
        dx = r * (gdy - xhat * c)
        dx_ref[...] = dx
        dxb_ref[...] = dx.astype(BF16)
        dg_part = jnp.sum(dy * xhat, axis=0, keepdims=True)
        sq_part = jnp.sum(err * err, axis=0, keepdims=True)

        @pl.when(i == 0)
        def _():
            dg_ref[...] = dg_part
            sq_ref[...] = sq_part

        @pl.when(i > 0)
        def _():
            dg_ref[...] += dg_part
            sq_ref[...] += sq_part

    row = pl.BlockSpec((T, D), lambda i: (i, 0))
    vec = pl.BlockSpec((1, D), lambda i: (0, 0))
    return pl.pallas_call(
        body, grid=(S // T,), in_specs=[row, vec, row], out_specs=[row, row, vec, vec],
        out_shape=[jax.ShapeDtypeStruct((S, D), F32), jax.ShapeDtypeStruct((S, D), BF16),
                   jax.ShapeDtypeStruct((1, D), F32), jax.ShapeDtypeStruct((1, D), F32)],
        compiler_params=_params("arbitrary"), name=name)(x, g, target)


def _prev_halo_spec(T, C, col_map):
    per = T // HALO
    return pl.BlockSpec((HALO, C), lambda *g: (jnp.maximum(g[0] * per - 1, 0), col_map(*g)))


def _pool_window_sums(ext, T, i):
    t = i * T + lax.broadcasted_iota(jnp.int32, (T, 1), 0)
    outs = []
    cnts = []
    for gi, w in enumerate(POOL_WINDOWS):
        s = ext[:, gi * POOL_GROUP_WIDTH:(gi + 1) * POOL_GROUP_WIDTH]
        sh = 1
        while sh < w:
            s = s + pltpu.roll(s, sh, 0)
            sh *= 2
        cnt = jnp.minimum(t + 1, w).astype(F32)
        outs.append(s[HALO:] / cnt - ext[HALO:, gi * POOL_GROUP_WIDTH:(gi + 1) * POOL_GROUP_WIDTH])
        cnts.append(cnt)
    return outs, cnts


def _load_ext(ext_ref, halo_ref, blk_ref, first):
    ext_ref[0:HALO, :] = jnp.where(first, 0.0, halo_ref[...].astype(F32))
    ext_ref[HALO:, :] = blk_ref[...].astype(F32)


def _pool_fwd(u, w_lin, scale):
    S = u.shape[0]
    T = 512
    W = POOL_WIDTH

    def body(u_ref, halo_ref, wl_ref, sc_ref, y_ref, ext_ref):
        i = pl.program_id(0)
        _load_ext(ext_ref, halo_ref, u_ref, i == 0)
        pooled, _ = _pool_window_sums(ext_ref[...], T, i)
        for gi in range(4):
            z = _dot(pooled[gi], wl_ref[gi], "nn")
            cs = slice(gi * POOL_GROUP_WIDTH, (gi + 1) * POOL_GROUP_WIDTH)
            y_ref[:, cs] = (z * sc_ref[:, cs]).astype(BF16)

    return pl.pallas_call(
        body, grid=(S // T,),
        in_specs=[pl.BlockSpec((T, W), lambda i: (i, 0)), _prev_halo_spec(T, W, lambda i: 0),
                  pl.BlockSpec((4, 256, 256), lambda i: (0, 0, 0)), pl.BlockSpec((1, W), lambda i: (0, 0))],
        out_specs=pl.BlockSpec((T, W), lambda i: (i, 0)),
        out_shape=jax.ShapeDtypeStruct((S, W), BF16),
        scratch_shapes=[pltpu.VMEM((HALO + T, W), F32)],
        compiler_params=_params("parallel"), name="pool_fwd")(u, u, w_lin, scale)


def _pool_bwd_a(u, dy_out, w_pool_out_t, w_lin, scale):
    S = u.shape[0]
    T = 512
    W = POOL_WIDTH

    def body(u_ref, halo_ref, dyo_ref, wpo_ref, wl_ref, sc_ref, e_ref, dwl_ref, dsc_ref, ext_ref):
        i = pl.program_id(0)
        dy = _dot(dyo_ref[...], wpo_ref[...], "nn")
        _load_ext(ext_ref, halo_ref, u_ref, i == 0)
        pooled, cnts = _pool_window_sums(ext_ref[...], T, i)
        for gi in range(4):
            cs = slice(gi * POOL_GROUP_WIDTH, (gi + 1) * POOL_GROUP_WIDTH)
            z = _dot(pooled[gi], wl_ref[gi], "nn")
            dyg = dy[:, cs]
            dsc = jnp.sum(dyg * z, axis=0, keepdims=True)
            dz = dyg * sc_ref[:, cs]
            dwl = _dot(pooled[gi], dz, "tn")
            dpooled = _dot(dz, wl_ref[gi], "nt")
            e_ref[:, cs] = dpooled / cnts[gi]

            @pl.when(i == 0)
            def _():
                dwl_ref[gi] = dwl
                dsc_ref[:, cs] = dsc

            @pl.when(i > 0)
            def _():
                dwl_ref[gi] += dwl
                dsc_ref[:, cs] += dsc

    return pl.pallas_call(
        body, grid=(S // T,),
        in_specs=[pl.BlockSpec((T, W), lambda i: (i, 0)), _prev_halo_spec(T, W, lambda i: 0),
                  pl.BlockSpec((T, D_MODEL), lambda i: (i, 0)), pl.BlockSpec((D_MODEL, W), lambda i: (0, 0)),
                  pl.BlockSpec((4, 256, 256), lambda i: (0, 0, 0)), pl.BlockSpec((1, W), lambda i: (0, 0))],
        out_specs=[pl.BlockSpec((T, W), lambda i: (i, 0)), pl.BlockSpec((4, 256, 256), lambda i: (0, 0, 0)),
                   pl.BlockSpec((1, W), lambda i: (0, 0))],
        out_shape=[jax.ShapeDtypeStruct((S, W), F32), jax.ShapeDtypeStruct((4, 256, 256), F32),
                   jax.ShapeDtypeStruct((1, W), F32)],
        scratch_shapes=[pltpu.VMEM((HALO + T, W), F32)],
        compiler_params=_params("arbitrary"), name="pool_bwd_a")(u, u, dy_out, w_pool_out_t, w_lin, scale)


def _pool_bwd_b(e):
    S = e.shape[0]
    T = 512
    W = POOL_WIDTH
    per = T // HALO
    n_blk = S // T

    def body(e_ref, halo_ref, du_ref, ext_ref):
        i = pl.program_id(0)
        ext_ref[0:T, :] = e_ref[...]
        ext_ref[T:, :] = jnp.where(i == n_blk - 1, 0.0, halo_ref[...])
        ext = ext_ref[...]
        n = T + HALO
        t = i * T + lax.broadcasted_iota(jnp.int32, (T, 1), 0)
        for gi, w in enumerate(POOL_WINDOWS):
            cs = slice(gi * POOL_GROUP_WIDTH, (gi + 1) * POOL_GROUP_WIDTH)
            s = ext[:, cs]
            sh = 1
            while sh < w:
                s = s + pltpu.roll(s, n - sh, 0)
                sh *= 2
            cnt = jnp.minimum(t + 1, w).astype(F32)
            du_ref[:, cs] = (s[0:T] - ext[0:T, cs] * cnt).astype(BF16)

    return pl.pallas_call(
        body, grid=(n_blk,),
        in_specs=[pl.BlockSpec((T, W), lambda i: (i, 0)),
                  pl.BlockSpec((HALO, W), lambda i: (jnp.minimum((i + 1) * per, S // HALO - 1), 0))],
        out_specs=pl.BlockSpec((T, W), lambda i: (i, 0)),
        out_shape=jax.ShapeDtypeStruct((S, W), BF16),
        scratch_shapes=[pltpu.VMEM((T + HALO, W), F32)],
        compiler_params=_params("parallel"), name="pool_bwd_b")(e, e)


def _slope(head):
    return 2.0 ** (-8.0 * (head + 1) / N_ATTN_HEADS)


def _attn_masks(b):
    row = lax.broadcasted_iota(jnp.int32, (SPAN, SPAN), 0)
    col = lax.broadcasted_iota(jnp.int32, (SPAN, SPAN), 1)
    dist_cur = (row - col).astype(F32)
    valid_cur = col <= row
    valid_prev = jnp.logical_and(col >= row, b > 0)
    return dist_cur, valid_cur, valid_prev


ATTN_BATCH = 16


def _attn_tiling(gi, S):
    d = ATTN_GROUPS[gi][1]
    hp = HEADS_PER_GROUP if d < 16 else 1
    return d, SPAN * d, S // (SPAN * d), hp, hp * HEAD_DIM, min(ATTN_BATCH // hp, d)


_BATCHED_DN = {"nt": (((2,), (2,)), ((0,), (0,))), "nn": (((2,), (1,)), ((0,), (0,))), "tn": (((1,), (1,)), ((0,), (0,)))}


def _bdot(a, b, mode):
    return lax.dot_general(a.astype(BF16), b.astype(BF16), _BATCHED_DN[mode], preferred_element_type=F32)


def _batch_rows(it, d, rb):
    return [pl.ds(it * rb + j, SPAN, stride=d) for j in range(rb)]


def _take(scr, hp, rows):
    return jnp.stack([scr[hi, rw, :] for hi in range(hp) for rw in rows])


def _put(scr, hp, rows, val):
    for n, (hi, rw) in enumerate((hi, rw) for hi in range(hp) for rw in rows):
        scr[hi, rw, :] = val[n]


def _batch_bias(gi, d, hp, rb, dist):
    return jnp.stack([(-(_head_slope(gi, hp, hi) * d)) * dist for hi in range(hp) for _ in range(rb)])


def _head_slope(gi, hp, hi):
    if hp == HEADS_PER_GROUP:
        return _slope(gi * HEADS_PER_GROUP + hi)
    first = pl.program_id(1) * hp
    sl = jnp.float32(_slope(gi * HEADS_PER_GROUP + hi))
    for h0 in range(hp, HEADS_PER_GROUP, hp):
        sl = jnp.where(first == h0, jnp.float32(_slope(gi * HEADS_PER_GROUP + h0 + hi)), sl)
    return sl


def _split_heads(src_ref, dst_ref, hp):
    for hi in range(hp):
        dst_ref[hi] = src_ref[:, hi * HEAD_DIM:(hi + 1) * HEAD_DIM].astype(F32)


def _attn_fwd(gi, qkv):
    S = qkv.shape[0]
    d, C, nb, hp, W, rb = _attn_tiling(gi, S)
    scale = HEAD_DIM ** -0.5

    def body(q_ref, kc_ref, kp_ref, vc_ref, vp_ref, o_ref, lse_ref, qs, kcs, kps, vcs, vps, os, ls):
        b = pl.program_id(0)
        for src, dst in ((q_ref, qs), (kc_ref, kcs), (kp_ref, kps), (vc_ref, vcs), (vp_ref, vps)):
            _split_heads(src, dst, hp)
        dist_cur, valid_cur, valid_prev = _attn_masks(b)
        bias_cur = _batch_bias(gi, d, hp, rb, dist_cur)
        bias_far = _batch_bias(gi, d, hp, rb, dist_cur + SPAN)

        def sub_sequences(it, carry):
            rows = _batch_rows(it, d, rb)
            q = _take(qs, hp, rows)
            s_cur = jnp.where(valid_cur, _bdot(q, _take(kcs, hp, rows), "nt") * scale + bias_cur, NEG_BIG)
            s_prev = jnp.where(valid_prev, _bdot(q, _take(kps, hp, rows), "nt") * scale + bias_far, NEG_BIG)
            m = jnp.maximum(jnp.max(s_cur, axis=-1, keepdims=True), jnp.max(s_prev, axis=-1, keepdims=True))
            p_cur = jnp.exp(s_cur - m)
            p_prev = jnp.exp(s_prev - m)
            l = jnp.sum(p_cur, axis=-1, keepdims=True) + jnp.sum(p_prev, axis=-1, keepdims=True)
            o = (_bdot(p_cur, _take(vcs, hp, rows), "nn") + _bdot(p_prev, _take(vps, hp, rows), "nn")) / l
            _put(os, hp, rows, o)
            _put(ls, hp, rows, jnp.broadcast_to(m + jnp.log(l), o.shape))
            return carry

        lax.fori_loop(0, d // rb, sub_sequences, 0)
        for hi in range(hp):
            hs = slice(hi * HEAD_DIM, (hi + 1) * HEAD_DIM)
            o_ref[:, hs] = os[hi]
            lse_ref[:, hs] = ls[hi]

    per_sec = ATTN_WIDTH // W
    per_grp = ATTN_OUT_WIDTH // W

    def spec(sec, shift):
        return pl.BlockSpec((C, W), lambda b, h: (jnp.maximum(b + shift, 0), sec * per_sec + gi * per_grp + h))

    out = pl.BlockSpec((C, W), lambda b, h: (b, h))
    scr = pltpu.VMEM((hp, C, HEAD_DIM), F32)
    return pl.pallas_call(
        body, grid=(nb, HEADS_PER_GROUP // hp),
        in_specs=[spec(0, 0), spec(1, 0), spec(1, -1), spec(2, 0), spec(2, -1)], out_specs=[out, out],
        out_shape=[jax.ShapeDtypeStruct((S, ATTN_OUT_WIDTH), F32)] * 2, scratch_shapes=[scr] * 7,
        compiler_params=_params("parallel", "parallel"), name=f"attn_fwd_g{gi}")(qkv, qkv, qkv, qkv, qkv)


def _attn_bwd(gi, qkv, do, lse, dp):
    S = qkv.shape[0]
    d, C, nb, hp, W, rb = _attn_tiling(gi, S)
    scale = HEAD_DIM ** -0.5
    n_in = 12

    def body(*refs):
        ins, (dq_ref, dk_ref, dv_ref) = refs[:n_in], refs[n_in:n_in + 3]
        scr = refs[n_in + 3:]
        for src, dst in zip(ins, scr[:n_in]):
            _split_heads(src, dst, hp)
        qs, kcs, vcs, kps, vps, qns, dos, dons, lses, lsens, dps, dpns = scr[:n_in]
        dqs, dks, dvs = scr[n_in:]
        b = pl.program_id(0)
        dist_cur, valid_cur, valid_prev = _attn_masks(b)
        row = lax.broadcasted_iota(jnp.int32, (SPAN, SPAN), 0)
        col = lax.broadcasted_iota(jnp.int32, (SPAN, SPAN), 1)
        valid_next = jnp.logical_and(col >= row, b < nb - 1)

        bias_cur = _batch_bias(gi, d, hp, rb, dist_cur)
        bias_far = _batch_bias(gi, d, hp, rb, dist_cur + SPAN)

        def sub_sequences(it, carry):
            rows = _batch_rows(it, d, rb)
            q, kc, vc, kp, vp, qn, dov, don = (_take(s, hp, rows) for s in (qs, kcs, vcs, kps, vps, qns, dos, dons))
            lse_b, lse_n, dp_b, dp_n = (_take(s, hp, rows) for s in (lses, lsens, dps, dpns))
            p_cur = jnp.where(valid_cur, jnp.exp(_bdot(q, kc, "nt") * scale + bias_cur - lse_b), 0.0)
            p_prev = jnp.where(valid_prev, jnp.exp(_bdot(q, kp, "nt") * scale + bias_far - lse_b), 0.0)
            ds_cur = p_cur * (_bdot(dov, vc, "nt") - dp_b)
            ds_prev = p_prev * (_bdot(dov, vp, "nt") - dp_b)
            _put(dqs, hp, rows, (_bdot(ds_cur, kc, "nn") + _bdot(ds_prev, kp, "nn")) * scale)
            p_next = jnp.where(valid_next, jnp.exp(_bdot(qn, kc, "nt") * scale + bias_far - lse_n), 0.0)
            ds_next = p_next * (_bdot(don, vc, "nt") - dp_n)
            _put(dvs, hp, rows, _bdot(p_cur, dov, "tn") + _bdot(p_next, don, "tn"))
            _put(dks, hp, rows, (_bdot(ds_cur, q, "tn") + _bdot(ds_next, qn, "tn")) * scale)
            return carry

        lax.fori_loop(0, d // rb, sub_sequences, 0)
        for hi in range(hp):
            hs = slice(hi * HEAD_DIM, (hi + 1) * HEAD_DIM)
            dq_ref[:, hs] = dqs[hi].astype(BF16)
            dk_ref[:, hs] = dks[hi].astype(BF16)
            dv_ref[:, hs] = dvs[hi].astype(BF16)

    per_sec = ATTN_WIDTH // W
    per_grp = ATTN_OUT_WIDTH // W

    def qkv_spec(sec, shift):
        return pl.BlockSpec((C, W), lambda b, h: (jnp.clip(b + shift, 0, nb - 1), sec * per_sec + gi * per_grp + h))

    def act_spec(shift):
        return pl.BlockSpec((C, W), lambda b, h: (jnp.clip(b + shift, 0, nb - 1), h))

    out = act_spec(0)
    scr = pltpu.VMEM((hp, C, HEAD_DIM), F32)
    return pl.pallas_call(
        body, grid=(nb, HEADS_PER_GROUP // hp),
        in_specs=[qkv_spec(0, 0), qkv_spec(1, 0), qkv_spec(2, 0), qkv_spec(1, -1), qkv_spec(2, -1), qkv_spec(0, 1),
                  act_spec(0), act_spec(1), act_spec(0), act_spec(1), act_spec(0), act_spec(1)],
        out_specs=[out, out, out], out_shape=[jax.ShapeDtypeStruct((S, ATTN_OUT_WIDTH), BF16)] * 3,
        scratch_shapes=[scr] * (n_in + 3),
        compiler_params=_params("parallel", "parallel"), name=f"attn_bwd_g{gi}")(
            qkv, qkv, qkv, qkv, qkv, qkv, do, do, lse, lse, dp, dp)


def _group_weights(l_refs):
    l0, l1, l2 = (r[...] for r in l_refs)
    m = jnp.maximum(jnp.maximum(l0, l1), l2)
    e = [jnp.exp(l0 - m), jnp.exp(l1 - m), jnp.exp(l2 - m)]
    tot = e[0] + e[1] + e[2]
    return [ei / tot for ei in e]


def _merge_fwd(outs, lses):
    S, W = outs[0].shape
    T = 512

    def body(o0, o1, o2, l0, l1, l2, y_ref):
        w = _group_weights((l0, l1, l2))
        y_ref[...] = (w[0] * o0[...] + w[1] * o1[...] + w[2] * o2[...]).astype(BF16)

    blk = pl.BlockSpec((T, W), lambda i: (i, 0))
    return pl.pallas_call(
        body, grid=(S // T,), in_specs=[blk] * 6, out_specs=blk, out_shape=jax.ShapeDtypeStruct((S, W), BF16),
        compiler_params=_params("parallel"), name="merge_fwd")(*outs, *lses)


def _merge_bwd(dy_out, w_attn_out_t, outs, lses):
    S, W = outs[0].shape
    T = 512

    def body(dyo_ref, wao_ref, o0, o1, o2, l0, l1, l2, do0, do1, do2, dp0, dp1, dp2):
        w = _group_weights((l0, l1, l2))
        dyv = _dot(dyo_ref[...], wao_ref[...], "nn")
        y = w[0] * o0[...] + w[1] * o1[...] + w[2] * o2[...]
        prod = dyv * y
        dots = []
        for hi in range(HEADS_PER_GROUP):
            hs = slice(hi * HEAD_DIM, (hi + 1) * HEAD_DIM)
            dots.append(jnp.broadcast_to(jnp.sum(prod[:, hs], axis=-1, keepdims=True), (T, HEAD_DIM)))
        dot_b = jnp.concatenate(dots, axis=1)
        for wg, do_ref, dp_ref in zip(w, (do0, do1, do2), (dp0, dp1, dp2)):
            do_ref[...] = (wg * dyv).astype(BF16)
            dp_ref[...] = wg * dot_b

    blk = pl.BlockSpec((T, W), lambda i: (i, 0))
    res = pl.pallas_call(
        body, grid=(S // T,),
        in_specs=[pl.BlockSpec((T, D_MODEL), lambda i: (i, 0)), pl.BlockSpec((D_MODEL, W), lambda i: (0, 0))] + [blk] * 6,
        out_specs=[blk] * 6,
        out_shape=[jax.ShapeDtypeStruct((S, W), BF16)] * 3 + [jax.ShapeDtypeStruct((S, W), F32)] * 3,
        compiler_params=_params("parallel"), name="merge_bwd")(dy_out, w_attn_out_t, *outs, *lses)
    return res[:3], res[3:]


def _mix_fwd(pool_y, attn_y, w_pool_out_t, w_attn_out_t, gates_pre, b_gate):
    S = pool_y.shape[0]
    D = D_MODEL
    tm, tn = 1024, 512
    nj = D // tn

    def body(p_ref, a_ref, wp_ref, wa_ref, g0_ref, g1_ref, b0_ref, b1_ref, mixed_ref, yp_ref, ya_ref):
        yp = _dot(p_ref[...], wp_ref[...], "nt")
        ya = _dot(a_ref[...], wa_ref[...], "nt")
        g0 = jax.nn.sigmoid(g0_ref[...] + b0_ref[...])
        g1 = jax.nn.sigmoid(g1_ref[...] + b1_ref[...])
        mixed_ref[...] = (g0 * yp + g1 * ya).astype(BF16)
        yp_ref[...] = yp.astype(BF16)
        ya_ref[...] = ya.astype(BF16)

    out = pl.BlockSpec((tm, tn), lambda i, j: (i, j))
    return pl.pallas_call(
        body, grid=(S // tm, nj),
        in_specs=[pl.BlockSpec((tm, POOL_WIDTH), lambda i, j: (i, 0)),
                  pl.BlockSpec((tm, ATTN_OUT_WIDTH), lambda i, j: (i, 0)),
                  pl.BlockSpec((tn, POOL_WIDTH), lambda i, j: (j, 0)),
                  pl.BlockSpec((tn, ATTN_OUT_WIDTH), lambda i, j: (j, 0)),
                  pl.BlockSpec((tm, tn), lambda i, j: (i, j)), pl.BlockSpec((tm, tn), lambda i, j: (i, j + nj)),
                  pl.BlockSpec((1, tn), lambda i, j: (0, j)), pl.BlockSpec((1, tn), lambda i, j: (0, j + nj))],
        out_specs=[out, out, out], out_shape=[jax.ShapeDtypeStruct((S, D), BF16)] * 3,
        compiler_params=_params("parallel", "parallel"), name="mix_fwd")(
            pool_y, attn_y, w_pool_out_t, w_attn_out_t, gates_pre, gates_pre, b_gate, b_gate)


def _mix_bwd(dx, w_out, gates_pre, b_gate, y_pool, y_attn):
    S = dx.shape[0]
    D = D_MODEL
    T = 256

    def body(dx_ref, w_ref, gp_ref, b_ref, yp_ref, ya_ref, dyp_ref, dya_ref, dg_ref, db_ref):
        i = pl.program_id(0)
        dm = _dot(dx_ref[...], w_ref[...], "nt")
        parts = []
        for br, (y_ref, dy_ref) in enumerate(((yp_ref, dyp_ref), (ya_ref, dya_ref))):
            cs = slice(br * D, (br + 1) * D)
            g = jax.nn.sigmoid(gp_ref[:, cs] + b_ref[:, cs])
            dy_ref[...] = (dm * g).astype(BF16)
            dpre = dm * y_ref[...].astype(F32) * g * (1.0 - g)
            dg_ref[:, cs] = dpre.astype(BF16)
            parts.append(jnp.sum(dpre, axis=0, keepdims=True))

        @pl.when(i == 0)
        def _():
            db_ref[:, 0:D] = parts[0]
            db_ref[:, D:2 * D] = parts[1]

        @pl.when(i > 0)
        def _():
            db_ref[:, 0:D] += parts[0]
            db_ref[:, D:2 * D] += parts[1]

    row = pl.BlockSpec((T, D), lambda i: (i, 0))
    row2 = pl.BlockSpec((T, 2 * D), lambda i: (i, 0))
    vec2 = pl.BlockSpec((1, 2 * D), lambda i: (0, 0))
    return pl.pallas_call(
        body, grid=(S // T,), in_specs=[row, pl.BlockSpec((D, D), lambda i: (0, 0)), row2, vec2, row, row],
        out_specs=[row, row, row2, vec2],
        out_shape=[jax.ShapeDtypeStruct((S, D), BF16), jax.ShapeDtypeStruct((S, D), BF16),
                   jax.ShapeDtypeStruct((S, 2 * D), BF16), jax.ShapeDtypeStruct((1, 2 * D), F32)],
        compiler_params=_params("arbitrary"), name="mix_bwd")(dx, w_out, gates_pre, b_gate, y_pool, y_attn)


_CONV_T, _CONV_C = 512, 512


def _conv_taps(ext_ref, h, w_ref, b_ref, cols=slice(None)):
    ext = ext_ref[h, :, cols]
    x2 = pltpu.roll(ext, 2, 0)[HALO:]
    x1 = pltpu.roll(ext, 1, 0)[HALO:]
    x0 = ext[HALO:]
    y = b_ref[:, cols] + w_ref[0:1, cols] * x2
    y = y + w_ref[1:2, cols] * x1
    y = y + w_ref[2:3, cols] * x0
    return y, (x2, x1, x0)


def _gelu_parts(a):
    cdf = 0.5 * (1.0 + lax.erf(a * (1.0 / math.sqrt(2.0))))
    return cdf, a * cdf


def _conv_specs(S):
    T, C = _CONV_T, _CONV_C
    nj = D_FF // C
    blk = pl.BlockSpec((2, T, C), lambda i, j: (0, i, j))
    per = T // HALO
    halo = pl.BlockSpec((2, HALO, C), lambda i, j: (0, jnp.maximum(i * per - 1, 0), j))
    w_a = pl.BlockSpec((3, C), lambda i, j: (0, j))
    w_b = pl.BlockSpec((3, C), lambda i, j: (0, j + nj))
    b_a = pl.BlockSpec((1, C), lambda i, j: (0, j))
    b_b = pl.BlockSpec((1, C), lambda i, j: (0, j + nj))
    return T, C, nj, blk, halo, w_a, w_b, b_a, b_b


def _conv_load(ext_ref, halo_ref, up_ref, first):
    for h in range(2):
        ext_ref[h, 0:HALO, :] = jnp.where(first, 0.0, halo_ref[h].astype(F32))
        ext_ref[h, HALO:, :] = up_ref[h].astype(F32)


def _up_proj_convffn(h, w_up_t, conv_w, conv_b):
    S = h.shape[0]
    tm, C = 1024, _CONV_C
    nj = D_FF // C

    def body(h_ref, wa_ref, wb_ref, cwa_ref, cwb_ref, cba_ref, cbb_ref, up_ref, act_ref, ext_ref, carry_ref):
        i, j = pl.program_id(0), pl.program_id(1)
        for half, w_ref in enumerate((wa_ref, wb_ref)):
            pre = _dot(h_ref[...], w_ref[...], "nt").astype(BF16)
            up_ref[half] = pre
            ext_ref[half, 0:HALO, :] = jnp.where(i == 0, 0.0, carry_ref[j, half])
            ext_ref[half, HALO:, :] = pre.astype(F32)
            carry_ref[j, half] = ext_ref[half, tm:tm + HALO, :]
        a, _ = _conv_taps(ext_ref, 0, cwa_ref, cba_ref)
        b, _ = _conv_taps(ext_ref, 1, cwb_ref, cbb_ref)
        _, gelu = _gelu_parts(a)
        act_ref[...] = (gelu * b).astype(BF16)

    def w_spec(half):
        return pl.BlockSpec((C, D_MODEL), lambda i, j: (j + half * nj, 0))

    def c_spec(rows, half):
        return pl.BlockSpec((rows, C), lambda i, j: (0, j + half * nj))

    return pl.pallas_call(
        body, grid=(S // tm, nj),
        in_specs=[pl.BlockSpec((tm, D_MODEL), lambda i, j: (i, 0)), w_spec(0), w_spec(1),
                  c_spec(3, 0), c_spec(3, 1), c_spec(1, 0), c_spec(1, 1)],
        out_specs=[pl.BlockSpec((2, tm, C), lambda i, j: (0, i, j)), pl.BlockSpec((tm, C), lambda i, j: (i, j))],
        out_shape=[jax.ShapeDtypeStruct((2, S, D_FF), BF16), jax.ShapeDtypeStruct((S, D_FF), BF16)],
        scratch_shapes=[pltpu.VMEM((2, HALO + tm, C), F32), pltpu.VMEM((nj, 2, HALO, C), F32)],
        compiler_params=_params("arbitrary", "arbitrary"), name="up_proj_convffn")(
            h, w_up_t, w_up_t, conv_w, conv_w, conv_b, conv_b)


def _convffn_bwd_a(up_pre, dx, w_down, conv_w, conv_b):
    S = up_pre.shape[1]
    T, C, nj, blk, halo, w_a, w_b, b_a, b_b = _conv_specs(S)
    def swap(spec):
        return pl.BlockSpec(spec.block_shape, lambda j, i, _m=spec.index_map: _m(i, j))

    def body(up_ref, halo_ref, dx_ref, wd_ref, wa_ref, wb_ref, ba_ref, bb_ref, dup_ref, dwa_ref, dwb_ref, dba_ref,
             dbb_ref, ext_ref):
        i = pl.program_id(1)
        _conv_load(ext_ref, halo_ref, up_ref, i == 0)

        @pl.when(i == 0)
        def _():
            for acc_ref in (dwa_ref, dwb_ref, dba_ref, dbb_ref):
                acc_ref[...] = jnp.zeros_like(acc_ref)

        for c in range(C // MXU_DIM):
            cs = slice(c * MXU_DIM, (c + 1) * MXU_DIM)
            dact = _dot(dx_ref[...], wd_ref[cs, :], "nt")
            a, xa = _conv_taps(ext_ref, 0, wa_ref, ba_ref, cs)
            b, xb = _conv_taps(ext_ref, 1, wb_ref, bb_ref, cs)
            cdf, gelu = _gelu_parts(a)
            dgelu = cdf + a * (jnp.exp(-0.5 * a * a) * (1.0 / math.sqrt(2.0 * math.pi)))
            da = dact * b * dgelu
            db = dact * gelu
            dup_ref[0, :, cs] = da.astype(BF16)
            dup_ref[1, :, cs] = db.astype(BF16)
            for dval, xs, dw_ref, dbias_ref in ((da, xa, dwa_ref, dba_ref), (db, xb, dwb_ref, dbb_ref)):
                dw_ref[:, cs] += jnp.concatenate([jnp.sum(dval * xk, axis=0, keepdims=True) for xk in xs], axis=0)
                dbias_ref[:, cs] += jnp.sum(dval, axis=0, keepdims=True)

    dx_spec = pl.BlockSpec((T, D_MODEL), lambda j, i: (i, 0))
    wd_spec = pl.BlockSpec((C, D_MODEL), lambda j, i: (j, 0))
    dw_spec = pl.BlockSpec((3, C), lambda j, i: (0, j))
    db_spec = pl.BlockSpec((1, C), lambda j, i: (0, j))
    res = pl.pallas_call(
        body, grid=(nj, S // T),
        in_specs=[swap(blk), swap(halo), dx_spec, wd_spec, swap(w_a), swap(w_b), swap(b_a), swap(b_b)],
        out_specs=[swap(blk), dw_spec, dw_spec, db_spec, db_spec],
        out_shape=[jax.ShapeDtypeStruct((2, S, D_FF), BF16), jax.ShapeDtypeStruct((3, D_FF), F32),
                   jax.ShapeDtypeStruct((3, D_FF), F32), jax.ShapeDtypeStruct((1, D_FF), F32),
                   jax.ShapeDtypeStruct((1, D_FF), F32)],
        scratch_shapes=[pltpu.VMEM((2, HALO + T, C), F32)],
        compiler_params=_params("parallel", "arbitrary"), name="convffn_bwd_a")(
            up_pre, up_pre, dx, w_down, conv_w, conv_w, conv_b, conv_b)
    dup, dwa, dwb, dba, dbb = res
    dconv_w = jnp.concatenate([dwa, dwb], axis=1)
    dconv_b = jnp.concatenate([dba, dbb], axis=1)
    return dup, dconv_w, dconv_b


def _convffn_bwd_b(dup, conv_w, w_up_t):
    S = dup.shape[1]
    tm, tk = 1024, D_FF // 4
    per_half = D_FF // tk
    nk = 2 * per_half
    per = tm // HALO
    n_blk = S // tm

    def body(d_ref, halo_ref, cw_ref, w_ref, dpre_ref, dh_ref, ext_ref, acc_ref):
        i, k = pl.program_id(0), pl.program_id(1)
        n = tm + HALO
        ext_ref[0:tm, :] = d_ref[...].astype(F32)
        ext_ref[tm:, :] = jnp.where(i == n_blk - 1, 0.0, halo_ref[...].astype(F32))
        ext = ext_ref[...]
        y = cw_ref[2:3, :] * ext[0:tm] + cw_ref[1:2, :] * pltpu.roll(ext, n - 1, 0)[0:tm]
        y = y + cw_ref[0:1, :] * pltpu.roll(ext, n - 2, 0)[0:tm]
        pre = y.astype(BF16)
        dpre_ref[...] = pre
        prod = _dot(pre, w_ref[...], "nn")

        @pl.when(k == 0)
        def _():
            acc_ref[...] = prod

        @pl.when(k > 0)
        def _():
            acc_ref[...] += prod

        @pl.when(k == nk - 1)
        def _():
            dh_ref[...] = acc_ref[...].astype(BF16)

    blk = pl.BlockSpec((None, tm, tk), lambda i, k: (k // per_half, i, k % per_half))
    halo = pl.BlockSpec((None, HALO, tk),
                        lambda i, k: (k // per_half, jnp.minimum((i + 1) * per, S // HALO - 1), k % per_half))
    return pl.pallas_call(
        body, grid=(n_blk, nk),
        in_specs=[blk, halo, pl.BlockSpec((3, tk), lambda i, k: (0, k)), pl.BlockSpec((tk, D_MODEL), lambda i, k: (k, 0))],
        out_specs=[blk, pl.BlockSpec((tm, D_MODEL), lambda i, k: (i, 0))],
        out_shape=[jax.ShapeDtypeStruct((2, S, D_FF), BF16), jax.ShapeDtypeStruct((S, D_MODEL), BF16)],
        scratch_shapes=[pltpu.VMEM((tm + HALO, tk), F32), pltpu.VMEM((tm, D_MODEL), F32)],
        compiler_params=_params("parallel", "arbitrary"), name="convffn_bwd_b")(dup, dup, conv_w, w_up_t)


def _position():
    return lax.axis_index("x"), lax.axis_index("y"), lax.axis_index("c")


def _flip(pos, k):
    x, y, c = pos
    return (1 - x if k & 4 else x, 1 - y if k & 2 else y, 1 - c if k & 1 else c)


def _index(pos):
    return 4 * pos[0] + 2 * pos[1] + pos[2]


_HBM = pl.BlockSpec(memory_space=pltpu.HBM)
_SEM = pl.BlockSpec(memory_space=pltpu.SEMAPHORE)
_ANY = pl.BlockSpec(memory_space=pl.ANY)
_EFFECT = pltpu.SideEffectType.DATAFLOW_SIDE_EFFECTING


def _in_hbm(a):
    return pltpu.with_memory_space_constraint(a, pltpu.HBM)


def _split_copies(name, srcs, lands, n_copies, plan, start_after, wait_after):
    ns, nl = len(srcs), len(lands)
    start_after = tuple(start_after)
    na = len(start_after)

    def start_body(*refs):
        src_refs, land_refs = refs[:ns], refs[ns:ns + nl]
        send_sems, recv_sems = refs[ns + nl + na], refs[ns + nl + na + 1]
        token = refs[-1]
        for k, (src, dst, _, peer) in enumerate(plan(src_refs, land_refs)):
            pltpu.make_async_remote_copy(src_ref=src, dst_ref=dst, send_sem=send_sems.at[k], recv_sem=recv_sems.at[k],
                                         device_id=peer, device_id_type=MESH).start()
        token[...] = jnp.zeros_like(token)

    thru = [pltpu.HBM(a.shape, a.dtype) for a in list(srcs) + list(lands)]
    res = pl.pallas_call(
        start_body, name=name + "_start",
        out_shape=[pltpu.SemaphoreType.DMA((n_copies,)), pltpu.SemaphoreType.DMA((n_copies,))] + thru
        + [jax.ShapeDtypeStruct((8, 128), F32)],
        in_specs=[_HBM] * (ns + nl) + [_ANY] * na,
        out_specs=[_SEM, _SEM] + [_HBM] * (ns + nl) + [pl.BlockSpec(memory_space=pltpu.VMEM)],
        input_output_aliases={i: 2 + i for i in range(ns + nl)},
        compiler_params=pltpu.CompilerParams(has_side_effects=_EFFECT),
    )(*[_in_hbm(a) for a in list(srcs) + list(lands)], *start_after)
    send_sems, recv_sems, token = res[0], res[1], res[-1]
    thru_vals = res[2:-1]

    def wait():
        def wait_body(*refs):
            src_refs, land_refs = refs[:ns], refs[ns:ns + nl]
            s_sems, r_sems = refs[ns + nl], refs[ns + nl + 1]
            for k, (src, _, dst, peer) in enumerate(plan(src_refs, land_refs)):
                cp = pltpu.make_async_remote_copy(src_ref=src, dst_ref=dst, send_sem=s_sems.at[k], recv_sem=r_sems.at[k],
                                                  device_id=peer, device_id_type=MESH)
                cp.wait_send()
                cp.wait_recv()

        out = pl.pallas_call(
            wait_body, name=name + "_wait", out_shape=thru,
            in_specs=[_HBM] * (ns + nl) + [_SEM, _SEM, _ANY], out_specs=[_HBM] * (ns + nl),
            input_output_aliases={i: i for i in range(ns + nl)},
            compiler_params=pltpu.CompilerParams(has_side_effects=_EFFECT),
        )(*thru_vals, send_sems, recv_sems, wait_after())
        return list(out[:ns]), list(out[ns:])

    return wait, token


def _gather_level1_plan(n):
    def plan(src_refs, land_refs):
        me = _position()
        x, y, c = me
        peers = [(x, y, 1 - c), (1 - x, y, c), (x, 1 - y, c), (1 - x, 1 - y, c)]
        return [(src_refs[a], land_refs[a].at[_index(me)], land_refs[a].at[_index(p)], p)
                for a in range(n) for p in peers]
    return plan


def _gather_level2(lands):
    n = len(lands)

    def body(*refs):
        lin = refs[:n]
        send_sems, recv_sems = refs[2 * n:]
        x, y, c = _position()
        sibling = (x, y, 1 - c)
        chips = [(1 - x, y), (x, 1 - y), (1 - x, 1 - y)]

        def copy(a, j, core):
            slot = lin[a].at[_index((*chips[j], core))]
            return pltpu.make_async_remote_copy(
                src_ref=slot, dst_ref=slot, send_sem=send_sems.at[3 * a + j], recv_sem=recv_sems.at[3 * a + j],
                device_id=sibling, device_id_type=MESH)

        sent = [copy(a, j, c) for a in range(n) for j in range(3)]
        for cp in sent:
            cp.start()
        for a in range(n):
            for j in range(3):
                copy(a, j, 1 - c).wait_recv()
        for cp in sent:
            cp.wait_send()

    return pl.pallas_call(
        body, out_shape=[jax.ShapeDtypeStruct(l.shape, l.dtype) for l in lands],
        in_specs=[_ANY] * n, out_specs=[_ANY] * n, input_output_aliases={a: a for a in range(n)},
        scratch_shapes=[pltpu.SemaphoreType.DMA((3 * n,)), pltpu.SemaphoreType.DMA((3 * n,))],
        name="gather_level2")(*lands)


def _gather_direct_plan(n):
    def plan(src_refs, land_refs):
        me = _position()
        peers = [_flip(me, k) for k in range(1, N_DEV)]
        return [(src_refs[a], land_refs[a].at[_index(me)], land_refs[a].at[_index(p)], p)
                for a in range(n) for p in peers]
    return plan


def _place_own(shards):
    me = _index(_position())
    return [lax.dynamic_update_slice(lax.empty((N_DEV,) + s.shape, s.dtype), s[None], (me, 0, 0)) for s in shards]


def _exchange_direct_plan(n):
    def plan(src_refs, land_refs):
        me = _position()
        peers = [_flip(me, k) for k in range(1, N_DEV)]
        return [(src_refs[a].at[_index(p)], land_refs[a].at[_index(me)], land_refs[a].at[_index(p)], p)
                for a in range(n) for p in peers]
    return plan


def _adamw(w, g, m, v):
    m = ADAM_B1 * m + (1.0 - ADAM_B1) * g
    v = ADAM_B2 * v + (1.0 - ADAM_B2) * (g * g)
    m_hat = m / (1.0 - ADAM_B1 ** ADAM_STEP)
    v_hat = v / (1.0 - ADAM_B2 ** ADAM_STEP)
    delta = -ADAM_LR * (m_hat / (jnp.sqrt(v_hat) + ADAM_EPS) + ADAM_WD * w)
    return delta, m, v


def _row_tile(rows, unit, cap=256):
    best = rows
    for t in range(unit, min(rows, cap) + 1, unit):
        if rows % t == 0:
            best = t
    return best


def _sum_partials(name, recv):
    n_src, R, C = recv.shape
    T = _row_tile(R, 16)

    def body(r_ref, g_out):
        g = r_ref[0].astype(F32)
        for s in range(1, n_src):
            g = g + r_ref[s].astype(F32)
        g_out[...] = g

    return pl.pallas_call(
        body, grid=(R // T,), in_specs=[pl.BlockSpec((n_src, T, C), lambda i: (0, i, 0))],
        out_specs=pl.BlockSpec((T, C), lambda i: (i, 0)), out_shape=jax.ShapeDtypeStruct((R, C), F32),
        compiler_params=_params("parallel"), name=name)(recv)


def _sum_adamw(name, recv, w, m, v):
    n_src, R, C = recv.shape
    T = _row_tile(R, 16)

    def body(r_ref, w_ref, m_ref, v_ref, g_out, d_out, m_out, v_out):
        g = r_ref[0].astype(F32)
        for s in range(1, n_src):
            g = g + r_ref[s].astype(F32)
        delta, mn, vn = _adamw(w_ref[...], g, m_ref[...], v_ref[...])
        g_out[...] = g
        d_out[...] = delta
        m_out[...] = mn
        v_out[...] = vn

    blk = pl.BlockSpec((T, C), lambda i: (i, 0))
    return pl.pallas_call(
        body, grid=(R // T,), in_specs=[pl.BlockSpec((n_src, T, C), lambda i: (0, i, 0)), blk, blk, blk],
        out_specs=[blk] * 4, out_shape=[jax.ShapeDtypeStruct((R, C), F32)] * 4,
        compiler_params=_params("parallel"), name=name)(recv, w, m, v)


def _adamw_update(name, w, g, m, v):
    R, C = w.shape
    T = _row_tile(R, 8)

    def body(w_ref, g_ref, m_ref, v_ref, d_out, m_out, v_out):
        delta, mn, vn = _adamw(w_ref[...], g_ref[...], m_ref[...], v_ref[...])
        d_out[...] = delta
        m_out[...] = mn
        v_out[...] = vn

    blk = pl.BlockSpec((T, C), lambda i: (i, 0))
    return pl.pallas_call(
        body, grid=(R // T,), in_specs=[blk] * 4, out_specs=[blk] * 3,
        out_shape=[jax.ShapeDtypeStruct((R, C), F32)] * 3, compiler_params=_params("parallel"), name=name)(w, g, m, v)


def _gather_vmem(p_ref, gath, send_sems, recv_sems):
    me = _position()
    gath[_index(me)] = p_ref[...]
    copies = []
    for k in range(1, N_DEV):
        peer = _flip(me, k)
        copies.append(pltpu.make_async_remote_copy(
            src_ref=p_ref, dst_ref=gath.at[_index(me)], send_sem=send_sems.at[k - 1],
            recv_sem=recv_sems.at[k - 1], device_id=peer, device_id_type=MESH))
    for cp in copies:
        cp.start()
    for k in range(1, N_DEV):
        peer = _flip(me, k)
        pltpu.make_async_remote_copy(
            src_ref=p_ref, dst_ref=gath.at[_index(peer)], send_sem=send_sems.at[k - 1],
            recv_sem=recv_sems.at[k - 1], device_id=peer, device_id_type=MESH).wait_recv()
    for cp in copies:
        cp.wait_send()


def _small_allreduce_adamw(part, w, m, v):
    R, C = part.shape

    def body(p_ref, w_ref, m_ref, v_ref, g_out, d_out, m_out, v_out, gath, send_sems, recv_sems):
        _gather_vmem(p_ref, gath, send_sems, recv_sems)
        g = gath[0]
        for s in range(1, N_DEV):
            g = g + gath[s]
        delta, mn, vn = _adamw(w_ref[...], g, m_ref[...], v_ref[...])
        g_out[...] = g
        d_out[...] = delta
        m_out[...] = mn
        v_out[...] = vn

    vm = pl.BlockSpec(memory_space=pltpu.VMEM)
    return pl.pallas_call(
        body, in_specs=[vm] * 4, out_specs=[vm] * 4, out_shape=[jax.ShapeDtypeStruct((R, C), F32)] * 4,
        scratch_shapes=[pltpu.VMEM((N_DEV, R, C), F32), pltpu.SemaphoreType.DMA((7,)), pltpu.SemaphoreType.DMA((7,))],
        name="small_allreduce_adamw")(part, w, m, v)


_TRANSPOSED = ("w_in", "w_up", "w_pool_out", "w_attn_out")
_SMALL = (("g_mix", 2048), ("b_gate", 4096), ("pool_scale", 1024), ("g_ffn", 2048), ("conv_b", 11264), ("g_final", 2048))
CONV_ROWS = 8


def _to_comm(name, shard):
    return shard.T if name in _TRANSPOSED else shard


def _pack_small(vals):
    flat = [vals[n].reshape(-1) for n, _ in _SMALL]
    used = sum(f.shape[0] for f in flat)
    flat.append(jnp.zeros((SMALL_ROWS * PACK_COLS - used,), F32))
    return jnp.concatenate(flat).reshape(SMALL_ROWS, PACK_COLS)


def _unpack_small(packed, like):
    flat = packed.reshape(-1)
    out, off = {}, 0
    for name, n in _SMALL:
        out[name] = flat[off:off + n].reshape(like[name].shape)
        off += n
    return out


def _local_step(x, target, W, small, first_token=0.0, first_weights=None, late_weights=None, emit=None):
    S = x.shape[0]
    o_qkv, o_gate = POOL_WIDTH, POOL_WIDTH + QKV_WIDTH
    if emit is None:
        emit = lambda grads: None

    def zero(token):
        return 0.0 if token is None else token[0, 0]

    h1 = _rms_fwd("rms1_fwd", x, small["g_mix"] + first_token)
    tok = None
    if first_weights is not None:
        w_first, tok = first_weights(h1)
        small = dict(small, conv_w_full=w_first.pop("conv_w_full"))
        W = dict(W, **w_first)
    w_in_t = W["w_in"]
    u = _mm_nt("proj_u", h1, w_in_t, tm=2048, tn=512, tk=2048, o_dtype=F32, n_rows=POOL_WIDTH, b_row_off=0,
               after=tok)
    qkv = _mm_nt("proj_qkv", h1, w_in_t, tm=2048, tn=512, tk=2048, o_dtype=BF16, n_rows=QKV_WIDTH, b_row_off=o_qkv,
                 after=tok)
    gates_pre = _mm_nt("proj_gates", h1, w_in_t, tm=2048, tn=512, tk=2048, o_dtype=BF16, n_rows=2 * D_MODEL,
                       b_row_off=o_gate, after=tok)
    pool_y = _pool_fwd(u, W["w_pool_lin"], small["pool_scale"])
    outs, lses = zip(*[_attn_fwd(gi, qkv) for gi in range(3)])
    attn_y = _merge_fwd(outs, lses)
    if late_weights is not None:
        W = dict(W, **late_weights(0, attn_y))
    w_up_t = W["w_up"]
    mixed, y_pool, y_attn = _mix_fwd(pool_y, attn_y, W["w_pool_out"], W["w_attn_out"], gates_pre, small["b_gate"])
    x1, h2 = _out_proj_norm(mixed, W["w_out"], x, small["g_ffn"])

    nj = D_FF // 512
    up_pre, act = _up_proj_convffn(h2, w_up_t, small["conv_w_full"], small["conv_b"])
    if late_weights is not None:
        W = dict(W, **late_weights(1, act))
    x2 = _mm_nn("down_proj", act, W["w_down"], tm=1024, tn=1024, tk=D_FF // 2, o_dtype=F32, res=x1)

    dx2, dx2_b, dg_final, sq_cols = _final_loss("final_loss", x2, small["g_final"], target)

    dw_down = _mm_tn("dw_down", act, dx2_b, tm=512, tn=2048, tk=TOKEN_TK)
    tok = emit(dict(w_down=dw_down))
    dup, dconv_w, dconv_b = _convffn_bwd_a(up_pre, dx2_b, W["w_down"], small["conv_w_full"],
                                           small["conv_b"] + zero(tok))
    dup_pre, dh2 = _convffn_bwd_b(dup, small["conv_w_full"], w_up_t)
    dw_up = _mm("dw_up", dup_pre, h2, mode="tn", grid=(2 * nj, 1, S // TOKEN_TK),
                a_spec=pl.BlockSpec((None, TOKEN_TK, 512), lambda i, j, k: (i // nj, k, i % nj)),
                b_spec=pl.BlockSpec((TOKEN_TK, D_MODEL), lambda i, j, k: (k, 0)),
                o_spec=pl.BlockSpec((512, D_MODEL), lambda i, j, k: (i, 0)),
                o_shape=(2 * D_FF, D_MODEL), o_dtype=BF16, acc_shape=(512, D_MODEL))
    tok = emit(dict(w_up=dw_up, conv_w=dconv_w))
    dx1, dx1_b, dg_ffn = _rms_bwd("rms2_bwd", x1, small["g_ffn"] + zero(tok), dh2, dx2, with_bf16=True)

    dw_out = _mm_tn("dw_out", mixed, dx1_b, tm=512, tn=2048, tk=TOKEN_TK)
    tok = emit(dict(w_out=dw_out))
    dy_pool, dy_attn, dgates, db_gate = _mix_bwd(dx1_b, W["w_out"], gates_pre, small["b_gate"] + zero(tok),
                                                 y_pool, y_attn)
    dw_pool_out = _mm_tn("dw_pool_out", dy_pool, pool_y, tm=1024, tn=1024, tk=TOKEN_TK)
    dw_attn_out = _mm_tn("dw_attn_out", dy_attn, attn_y, tm=1024, tn=512, tk=TOKEN_TK)
    e, dw_pool_lin, dpool_scale = _pool_bwd_a(u, dy_pool, W["w_pool_out"], W["w_pool_lin"], small["pool_scale"])
    du = _pool_bwd_b(e)
    dos, dps = _merge_bwd(dy_attn, W["w_attn_out"], outs, lses)
    dqkv = [_attn_bwd(gi, qkv, dos[gi], lses[gi], dps[gi]) for gi in range(3)]
    dproj = jnp.concatenate([du] + [dqkv[gi][sec] for sec in range(3) for gi in range(3)] + [dgates], axis=1)
    dw_in = _mm_tn("dw_in", dproj, h1, tm=512, tn=2048, tk=TOKEN_TK)
    tok = emit(dict(w_in=dw_in, w_pool_out=dw_pool_out, w_attn_out=dw_attn_out, w_pool_lin=dw_pool_lin))
    dh1 = _mm_nn("d_h1", dproj, w_in_t, tm=1024, tn=1024, tk=IN_WIDTH // 4, o_dtype=BF16,
                 after=tok)
    grad_x, dg_mix = _rms_bwd("rms1_bwd", x, small["g_mix"] + zero(tok), dh1, dx1, with_bf16=False)

    grads = dict(w_in=dw_in, w_up=dw_up, w_down=dw_down, w_out=dw_out, w_pool_out=dw_pool_out,
                 w_attn_out=dw_attn_out, w_pool_lin=dw_pool_lin, conv_w=dconv_w)
    small_grads = dict(g_mix=dg_mix, b_gate=db_gate, pool_scale=dpool_scale, g_ffn=dg_ffn, conv_b=dconv_b,
                       g_final=dg_final)
    return sq_cols, grad_x, grads, small_grads


def kernel(x, g_mix, w_in, b_gate, w_pool_lin, pool_scale, w_pool_out, w_attn_out, w_out, g_ffn, w_up, conv_w, conv_b, w_down, g_final, loss_target, m_g_mix, m_w_in, m_b_gate, m_w_pool_lin, m_pool_scale, m_w_pool_out, m_w_attn_out, m_w_out, m_g_ffn, m_w_up, m_conv_w, m_conv_b, m_w_down, m_g_final, v_g_mix, v_w_in, v_b_gate, v_w_pool_lin, v_pool_scale, v_w_pool_out, v_w_attn_out, v_w_out, v_g_ffn, v_w_up, v_conv_w, v_conv_b, v_w_down, v_g_final):
    given = dict(g_mix=g_mix, w_in=w_in, b_gate=b_gate, w_pool_lin=w_pool_lin, pool_scale=pool_scale,
                 w_pool_out=w_pool_out, w_attn_out=w_attn_out, w_out=w_out, g_ffn=g_ffn, w_up=w_up, conv_w=conv_w,
                 conv_b=conv_b, w_down=w_down, g_final=g_final)
    mom_m = dict(g_mix=m_g_mix, w_in=m_w_in, b_gate=m_b_gate, w_pool_lin=m_w_pool_lin, pool_scale=m_pool_scale,
                 w_pool_out=m_w_pool_out, w_attn_out=m_w_attn_out, w_out=m_w_out, g_ffn=m_g_ffn, w_up=m_w_up,
                 conv_w=m_conv_w, conv_b=m_conv_b, w_down=m_w_down, g_final=m_g_final)
    mom_v = dict(g_mix=v_g_mix, w_in=v_w_in, b_gate=v_b_gate, w_pool_lin=v_w_pool_lin, pool_scale=v_pool_scale,
                 w_pool_out=v_w_pool_out, w_attn_out=v_w_attn_out, w_out=v_w_out, g_ffn=v_g_ffn, w_up=v_w_up,
                 conv_w=v_conv_w, conv_b=v_conv_b, w_down=v_w_down, g_final=v_g_final)
    def full(g):
        return g.reshape(N_DEV * g.shape[1], g.shape[2])

    shard = dict(w_in=_to_comm("w_in", w_in[0]).astype(BF16))
    W = {}
    first = [shard["w_in"], w_pool_lin[0].reshape(4 * 32, POOL_GROUP_WIDTH),
             jnp.pad(conv_w[0], ((0, CONV_ROWS - 3), (0, 0)))]
    hold = {}
    w_in_wait, first_token = _split_copies(
        "gather_w_in", first, _place_own(first), 4 * len(first), _gather_level1_plan(len(first)),
        start_after=(), wait_after=lambda: hold["h1"])

    stages = (("w_out", "w_pool_out", "w_attn_out", "w_up"), ("w_down",))
    stage_after = [None, None]
    stage_wait = []
    for n in (n for names in stages for n in names):
        shard[n] = _to_comm(n, given[n][0] + first_token[0, 0]).astype(BF16)

    def first_weights(h1):
        hold["h1"] = h1
        w_in_all, lin_all, conv_w_all = _gather_level2(w_in_wait()[1])
        w_in_full = full(w_in_all)
        lin_full = jnp.moveaxis(lin_all.reshape(N_DEV, 4, 32, POOL_GROUP_WIDTH), 0, 1).reshape(
            4, POOL_GROUP_WIDTH, POOL_GROUP_WIDTH).astype(BF16)
        conv_w_full = jnp.moveaxis(conv_w_all[:, :3, :], 0, 1).reshape(3, 2 * D_FF)
        order_after = (w_in_full,)
        for k, names in enumerate(stages):
            srcs = [shard[n] for n in names]
            wait, token = _split_copies(f"gather_stage{k}", srcs, _place_own(srcs), 7 * len(names),
                                        _gather_direct_plan(len(names)), start_after=order_after,
                                        wait_after=lambda k=k: stage_after[k])
            stage_wait.append(wait)
            order_after = (token,)
        return dict(w_in=w_in_full, w_pool_lin=lin_full, conv_w_full=conv_w_full), token

    def late_weights(stage, after):
        stage_after[stage] = after
        return {n: full(l) for n, l in zip(stages[stage], stage_wait[stage]()[1])}

    small = dict(g_mix=g_mix, b_gate=b_gate, pool_scale=pool_scale, g_ffn=g_ffn, conv_b=conv_b,
                 g_final=g_final.reshape(1, D_MODEL))

    pending = []
    done = {}

    def emit(group):
        names = list(group)
        parts = []
        for n in names:
            g = group[n]
            if n == "w_pool_lin":
                g = jnp.moveaxis(g.reshape(4, N_DEV, 32, POOL_GROUP_WIDTH), 1, 0).reshape(
                    N_DEV, 4 * 32, POOL_GROUP_WIDTH).astype(BF16)
            elif n == "conv_w":
                g = jnp.pad(jnp.moveaxis(g.reshape(3, N_DEV, 2 * D_FF // N_DEV), 1, 0),
                            ((0, 0), (0, CONV_ROWS - 3), (0, 0))).astype(BF16)
            else:
                g = g.reshape(N_DEV, g.shape[0] // N_DEV, g.shape[1])
            parts.append(g)
        me = _index(_position())
        lands = [lax.dynamic_update_slice(lax.empty(p.shape, p.dtype), lax.dynamic_index_in_dim(p, me, 0),
                                          (me, 0, 0)) for p in parts]
        wait, token = _split_copies(
            "exchange_" + names[0], parts, lands, 7 * len(names), _exchange_direct_plan(len(names)),
            start_after=(), wait_after=lambda: done["grad_x"])
        pending.append((names, wait))
        return token

    prepared = sum(shard[n][0, 0].astype(F32) for names in stages for n in names) * 0.0
    sq_cols, grad_x, _, small_grads = _local_step(x[0], loss_target[0], W, small,
                                                  first_token=first_token[0, 0] + prepared,
                                                  first_weights=first_weights, late_weights=late_weights, emit=emit)
    done["grad_x"] = grad_x
    loss = lax.psum(0.5 / D_MODEL * jnp.sum(sq_cols), AXES)

    res = {}
    for n, r in [(n, r) for names, wait in pending for n, r in zip(names, wait()[1])]:
        shape = given[n].shape
        rows = math.prod(shape[:-1])
        w2, m2, v2 = (d[n].reshape(rows, shape[-1]) for d in (given, mom_m, mom_v))
        if n in _TRANSPOSED or n == "conv_w":
            g = _sum_partials("sum_" + n, r)
            g = g.T if n in _TRANSPOSED else g[:3]
            upd = (g,) + tuple(_adamw_update("adamw_" + n, w2, g, m2, v2))
        else:
            upd = _sum_adamw("adamw_" + n, r, w2, m2, v2)
        res[n] = [a.reshape(shape) for a in upd]

    packed_small = [_pack_small(d) for d in (given, mom_m, mom_v)]
    res_small = [_unpack_small(p, given) for p in _small_allreduce_adamw(_pack_small(small_grads), *packed_small)]

    order = ["g_mix", "w_in", "b_gate", "w_pool_lin", "pool_scale", "w_pool_out", "w_attn_out", "w_out", "g_ffn",
             "w_up", "conv_w", "conv_b", "w_down", "g_final"]
    outs = [loss, grad_x[None]]
    for kind in range(4):
        for n in order:
            outs.append(res[n][kind] if n in res else res_small[kind][n])
    return tuple(outs)
```

```python
import math

import jax
import jax.numpy as jnp
from jax import lax
from jax.experimental import pallas as pl
from jax.experimental.pallas import tpu as pltpu

F32 = jnp.float32
BF16 = jnp.bfloat16
MESH = pl.DeviceIdType.MESH
AXES = ("x", "y", "c")
N_DEV = 8

D_MODEL = 2048
POOL_WINDOWS = (2, 4, 8, 16)
POOL_GROUP_WIDTH = 256
POOL_WIDTH = 1024
ATTN_GROUPS = ((128, 1), (512, 4), (2048, 16))
SPAN = 128
HEADS_PER_GROUP = 4
N_ATTN_HEADS = 12
HEAD_DIM = 128
ATTN_WIDTH = 1536
QKV_WIDTH = 3 * ATTN_WIDTH
ATTN_OUT_WIDTH = 512
IN_WIDTH = 9728
D_FF = 5632
RMS_EPS = 1e-6
ADAM_LR, ADAM_B1, ADAM_B2, ADAM_EPS, ADAM_WD, ADAM_STEP = 0.001, 0.9, 0.999, 1e-08, 0.01, 10

VMEM_LIMIT_BYTES = 48 * 1024 * 1024
HALO = 16
TOKEN_TK = 2048
MXU_DIM = 256
PACK_COLS = 1024
SMALL_ROWS = 24
NEG_BIG = -1e30

_DN = {"nn": (((1,), (0,)), ((), ())), "nt": (((1,), (1,)), ((), ())), "tn": (((0,), (0,)), ((), ()))}


def _params(*sem):
    return pltpu.CompilerParams(dimension_semantics=sem, vmem_limit_bytes=VMEM_LIMIT_BYTES)


def _dot(a, b, mode):
    return lax.dot_general(a.astype(BF16), b.astype(BF16), _DN[mode], preferred_element_type=F32)


def _mm(name, a, b, *, mode, grid, a_spec, b_spec, o_spec, o_shape, o_dtype, acc_shape, res=None, res_spec=None,
        after=None):
    nk = grid[2]
    n_in = 2 + (res is not None) + (after is not None)

    def body(*refs):
        a_ref, b_ref = refs[:2]
        r_ref = refs[2] if res is not None else None
        o_ref = refs[n_in]
        scr = refs[n_in + 1:]
        prod = _dot(a_ref[...], b_ref[...], mode)

        def finish(val):
            if r_ref is not None:
                val = val + r_ref[...]
            o_ref[...] = val.astype(o_ref.dtype)

        if nk == 1:
            finish(prod)
        else:
            acc = scr[0]
            k = pl.program_id(2)

            @pl.when(k == 0)
            def _():
                acc[...] = prod

            @pl.when(k > 0)
            def _():
                acc[...] += prod

            @pl.when(k == nk - 1)
            def _():
                finish(acc[...])

    in_specs = [a_spec, b_spec]
    args = [a, b]
    if res is not None:
        in_specs.append(res_spec)
        args.append(res)
    if after is not None:
        in_specs.append(pl.BlockSpec((8, 128), lambda i, j, k: (0, 0)))
        args.append(after)
    return pl.pallas_call(
        body, grid=grid, in_specs=in_specs, out_specs=o_spec,
        out_shape=jax.ShapeDtypeStruct(o_shape, o_dtype),
        scratch_shapes=[] if nk == 1 else [pltpu.VMEM(acc_shape, F32)],
        compiler_params=_params("parallel", "parallel", "arbitrary"), name=name)(*args)


def _mm_nn(name, a, b, *, tm, tn, tk, o_dtype, res=None, after=None):
    M, K = a.shape
    N = b.shape[1]
    return _mm(name, a, b, mode="nn", grid=(M // tm, N // tn, K // tk),
               a_spec=pl.BlockSpec((tm, tk), lambda i, j, k: (i, k)),
               b_spec=pl.BlockSpec((tk, tn), lambda i, j, k: (k, j)),
               o_spec=pl.BlockSpec((tm, tn), lambda i, j, k: (i, j)),
               o_shape=(M, N), o_dtype=o_dtype, acc_shape=(tm, tn), res=res,
               res_spec=pl.BlockSpec((tm, tn), lambda i, j, k: (i, j)), after=after)


def _mm_nt(name, a, b, *, tm, tn, tk, o_dtype, n_rows=None, b_row_off=0, after=None):
    M, K = a.shape
    N = b.shape[0] if n_rows is None else n_rows
    off = b_row_off // tn
    return _mm(name, a, b, mode="nt", grid=(M // tm, N // tn, K // tk),
               a_spec=pl.BlockSpec((tm, tk), lambda i, j, k: (i, k)),
               b_spec=pl.BlockSpec((tn, tk), lambda i, j, k: (j + off, k)),
               o_spec=pl.BlockSpec((tm, tn), lambda i, j, k: (i, j)),
               o_shape=(M, N), o_dtype=o_dtype, acc_shape=(tm, tn), after=after)


def _mm_tn(name, a, b, *, tm, tn, tk):
    K, M = a.shape
    N = b.shape[1]
    return _mm(name, a, b, mode="tn", grid=(M // tm, N // tn, K // tk),
               a_spec=pl.BlockSpec((tk, tm), lambda i, j, k: (k, i)),
               b_spec=pl.BlockSpec((tk, tn), lambda i, j, k: (k, j)),
               o_spec=pl.BlockSpec((tm, tn), lambda i, j, k: (i, j)),
               o_shape=(M, N), o_dtype=BF16, acc_shape=(tm, tn))


def _rms_fwd(name, x, g):
    S, D = x.shape
    T = 512

    def body(x_ref, g_ref, h_ref):
        xv = x_ref[...]
        r = lax.rsqrt(jnp.mean(xv * xv, axis=-1, keepdims=True) + RMS_EPS)
        h_ref[...] = (xv * r * g_ref[...]).astype(BF16)

    return pl.pallas_call(
        body, grid=(S // T,),
        in_specs=[pl.BlockSpec((T, D), lambda i: (i, 0)), pl.BlockSpec((1, D), lambda i: (0, 0))],
        out_specs=pl.BlockSpec((T, D), lambda i: (i, 0)),
        out_shape=jax.ShapeDtypeStruct((S, D), BF16), compiler_params=_params("parallel"), name=name)(x, g)


def _out_proj_norm(mixed, w_out, x, g):
    S, D = x.shape
    tm = 512

    def body(m_ref, w_ref, x_ref, g_ref, x1_ref, h_ref):
        x1 = x_ref[...] + _dot(m_ref[...], w_ref[...], "nn")
        x1_ref[...] = x1
        r = lax.rsqrt(jnp.mean(x1 * x1, axis=-1, keepdims=True) + RMS_EPS)
        h_ref[...] = (x1 * r * g_ref[...]).astype(BF16)

    row = pl.BlockSpec((tm, D), lambda i: (i, 0))
    return pl.pallas_call(
        body, grid=(S // tm,),
        in_specs=[row, pl.BlockSpec((D, D), lambda i: (0, 0)), row, pl.BlockSpec((1, D), lambda i: (0, 0))],
        out_specs=[row, row], out_shape=[jax.ShapeDtypeStruct((S, D), F32), jax.ShapeDtypeStruct((S, D), BF16)],
        compiler_params=_params("parallel"), name="out_proj_norm")(mixed, w_out, x, g)


def _rms_bwd(name, x, g, dh, dres, with_bf16):
    S, D = x.shape
    T = 256

    def body(x_ref, g_ref, dh_ref, dres_ref, dx_ref, *rest):
        dg_ref = rest[-1]
        i = pl.program_id(0)
        xv = x_ref[...]
        r = lax.rsqrt(jnp.mean(xv * xv, axis=-1, keepdims=True) + RMS_EPS)
        xhat = xv * r
        dhv = dh_ref[...].astype(F32)
        gdh = dhv * g_ref[...]
        c = jnp.mean(xhat * gdh, axis=-1, keepdims=True)
        dx = dres_ref[...] + r * (gdh - xhat * c)
        dx_ref[...] = dx
        if with_bf16:
            rest[0][...] = dx.astype(BF16)
        part = jnp.sum(dhv * xhat, axis=0, keepdims=True)

        @pl.when(i == 0)
        def _():
            dg_ref[...] = part

        @pl.when(i > 0)
        def _():
            dg_ref[...] += part

    row = pl.BlockSpec((T, D), lambda i: (i, 0))
    vec = pl.BlockSpec((1, D), lambda i: (0, 0))
    narrow = [jax.ShapeDtypeStruct((S, D), BF16)] if with_bf16 else []
    return pl.pallas_call(
        body, grid=(S // T,), in_specs=[row, vec, row, row], out_specs=[row] * (1 + len(narrow)) + [vec],
        out_shape=[jax.ShapeDtypeStruct((S, D), F32)] + narrow + [jax.ShapeDtypeStruct((1, D), F32)],
        compiler_params=_params("arbitrary"), name=name)(x, g, dh, dres)


def _final_loss(name, x, g, target):
    S, D = x.shape
    T = 256

    def body(x_ref, g_ref, t_ref, dx_ref, dxb_ref, dg_ref, sq_ref):
        i = pl.program_id(0)
        xv = x_ref[...]
        r = lax.rsqrt(jnp.mean(xv * xv, axis=-1, keepdims=True) + RMS_EPS)
        xhat = xv * r
        err = xhat * g_ref[...] - t_ref[...]
        dy = err * (1.0 / D)
        gdy = dy * g_ref[...]
        c = jnp.mean(xhat * gdy, axis=-1, keepdims=True)
        dx = r * (gdy - xhat * c)
        dx_ref[...] = dx
        dxb_ref[...] = dx.astype(BF16)
        dg_part = jnp.sum(dy * xhat, axis=0, keepdims=True)
        sq_part = jnp.sum(err * err, axis=0, keepdims=True)

        @pl.when(i == 0)
        def _():
            dg_ref[...] = dg_part
            sq_ref[...] = sq_part

        @pl.when(i > 0)
        def _():
            dg_ref[...] += dg_part
            sq_ref[...] += sq_part

    row = pl.BlockSpec((T, D), lambda i: (i, 0))
    vec = pl.BlockSpec((1, D), lambda i: (0, 0))
    return pl.pallas_call(
        body, grid=(S // T,), in_specs=[row, vec, row], out_specs=[row, row, vec, vec],
        out_shape=[jax.ShapeDtypeStruct((S, D), F32), jax.ShapeDtypeStruct((S, D), BF16),
                   jax.ShapeDtypeStruct((1, D), F32), jax.ShapeDtypeStruct((1, D), F32)],
        compiler_params=_params("arbitrary"), name=name)(x, g, target)


def _prev_halo_spec(T, C, col_map):
    per = T // HALO
    return pl.BlockSpec((HALO, C), lambda *g: (jnp.maximum(g[0] * per - 1, 0), col_map(*g)))


def _pool_window_sums(ext, T, i):
    t = i * T + lax.broadcasted_iota(jnp.int32, (T, 1), 0)
    outs = []
    cnts = []
    for gi, w in enumerate(POOL_WINDOWS):
        s = ext[:, gi * POOL_GROUP_WIDTH:(gi + 1) * POOL_GROUP_WIDTH]
        sh = 1
        while sh < w:
            s = s + pltpu.roll(s, sh, 0)
            sh *= 2
        cnt = jnp.minimum(t + 1, w).astype(F32)
        outs.append(s[HALO:] / cnt - ext[HALO:, gi * POOL_GROUP_WIDTH:(gi + 1) * POOL_GROUP_WIDTH])
        cnts.append(cnt)
    return outs, cnts


def _load_ext(ext_ref, halo_ref, blk_ref, first):
    ext_ref[0:HALO, :] = jnp.where(first, 0.0, halo_ref[...].astype(F32))
    ext_ref[HALO:, :] = blk_ref[...].astype(F32)


def _pool_fwd(u, w_lin, scale):
    S = u.shape[0]
    T = 512
    W = POOL_WIDTH

    def body(u_ref, halo_ref, wl_ref, sc_ref, y_ref, ext_ref):
        i = pl.program_id(0)
        _load_ext(ext_ref, halo_ref, u_ref, i == 0)
        pooled, _ = _pool_window_sums(ext_ref[...], T, i)
        for gi in range(4):
            z = _dot(pooled[gi], wl_ref[gi], "nn")
            cs = slice(gi * POOL_GROUP_WIDTH, (gi + 1) * POOL_GROUP_WIDTH)
            y_ref[:, cs] = (z * sc_ref[:, cs]).astype(BF16)

    return pl.pallas_call(
        body, grid=(S // T,),
        in_specs=[pl.BlockSpec((T, W), lambda i: (i, 0)), _prev_halo_spec(T, W, lambda i: 0),
                  pl.BlockSpec((4, 256, 256), lambda i: (0, 0, 0)), pl.BlockSpec((1, W), lambda i: (0, 0))],
        out_specs=pl.BlockSpec((T, W), lambda i: (i, 0)),
        out_shape=jax.ShapeDtypeStruct((S, W), BF16),
        scratch_shapes=[pltpu.VMEM((HALO + T, W), F32)],
        compiler_params=_params("parallel"), name="pool_fwd")(u, u, w_lin, scale)


def _pool_bwd_a(u, dy_out, w_pool_out_t, w_lin, scale):
    S = u.shape[0]
    T = 512
    W = POOL_WIDTH

    def body(u_ref, halo_ref, dyo_ref, wpo_ref, wl_ref, sc_ref, e_ref, dwl_ref, dsc_ref, ext_ref):
        i = pl.program_id(0)
        dy = _dot(dyo_ref[...], wpo_ref[...], "nn")
        _load_ext(ext_ref, halo_ref, u_ref, i == 0)
        pooled, cnts = _pool_window_sums(ext_ref[...], T, i)
        for gi in range(4):
            cs = slice(gi * POOL_GROUP_WIDTH, (gi + 1) * POOL_GROUP_WIDTH)
            z = _dot(pooled[gi], wl_ref[gi], "nn")
            dyg = dy[:, cs]
            dsc = jnp.sum(dyg * z, axis=0, keepdims=True)
            dz = dyg * sc_ref[:, cs]
            dwl = _dot(pooled[gi], dz, "tn")
            dpooled = _dot(dz, wl_ref[gi], "nt")
            e_ref[:, cs] = dpooled / cnts[gi]

            @pl.when(i == 0)
            def _():
                dwl_ref[gi] = dwl
                dsc_ref[:, cs] = dsc

            @pl.when(i > 0)
            def _():
                dwl_ref[gi] += dwl
                dsc_ref[:, cs] += dsc

    return pl.pallas_call(
        body, grid=(S // T,),
        in_specs=[pl.BlockSpec((T, W), lambda i: (i, 0)), _prev_halo_spec(T, W, lambda i: 0),
                  pl.BlockSpec((T, D_MODEL), lambda i: (i, 0)), pl.BlockSpec((D_MODEL, W), lambda i: (0, 0)),
                  pl.BlockSpec((4, 256, 256), lambda i: (0, 0, 0)), pl.BlockSpec((1, W), lambda i: (0, 0))],
        out_specs=[pl.BlockSpec((T, W), lambda i: (i, 0)), pl.BlockSpec((4, 256, 256), lambda i: (0, 0, 0)),
                   pl.BlockSpec((1, W), lambda i: (0, 0))],
        out_shape=[jax.ShapeDtypeStruct((S, W), F32), jax.ShapeDtypeStruct((4, 256, 256), F32),
                   jax.ShapeDtypeStruct((1, W), F32)],
        scratch_shapes=[pltpu.VMEM((HALO + T, W), F32)],
        compiler_params=_params("arbitrary"), name="pool_bwd_a")(u, u, dy_out, w_pool_out_t, w_lin, scale)


def _pool_bwd_b(e):
    S = e.shape[0]
    T = 512
    W = POOL_WIDTH
    per = T // HALO
    n_blk = S // T

    def body(e_ref, halo_ref, du_ref, ext_ref):
        i = pl.program_id(0)
        ext_ref[0:T, :] = e_ref[...]
        ext_ref[T:, :] = jnp.where(i == n_blk - 1, 0.0, halo_ref[...])
        ext = ext_ref[...]
        n = T + HALO
        t = i * T + lax.broadcasted_iota(jnp.int32, (T, 1), 0)
        for gi, w in enumerate(POOL_WINDOWS):
            cs = slice(gi * POOL_GROUP_WIDTH, (gi + 1) * POOL_GROUP_WIDTH)
            s = ext[:, cs]
            sh = 1
            while sh < w:
                s = s + pltpu.roll(s, n - sh, 0)
                sh *= 2
            cnt = jnp.minimum(t + 1, w).astype(F32)
            du_ref[:, cs] = (s[0:T] - ext[0:T, cs] * cnt).astype(BF16)

    return pl.pallas_call(
        body, grid=(n_blk,),
        in_specs=[pl.BlockSpec((T, W), lambda i: (i, 0)),
                  pl.BlockSpec((HALO, W), lambda i: (jnp.minimum((i + 1) * per, S // HALO - 1), 0))],
        out_specs=pl.BlockSpec((T, W), lambda i: (i, 0)),
        out_shape=jax.ShapeDtypeStruct((S, W), BF16),
        scratch_shapes=[pltpu.VMEM((T + HALO, W), F32)],
        compiler_params=_params("parallel"), name="pool_bwd_b")(e, e)


def _slope(head):
    return 2.0 ** (-8.0 * (head + 1) / N_ATTN_HEADS)


def _attn_masks():
    row = lax.broadcasted_iota(jnp.int32, (SPAN, SPAN), 0)
    col = lax.broadcasted_iota(jnp.int32, (SPAN, SPAN), 1)
    dist_cur = (row - col).astype(F32)
    return dist_cur, col <= row, col >= row


ATTN_BATCH = 16


def _attn_tiling(gi, S):
    d = ATTN_GROUPS[gi][1]
    hp = HEADS_PER_GROUP if d < 16 else 1
    nblk = ATTN_BATCH // hp if d == 1 else 1
    C = SPAN * d * nblk
    return d, C, S // C, hp, hp * HEAD_DIM, min(ATTN_BATCH // (hp * nblk), d), nblk


_BATCHED_DN = {"nt": (((2,), (2,)), ((0,), (0,))), "nn": (((2,), (1,)), ((0,), (0,))), "tn": (((1,), (1,)), ((0,), (0,)))}


def _bdot(a, b, mode):
    return lax.dot_general(a.astype(BF16), b.astype(BF16), _BATCHED_DN[mode], preferred_element_type=F32)


def _block_rows(blk, r, d):
    return pl.ds(blk * SPAN * d + r, SPAN, stride=d)


def _take(scr, hp, it, d, rb, nblk, shift=0, other=None):
    out = []
    for hi in range(hp):
        for blk in range(nblk):
            inside = 0 <= blk + shift < nblk
            src, src_blk = (scr, blk + shift) if inside else (other, nblk - 1 if shift < 0 else 0)
            out += [src[hi, _block_rows(src_blk, it * rb + j, d), :] for j in range(rb)]
    return jnp.stack(out)


def _put(scr, hp, it, d, rb, nblk, val):
    n = 0
    for hi in range(hp):
        for blk in range(nblk):
            for j in range(rb):
                scr[hi, _block_rows(blk, it * rb + j, d), :] = val[n]
                n += 1


def _batch_bias(gi, d, hp, rb, nblk, dist, valid, edge_blk=None, edge_ok=None):
    out = []
    for hi in range(hp):
        bias = (-(_head_slope(gi, hp, hi) * d)) * dist
        for blk in range(nblk):
            ok = jnp.logical_and(valid, edge_ok) if blk == edge_blk else valid
            out += [jnp.where(ok, bias, NEG_BIG)] * rb
    return jnp.stack(out)


def _head_slope(gi, hp, hi):
    if hp == HEADS_PER_GROUP:
        return _slope(gi * HEADS_PER_GROUP + hi)
    first = pl.program_id(1) * hp
    sl = jnp.float32(_slope(gi * HEADS_PER_GROUP + hi))
    for h0 in range(hp, HEADS_PER_GROUP, hp):
        sl = jnp.where(first == h0, jnp.float32(_slope(gi * HEADS_PER_GROUP + h0 + hi)), sl)
    return sl


def _split_heads(src_ref, dst_ref, hp):
    for hi in range(hp):
        dst_ref[hi] = src_ref[:, hi * HEAD_DIM:(hi + 1) * HEAD_DIM].astype(F32)


def _attn_fwd(gi, qkv):
    S = qkv.shape[0]
    d, C, nb, hp, W, rb, nblk = _attn_tiling(gi, S)
    scale = HEAD_DIM ** -0.5

    def body(q_ref, kc_ref, kp_ref, vc_ref, vp_ref, o_ref, lse_ref, qs, kcs, kps, vcs, vps, os, ls):
        b = pl.program_id(0)
        for src, dst in ((q_ref, qs), (kc_ref, kcs), (kp_ref, kps), (vc_ref, vcs), (vp_ref, vps)):
            _split_heads(src, dst, hp)
        dist_cur, valid_cur, valid_near = _attn_masks()
        bias_cur = _batch_bias(gi, d, hp, rb, nblk, dist_cur, valid_cur)
        bias_prev = _batch_bias(gi, d, hp, rb, nblk, dist_cur + SPAN, valid_near, edge_blk=0, edge_ok=b > 0)
        at = (d, rb, nblk)

        def sub_sequences(it, carry):
            q = _take(qs, hp, it, *at)
            s_cur = _bdot(q, _take(kcs, hp, it, *at), "nt") * scale + bias_cur
            s_prev = _bdot(q, _take(kcs, hp, it, *at, shift=-1, other=kps), "nt") * scale + bias_prev
            m = jnp.maximum(jnp.max(s_cur, axis=-1, keepdims=True), jnp.max(s_prev, axis=-1, keepdims=True))
            p_cur = jnp.exp(s_cur - m)
            p_prev = jnp.exp(s_prev - m)
            l = jnp.sum(p_cur, axis=-1, keepdims=True) + jnp.sum(p_prev, axis=-1, keepdims=True)
            o = (_bdot(p_cur, _take(vcs, hp, it, *at), "nn")
                 + _bdot(p_prev, _take(vcs, hp, it, *at, shift=-1, other=vps), "nn")) / l
            _put(os, hp, it, *at, o)
            _put(ls, hp, it, *at, jnp.broadcast_to(m + jnp.log(l), o.shape))
            return carry

        lax.fori_loop(0, d // rb, sub_sequences, 0)
        for hi in range(hp):
            hs = slice(hi * HEAD_DIM, (hi + 1) * HEAD_DIM)
            o_ref[:, hs] = os[hi]
            lse_ref[:, hs] = ls[hi]

    per_sec = ATTN_WIDTH // W
    per_grp = ATTN_OUT_WIDTH // W

    def spec(sec, shift):
        return pl.BlockSpec((C, W), lambda b, h: (jnp.maximum(b + shift, 0), sec * per_sec + gi * per_grp + h))

    out = pl.BlockSpec((C, W), lambda b, h: (b, h))
    scr = pltpu.VMEM((hp, C, HEAD_DIM), F32)
    return pl.pallas_call(
        body, grid=(nb, HEADS_PER_GROUP // hp),
        in_specs=[spec(0, 0), spec(1, 0), spec(1, -1), spec(2, 0), spec(2, -1)], out_specs=[out, out],
        out_shape=[jax.ShapeDtypeStruct((S, ATTN_OUT_WIDTH), F32)] * 2, scratch_shapes=[scr] * 7,
        compiler_params=_params("parallel", "parallel"), name=f"attn_fwd_g{gi}")(qkv, qkv, qkv, qkv, qkv)


def _attn_bwd(gi, qkv, do, lse, dp):
    S = qkv.shape[0]
    d, C, nb, hp, W, rb, nblk = _attn_tiling(gi, S)
    scale = HEAD_DIM ** -0.5
    n_in = 12

    def body(*refs):
        ins, (dq_ref, dk_ref, dv_ref) = refs[:n_in], refs[n_in:n_in + 3]
        scr = refs[n_in + 3:]
        for src, dst in zip(ins, scr[:n_in]):
            _split_heads(src, dst, hp)
        qs, kcs, vcs, kps, vps, qns, dos, dons, lses, lsens, dps, dpns = scr[:n_in]
        dqs, dks, dvs = scr[n_in:]
        b = pl.program_id(0)
        dist_cur, valid_cur, valid_near = _attn_masks()
        bias_cur = _batch_bias(gi, d, hp, rb, nblk, dist_cur, valid_cur)
        bias_prev = _batch_bias(gi, d, hp, rb, nblk, dist_cur + SPAN, valid_near, edge_blk=0, edge_ok=b > 0)
        bias_next = _batch_bias(gi, d, hp, rb, nblk, dist_cur + SPAN, valid_near, edge_blk=nblk - 1,
                                edge_ok=b < nb - 1)
        at = (d, rb, nblk)

        def sub_sequences(it, carry):
            q, kc, vc, dov, lse_b, dp_b = (_take(s, hp, it, *at) for s in (qs, kcs, vcs, dos, lses, dps))
            kp = _take(kcs, hp, it, *at, shift=-1, other=kps)
            vp = _take(vcs, hp, it, *at, shift=-1, other=vps)
            qn, don, lse_n, dp_n = (_take(s, hp, it, *at, shift=1, other=o)
                                    for s, o in ((qs, qns), (dos, dons), (lses, lsens), (dps, dpns)))
            p_cur = jnp.exp(_bdot(q, kc, "nt") * scale + bias_cur - lse_b)
            p_prev = jnp.exp(_bdot(q, kp, "nt") * scale + bias_prev - lse_b)
            ds_cur = p_cur * (_bdot(dov, vc, "nt") - dp_b)
            ds_prev = p_prev * (_bdot(dov, vp, "nt") - dp_b)
            _put(dqs, hp, it, *at, (_bdot(ds_cur, kc, "nn") + _bdot(ds_prev, kp, "nn")) * scale)
            p_next = jnp.exp(_bdot(qn, kc, "nt") * scale + bias_next - lse_n)
            ds_next = p_next * (_bdot(don, vc, "nt") - dp_n)
            _put(dvs, hp, it, *at, _bdot(p_cur, dov, "tn") + _bdot(p_next, don, "tn"))
            _put(dks, hp, it, *at, (_bdot(ds_cur, q, "tn") + _bdot(ds_next, qn, "tn")) * scale)
            return carry

        lax.fori_loop(0, d // rb, sub_sequences, 0)
        for hi in range(hp):
            hs = slice(hi * HEAD_DIM, (hi + 1) * HEAD_DIM)
            dq_ref[:, hs] = dqs[hi].astype(BF16)
            dk_ref[:, hs] = dks[hi].astype(BF16)
            dv_ref[:, hs] = dvs[hi].astype(BF16)

    per_sec = ATTN_WIDTH // W
    per_grp = ATTN_OUT_WIDTH // W

    def qkv_spec(sec, shift):
        return pl.BlockSpec((C, W), lambda b, h: (jnp.clip(b + shift, 0, nb - 1), sec * per_sec + gi * per_grp + h))

    def act_spec(shift):
        return pl.BlockSpec((C, W), lambda b, h: (jnp.clip(b + shift, 0, nb - 1), h))

    out = act_spec(0)
    scr = pltpu.VMEM((hp, C, HEAD_DIM), F32)
    return pl.pallas_call(
        body, grid=(nb, HEADS_PER_GROUP // hp),
        in_specs=[qkv_spec(0, 0), qkv_spec(1, 0), qkv_spec(2, 0), qkv_spec(1, -1), qkv_spec(2, -1), qkv_spec(0, 1),
                  act_spec(0), act_spec(1), act_spec(0), act_spec(1), act_spec(0), act_spec(1)],
        out_specs=[out, out, out], out_shape=[jax.ShapeDtypeStruct((S, ATTN_OUT_WIDTH), BF16)] * 3,
        scratch_shapes=[scr] * (n_in + 3),
        compiler_params=_params("parallel", "parallel"), name=f"attn_bwd_g{gi}")(
            qkv, qkv, qkv, qkv, qkv, qkv, do, do, lse, lse, dp, dp)


def _group_weights(l_refs):
    l0, l1, l2 = (r[...] for r in l_refs)
    m = jnp.maximum(jnp.maximum(l0, l1), l2)
    e = [jnp.exp(l0 - m), jnp.exp(l1 - m), jnp.exp(l2 - m)]
    tot = e[0] + e[1] + e[2]
    return [ei / tot for ei in e]


def _merge_fwd(outs, lses):
    S, W = outs[0].shape
    T = 512

    def body(o0, o1, o2, l0, l1, l2, y_ref):
        w = _group_weights((l0, l1, l2))
        y_ref[...] = (w[0] * o0[...] + w[1] * o1[...] + w[2] * o2[...]).astype(BF16)

    blk = pl.BlockSpec((T, W), lambda i: (i, 0))
    return pl.pallas_call(
        body, grid=(S // T,), in_specs=[blk] * 6, out_specs=blk, out_shape=jax.ShapeDtypeStruct((S, W), BF16),
        compiler_params=_params("parallel"), name="merge_fwd")(*outs, *lses)


def _merge_bwd(dy_out, w_attn_out_t, outs, lses):
    S, W = outs[0].shape
    T = 512

    def body(dyo_ref, wao_ref, o0, o1, o2, l0, l1, l2, do0, do1, do2, dp0, dp1, dp2):
        w = _group_weights((l0, l1, l2))
        dyv = _dot(dyo_ref[...], wao_ref[...], "nn")
        y = w[0] * o0[...] + w[1] * o1[...] + w[2] * o2[...]
        prod = dyv * y
        dots = []
        for hi in range(HEADS_PER_GROUP):
            hs = slice(hi * HEAD_DIM, (hi + 1) * HEAD_DIM)
            dots.append(jnp.broadcast_to(jnp.sum(prod[:, hs], axis=-1, keepdims=True), (T, HEAD_DIM)))
        dot_b = jnp.concatenate(dots, axis=1)
        for wg, do_ref, dp_ref in zip(w, (do0, do1, do2), (dp0, dp1, dp2)):
            do_ref[...] = (wg * dyv).astype(BF16)
            dp_ref[...] = wg * dot_b

    blk = pl.BlockSpec((T, W), lambda i: (i, 0))
    res = pl.pallas_call(
        body, grid=(S // T,),
        in_specs=[pl.BlockSpec((T, D_MODEL), lambda i: (i, 0)), pl.BlockSpec((D_MODEL, W), lambda i: (0, 0))] + [blk] * 6,
        out_specs=[blk] * 6,
        out_shape=[jax.ShapeDtypeStruct((S, W), BF16)] * 3 + [jax.ShapeDtypeStruct((S, W), F32)] * 3,
        compiler_params=_params("parallel"), name="merge_bwd")(dy_out, w_attn_out_t, *outs, *lses)
    return res[:3], res[3:]


def _mix_fwd(pool_y, attn_y, w_pool_out_t, w_attn_out_t, gates_pre, b_gate):
    S = pool_y.shape[0]
    D = D_MODEL
    tm, tn = 1024, 512
    nj = D // tn

    def body(p_ref, a_ref, wp_ref, wa_ref, g0_ref, g1_ref, b0_ref, b1_ref, mixed_ref, yp_ref, ya_ref):
        yp = _dot(p_ref[...], wp_ref[...], "nt")
        ya = _dot(a_ref[...], wa_ref[...], "nt")
        g0 = jax.nn.sigmoid(g0_ref[...] + b0_ref[...])
        g1 = jax.nn.sigmoid(g1_ref[...] + b1_ref[...])
        mixed_ref[...] = (g0 * yp + g1 * ya).astype(BF16)
        yp_ref[...] = yp.astype(BF16)
        ya_ref[...] = ya.astype(BF16)

    out = pl.BlockSpec((tm, tn), lambda i, j: (i, j))
    return pl.pallas_call(
        body, grid=(S // tm, nj),
        in_specs=[pl.BlockSpec((tm, POOL_WIDTH), lambda i, j: (i, 0)),
                  pl.BlockSpec((tm, ATTN_OUT_WIDTH), lambda i, j: (i, 0)),
                  pl.BlockSpec((tn, POOL_WIDTH), lambda i, j: (j, 0)),
                  pl.BlockSpec((tn, ATTN_OUT_WIDTH), lambda i, j: (j, 0)),
                  pl.BlockSpec((tm, tn), lambda i, j: (i, j)), pl.BlockSpec((tm, tn), lambda i, j: (i, j + nj)),
                  pl.BlockSpec((1, tn), lambda i, j: (0, j)), pl.BlockSpec((1, tn), lambda i, j: (0, j + nj))],
        out_specs=[out, out, out], out_shape=[jax.ShapeDtypeStruct((S, D), BF16)] * 3,
        compiler_params=_params("parallel", "parallel"), name="mix_fwd")(
            pool_y, attn_y, w_pool_out_t, w_attn_out_t, gates_pre, gates_pre, b_gate, b_gate)


def _mix_bwd(dx, w_out, gates_pre, b_gate, y_pool, y_attn):
    S = dx.shape[0]
    D = D_MODEL
    T = 256

    def body(dx_ref, w_ref, gp_ref, b_ref, yp_ref, ya_ref, dyp_ref, dya_ref, dg_ref, db_ref):
        i = pl.program_id(0)
        dm = _dot(dx_ref[...], w_ref[...], "nt")
        parts = []
        for br, (y_ref, dy_ref) in enumerate(((yp_ref, dyp_ref), (ya_ref, dya_ref))):
            cs = slice(br * D, (br + 1) * D)
            g = jax.nn.sigmoid(gp_ref[:, cs] + b_ref[:, cs])
            dy_ref[...] = (dm * g).astype(BF16)
            dpre = dm * y_ref[...].astype(F32) * g * (1.0 - g)
            dg_ref[:, cs] = dpre.astype(BF16)
            parts.append(jnp.sum(dpre, axis=0, keepdims=True))

        @pl.when(i == 0)
        def _():
            db_ref[:, 0:D] = parts[0]
            db_ref[:, D:2 * D] = parts[1]

        @pl.when(i > 0)
        def _():
            db_ref[:, 0:D] += parts[0]
            db_ref[:, D:2 * D] += parts[1]

    row = pl.BlockSpec((T, D), lambda i: (i, 0))
    row2 = pl.BlockSpec((T, 2 * D), lambda i: (i, 0))
    vec2 = pl.BlockSpec((1, 2 * D), lambda i: (0, 0))
    return pl.pallas_call(
        body, grid=(S // T,), in_specs=[row, pl.BlockSpec((D, D), lambda i: (0, 0)), row2, vec2, row, row],
        out_specs=[row, row, row2, vec2],
        out_shape=[jax.ShapeDtypeStruct((S, D), BF16), jax.ShapeDtypeStruct((S, D), BF16),
                   jax.ShapeDtypeStruct((S, 2 * D), BF16), jax.ShapeDtypeStruct((1, 2 * D), F32)],
        compiler_params=_params("arbitrary"), name="mix_bwd")(dx, w_out, gates_pre, b_gate, y_pool, y_attn)


_CONV_T, _CONV_C = 512, 512


def _conv_taps(ext_ref, h, w_ref, b_ref, cols=slice(None)):
    ext = ext_ref[h, :, cols]
    x2 = pltpu.roll(ext, 2, 0)[HALO:]
    x1 = pltpu.roll(ext, 1, 0)[HALO:]
    x0 = ext[HALO:]
    y = b_ref[:, cols] + w_ref[0:1, cols] * x2
    y = y + w_ref[1:2, cols] * x1
    y = y + w_ref[2:3, cols] * x0
    return y, (x2, x1, x0)


def _gelu_parts(a):
    cdf = 0.5 * (1.0 + lax.erf(a * (1.0 / math.sqrt(2.0))))
    return cdf, a * cdf


def _conv_specs(S):
    T, C = _CONV_T, _CONV_C
    nj = D_FF // C
    blk = pl.BlockSpec((2, T, C), lambda i, j: (0, i, j))
    per = T // HALO
    halo = pl.BlockSpec((2, HALO, C), lambda i, j: (0, jnp.maximum(i * per - 1, 0), j))
    w_a = pl.BlockSpec((3, C), lambda i, j: (0, j))
    w_b = pl.BlockSpec((3, C), lambda i, j: (0, j + nj))
    b_a = pl.BlockSpec((1, C), lambda i, j: (0, j))
    b_b = pl.BlockSpec((1, C), lambda i, j: (0, j + nj))
    return T, C, nj, blk, halo, w_a, w_b, b_a, b_b


def _conv_load(ext_ref, halo_ref, up_ref, first):
    for h in range(2):
        ext_ref[h, 0:HALO, :] = jnp.where(first, 0.0, halo_ref[h].astype(F32))
        ext_ref[h, HALO:, :] = up_ref[h].astype(F32)


def _up_proj_convffn(h, w_up_t, conv_w, conv_b):
    S = h.shape[0]
    tm, C = 1024, _CONV_C
    nj = D_FF // C

    def body(h_ref, wa_ref, wb_ref, cwa_ref, cwb_ref, cba_ref, cbb_ref, up_ref, act_ref, ext_ref, carry_ref):
        i, j = pl.program_id(0), pl.program_id(1)
        for half, w_ref in enumerate((wa_ref, wb_ref)):
            pre = _dot(h_ref[...], w_ref[...], "nt").astype(BF16)
            up_ref[half] = pre
            ext_ref[half, 0:HALO, :] = jnp.where(i == 0, 0.0, carry_ref[j, half])
            ext_ref[half, HALO:, :] = pre.astype(F32)
            carry_ref[j, half] = ext_ref[half, tm:tm + HALO, :]
        a, _ = _conv_taps(ext_ref, 0, cwa_ref, cba_ref)
        b, _ = _conv_taps(ext_ref, 1, cwb_ref, cbb_ref)
        _, gelu = _gelu_parts(a)
        act_ref[...] = (gelu * b).astype(BF16)

    def w_spec(half):
        return pl.BlockSpec((C, D_MODEL), lambda i, j: (j + half * nj, 0))

    def c_spec(rows, half):
        return pl.BlockSpec((rows, C), lambda i, j: (0, j + half * nj))

    return pl.pallas_call(
        body, grid=(S // tm, nj),
        in_specs=[pl.BlockSpec((tm, D_MODEL), lambda i, j: (i, 0)), w_spec(0), w_spec(1),
                  c_spec(3, 0), c_spec(3, 1), c_spec(1, 0), c_spec(1, 1)],
        out_specs=[pl.BlockSpec((2, tm, C), lambda i, j: (0, i, j)), pl.BlockSpec((tm, C), lambda i, j: (i, j))],
        out_shape=[jax.ShapeDtypeStruct((2, S, D_FF), BF16), jax.ShapeDtypeStruct((S, D_FF), BF16)],
        scratch_shapes=[pltpu.VMEM((2, HALO + tm, C), F32), pltpu.VMEM((nj, 2, HALO, C), F32)],
        compiler_params=_params("arbitrary", "arbitrary"), name="up_proj_convffn")(
            h, w_up_t, w_up_t, conv_w, conv_w, conv_b, conv_b)


def _convffn_bwd_a(up_pre, dx, w_down, conv_w, conv_b):
    S = up_pre.shape[1]
    T, C, nj, blk, halo, w_a, w_b, b_a, b_b = _conv_specs(S)
    def swap(spec):
        return pl.BlockSpec(spec.block_shape, lambda j, i, _m=spec.index_map: _m(i, j))

    def body(up_ref, halo_ref, dx_ref, wd_ref, wa_ref, wb_ref, ba_ref, bb_ref, dup_ref, dwa_ref, dwb_ref, dba_ref,
             dbb_ref, ext_ref):
        i = pl.program_id(1)
        _conv_load(ext_ref, halo_ref, up_ref, i == 0)

        @pl.when(i == 0)
        def _():
            for acc_ref in (dwa_ref, dwb_ref, dba_ref, dbb_ref):
                acc_ref[...] = jnp.zeros_like(acc_ref)

        for c in range(C // MXU_DIM):
            cs = slice(c * MXU_DIM, (c + 1) * MXU_DIM)
            dact = _dot(dx_ref[...], wd_ref[cs, :], "nt")
            a, xa = _conv_taps(ext_ref, 0, wa_ref, ba_ref, cs)
            b, xb = _conv_taps(ext_ref, 1, wb_ref, bb_ref, cs)
            cdf, gelu = _gelu_parts(a)
            dgelu = cdf + a * (jnp.exp(-0.5 * a * a) * (1.0 / math.sqrt(2.0 * math.pi)))
            da = dact * b * dgelu
            db = dact * gelu
            dup_ref[0, :, cs] = da.astype(BF16)
            dup_ref[1, :, cs] = db.astype(BF16)
            for dval, xs, dw_ref, dbias_ref in ((da, xa, dwa_ref, dba_ref), (db, xb, dwb_ref, dbb_ref)):
                dw_ref[:, cs] += jnp.concatenate([jnp.sum(dval * xk, axis=0, keepdims=True) for xk in xs], axis=0)
                dbias_ref[:, cs] += jnp.sum(dval, axis=0, keepdims=True)

    dx_spec = pl.BlockSpec((T, D_MODEL), lambda j, i: (i, 0))
    wd_spec = pl.BlockSpec((C, D_MODEL), lambda j, i: (j, 0))
    dw_spec = pl.BlockSpec((3, C), lambda j, i: (0, j))
    db_spec = pl.BlockSpec((1, C), lambda j, i: (0, j))
    res = pl.pallas_call(
        body, grid=(nj, S // T),
        in_specs=[swap(blk), swap(halo), dx_spec, wd_spec, swap(w_a), swap(w_b), swap(b_a), swap(b_b)],
        out_specs=[swap(blk), dw_spec, dw_spec, db_spec, db_spec],
        out_shape=[jax.ShapeDtypeStruct((2, S, D_FF), BF16), jax.ShapeDtypeStruct((3, D_FF), F32),
                   jax.ShapeDtypeStruct((3, D_FF), F32), jax.ShapeDtypeStruct((1, D_FF), F32),
                   jax.ShapeDtypeStruct((1, D_FF), F32)],
        scratch_shapes=[pltpu.VMEM((2, HALO + T, C), F32)],
        compiler_params=_params("parallel", "arbitrary"), name="convffn_bwd_a")(
            up_pre, up_pre, dx, w_down, conv_w, conv_w, conv_b, conv_b)
    dup, dwa, dwb, dba, dbb = res
    dconv_w = jnp.concatenate([dwa, dwb], axis=1)
    dconv_b = jnp.concatenate([dba, dbb], axis=1)
    return dup, dconv_w, dconv_b


def _convffn_bwd_b(dup, conv_w, w_up_t):
    S = dup.shape[1]
    tm, tk = 1024, D_FF // 4
    per_half = D_FF // tk
    nk = 2 * per_half
    per = tm // HALO
    n_blk = S // tm

    def body(d_ref, halo_ref, cw_ref, w_ref, dpre_ref, dh_ref, ext_ref, acc_ref):
        i, k = pl.program_id(0), pl.program_id(1)
        n = tm + HALO
        ext_ref[0:tm, :] = d_ref[...].astype(F32)
        ext_ref[tm:, :] = jnp.where(i == n_blk - 1, 0.0, halo_ref[...].astype(F32))
        ext = ext_ref[...]
        y = cw_ref[2:3, :] * ext[0:tm] + cw_ref[1:2, :] * pltpu.roll(ext, n - 1, 0)[0:tm]
        y = y + cw_ref[0:1, :] * pltpu.roll(ext, n - 2, 0)[0:tm]
        pre = y.astype(BF16)
        dpre_ref[...] = pre
        prod = _dot(pre, w_ref[...], "nn")

        @pl.when(k == 0)
        def _():
            acc_ref[...] = prod

        @pl.when(k > 0)
        def _():
            acc_ref[...] += prod

        @pl.when(k == nk - 1)
        def _():
            dh_ref[...] = acc_ref[...].astype(BF16)

    blk = pl.BlockSpec((None, tm, tk), lambda i, k: (k // per_half, i, k % per_half))
    halo = pl.BlockSpec((None, HALO, tk),
                        lambda i, k: (k // per_half, jnp.minimum((i + 1) * per, S // HALO - 1), k % per_half))
    return pl.pallas_call(
        body, grid=(n_blk, nk),
        in_specs=[blk, halo, pl.BlockSpec((3, tk), lambda i, k: (0, k)), pl.BlockSpec((tk, D_MODEL), lambda i, k: (k, 0))],
        out_specs=[blk, pl.BlockSpec((tm, D_MODEL), lambda i, k: (i, 0))],
        out_shape=[jax.ShapeDtypeStruct((2, S, D_FF), BF16), jax.ShapeDtypeStruct((S, D_MODEL), BF16)],
        scratch_shapes=[pltpu.VMEM((tm + HALO, tk), F32), pltpu.VMEM((tm, D_MODEL), F32)],
        compiler_params=_params("parallel", "arbitrary"), name="convffn_bwd_b")(dup, dup, conv_w, w_up_t)


def _position():
    return lax.axis_index("x"), lax.axis_index("y"), lax.axis_index("c")


def _flip(pos, k):
    x, y, c = pos
    return (1 - x if k & 4 else x, 1 - y if k & 2 else y, 1 - c if k & 1 else c)


def _index(pos):
    return 4 * pos[0] + 2 * pos[1] + pos[2]


_HBM = pl.BlockSpec(memory_space=pltpu.HBM)
_SEM = pl.BlockSpec(memory_space=pltpu.SEMAPHORE)
_ANY = pl.BlockSpec(memory_space=pl.ANY)
_EFFECT = pltpu.SideEffectType.DATAFLOW_SIDE_EFFECTING


def _in_hbm(a):
    return pltpu.with_memory_space_constraint(a, pltpu.HBM)


def _split_copies(name, srcs, lands, n_copies, plan, start_after, wait_after):
    ns, nl = len(srcs), len(lands)
    start_after = tuple(start_after)
    na = len(start_after)

    def start_body(*refs):
        src_refs, land_refs = refs[:ns], refs[ns:ns + nl]
        send_sems, recv_sems = refs[ns + nl + na], refs[ns + nl + na + 1]
        token = refs[-1]
        for k, (src, dst, _, peer) in enumerate(plan(src_refs, land_refs)):
            pltpu.make_async_remote_copy(src_ref=src, dst_ref=dst, send_sem=send_sems.at[k], recv_sem=recv_sems.at[k],
                                         device_id=peer, device_id_type=MESH).start()
        token[...] = jnp.zeros_like(token)

    thru = [pltpu.HBM(a.shape, a.dtype) for a in list(srcs) + list(lands)]
    res = pl.pallas_call(
        start_body, name=name + "_start",
        out_shape=[pltpu.SemaphoreType.DMA((n_copies,)), pltpu.SemaphoreType.DMA((n_copies,))] + thru
        + [jax.ShapeDtypeStruct((8, 128), F32)],
        in_specs=[_HBM] * (ns + nl) + [_ANY] * na,
        out_specs=[_SEM, _SEM] + [_HBM] * (ns + nl) + [pl.BlockSpec(memory_space=pltpu.VMEM)],
        input_output_aliases={i: 2 + i for i in range(ns + nl)},
        compiler_params=pltpu.CompilerParams(has_side_effects=_EFFECT),
    )(*[_in_hbm(a) for a in list(srcs) + list(lands)], *start_after)
    send_sems, recv_sems, token = res[0], res[1], res[-1]
    thru_vals = res[2:-1]

    def wait():
        def wait_body(*refs):
            src_refs, land_refs = refs[:ns], refs[ns:ns + nl]
            s_sems, r_sems = refs[ns + nl], refs[ns + nl + 1]
            for k, (src, _, dst, peer) in enumerate(plan(src_refs, land_refs)):
                cp = pltpu.make_async_remote_copy(src_ref=src, dst_ref=dst, send_sem=s_sems.at[k], recv_sem=r_sems.at[k],
                                                  device_id=peer, device_id_type=MESH)
                cp.wait_send()
                cp.wait_recv()

        out = pl.pallas_call(
            wait_body, name=name + "_wait", out_shape=thru,
            in_specs=[_HBM] * (ns + nl) + [_SEM, _SEM, _ANY], out_specs=[_HBM] * (ns + nl),
            input_output_aliases={i: i for i in range(ns + nl)},
            compiler_params=pltpu.CompilerParams(has_side_effects=_EFFECT),
        )(*thru_vals, send_sems, recv_sems, wait_after())
        return list(out[:ns]), list(out[ns:])

    return wait, token


def _gather_level1_plan(n):
    def plan(src_refs, land_refs):
        me = _position()
        x, y, c = me
        peers = [(x, y, 1 - c), (1 - x, y, c), (x, 1 - y, c), (1 - x, 1 - y, c)]
        return [(src_refs[a], land_refs[a].at[_index(me)], land_refs[a].at[_index(p)], p)
                for a in range(n) for p in peers]
    return plan


def _gather_level2(lands):
    n = len(lands)

    def body(*refs):
        lin = refs[:n]
        send_sems, recv_sems = refs[2 * n:]
        x, y, c = _position()
        sibling = (x, y, 1 - c)
        chips = [(1 - x, y), (x, 1 - y), (1 - x, 1 - y)]

        def copy(a, j, core):
            slot = lin[a].at[_index((*chips[j], core))]
            return pltpu.make_async_remote_copy(
                src_ref=slot, dst_ref=slot, send_sem=send_sems.at[3 * a + j], recv_sem=recv_sems.at[3 * a + j],
                device_id=sibling, device_id_type=MESH)

        sent = [copy(a, j, c) for a in range(n) for j in range(3)]
        for cp in sent:
            cp.start()
        for a in range(n):
            for j in range(3):
                copy(a, j, 1 - c).wait_recv()
        for cp in sent:
            cp.wait_send()

    return pl.pallas_call(
        body, out_shape=[jax.ShapeDtypeStruct(l.shape, l.dtype) for l in lands],
        in_specs=[_ANY] * n, out_specs=[_ANY] * n, input_output_aliases={a: a for a in range(n)},
        scratch_shapes=[pltpu.SemaphoreType.DMA((3 * n,)), pltpu.SemaphoreType.DMA((3 * n,))],
        name="gather_level2")(*lands)


def _gather_direct_plan(n):
    def plan(src_refs, land_refs):
        me = _position()
        peers = [_flip(me, k) for k in range(1, N_DEV)]
        return [(src_refs[a], land_refs[a].at[_index(me)], land_refs[a].at[_index(p)], p)
                for a in range(n) for p in peers]
    return plan


def _place_own(shards):
    me = _index(_position())
    return [lax.dynamic_update_slice(lax.empty((N_DEV,) + s.shape, s.dtype), s[None], (me, 0, 0)) for s in shards]


def _exchange_direct_plan(n):
    def plan(src_refs, land_refs):
        me = _position()
        peers = [_flip(me, k) for k in range(1, N_DEV)]
        return [(src_refs[a].at[_index(p)], land_refs[a].at[_index(me)], land_refs[a].at[_index(p)], p)
                for a in range(n) for p in peers]
    return plan


def _adamw(w, g, m, v):
    m = ADAM_B1 * m + (1.0 - ADAM_B1) * g
    v = ADAM_B2 * v + (1.0 - ADAM_B2) * (g * g)
    m_hat = m / (1.0 - ADAM_B1 ** ADAM_STEP)
    v_hat = v / (1.0 - ADAM_B2 ** ADAM_STEP)
    delta = -ADAM_LR * (m_hat / (jnp.sqrt(v_hat) + ADAM_EPS) + ADAM_WD * w)
    return delta, m, v


def _row_tile(rows, unit, cap=256):
    best = rows
    for t in range(unit, min(rows, cap) + 1, unit):
        if rows % t == 0:
            best = t
    return best


def _sum_partials(name, recv):
    n_src, R, C = recv.shape
    T = _row_tile(R, 16)

    def body(r_ref, g_out):
        g = r_ref[0].astype(F32)
        for s in range(1, n_src):
            g = g + r_ref[s].astype(F32)
        g_out[...] = g

    return pl.pallas_call(
        body, grid=(R // T,), in_specs=[pl.BlockSpec((n_src, T, C), lambda i: (0, i, 0))],
        out_specs=pl.BlockSpec((T, C), lambda i: (i, 0)), out_shape=jax.ShapeDtypeStruct((R, C), F32),
        compiler_params=_params("parallel"), name=name)(recv)


def _sum_adamw(name, recv, w, m, v):
    n_src, R, C = recv.shape
    T = _row_tile(R, 16)

    def body(r_ref, w_ref, m_ref, v_ref, g_out, d_out, m_out, v_out):
        g = r_ref[0].astype(F32)
        for s in range(1, n_src):
            g = g + r_ref[s].astype(F32)
        delta, mn, vn = _adamw(w_ref[...], g, m_ref[...], v_ref[...])
        g_out[...] = g
        d_out[...] = delta
        m_out[...] = mn
        v_out[...] = vn

    blk = pl.BlockSpec((T, C), lambda i: (i, 0))
    return pl.pallas_call(
        body, grid=(R // T,), in_specs=[pl.BlockSpec((n_src, T, C), lambda i: (0, i, 0)), blk, blk, blk],
        out_specs=[blk] * 4, out_shape=[jax.ShapeDtypeStruct((R, C), F32)] * 4,
        compiler_params=_params("parallel"), name=name)(recv, w, m, v)


def _adamw_update(name, w, g, m, v):
    R, C = w.shape
    T = _row_tile(R, 8)

    def body(w_ref, g_ref, m_ref, v_ref, d_out, m_out, v_out):
        delta, mn, vn = _adamw(w_ref[...], g_ref[...], m_ref[...], v_ref[...])
        d_out[...] = delta
        m_out[...] = mn
        v_out[...] = vn

    blk = pl.BlockSpec((T, C), lambda i: (i, 0))
    return pl.pallas_call(
        body, grid=(R // T,), in_specs=[blk] * 4, out_specs=[blk] * 3,
        out_shape=[jax.ShapeDtypeStruct((R, C), F32)] * 3, compiler_params=_params("parallel"), name=name)(w, g, m, v)


def _gather_vmem(p_ref, gath, send_sems, recv_sems):
    me = _position()
    gath[_index(me)] = p_ref[...]
    copies = []
    for k in range(1, N_DEV):
        peer = _flip(me, k)
        copies.append(pltpu.make_async_remote_copy(
            src_ref=p_ref, dst_ref=gath.at[_index(me)], send_sem=send_sems.at[k - 1],
            recv_sem=recv_sems.at[k - 1], device_id=peer, device_id_type=MESH))
    for cp in copies:
        cp.start()
    for k in range(1, N_DEV):
        peer = _flip(me, k)
        pltpu.make_async_remote_copy(
            src_ref=p_ref, dst_ref=gath.at[_index(peer)], send_sem=send_sems.at[k - 1],
            recv_sem=recv_sems.at[k - 1], device_id=peer, device_id_type=MESH).wait_recv()
    for cp in copies:
        cp.wait_send()


def _small_allreduce_adamw(part, w, m, v):
    R, C = part.shape

    def body(p_ref, w_ref, m_ref, v_ref, g_out, d_out, m_out, v_out, gath, send_sems, recv_sems):
        _gather_vmem(p_ref, gath, send_sems, recv_sems)
        g = gath[0]
        for s in range(1, N_DEV):
            g = g + gath[s]
        delta, mn, vn = _adamw(w_ref[...], g, m_ref[...], v_ref[...])
        g_out[...] = g
        d_out[...] = delta
        m_out[...] = mn
        v_out[...] = vn

    vm = pl.BlockSpec(memory_space=pltpu.VMEM)
    return pl.pallas_call(
        body, in_specs=[vm] * 4, out_specs=[vm] * 4, out_shape=[jax.ShapeDtypeStruct((R, C), F32)] * 4,
        scratch_shapes=[pltpu.VMEM((N_DEV, R, C), F32), pltpu.SemaphoreType.DMA((7,)), pltpu.SemaphoreType.DMA((7,))],
        name="small_allreduce_adamw")(part, w, m, v)


_TRANSPOSED = ("w_in", "w_up", "w_pool_out", "w_attn_out")
_SMALL = (("g_mix", 2048), ("b_gate", 4096), ("pool_scale", 1024), ("g_ffn", 2048), ("conv_b", 11264), ("g_final", 2048))
CONV_ROWS = 8


def _to_comm(name, shard):
    return shard.T if name in _TRANSPOSED else shard


def _pack_small(vals):
    flat = [vals[n].reshape(-1) for n, _ in _SMALL]
    used = sum(f.shape[0] for f in flat)
    flat.append(jnp.zeros((SMALL_ROWS * PACK_COLS - used,), F32))
    return jnp.concatenate(flat).reshape(SMALL_ROWS, PACK_COLS)


def _unpack_small(packed, like):
    flat = packed.reshape(-1)
    out, off = {}, 0
    for name, n in _SMALL:
        out[name] = flat[off:off + n].reshape(like[name].shape)
        off += n
    return out


def _local_step(x, target, W, small, first_token=0.0, first_weights=None, late_weights=None, emit=None):
    S = x.shape[0]
    o_qkv, o_gate = POOL_WIDTH, POOL_WIDTH + QKV_WIDTH
    if emit is None:
        emit = lambda grads: None

    def zero(token):
        return 0.0 if token is None else token[0, 0]

    h1 = _rms_fwd("rms1_fwd", x, small["g_mix"] + first_token)
    tok = None
    if first_weights is not None:
        w_first, tok = first_weights(h1)
        small = dict(small, conv_w_full=w_first.pop("conv_w_full"))
        W = dict(W, **w_first)
    w_in_t = W["w_in"]
    u = _mm_nt("proj_u", h1, w_in_t, tm=2048, tn=512, tk=2048, o_dtype=F32, n_rows=POOL_WIDTH, b_row_off=0,
               after=tok)
    qkv = _mm_nt("proj_qkv", h1, w_in_t, tm=2048, tn=512, tk=2048, o_dtype=BF16, n_rows=QKV_WIDTH, b_row_off=o_qkv,
                 after=tok)
    gates_pre = _mm_nt("proj_gates", h1, w_in_t, tm=2048, tn=512, tk=2048, o_dtype=BF16, n_rows=2 * D_MODEL,
                       b_row_off=o_gate, after=tok)
    pool_y = _pool_fwd(u, W["w_pool_lin"], small["pool_scale"])
    outs, lses = zip(*[_attn_fwd(gi, qkv) for gi in range(3)])
    attn_y = _merge_fwd(outs, lses)
    if late_weights is not None:
        W = dict(W, **late_weights(0, attn_y))
    w_up_t = W["w_up"]
    mixed, y_pool, y_attn = _mix_fwd(pool_y, attn_y, W["w_pool_out"], W["w_attn_out"], gates_pre, small["b_gate"])
    x1, h2 = _out_proj_norm(mixed, W["w_out"], x, small["g_ffn"])

    nj = D_FF // 512
    up_pre, act = _up_proj_convffn(h2, w_up_t, small["conv_w_full"], small["conv_b"])
    if late_weights is not None:
        W = dict(W, **late_weights(1, act))
    x2 = _mm_nn("down_proj", act, W["w_down"], tm=1024, tn=1024, tk=D_FF // 2, o_dtype=F32, res=x1)

    dx2, dx2_b, dg_final, sq_cols = _final_loss("final_loss", x2, small["g_final"], target)

    dw_down = _mm_tn("dw_down", act, dx2_b, tm=512, tn=2048, tk=TOKEN_TK)
    tok = emit(dict(w_down=dw_down))
    dup, dconv_w, dconv_b = _convffn_bwd_a(up_pre, dx2_b, W["w_down"], small["conv_w_full"],
                                           small["conv_b"] + zero(tok))
    dup_pre, dh2 = _convffn_bwd_b(dup, small["conv_w_full"], w_up_t)
    dw_up = _mm("dw_up", dup_pre, h2, mode="tn", grid=(2 * nj, 1, S // TOKEN_TK),
                a_spec=pl.BlockSpec((None, TOKEN_TK, 512), lambda i, j, k: (i // nj, k, i % nj)),
                b_spec=pl.BlockSpec((TOKEN_TK, D_MODEL), lambda i, j, k: (k, 0)),
                o_spec=pl.BlockSpec((512, D_MODEL), lambda i, j, k: (i, 0)),
                o_shape=(2 * D_FF, D_MODEL), o_dtype=BF16, acc_shape=(512, D_MODEL))
    tok = emit(dict(w_up=dw_up, conv_w=dconv_w))
    dx1, dx1_b, dg_ffn = _rms_bwd("rms2_bwd", x1, small["g_ffn"] + zero(tok), dh2, dx2, with_bf16=True)

    dw_out = _mm_tn("dw_out", mixed, dx1_b, tm=512, tn=2048, tk=TOKEN_TK)
    tok = emit(dict(w_out=dw_out))
    dy_pool, dy_attn, dgates, db_gate = _mix_bwd(dx1_b, W["w_out"], gates_pre, small["b_gate"] + zero(tok),
                                                 y_pool, y_attn)
    dw_pool_out = _mm_tn("dw_pool_out", dy_pool, pool_y, tm=1024, tn=1024, tk=TOKEN_TK)
    dw_attn_out = _mm_tn("dw_attn_out", dy_attn, attn_y, tm=1024, tn=512, tk=TOKEN_TK)
    e, dw_pool_lin, dpool_scale = _pool_bwd_a(u, dy_pool, W["w_pool_out"], W["w_pool_lin"], small["pool_scale"])
    du = _pool_bwd_b(e)
    dos, dps = _merge_bwd(dy_attn, W["w_attn_out"], outs, lses)
    dqkv = [_attn_bwd(gi, qkv, dos[gi], lses[gi], dps[gi]) for gi in range(3)]
    dproj = jnp.concatenate([du] + [dqkv[gi][sec] for sec in range(3) for gi in range(3)] + [dgates], axis=1)
    dw_in = _mm_tn("dw_in", dproj, h1, tm=512, tn=2048, tk=TOKEN_TK)
    tok = emit(dict(w_in=dw_in, w_pool_out=dw_pool_out, w_attn_out=dw_attn_out, w_pool_lin=dw_pool_lin))
    dh1 = _mm_nn("d_h1", dproj, w_in_t, tm=1024, tn=1024, tk=IN_WIDTH // 4, o_dtype=BF16,
                 after=tok)
    grad_x, dg_mix = _rms_bwd("rms1_bwd", x, small["g_mix"] + zero(tok), dh1, dx1, with_bf16=False)

    grads = dict(w_in=dw_in, w_up=dw_up, w_down=dw_down, w_out=dw_out, w_pool_out=dw_pool_out,
                 w_attn_out=dw_attn_out, w_pool_lin=dw_pool_lin, conv_w=dconv_w)
    small_grads = dict(g_mix=dg_mix, b_gate=db_gate, pool_scale=dpool_scale, g_ffn=dg_ffn, conv_b=dconv_b,
                       g_final=dg_final)
    return sq_cols, grad_x, grads, small_grads


def kernel(x, g_mix, w_in, b_gate, w_pool_lin, pool_scale, w_pool_out, w_attn_out, w_out, g_ffn, w_up, conv_w, conv_b, w_down, g_final, loss_target, m_g_mix, m_w_in, m_b_gate, m_w_pool_lin, m_pool_scale, m_w_pool_out, m_w_attn_out, m_w_out, m_g_ffn, m_w_up, m_conv_w, m_conv_b, m_w_down, m_g_final, v_g_mix, v_w_in, v_b_gate, v_w_pool_lin, v_pool_scale, v_w_pool_out, v_w_attn_out, v_w_out, v_g_ffn, v_w_up, v_conv_w, v_conv_b, v_w_down, v_g_final):
    given = dict(g_mix=g_mix, w_in=w_in, b_gate=b_gate, w_pool_lin=w_pool_lin, pool_scale=pool_scale,
                 w_pool_out=w_pool_out, w_attn_out=w_attn_out, w_out=w_out, g_ffn=g_ffn, w_up=w_up, conv_w=conv_w,
                 conv_b=conv_b, w_down=w_down, g_final=g_final)
    mom_m = dict(g_mix=m_g_mix, w_in=m_w_in, b_gate=m_b_gate, w_pool_lin=m_w_pool_lin, pool_scale=m_pool_scale,
                 w_pool_out=m_w_pool_out, w_attn_out=m_w_attn_out, w_out=m_w_out, g_ffn=m_g_ffn, w_up=m_w_up,
                 conv_w=m_conv_w, conv_b=m_conv_b, w_down=m_w_down, g_final=m_g_final)
    mom_v = dict(g_mix=v_g_mix, w_in=v_w_in, b_gate=v_b_gate, w_pool_lin=v_w_pool_lin, pool_scale=v_pool_scale,
                 w_pool_out=v_w_pool_out, w_attn_out=v_w_attn_out, w_out=v_w_out, g_ffn=v_g_ffn, w_up=v_w_up,
                 conv_w=v_conv_w, conv_b=v_conv_b, w_down=v_w_down, g_final=v_g_final)
    def full(g):
        return g.reshape(N_DEV * g.shape[1], g.shape[2])

    shard = dict(w_in=_to_comm("w_in", w_in[0]).astype(BF16))
    W = {}
    first = [shard["w_in"], w_pool_lin[0].reshape(4 * 32, POOL_GROUP_WIDTH),
             jnp.pad(conv_w[0], ((0, CONV_ROWS - 3), (0, 0)))]
    hold = {}
    w_in_wait, first_token = _split_copies(
        "gather_w_in", first, _place_own(first), 4 * len(first), _gather_level1_plan(len(first)),
        start_after=(), wait_after=lambda: hold["h1"])

    stages = (("w_out", "w_pool_out", "w_attn_out", "w_up"), ("w_down",))
    stage_after = [None, None]
    stage_wait = []
    for n in (n for names in stages for n in names):
        shard[n] = _to_comm(n, given[n][0] + first_token[0, 0]).astype(BF16)

    def first_weights(h1):
        hold["h1"] = h1
        w_in_all, lin_all, conv_w_all = _gather_level2(w_in_wait()[1])
        w_in_full = full(w_in_all)
        lin_full = jnp.moveaxis(lin_all.reshape(N_DEV, 4, 32, POOL_GROUP_WIDTH), 0, 1).reshape(
            4, POOL_GROUP_WIDTH, POOL_GROUP_WIDTH).astype(BF16)
        conv_w_full = jnp.moveaxis(conv_w_all[:, :3, :], 0, 1).reshape(3, 2 * D_FF)
        order_after = (w_in_full,)
        for k, names in enumerate(stages):
            srcs = [shard[n] for n in names]
            wait, token = _split_copies(f"gather_stage{k}", srcs, _place_own(srcs), 7 * len(names),
                                        _gather_direct_plan(len(names)), start_after=order_after,
                                        wait_after=lambda k=k: stage_after[k])
            stage_wait.append(wait)
            order_after = (token,)
        return dict(w_in=w_in_full, w_pool_lin=lin_full, conv_w_full=conv_w_full), token

    def late_weights(stage, after):
        stage_after[stage] = after
        return {n: full(l) for n, l in zip(stages[stage], stage_wait[stage]()[1])}

    small = dict(g_mix=g_mix, b_gate=b_gate, pool_scale=pool_scale, g_ffn=g_ffn, conv_b=conv_b,
                 g_final=g_final.reshape(1, D_MODEL))

    pending = []
    done = {}

    def emit(group):
        names = list(group)
        parts = []
        for n in names:
            g = group[n]
            if n == "w_pool_lin":
                g = jnp.moveaxis(g.reshape(4, N_DEV, 32, POOL_GROUP_WIDTH), 1, 0).reshape(
                    N_DEV, 4 * 32, POOL_GROUP_WIDTH).astype(BF16)
            elif n == "conv_w":
                g = jnp.pad(jnp.moveaxis(g.reshape(3, N_DEV, 2 * D_FF // N_DEV), 1, 0),
                            ((0, 0), (0, CONV_ROWS - 3), (0, 0))).astype(BF16)
            else:
                g = g.reshape(N_DEV, g.shape[0] // N_DEV, g.shape[1])
            parts.append(g)
        me = _index(_position())
        lands = [lax.dynamic_update_slice(lax.empty(p.shape, p.dtype), lax.dynamic_index_in_dim(p, me, 0),
                                          (me, 0, 0)) for p in parts]
        wait, token = _split_copies(
            "exchange_" + names[0], parts, lands, 7 * len(names), _exchange_direct_plan(len(names)),
            start_after=(), wait_after=lambda: done["grad_x"])
        pending.append((names, wait))
        return token

    prepared = sum(shard[n][0, 0].astype(F32) for names in stages for n in names) * 0.0
    sq_cols, grad_x, _, small_grads = _local_step(x[0], loss_target[0], W, small,
                                                  first_token=first_token[0, 0] + prepared,
                                                  first_weights=first_weights, late_weights=late_weights, emit=emit)
    done["grad_x"] = grad_x
    loss = lax.psum(0.5 / D_MODEL * jnp.sum(sq_cols), AXES)

    res = {}
    for n, r in [(n, r) for names, wait in pending for n, r in zip(names, wait()[1])]:
        shape = given[n].shape
        rows = math.prod(shape[:-1])
        w2, m2, v2 = (d[n].reshape(rows, shape[-1]) for d in (given, mom_m, mom_v))
        if n in _TRANSPOSED or n == "conv_w":
            g = _sum_partials("sum_" + n, r)
            g = g.T if n in _TRANSPOSED else g[:3]
            upd = (g,) + tuple(_adamw_update("adamw_" + n, w2, g, m2, v2))
        else:
            upd = _sum_adamw("adamw_" + n, r, w2, m2, v2)
        res[n] = [a.reshape(shape) for a in upd]

    packed_small = [_pack_small(d) for d in (given, mom_m, mom_v)]
    res_small = [_unpack_small(p, given) for p in _small_allreduce_adamw(_pack_small(small_grads), *packed_small)]

    order = ["g_mix", "w_in", "b_gate", "w_pool_lin", "pool_scale", "w_pool_out", "w_attn_out", "w_out", "g_ffn",
             "w_up", "conv_w", "conv_b", "w_down", "g_final"]
    outs = [loss, grad_x[None]]
    for kind in range(4):
        for n in order:
            outs.append(res[n][kind] if n in res else res_small[kind][n])
    return tuple(outs)
```

```python
import math

import jax
import jax.numpy as jnp
from jax import lax
from jax.experimental import pallas as pl
from jax.experimental.pallas import tpu as pltpu

F32 = jnp.float32
BF16 = jnp.bfloat16
MESH = pl.DeviceIdType.MESH
N_DEV = 8

D_MODEL = 2048
POOL_WINDOWS = (2, 4, 8, 16)
POOL_GROUP_WIDTH = 256
POOL_WIDTH = 1024
ATTN_GROUPS = ((128, 1), (512, 4), (2048, 16))
SPAN = 128
HEADS_PER_GROUP = 4
N_ATTN_HEADS = 12
HEAD_DIM = 128
ATTN_WIDTH = 1536
QKV_WIDTH = 3 * ATTN_WIDTH
ATTN_OUT_WIDTH = 512
IN_WIDTH = 9728
D_FF = 5632
RMS_EPS = 1e-6
ADAM_LR, ADAM_B1, ADAM_B2, ADAM_EPS, ADAM_WD, ADAM_STEP = 0.001, 0.9, 0.999, 1e-08, 0.01, 10

VMEM_LIMIT_BYTES = 48 * 1024 * 1024
HALO = 16
TOKEN_TK = 2048
MXU_DIM = 256
PACK_COLS = 1024
SMALL_ROWS = 24
NEG_BIG = -1e30

_DN = {"nn": (((1,), (0,)), ((), ())), "nt": (((1,), (1,)), ((), ())), "tn": (((0,), (0,)), ((), ()))}


def _params(*sem):
    return pltpu.CompilerParams(dimension_semantics=sem, vmem_limit_bytes=VMEM_LIMIT_BYTES)


def _dot(a, b, mode):
    return lax.dot_general(a.astype(BF16), b.astype(BF16), _DN[mode], preferred_element_type=F32)


def _mm(name, a, b, *, mode, grid, a_spec, b_spec, o_spec, o_shape, o_dtype, acc_shape, res=None, res_spec=None,
        after=None):
    nk = grid[2]
    n_in = 2 + (res is not None) + (after is not None)

    def body(*refs):
        a_ref, b_ref = refs[:2]
        r_ref = refs[2] if res is not None else None
        o_ref = refs[n_in]
        scr = refs[n_in + 1:]
        prod = _dot(a_ref[...], b_ref[...], mode)

        def finish(val):
            if r_ref is not None:
                val = val + r_ref[...]
            o_ref[...] = val.astype(o_ref.dtype)

        if nk == 1:
            finish(prod)
        else:
            acc = scr[0]
            k = pl.program_id(2)

            @pl.when(k == 0)
            def _():
                acc[...] = prod

            @pl.when(k > 0)
            def _():
                acc[...] += prod

            @pl.when(k == nk - 1)
            def _():
                finish(acc[...])

    in_specs = [a_spec, b_spec]
    args = [a, b]
    if res is not None:
        in_specs.append(res_spec)
        args.append(res)
    if after is not None:
        in_specs.append(pl.BlockSpec((8, 128), lambda i, j, k: (0, 0)))
        args.append(after)
    return pl.pallas_call(
        body, grid=grid, in_specs=in_specs, out_specs=o_spec,
        out_shape=jax.ShapeDtypeStruct(o_shape, o_dtype),
        scratch_shapes=[] if nk == 1 else [pltpu.VMEM(acc_shape, F32)],
        compiler_params=_params("parallel", "parallel", "arbitrary"), name=name)(*args)


def _mm_nn(name, a, b, *, tm, tn, tk, o_dtype, res=None, after=None):
    M, K = a.shape
    N = b.shape[1]
    return _mm(name, a, b, mode="nn", grid=(M // tm, N // tn, K // tk),
               a_spec=pl.BlockSpec((tm, tk), lambda i, j, k: (i, k)),
               b_spec=pl.BlockSpec((tk, tn), lambda i, j, k: (k, j)),
               o_spec=pl.BlockSpec((tm, tn), lambda i, j, k: (i, j)),
               o_shape=(M, N), o_dtype=o_dtype, acc_shape=(tm, tn), res=res,
               res_spec=pl.BlockSpec((tm, tn), lambda i, j, k: (i, j)), after=after)


def _mm_nt(name, a, b, *, tm, tn, tk, o_dtype, n_rows=None, b_row_off=0, after=None):
    M, K = a.shape
    N = b.shape[0] if n_rows is None else n_rows
    off = b_row_off // tn
    return _mm(name, a, b, mode="nt", grid=(M // tm, N // tn, K // tk),
               a_spec=pl.BlockSpec((tm, tk), lambda i, j, k: (i, k)),
               b_spec=pl.BlockSpec((tn, tk), lambda i, j, k: (j + off, k)),
               o_spec=pl.BlockSpec((tm, tn), lambda i, j, k: (i, j)),
               o_shape=(M, N), o_dtype=o_dtype, acc_shape=(tm, tn), after=after)


def _mm_tn(name, a, b, *, tm, tn, tk):
    K, M = a.shape
    N = b.shape[1]
    return _mm(name, a, b, mode="tn", grid=(M // tm, N // tn, K // tk),
               a_spec=pl.BlockSpec((tk, tm), lambda i, j, k: (k, i)),
               b_spec=pl.BlockSpec((tk, tn), lambda i, j, k: (k, j)),
               o_spec=pl.BlockSpec((tm, tn), lambda i, j, k: (i, j)),
               o_shape=(M, N), o_dtype=BF16, acc_shape=(tm, tn))


def _rms_fwd(name, x, g):
    S, D = x.shape
    T = 512

    def body(x_ref, g_ref, h_ref):
        xv = x_ref[...]
        r = lax.rsqrt(jnp.mean(xv * xv, axis=-1, keepdims=True) + RMS_EPS)
        h_ref[...] = (xv * r * g_ref[...]).astype(BF16)

    return pl.pallas_call(
        body, grid=(S // T,),
        in_specs=[pl.BlockSpec((T, D), lambda i: (i, 0)), pl.BlockSpec((1, D), lambda i: (0, 0))],
        out_specs=pl.BlockSpec((T, D), lambda i: (i, 0)),
        out_shape=jax.ShapeDtypeStruct((S, D), BF16), compiler_params=_params("parallel"), name=name)(x, g)


def _out_proj_norm(mixed, w_out, x, g):
    S, D = x.shape
    tm = 512

    def body(m_ref, w_ref, x_ref, g_ref, x1_ref, h_ref):
        x1 = x_ref[...] + _dot(m_ref[...], w_ref[...], "nn")
        x1_ref[...] = x1
        r = lax.rsqrt(jnp.mean(x1 * x1, axis=-1, keepdims=True) + RMS_EPS)
        h_ref[...] = (x1 * r * g_ref[...]).astype(BF16)

    row = pl.BlockSpec((tm, D), lambda i: (i, 0))
    return pl.pallas_call(
        body, grid=(S // tm,),
        in_specs=[row, pl.BlockSpec((D, D), lambda i: (0, 0)), row, pl.BlockSpec((1, D), lambda i: (0, 0))],
        out_specs=[row, row], out_shape=[jax.ShapeDtypeStruct((S, D), F32), jax.ShapeDtypeStruct((S, D), BF16)],
        compiler_params=_params("parallel"), name="out_proj_norm")(mixed, w_out, x, g)


def _rms_bwd(name, x, g, dh, dres, with_bf16):
    S, D = x.shape
    T = 256

    def body(x_ref, g_ref, dh_ref, dres_ref, dx_ref, *rest):
        dg_ref = rest[-1]
        i = pl.program_id(0)
        xv = x_ref[...]
        r = lax.rsqrt(jnp.mean(xv * xv, axis=-1, keepdims=True) + RMS_EPS)
        xhat = xv * r
        dhv = dh_ref[...].astype(F32)
        gdh = dhv * g_ref[...]
        c = jnp.mean(xhat * gdh, axis=-1, keepdims=True)
        dx = dres_ref[...] + r * (gdh - xhat * c)
        dx_ref[...] = dx
        if with_bf16:
            rest[0][...] = dx.astype(BF16)
        part = jnp.sum(dhv * xhat, axis=0, keepdims=True)

        @pl.when(i == 0)
        def _():
            dg_ref[...] = part

        @pl.when(i > 0)
        def _():
            dg_ref[...] += part

    row = pl.BlockSpec((T, D), lambda i: (i, 0))
    vec = pl.BlockSpec((1, D), lambda i: (0, 0))
    narrow = [jax.ShapeDtypeStruct((S, D), BF16)] if with_bf16 else []
    return pl.pallas_call(
        body, grid=(S // T,), in_specs=[row, vec, row, row], out_specs=[row] * (1 + len(narrow)) + [vec],
        out_shape=[jax.ShapeDtypeStruct((S, D), F32)] + narrow + [jax.ShapeDtypeStruct((1, D), F32)],
        compiler_params=_params("arbitrary"), name=name)(x, g, dh, dres)


def _final_loss(name, x, g, target):
    S, D = x.shape
    T = 256

    def body(x_ref, g_ref, t_ref, dx_ref, dxb_ref, dg_ref, sq_ref):
        i = pl.program_id(0)
        xv = x_ref[...]
        r = lax.rsqrt(jnp.mean(xv * xv, axis=-1, keepdims=True) + RMS_EPS)
        xhat = xv * r
        err = xhat * g_ref[...] - t_ref[...]
        dy = err * (1.0 / D)
        gdy = dy * g_ref[...]
        c = jnp.mean(xhat * gdy, axis=-1, keepdims=True)
        dx = r * (gdy - xhat * c)
        dx_ref[...] = dx
        dxb_ref[...] = dx.astype(BF16)
        dg_part = jnp.sum(dy * xhat, axis=0, keepdims=True)
        sq_part = jnp.sum(err * err, axis=0, keepdims=True)

        @pl.when(i == 0)
        def _():
            dg_ref[...] = dg_part
            sq_ref[...] = sq_part

        @pl.when(i > 0)
        def _():
            dg_ref[...] += dg_part
            sq_ref[...] += sq_part

    row = pl.BlockSpec((T, D), lambda i: (i, 0))
    vec = pl.BlockSpec((1, D), lambda i: (0, 0))
    return pl.pallas_call(
        body, grid=(S // T,), in_specs=[row, vec, row], out_specs=[row, row, vec, vec],
        out_shape=[jax.ShapeDtypeStruct((S, D), F32), jax.ShapeDtypeStruct((S, D), BF16),
                   jax.ShapeDtypeStruct((1, D), F32), jax.ShapeDtypeStruct((1, D), F32)],
        compiler_params=_params("arbitrary"), name=name)(x, g, target)


def _prev_halo_spec(T, C, col_map):
    per = T // HALO
    return pl.BlockSpec((HALO, C), lambda *g: (jnp.maximum(g[0] * per - 1, 0), col_map(*g)))


def _pool_window_sums(ext, T, i):
    t = i * T + lax.broadcasted_iota(jnp.int32, (T, 1), 0)
    outs = []
    cnts = []
    for gi, w in enumerate(POOL_WINDOWS):
        s = ext[:, gi * POOL_GROUP_WIDTH:(gi + 1) * POOL_GROUP_WIDTH]
        sh = 1
        while sh < w:
            s = s + pltpu.roll(s, sh, 0)
            sh *= 2
        cnt = jnp.minimum(t + 1, w).astype(F32)
        outs.append(s[HALO:] / cnt - ext[HALO:, gi * POOL_GROUP_WIDTH:(gi + 1) * POOL_GROUP_WIDTH])
        cnts.append(cnt)
    return outs, cnts


def _load_ext(ext_ref, halo_ref, blk_ref, first):
    ext_ref[0:HALO, :] = jnp.where(first, 0.0, halo_ref[...].astype(F32))
    ext_ref[HALO:, :] = blk_ref[...].astype(F32)


def _pool_fwd(u, w_lin, scale):
    S = u.shape[0]
    T = 512
    W = POOL_WIDTH

    def body(u_ref, halo_ref, wl_ref, sc_ref, y_ref, ext_ref):
        i = pl.program_id(0)
        _load_ext(ext_ref, halo_ref, u_ref, i == 0)
        pooled, _ = _pool_window_sums(ext_ref[...], T, i)
        for gi in range(4):
            z = _dot(pooled[gi], wl_ref[gi], "nn")
            cs = slice(gi * POOL_GROUP_WIDTH, (gi + 1) * POOL_GROUP_WIDTH)
            y_ref[:, cs] = (z * sc_ref[:, cs]).astype(BF16)

    return pl.pallas_call(
        body, grid=(S // T,),
        in_specs=[pl.BlockSpec((T, W), lambda i: (i, 0)), _prev_halo_spec(T, W, lambda i: 0),
                  pl.BlockSpec((4, 256, 256), lambda i: (0, 0, 0)), pl.BlockSpec((1, W), lambda i: (0, 0))],
        out_specs=pl.BlockSpec((T, W), lambda i: (i, 0)),
        out_shape=jax.ShapeDtypeStruct((S, W), BF16),
        scratch_shapes=[pltpu.VMEM((HALO + T, W), F32)],
        compiler_params=_params("parallel"), name="pool_fwd")(u, u, w_lin, scale)


def _pool_bwd_a(u, dy_out, w_pool_out_t, w_lin, scale):
    S = u.shape[0]
    T = 512
    W = POOL_WIDTH

    def body(u_ref, halo_ref, dyo_ref, wpo_ref, wl_ref, sc_ref, e_ref, dwl_ref, dsc_ref, ext_ref):
        i = pl.program_id(0)
        dy = _dot(dyo_ref[...], wpo_ref[...], "nn")
        _load_ext(ext_ref, halo_ref, u_ref, i == 0)
        pooled, cnts = _pool_window_sums(ext_ref[...], T, i)
        for gi in range(4):
            cs = slice(gi * POOL_GROUP_WIDTH, (gi + 1) * POOL_GROUP_WIDTH)
            z = _dot(pooled[gi], wl_ref[gi], "nn")
            dyg = dy[:, cs]
            dsc = jnp.sum(dyg * z, axis=0, keepdims=True)
            dz = dyg * sc_ref[:, cs]
            dwl = _dot(pooled[gi], dz, "tn")
            dpooled = _dot(dz, wl_ref[gi], "nt")
            e_ref[:, cs] = dpooled / cnts[gi]

            @pl.when(i == 0)
            def _():
                dwl_ref[gi] = dwl
                dsc_ref[:, cs] = dsc

            @pl.when(i > 0)
            def _():
                dwl_ref[gi] += dwl
                dsc_ref[:, cs] += dsc

    return pl.pallas_call(
        body, grid=(S // T,),
        in_specs=[pl.BlockSpec((T, W), lambda i: (i, 0)), _prev_halo_spec(T, W, lambda i: 0),
                  pl.BlockSpec((T, D_MODEL), lambda i: (i, 0)), pl.BlockSpec((D_MODEL, W), lambda i: (0, 0)),
                  pl.BlockSpec((4, 256, 256), lambda i: (0, 0, 0)), pl.BlockSpec((1, W), lambda i: (0, 0))],
        out_specs=[pl.BlockSpec((T, W), lambda i: (i, 0)), pl.BlockSpec((4, 256, 256), lambda i: (0, 0, 0)),
                   pl.BlockSpec((1, W), lambda i: (0, 0))],
        out_shape=[jax.ShapeDtypeStruct((S, W), F32), jax.ShapeDtypeStruct((4, 256, 256), F32),
                   jax.ShapeDtypeStruct((1, W), F32)],
        scratch_shapes=[pltpu.VMEM((HALO + T, W), F32)],
        compiler_params=_params("arbitrary"), name="pool_bwd_a")(u, u, dy_out, w_pool_out_t, w_lin, scale)


def _pool_bwd_b(e):
    S = e.shape[0]
    T = 512
    W = POOL_WIDTH
    per = T // HALO
    n_blk = S // T

    def body(e_ref, halo_ref, du_ref, ext_ref):
        i = pl.program_id(0)
        ext_ref[0:T, :] = e_ref[...]
        ext_ref[T:, :] = jnp.where(i == n_blk - 1, 0.0, halo_ref[...])
        ext = ext_ref[...]
        n = T + HALO
        t = i * T + lax.broadcasted_iota(jnp.int32, (T, 1), 0)
        for gi, w in enumerate(POOL_WINDOWS):
            cs = slice(gi * POOL_GROUP_WIDTH, (gi + 1) * POOL_GROUP_WIDTH)
            s = ext[:, cs]
            sh = 1
            while sh < w:
                s = s + pltpu.roll(s, n - sh, 0)
                sh *= 2
            cnt = jnp.minimum(t + 1, w).astype(F32)
            du_ref[:, cs] = (s[0:T] - ext[0:T, cs] * cnt).astype(BF16)

    return pl.pallas_call(
        body, grid=(n_blk,),
        in_specs=[pl.BlockSpec((T, W), lambda i: (i, 0)),
                  pl.BlockSpec((HALO, W), lambda i: (jnp.minimum((i + 1) * per, S // HALO - 1), 0))],
        out_specs=pl.BlockSpec((T, W), lambda i: (i, 0)),
        out_shape=jax.ShapeDtypeStruct((S, W), BF16),
        scratch_shapes=[pltpu.VMEM((T + HALO, W), F32)],
        compiler_params=_params("parallel"), name="pool_bwd_b")(e, e)


def _slope(head):
    return 2.0 ** (-8.0 * (head + 1) / N_ATTN_HEADS)


def _attn_masks():
    row = lax.broadcasted_iota(jnp.int32, (SPAN, SPAN), 0)
    col = lax.broadcasted_iota(jnp.int32, (SPAN, SPAN), 1)
    dist_cur = (row - col).astype(F32)
    return dist_cur, col <= row, col >= row


ATTN_BATCH = 16


def _attn_tiling(gi, S):
    d = ATTN_GROUPS[gi][1]
    hp = HEADS_PER_GROUP if d < 16 else 1
    nblk = ATTN_BATCH // hp if d == 1 else 1
    C = SPAN * d * nblk
    return d, C, S // C, hp, hp * HEAD_DIM, min(ATTN_BATCH // (hp * nblk), d), nblk


_BATCHED_DN = {"nt": (((2,), (2,)), ((0,), (0,))), "nn": (((2,), (1,)), ((0,), (0,))), "tn": (((1,), (1,)), ((0,), (0,)))}


def _bdot(a, b, mode):
    return lax.dot_general(a.astype(BF16), b.astype(BF16), _BATCHED_DN[mode], preferred_element_type=F32)


def _block_rows(blk, r, d):
    return pl.ds(blk * SPAN * d + r, SPAN, stride=d)


def _take(scr, hp, it, d, rb, nblk, shift=0, other=None):
    out = []
    for hi in range(hp):
        for blk in range(nblk):
            inside = 0 <= blk + shift < nblk
            src, src_blk = (scr, blk + shift) if inside else (other, nblk - 1 if shift < 0 else 0)
            out += [src[hi, _block_rows(src_blk, it * rb + j, d), :] for j in range(rb)]
    return jnp.stack(out)


def _put(scr, hp, it, d, rb, nblk, val):
    n = 0
    for hi in range(hp):
        for blk in range(nblk):
            for j in range(rb):
                scr[hi, _block_rows(blk, it * rb + j, d), :] = val[n]
                n += 1


def _batch_bias(gi, d, hp, rb, nblk, dist, valid, edge_blk=None, edge_ok=None):
    out = []
    for hi in range(hp):
        bias = (-(_head_slope(gi, hp, hi) * d)) * dist
        for blk in range(nblk):
            ok = jnp.logical_and(valid, edge_ok) if blk == edge_blk else valid
            out += [jnp.where(ok, bias, NEG_BIG)] * rb
    return jnp.stack(out)


def _head_slope(gi, hp, hi):
    if hp == HEADS_PER_GROUP:
        return _slope(gi * HEADS_PER_GROUP + hi)
    first = pl.program_id(1) * hp
    sl = jnp.float32(_slope(gi * HEADS_PER_GROUP + hi))
    for h0 in range(hp, HEADS_PER_GROUP, hp):
        sl = jnp.where(first == h0, jnp.float32(_slope(gi * HEADS_PER_GROUP + h0 + hi)), sl)
    return sl


def _split_heads(src_ref, dst_ref, hp):
    for hi in range(hp):
        dst_ref[hi] = src_ref[:, hi * HEAD_DIM:(hi + 1) * HEAD_DIM].astype(F32)


def _attn_fwd(gi, qkv):
    S = qkv.shape[0]
    d, C, nb, hp, W, rb, nblk = _attn_tiling(gi, S)
    scale = HEAD_DIM ** -0.5

    def body(q_ref, kc_ref, kp_ref, vc_ref, vp_ref, o_ref, lse_ref, qs, kcs, kps, vcs, vps, os, ls):
        b = pl.program_id(0)
        for src, dst in ((q_ref, qs), (kc_ref, kcs), (kp_ref, kps), (vc_ref, vcs), (vp_ref, vps)):
            _split_heads(src, dst, hp)
        dist_cur, valid_cur, valid_near = _attn_masks()
        bias_cur = _batch_bias(gi, d, hp, rb, nblk, dist_cur, valid_cur)
        bias_prev = _batch_bias(gi, d, hp, rb, nblk, dist_cur + SPAN, valid_near, edge_blk=0, edge_ok=b > 0)
        at = (d, rb, nblk)

        def sub_sequences(it, carry):
            q = _take(qs, hp, it, *at)
            s_cur = _bdot(q, _take(kcs, hp, it, *at), "nt") * scale + bias_cur
            s_prev = _bdot(q, _take(kcs, hp, it, *at, shift=-1, other=kps), "nt") * scale + bias_prev
            m = jnp.maximum(jnp.max(s_cur, axis=-1, keepdims=True), jnp.max(s_prev, axis=-1, keepdims=True))
            p_cur = jnp.exp(s_cur - m)
            p_prev = jnp.exp(s_prev - m)
            l = jnp.sum(p_cur, axis=-1, keepdims=True) + jnp.sum(p_prev, axis=-1, keepdims=True)
            o = (_bdot(p_cur, _take(vcs, hp, it, *at), "nn")
                 + _bdot(p_prev, _take(vcs, hp, it, *at, shift=-1, other=vps), "nn")) / l
            _put(os, hp, it, *at, o)
            _put(ls, hp, it, *at, jnp.broadcast_to(m + jnp.log(l), o.shape))
            return carry

        lax.fori_loop(0, d // rb, sub_sequences, 0)
        for hi in range(hp):
            hs = slice(hi * HEAD_DIM, (hi + 1) * HEAD_DIM)
            o_ref[:, hs] = os[hi]
            lse_ref[:, hs] = ls[hi]

    per_sec = ATTN_WIDTH // W
    per_grp = ATTN_OUT_WIDTH // W

    def spec(sec, shift):
        return pl.BlockSpec((C, W), lambda b, h: (jnp.maximum(b + shift, 0), sec * per_sec + gi * per_grp + h))

    out = pl.BlockSpec((C, W), lambda b, h: (b, h))
    scr = pltpu.VMEM((hp, C, HEAD_DIM), F32)
    return pl.pallas_call(
        body, grid=(nb, HEADS_PER_GROUP // hp),
        in_specs=[spec(0, 0), spec(1, 0), spec(1, -1), spec(2, 0), spec(2, -1)], out_specs=[out, out],
        out_shape=[jax.ShapeDtypeStruct((S, ATTN_OUT_WIDTH), F32)] * 2, scratch_shapes=[scr] * 7,
        compiler_params=_params("parallel", "parallel"), name=f"attn_fwd_g{gi}")(qkv, qkv, qkv, qkv, qkv)


def _attn_bwd(gi, qkv, do, lse, dp):
    S = qkv.shape[0]
    d, C, nb, hp, W, rb, nblk = _attn_tiling(gi, S)
    scale = HEAD_DIM ** -0.5
    n_in = 12

    def body(*refs):
        ins, (dq_ref, dk_ref, dv_ref) = refs[:n_in], refs[n_in:n_in + 3]
        scr = refs[n_in + 3:]
        for src, dst in zip(ins, scr[:n_in]):
            _split_heads(src, dst, hp)
        qs, kcs, vcs, kps, vps, qns, dos, dons, lses, lsens, dps, dpns = scr[:n_in]
        dqs, dks, dvs = scr[n_in:]
        b = pl.program_id(0)
        dist_cur, valid_cur, valid_near = _attn_masks()
        bias_cur = _batch_bias(gi, d, hp, rb, nblk, dist_cur, valid_cur)
        bias_prev = _batch_bias(gi, d, hp, rb, nblk, dist_cur + SPAN, valid_near, edge_blk=0, edge_ok=b > 0)
        bias_next = _batch_bias(gi, d, hp, rb, nblk, dist_cur + SPAN, valid_near, edge_blk=nblk - 1,
                                edge_ok=b < nb - 1)
        at = (d, rb, nblk)

        def sub_sequences(it, carry):
            q, kc, vc, dov, lse_b, dp_b = (_take(s, hp, it, *at) for s in (qs, kcs, vcs, dos, lses, dps))
            kp = _take(kcs, hp, it, *at, shift=-1, other=kps)
            vp = _take(vcs, hp, it, *at, shift=-1, other=vps)
            qn, don, lse_n, dp_n = (_take(s, hp, it, *at, shift=1, other=o)
                                    for s, o in ((qs, qns), (dos, dons), (lses, lsens), (dps, dpns)))
            p_cur = jnp.exp(_bdot(q, kc, "nt") * scale + bias_cur - lse_b)
            p_prev = jnp.exp(_bdot(q, kp, "nt") * scale + bias_prev - lse_b)
            ds_cur = p_cur * (_bdot(dov, vc, "nt") - dp_b)
            ds_prev = p_prev * (_bdot(dov, vp, "nt") - dp_b)
            _put(dqs, hp, it, *at, (_bdot(ds_cur, kc, "nn") + _bdot(ds_prev, kp, "nn")) * scale)
            p_next = jnp.exp(_bdot(qn, kc, "nt") * scale + bias_next - lse_n)
            ds_next = p_next * (_bdot(don, vc, "nt") - dp_n)
            _put(dvs, hp, it, *at, _bdot(p_cur, dov, "tn") + _bdot(p_next, don, "tn"))
            _put(dks, hp, it, *at, (_bdot(ds_cur, q, "tn") + _bdot(ds_next, qn, "tn")) * scale)
            return carry

        lax.fori_loop(0, d // rb, sub_sequences, 0)
        for hi in range(hp):
            hs = slice(hi * HEAD_DIM, (hi + 1) * HEAD_DIM)
            dq_ref[:, hs] = dqs[hi].astype(BF16)
            dk_ref[:, hs] = dks[hi].astype(BF16)
            dv_ref[:, hs] = dvs[hi].astype(BF16)

    per_sec = ATTN_WIDTH // W
    per_grp = ATTN_OUT_WIDTH // W

    def qkv_spec(sec, shift):
        return pl.BlockSpec((C, W), lambda b, h: (jnp.clip(b + shift, 0, nb - 1), sec * per_sec + gi * per_grp + h))

    def act_spec(shift):
        return pl.BlockSpec((C, W), lambda b, h: (jnp.clip(b + shift, 0, nb - 1), h))

    out = act_spec(0)
    scr = pltpu.VMEM((hp, C, HEAD_DIM), F32)
    return pl.pallas_call(
        body, grid=(nb, HEADS_PER_GROUP // hp),
        in_specs=[qkv_spec(0, 0), qkv_spec(1, 0), qkv_spec(2, 0), qkv_spec(1, -1), qkv_spec(2, -1), qkv_spec(0, 1),
                  act_spec(0), act_spec(1), act_spec(0), act_spec(1), act_spec(0), act_spec(1)],
        out_specs=[out, out, out], out_shape=[jax.ShapeDtypeStruct((S, ATTN_OUT_WIDTH), BF16)] * 3,
        scratch_shapes=[scr] * (n_in + 3),
        compiler_params=_params("parallel", "parallel"), name=f"attn_bwd_g{gi}")(
            qkv, qkv, qkv, qkv, qkv, qkv, do, do, lse, lse, dp, dp)


def _group_weights(l_refs):
    l0, l1, l2 = (r[...] for r in l_refs)
    m = jnp.maximum(jnp.maximum(l0, l1), l2)
    e = [jnp.exp(l0 - m), jnp.exp(l1 - m), jnp.exp(l2 - m)]
    tot = e[0] + e[1] + e[2]
    return [ei / tot for ei in e]


def _merge_fwd(outs, lses):
    S, W = outs[0].shape
    T = 512

    def body(o0, o1, o2, l0, l1, l2, y_ref):
        w = _group_weights((l0, l1, l2))
        y_ref[...] = (w[0] * o0[...] + w[1] * o1[...] + w[2] * o2[...]).astype(BF16)

    blk = pl.BlockSpec((T, W), lambda i: (i, 0))
    return pl.pallas_call(
        body, grid=(S // T,), in_specs=[blk] * 6, out_specs=blk, out_shape=jax.ShapeDtypeStruct((S, W), BF16),
        compiler_params=_params("parallel"), name="merge_fwd")(*outs, *lses)


def _merge_bwd(dy_out, w_attn_out_t, outs, lses):
    S, W = outs[0].shape
    T = 512

    def body(dyo_ref, wao_ref, o0, o1, o2, l0, l1, l2, do0, do1, do2, dp0, dp1, dp2):
        w = _group_weights((l0, l1, l2))
        dyv = _dot(dyo_ref[...], wao_ref[...], "nn")
        y = w[0] * o0[...] + w[1] * o1[...] + w[2] * o2[...]
        prod = dyv * y
        dots = []
        for hi in range(HEADS_PER_GROUP):
            hs = slice(hi * HEAD_DIM, (hi + 1) * HEAD_DIM)
            dots.append(jnp.broadcast_to(jnp.sum(prod[:, hs], axis=-1, keepdims=True), (T, HEAD_DIM)))
        dot_b = jnp.concatenate(dots, axis=1)
        for wg, do_ref, dp_ref in zip(w, (do0, do1, do2), (dp0, dp1, dp2)):
            do_ref[...] = (wg * dyv).astype(BF16)
            dp_ref[...] = wg * dot_b

    blk = pl.BlockSpec((T, W), lambda i: (i, 0))
    res = pl.pallas_call(
        body, grid=(S // T,),
        in_specs=[pl.BlockSpec((T, D_MODEL), lambda i: (i, 0)), pl.BlockSpec((D_MODEL, W), lambda i: (0, 0))] + [blk] * 6,
        out_specs=[blk] * 6,
        out_shape=[jax.ShapeDtypeStruct((S, W), BF16)] * 3 + [jax.ShapeDtypeStruct((S, W), F32)] * 3,
        compiler_params=_params("parallel"), name="merge_bwd")(dy_out, w_attn_out_t, *outs, *lses)
    return res[:3], res[3:]


def _mix_fwd(pool_y, attn_y, w_pool_out_t, w_attn_out_t, gates_pre, b_gate):
    S = pool_y.shape[0]
    D = D_MODEL
    tm, tn = 1024, 512
    nj = D // tn

    def body(p_ref, a_ref, wp_ref, wa_ref, g0_ref, g1_ref, b0_ref, b1_ref, mixed_ref, yp_ref, ya_ref):
        yp = _dot(p_ref[...], wp_ref[...], "nt")
        ya = _dot(a_ref[...], wa_ref[...], "nt")
        g0 = jax.nn.sigmoid(g0_ref[...] + b0_ref[...])
        g1 = jax.nn.sigmoid(g1_ref[...] + b1_ref[...])
        mixed_ref[...] = (g0 * yp + g1 * ya).astype(BF16)
        yp_ref[...] = yp.astype(BF16)
        ya_ref[...] = ya.astype(BF16)

    out = pl.BlockSpec((tm, tn), lambda i, j: (i, j))
    return pl.pallas_call(
        body, grid=(S // tm, nj),
        in_specs=[pl.BlockSpec((tm, POOL_WIDTH), lambda i, j: (i, 0)),
                  pl.BlockSpec((tm, ATTN_OUT_WIDTH), lambda i, j: (i, 0)),
                  pl.BlockSpec((tn, POOL_WIDTH), lambda i, j: (j, 0)),
                  pl.BlockSpec((tn, ATTN_OUT_WIDTH), lambda i, j: (j, 0)),
                  pl.BlockSpec((tm, tn), lambda i, j: (i, j)), pl.BlockSpec((tm, tn), lambda i, j: (i, j + nj)),
                  pl.BlockSpec((1, tn), lambda i, j: (0, j)), pl.BlockSpec((1, tn), lambda i, j: (0, j + nj))],
        out_specs=[out, out, out], out_shape=[jax.ShapeDtypeStruct((S, D), BF16)] * 3,
        compiler_params=_params("parallel", "parallel"), name="mix_fwd")(
            pool_y, attn_y, w_pool_out_t, w_attn_out_t, gates_pre, gates_pre, b_gate, b_gate)


def _mix_bwd(dx, w_out, gates_pre, b_gate, y_pool, y_attn):
    S = dx.shape[0]
    D = D_MODEL
    T = 256

    def body(dx_ref, w_ref, gp_ref, b_ref, yp_ref, ya_ref, dyp_ref, dya_ref, dg_ref, db_ref):
        i = pl.program_id(0)
        dm = _dot(dx_ref[...], w_ref[...], "nt")
        parts = []
        for br, (y_ref, dy_ref) in enumerate(((yp_ref, dyp_ref), (ya_ref, dya_ref))):
            cs = slice(br * D, (br + 1) * D)
            g = jax.nn.sigmoid(gp_ref[:, cs] + b_ref[:, cs])
            dy_ref[...] = (dm * g).astype(BF16)
            dpre = dm * y_ref[...].astype(F32) * g * (1.0 - g)
            dg_ref[:, cs] = dpre.astype(BF16)
            parts.append(jnp.sum(dpre, axis=0, keepdims=True))

        @pl.when(i == 0)
        def _():
            db_ref[:, 0:D] = parts[0]
            db_ref[:, D:2 * D] = parts[1]

        @pl.when(i > 0)
        def _():
            db_ref[:, 0:D] += parts[0]
            db_ref[:, D:2 * D] += parts[1]

    row = pl.BlockSpec((T, D), lambda i: (i, 0))
    row2 = pl.BlockSpec((T, 2 * D), lambda i: (i, 0))
    vec2 = pl.BlockSpec((1, 2 * D), lambda i: (0, 0))
    return pl.pallas_call(
        body, grid=(S // T,), in_specs=[row, pl.BlockSpec((D, D), lambda i: (0, 0)), row2, vec2, row, row],
        out_specs=[row, row, row2, vec2],
        out_shape=[jax.ShapeDtypeStruct((S, D), BF16), jax.ShapeDtypeStruct((S, D), BF16),
                   jax.ShapeDtypeStruct((S, 2 * D), BF16), jax.ShapeDtypeStruct((1, 2 * D), F32)],
        compiler_params=_params("arbitrary"), name="mix_bwd")(dx, w_out, gates_pre, b_gate, y_pool, y_attn)


_CONV_T, _CONV_C = 512, 512


def _conv_taps(ext_ref, h, w_ref, b_ref, cols=slice(None)):
    ext = ext_ref[h, :, cols]
    x2 = pltpu.roll(ext, 2, 0)[HALO:]
    x1 = pltpu.roll(ext, 1, 0)[HALO:]
    x0 = ext[HALO:]
    y = b_ref[:, cols] + w_ref[0:1, cols] * x2
    y = y + w_ref[1:2, cols] * x1
    y = y + w_ref[2:3, cols] * x0
    return y, (x2, x1, x0)


def _gelu_parts(a):
    cdf = 0.5 * (1.0 + lax.erf(a * (1.0 / math.sqrt(2.0))))
    return cdf, a * cdf


def _conv_specs(S):
    T, C = _CONV_T, _CONV_C
    nj = D_FF // C
    blk = pl.BlockSpec((2, T, C), lambda i, j: (0, i, j))
    per = T // HALO
    halo = pl.BlockSpec((2, HALO, C), lambda i, j: (0, jnp.maximum(i * per - 1, 0), j))
    w_a = pl.BlockSpec((3, C), lambda i, j: (0, j))
    w_b = pl.BlockSpec((3, C), lambda i, j: (0, j + nj))
    b_a = pl.BlockSpec((1, C), lambda i, j: (0, j))
    b_b = pl.BlockSpec((1, C), lambda i, j: (0, j + nj))
    return T, C, nj, blk, halo, w_a, w_b, b_a, b_b


def _conv_load(ext_ref, halo_ref, up_ref, first):
    for h in range(2):
        ext_ref[h, 0:HALO, :] = jnp.where(first, 0.0, halo_ref[h].astype(F32))
        ext_ref[h, HALO:, :] = up_ref[h].astype(F32)


def _up_proj_convffn(h, w_up_t, conv_w, conv_b):
    S = h.shape[0]
    tm, C = 1024, _CONV_C
    nj = D_FF // C

    def body(h_ref, wa_ref, wb_ref, cwa_ref, cwb_ref, cba_ref, cbb_ref, up_ref, act_ref, ext_ref, carry_ref):
        i, j = pl.program_id(0), pl.program_id(1)
        for half, w_ref in enumerate((wa_ref, wb_ref)):
            pre = _dot(h_ref[...], w_ref[...], "nt").astype(BF16)
            up_ref[half] = pre
            ext_ref[half, 0:HALO, :] = jnp.where(i == 0, 0.0, carry_ref[j, half])
            ext_ref[half, HALO:, :] = pre.astype(F32)
            carry_ref[j, half] = ext_ref[half, tm:tm + HALO, :]
        a, _ = _conv_taps(ext_ref, 0, cwa_ref, cba_ref)
        b, _ = _conv_taps(ext_ref, 1, cwb_ref, cbb_ref)
        _, gelu = _gelu_parts(a)
        act_ref[...] = (gelu * b).astype(BF16)

    def w_spec(half):
        return pl.BlockSpec((C, D_MODEL), lambda i, j: (j + half * nj, 0))

    def c_spec(rows, half):
        return pl.BlockSpec((rows, C), lambda i, j: (0, j + half * nj))

    return pl.pallas_call(
        body, grid=(S // tm, nj),
        in_specs=[pl.BlockSpec((tm, D_MODEL), lambda i, j: (i, 0)), w_spec(0), w_spec(1),
                  c_spec(3, 0), c_spec(3, 1), c_spec(1, 0), c_spec(1, 1)],
        out_specs=[pl.BlockSpec((2, tm, C), lambda i, j: (0, i, j)), pl.BlockSpec((tm, C), lambda i, j: (i, j))],
        out_shape=[jax.ShapeDtypeStruct((2, S, D_FF), BF16), jax.ShapeDtypeStruct((S, D_FF), BF16)],
        scratch_shapes=[pltpu.VMEM((2, HALO + tm, C), F32), pltpu.VMEM((nj, 2, HALO, C), F32)],
        compiler_params=_params("arbitrary", "arbitrary"), name="up_proj_convffn")(
            h, w_up_t, w_up_t, conv_w, conv_w, conv_b, conv_b)


def _convffn_bwd_a(up_pre, dx, w_down, conv_w, conv_b):
    S = up_pre.shape[1]
    T, C, nj, blk, halo, w_a, w_b, b_a, b_b = _conv_specs(S)
    def swap(spec):
        return pl.BlockSpec(spec.block_shape, lambda j, i, _m=spec.index_map: _m(i, j))

    def body(up_ref, halo_ref, dx_ref, wd_ref, wa_ref, wb_ref, ba_ref, bb_ref, dup_ref, dwa_ref, dwb_ref, dba_ref,
             dbb_ref, ext_ref):
        i = pl.program_id(1)
        _conv_load(ext_ref, halo_ref, up_ref, i == 0)

        @pl.when(i == 0)
        def _():
            for acc_ref in (dwa_ref, dwb_ref, dba_ref, dbb_ref):
                acc_ref[...] = jnp.zeros_like(acc_ref)

        for c in range(C // MXU_DIM):
            cs = slice(c * MXU_DIM, (c + 1) * MXU_DIM)
            dact = _dot(dx_ref[...], wd_ref[cs, :], "nt")
            a, xa = _conv_taps(ext_ref, 0, wa_ref, ba_ref, cs)
            b, xb = _conv_taps(ext_ref, 1, wb_ref, bb_ref, cs)
            cdf, gelu = _gelu_parts(a)
            dgelu = cdf + a * (jnp.exp(-0.5 * a * a) * (1.0 / math.sqrt(2.0 * math.pi)))
            da = dact * b * dgelu
            db = dact * gelu
            dup_ref[0, :, cs] = da.astype(BF16)
            dup_ref[1, :, cs] = db.astype(BF16)
            for dval, xs, dw_ref, dbias_ref in ((da, xa, dwa_ref, dba_ref), (db, xb, dwb_ref, dbb_ref)):
                dw_ref[:, cs] += jnp.concatenate([jnp.sum(dval * xk, axis=0, keepdims=True) for xk in xs], axis=0)
                dbias_ref[:, cs] += jnp.sum(dval, axis=0, keepdims=True)

    dx_spec = pl.BlockSpec((T, D_MODEL), lambda j, i: (i, 0))
    wd_spec = pl.BlockSpec((C, D_MODEL), lambda j, i: (j, 0))
    dw_spec = pl.BlockSpec((3, C), lambda j, i: (0, j))
    db_spec = pl.BlockSpec((1, C), lambda j, i: (0, j))
    res = pl.pallas_call(
        body, grid=(nj, S // T),
        in_specs=[swap(blk), swap(halo), dx_spec, wd_spec, swap(w_a), swap(w_b), swap(b_a), swap(b_b)],
        out_specs=[swap(blk), dw_spec, dw_spec, db_spec, db_spec],
        out_shape=[jax.ShapeDtypeStruct((2, S, D_FF), BF16), jax.ShapeDtypeStruct((3, D_FF), F32),
                   jax.ShapeDtypeStruct((3, D_FF), F32), jax.ShapeDtypeStruct((1, D_FF), F32),
                   jax.ShapeDtypeStruct((1, D_FF), F32)],
        scratch_shapes=[pltpu.VMEM((2, HALO + T, C), F32)],
        compiler_params=_params("parallel", "arbitrary"), name="convffn_bwd_a")(
            up_pre, up_pre, dx, w_down, conv_w, conv_w, conv_b, conv_b)
    dup, dwa, dwb, dba, dbb = res
    dconv_w = jnp.concatenate([dwa, dwb], axis=1)
    dconv_b = jnp.concatenate([dba, dbb], axis=1)
    return dup, dconv_w, dconv_b


def _convffn_bwd_b(dup, conv_w, w_up_t):
    S = dup.shape[1]
    tm, tk = 1024, D_FF // 4
    per_half = D_FF // tk
    nk = 2 * per_half
    per = tm // HALO
    n_blk = S // tm

    def body(d_ref, halo_ref, cw_ref, w_ref, dpre_ref, dh_ref, ext_ref, acc_ref):
        i, k = pl.program_id(0), pl.program_id(1)
        n = tm + HALO
        ext_ref[0:tm, :] = d_ref[...].astype(F32)
        ext_ref[tm:, :] = jnp.where(i == n_blk - 1, 0.0, halo_ref[...].astype(F32))
        ext = ext_ref[...]
        y = cw_ref[2:3, :] * ext[0:tm] + cw_ref[1:2, :] * pltpu.roll(ext, n - 1, 0)[0:tm]
        y = y + cw_ref[0:1, :] * pltpu.roll(ext, n - 2, 0)[0:tm]
        pre = y.astype(BF16)
        dpre_ref[...] = pre
        prod = _dot(pre, w_ref[...], "nn")

        @pl.when(k == 0)
        def _():
            acc_ref[...] = prod

        @pl.when(k > 0)
        def _():
            acc_ref[...] += prod

        @pl.when(k == nk - 1)
        def _():
            dh_ref[...] = acc_ref[...].astype(BF16)

    blk = pl.BlockSpec((None, tm, tk), lambda i, k: (k // per_half, i, k % per_half))
    halo = pl.BlockSpec((None, HALO, tk),
                        lambda i, k: (k // per_half, jnp.minimum((i + 1) * per, S // HALO - 1), k % per_half))
    return pl.pallas_call(
        body, grid=(n_blk, nk),
        in_specs=[blk, halo, pl.BlockSpec((3, tk), lambda i, k: (0, k)), pl.BlockSpec((tk, D_MODEL), lambda i, k: (k, 0))],
        out_specs=[blk, pl.BlockSpec((tm, D_MODEL), lambda i, k: (i, 0))],
        out_shape=[jax.ShapeDtypeStruct((2, S, D_FF), BF16), jax.ShapeDtypeStruct((S, D_MODEL), BF16)],
        scratch_shapes=[pltpu.VMEM((tm + HALO, tk), F32), pltpu.VMEM((tm, D_MODEL), F32)],
        compiler_params=_params("parallel", "arbitrary"), name="convffn_bwd_b")(dup, dup, conv_w, w_up_t)


def _position():
    return lax.axis_index("x"), lax.axis_index("y"), lax.axis_index("c")


def _flip(pos, k):
    x, y, c = pos
    return (1 - x if k & 4 else x, 1 - y if k & 2 else y, 1 - c if k & 1 else c)


def _index(pos):
    return 4 * pos[0] + 2 * pos[1] + pos[2]


_HBM = pl.BlockSpec(memory_space=pltpu.HBM)
_SEM = pl.BlockSpec(memory_space=pltpu.SEMAPHORE)
_ANY = pl.BlockSpec(memory_space=pl.ANY)
_EFFECT = pltpu.SideEffectType.DATAFLOW_SIDE_EFFECTING


def _in_hbm(a):
    return pltpu.with_memory_space_constraint(a, pltpu.HBM)


def _split_copies(name, srcs, lands, n_copies, plan, start_after, wait_after):
    ns, nl = len(srcs), len(lands)
    start_after = tuple(start_after)
    na = len(start_after)

    def start_body(*refs):
        src_refs, land_refs = refs[:ns], refs[ns:ns + nl]
        send_sems, recv_sems = refs[ns + nl + na], refs[ns + nl + na + 1]
        token = refs[-1]
        for k, (src, dst, _, peer) in enumerate(plan(src_refs, land_refs)):
            pltpu.make_async_remote_copy(src_ref=src, dst_ref=dst, send_sem=send_sems.at[k], recv_sem=recv_sems.at[k],
                                         device_id=peer, device_id_type=MESH).start()
        token[...] = jnp.zeros_like(token)

    thru = [pltpu.HBM(a.shape, a.dtype) for a in list(srcs) + list(lands)]
    res = pl.pallas_call(
        start_body, name=name + "_start",
        out_shape=[pltpu.SemaphoreType.DMA((n_copies,)), pltpu.SemaphoreType.DMA((n_copies,))] + thru
        + [jax.ShapeDtypeStruct((8, 128), F32)],
        in_specs=[_HBM] * (ns + nl) + [_ANY] * na,
        out_specs=[_SEM, _SEM] + [_HBM] * (ns + nl) + [pl.BlockSpec(memory_space=pltpu.VMEM)],
        input_output_aliases={i: 2 + i for i in range(ns + nl)},
        compiler_params=pltpu.CompilerParams(has_side_effects=_EFFECT),
    )(*[_in_hbm(a) for a in list(srcs) + list(lands)], *start_after)
    send_sems, recv_sems, token = res[0], res[1], res[-1]
    thru_vals = res[2:-1]

    def wait():
        def wait_body(*refs):
            src_refs, land_refs = refs[:ns], refs[ns:ns + nl]
            s_sems, r_sems = refs[ns + nl], refs[ns + nl + 1]
            for k, (src, _, dst, peer) in enumerate(plan(src_refs, land_refs)):
                cp = pltpu.make_async_remote_copy(src_ref=src, dst_ref=dst, send_sem=s_sems.at[k], recv_sem=r_sems.at[k],
                                                  device_id=peer, device_id_type=MESH)
                cp.wait_send()
                cp.wait_recv()

        out = pl.pallas_call(
            wait_body, name=name + "_wait", out_shape=thru,
            in_specs=[_HBM] * (ns + nl) + [_SEM, _SEM, _ANY], out_specs=[_HBM] * (ns + nl),
            input_output_aliases={i: i for i in range(ns + nl)},
            compiler_params=pltpu.CompilerParams(has_side_effects=_EFFECT),
        )(*thru_vals, send_sems, recv_sems, wait_after())
        return list(out[:ns]), list(out[ns:])

    return wait, token


def _gather_level1_plan(n):
    def plan(src_refs, land_refs):
        me = _position()
        x, y, c = me
        peers = [(x, y, 1 - c), (1 - x, y, c), (x, 1 - y, c), (1 - x, 1 - y, c)]
        return [(src_refs[a], land_refs[a].at[_index(me)], land_refs[a].at[_index(p)], p)
                for a in range(n) for p in peers]
    return plan


def _gather_level2(lands):
    n = len(lands)

    def body(*refs):
        lin = refs[:n]
        send_sems, recv_sems = refs[2 * n:]
        x, y, c = _position()
        sibling = (x, y, 1 - c)
        chips = [(1 - x, y), (x, 1 - y), (1 - x, 1 - y)]

        def copy(a, j, core):
            slot = lin[a].at[_index((*chips[j], core))]
            return pltpu.make_async_remote_copy(
                src_ref=slot, dst_ref=slot, send_sem=send_sems.at[3 * a + j], recv_sem=recv_sems.at[3 * a + j],
                device_id=sibling, device_id_type=MESH)

        sent = [copy(a, j, c) for a in range(n) for j in range(3)]
        for cp in sent:
            cp.start()
        for a in range(n):
            for j in range(3):
                copy(a, j, 1 - c).wait_recv()
        for cp in sent:
            cp.wait_send()

    return pl.pallas_call(
        body, out_shape=[jax.ShapeDtypeStruct(l.shape, l.dtype) for l in lands],
        in_specs=[_ANY] * n, out_specs=[_ANY] * n, input_output_aliases={a: a for a in range(n)},
        scratch_shapes=[pltpu.SemaphoreType.DMA((3 * n,)), pltpu.SemaphoreType.DMA((3 * n,))],
        name="gather_level2")(*lands)


def _gather_direct_plan(n):
    def plan(src_refs, land_refs):
        me = _position()
        peers = [_flip(me, k) for k in range(1, N_DEV)]
        return [(src_refs[a], land_refs[a].at[_index(me)], land_refs[a].at[_index(p)], p)
                for a in range(n) for p in peers]
    return plan


def _place_own(shards):
    me = _index(_position())
    return [lax.dynamic_update_slice(lax.empty((N_DEV,) + s.shape, s.dtype), s[None], (me, 0, 0)) for s in shards]


def _exchange_direct_plan(n):
    def plan(src_refs, land_refs):
        me = _position()
        peers = [_flip(me, k) for k in range(1, N_DEV)]
        return [(src_refs[a].at[_index(p)], land_refs[a].at[_index(me)], land_refs[a].at[_index(p)], p)
                for a in range(n) for p in peers]
    return plan


def _adamw(w, g, m, v):
    m = ADAM_B1 * m + (1.0 - ADAM_B1) * g
    v = ADAM_B2 * v + (1.0 - ADAM_B2) * (g * g)
    m_hat = m / (1.0 - ADAM_B1 ** ADAM_STEP)
    v_hat = v / (1.0 - ADAM_B2 ** ADAM_STEP)
    delta = -ADAM_LR * (m_hat / (jnp.sqrt(v_hat) + ADAM_EPS) + ADAM_WD * w)
    return delta, m, v


def _row_tile(rows, unit, cap=256):
    best = rows
    for t in range(unit, min(rows, cap) + 1, unit):
        if rows % t == 0:
            best = t
    return best


def _sum_partials(name, recv):
    n_src, R, C = recv.shape
    T = _row_tile(R, 16)

    def body(r_ref, g_out):
        g = r_ref[0].astype(F32)
        for s in range(1, n_src):
            g = g + r_ref[s].astype(F32)
        g_out[...] = g

    return pl.pallas_call(
        body, grid=(R // T,), in_specs=[pl.BlockSpec((n_src, T, C), lambda i: (0, i, 0))],
        out_specs=pl.BlockSpec((T, C), lambda i: (i, 0)), out_shape=jax.ShapeDtypeStruct((R, C), F32),
        compiler_params=_params("parallel"), name=name)(recv)


def _sum_adamw(name, recv, w, m, v):
    n_src, R, C = recv.shape
    T = _row_tile(R, 16)

    def body(r_ref, w_ref, m_ref, v_ref, g_out, d_out, m_out, v_out):
        g = r_ref[0].astype(F32)
        for s in range(1, n_src):
            g = g + r_ref[s].astype(F32)
        delta, mn, vn = _adamw(w_ref[...], g, m_ref[...], v_ref[...])
        g_out[...] = g
        d_out[...] = delta
        m_out[...] = mn
        v_out[...] = vn

    blk = pl.BlockSpec((T, C), lambda i: (i, 0))
    return pl.pallas_call(
        body, grid=(R // T,), in_specs=[pl.BlockSpec((n_src, T, C), lambda i: (0, i, 0)), blk, blk, blk],
        out_specs=[blk] * 4, out_shape=[jax.ShapeDtypeStruct((R, C), F32)] * 4,
        compiler_params=_params("parallel"), name=name)(recv, w, m, v)


def _adamw_update(name, w, g, m, v):
    R, C = w.shape
    T = _row_tile(R, 8)

    def body(w_ref, g_ref, m_ref, v_ref, d_out, m_out, v_out):
        delta, mn, vn = _adamw(w_ref[...], g_ref[...], m_ref[...], v_ref[...])
        d_out[...] = delta
        m_out[...] = mn
        v_out[...] = vn

    blk = pl.BlockSpec((T, C), lambda i: (i, 0))
    return pl.pallas_call(
        body, grid=(R // T,), in_specs=[blk] * 4, out_specs=[blk] * 3,
        out_shape=[jax.ShapeDtypeStruct((R, C), F32)] * 3, compiler_params=_params("parallel"), name=name)(w, g, m, v)


def _gather_vmem(p_ref, gath, send_sems, recv_sems):
    me = _position()
    gath[_index(me)] = p_ref[...]
    copies = []
    for k in range(1, N_DEV):
        peer = _flip(me, k)
        copies.append(pltpu.make_async_remote_copy(
            src_ref=p_ref, dst_ref=gath.at[_index(me)], send_sem=send_sems.at[k - 1],
            recv_sem=recv_sems.at[k - 1], device_id=peer, device_id_type=MESH))
    for cp in copies:
        cp.start()
    for k in range(1, N_DEV):
        peer = _flip(me, k)
        pltpu.make_async_remote_copy(
            src_ref=p_ref, dst_ref=gath.at[_index(peer)], send_sem=send_sems.at[k - 1],
            recv_sem=recv_sems.at[k - 1], device_id=peer, device_id_type=MESH).wait_recv()
    for cp in copies:
        cp.wait_send()


def _small_allreduce_adamw(part, w, m, v):
    R, C = part.shape

    def body(p_ref, w_ref, m_ref, v_ref, g_out, d_out, m_out, v_out, gath, send_sems, recv_sems):
        _gather_vmem(p_ref, gath, send_sems, recv_sems)
        g = gath[0]
        for s in range(1, N_DEV):
            g = g + gath[s]
        delta, mn, vn = _adamw(w_ref[...], g, m_ref[...], v_ref[...])
        g_out[...] = g
        d_out[...] = delta
        m_out[...] = mn
        v_out[...] = vn

    vm = pl.BlockSpec(memory_space=pltpu.VMEM)
    return pl.pallas_call(
        body, in_specs=[vm] * 4, out_specs=[vm] * 4, out_shape=[jax.ShapeDtypeStruct((R, C), F32)] * 4,
        scratch_shapes=[pltpu.VMEM((N_DEV, R, C), F32), pltpu.SemaphoreType.DMA((7,)), pltpu.SemaphoreType.DMA((7,))],
        name="small_allreduce_adamw")(part, w, m, v)


_TRANSPOSED = ("w_in", "w_up", "w_pool_out", "w_attn_out")
_SMALL = (("g_mix", 2048), ("b_gate", 4096), ("pool_scale", 1024), ("g_ffn", 2048), ("conv_b", 11264), ("g_final", 2048))
CONV_ROWS = 8


def _to_comm(name, shard):
    return shard.T if name in _TRANSPOSED else shard


SMALL_USED = sum(n for _, n in _SMALL)


def _pack_small(vals, extra=None):
    flat = [vals[n].reshape(-1) for n, _ in _SMALL]
    if extra is not None:
        flat.append(extra.reshape(1))
    used = sum(f.shape[0] for f in flat)
    flat.append(jnp.zeros((SMALL_ROWS * PACK_COLS - used,), F32))
    return jnp.concatenate(flat).reshape(SMALL_ROWS, PACK_COLS)


def _unpack_small(packed, like):
    flat = packed.reshape(-1)
    out, off = {}, 0
    for name, n in _SMALL:
        out[name] = flat[off:off + n].reshape(like[name].shape)
        off += n
    return out


def _local_step(x, target, W, small, first_token=0.0, first_weights=None, late_weights=None, emit=None):
    S = x.shape[0]
    o_qkv, o_gate = POOL_WIDTH, POOL_WIDTH + QKV_WIDTH
    if emit is None:
        emit = lambda grads: None

    def zero(token):
        return 0.0 if token is None else token[0, 0]

    h1 = _rms_fwd("rms1_fwd", x, small["g_mix"] + first_token)
    tok = None
    if first_weights is not None:
        w_first, tok = first_weights(h1)
        small = dict(small, conv_w_full=w_first.pop("conv_w_full"))
        W = dict(W, **w_first)
    w_in_t = W["w_in"]
    u = _mm_nt("proj_u", h1, w_in_t, tm=2048, tn=512, tk=2048, o_dtype=F32, n_rows=POOL_WIDTH, b_row_off=0,
               after=tok)
    qkv = _mm_nt("proj_qkv", h1, w_in_t, tm=2048, tn=512, tk=2048, o_dtype=BF16, n_rows=QKV_WIDTH, b_row_off=o_qkv,
                 after=tok)
    gates_pre = _mm_nt("proj_gates", h1, w_in_t, tm=2048, tn=512, tk=2048, o_dtype=BF16, n_rows=2 * D_MODEL,
                       b_row_off=o_gate, after=tok)
    pool_y = _pool_fwd(u, W["w_pool_lin"], small["pool_scale"])
    outs, lses = zip(*[_attn_fwd(gi, qkv) for gi in range(3)])
    attn_y = _merge_fwd(outs, lses)
    if late_weights is not None:
        W = dict(W, **late_weights(0, attn_y))
    w_up_t = W["w_up"]
    mixed, y_pool, y_attn = _mix_fwd(pool_y, attn_y, W["w_pool_out"], W["w_attn_out"], gates_pre, small["b_gate"])
    x1, h2 = _out_proj_norm(mixed, W["w_out"], x, small["g_ffn"])

    nj = D_FF // 512
    up_pre, act = _up_proj_convffn(h2, w_up_t, small["conv_w_full"], small["conv_b"])
    if late_weights is not None:
        W = dict(W, **late_weights(1, act))
    x2 = _mm_nn("down_proj", act, W["w_down"], tm=1024, tn=1024, tk=D_FF // 2, o_dtype=F32, res=x1)

    dx2, dx2_b, dg_final, sq_cols = _final_loss("final_loss", x2, small["g_final"], target)

    dw_down = _mm_tn("dw_down", act, dx2_b, tm=512, tn=2048, tk=TOKEN_TK)
    tok = emit(dict(w_down=dw_down))
    dup, dconv_w, dconv_b = _convffn_bwd_a(up_pre, dx2_b, W["w_down"], small["conv_w_full"],
                                           small["conv_b"] + zero(tok))
    dup_pre, dh2 = _convffn_bwd_b(dup, small["conv_w_full"], w_up_t)
    dw_up = _mm("dw_up", dup_pre, h2, mode="tn", grid=(2 * nj, 1, S // TOKEN_TK),
                a_spec=pl.BlockSpec((None, TOKEN_TK, 512), lambda i, j, k: (i // nj, k, i % nj)),
                b_spec=pl.BlockSpec((TOKEN_TK, D_MODEL), lambda i, j, k: (k, 0)),
                o_spec=pl.BlockSpec((512, D_MODEL), lambda i, j, k: (i, 0)),
                o_shape=(2 * D_FF, D_MODEL), o_dtype=BF16, acc_shape=(512, D_MODEL))
    tok = emit(dict(w_up=dw_up, conv_w=dconv_w))
    dx1, dx1_b, dg_ffn = _rms_bwd("rms2_bwd", x1, small["g_ffn"] + zero(tok), dh2, dx2, with_bf16=True)

    dw_out = _mm_tn("dw_out", mixed, dx1_b, tm=512, tn=2048, tk=TOKEN_TK)
    tok = emit(dict(w_out=dw_out))
    dy_pool, dy_attn, dgates, db_gate = _mix_bwd(dx1_b, W["w_out"], gates_pre, small["b_gate"] + zero(tok),
                                                 y_pool, y_attn)
    dw_pool_out = _mm_tn("dw_pool_out", dy_pool, pool_y, tm=1024, tn=1024, tk=TOKEN_TK)
    dw_attn_out = _mm_tn("dw_attn_out", dy_attn, attn_y, tm=1024, tn=512, tk=TOKEN_TK)
    e, dw_pool_lin, dpool_scale = _pool_bwd_a(u, dy_pool, W["w_pool_out"], W["w_pool_lin"], small["pool_scale"])
    du = _pool_bwd_b(e)
    dos, dps = _merge_bwd(dy_attn, W["w_attn_out"], outs, lses)
    dqkv = [_attn_bwd(gi, qkv, dos[gi], lses[gi], dps[gi]) for gi in range(3)]
    dproj = jnp.concatenate([du] + [dqkv[gi][sec] for sec in range(3) for gi in range(3)] + [dgates], axis=1)
    dw_in = _mm_tn("dw_in", dproj, h1, tm=512, tn=2048, tk=TOKEN_TK)
    tok = emit(dict(w_in=dw_in, w_pool_out=dw_pool_out, w_attn_out=dw_attn_out, w_pool_lin=dw_pool_lin))
    dh1 = _mm_nn("d_h1", dproj, w_in_t, tm=1024, tn=1024, tk=IN_WIDTH // 4, o_dtype=BF16,
                 after=tok)
    grad_x, dg_mix = _rms_bwd("rms1_bwd", x, small["g_mix"] + zero(tok), dh1, dx1, with_bf16=False)

    grads = dict(w_in=dw_in, w_up=dw_up, w_down=dw_down, w_out=dw_out, w_pool_out=dw_pool_out,
                 w_attn_out=dw_attn_out, w_pool_lin=dw_pool_lin, conv_w=dconv_w)
    small_grads = dict(g_mix=dg_mix, b_gate=db_gate, pool_scale=dpool_scale, g_ffn=dg_ffn, conv_b=dconv_b,
                       g_final=dg_final)
    return sq_cols, grad_x, grads, small_grads


def kernel(x, g_mix, w_in, b_gate, w_pool_lin, pool_scale, w_pool_out, w_attn_out, w_out, g_ffn, w_up, conv_w, conv_b, w_down, g_final, loss_target, m_g_mix, m_w_in, m_b_gate, m_w_pool_lin, m_pool_scale, m_w_pool_out, m_w_attn_out, m_w_out, m_g_ffn, m_w_up, m_conv_w, m_conv_b, m_w_down, m_g_final, v_g_mix, v_w_in, v_b_gate, v_w_pool_lin, v_pool_scale, v_w_pool_out, v_w_attn_out, v_w_out, v_g_ffn, v_w_up, v_conv_w, v_conv_b, v_w_down, v_g_final):
    given = dict(g_mix=g_mix, w_in=w_in, b_gate=b_gate, w_pool_lin=w_pool_lin, pool_scale=pool_scale,
                 w_pool_out=w_pool_out, w_attn_out=w_attn_out, w_out=w_out, g_ffn=g_ffn, w_up=w_up, conv_w=conv_w,
                 conv_b=conv_b, w_down=w_down, g_final=g_final)
    mom_m = dict(g_mix=m_g_mix, w_in=m_w_in, b_gate=m_b_gate, w_pool_lin=m_w_pool_lin, pool_scale=m_pool_scale,
                 w_pool_out=m_w_pool_out, w_attn_out=m_w_attn_out, w_out=m_w_out, g_ffn=m_g_ffn, w_up=m_w_up,
                 conv_w=m_conv_w, conv_b=m_conv_b, w_down=m_w_down, g_final=m_g_final)
    mom_v = dict(g_mix=v_g_mix, w_in=v_w_in, b_gate=v_b_gate, w_pool_lin=v_w_pool_lin, pool_scale=v_pool_scale,
                 w_pool_out=v_w_pool_out, w_attn_out=v_w_attn_out, w_out=v_w_out, g_ffn=v_g_ffn, w_up=v_w_up,
                 conv_w=v_conv_w, conv_b=v_conv_b, w_down=v_w_down, g_final=v_g_final)
    def full(g):
        return g.reshape(N_DEV * g.shape[1], g.shape[2])

    shard = dict(w_in=_to_comm("w_in", w_in[0]).astype(BF16))
    W = {}
    first = [shard["w_in"], w_pool_lin[0].reshape(4 * 32, POOL_GROUP_WIDTH),
             jnp.pad(conv_w[0], ((0, CONV_ROWS - 3), (0, 0)))]
    hold = {}
    w_in_wait, first_token = _split_copies(
        "gather_w_in", first, _place_own(first), 4 * len(first), _gather_level1_plan(len(first)),
        start_after=(), wait_after=lambda: hold["h1"])

    stages = (("w_out", "w_pool_out", "w_attn_out", "w_up"), ("w_down",))
    stage_after = [None, None]
    stage_wait = []
    for n in (n for names in stages for n in names):
        shard[n] = _to_comm(n, given[n][0] + first_token[0, 0]).astype(BF16)

    def first_weights(h1):
        hold["h1"] = h1
        w_in_all, lin_all, conv_w_all = _gather_level2(w_in_wait()[1])
        w_in_full = full(w_in_all)
        lin_full = jnp.moveaxis(lin_all.reshape(N_DEV, 4, 32, POOL_GROUP_WIDTH), 0, 1).reshape(
            4, POOL_GROUP_WIDTH, POOL_GROUP_WIDTH).astype(BF16)
        conv_w_full = jnp.moveaxis(conv_w_all[:, :3, :], 0, 1).reshape(3, 2 * D_FF)
        order_after = (w_in_full,)
        for k, names in enumerate(stages):
            srcs = [shard[n] for n in names]
            wait, token = _split_copies(f"gather_stage{k}", srcs, _place_own(srcs), 7 * len(names),
                                        _gather_direct_plan(len(names)), start_after=order_after,
                                        wait_after=lambda k=k: stage_after[k])
            stage_wait.append(wait)
            order_after = (token,)
        return dict(w_in=w_in_full, w_pool_lin=lin_full, conv_w_full=conv_w_full), token

    def late_weights(stage, after):
        stage_after[stage] = after
        return {n: full(l) for n, l in zip(stages[stage], stage_wait[stage]()[1])}

    small = dict(g_mix=g_mix, b_gate=b_gate, pool_scale=pool_scale, g_ffn=g_ffn, conv_b=conv_b,
                 g_final=g_final.reshape(1, D_MODEL))

    pending = []
    done = {}

    def emit(group):
        names = list(group)
        parts = []
        for n in names:
            g = group[n]
            if n == "w_pool_lin":
                g = jnp.moveaxis(g.reshape(4, N_DEV, 32, POOL_GROUP_WIDTH), 1, 0).reshape(
                    N_DEV, 4 * 32, POOL_GROUP_WIDTH).astype(BF16)
            elif n == "conv_w":
                g = jnp.pad(jnp.moveaxis(g.reshape(3, N_DEV, 2 * D_FF // N_DEV), 1, 0),
                            ((0, 0), (0, CONV_ROWS - 3), (0, 0))).astype(BF16)
            else:
                g = g.reshape(N_DEV, g.shape[0] // N_DEV, g.shape[1])
            parts.append(g)
        me = _index(_position())
        lands = [lax.dynamic_update_slice(lax.empty(p.shape, p.dtype), lax.dynamic_index_in_dim(p, me, 0),
                                          (me, 0, 0)) for p in parts]
        wait, token = _split_copies(
            "exchange_" + names[0], parts, lands, 7 * len(names), _exchange_direct_plan(len(names)),
            start_after=(), wait_after=lambda: done["grad_x"])
        pending.append((names, wait))
        return token

    prepared = sum(shard[n][0, 0].astype(F32) for names in stages for n in names) * 0.0
    sq_cols, grad_x, _, small_grads = _local_step(x[0], loss_target[0], W, small,
                                                  first_token=first_token[0, 0] + prepared,
                                                  first_weights=first_weights, late_weights=late_weights, emit=emit)
    done["grad_x"] = grad_x

    res = {}
    for n, r in [(n, r) for names, wait in pending for n, r in zip(names, wait()[1])]:
        shape = given[n].shape
        rows = math.prod(shape[:-1])
        w2, m2, v2 = (d[n].reshape(rows, shape[-1]) for d in (given, mom_m, mom_v))
        if n in _TRANSPOSED or n == "conv_w":
            g = _sum_partials("sum_" + n, r)
            g = g.T if n in _TRANSPOSED else g[:3]
            upd = (g,) + tuple(_adamw_update("adamw_" + n, w2, g, m2, v2))
        else:
            upd = _sum_adamw("adamw_" + n, r, w2, m2, v2)
        res[n] = [a.reshape(shape) for a in upd]

    packed_small = [_pack_small(d) for d in (given, mom_m, mom_v)]
    packed_out = _small_allreduce_adamw(_pack_small(small_grads, extra=jnp.sum(sq_cols)), *packed_small)
    res_small = [_unpack_small(p, given) for p in packed_out]
    loss = (0.5 / D_MODEL) * packed_out[0][SMALL_USED // PACK_COLS, SMALL_USED % PACK_COLS]

    order = ["g_mix", "w_in", "b_gate", "w_pool_lin", "pool_scale", "w_pool_out", "w_attn_out", "w_out", "g_ffn",
             "w_up", "conv_w", "conv_b", "w_down", "g_final"]
    outs = [loss, grad_x[None]]
    for kind in range(4):
        for n in order:
            outs.append(res[n][kind] if n in res else res_small[kind][n])
    return tuple(outs)
```

```python
import math

import jax
import jax.numpy as jnp
from jax import lax
from jax.experimental import pallas as pl
from jax.experimental.pallas import tpu as pltpu

F32 = jnp.float32
BF16 = jnp.bfloat16
MESH = pl.DeviceIdType.MESH
N_DEV = 8

D_MODEL = 2048
POOL_WINDOWS = (2, 4, 8, 16)
POOL_GROUP_WIDTH = 256
POOL_WIDTH = 1024
ATTN_GROUPS = ((128, 1), (512, 4), (2048, 16))
SPAN = 128
HEADS_PER_GROUP = 4
N_ATTN_HEADS = 12
HEAD_DIM = 128
ATTN_WIDTH = 1536
QKV_WIDTH = 3 * ATTN_WIDTH
ATTN_OUT_WIDTH = 512
IN_WIDTH = 9728
D_FF = 5632
RMS_EPS = 1e-6
ADAM_LR, ADAM_B1, ADAM_B2, ADAM_EPS, ADAM_WD, ADAM_STEP = 0.001, 0.9, 0.999, 1e-08, 0.01, 10

VMEM_LIMIT_BYTES = 48 * 1024 * 1024
HALO = 16
TOKEN_TK = 2048
MXU_DIM = 256
PACK_COLS = 1024
SMALL_ROWS = 24
NEG_BIG = -1e30

_DN = {"nn": (((1,), (0,)), ((), ())), "nt": (((1,), (1,)), ((), ())), "tn": (((0,), (0,)), ((), ()))}


def _params(*sem):
    return pltpu.CompilerParams(dimension_semantics=sem, vmem_limit_bytes=VMEM_LIMIT_BYTES)


def _dot(a, b, mode):
    return lax.dot_general(a.astype(BF16), b.astype(BF16), _DN[mode], preferred_element_type=F32)


def _mm(name, a, b, *, mode, grid, a_spec, b_spec, o_spec, o_shape, o_dtype, acc_shape, res=None, res_spec=None,
        after=None):
    nk = grid[2]
    n_in = 2 + (res is not None) + (after is not None)

    def body(*refs):
        a_ref, b_ref = refs[:2]
        r_ref = refs[2] if res is not None else None
        o_ref = refs[n_in]
        scr = refs[n_in + 1:]
        prod = _dot(a_ref[...], b_ref[...], mode)

        def finish(val):
            if r_ref is not None:
                val = val + r_ref[...]
            o_ref[...] = val.astype(o_ref.dtype)

        if nk == 1:
            finish(prod)
        else:
            acc = scr[0]
            k = pl.program_id(2)

            @pl.when(k == 0)
            def _():
                acc[...] = prod

            @pl.when(k > 0)
            def _():
                acc[...] += prod

            @pl.when(k == nk - 1)
            def _():
                finish(acc[...])

    in_specs = [a_spec, b_spec]
    args = [a, b]
    if res is not None:
        in_specs.append(res_spec)
        args.append(res)
    if after is not None:
        in_specs.append(pl.BlockSpec((8, 128), lambda i, j, k: (0, 0)))
        args.append(after)
    return pl.pallas_call(
        body, grid=grid, in_specs=in_specs, out_specs=o_spec,
        out_shape=jax.ShapeDtypeStruct(o_shape, o_dtype),
        scratch_shapes=[] if nk == 1 else [pltpu.VMEM(acc_shape, F32)],
        compiler_params=_params("parallel", "parallel", "arbitrary"), name=name)(*args)


def _mm_nn(name, a, b, *, tm, tn, tk, o_dtype, res=None, after=None):
    M, K = a.shape
    N = b.shape[1]
    return _mm(name, a, b, mode="nn", grid=(M // tm, N // tn, K // tk),
               a_spec=pl.BlockSpec((tm, tk), lambda i, j, k: (i, k)),
               b_spec=pl.BlockSpec((tk, tn), lambda i, j, k: (k, j)),
               o_spec=pl.BlockSpec((tm, tn), lambda i, j, k: (i, j)),
               o_shape=(M, N), o_dtype=o_dtype, acc_shape=(tm, tn), res=res,
               res_spec=pl.BlockSpec((tm, tn), lambda i, j, k: (i, j)), after=after)


def _mm_nt(name, a, b, *, tm, tn, tk, o_dtype, n_rows=None, b_row_off=0, after=None):
    M, K = a.shape
    N = b.shape[0] if n_rows is None else n_rows
    off = b_row_off // tn
    return _mm(name, a, b, mode="nt", grid=(M // tm, N // tn, K // tk),
               a_spec=pl.BlockSpec((tm, tk), lambda i, j, k: (i, k)),
               b_spec=pl.BlockSpec((tn, tk), lambda i, j, k: (j + off, k)),
               o_spec=pl.BlockSpec((tm, tn), lambda i, j, k: (i, j)),
               o_shape=(M, N), o_dtype=o_dtype, acc_shape=(tm, tn), after=after)


def _mm_tn(name, a, b, *, tm, tn, tk):
    K, M = a.shape
    N = b.shape[1]
    return _mm(name, a, b, mode="tn", grid=(M // tm, N // tn, K // tk),
               a_spec=pl.BlockSpec((tk, tm), lambda i, j, k: (k, i)),
               b_spec=pl.BlockSpec((tk, tn), lambda i, j, k: (k, j)),
               o_spec=pl.BlockSpec((tm, tn), lambda i, j, k: (i, j)),
               o_shape=(M, N), o_dtype=BF16, acc_shape=(tm, tn))


def _rms_fwd(name, x, g):
    S, D = x.shape
    T = 512

    def body(x_ref, g_ref, h_ref):
        xv = x_ref[...]
        r = lax.rsqrt(jnp.mean(xv * xv, axis=-1, keepdims=True) + RMS_EPS)
        h_ref[...] = (xv * r * g_ref[...]).astype(BF16)

    return pl.pallas_call(
        body, grid=(S // T,),
        in_specs=[pl.BlockSpec((T, D), lambda i: (i, 0)), pl.BlockSpec((1, D), lambda i: (0, 0))],
        out_specs=pl.BlockSpec((T, D), lambda i: (i, 0)),
        out_shape=jax.ShapeDtypeStruct((S, D), BF16), compiler_params=_params("parallel"), name=name)(x, g)


def _out_proj_norm(mixed, w_out, x, g):
    S, D = x.shape
    tm = 512

    def body(m_ref, w_ref, x_ref, g_ref, x1_ref, h_ref):
        x1 = x_ref[...] + _dot(m_ref[...], w_ref[...], "nn")
        x1_ref[...] = x1
        r = lax.rsqrt(jnp.mean(x1 * x1, axis=-1, keepdims=True) + RMS_EPS)
        h_ref[...] = (x1 * r * g_ref[...]).astype(BF16)

    row = pl.BlockSpec((tm, D), lambda i: (i, 0))
    return pl.pallas_call(
        body, grid=(S // tm,),
        in_specs=[row, pl.BlockSpec((D, D), lambda i: (0, 0)), row, pl.BlockSpec((1, D), lambda i: (0, 0))],
        out_specs=[row, row], out_shape=[jax.ShapeDtypeStruct((S, D), F32), jax.ShapeDtypeStruct((S, D), BF16)],
        compiler_params=_params("parallel"), name="out_proj_norm")(mixed, w_out, x, g)


def _rms_bwd(name, x, g, dh, dres, with_bf16):
    S, D = x.shape
    T = 512

    def body(x_ref, g_ref, dh_ref, dres_ref, dx_ref, *rest):
        dg_ref = rest[-1]
        i = pl.program_id(0)
        xv = x_ref[...]
        r = lax.rsqrt(jnp.mean(xv * xv, axis=-1, keepdims=True) + RMS_EPS)
        xhat = xv * r
        dhv = dh_ref[...].astype(F32)
        gdh = dhv * g_ref[...]
        c = jnp.mean(xhat * gdh, axis=-1, keepdims=True)
        dx = dres_ref[...] + r * (gdh - xhat * c)
        dx_ref[...] = dx
        if with_bf16:
            rest[0][...] = dx.astype(BF16)
        part = jnp.sum(dhv * xhat, axis=0, keepdims=True)

        @pl.when(i == 0)
        def _():
            dg_ref[...] = part

        @pl.when(i > 0)
        def _():
            dg_ref[...] += part

    row = pl.BlockSpec((T, D), lambda i: (i, 0))
    vec = pl.BlockSpec((1, D), lambda i: (0, 0))
    narrow = [jax.ShapeDtypeStruct((S, D), BF16)] if with_bf16 else []
    return pl.pallas_call(
        body, grid=(S // T,), in_specs=[row, vec, row, row], out_specs=[row] * (1 + len(narrow)) + [vec],
        out_shape=[jax.ShapeDtypeStruct((S, D), F32)] + narrow + [jax.ShapeDtypeStruct((1, D), F32)],
        compiler_params=_params("arbitrary"), name=name)(x, g, dh, dres)


def _final_loss(name, x, g, target):
    S, D = x.shape
    T = 512

    def body(x_ref, g_ref, t_ref, dx_ref, dxb_ref, dg_ref, sq_ref):
        i = pl.program_id(0)
        xv = x_ref[...]
        r = lax.rsqrt(jnp.mean(xv * xv, axis=-1, keepdims=True) + RMS_EPS)
        xhat = xv * r
        err = xhat * g_ref[...] - t_ref[...]
        dy = err * (1.0 / D)
        gdy = dy * g_ref[...]
        c = jnp.mean(xhat * gdy, axis=-1, keepdims=True)
        dx = r * (gdy - xhat * c)
        dx_ref[...] = dx
        dxb_ref[...] = dx.astype(BF16)
        dg_part = jnp.sum(dy * xhat, axis=0, keepdims=True)
        sq_part = jnp.sum(err * err, axis=0, keepdims=True)

        @pl.when(i == 0)
        def _():
            dg_ref[...] = dg_part
            sq_ref[...] = sq_part

        @pl.when(i > 0)
        def _():
            dg_ref[...] += dg_part
            sq_ref[...] += sq_part

    row = pl.BlockSpec((T, D), lambda i: (i, 0))
    vec = pl.BlockSpec((1, D), lambda i: (0, 0))
    return pl.pallas_call(
        body, grid=(S // T,), in_specs=[row, vec, row], out_specs=[row, row, vec, vec],
        out_shape=[jax.ShapeDtypeStruct((S, D), F32), jax.ShapeDtypeStruct((S, D), BF16),
                   jax.ShapeDtypeStruct((1, D), F32), jax.ShapeDtypeStruct((1, D), F32)],
        compiler_params=_params("arbitrary"), name=name)(x, g, target)


def _prev_halo_spec(T, C, col_map):
    per = T // HALO
    return pl.BlockSpec((HALO, C), lambda *g: (jnp.maximum(g[0] * per - 1, 0), col_map(*g)))


def _pool_window_sums(ext, T, i):
    t = i * T + lax.broadcasted_iota(jnp.int32, (T, 1), 0)
    outs = []
    cnts = []
    for gi, w in enumerate(POOL_WINDOWS):
        s = ext[:, gi * POOL_GROUP_WIDTH:(gi + 1) * POOL_GROUP_WIDTH]
        sh = 1
        while sh < w:
            s = s + pltpu.roll(s, sh, 0)
            sh *= 2
        cnt = jnp.minimum(t + 1, w).astype(F32)
        outs.append(s[HALO:] / cnt - ext[HALO:, gi * POOL_GROUP_WIDTH:(gi + 1) * POOL_GROUP_WIDTH])
        cnts.append(cnt)
    return outs, cnts


def _load_ext(ext_ref, halo_ref, blk_ref, first):
    ext_ref[0:HALO, :] = jnp.where(first, 0.0, halo_ref[...].astype(F32))
    ext_ref[HALO:, :] = blk_ref[...].astype(F32)


def _pool_fwd(u, w_lin, scale):
    S = u.shape[0]
    T = 512
    W = POOL_WIDTH

    def body(u_ref, halo_ref, wl_ref, sc_ref, y_ref, ext_ref):
        i = pl.program_id(0)
        _load_ext(ext_ref, halo_ref, u_ref, i == 0)
        pooled, _ = _pool_window_sums(ext_ref[...], T, i)
        for gi in range(4):
            z = _dot(pooled[gi], wl_ref[gi], "nn")
            cs = slice(gi * POOL_GROUP_WIDTH, (gi + 1) * POOL_GROUP_WIDTH)
            y_ref[:, cs] = (z * sc_ref[:, cs]).astype(BF16)

    return pl.pallas_call(
        body, grid=(S // T,),
        in_specs=[pl.BlockSpec((T, W), lambda i: (i, 0)), _prev_halo_spec(T, W, lambda i: 0),
                  pl.BlockSpec((4, 256, 256), lambda i: (0, 0, 0)), pl.BlockSpec((1, W), lambda i: (0, 0))],
        out_specs=pl.BlockSpec((T, W), lambda i: (i, 0)),
        out_shape=jax.ShapeDtypeStruct((S, W), BF16),
        scratch_shapes=[pltpu.VMEM((HALO + T, W), F32)],
        compiler_params=_params("parallel"), name="pool_fwd")(u, u, w_lin, scale)


def _pool_bwd_a(u, dy_out, w_pool_out_t, w_lin, scale):
    S = u.shape[0]
    T = 512
    W = POOL_WIDTH

    def body(u_ref, halo_ref, dyo_ref, wpo_ref, wl_ref, sc_ref, e_ref, dwl_ref, dsc_ref, ext_ref):
        i = pl.program_id(0)
        dy = _dot(dyo_ref[...], wpo_ref[...], "nn")
        _load_ext(ext_ref, halo_ref, u_ref, i == 0)
        pooled, cnts = _pool_window_sums(ext_ref[...], T, i)
        for gi in range(4):
            cs = slice(gi * POOL_GROUP_WIDTH, (gi + 1) * POOL_GROUP_WIDTH)
            z = _dot(pooled[gi], wl_ref[gi], "nn")
            dyg = dy[:, cs]
            dsc = jnp.sum(dyg * z, axis=0, keepdims=True)
            dz = dyg * sc_ref[:, cs]
            dwl = _dot(pooled[gi], dz, "tn")
            dpooled = _dot(dz, wl_ref[gi], "nt")
            e_ref[:, cs] = dpooled / cnts[gi]

            @pl.when(i == 0)
            def _():
                dwl_ref[gi] = dwl
                dsc_ref[:, cs] = dsc

            @pl.when(i > 0)
            def _():
                dwl_ref[gi] += dwl
                dsc_ref[:, cs] += dsc

    return pl.pallas_call(
        body, grid=(S // T,),
        in_specs=[pl.BlockSpec((T, W), lambda i: (i, 0)), _prev_halo_spec(T, W, lambda i: 0),
                  pl.BlockSpec((T, D_MODEL), lambda i: (i, 0)), pl.BlockSpec((D_MODEL, W), lambda i: (0, 0)),
                  pl.BlockSpec((4, 256, 256), lambda i: (0, 0, 0)), pl.BlockSpec((1, W), lambda i: (0, 0))],
        out_specs=[pl.BlockSpec((T, W), lambda i: (i, 0)), pl.BlockSpec((4, 256, 256), lambda i: (0, 0, 0)),
                   pl.BlockSpec((1, W), lambda i: (0, 0))],
        out_shape=[jax.ShapeDtypeStruct((S, W), F32), jax.ShapeDtypeStruct((4, 256, 256), F32),
                   jax.ShapeDtypeStruct((1, W), F32)],
        scratch_shapes=[pltpu.VMEM((HALO + T, W), F32)],
        compiler_params=_params("arbitrary"), name="pool_bwd_a")(u, u, dy_out, w_pool_out_t, w_lin, scale)


def _pool_bwd_b(e):
    S = e.shape[0]
    T = 512
    W = POOL_WIDTH
    per = T // HALO
    n_blk = S // T

    def body(e_ref, halo_ref, du_ref, ext_ref):
        i = pl.program_id(0)
        ext_ref[0:T, :] = e_ref[...]
        ext_ref[T:, :] = jnp.where(i == n_blk - 1, 0.0, halo_ref[...])
        ext = ext_ref[...]
        n = T + HALO
        t = i * T + lax.broadcasted_iota(jnp.int32, (T, 1), 0)
        for gi, w in enumerate(POOL_WINDOWS):
            cs = slice(gi * POOL_GROUP_WIDTH, (gi + 1) * POOL_GROUP_WIDTH)
            s = ext[:, cs]
            sh = 1
            while sh < w:
                s = s + pltpu.roll(s, n - sh, 0)
                sh *= 2
            cnt = jnp.minimum(t + 1, w).astype(F32)
            du_ref[:, cs] = (s[0:T] - ext[0:T, cs] * cnt).astype(BF16)

    return pl.pallas_call(
        body, grid=(n_blk,),
        in_specs=[pl.BlockSpec((T, W), lambda i: (i, 0)),
                  pl.BlockSpec((HALO, W), lambda i: (jnp.minimum((i + 1) * per, S // HALO - 1), 0))],
        out_specs=pl.BlockSpec((T, W), lambda i: (i, 0)),
        out_shape=jax.ShapeDtypeStruct((S, W), BF16),
        scratch_shapes=[pltpu.VMEM((T + HALO, W), F32)],
        compiler_params=_params("parallel"), name="pool_bwd_b")(e, e)


def _slope(head):
    return 2.0 ** (-8.0 * (head + 1) / N_ATTN_HEADS)


def _attn_masks():
    row = lax.broadcasted_iota(jnp.int32, (SPAN, SPAN), 0)
    col = lax.broadcasted_iota(jnp.int32, (SPAN, SPAN), 1)
    dist_cur = (row - col).astype(F32)
    return dist_cur, col <= row, col >= row


ATTN_BATCH = 16


def _attn_tiling(gi, S):
    d = ATTN_GROUPS[gi][1]
    hp = HEADS_PER_GROUP if d < 16 else 1
    nblk = ATTN_BATCH // hp if d == 1 else 1
    C = SPAN * d * nblk
    return d, C, S // C, hp, hp * HEAD_DIM, min(ATTN_BATCH // (hp * nblk), d), nblk


_BATCHED_DN = {"nt": (((2,), (2,)), ((0,), (0,))), "nn": (((2,), (1,)), ((0,), (0,))), "tn": (((1,), (1,)), ((0,), (0,)))}


def _bdot(a, b, mode):
    return lax.dot_general(a.astype(BF16), b.astype(BF16), _BATCHED_DN[mode], preferred_element_type=F32)


def _block_rows(blk, r, d):
    return pl.ds(blk * SPAN * d + r, SPAN, stride=d)


def _take(scr, hp, it, d, rb, nblk, shift=0, other=None):
    out = []
    for hi in range(hp):
        for blk in range(nblk):
            inside = 0 <= blk + shift < nblk
            src, src_blk = (scr, blk + shift) if inside else (other, nblk - 1 if shift < 0 else 0)
            out += [src[hi, _block_rows(src_blk, it * rb + j, d), :] for j in range(rb)]
    return jnp.stack(out)


def _put(scr, hp, it, d, rb, nblk, val):
    n = 0
    for hi in range(hp):
        for blk in range(nblk):
            for j in range(rb):
                scr[hi, _block_rows(blk, it * rb + j, d), :] = val[n]
                n += 1


def _batch_bias(gi, d, hp, rb, nblk, dist, valid, edge_blk=None, edge_ok=None):
    out = []
    for hi in range(hp):
        bias = (-(_head_slope(gi, hp, hi) * d)) * dist
        for blk in range(nblk):
            ok = jnp.logical_and(valid, edge_ok) if blk == edge_blk else valid
            out += [jnp.where(ok, bias, NEG_BIG)] * rb
    return jnp.stack(out)


def _head_slope(gi, hp, hi):
    if hp == HEADS_PER_GROUP:
        return _slope(gi * HEADS_PER_GROUP + hi)
    first = pl.program_id(1) * hp
    sl = jnp.float32(_slope(gi * HEADS_PER_GROUP + hi))
    for h0 in range(hp, HEADS_PER_GROUP, hp):
        sl = jnp.where(first == h0, jnp.float32(_slope(gi * HEADS_PER_GROUP + h0 + hi)), sl)
    return sl


def _split_heads(src_ref, dst_ref, hp):
    for hi in range(hp):
        dst_ref[hi] = src_ref[:, hi * HEAD_DIM:(hi + 1) * HEAD_DIM].astype(F32)


def _attn_fwd(gi, qkv):
    S = qkv.shape[0]
    d, C, nb, hp, W, rb, nblk = _attn_tiling(gi, S)
    scale = HEAD_DIM ** -0.5

    def body(q_ref, kc_ref, kp_ref, vc_ref, vp_ref, o_ref, lse_ref, qs, kcs, kps, vcs, vps, os, ls):
        b = pl.program_id(0)
        for src, dst in ((q_ref, qs), (kc_ref, kcs), (kp_ref, kps), (vc_ref, vcs), (vp_ref, vps)):
            _split_heads(src, dst, hp)
        dist_cur, valid_cur, valid_near = _attn_masks()
        bias_cur = _batch_bias(gi, d, hp, rb, nblk, dist_cur, valid_cur)
        bias_prev = _batch_bias(gi, d, hp, rb, nblk, dist_cur + SPAN, valid_near, edge_blk=0, edge_ok=b > 0)
        at = (d, rb, nblk)

        def sub_sequences(it, carry):
            q = _take(qs, hp, it, *at)
            s_cur = _bdot(q, _take(kcs, hp, it, *at), "nt") * scale + bias_cur
            s_prev = _bdot(q, _take(kcs, hp, it, *at, shift=-1, other=kps), "nt") * scale + bias_prev
            m = jnp.maximum(jnp.max(s_cur, axis=-1, keepdims=True), jnp.max(s_prev, axis=-1, keepdims=True))
            p_cur = jnp.exp(s_cur - m)
            p_prev = jnp.exp(s_prev - m)
            l = jnp.sum(p_cur, axis=-1, keepdims=True) + jnp.sum(p_prev, axis=-1, keepdims=True)
            o = (_bdot(p_cur, _take(vcs, hp, it, *at), "nn")
                 + _bdot(p_prev, _take(vcs, hp, it, *at, shift=-1, other=vps), "nn")) / l
            _put(os, hp, it, *at, o)
            _put(ls, hp, it, *at, jnp.broadcast_to(m + jnp.log(l), o.shape))
            return carry

        lax.fori_loop(0, d // rb, sub_sequences, 0)
        for hi in range(hp):
            hs = slice(hi * HEAD_DIM, (hi + 1) * HEAD_DIM)
            o_ref[:, hs] = os[hi]
            lse_ref[:, hs] = ls[hi]

    per_sec = ATTN_WIDTH // W
    per_grp = ATTN_OUT_WIDTH // W

    def spec(sec, shift):
        return pl.BlockSpec((C, W), lambda b, h: (jnp.maximum(b + shift, 0), sec * per_sec + gi * per_grp + h))

    out = pl.BlockSpec((C, W), lambda b, h: (b, h))
    scr = pltpu.VMEM((hp, C, HEAD_DIM), F32)
    return pl.pallas_call(
        body, grid=(nb, HEADS_PER_GROUP // hp),
        in_specs=[spec(0, 0), spec(1, 0), spec(1, -1), spec(2, 0), spec(2, -1)], out_specs=[out, out],
        out_shape=[jax.ShapeDtypeStruct((S, ATTN_OUT_WIDTH), F32)] * 2, scratch_shapes=[scr] * 7,
        compiler_params=_params("parallel", "parallel"), name=f"attn_fwd_g{gi}")(qkv, qkv, qkv, qkv, qkv)


def _attn_bwd(gi, qkv, do, lse, dp):
    S = qkv.shape[0]
    d, C, nb, hp, W, rb, nblk = _attn_tiling(gi, S)
    scale = HEAD_DIM ** -0.5
    n_in = 12

    def body(*refs):
        ins, (dq_ref, dk_ref, dv_ref) = refs[:n_in], refs[n_in:n_in + 3]
        scr = refs[n_in + 3:]
        for src, dst in zip(ins, scr[:n_in]):
            _split_heads(src, dst, hp)
        qs, kcs, vcs, kps, vps, qns, dos, dons, lses, lsens, dps, dpns = scr[:n_in]
        dqs, dks, dvs = scr[n_in:]
        b = pl.program_id(0)
        dist_cur, valid_cur, valid_near = _attn_masks()
        bias_cur = _batch_bias(gi, d, hp, rb, nblk, dist_cur, valid_cur)
        bias_prev = _batch_bias(gi, d, hp, rb, nblk, dist_cur + SPAN, valid_near, edge_blk=0, edge_ok=b > 0)
        bias_next = _batch_bias(gi, d, hp, rb, nblk, dist_cur + SPAN, valid_near, edge_blk=nblk - 1,
                                edge_ok=b < nb - 1)
        at = (d, rb, nblk)

        def sub_sequences(it, carry):
            q, kc, vc, dov, lse_b, dp_b = (_take(s, hp, it, *at) for s in (qs, kcs, vcs, dos, lses, dps))
            kp = _take(kcs, hp, it, *at, shift=-1, other=kps)
            vp = _take(vcs, hp, it, *at, shift=-1, other=vps)
            qn, don, lse_n, dp_n = (_take(s, hp, it, *at, shift=1, other=o)
                                    for s, o in ((qs, qns), (dos, dons), (lses, lsens), (dps, dpns)))
            p_cur = jnp.exp(_bdot(q, kc, "nt") * scale + bias_cur - lse_b)
            p_prev = jnp.exp(_bdot(q, kp, "nt") * scale + bias_prev - lse_b)
            ds_cur = p_cur * (_bdot(dov, vc, "nt") - dp_b)
            ds_prev = p_prev * (_bdot(dov, vp, "nt") - dp_b)
            _put(dqs, hp, it, *at, (_bdot(ds_cur, kc, "nn") + _bdot(ds_prev, kp, "nn")) * scale)
            p_next = jnp.exp(_bdot(qn, kc, "nt") * scale + bias_next - lse_n)
            ds_next = p_next * (_bdot(don, vc, "nt") - dp_n)
            _put(dvs, hp, it, *at, _bdot(p_cur, dov, "tn") + _bdot(p_next, don, "tn"))
            _put(dks, hp, it, *at, (_bdot(ds_cur, q, "tn") + _bdot(ds_next, qn, "tn")) * scale)
            return carry

        lax.fori_loop(0, d // rb, sub_sequences, 0)
        for hi in range(hp):
            hs = slice(hi * HEAD_DIM, (hi + 1) * HEAD_DIM)
            dq_ref[:, hs] = dqs[hi].astype(BF16)
            dk_ref[:, hs] = dks[hi].astype(BF16)
            dv_ref[:, hs] = dvs[hi].astype(BF16)

    per_sec = ATTN_WIDTH // W
    per_grp = ATTN_OUT_WIDTH // W

    def qkv_spec(sec, shift):
        return pl.BlockSpec((C, W), lambda b, h: (jnp.clip(b + shift, 0, nb - 1), sec * per_sec + gi * per_grp + h))

    def act_spec(shift):
        return pl.BlockSpec((C, W), lambda b, h: (jnp.clip(b + shift, 0, nb - 1), h))

    out = act_spec(0)
    scr = pltpu.VMEM((hp, C, HEAD_DIM), F32)
    return pl.pallas_call(
        body, grid=(nb, HEADS_PER_GROUP // hp),
        in_specs=[qkv_spec(0, 0), qkv_spec(1, 0), qkv_spec(2, 0), qkv_spec(1, -1), qkv_spec(2, -1), qkv_spec(0, 1),
                  act_spec(0), act_spec(1), act_spec(0), act_spec(1), act_spec(0), act_spec(1)],
        out_specs=[out, out, out], out_shape=[jax.ShapeDtypeStruct((S, ATTN_OUT_WIDTH), BF16)] * 3,
        scratch_shapes=[scr] * (n_in + 3),
        compiler_params=_params("parallel", "parallel"), name=f"attn_bwd_g{gi}")(
            qkv, qkv, qkv, qkv, qkv, qkv, do, do, lse, lse, dp, dp)


def _group_weights(l_refs):
    l0, l1, l2 = (r[...] for r in l_refs)
    m = jnp.maximum(jnp.maximum(l0, l1), l2)
    e = [jnp.exp(l0 - m), jnp.exp(l1 - m), jnp.exp(l2 - m)]
    tot = e[0] + e[1] + e[2]
    return [ei / tot for ei in e]


def _merge_fwd(outs, lses):
    S, W = outs[0].shape
    T = 512

    def body(o0, o1, o2, l0, l1, l2, y_ref):
        w = _group_weights((l0, l1, l2))
        y_ref[...] = (w[0] * o0[...] + w[1] * o1[...] + w[2] * o2[...]).astype(BF16)

    blk = pl.BlockSpec((T, W), lambda i: (i, 0))
    return pl.pallas_call(
        body, grid=(S // T,), in_specs=[blk] * 6, out_specs=blk, out_shape=jax.ShapeDtypeStruct((S, W), BF16),
        compiler_params=_params("parallel"), name="merge_fwd")(*outs, *lses)


def _merge_bwd(dy_out, w_attn_out_t, outs, lses):
    S, W = outs[0].shape
    T = 512

    def body(dyo_ref, wao_ref, o0, o1, o2, l0, l1, l2, do0, do1, do2, dp0, dp1, dp2):
        w = _group_weights((l0, l1, l2))
        dyv = _dot(dyo_ref[...], wao_ref[...], "nn")
        y = w[0] * o0[...] + w[1] * o1[...] + w[2] * o2[...]
        prod = dyv * y
        dots = []
        for hi in range(HEADS_PER_GROUP):
            hs = slice(hi * HEAD_DIM, (hi + 1) * HEAD_DIM)
            dots.append(jnp.broadcast_to(jnp.sum(prod[:, hs], axis=-1, keepdims=True), (T, HEAD_DIM)))
        dot_b = jnp.concatenate(dots, axis=1)
        for wg, do_ref, dp_ref in zip(w, (do0, do1, do2), (dp0, dp1, dp2)):
            do_ref[...] = (wg * dyv).astype(BF16)
            dp_ref[...] = wg * dot_b

    blk = pl.BlockSpec((T, W), lambda i: (i, 0))
    res = pl.pallas_call(
        body, grid=(S // T,),
        in_specs=[pl.BlockSpec((T, D_MODEL), lambda i: (i, 0)), pl.BlockSpec((D_MODEL, W), lambda i: (0, 0))] + [blk] * 6,
        out_specs=[blk] * 6,
        out_shape=[jax.ShapeDtypeStruct((S, W), BF16)] * 3 + [jax.ShapeDtypeStruct((S, W), F32)] * 3,
        compiler_params=_params("parallel"), name="merge_bwd")(dy_out, w_attn_out_t, *outs, *lses)
    return res[:3], res[3:]


def _mix_fwd(pool_y, attn_y, w_pool_out_t, w_attn_out_t, gates_pre, b_gate):
    S = pool_y.shape[0]
    D = D_MODEL
    tm, tn = 1024, 512
    nj = D // tn

    def body(p_ref, a_ref, wp_ref, wa_ref, g0_ref, g1_ref, b0_ref, b1_ref, mixed_ref, yp_ref, ya_ref):
        yp = _dot(p_ref[...], wp_ref[...], "nt")
        ya = _dot(a_ref[...], wa_ref[...], "nt")
        g0 = jax.nn.sigmoid(g0_ref[...] + b0_ref[...])
        g1 = jax.nn.sigmoid(g1_ref[...] + b1_ref[...])
        mixed_ref[...] = (g0 * yp + g1 * ya).astype(BF16)
        yp_ref[...] = yp.astype(BF16)
        ya_ref[...] = ya.astype(BF16)

    out = pl.BlockSpec((tm, tn), lambda i, j: (i, j))
    return pl.pallas_call(
        body, grid=(S // tm, nj),
        in_specs=[pl.BlockSpec((tm, POOL_WIDTH), lambda i, j: (i, 0)),
                  pl.BlockSpec((tm, ATTN_OUT_WIDTH), lambda i, j: (i, 0)),
                  pl.BlockSpec((tn, POOL_WIDTH), lambda i, j: (j, 0)),
                  pl.BlockSpec((tn, ATTN_OUT_WIDTH), lambda i, j: (j, 0)),
                  pl.BlockSpec((tm, tn), lambda i, j: (i, j)), pl.BlockSpec((tm, tn), lambda i, j: (i, j + nj)),
                  pl.BlockSpec((1, tn), lambda i, j: (0, j)), pl.BlockSpec((1, tn), lambda i, j: (0, j + nj))],
        out_specs=[out, out, out], out_shape=[jax.ShapeDtypeStruct((S, D), BF16)] * 3,
        compiler_params=_params("parallel", "parallel"), name="mix_fwd")(
            pool_y, attn_y, w_pool_out_t, w_attn_out_t, gates_pre, gates_pre, b_gate, b_gate)


def _mix_bwd(dx, w_out, gates_pre, b_gate, y_pool, y_attn):
    S = dx.shape[0]
    D = D_MODEL
    T = 256

    def body(dx_ref, w_ref, gp_ref, b_ref, yp_ref, ya_ref, dyp_ref, dya_ref, dg_ref, db_ref):
        i = pl.program_id(0)
        dm = _dot(dx_ref[...], w_ref[...], "nt")
        parts = []
        for br, (y_ref, dy_ref) in enumerate(((yp_ref, dyp_ref), (ya_ref, dya_ref))):
            cs = slice(br * D, (br + 1) * D)
            g = jax.nn.sigmoid(gp_ref[:, cs] + b_ref[:, cs])
            dy_ref[...] = (dm * g).astype(BF16)
            dpre = dm * y_ref[...].astype(F32) * g * (1.0 - g)
            dg_ref[:, cs] = dpre.astype(BF16)
            parts.append(jnp.sum(dpre, axis=0, keepdims=True))

        @pl.when(i == 0)
        def _():
            db_ref[:, 0:D] = parts[0]
            db_ref[:, D:2 * D] = parts[1]

        @pl.when(i > 0)
        def _():
            db_ref[:, 0:D] += parts[0]
            db_ref[:, D:2 * D] += parts[1]

    row = pl.BlockSpec((T, D), lambda i: (i, 0))
    row2 = pl.BlockSpec((T, 2 * D), lambda i: (i, 0))
    vec2 = pl.BlockSpec((1, 2 * D), lambda i: (0, 0))
    return pl.pallas_call(
        body, grid=(S // T,), in_specs=[row, pl.BlockSpec((D, D), lambda i: (0, 0)), row2, vec2, row, row],
        out_specs=[row, row, row2, vec2],
        out_shape=[jax.ShapeDtypeStruct((S, D), BF16), jax.ShapeDtypeStruct((S, D), BF16),
                   jax.ShapeDtypeStruct((S, 2 * D), BF16), jax.ShapeDtypeStruct((1, 2 * D), F32)],
        compiler_params=_params("arbitrary"), name="mix_bwd")(dx, w_out, gates_pre, b_gate, y_pool, y_attn)


_CONV_T, _CONV_C = 512, 512


def _conv_taps(ext_ref, h, w_ref, b_ref, cols=slice(None)):
    ext = ext_ref[h, :, cols]
    x2 = pltpu.roll(ext, 2, 0)[HALO:]
    x1 = pltpu.roll(ext, 1, 0)[HALO:]
    x0 = ext[HALO:]
    y = b_ref[:, cols] + w_ref[0:1, cols] * x2
    y = y + w_ref[1:2, cols] * x1
    y = y + w_ref[2:3, cols] * x0
    return y, (x2, x1, x0)


def _gelu_parts(a):
    cdf = 0.5 * (1.0 + lax.erf(a * (1.0 / math.sqrt(2.0))))
    return cdf, a * cdf


def _conv_specs(S):
    T, C = _CONV_T, _CONV_C
    nj = D_FF // C
    blk = pl.BlockSpec((2, T, C), lambda i, j: (0, i, j))
    per = T // HALO
    halo = pl.BlockSpec((2, HALO, C), lambda i, j: (0, jnp.maximum(i * per - 1, 0), j))
    w_a = pl.BlockSpec((3, C), lambda i, j: (0, j))
    w_b = pl.BlockSpec((3, C), lambda i, j: (0, j + nj))
    b_a = pl.BlockSpec((1, C), lambda i, j: (0, j))
    b_b = pl.BlockSpec((1, C), lambda i, j: (0, j + nj))
    return T, C, nj, blk, halo, w_a, w_b, b_a, b_b


def _conv_load(ext_ref, halo_ref, up_ref, first):
    for h in range(2):
        ext_ref[h, 0:HALO, :] = jnp.where(first, 0.0, halo_ref[h].astype(F32))
        ext_ref[h, HALO:, :] = up_ref[h].astype(F32)


def _up_proj_convffn(h, w_up_t, conv_w, conv_b):
    S = h.shape[0]
    tm, C = 1024, _CONV_C
    nj = D_FF // C

    def body(h_ref, wa_ref, wb_ref, cwa_ref, cwb_ref, cba_ref, cbb_ref, up_ref, act_ref, ext_ref, carry_ref):
        i, j = pl.program_id(0), pl.program_id(1)
        for half, w_ref in enumerate((wa_ref, wb_ref)):
            pre = _dot(h_ref[...], w_ref[...], "nt").astype(BF16)
            up_ref[half] = pre
            ext_ref[half, 0:HALO, :] = jnp.where(i == 0, 0.0, carry_ref[j, half])
            ext_ref[half, HALO:, :] = pre.astype(F32)
            carry_ref[j, half] = ext_ref[half, tm:tm + HALO, :]
        a, _ = _conv_taps(ext_ref, 0, cwa_ref, cba_ref)
        b, _ = _conv_taps(ext_ref, 1, cwb_ref, cbb_ref)
        _, gelu = _gelu_parts(a)
        act_ref[...] = (gelu * b).astype(BF16)

    def w_spec(half):
        return pl.BlockSpec((C, D_MODEL), lambda i, j: (j + half * nj, 0))

    def c_spec(rows, half):
        return pl.BlockSpec((rows, C), lambda i, j: (0, j + half * nj))

    return pl.pallas_call(
        body, grid=(S // tm, nj),
        in_specs=[pl.BlockSpec((tm, D_MODEL), lambda i, j: (i, 0)), w_spec(0), w_spec(1),
                  c_spec(3, 0), c_spec(3, 1), c_spec(1, 0), c_spec(1, 1)],
        out_specs=[pl.BlockSpec((2, tm, C), lambda i, j: (0, i, j)), pl.BlockSpec((tm, C), lambda i, j: (i, j))],
        out_shape=[jax.ShapeDtypeStruct((2, S, D_FF), BF16), jax.ShapeDtypeStruct((S, D_FF), BF16)],
        scratch_shapes=[pltpu.VMEM((2, HALO + tm, C), F32), pltpu.VMEM((nj, 2, HALO, C), F32)],
        compiler_params=_params("arbitrary", "arbitrary"), name="up_proj_convffn")(
            h, w_up_t, w_up_t, conv_w, conv_w, conv_b, conv_b)


def _convffn_bwd_a(up_pre, dx, w_down, conv_w, conv_b):
    S = up_pre.shape[1]
    T, C, nj, blk, halo, w_a, w_b, b_a, b_b = _conv_specs(S)
    def swap(spec):
        return pl.BlockSpec(spec.block_shape, lambda j, i, _m=spec.index_map: _m(i, j))

    def body(up_ref, halo_ref, dx_ref, wd_ref, wa_ref, wb_ref, ba_ref, bb_ref, dup_ref, dwa_ref, dwb_ref, dba_ref,
             dbb_ref, ext_ref):
        i = pl.program_id(1)
        _conv_load(ext_ref, halo_ref, up_ref, i == 0)

        @pl.when(i == 0)
        def _():
            for acc_ref in (dwa_ref, dwb_ref, dba_ref, dbb_ref):
                acc_ref[...] = jnp.zeros_like(acc_ref)

        for c in range(C // MXU_DIM):
            cs = slice(c * MXU_DIM, (c + 1) * MXU_DIM)
            dact = _dot(dx_ref[...], wd_ref[cs, :], "nt")
            a, xa = _conv_taps(ext_ref, 0, wa_ref, ba_ref, cs)
            b, xb = _conv_taps(ext_ref, 1, wb_ref, bb_ref, cs)
            cdf, gelu = _gelu_parts(a)
            dgelu = cdf + a * (jnp.exp(-0.5 * a * a) * (1.0 / math.sqrt(2.0 * math.pi)))
            da = dact * b * dgelu
            db = dact * gelu
            dup_ref[0, :, cs] = da.astype(BF16)
            dup_ref[1, :, cs] = db.astype(BF16)
            for dval, xs, dw_ref, dbias_ref in ((da, xa, dwa_ref, dba_ref), (db, xb, dwb_ref, dbb_ref)):
                dw_ref[:, cs] += jnp.concatenate([jnp.sum(dval * xk, axis=0, keepdims=True) for xk in xs], axis=0)
                dbias_ref[:, cs] += jnp.sum(dval, axis=0, keepdims=True)

    dx_spec = pl.BlockSpec((T, D_MODEL), lambda j, i: (i, 0))
    wd_spec = pl.BlockSpec((C, D_MODEL), lambda j, i: (j, 0))
    dw_spec = pl.BlockSpec((3, C), lambda j, i: (0, j))
    db_spec = pl.BlockSpec((1, C), lambda j, i: (0, j))
    res = pl.pallas_call(
        body, grid=(nj, S // T),
        in_specs=[swap(blk), swap(halo), dx_spec, wd_spec, swap(w_a), swap(w_b), swap(b_a), swap(b_b)],
        out_specs=[swap(blk), dw_spec, dw_spec, db_spec, db_spec],
        out_shape=[jax.ShapeDtypeStruct((2, S, D_FF), BF16), jax.ShapeDtypeStruct((3, D_FF), F32),
                   jax.ShapeDtypeStruct((3, D_FF), F32), jax.ShapeDtypeStruct((1, D_FF), F32),
                   jax.ShapeDtypeStruct((1, D_FF), F32)],
        scratch_shapes=[pltpu.VMEM((2, HALO + T, C), F32)],
        compiler_params=_params("parallel", "arbitrary"), name="convffn_bwd_a")(
            up_pre, up_pre, dx, w_down, conv_w, conv_w, conv_b, conv_b)
    dup, dwa, dwb, dba, dbb = res
    dconv_w = jnp.concatenate([dwa, dwb], axis=1)
    dconv_b = jnp.concatenate([dba, dbb], axis=1)
    return dup, dconv_w, dconv_b


def _convffn_bwd_b(dup, conv_w, w_up_t):
    S = dup.shape[1]
    tm, tk = 1024, D_FF // 4
    per_half = D_FF // tk
    nk = 2 * per_half
    per = tm // HALO
    n_blk = S // tm

    def body(d_ref, halo_ref, cw_ref, w_ref, dpre_ref, dh_ref, ext_ref, acc_ref):
        i, k = pl.program_id(0), pl.program_id(1)
        n = tm + HALO
        ext_ref[0:tm, :] = d_ref[...].astype(F32)
        ext_ref[tm:, :] = jnp.where(i == n_blk - 1, 0.0, halo_ref[...].astype(F32))
        ext = ext_ref[...]
        y = cw_ref[2:3, :] * ext[0:tm] + cw_ref[1:2, :] * pltpu.roll(ext, n - 1, 0)[0:tm]
        y = y + cw_ref[0:1, :] * pltpu.roll(ext, n - 2, 0)[0:tm]
        pre = y.astype(BF16)
        dpre_ref[...] = pre
        prod = _dot(pre, w_ref[...], "nn")

        @pl.when(k == 0)
        def _():
            acc_ref[...] = prod

        @pl.when(k > 0)
        def _():
            acc_ref[...] += prod

        @pl.when(k == nk - 1)
        def _():
            dh_ref[...] = acc_ref[...].astype(BF16)

    blk = pl.BlockSpec((None, tm, tk), lambda i, k: (k // per_half, i, k % per_half))
    halo = pl.BlockSpec((None, HALO, tk),
                        lambda i, k: (k // per_half, jnp.minimum((i + 1) * per, S // HALO - 1), k % per_half))
    return pl.pallas_call(
        body, grid=(n_blk, nk),
        in_specs=[blk, halo, pl.BlockSpec((3, tk), lambda i, k: (0, k)), pl.BlockSpec((tk, D_MODEL), lambda i, k: (k, 0))],
        out_specs=[blk, pl.BlockSpec((tm, D_MODEL), lambda i, k: (i, 0))],
        out_shape=[jax.ShapeDtypeStruct((2, S, D_FF), BF16), jax.ShapeDtypeStruct((S, D_MODEL), BF16)],
        scratch_shapes=[pltpu.VMEM((tm + HALO, tk), F32), pltpu.VMEM((tm, D_MODEL), F32)],
        compiler_params=_params("parallel", "arbitrary"), name="convffn_bwd_b")(dup, dup, conv_w, w_up_t)


def _position():
    return lax.axis_index("x"), lax.axis_index("y"), lax.axis_index("c")


def _flip(pos, k):
    x, y, c = pos
    return (1 - x if k & 4 else x, 1 - y if k & 2 else y, 1 - c if k & 1 else c)


def _index(pos):
    return 4 * pos[0] + 2 * pos[1] + pos[2]


_HBM = pl.BlockSpec(memory_space=pltpu.HBM)
_SEM = pl.BlockSpec(memory_space=pltpu.SEMAPHORE)
_ANY = pl.BlockSpec(memory_space=pl.ANY)
_EFFECT = pltpu.SideEffectType.DATAFLOW_SIDE_EFFECTING


def _in_hbm(a):
    return pltpu.with_memory_space_constraint(a, pltpu.HBM)


def _split_copies(name, srcs, lands, n_copies, plan, start_after, wait_after):
    ns, nl = len(srcs), len(lands)
    start_after = tuple(start_after)
    na = len(start_after)

    def start_body(*refs):
        src_refs, land_refs = refs[:ns], refs[ns:ns + nl]
        send_sems, recv_sems = refs[ns + nl + na], refs[ns + nl + na + 1]
        token = refs[-1]
        for k, (src, dst, _, peer) in enumerate(plan(src_refs, land_refs)):
            pltpu.make_async_remote_copy(src_ref=src, dst_ref=dst, send_sem=send_sems.at[k], recv_sem=recv_sems.at[k],
                                         device_id=peer, device_id_type=MESH).start()
        token[...] = jnp.zeros_like(token)

    thru = [pltpu.HBM(a.shape, a.dtype) for a in list(srcs) + list(lands)]
    res = pl.pallas_call(
        start_body, name=name + "_start",
        out_shape=[pltpu.SemaphoreType.DMA((n_copies,)), pltpu.SemaphoreType.DMA((n_copies,))] + thru
        + [jax.ShapeDtypeStruct((8, 128), F32)],
        in_specs=[_HBM] * (ns + nl) + [_ANY] * na,
        out_specs=[_SEM, _SEM] + [_HBM] * (ns + nl) + [pl.BlockSpec(memory_space=pltpu.VMEM)],
        input_output_aliases={i: 2 + i for i in range(ns + nl)},
        compiler_params=pltpu.CompilerParams(has_side_effects=_EFFECT),
    )(*[_in_hbm(a) for a in list(srcs) + list(lands)], *start_after)
    send_sems, recv_sems, token = res[0], res[1], res[-1]
    thru_vals = res[2:-1]

    def wait():
        def wait_body(*refs):
            src_refs, land_refs = refs[:ns], refs[ns:ns + nl]
            s_sems, r_sems = refs[ns + nl], refs[ns + nl + 1]
            for k, (src, _, dst, peer) in enumerate(plan(src_refs, land_refs)):
                cp = pltpu.make_async_remote_copy(src_ref=src, dst_ref=dst, send_sem=s_sems.at[k], recv_sem=r_sems.at[k],
                                                  device_id=peer, device_id_type=MESH)
                cp.wait_send()
                cp.wait_recv()

        out = pl.pallas_call(
            wait_body, name=name + "_wait", out_shape=thru,
            in_specs=[_HBM] * (ns + nl) + [_SEM, _SEM, _ANY], out_specs=[_HBM] * (ns + nl),
            input_output_aliases={i: i for i in range(ns + nl)},
            compiler_params=pltpu.CompilerParams(has_side_effects=_EFFECT),
        )(*thru_vals, send_sems, recv_sems, wait_after())
        return list(out[:ns]), list(out[ns:])

    return wait, token


def _gather_level1_plan(n):
    def plan(src_refs, land_refs):
        me = _position()
        x, y, c = me
        peers = [(x, y, 1 - c), (1 - x, y, c), (x, 1 - y, c), (1 - x, 1 - y, c)]
        return [(src_refs[a], land_refs[a].at[_index(me)], land_refs[a].at[_index(p)], p)
                for a in range(n) for p in peers]
    return plan


def _gather_level2(lands):
    n = len(lands)

    def body(*refs):
        lin = refs[:n]
        send_sems, recv_sems = refs[2 * n:]
        x, y, c = _position()
        sibling = (x, y, 1 - c)
        chips = [(1 - x, y), (x, 1 - y), (1 - x, 1 - y)]

        def copy(a, j, core):
            slot = lin[a].at[_index((*chips[j], core))]
            return pltpu.make_async_remote_copy(
                src_ref=slot, dst_ref=slot, send_sem=send_sems.at[3 * a + j], recv_sem=recv_sems.at[3 * a + j],
                device_id=sibling, device_id_type=MESH)

        sent = [copy(a, j, c) for a in range(n) for j in range(3)]
        for cp in sent:
            cp.start()
        for a in range(n):
            for j in range(3):
                copy(a, j, 1 - c).wait_recv()
        for cp in sent:
            cp.wait_send()

    return pl.pallas_call(
        body, out_shape=[jax.ShapeDtypeStruct(l.shape, l.dtype) for l in lands],
        in_specs=[_ANY] * n, out_specs=[_ANY] * n, input_output_aliases={a: a for a in range(n)},
        scratch_shapes=[pltpu.SemaphoreType.DMA((3 * n,)), pltpu.SemaphoreType.DMA((3 * n,))],
        name="gather_level2")(*lands)


def _gather_direct_plan(n):
    def plan(src_refs, land_refs):
        me = _position()
        peers = [_flip(me, k) for k in range(1, N_DEV)]
        return [(src_refs[a], land_refs[a].at[_index(me)], land_refs[a].at[_index(p)], p)
                for a in range(n) for p in peers]
    return plan


def _place_own(shards):
    me = _index(_position())
    return [lax.dynamic_update_slice(lax.empty((N_DEV,) + s.shape, s.dtype), s[None], (me, 0, 0)) for s in shards]


def _exchange_direct_plan(n):
    def plan(src_refs, land_refs):
        me = _position()
        peers = [_flip(me, k) for k in range(1, N_DEV)]
        return [(src_refs[a].at[_index(p)], land_refs[a].at[_index(me)], land_refs[a].at[_index(p)], p)
                for a in range(n) for p in peers]
    return plan


def _adamw(w, g, m, v):
    m = ADAM_B1 * m + (1.0 - ADAM_B1) * g
    v = ADAM_B2 * v + (1.0 - ADAM_B2) * (g * g)
    m_hat = m / (1.0 - ADAM_B1 ** ADAM_STEP)
    v_hat = v / (1.0 - ADAM_B2 ** ADAM_STEP)
    delta = -ADAM_LR * (m_hat / (jnp.sqrt(v_hat) + ADAM_EPS) + ADAM_WD * w)
    return delta, m, v


def _row_tile(rows, unit, cap=256):
    best = rows
    for t in range(unit, min(rows, cap) + 1, unit):
        if rows % t == 0:
            best = t
    return best


def _sum_partials(name, recv):
    n_src, R, C = recv.shape
    T = _row_tile(R, 16)

    def body(r_ref, g_out):
        g = r_ref[0].astype(F32)
        for s in range(1, n_src):
            g = g + r_ref[s].astype(F32)
        g_out[...] = g

    return pl.pallas_call(
        body, grid=(R // T,), in_specs=[pl.BlockSpec((n_src, T, C), lambda i: (0, i, 0))],
        out_specs=pl.BlockSpec((T, C), lambda i: (i, 0)), out_shape=jax.ShapeDtypeStruct((R, C), F32),
        compiler_params=_params("parallel"), name=name)(recv)


def _sum_adamw(name, recv, w, m, v):
    n_src, R, C = recv.shape
    T = _row_tile(R, 16)

    def body(r_ref, w_ref, m_ref, v_ref, g_out, d_out, m_out, v_out):
        g = r_ref[0].astype(F32)
        for s in range(1, n_src):
            g = g + r_ref[s].astype(F32)
        delta, mn, vn = _adamw(w_ref[...], g, m_ref[...], v_ref[...])
        g_out[...] = g
        d_out[...] = delta
        m_out[...] = mn
        v_out[...] = vn

    blk = pl.BlockSpec((T, C), lambda i: (i, 0))
    return pl.pallas_call(
        body, grid=(R // T,), in_specs=[pl.BlockSpec((n_src, T, C), lambda i: (0, i, 0)), blk, blk, blk],
        out_specs=[blk] * 4, out_shape=[jax.ShapeDtypeStruct((R, C), F32)] * 4,
        compiler_params=_params("parallel"), name=name)(recv, w, m, v)


def _adamw_update(name, w, g, m, v):
    R, C = w.shape
    T = _row_tile(R, 8)

    def body(w_ref, g_ref, m_ref, v_ref, d_out, m_out, v_out):
        delta, mn, vn = _adamw(w_ref[...], g_ref[...], m_ref[...], v_ref[...])
        d_out[...] = delta
        m_out[...] = mn
        v_out[...] = vn

    blk = pl.BlockSpec((T, C), lambda i: (i, 0))
    return pl.pallas_call(
        body, grid=(R // T,), in_specs=[blk] * 4, out_specs=[blk] * 3,
        out_shape=[jax.ShapeDtypeStruct((R, C), F32)] * 3, compiler_params=_params("parallel"), name=name)(w, g, m, v)


def _gather_vmem(p_ref, gath, send_sems, recv_sems):
    me = _position()
    gath[_index(me)] = p_ref[...]
    copies = []
    for k in range(1, N_DEV):
        peer = _flip(me, k)
        copies.append(pltpu.make_async_remote_copy(
            src_ref=p_ref, dst_ref=gath.at[_index(me)], send_sem=send_sems.at[k - 1],
            recv_sem=recv_sems.at[k - 1], device_id=peer, device_id_type=MESH))
    for cp in copies:
        cp.start()
    for k in range(1, N_DEV):
        peer = _flip(me, k)
        pltpu.make_async_remote_copy(
            src_ref=p_ref, dst_ref=gath.at[_index(peer)], send_sem=send_sems.at[k - 1],
            recv_sem=recv_sems.at[k - 1], device_id=peer, device_id_type=MESH).wait_recv()
    for cp in copies:
        cp.wait_send()


def _small_allreduce_adamw(part, w, m, v):
    R, C = part.shape

    def body(p_ref, w_ref, m_ref, v_ref, g_out, d_out, m_out, v_out, gath, send_sems, recv_sems):
        _gather_vmem(p_ref, gath, send_sems, recv_sems)
        g = gath[0]
        for s in range(1, N_DEV):
            g = g + gath[s]
        delta, mn, vn = _adamw(w_ref[...], g, m_ref[...], v_ref[...])
        g_out[...] = g
        d_out[...] = delta
        m_out[...] = mn
        v_out[...] = vn

    vm = pl.BlockSpec(memory_space=pltpu.VMEM)
    return pl.pallas_call(
        body, in_specs=[vm] * 4, out_specs=[vm] * 4, out_shape=[jax.ShapeDtypeStruct((R, C), F32)] * 4,
        scratch_shapes=[pltpu.VMEM((N_DEV, R, C), F32), pltpu.SemaphoreType.DMA((7,)), pltpu.SemaphoreType.DMA((7,))],
        name="small_allreduce_adamw")(part, w, m, v)


_TRANSPOSED = ("w_in", "w_up", "w_pool_out", "w_attn_out")
_SMALL = (("g_mix", 2048), ("b_gate", 4096), ("pool_scale", 1024), ("g_ffn", 2048), ("conv_b", 11264), ("g_final", 2048))
CONV_ROWS = 8


def _to_comm(name, shard):
    return shard.T if name in _TRANSPOSED else shard


SMALL_USED = sum(n for _, n in _SMALL)


def _pack_small(vals, extra=None):
    flat = [vals[n].reshape(-1) for n, _ in _SMALL]
    if extra is not None:
        flat.append(extra.reshape(1))
    used = sum(f.shape[0] for f in flat)
    flat.append(jnp.zeros((SMALL_ROWS * PACK_COLS - used,), F32))
    return jnp.concatenate(flat).reshape(SMALL_ROWS, PACK_COLS)


def _unpack_small(packed, like):
    flat = packed.reshape(-1)
    out, off = {}, 0
    for name, n in _SMALL:
        out[name] = flat[off:off + n].reshape(like[name].shape)
        off += n
    return out


def _local_step(x, target, W, small, first_token=0.0, first_weights=None, late_weights=None, emit=None):
    S = x.shape[0]
    o_qkv, o_gate = POOL_WIDTH, POOL_WIDTH + QKV_WIDTH
    if emit is None:
        emit = lambda grads: None

    def zero(token):
        return 0.0 if token is None else token[0, 0]

    h1 = _rms_fwd("rms1_fwd", x, small["g_mix"] + first_token)
    tok = None
    if first_weights is not None:
        w_first, tok = first_weights(h1)
        small = dict(small, conv_w_full=w_first.pop("conv_w_full"))
        W = dict(W, **w_first)
    w_in_t = W["w_in"]
    u = _mm_nt("proj_u", h1, w_in_t, tm=2048, tn=512, tk=2048, o_dtype=F32, n_rows=POOL_WIDTH, b_row_off=0,
               after=tok)
    qkv = _mm_nt("proj_qkv", h1, w_in_t, tm=2048, tn=512, tk=2048, o_dtype=BF16, n_rows=QKV_WIDTH, b_row_off=o_qkv,
                 after=tok)
    gates_pre = _mm_nt("proj_gates", h1, w_in_t, tm=2048, tn=512, tk=2048, o_dtype=BF16, n_rows=2 * D_MODEL,
                       b_row_off=o_gate, after=tok)
    pool_y = _pool_fwd(u, W["w_pool_lin"], small["pool_scale"])
    outs, lses = zip(*[_attn_fwd(gi, qkv) for gi in range(3)])
    attn_y = _merge_fwd(outs, lses)
    if late_weights is not None:
        W = dict(W, **late_weights(0, attn_y))
    w_up_t = W["w_up"]
    mixed, y_pool, y_attn = _mix_fwd(pool_y, attn_y, W["w_pool_out"], W["w_attn_out"], gates_pre, small["b_gate"])
    x1, h2 = _out_proj_norm(mixed, W["w_out"], x, small["g_ffn"])

    nj = D_FF // 512
    up_pre, act = _up_proj_convffn(h2, w_up_t, small["conv_w_full"], small["conv_b"])
    if late_weights is not None:
        W = dict(W, **late_weights(1, act))
    x2 = _mm_nn("down_proj", act, W["w_down"], tm=1024, tn=1024, tk=D_FF // 2, o_dtype=F32, res=x1)

    dx2, dx2_b, dg_final, sq_cols = _final_loss("final_loss", x2, small["g_final"], target)

    dw_down = _mm_tn("dw_down", act, dx2_b, tm=512, tn=2048, tk=TOKEN_TK)
    tok = emit(dict(w_down=dw_down))
    dup, dconv_w, dconv_b = _convffn_bwd_a(up_pre, dx2_b, W["w_down"], small["conv_w_full"],
                                           small["conv_b"] + zero(tok))
    dup_pre, dh2 = _convffn_bwd_b(dup, small["conv_w_full"], w_up_t)
    dw_up = _mm("dw_up", dup_pre, h2, mode="tn", grid=(2 * nj, 1, S // TOKEN_TK),
                a_spec=pl.BlockSpec((None, TOKEN_TK, 512), lambda i, j, k: (i // nj, k, i % nj)),
                b_spec=pl.BlockSpec((TOKEN_TK, D_MODEL), lambda i, j, k: (k, 0)),
                o_spec=pl.BlockSpec((512, D_MODEL), lambda i, j, k: (i, 0)),
                o_shape=(2 * D_FF, D_MODEL), o_dtype=BF16, acc_shape=(512, D_MODEL))
    tok = emit(dict(w_up=dw_up, conv_w=dconv_w))
    dx1, dx1_b, dg_ffn = _rms_bwd("rms2_bwd", x1, small["g_ffn"] + zero(tok), dh2, dx2, with_bf16=True)

    dw_out = _mm_tn("dw_out", mixed, dx1_b, tm=512, tn=2048, tk=TOKEN_TK)
    tok = emit(dict(w_out=dw_out))
    dy_pool, dy_attn, dgates, db_gate = _mix_bwd(dx1_b, W["w_out"], gates_pre, small["b_gate"] + zero(tok),
                                                 y_pool, y_attn)
    dw_pool_out = _mm_tn("dw_pool_out", dy_pool, pool_y, tm=1024, tn=1024, tk=TOKEN_TK)
    dw_attn_out = _mm_tn("dw_attn_out", dy_attn, attn_y, tm=1024, tn=512, tk=TOKEN_TK)
    e, dw_pool_lin, dpool_scale = _pool_bwd_a(u, dy_pool, W["w_pool_out"], W["w_pool_lin"], small["pool_scale"])
    du = _pool_bwd_b(e)
    dos, dps = _merge_bwd(dy_attn, W["w_attn_out"], outs, lses)
    dqkv = [_attn_bwd(gi, qkv, dos[gi], lses[gi], dps[gi]) for gi in range(3)]
    dproj = jnp.concatenate([du] + [dqkv[gi][sec] for sec in range(3) for gi in range(3)] + [dgates], axis=1)
    dw_in = _mm_tn("dw_in", dproj, h1, tm=512, tn=2048, tk=TOKEN_TK)
    tok = emit(dict(w_in=dw_in, w_pool_out=dw_pool_out, w_attn_out=dw_attn_out, w_pool_lin=dw_pool_lin))
    dh1 = _mm_nn("d_h1", dproj, w_in_t, tm=1024, tn=1024, tk=IN_WIDTH // 4, o_dtype=BF16,
                 after=tok)
    grad_x, dg_mix = _rms_bwd("rms1_bwd", x, small["g_mix"] + zero(tok), dh1, dx1, with_bf16=False)

    grads = dict(w_in=dw_in, w_up=dw_up, w_down=dw_down, w_out=dw_out, w_pool_out=dw_pool_out,
                 w_attn_out=dw_attn_out, w_pool_lin=dw_pool_lin, conv_w=dconv_w)
    small_grads = dict(g_mix=dg_mix, b_gate=db_gate, pool_scale=dpool_scale, g_ffn=dg_ffn, conv_b=dconv_b,
                       g_final=dg_final)
    return sq_cols, grad_x, grads, small_grads


def kernel(x, g_mix, w_in, b_gate, w_pool_lin, pool_scale, w_pool_out, w_attn_out, w_out, g_ffn, w_up, conv_w, conv_b, w_down, g_final, loss_target, m_g_mix, m_w_in, m_b_gate, m_w_pool_lin, m_pool_scale, m_w_pool_out, m_w_attn_out, m_w_out, m_g_ffn, m_w_up, m_conv_w, m_conv_b, m_w_down, m_g_final, v_g_mix, v_w_in, v_b_gate, v_w_pool_lin, v_pool_scale, v_w_pool_out, v_w_attn_out, v_w_out, v_g_ffn, v_w_up, v_conv_w, v_conv_b, v_w_down, v_g_final):
    given = dict(g_mix=g_mix, w_in=w_in, b_gate=b_gate, w_pool_lin=w_pool_lin, pool_scale=pool_scale,
                 w_pool_out=w_pool_out, w_attn_out=w_attn_out, w_out=w_out, g_ffn=g_ffn, w_up=w_up, conv_w=conv_w,
                 conv_b=conv_b, w_down=w_down, g_final=g_final)
    mom_m = dict(g_mix=m_g_mix, w_in=m_w_in, b_gate=m_b_gate, w_pool_lin=m_w_pool_lin, pool_scale=m_pool_scale,
                 w_pool_out=m_w_pool_out, w_attn_out=m_w_attn_out, w_out=m_w_out, g_ffn=m_g_ffn, w_up=m_w_up,
                 conv_w=m_conv_w, conv_b=m_conv_b, w_down=m_w_down, g_final=m_g_final)
    mom_v = dict(g_mix=v_g_mix, w_in=v_w_in, b_gate=v_b_gate, w_pool_lin=v_w_pool_lin, pool_scale=v_pool_scale,
                 w_pool_out=v_w_pool_out, w_attn_out=v_w_attn_out, w_out=v_w_out, g_ffn=v_g_ffn, w_up=v_w_up,
                 conv_w=v_conv_w, conv_b=v_conv_b, w_down=v_w_down, g_final=v_g_final)
    def full(g):
        return g.reshape(N_DEV * g.shape[1], g.shape[2])

    shard = dict(w_in=_to_comm("w_in", w_in[0]).astype(BF16))
    W = {}
    first = [shard["w_in"], w_pool_lin[0].reshape(4 * 32, POOL_GROUP_WIDTH),
             jnp.pad(conv_w[0], ((0, CONV_ROWS - 3), (0, 0)))]
    hold = {}
    w_in_wait, first_token = _split_copies(
        "gather_w_in", first, _place_own(first), 4 * len(first), _gather_level1_plan(len(first)),
        start_after=(), wait_after=lambda: hold["h1"])

    stages = (("w_out", "w_pool_out", "w_attn_out", "w_up"), ("w_down",))
    stage_after = [None, None]
    stage_wait = []
    for n in (n for names in stages for n in names):
        shard[n] = _to_comm(n, given[n][0] + first_token[0, 0]).astype(BF16)

    def first_weights(h1):
        hold["h1"] = h1
        w_in_all, lin_all, conv_w_all = _gather_level2(w_in_wait()[1])
        w_in_full = full(w_in_all)
        lin_full = jnp.moveaxis(lin_all.reshape(N_DEV, 4, 32, POOL_GROUP_WIDTH), 0, 1).reshape(
            4, POOL_GROUP_WIDTH, POOL_GROUP_WIDTH).astype(BF16)
        conv_w_full = jnp.moveaxis(conv_w_all[:, :3, :], 0, 1).reshape(3, 2 * D_FF)
        order_after = (w_in_full,)
        for k, names in enumerate(stages):
            srcs = [shard[n] for n in names]
            wait, token = _split_copies(f"gather_stage{k}", srcs, _place_own(srcs), 7 * len(names),
                                        _gather_direct_plan(len(names)), start_after=order_after,
                                        wait_after=lambda k=k: stage_after[k])
            stage_wait.append(wait)
            order_after = (token,)
        return dict(w_in=w_in_full, w_pool_lin=lin_full, conv_w_full=conv_w_full), token

    def late_weights(stage, after):
        stage_after[stage] = after
        return {n: full(l) for n, l in zip(stages[stage], stage_wait[stage]()[1])}

    small = dict(g_mix=g_mix, b_gate=b_gate, pool_scale=pool_scale, g_ffn=g_ffn, conv_b=conv_b,
                 g_final=g_final.reshape(1, D_MODEL))

    pending = []
    done = {}

    def emit(group):
        names = list(group)
        parts = []
        for n in names:
            g = group[n]
            if n == "w_pool_lin":
                g = jnp.moveaxis(g.reshape(4, N_DEV, 32, POOL_GROUP_WIDTH), 1, 0).reshape(
                    N_DEV, 4 * 32, POOL_GROUP_WIDTH).astype(BF16)
            elif n == "conv_w":
                g = jnp.pad(jnp.moveaxis(g.reshape(3, N_DEV, 2 * D_FF // N_DEV), 1, 0),
                            ((0, 0), (0, CONV_ROWS - 3), (0, 0))).astype(BF16)
            else:
                g = g.reshape(N_DEV, g.shape[0] // N_DEV, g.shape[1])
            parts.append(g)
        me = _index(_position())
        lands = [lax.dynamic_update_slice(lax.empty(p.shape, p.dtype), lax.dynamic_index_in_dim(p, me, 0),
                                          (me, 0, 0)) for p in parts]
        wait, token = _split_copies(
            "exchange_" + names[0], parts, lands, 7 * len(names), _exchange_direct_plan(len(names)),
            start_after=(), wait_after=lambda: done["grad_x"])
        pending.append((names, wait))
        return token

    prepared = sum(shard[n][0, 0].astype(F32) for names in stages for n in names) * 0.0
    sq_cols, grad_x, _, small_grads = _local_step(x[0], loss_target[0], W, small,
                                                  first_token=first_token[0, 0] + prepared,
                                                  first_weights=first_weights, late_weights=late_weights, emit=emit)
    done["grad_x"] = grad_x

    res = {}
    for n, r in [(n, r) for names, wait in pending for n, r in zip(names, wait()[1])]:
        shape = given[n].shape
        rows = math.prod(shape[:-1])
        w2, m2, v2 = (d[n].reshape(rows, shape[-1]) for d in (given, mom_m, mom_v))
        if n in _TRANSPOSED or n == "conv_w":
            g = _sum_partials("sum_" + n, r)
            g = g.T if n in _TRANSPOSED else g[:3]
            upd = (g,) + tuple(_adamw_update("adamw_" + n, w2, g, m2, v2))
        else:
            upd = _sum_adamw("adamw_" + n, r, w2, m2, v2)
        res[n] = [a.reshape(shape) for a in upd]

    packed_small = [_pack_small(d) for d in (given, mom_m, mom_v)]
    packed_out = _small_allreduce_adamw(_pack_small(small_grads, extra=jnp.sum(sq_cols)), *packed_small)
    res_small = [_unpack_small(p, given) for p in packed_out]
    loss = (0.5 / D_MODEL) * packed_out[0][SMALL_USED // PACK_COLS, SMALL_USED % PACK_COLS]

    order = ["g_mix", "w_in", "b_gate", "w_pool_lin", "pool_scale", "w_pool_out", "w_attn_out", "w_out", "g_ffn",
             "w_up", "conv_w", "conv_b", "w_down", "g_final"]
    outs = [loss, grad_x[None]]
    for kind in range(4):
        for n in order:
            outs.append(res[n][kind] if n in res else res_small[kind][n])
    return tuple(outs)
```
